```python
import math
import jax, jax.numpy as jnp
from jax import lax
import numpy as np

D_MODEL = 2048
BATCH = 8
SEQ = 4096
DEPTH = 1

HEAD_DIM = 128
N_HEAD_SLOTS = 8
DILATED_GROUPS = ((128, 1), (512, 4), (2048, 16))
N_GROUPS = len(DILATED_GROUPS)
ATTN_WIDTH = N_HEAD_SLOTS * HEAD_DIM
POOL_WINDOWS = (2, 4, 8, 16)
N_POOL_GROUPS = len(POOL_WINDOWS)
POOL_WIDTH = D_MODEL // 2
POOL_GROUP = POOL_WIDTH // N_POOL_GROUPS
N_BRANCHES = 2
SPLIT_SIZES = (N_GROUPS * ATTN_WIDTH,
               N_GROUPS * ATTN_WIDTH,
               N_GROUPS * ATTN_WIDTH,
               ATTN_WIDTH,
               POOL_WIDTH,
               POOL_WIDTH,
               N_BRANCHES * D_MODEL)
IN_WIDTH = sum(SPLIT_SIZES)
DEEPNORM_ALPHA = (2.0 * DEPTH) ** 0.25
DEEPNORM_BETA = (8.0 * DEPTH) ** -0.25
LN_EPS = 1e-5
NEG_INF = -1e30

kernel_name = "hybrid_dilated_attn_pool_gated_deepnorm"


def layer_norm(x, gamma, beta):
    xf = x.astype(jnp.float32)
    mu = jnp.mean(xf, axis=-1, keepdims=True)
    var = jnp.mean(jnp.square(xf - mu), axis=-1, keepdims=True)
    y = (xf - mu) * lax.rsqrt(var + LN_EPS) * gamma.astype(jnp.float32) + beta.astype(jnp.float32)
    return y.astype(x.dtype)


def dilated_window_attention(q, k, v, window, dilation):
    B, S, H, hd = q.shape
    steps = window // dilation
    span = steps * dilation
    Sp = -(-S // span) * span
    L = Sp // dilation
    nb = L // steps

    def to_blocks(t):
        t = jnp.pad(t, ((0, 0), (0, Sp - S), (0, 0), (0, 0)))
        t = t.reshape(B, L, dilation, H, hd).transpose(0, 2, 1, 3, 4)
        return t.reshape(B, dilation, nb, steps, H, hd)

    def with_prev(t):
        prev = jnp.pad(t[:, :, :-1], ((0, 0), (0, 0), (1, 0), (0, 0), (0, 0), (0, 0)))
        return jnp.concatenate([prev, t], axis=3)

    qb = to_blocks(q)
    kb = with_prev(to_blocks(k))
    vb = with_prev(to_blocks(v))
    scores = jnp.einsum('brnqhd,brnkhd->brnhqk', qb, kb,
                        preferred_element_type=jnp.float32) * (hd ** -0.5)
    qi = np.arange(steps)[:, None]
    kj = np.arange(2 * steps)[None, :]
    dist = steps + qi - kj
    blk = np.arange(nb)[:, None, None]
    mask = (dist >= 0) & (dist <= steps) & ((blk > 0) | (kj >= steps))[...]
    scores = jnp.where(jnp.asarray(mask)[None, None, :, None], scores, NEG_INF)
    m = jnp.max(scores, axis=-1, keepdims=True)
    e = jnp.exp(scores - m)
    den = jnp.sum(e, axis=-1)
    o = jnp.einsum('brnhqk,brnkhd->brnqhd', e.astype(v.dtype), vb,
                   preferred_element_type=jnp.float32)
    o = o / jnp.transpose(den, (0, 1, 2, 4, 3))[..., None]
    lse = m[..., 0] + jnp.log(den)
    o = o.reshape(B, dilation, L, H, hd).transpose(0, 2, 1, 3, 4).reshape(B, Sp, H, hd)[:, :S]
    lse = jnp.transpose(lse, (0, 1, 2, 4, 3)).reshape(B, dilation, L, H)
    lse = lse.transpose(0, 2, 1, 3).reshape(B, Sp, H)[:, :S]
    return o, lse


def causal_pool_mixer(u, w_pool, pool_scale):
    B, S, _ = u.shape
    uf = u.astype(jnp.float32)
    c = jnp.concatenate([jnp.zeros((B, 1, POOL_WIDTH), jnp.float32), jnp.cumsum(uf, axis=1)], axis=1)
    hi = np.arange(1, S + 1)
    outs = []
    for g, w in enumerate(POOL_WINDOWS):
        lo = np.maximum(hi - w, 0)
        cnt = np.minimum(hi, w).astype(np.float32)
        sl = slice(g * POOL_GROUP, (g + 1) * POOL_GROUP)
        window_sum = jnp.take(c[..., sl], hi, axis=1) - jnp.take(c[..., sl], lo, axis=1)
        outs.append(window_sum / jnp.asarray(cnt)[None, :, None] - uf[..., sl])
    p = jnp.stack(outs, axis=2).astype(u.dtype)
    y = jnp.einsum('bsgc,gcd->bsgd', p, w_pool).reshape(B, S, POOL_WIDTH)
    return y * pool_scale


def hybrid_layer(x, w_in, b_gate, w_pool, pool_scale, w_proj_attn, w_proj_pool, w_out, ln_gamma, ln_beta):
    B, S, _ = x.shape
    h = jnp.einsum('bsd,de->bse', x, w_in)
    idx = list(np.cumsum(SPLIT_SIZES)[:-1])
    q, k, v, z_attn, u_pool, z_pool, g_pre = jnp.split(h, idx, axis=-1)
    q = q.reshape(B, S, N_GROUPS, N_HEAD_SLOTS, HEAD_DIM)
    k = k.reshape(B, S, N_GROUPS, N_HEAD_SLOTS, HEAD_DIM)
    v = v.reshape(B, S, N_GROUPS, N_HEAD_SLOTS, HEAD_DIM)
    outs, lses = [], []
    for g, (window, dilation) in enumerate(DILATED_GROUPS):
        o_g, lse_g = dilated_window_attention(q[:, :, g], k[:, :, g], v[:, :, g], window, dilation)
        outs.append(o_g)
        lses.append(lse_g)
    wts = jax.nn.softmax(jnp.stack(lses, axis=0), axis=0)
    o = jnp.sum(wts[..., None] * jnp.stack(outs, axis=0), axis=0).reshape(B, S, ATTN_WIDTH)
    y_attn = o.astype(x.dtype) * jax.nn.silu(z_attn)
    y_pool = causal_pool_mixer(u_pool, w_pool, pool_scale) * jax.nn.silu(z_pool)
    gates = jax.nn.sigmoid((g_pre + b_gate).astype(jnp.float32)).astype(x.dtype)
    g_attn, g_pool = jnp.split(gates, 2, axis=-1)
    merged = g_attn * jnp.einsum('bsc,cd->bsd', y_attn, w_proj_attn) \
        + g_pool * jnp.einsum('bsc,cd->bsd', y_pool, w_proj_pool)
    out = jnp.einsum('bsd,de->bse', merged, w_out)
    return layer_norm(DEEPNORM_ALPHA * x + out, ln_gamma, ln_beta)


def _fwd_setup_inputs(seed: int = 0) -> dict:
    key = jax.random.key(seed)
    ks = jax.random.split(key, 11)
    f32 = jnp.float32
    x = jax.random.normal(ks[0], (BATCH, SEQ, D_MODEL), f32)
    w_in = jax.random.normal(ks[1], (DEPTH, D_MODEL, IN_WIDTH), f32) * D_MODEL ** -0.5
    b_gate = jax.random.normal(ks[2], (DEPTH, N_BRANCHES * D_MODEL), f32) * 0.02
    w_pool = jax.random.normal(ks[3], (DEPTH, N_POOL_GROUPS, POOL_GROUP, POOL_GROUP), f32) * POOL_GROUP ** -0.5
    pool_scale = 1.0 + 0.02 * jax.random.normal(ks[4], (DEPTH, POOL_WIDTH), f32)
    w_proj_attn = jax.random.normal(ks[5], (DEPTH, ATTN_WIDTH, D_MODEL), f32) * ATTN_WIDTH ** -0.5 * DEEPNORM_BETA
    w_proj_pool = jax.random.normal(ks[6], (DEPTH, POOL_WIDTH, D_MODEL), f32) * POOL_WIDTH ** -0.5 * DEEPNORM_BETA
    w_out = jax.random.normal(ks[7], (DEPTH, D_MODEL, D_MODEL), f32) * D_MODEL ** -0.5 * DEEPNORM_BETA
    ln_gamma = 1.0 + 0.02 * jax.random.normal(ks[8], (DEPTH, D_MODEL), f32)
    ln_beta = 0.02 * jax.random.normal(ks[9], (DEPTH, D_MODEL), f32)
    return {"x": x, "w_in": w_in, "b_gate": b_gate, "w_pool": w_pool, "pool_scale": pool_scale,
            "w_proj_attn": w_proj_attn, "w_proj_pool": w_proj_pool, "w_out": w_out,
            "ln_gamma": ln_gamma, "ln_beta": ln_beta}


def _fwd_reference(x, w_in, b_gate, w_pool, pool_scale, w_proj_attn, w_proj_pool, w_out, ln_gamma, ln_beta):
    for layer in range(DEPTH):
        x = hybrid_layer(x, w_in[layer], b_gate[layer], w_pool[layer], pool_scale[layer],
                         w_proj_attn[layer], w_proj_pool[layer], w_out[layer],
                         ln_gamma[layer], ln_beta[layer])
    return x


import jax as _jax
import jax.numpy as _jnp

TWIN_FORMAT = 'train_step'
FWD_PARAMS = ['x', 'w_in', 'b_gate', 'w_pool', 'pool_scale', 'w_proj_attn', 'w_proj_pool', 'w_out', 'ln_gamma', 'ln_beta']
TWIN_WEIGHTS = ['w_in', 'b_gate', 'w_pool', 'pool_scale', 'w_proj_attn', 'w_proj_pool', 'w_out', 'ln_gamma', 'ln_beta']
TWIN_DIFF_INPUT = 'x'
TWIN_INPUTS = ['x', 'w_in', 'b_gate', 'w_pool', 'pool_scale', 'w_proj_attn', 'w_proj_pool', 'w_out', 'ln_gamma', 'ln_beta', 'loss_target', 'm_w_in', 'm_b_gate', 'm_w_pool', 'm_pool_scale', 'm_w_proj_attn', 'm_w_proj_pool', 'm_w_out', 'm_ln_gamma', 'm_ln_beta', 'v_w_in', 'v_b_gate', 'v_w_pool', 'v_pool_scale', 'v_w_proj_attn', 'v_w_proj_pool', 'v_w_out', 'v_ln_gamma', 'v_ln_beta']
TWIN_OUTPUTS = ['loss', 'grad_x', 'grad_w_in', 'grad_b_gate', 'grad_w_pool', 'grad_pool_scale', 'grad_w_proj_attn', 'grad_w_proj_pool', 'grad_w_out', 'grad_ln_gamma', 'grad_ln_beta', 'delta_w_in', 'delta_b_gate', 'delta_w_pool', 'delta_pool_scale', 'delta_w_proj_attn', 'delta_w_proj_pool', 'delta_w_out', 'delta_ln_gamma', 'delta_ln_beta', 'new_m_w_in', 'new_m_b_gate', 'new_m_w_pool', 'new_m_pool_scale', 'new_m_w_proj_attn', 'new_m_w_proj_pool', 'new_m_w_out', 'new_m_ln_gamma', 'new_m_ln_beta', 'new_v_w_in', 'new_v_b_gate', 'new_v_w_pool', 'new_v_pool_scale', 'new_v_w_proj_attn', 'new_v_w_proj_pool', 'new_v_w_out', 'new_v_ln_gamma', 'new_v_ln_beta']
TWIN_LEAF_KINDS = {'loss': 'loss', 'grad_x': 'grad_x', 'grad_w_in': 'grad_w', 'grad_b_gate': 'grad_w', 'grad_w_pool': 'grad_w', 'grad_pool_scale': 'grad_w', 'grad_w_proj_attn': 'grad_w', 'grad_w_proj_pool': 'grad_w', 'grad_w_out': 'grad_w', 'grad_ln_gamma': 'grad_w', 'grad_ln_beta': 'grad_w', 'delta_w_in': 'delta_w', 'delta_b_gate': 'delta_w', 'delta_w_pool': 'delta_w', 'delta_pool_scale': 'delta_w', 'delta_w_proj_attn': 'delta_w', 'delta_w_proj_pool': 'delta_w', 'delta_w_out': 'delta_w', 'delta_ln_gamma': 'delta_w', 'delta_ln_beta': 'delta_w', 'new_m_w_in': 'new_m', 'new_m_b_gate': 'new_m', 'new_m_w_pool': 'new_m', 'new_m_pool_scale': 'new_m', 'new_m_w_proj_attn': 'new_m', 'new_m_w_proj_pool': 'new_m', 'new_m_w_out': 'new_m', 'new_m_ln_gamma': 'new_m', 'new_m_ln_beta': 'new_m', 'new_v_w_in': 'new_v', 'new_v_b_gate': 'new_v', 'new_v_w_pool': 'new_v', 'new_v_pool_scale': 'new_v', 'new_v_w_proj_attn': 'new_v', 'new_v_w_proj_pool': 'new_v', 'new_v_w_out': 'new_v', 'new_v_ln_gamma': 'new_v', 'new_v_ln_beta': 'new_v'}


def _forward(args):
    return _fwd_reference(*[args[k] for k in FWD_PARAMS])


def _output_shape():
    def fwd():
        inp = _fwd_setup_inputs(0)
        return _fwd_reference(*[inp[k] for k in FWD_PARAMS])
    out = _jax.eval_shape(fwd)
    return out.shape, out.dtype

N_MICROBATCH = 1
ADAM_LR = 0.001
ADAM_B1 = 0.9
ADAM_B2 = 0.999
ADAM_EPS = 1e-08
ADAM_WD = 0.01
ADAM_STEP = 10
PER_EXAMPLE_BATCH_AXIS = {'x': 0, 'loss_target': 0}
SHARED_INPUTS = []
_WEIGHT_DTYPES = {'w_in': _jnp.float32, 'b_gate': _jnp.float32, 'w_pool': _jnp.float32, 'pool_scale': _jnp.float32, 'w_proj_attn': _jnp.float32, 'w_proj_pool': _jnp.float32, 'w_out': _jnp.float32, 'ln_gamma': _jnp.float32, 'ln_beta': _jnp.float32}
MOMENT_SCALE = {'w_in': 4.038524e-03, 'b_gate': 2.195879e-03, 'w_pool': 1.052088e-02, 'pool_scale': 1.094595e-02, 'w_proj_attn': 2.709898e-03, 'w_proj_pool': 1.246006e-02, 'w_out': 1.264155e-02, 'ln_gamma': 1.599565e+01, 'ln_beta': 2.759684e-01}


def _to_microbatches(a, axis):
    t = _jnp.moveaxis(a, axis, 0)
    t = t.reshape((N_MICROBATCH, t.shape[0] // N_MICROBATCH) + t.shape[1:])
    return _jnp.moveaxis(t, 1, axis + 1)


def setup_inputs(seed: int = 0) -> dict:
    inp = _fwd_setup_inputs(seed)
    key = _jax.random.fold_in(_jax.random.key(seed), 7919)
    shape, _ = _output_shape()
    out = dict(inp)
    out["loss_target"] = _jax.random.normal(_jax.random.fold_in(key, 0), shape, _jnp.float32)
    for i, name in enumerate(TWIN_WEIGHTS):
        w = inp[name].astype(_jnp.float32)
        if MOMENT_SCALE is None:
            s = _jnp.sqrt(_jnp.mean(_jnp.square(w)) + 1e-30)
        else:
            s = MOMENT_SCALE[name]
        km, kv = _jax.random.split(_jax.random.fold_in(key, i + 1))
        out[name] = w
        out["m_" + name] = s * _jax.random.normal(km, w.shape, _jnp.float32)
        out["v_" + name] = (s * s) * _jax.random.uniform(kv, w.shape, _jnp.float32, 0.5, 1.5)
    if N_MICROBATCH > 1:
        for name, axis in PER_EXAMPLE_BATCH_AXIS.items():
            out[name] = _to_microbatches(out[name], axis)
    return {'x': out['x'], 'w_in': out['w_in'], 'b_gate': out['b_gate'], 'w_pool': out['w_pool'], 'pool_scale': out['pool_scale'], 'w_proj_attn': out['w_proj_attn'], 'w_proj_pool': out['w_proj_pool'], 'w_out': out['w_out'], 'ln_gamma': out['ln_gamma'], 'ln_beta': out['ln_beta'], 'loss_target': out['loss_target'], 'm_w_in': out['m_w_in'], 'm_b_gate': out['m_b_gate'], 'm_w_pool': out['m_w_pool'], 'm_pool_scale': out['m_pool_scale'], 'm_w_proj_attn': out['m_w_proj_attn'], 'm_w_proj_pool': out['m_w_proj_pool'], 'm_w_out': out['m_w_out'], 'm_ln_gamma': out['m_ln_gamma'], 'm_ln_beta': out['m_ln_beta'], 'v_w_in': out['v_w_in'], 'v_b_gate': out['v_b_gate'], 'v_w_pool': out['v_w_pool'], 'v_pool_scale': out['v_pool_scale'], 'v_w_proj_attn': out['v_w_proj_attn'], 'v_w_proj_pool': out['v_w_proj_pool'], 'v_w_out': out['v_w_out'], 'v_ln_gamma': out['v_ln_gamma'], 'v_ln_beta': out['v_ln_beta']}


def _loss(weights, diff, rest, loss_target):
    with _jax.named_scope("forward"):
        args = {**rest, TWIN_DIFF_INPUT: diff, **{k: w.astype(_WEIGHT_DTYPES[k]) for k, w in weights.items()}}
        y = _forward(args)
    with _jax.named_scope("loss_head"):
        err = _jnp.square(y.astype(_jnp.float32) - loss_target)
        return 0.5 * _jnp.sum(_jnp.mean(err, axis=-1)) if err.ndim else 0.5 * err


def _adamw(w, g, m, v):
    m = ADAM_B1 * m + (1.0 - ADAM_B1) * g
    v = ADAM_B2 * v + (1.0 - ADAM_B2) * _jnp.square(g)
    m_hat = m / (1.0 - ADAM_B1 ** ADAM_STEP)
    v_hat = v / (1.0 - ADAM_B2 ** ADAM_STEP)
    delta = -ADAM_LR * (m_hat / (_jnp.sqrt(v_hat) + ADAM_EPS) + ADAM_WD * w)
    return delta, m, v


def reference(x, w_in, b_gate, w_pool, pool_scale, w_proj_attn, w_proj_pool, w_out, ln_gamma, ln_beta, loss_target, m_w_in, m_b_gate, m_w_pool, m_pool_scale, m_w_proj_attn, m_w_proj_pool, m_w_out, m_ln_gamma, m_ln_beta, v_w_in, v_b_gate, v_w_pool, v_pool_scale, v_w_proj_attn, v_w_proj_pool, v_w_out, v_ln_gamma, v_ln_beta):
    given = dict(x=x, w_in=w_in, b_gate=b_gate, w_pool=w_pool, pool_scale=pool_scale, w_proj_attn=w_proj_attn, w_proj_pool=w_proj_pool, w_out=w_out, ln_gamma=ln_gamma, ln_beta=ln_beta, loss_target=loss_target, m_w_in=m_w_in, m_b_gate=m_b_gate, m_w_pool=m_w_pool, m_pool_scale=m_pool_scale, m_w_proj_attn=m_w_proj_attn, m_w_proj_pool=m_w_proj_pool, m_w_out=m_w_out, m_ln_gamma=m_ln_gamma, m_ln_beta=m_ln_beta, v_w_in=v_w_in, v_b_gate=v_b_gate, v_w_pool=v_w_pool, v_pool_scale=v_pool_scale, v_w_proj_attn=v_w_proj_attn, v_w_proj_pool=v_w_proj_pool, v_w_out=v_w_out, v_ln_gamma=v_ln_gamma, v_ln_beta=v_ln_beta)
    weights = {n: given[n] for n in TWIN_WEIGHTS}
    shared = {n: given[n] for n in SHARED_INPUTS}
    per_example = {n: given[n] for n in ['x']}
    grad_fn = _jax.value_and_grad(_loss, argnums=(0, 1))

    def one_microbatch(ex, loss_target):
        ex = dict(ex)
        diff = ex.pop(TWIN_DIFF_INPUT)
        return grad_fn(weights, diff, {**shared, **ex}, loss_target)

    if N_MICROBATCH == 1:
        loss, (grad_w, grad_x) = one_microbatch(per_example, given["loss_target"])
    else:
        def body(carry, xs):
            loss_sum, grad_sum = carry
            l_k, (gw_k, gx_k) = one_microbatch(xs[0], xs[1])
            with _jax.named_scope("update"):
                return (loss_sum + l_k, _jax.tree.map(_jnp.add, grad_sum, gw_k)), gx_k

        init = (_jnp.zeros((), _jnp.float32), _jax.tree.map(_jnp.zeros_like, weights))
        (loss, grad_w), grad_x = _jax.lax.scan(body, init, (per_example, given["loss_target"]))
    with _jax.named_scope("update"):
        delta_w, new_m, new_v = {}, {}, {}
        for n in TWIN_WEIGHTS:
            delta_w[n], new_m[n], new_v[n] = _adamw(weights[n], grad_w[n], given["m_" + n], given["v_" + n])
    return (loss, grad_x, *[grad_w[n] for n in TWIN_WEIGHTS], *[delta_w[n] for n in TWIN_WEIGHTS],
            *[new_m[n] for n in TWIN_WEIGHTS], *[new_v[n] for n in TWIN_WEIGHTS])
```

```python
import functools
import math

import jax
import jax.numpy as jnp
from jax import lax
from jax.experimental import pallas as pl
from jax.experimental.pallas import tpu as pltpu

F32 = jnp.float32
BF16 = jnp.bfloat16
MESH = pl.DeviceIdType.MESH
ANY = pl.BlockSpec(memory_space=pl.ANY)

HEAD_DIM = 128
STEPS = 128
DILATIONS = (1, 4, 16)
POOL_WINDOWS = (2, 4, 8, 16)
POOL_HALO = 16
N_CHIPS = 4
N_DEV = 8
ALPHA = 2.0 ** 0.25
LN_EPS = 1e-5
NEG_INF = -1e30
SCORE_SCALE = HEAD_DIM ** -0.5
ADAM_LR = 0.001
ADAM_B1 = 0.9
ADAM_B2 = 0.999
ADAM_EPS = 1e-08
ADAM_WD = 0.01
ADAM_STEP = 10
MIB = 2 ** 20
NT = (((1,), (1,)), ((), ()))


def _params(semantics=None, vmem_mib=48):
    return pltpu.CompilerParams(dimension_semantics=semantics, vmem_limit_bytes=vmem_mib * MIB)


def _divisor_tile(n, target, multiple):
    best = None
    for t in range(multiple, min(n, target) + 1, multiple):
        if n % t == 0:
            best = t
    assert best is not None, (n, target, multiple)
    return best


def _col_tile(*widths):
    g = 0
    for w in widths:
        g = math.gcd(g, w)
    return _divisor_tile(g, 1024, 128)


def _sigmoid(z):
    return jax.nn.sigmoid(z)


def _dsilu(z, sg):
    return sg * (1.0 + z * (1.0 - sg))


def _place():
    x, y, c = lax.axis_index("x"), lax.axis_index("y"), lax.axis_index("c")
    others = [(1 - x, y), (x, 1 - y), (1 - x, 1 - y)]
    return x, y, c, (x, y, 1 - c), others


def _remote(src, dst, send_sem, recv_sem, dev):
    return pltpu.make_async_remote_copy(src_ref=src, dst_ref=dst, send_sem=send_sem, recv_sem=recv_sem,
                                        device_id=dev, device_id_type=MESH)


def _gather_weights(shards):
    n = len(shards)

    def body(*refs):
        src, dst = refs[:n], refs[n:2 * n]
        send_sems, recv_sems, local_sems = refs[2 * n:]
        x, y, c, sibling, others = _place()
        me = 2 * x + y
        started = []
        local = []
        for i in range(n):
            half = src[i].shape[0] // 2
            rows = pl.ds(c * half, half)
            cp = pltpu.make_async_copy(src[i], dst[i].at[me], local_sems.at[i])
            cp.start()
            local.append(cp)
            for j, (ox, oy) in enumerate(others):
                cp = _remote(src[i].at[rows], dst[i].at[me, rows], send_sems.at[6 * i + j], recv_sems.at[6 * i + j],
                             (ox, oy, c))
                cp.start()
                started.append(cp)
        for i in range(n):
            half = src[i].shape[0] // 2
            rows = pl.ds(c * half, half)
            for j, (ox, oy) in enumerate(others):
                slab = dst[i].at[2 * ox + oy, rows]
                _remote(slab, slab, send_sems.at[6 * i + j], recv_sems.at[6 * i + j], (ox, oy, c)).wait_recv()
                cp = _remote(slab, slab, send_sems.at[6 * i + 3 + j], recv_sems.at[6 * i + 3 + j], sibling)
                cp.start()
                started.append(cp)
        for i in range(n):
            half = src[i].shape[0] // 2
            rows = pl.ds((1 - c) * half, half)
            for j, (ox, oy) in enumerate(others):
                slab = dst[i].at[2 * ox + oy, rows]
                _remote(slab, slab, send_sems.at[6 * i + 3 + j], recv_sems.at[6 * i + 3 + j], sibling).wait_recv()
        for cp in started:
            cp.wait_send()
        for cp in local:
            cp.wait()

    return pl.pallas_call(
        body, name="gather_weights",
        out_shape=[jax.ShapeDtypeStruct((N_CHIPS,) + s.shape, s.dtype) for s in shards],
        in_specs=[ANY] * n, out_specs=[ANY] * n,
        scratch_shapes=[pltpu.SemaphoreType.DMA((6 * n,)), pltpu.SemaphoreType.DMA((6 * n,)),
                        pltpu.SemaphoreType.DMA((n,))],
    )(*shards)


def _swap_halves(grads):
    n = len(grads)

    def body(*refs):
        g, mine, theirs = refs[:n], refs[n:2 * n], refs[2 * n:3 * n]
        send_sems, recv_sems, local_sems = refs[3 * n:]
        x, y, c, sibling, _ = _place()
        remote, local = [], []
        for i in range(n):
            half = g[i].shape[1] // 2
            cp = pltpu.make_async_copy(g[i].at[:, pl.ds(c * half, half)], mine[i], local_sems.at[i])
            cp.start()
            local.append(cp)
            cp = _remote(g[i].at[:, pl.ds((1 - c) * half, half)], theirs[i], send_sems.at[i], recv_sems.at[i], sibling)
            cp.start()
            remote.append(cp)
        for cp in remote:
            cp.wait()
        for cp in local:
            cp.wait()

    halves = [jax.ShapeDtypeStruct((s.shape[0], s.shape[1] // 2) + s.shape[2:], s.dtype) for s in grads]
    return pl.pallas_call(
        body, name="swap_halves", out_shape=halves + halves,
        in_specs=[ANY] * n, out_specs=[ANY] * (2 * n),
        scratch_shapes=[pltpu.SemaphoreType.DMA((n,)), pltpu.SemaphoreType.DMA((n,)), pltpu.SemaphoreType.DMA((n,))],
    )(*grads)


def _scatter_to_chips(sums):
    n = len(sums)

    def body(*refs):
        s, got = refs[:n], refs[n:2 * n]
        send_sems, recv_sems, local_sems = refs[2 * n:]
        x, y, c, _, others = _place()
        me = 2 * x + y
        remote, local = [], []
        for i in range(n):
            cp = pltpu.make_async_copy(s[i].at[me], got[i].at[me], local_sems.at[i])
            cp.start()
            local.append(cp)
            for j, (ox, oy) in enumerate(others):
                cp = _remote(s[i].at[2 * ox + oy], got[i].at[me], send_sems.at[3 * i + j], recv_sems.at[3 * i + j],
                             (ox, oy, c))
                cp.start()
                remote.append(cp)
        for i in range(n):
            for j, (ox, oy) in enumerate(others):
                slot = got[i].at[2 * ox + oy]
                _remote(slot, slot, send_sems.at[3 * i + j], recv_sems.at[3 * i + j], (ox, oy, c)).wait_recv()
        for cp in remote:
            cp.wait_send()
        for cp in local:
            cp.wait()

    return pl.pallas_call(
        body, name="scatter_to_chips", out_shape=[jax.ShapeDtypeStruct(s.shape, s.dtype) for s in sums],
        in_specs=[ANY] * n, out_specs=[ANY] * n,
        scratch_shapes=[pltpu.SemaphoreType.DMA((3 * n,)), pltpu.SemaphoreType.DMA((3 * n,)),
                        pltpu.SemaphoreType.DMA((n,))],
    )(*sums)


def _join_halves(halves):
    n = len(halves)

    def body(*refs):
        h, full = refs[:n], refs[n:2 * n]
        send_sems, recv_sems, local_sems = refs[2 * n:]
        x, y, c, sibling, _ = _place()
        remote, local = [], []
        for i in range(n):
            half = h[i].shape[0]
            rows = pl.ds(c * half, half)
            cp = pltpu.make_async_copy(h[i], full[i].at[rows], local_sems.at[i])
            cp.start()
            local.append(cp)
            cp = _remote(h[i], full[i].at[rows], send_sems.at[i], recv_sems.at[i], sibling)
            cp.start()
            remote.append(cp)
        for i in range(n):
            half = h[i].shape[0]
            theirs = full[i].at[pl.ds((1 - c) * half, half)]
            _remote(theirs, theirs, send_sems.at[i], recv_sems.at[i], sibling).wait_recv()
        for cp in remote:
            cp.wait_send()
        for cp in local:
            cp.wait()

    return pl.pallas_call(
        body, name="join_halves",
        out_shape=[jax.ShapeDtypeStruct((2 * s.shape[0],) + s.shape[1:], s.dtype) for s in halves],
        in_specs=[ANY] * n, out_specs=[ANY] * n,
        scratch_shapes=[pltpu.SemaphoreType.DMA((n,)), pltpu.SemaphoreType.DMA((n,)), pltpu.SemaphoreType.DMA((n,))],
    )(*halves)


def _gather_rows(row):
    def body(row_ref, out_ref, send_sems, recv_sems, local_sem):
        x, y, c = lax.axis_index("x"), lax.axis_index("y"), lax.axis_index("c")
        me = 4 * x + 2 * y + c
        local = pltpu.make_async_copy(row_ref, out_ref.at[me], local_sem)
        local.start()
        sent = []
        peers = []
        for k in range(1, N_DEV):
            px, py, pc = x ^ (k >> 2), y ^ ((k >> 1) & 1), c ^ (k & 1)
            peers.append((k, px, py, pc))
            cp = _remote(row_ref, out_ref.at[me], send_sems.at[k - 1], recv_sems.at[k - 1], (px, py, pc))
            cp.start()
            sent.append(cp)
        for k, px, py, pc in peers:
            slot = out_ref.at[4 * px + 2 * py + pc]
            _remote(slot, slot, send_sems.at[k - 1], recv_sems.at[k - 1], (px, py, pc)).wait_recv()
        for cp in sent:
            cp.wait_send()
        local.wait()

    return pl.pallas_call(
        body, name="gather_rows", out_shape=jax.ShapeDtypeStruct((N_DEV,) + row.shape, row.dtype),
        in_specs=[ANY], out_specs=ANY,
        scratch_shapes=[pltpu.SemaphoreType.DMA((N_DEV - 1,)), pltpu.SemaphoreType.DMA((N_DEV - 1,)),
                        pltpu.SemaphoreType.DMA],
    )(row)


def _add_pair(a, b, name):
    n, rows, cols = a.shape
    tr = _divisor_tile(rows, max(16, (2 * MIB) // (cols * 4)), 16)

    def body(a_ref, b_ref, o_ref):
        o_ref[...] = (a_ref[...].astype(F32) + b_ref[...].astype(F32)).astype(o_ref.dtype)

    spec = pl.BlockSpec((None, tr, cols), lambda i, r: (i, r, 0))
    return pl.pallas_call(body, name=name, grid=(n, rows // tr), in_specs=[spec, spec], out_specs=spec,
                          out_shape=jax.ShapeDtypeStruct(a.shape, BF16),
                          compiler_params=_params(("parallel", "parallel")))(a, b)


def _sum_slots(a, name):
    n, rows, cols = a.shape
    tr = _divisor_tile(rows, max(16, (2 * MIB) // (cols * 4 * n)), 16)

    def body(a_ref, o_ref):
        acc = a_ref[0].astype(F32)
        for i in range(1, n):
            acc = acc + a_ref[i].astype(F32)
        o_ref[...] = acc

    return pl.pallas_call(body, name=name, grid=(rows // tr,),
                          in_specs=[pl.BlockSpec((n, tr, cols), lambda r: (0, r, 0))],
                          out_specs=pl.BlockSpec((tr, cols), lambda r: (r, 0)),
                          out_shape=jax.ShapeDtypeStruct((rows, cols), F32),
                          compiler_params=_params(("parallel",)))(a)


def _adamw_math(w, g, m, v):
    m = ADAM_B1 * m + (1.0 - ADAM_B1) * g
    v = ADAM_B2 * v + (1.0 - ADAM_B2) * (g * g)
    m_hat = m / (1.0 - ADAM_B1 ** ADAM_STEP)
    v_hat = v / (1.0 - ADAM_B2 ** ADAM_STEP)
    delta = -ADAM_LR * (m_hat / (jnp.sqrt(v_hat) + ADAM_EPS) + ADAM_WD * w)
    return delta, m, v


def _adamw(w, g, m, v, name):
    rows, cols = w.shape
    tr = _divisor_tile(rows, max(8, MIB // (cols * 4)), 8)

    def body(w_ref, g_ref, m_ref, v_ref, d_ref, nm_ref, nv_ref):
        d, nm, nv = _adamw_math(w_ref[...], g_ref[...], m_ref[...], v_ref[...])
        d_ref[...] = d
        nm_ref[...] = nm
        nv_ref[...] = nv

    spec = pl.BlockSpec((tr, cols), lambda r: (r, 0))
    shape = jax.ShapeDtypeStruct((rows, cols), F32)
    return pl.pallas_call(body, name=name, grid=(rows // tr,), in_specs=[spec] * 4, out_specs=[spec] * 3,
                          out_shape=[shape] * 3, compiler_params=_params(("parallel",)))(w, g, m, v)


def _sum_rows_adamw(parts, w, m, v):
    def body(p_ref, w_ref, m_ref, v_ref, g_ref, d_ref, nm_ref, nv_ref):
        g = p_ref[0]
        for i in range(1, N_DEV):
            g = g + p_ref[i]
        d, nm, nv = _adamw_math(w_ref[...], g, m_ref[...], v_ref[...])
        g_ref[...] = g
        d_ref[...] = d
        nm_ref[...] = nm
        nv_ref[...] = nv

    shape = jax.ShapeDtypeStruct(w.shape, F32)
    return pl.pallas_call(body, name="sum_rows_adamw", out_shape=[shape] * 4)(parts, w, m, v)


def _in_proj(xb, w4, col0, ncols, tn, out_dtype, name):
    s, d = xb.shape
    per_chip = w4.shape[2] // tn
    j0 = col0 // tn
    tm = _divisor_tile(s, 512, 16)

    def body(a_ref, b_ref, o_ref):
        o_ref[...] = jnp.dot(a_ref[...], b_ref[...], preferred_element_type=F32).astype(o_ref.dtype)

    return pl.pallas_call(
        body, name=name, grid=(ncols // tn, s // tm),
        in_specs=[pl.BlockSpec((tm, d), lambda j, m: (m, 0)),
                  pl.BlockSpec((None, d, tn), lambda j, m: ((j0 + j) // per_chip, 0, (j0 + j) % per_chip))],
        out_specs=pl.BlockSpec((tm, tn), lambda j, m: (m, j)),
        out_shape=jax.ShapeDtypeStruct((s, ncols), out_dtype),
        compiler_params=_params(("parallel", "parallel")))(xb, w4)


def _blocks_per_subsequence(g, n_blocks):
    return jnp.where(g == 0, n_blocks // DILATIONS[0],
                     jnp.where(g == 1, n_blocks // DILATIONS[1], n_blocks // DILATIONS[2]))


def _window_mask(first):
    qi = lax.broadcasted_iota(jnp.int32, (STEPS, 2 * STEPS), 0)
    kj = lax.broadcasted_iota(jnp.int32, (STEPS, 2 * STEPS), 1)
    lowest = jnp.where(first, STEPS, 0)
    return (kj >= qi) & (kj <= qi + STEPS) & (kj >= lowest)


def _attn_fwd(qp, kp, vp):
    n_groups, s, aw = qp.shape
    heads = aw // HEAD_DIM
    n_blocks = s // STEPS

    def body(q_ref, kc_ref, kp_ref, vc_ref, vp_ref, o_ref, l_ref):
        g, b = pl.program_id(0), pl.program_id(1)
        first = lax.rem(b, _blocks_per_subsequence(g, n_blocks)) == 0
        mask = _window_mask(first)
        for h in range(heads):
            hs = slice(h * HEAD_DIM, (h + 1) * HEAD_DIM)
            kk = jnp.concatenate([kp_ref[:, hs], kc_ref[:, hs]], axis=0)
            vv = jnp.concatenate([vp_ref[:, hs], vc_ref[:, hs]], axis=0)
            sc = lax.dot_general(q_ref[:, hs], kk, NT, preferred_element_type=F32) * SCORE_SCALE
            sc = jnp.where(mask, sc, NEG_INF)
            mx = jnp.max(sc, axis=1, keepdims=True)
            e = jnp.exp(sc - mx)
            den = jnp.sum(e, axis=1, keepdims=True)
            o = jnp.dot(e.astype(BF16), vv, preferred_element_type=F32) / den
            o_ref[:, hs] = o
            l_ref[:, hs] = jnp.broadcast_to(mx + jnp.log(den), (STEPS, HEAD_DIM))

    cur = pl.BlockSpec((None, STEPS, aw), lambda g, b: (g, b, 0))
    prev = pl.BlockSpec((None, STEPS, aw), lambda g, b: (g, jnp.maximum(b - 1, 0), 0))
    shape = jax.ShapeDtypeStruct((n_groups, s, aw), F32)
    return pl.pallas_call(
        body, name="attn_fwd", grid=(n_groups, n_blocks),
        in_specs=[cur, cur, prev, cur, prev], out_specs=[cur, cur], out_shape=[shape, shape],
        compiler_params=_params(("parallel", "parallel")))(qp, kp, kp, vp, vp)


def _combine_groups(o3, l3, zuz, aw):
    n_groups, s, _ = o3.shape
    tr = _divisor_tile(s, 256, 16)

    def body(o_ref, l_ref, z_ref, oo_ref, ll_ref, y_ref):
        l0, l1, l2 = l_ref[0], l_ref[1], l_ref[2]
        mx = jnp.maximum(jnp.maximum(l0, l1), l2)
        w0, w1, w2 = jnp.exp(l0 - mx), jnp.exp(l1 - mx), jnp.exp(l2 - mx)
        den = w0 + w1 + w2
        o = (w0 * o_ref[0] + w1 * o_ref[1] + w2 * o_ref[2]) / den
        z = z_ref[...]
        oo_ref[...] = o
        ll_ref[...] = mx + jnp.log(den)
        y_ref[...] = (o * (z * _sigmoid(z))).astype(BF16)

    three = pl.BlockSpec((n_groups, tr, aw), lambda r: (0, r, 0))
    one = pl.BlockSpec((tr, aw), lambda r: (r, 0))
    f = jax.ShapeDtypeStruct((s, aw), F32)
    return pl.pallas_call(
        body, name="combine_groups", grid=(s // tr,), in_specs=[three, three, one], out_specs=[one, one, one],
        out_shape=[f, f, jax.ShapeDtypeStruct((s, aw), BF16)], compiler_params=_params(("parallel",)))(o3, l3, zuz)


def _pool_counts(row0, rows, window):
    t = row0 + lax.broadcasted_iota(jnp.int32, (rows, 1), 0)
    return jnp.minimum(t + 1, window).astype(F32)


def _pool_fwd(zuz, w_pool, pool_scale, aw, pw):
    s = zuz.shape[0]
    pg = pw // len(POOL_WINDOWS)
    tr = _divisor_tile(s, 256, POOL_HALO)
    u_col, z_col = aw // pw, aw // pw + 1
    assert aw % pw == 0

    def body(u_ref, up_ref, z_ref, w_ref, sc_ref, p_ref, l_ref, y_ref):
        r = pl.program_id(0)
        u = u_ref[...]
        halo = jnp.where(r > 0, up_ref[...], 0.0)
        ext = jnp.concatenate([halo, u], axis=0)
        pieces, lins = [], []
        for gi, window in enumerate(POOL_WINDOWS):
            cs = slice(gi * pg, (gi + 1) * pg)
            acc = ext[:, cs]
            shift = 1
            while shift < window:
                acc = acc + pltpu.roll(acc, shift, 0)
                shift *= 2
            p = acc[POOL_HALO:] / _pool_counts(r * tr, tr, window) - u[:, cs]
            pieces.append(p)
            lins.append(jnp.dot(p.astype(BF16), w_ref[gi], preferred_element_type=F32))
        p = jnp.concatenate(pieces, axis=1)
        lin = jnp.concatenate(lins, axis=1)
        z = z_ref[...]
        p_ref[...] = p.astype(BF16)
        l_ref[...] = lin
        y_ref[...] = (lin * sc_ref[...] * (z * _sigmoid(z))).astype(BF16)

    per = tr // POOL_HALO
    out = pl.BlockSpec((tr, pw), lambda r: (r, 0))
    return pl.pallas_call(
        body, name="pool_fwd", grid=(s // tr,),
        in_specs=[pl.BlockSpec((tr, pw), lambda r: (r, u_col)),
                  pl.BlockSpec((POOL_HALO, pw), lambda r: (jnp.maximum(r * per - 1, 0), u_col)),
                  pl.BlockSpec((tr, pw), lambda r: (r, z_col)),
                  pl.BlockSpec((len(POOL_WINDOWS), pg, pg), lambda r: (0, 0, 0)),
                  pl.BlockSpec((1, pw), lambda r: (0, 0))],
        out_specs=[out, out, out],
        out_shape=[jax.ShapeDtypeStruct((s, pw), BF16), jax.ShapeDtypeStruct((s, pw), F32),
                   jax.ShapeDtypeStruct((s, pw), BF16)],
        compiler_params=_params(("parallel",)))(zuz, zuz, zuz, w_pool, pool_scale)


def _proj_merge(y_attn, y_pool, wpa4, wpp4, gpre, b_gate):
    s, aw = y_attn.shape
    pw = y_pool.shape[1]
    tn = wpa4.shape[2]
    d = N_CHIPS * tn
    tm = _divisor_tile(s, 512, 16)

    def body(ya_ref, yp_ref, wa_ref, wp_ref, ga_ref, gp_ref, ba_ref, bp_ref, a_ref, p_ref, m_ref):
        a = jnp.dot(ya_ref[...], wa_ref[...], preferred_element_type=F32)
        p = jnp.dot(yp_ref[...], wp_ref[...], preferred_element_type=F32)
        a_ref[...] = a
        p_ref[...] = p
        m_ref[...] = (_sigmoid(ga_ref[...] + ba_ref[...]) * a + _sigmoid(gp_ref[...] + bp_ref[...]) * p).astype(BF16)

    out = pl.BlockSpec((tm, tn), lambda n, m: (m, n))
    f = jax.ShapeDtypeStruct((s, d), F32)
    return pl.pallas_call(
        body, name="proj_merge", grid=(N_CHIPS, s // tm),
        in_specs=[pl.BlockSpec((tm, aw), lambda n, m: (m, 0)), pl.BlockSpec((tm, pw), lambda n, m: (m, 0)),
                  pl.BlockSpec((None, aw, tn), lambda n, m: (n, 0, 0)),
                  pl.BlockSpec((None, pw, tn), lambda n, m: (n, 0, 0)),
                  pl.BlockSpec((tm, tn), lambda n, m: (m, n)), pl.BlockSpec((tm, tn), lambda n, m: (m, N_CHIPS + n)),
                  pl.BlockSpec((1, tn), lambda n, m: (0, n)), pl.BlockSpec((1, tn), lambda n, m: (0, N_CHIPS + n))],
        out_specs=[out, out, out], out_shape=[f, f, jax.ShapeDtypeStruct((s, d), BF16)],
        compiler_params=_params(("parallel", "parallel")))(y_attn, y_pool, wpa4, wpp4, gpre, gpre, b_gate, b_gate)


def _out_norm_loss(merged, w_out, x, target, gamma, beta):
    s, d = x.shape
    tm = _divisor_tile(s, 256, 16)

    def body(m_ref, w_ref, x_ref, t_ref, g_ref, b_ref, dr_ref, drb_ref, loss_ref, dg_ref, db_ref):
        @pl.when(pl.program_id(0) == 0)
        def _():
            loss_ref[...] = jnp.zeros_like(loss_ref)
            dg_ref[...] = jnp.zeros_like(dg_ref)
            db_ref[...] = jnp.zeros_like(db_ref)

        r = ALPHA * x_ref[...] + jnp.dot(m_ref[...], w_ref[...], preferred_element_type=F32)
        mu = jnp.mean(r, axis=1, keepdims=True)
        rc = r - mu
        rstd = lax.rsqrt(jnp.mean(rc * rc, axis=1, keepdims=True) + LN_EPS)
        xhat = rc * rstd
        diff = xhat * g_ref[...] + b_ref[...] - t_ref[...]
        dy = diff / d
        loss_ref[...] += jnp.sum(diff * diff, axis=0, keepdims=True)
        dg_ref[...] += jnp.sum(dy * xhat, axis=0, keepdims=True)
        db_ref[...] += jnp.sum(dy, axis=0, keepdims=True)
        dxhat = dy * g_ref[...]
        dr = rstd * (dxhat - jnp.mean(dxhat, axis=1, keepdims=True)
                     - xhat * jnp.mean(dxhat * xhat, axis=1, keepdims=True))
        dr_ref[...] = dr
        drb_ref[...] = dr.astype(BF16)

    row = pl.BlockSpec((tm, d), lambda m: (m, 0))
    vec = pl.BlockSpec((1, d), lambda m: (0, 0))
    v = jax.ShapeDtypeStruct((1, d), F32)
    return pl.pallas_call(
        body, name="out_norm_loss", grid=(s // tm,),
        in_specs=[row, pl.BlockSpec((d, d), lambda m: (0, 0)), row, row, vec, vec],
        out_specs=[row, row, vec, vec, vec],
        out_shape=[jax.ShapeDtypeStruct((s, d), F32), jax.ShapeDtypeStruct((s, d), BF16), v, v, v],
        compiler_params=_params(("arbitrary",), vmem_mib=56))(merged, w_out, x, target, gamma, beta)


def _merge_bwd(drb, w_out, a, p, gpre, b_gate):
    s, d = drb.shape
    tm = _divisor_tile(s, 128, 16)

    def body(dr_ref, w_ref, a_ref, p_ref, ga_ref, gp_ref, ba_ref, bp_ref, da_ref, dp_ref, dg_ref, db_ref):
        @pl.when(pl.program_id(0) == 0)
        def _():
            db_ref[...] = jnp.zeros_like(db_ref)

        dm = lax.dot_general(dr_ref[...], w_ref[...], NT, preferred_element_type=F32)
        sa = _sigmoid(ga_ref[...] + ba_ref[...])
        sp = _sigmoid(gp_ref[...] + bp_ref[...])
        da_ref[...] = (dm * sa).astype(BF16)
        dp_ref[...] = (dm * sp).astype(BF16)
        dga = dm * a_ref[...] * sa * (1.0 - sa)
        dgp = dm * p_ref[...] * sp * (1.0 - sp)
        dg_ref[:, :d] = dga.astype(BF16)
        dg_ref[:, d:] = dgp.astype(BF16)
        db_ref[:, :d] += jnp.sum(dga, axis=0, keepdims=True)
        db_ref[:, d:] += jnp.sum(dgp, axis=0, keepdims=True)

    row = pl.BlockSpec((tm, d), lambda m: (m, 0))
    row1 = pl.BlockSpec((tm, d), lambda m: (m, 1))
    return pl.pallas_call(
        body, name="merge_bwd", grid=(s // tm,),
        in_specs=[row, pl.BlockSpec((d, d), lambda m: (0, 0)), row, row, row, row1,
                  pl.BlockSpec((1, d), lambda m: (0, 0)), pl.BlockSpec((1, d), lambda m: (0, 1))],
        out_specs=[row, row, pl.BlockSpec((tm, 2 * d), lambda m: (m, 0)), pl.BlockSpec((1, 2 * d), lambda m: (0, 0))],
        out_shape=[jax.ShapeDtypeStruct((s, d), BF16), jax.ShapeDtypeStruct((s, d), BF16),
                   jax.ShapeDtypeStruct((s, 2 * d), BF16), jax.ShapeDtypeStruct((1, 2 * d), F32)],
        compiler_params=_params(("arbitrary",)))(drb, w_out, a, p, gpre, gpre, b_gate, b_gate)


def _proj_t(dy_ref, w_ref, tn):
    acc = None
    for n in range(N_CHIPS):
        t = lax.dot_general(dy_ref[:, n * tn:(n + 1) * tn], w_ref[n], NT, preferred_element_type=F32)
        acc = t if acc is None else acc + t
    return acc


def _attn_gate_bwd(da, wpa4, zuz, o):
    s, d = da.shape
    aw, tn = wpa4.shape[1], wpa4.shape[2]
    heads = aw // HEAD_DIM
    tm = _divisor_tile(s, 512, 16)

    def body(da_ref, w_ref, z_ref, o_ref, do_ref, dd_ref, dz_ref):
        dy = _proj_t(da_ref, w_ref, tn)
        z, o = z_ref[...], o_ref[...]
        sg = _sigmoid(z)
        do = dy * (z * sg)
        do_ref[...] = do.astype(BF16)
        dz_ref[...] = (dy * o * _dsilu(z, sg)).astype(BF16)
        prod = do * o
        for h in range(heads):
            hs = slice(h * HEAD_DIM, (h + 1) * HEAD_DIM)
            dd_ref[:, hs] = jnp.broadcast_to(jnp.sum(prod[:, hs], axis=1, keepdims=True), (tm, HEAD_DIM))

    row = pl.BlockSpec((tm, aw), lambda m: (m, 0))
    return pl.pallas_call(
        body, name="attn_gate_bwd", grid=(s // tm,),
        in_specs=[pl.BlockSpec((tm, d), lambda m: (m, 0)), pl.BlockSpec((N_CHIPS, aw, tn), lambda m: (0, 0, 0)),
                  row, row],
        out_specs=[row, row, row],
        out_shape=[jax.ShapeDtypeStruct((s, aw), BF16), jax.ShapeDtypeStruct((s, aw), F32),
                   jax.ShapeDtypeStruct((s, aw), BF16)],
        compiler_params=_params(("parallel",)))(da, wpa4, zuz, o)


def _pool_gate_bwd(dp_in, wpp4, zuz, lin, pooled, w_pool, pool_scale, aw):
    s, d = dp_in.shape
    pw, tn = wpp4.shape[1], wpp4.shape[2]
    n_win = len(POOL_WINDOWS)
    pg = pw // n_win
    tm = _divisor_tile(s, 256, 16)
    z_col = aw // pw + 1

    def body(dp_ref, w_ref, z_ref, l_ref, p_ref, wp_ref, sc_ref, dz_ref, dpo_ref, dw_ref, ds_ref):
        @pl.when(pl.program_id(0) == 0)
        def _():
            dw_ref[...] = jnp.zeros_like(dw_ref)
            ds_ref[...] = jnp.zeros_like(ds_ref)

        dy = _proj_t(dp_ref, w_ref, tn)
        z, lin_ = z_ref[...], l_ref[...]
        sg = _sigmoid(z)
        dypp = dy * (z * sg)
        dz_ref[...] = (dy * (lin_ * sc_ref[...]) * _dsilu(z, sg)).astype(BF16)
        ds_ref[...] += jnp.sum(dypp * lin_, axis=0, keepdims=True)
        dlin = (dypp * sc_ref[...]).astype(BF16)
        for gi in range(n_win):
            cs = slice(gi * pg, (gi + 1) * pg)
            pt = p_ref[:, cs].astype(F32).T.astype(BF16)
            dw_ref[gi] += jnp.dot(pt, dlin[:, cs], preferred_element_type=F32)
            dpo_ref[:, cs] = lax.dot_general(dlin[:, cs], wp_ref[gi], NT, preferred_element_type=F32)

    row = pl.BlockSpec((tm, pw), lambda m: (m, 0))
    return pl.pallas_call(
        body, name="pool_gate_bwd", grid=(s // tm,),
        in_specs=[pl.BlockSpec((tm, d), lambda m: (m, 0)), pl.BlockSpec((N_CHIPS, pw, tn), lambda m: (0, 0, 0)),
                  pl.BlockSpec((tm, pw), lambda m: (m, z_col)), row, row,
                  pl.BlockSpec((n_win, pg, pg), lambda m: (0, 0, 0)), pl.BlockSpec((1, pw), lambda m: (0, 0))],
        out_specs=[row, row, pl.BlockSpec((n_win, pg, pg), lambda m: (0, 0, 0)),
                   pl.BlockSpec((1, pw), lambda m: (0, 0))],
        out_shape=[jax.ShapeDtypeStruct((s, pw), BF16), jax.ShapeDtypeStruct((s, pw), F32),
                   jax.ShapeDtypeStruct((n_win, pg, pg), F32), jax.ShapeDtypeStruct((1, pw), F32)],
        compiler_params=_params(("arbitrary",)))(dp_in, wpp4, zuz, lin, pooled, w_pool, pool_scale)


def _pool_bwd(dpooled):
    s, pw = dpooled.shape
    pg = pw // len(POOL_WINDOWS)
    tr = _divisor_tile(s, 256, POOL_HALO)
    per = tr // POOL_HALO
    n_tiles = s // tr

    def body(c_ref, n_ref, du_ref):
        r = pl.program_id(0)
        cur = c_ref[...]
        halo = jnp.where(r < n_tiles - 1, n_ref[...], 0.0)
        ext = jnp.concatenate([cur, halo], axis=0)
        rows = tr + POOL_HALO
        for gi, window in enumerate(POOL_WINDOWS):
            cs = slice(gi * pg, (gi + 1) * pg)
            acc = ext[:, cs] / _pool_counts(r * tr, rows, window)
            shift = 1
            while shift < window:
                acc = acc + pltpu.roll(acc, rows - shift, 0)
                shift *= 2
            du_ref[:, cs] = (acc[:tr] - cur[:, cs]).astype(BF16)

    return pl.pallas_call(
        body, name="pool_bwd", grid=(n_tiles,),
        in_specs=[pl.BlockSpec((tr, pw), lambda r: (r, 0)),
                  pl.BlockSpec((POOL_HALO, pw), lambda r: (jnp.minimum((r + 1) * per, s // POOL_HALO - 1), 0))],
        out_specs=pl.BlockSpec((tr, pw), lambda r: (r, 0)),
        out_shape=jax.ShapeDtypeStruct((s, pw), BF16), compiler_params=_params(("parallel",)))(dpooled, dpooled)


def _attn_bwd(qp, kp, vp, dop, lsep, ddp):
    n_groups, s, aw = qp.shape
    heads = aw // HEAD_DIM
    n_blocks = s // STEPS

    def body(q_ref, do_ref, l_ref, dd_ref, kc_ref, kp_ref, vc_ref, vp_ref, dq_ref, dk_ref, dv_ref, ck_ref, cv_ref):
        g, b = pl.program_id(0), pl.program_id(1)

        @pl.when(b == 0)
        def _():
            ck_ref[...] = jnp.zeros_like(ck_ref)
            cv_ref[...] = jnp.zeros_like(cv_ref)

        @pl.when(b < n_blocks)
        def _():
            first = lax.rem(b, _blocks_per_subsequence(g, n_blocks)) == 0
            mask = _window_mask(first)
            for h in range(heads):
                hs = slice(h * HEAD_DIM, (h + 1) * HEAD_DIM)
                q, do = q_ref[:, hs], do_ref[:, hs]
                kk = jnp.concatenate([kp_ref[:, hs], kc_ref[:, hs]], axis=0)
                vv = jnp.concatenate([vp_ref[:, hs], vc_ref[:, hs]], axis=0)
                lse = jnp.concatenate([l_ref[:, hs], l_ref[:, hs]], axis=1)
                dd = jnp.concatenate([dd_ref[:, hs], dd_ref[:, hs]], axis=1)
                sc = lax.dot_general(q, kk, NT, preferred_element_type=F32) * SCORE_SCALE
                prob = jnp.where(mask, jnp.exp(sc - lse), 0.0)
                dprob = lax.dot_general(do, vv, NT, preferred_element_type=F32)
                dsc = prob * (dprob - dd) * SCORE_SCALE
                dq_ref[:, hs] = jnp.dot(dsc.astype(BF16), kk, preferred_element_type=F32).astype(BF16)
                dkk = jnp.dot(dsc.T.astype(BF16), q, preferred_element_type=F32)
                dvv = jnp.dot(prob.T.astype(BF16), do, preferred_element_type=F32)
                dk_ref[:, hs] = (ck_ref[:, hs] + dkk[:STEPS]).astype(BF16)
                dv_ref[:, hs] = (cv_ref[:, hs] + dvv[:STEPS]).astype(BF16)
                ck_ref[:, hs] = dkk[STEPS:]
                cv_ref[:, hs] = dvv[STEPS:]

        @pl.when(b == n_blocks)
        def _():
            dk_ref[...] = ck_ref[...].astype(BF16)
            dv_ref[...] = cv_ref[...].astype(BF16)

    last = n_blocks - 1
    cur = pl.BlockSpec((None, STEPS, aw), lambda g, b: (g, jnp.minimum(b, last), 0))
    prev = pl.BlockSpec((None, STEPS, aw), lambda g, b: (g, jnp.clip(b - 1, 0, last), 0))
    shape = jax.ShapeDtypeStruct((n_groups, s, aw), BF16)
    return pl.pallas_call(
        body, name="attn_bwd", grid=(n_groups, n_blocks + 1),
        in_specs=[cur, cur, cur, cur, cur, prev, cur, prev], out_specs=[cur, prev, prev],
        out_shape=[shape, shape, shape],
        scratch_shapes=[pltpu.VMEM((STEPS, aw), F32), pltpu.VMEM((STEPS, aw), F32)],
        compiler_params=_params(("parallel", "arbitrary")))(qp, dop, lsep, ddp, kp, kp, vp, vp)


def _weight_grad(at, b, tn, col_blocks, name):
    m, k = at.shape
    n = b.shape[1]
    tm = _divisor_tile(m, 1024, 16)
    tk = _divisor_tile(k, 2048, 128)
    nk = k // tk

    def body(a_ref, b_ref, o_ref, acc_ref):
        kk = pl.program_id(2)

        @pl.when(kk == 0)
        def _():
            acc_ref[...] = jnp.zeros_like(acc_ref)

        acc_ref[...] += jnp.dot(a_ref[...], b_ref[...], preferred_element_type=F32)

        @pl.when(kk == nk - 1)
        def _():
            o_ref[...] = acc_ref[...].astype(BF16)

    if col_blocks:
        out_spec = pl.BlockSpec((None, tm, tn), lambda i, j, kk: (j, i, 0))
        out_shape = jax.ShapeDtypeStruct((n // tn, m, tn), BF16)
    else:
        out_spec = pl.BlockSpec((tm, tn), lambda i, j, kk: (i, j))
        out_shape = jax.ShapeDtypeStruct((m, n), BF16)
    return pl.pallas_call(
        body, name=name, grid=(m // tm, n // tn, nk),
        in_specs=[pl.BlockSpec((tm, tk), lambda i, j, kk: (i, kk)), pl.BlockSpec((tk, tn), lambda i, j, kk: (kk, j))],
        out_specs=out_spec, out_shape=out_shape, scratch_shapes=[pltpu.VMEM((tm, tn), F32)],
        compiler_params=_params(("parallel", "parallel", "arbitrary")))(at, b)


def _piece_ranges(pieces, tn):
    los, lo = [], 0
    for p in pieces:
        los.append(lo)
        lo += p.shape[1] // tn
    return los, lo


def _w_in_grad(xt, pieces, tn):
    d, s = xt.shape
    los, n_blocks = _piece_ranges(pieces, tn)
    per_chip = n_blocks // N_CHIPS
    tm = _divisor_tile(d, 1024, 16)
    tk = _divisor_tile(s, 2048, 128)
    nk = s // tk
    tnn = min(tn, 512)
    sub = tn // tnn

    def body(*refs):
        a_ref, piece_refs, o_ref, acc_ref = refs[0], refs[1:1 + len(pieces)], refs[-2], refs[-1]
        j, kk = pl.program_id(0) // sub, pl.program_id(2)

        @pl.when(kk == 0)
        def _():
            acc_ref[...] = jnp.zeros_like(acc_ref)

        for p_ref, lo, piece in zip(piece_refs, los, pieces):
            @pl.when((j >= lo) & (j < lo + piece.shape[1] // tn))
            def _(p_ref=p_ref):
                acc_ref[...] += jnp.dot(a_ref[...], p_ref[...], preferred_element_type=F32)

        @pl.when(kk == nk - 1)
        def _():
            o_ref[...] = acc_ref[...].astype(BF16)

    def piece_spec(lo, piece):
        hi = lo + piece.shape[1] // tn

        def index(jj, i, kk):
            inside = (jj // sub >= lo) & (jj // sub < hi)
            return jnp.where(inside, kk, 0), jnp.clip(jj - lo * sub, 0, (hi - lo) * sub - 1)

        return pl.BlockSpec((tk, tnn), index)

    per = per_chip * sub
    return pl.pallas_call(
        body, name="w_in_grad", grid=(n_blocks * sub, d // tm, nk),
        in_specs=[pl.BlockSpec((tm, tk), lambda jj, i, kk: (i, kk))] + [piece_spec(lo, p) for lo, p in zip(los, pieces)],
        out_specs=pl.BlockSpec((None, tm, tnn), lambda jj, i, kk: (jj // per, i, jj % per)),
        out_shape=jax.ShapeDtypeStruct((N_CHIPS, d, per_chip * tn), BF16),
        scratch_shapes=[pltpu.VMEM((tm, tnn), F32)],
        compiler_params=_params(("parallel", "parallel", "arbitrary")))(xt, *pieces)


def _x_grad(pieces, w4, dr, tn):
    s, d = dr.shape
    los, n_blocks = _piece_ranges(pieces, tn)
    per_chip = n_blocks // N_CHIPS
    tm = _divisor_tile(s, 512, 16)

    def body(*refs):
        piece_refs, w_ref, dr_ref, o_ref, acc_ref = refs[:len(pieces)], refs[-4], refs[-3], refs[-2], refs[-1]
        j = pl.program_id(1)

        @pl.when(j == 0)
        def _():
            acc_ref[...] = ALPHA * dr_ref[...]

        for p_ref, lo, piece in zip(piece_refs, los, pieces):
            @pl.when((j >= lo) & (j < lo + piece.shape[1] // tn))
            def _(p_ref=p_ref):
                acc_ref[...] += lax.dot_general(p_ref[...], w_ref[...], NT, preferred_element_type=F32)

        @pl.when(j == n_blocks - 1)
        def _():
            o_ref[...] = acc_ref[...]

    def piece_spec(lo, piece):
        hi = lo + piece.shape[1] // tn
        return pl.BlockSpec((tm, tn), lambda i, j: (i, jnp.clip(j - lo, 0, hi - lo - 1)))

    row = pl.BlockSpec((tm, d), lambda i, j: (i, 0))
    return pl.pallas_call(
        body, name="x_grad", grid=(s // tm, n_blocks),
        in_specs=[piece_spec(lo, p) for lo, p in zip(los, pieces)]
        + [pl.BlockSpec((None, d, tn), lambda i, j: (j // per_chip, 0, j % per_chip)), row],
        out_specs=row, out_shape=jax.ShapeDtypeStruct((s, d), F32), scratch_shapes=[pltpu.VMEM((tm, d), F32)],
        compiler_params=_params(("parallel", "arbitrary"), vmem_mib=56))(*pieces, w4, dr)


def _to_subsequences(t, dilation):
    s, w = t.shape
    return t.reshape(s // dilation, dilation, w).transpose(1, 0, 2).reshape(s, w)


def _from_subsequences(t, dilation):
    s, w = t.shape
    return t.reshape(dilation, s // dilation, w).transpose(1, 0, 2).reshape(s, w)


def _per_group(t, to_groups):
    if to_groups:
        return jnp.stack([_to_subsequences(t, dil) for dil in DILATIONS])
    return jnp.stack([_from_subsequences(t[g], dil) for g, dil in enumerate(DILATIONS)])


def _local_step(x, target, w_in4, b_gate, w_pool, pool_scale, wpa4, wpp4, w_out, gamma, beta):
    s, d = x.shape
    aw, pw = wpa4.shape[1], wpp4.shape[1]
    n_groups = len(DILATIONS)
    tn = _col_tile(aw, pw, w_in4.shape[2])
    qkv_w = 3 * n_groups * aw

    xb = x.astype(BF16)
    hqkv = _in_proj(xb, w_in4, 0, qkv_w, tn, BF16, "in_proj_qkv")
    zuz = _in_proj(xb, w_in4, qkv_w, aw + 2 * pw, tn, F32, "in_proj_zuz")
    gpre = _in_proj(xb, w_in4, qkv_w + aw + 2 * pw, 2 * d, tn, F32, "in_proj_gates")

    def grouped(which):
        cols = [hqkv[:, (which * n_groups + g) * aw:(which * n_groups + g + 1) * aw] for g in range(n_groups)]
        return jnp.stack([_to_subsequences(c, dil) for c, dil in zip(cols, DILATIONS)])

    qp, kp, vp = grouped(0), grouped(1), grouped(2)
    o3p, l3p = _attn_fwd(qp, kp, vp)
    o, lse, y_attn = _combine_groups(_per_group(o3p, False), _per_group(l3p, False), zuz, aw)
    pooled, lin, y_pool = _pool_fwd(zuz, w_pool, pool_scale, aw, pw)
    a, p, merged = _proj_merge(y_attn, y_pool, wpa4, wpp4, gpre, b_gate)
    dr, drb, loss_lanes, d_gamma, d_beta = _out_norm_loss(merged, w_out, x, target, gamma, beta)

    da, dp, d_gpre, d_b_gate = _merge_bwd(drb, w_out, a, p, gpre, b_gate)
    d_w_out = _weight_grad(merged.T, drb, d // N_CHIPS, False, "w_out_grad")
    d_wpa4 = _weight_grad(y_attn.T, da, d // N_CHIPS, True, "w_proj_attn_grad")
    d_wpp4 = _weight_grad(y_pool.T, dp, d // N_CHIPS, True, "w_proj_pool_grad")
    d_o, dd, d_z_attn = _attn_gate_bwd(da, wpa4, zuz, o)
    d_z_pool, d_pooled, d_w_pool, d_pool_scale = _pool_gate_bwd(dp, wpp4, zuz, lin, pooled, w_pool, pool_scale, aw)
    d_u = _pool_bwd(d_pooled)
    dqp, dkp, dvp = _attn_bwd(qp, kp, vp, _per_group(d_o, True), _per_group(lse, True), _per_group(dd, True))
    d_qkv = jnp.concatenate([_from_subsequences(t[g], dil) for t in (dqp, dkp, dvp)
                             for g, dil in enumerate(DILATIONS)], axis=1)
    pieces = [d_qkv, d_z_attn, d_u, d_z_pool, d_gpre]
    d_w_in4 = _w_in_grad(xb.T, pieces, tn)
    d_x = _x_grad(pieces, w_in4, dr, tn)
    return dict(d_x=d_x, loss_lanes=loss_lanes, w_in=d_w_in4, b_gate=d_b_gate, w_pool=d_w_pool,
                pool_scale=d_pool_scale, w_proj_attn=d_wpa4, w_proj_pool=d_wpp4, w_out=d_w_out,
                ln_gamma=d_gamma, ln_beta=d_beta)


def _pack_small(wpa, wpp, w_out, w_pool):
    width = wpa.shape[1]
    return jnp.concatenate([wpa, wpp, w_out.reshape(-1, width), w_pool.reshape(-1, width)], axis=0)


def _unpack_small(packed, aw, pw, d, pg):
    lead = packed.shape[:-2]
    width = d // N_CHIPS
    r0, r1, r2 = aw, aw + pw, aw + pw + d
    return (packed[..., :r0, :], packed[..., r0:r1, :], packed[..., r1:r2, :].reshape(lead + (width, d)),
            packed[..., r2:, :].reshape(lead + (len(POOL_WINDOWS), pg // N_CHIPS, pg)))


def _pack_rows(vectors, rows):
    flat = jnp.concatenate([v.reshape(-1) for v in vectors])
    return jnp.pad(flat, (0, rows * 128 - flat.shape[0])).reshape(rows, 128)


def _unpack_rows(packed, sizes):
    flat, out, lo = packed.reshape(-1), [], 0
    for n in sizes:
        out.append(flat[lo:lo + n].reshape(1, n))
        lo += n
    return out


def kernel(x, w_in, b_gate, w_pool, pool_scale, w_proj_attn, w_proj_pool, w_out, ln_gamma, ln_beta, loss_target, m_w_in, m_b_gate, m_w_pool, m_pool_scale, m_w_proj_attn, m_w_proj_pool, m_w_out, m_ln_gamma, m_ln_beta, v_w_in, v_b_gate, v_w_pool, v_pool_scale, v_w_proj_attn, v_w_proj_pool, v_w_out, v_ln_gamma, v_ln_beta):
    s, d = x.shape[1], x.shape[2]
    aw, pw = w_proj_attn.shape[1], w_proj_pool.shape[1]
    pg = w_pool.shape[3]
    n_win = len(POOL_WINDOWS)

    def small(wpa, wpp, wo, wpl):
        return _pack_small(wpa[0], wpp[0], wo[0], wpl[0])

    w_small = small(w_proj_attn, w_proj_pool, w_out, w_pool)
    w_in4, small4 = _gather_weights([w_in[0].astype(BF16), w_small.astype(BF16)])
    wpa4, wpp4, w_out4, w_pool4 = _unpack_small(small4, aw, pw, d, pg)
    w_out_full = w_out4.reshape(d, d)
    w_pool_full = w_pool4.transpose(1, 0, 2, 3).reshape(n_win, pg, pg)

    g = _local_step(x[0], loss_target[0], w_in4, b_gate, w_pool_full, pool_scale, wpa4, wpp4, w_out_full,
                    ln_gamma, ln_beta)

    g_pool4 = g["w_pool"].reshape(n_win, N_CHIPS, pg // N_CHIPS, pg).transpose(1, 0, 2, 3).astype(BF16)
    g_out4 = g["w_out"].reshape(N_CHIPS, d // N_CHIPS, d)
    g_small4 = jnp.concatenate([g["w_proj_attn"], g["w_proj_pool"], g_out4.reshape(N_CHIPS, -1, d // N_CHIPS),
                                g_pool4.reshape(N_CHIPS, -1, d // N_CHIPS)], axis=1)
    mine_big, mine_small, theirs_big, theirs_small = _swap_halves([g["w_in"], g_small4])
    chip_big = _add_pair(mine_big, theirs_big, "add_cores_big")
    chip_small = _add_pair(mine_small, theirs_small, "add_cores_small")
    got_big, got_small = _scatter_to_chips([chip_big, chip_small])
    half_big = _sum_slots(got_big, "sum_chips_big")
    half_small = _sum_slots(got_small, "sum_chips_small")
    grad_w_in, grad_small = _join_halves([half_big, half_small])

    sizes = [b_gate.shape[1], pool_scale.shape[1], d, d, 1]
    rows = -(-sum(sizes) // (8 * 128)) * 8
    loss_part = (0.5 / d) * jnp.sum(g["loss_lanes"]).reshape(1, 1)
    parts = _gather_rows(_pack_rows([g["b_gate"], g["pool_scale"], g["ln_gamma"], g["ln_beta"], loss_part], rows))
    zero = jnp.zeros((1, 1), F32)
    packed = [_pack_rows(vs, rows) for vs in ([b_gate, pool_scale, ln_gamma, ln_beta, zero],
                                              [m_b_gate, m_pool_scale, m_ln_gamma, m_ln_beta, zero],
                                              [v_b_gate, v_pool_scale, v_ln_gamma, v_ln_beta, zero])]
    rep = [_unpack_rows(t, sizes) for t in _sum_rows_adamw(parts, *packed)]
    loss = rep[0][4].reshape(())

    upd_in = _adamw(w_in[0], grad_w_in, m_w_in[0], v_w_in[0], "adamw_w_in")
    upd_small = _adamw(w_small, grad_small, small(m_w_proj_attn, m_w_proj_pool, m_w_out, m_w_pool),
                       small(v_w_proj_attn, v_w_proj_pool, v_w_out, v_w_pool), "adamw_small")

    def leaves(big, packed_small, replicated):
        wpa_, wpp_, wo_, wpl_ = _unpack_small(packed_small, aw, pw, d, pg)
        return [big[None], replicated[0], wpl_[None], replicated[1], wpa_[None], wpp_[None], wo_[None],
                replicated[2], replicated[3]]

    out = [loss, g["d_x"][None]]
    out += leaves(grad_w_in, grad_small, rep[0])
    for i in range(3):
        out += leaves(upd_in[i], upd_small[i], rep[1 + i])
    return tuple(out)
```

```python
import math

import jax
import jax.numpy as jnp
from jax import lax
from jax.experimental import pallas as pl
from jax.experimental.pallas import tpu as pltpu

F32 = jnp.float32
BF16 = jnp.bfloat16
MESH = pl.DeviceIdType.MESH
ANY = pl.BlockSpec(memory_space=pl.ANY)

HEAD_DIM = 128
STEPS = 128
DILATIONS = (1, 4, 16)
N_GROUPS = len(DILATIONS)
POOL_WINDOWS = (2, 4, 8, 16)
POOL_HALO = 16
N_CHIPS = 4
N_DEV = 8
ALPHA = 2.0 ** 0.25
LN_EPS = 1e-5
NEG_INF = -1e30
SCORE_SCALE = HEAD_DIM ** -0.5
ADAM_LR = 0.001
ADAM_B1 = 0.9
ADAM_B2 = 0.999
ADAM_EPS = 1e-08
ADAM_WD = 0.01
ADAM_STEP = 10
MIB = 2 ** 20
NT = (((1,), (1,)), ((), ()))
DMA_STREAMS = 8


def _params(semantics=None, vmem_mib=48):
    return pltpu.CompilerParams(dimension_semantics=semantics, vmem_limit_bytes=vmem_mib * MIB)


def _divisor_tile(n, target, multiple):
    best = None
    for t in range(multiple, min(n, target) + 1, multiple):
        if n % t == 0:
            best = t
    assert best is not None, (n, target, multiple)
    return best


def _col_tile(*widths):
    g = 0
    for w in widths:
        g = math.gcd(g, w)
    return _divisor_tile(g, 1024, 128)


def _sigmoid(z):
    return jax.nn.sigmoid(z)


def _dsilu(z, sg):
    return sg * (1.0 + z * (1.0 - sg))


def _place():
    x, y, c = lax.axis_index("x"), lax.axis_index("y"), lax.axis_index("c")
    others = [(1 - x, y), (x, 1 - y), (1 - x, 1 - y)]
    return x, y, c, (x, y, 1 - c), others


def _remote(src, dst, send_sem, recv_sem, dev):
    return pltpu.make_async_remote_copy(src_ref=src, dst_ref=dst, send_sem=send_sem, recv_sem=recv_sem,
                                        device_id=dev, device_id_type=MESH)


def _row_pieces(n_rows, streams=DMA_STREAMS, multiple=16):
    size = -(-n_rows // (streams * multiple)) * multiple
    return [(lo, min(size, n_rows - lo)) for lo in range(0, n_rows, size)]


def _start_streams(make, n_rows):
    for lo, size in _row_pieces(n_rows):
        make(pl.ds(lo, size)).start()


def _gather_weights(shards):
    n = len(shards)

    def body(*refs):
        src, dst = refs[:n], refs[n:2 * n]
        send_sems, recv_sems, local_sems = refs[2 * n:]
        x, y, c, sibling, others = _place()
        me = 2 * x + y
        sent = []
        for i in range(n):
            half = src[i].shape[0] // 2
            mine = c * half
            _start_streams(lambda r, i=i: pltpu.make_async_copy(src[i].at[r], dst[i].at[me, r], local_sems.at[i]),
                           src[i].shape[0])
            for j, (ox, oy) in enumerate(others):
                cp = _remote(src[i].at[pl.ds(mine, half)], dst[i].at[me, pl.ds(mine, half)],
                             send_sems.at[6 * i + j], recv_sems.at[6 * i + j], (ox, oy, c))
                cp.start()
                sent.append(cp)
        for i in range(n):
            half = src[i].shape[0] // 2
            mine = c * half
            for j, (ox, oy) in enumerate(others):
                blk = dst[i].at[2 * ox + oy]
                slab = blk.at[pl.ds(mine, half)]
                _remote(slab, slab, send_sems.at[6 * i + j], recv_sems.at[6 * i + j], (ox, oy, c)).wait_recv()
                k = 6 * i + 3 + j
                _start_streams(lambda r, blk=blk, k=k: _remote(blk.at[pl.ds(mine + r.start, r.size)],
                                                               blk.at[pl.ds(mine + r.start, r.size)],
                                                               send_sems.at[k], recv_sems.at[k], sibling), half)
                sent.append(_remote(slab, slab, send_sems.at[k], recv_sems.at[k], sibling))
        for i in range(n):
            half = src[i].shape[0] // 2
            for j, (ox, oy) in enumerate(others):
                slab = dst[i].at[2 * ox + oy, pl.ds((1 - c) * half, half)]
                _remote(slab, slab, send_sems.at[6 * i + 3 + j], recv_sems.at[6 * i + 3 + j], sibling).wait_recv()
        for cp in sent:
            cp.wait_send()
        for i in range(n):
            pltpu.make_async_copy(src[i], dst[i].at[me], local_sems.at[i]).wait()

    return pl.pallas_call(
        body, name="gather_weights",
        out_shape=[jax.ShapeDtypeStruct((N_CHIPS,) + s.shape, s.dtype) for s in shards],
        in_specs=[ANY] * n, out_specs=[ANY] * n,
        scratch_shapes=[pltpu.SemaphoreType.DMA((6 * n,)), pltpu.SemaphoreType.DMA((6 * n,)),
                        pltpu.SemaphoreType.DMA((n,))],
    )(*shards)


def _swap_halves(grads):
    n = len(grads)

    def body(*refs):
        g, mine, theirs = refs[:n], refs[n:2 * n], refs[2 * n:3 * n]
        send_sems, recv_sems, local_sems = refs[3 * n:]
        x, y, c, sibling, _ = _place()
        for i in range(n):
            half = g[i].shape[1] // 2
            keep, give = c * half, (1 - c) * half
            for b in range(N_CHIPS):
                _start_streams(lambda r, i=i, b=b: pltpu.make_async_copy(
                    g[i].at[b, pl.ds(keep + r.start, r.size)], mine[i].at[b, r], local_sems.at[i]), half)
                _start_streams(lambda r, i=i, b=b: _remote(
                    g[i].at[b, pl.ds(give + r.start, r.size)], theirs[i].at[b, r], send_sems.at[i], recv_sems.at[i],
                    sibling), half)
        for i in range(n):
            _remote(theirs[i], theirs[i], send_sems.at[i], recv_sems.at[i], sibling).wait()
            pltpu.make_async_copy(mine[i], mine[i], local_sems.at[i]).wait()

    halves = [jax.ShapeDtypeStruct((s.shape[0], s.shape[1] // 2) + s.shape[2:], s.dtype) for s in grads]
    return pl.pallas_call(
        body, name="swap_halves", out_shape=halves + halves,
        in_specs=[ANY] * n, out_specs=[ANY] * (2 * n),
        scratch_shapes=[pltpu.SemaphoreType.DMA((n,)), pltpu.SemaphoreType.DMA((n,)), pltpu.SemaphoreType.DMA((n,))],
    )(*grads)


def _scatter_to_chips(sums):
    n = len(sums)

    def body(*refs):
        s, got = refs[:n], refs[n:2 * n]
        send_sems, recv_sems, local_sems = refs[2 * n:]
        x, y, c, _, others = _place()
        me = 2 * x + y
        sent = []
        for i in range(n):
            _start_streams(lambda r, i=i: pltpu.make_async_copy(s[i].at[me, r], got[i].at[me, r], local_sems.at[i]),
                           s[i].shape[1])
            for j, (ox, oy) in enumerate(others):
                cp = _remote(s[i].at[2 * ox + oy], got[i].at[me], send_sems.at[3 * i + j], recv_sems.at[3 * i + j],
                             (ox, oy, c))
                cp.start()
                sent.append(cp)
        for i in range(n):
            for j, (ox, oy) in enumerate(others):
                slot = got[i].at[2 * ox + oy]
                _remote(slot, slot, send_sems.at[3 * i + j], recv_sems.at[3 * i + j], (ox, oy, c)).wait_recv()
        for cp in sent:
            cp.wait_send()
        for i in range(n):
            pltpu.make_async_copy(s[i].at[me], got[i].at[me], local_sems.at[i]).wait()

    return pl.pallas_call(
        body, name="scatter_to_chips", out_shape=[jax.ShapeDtypeStruct(s.shape, s.dtype) for s in sums],
        in_specs=[ANY] * n, out_specs=[ANY] * n,
        scratch_shapes=[pltpu.SemaphoreType.DMA((3 * n,)), pltpu.SemaphoreType.DMA((3 * n,)),
                        pltpu.SemaphoreType.DMA((n,))],
    )(*sums)


def _join_halves(halves):
    n = len(halves)

    def body(*refs):
        h, full = refs[:n], refs[n:2 * n]
        send_sems, recv_sems, local_sems = refs[2 * n:]
        x, y, c, sibling, _ = _place()
        for i in range(n):
            half = h[i].shape[0]
            mine = c * half
            _start_streams(lambda r, i=i: pltpu.make_async_copy(
                h[i].at[r], full[i].at[pl.ds(mine + r.start, r.size)], local_sems.at[i]), half)
            _start_streams(lambda r, i=i: _remote(
                h[i].at[r], full[i].at[pl.ds(mine + r.start, r.size)], send_sems.at[i], recv_sems.at[i], sibling), half)
        for i in range(n):
            half = h[i].shape[0]
            theirs = full[i].at[pl.ds((1 - c) * half, half)]
            cp = _remote(h[i], theirs, send_sems.at[i], recv_sems.at[i], sibling)
            cp.wait_recv()
            cp.wait_send()
            pltpu.make_async_copy(h[i], theirs, local_sems.at[i]).wait()

    return pl.pallas_call(
        body, name="join_halves",
        out_shape=[jax.ShapeDtypeStruct((2 * s.shape[0],) + s.shape[1:], s.dtype) for s in halves],
        in_specs=[ANY] * n, out_specs=[ANY] * n,
        scratch_shapes=[pltpu.SemaphoreType.DMA((n,)), pltpu.SemaphoreType.DMA((n,)), pltpu.SemaphoreType.DMA((n,))],
    )(*halves)


def _gather_rows(row):
    def body(row_ref, out_ref, send_sems, recv_sems, local_sem):
        x, y, c = lax.axis_index("x"), lax.axis_index("y"), lax.axis_index("c")
        me = 4 * x + 2 * y + c
        local = pltpu.make_async_copy(row_ref, out_ref.at[me], local_sem)
        local.start()
        sent = []
        peers = []
        for k in range(1, N_DEV):
            px, py, pc = x ^ (k >> 2), y ^ ((k >> 1) & 1), c ^ (k & 1)
            peers.append((k, px, py, pc))
            cp = _remote(row_ref, out_ref.at[me], send_sems.at[k - 1], recv_sems.at[k - 1], (px, py, pc))
            cp.start()
            sent.append(cp)
        for k, px, py, pc in peers:
            slot = out_ref.at[4 * px + 2 * py + pc]
            _remote(slot, slot, send_sems.at[k - 1], recv_sems.at[k - 1], (px, py, pc)).wait_recv()
        for cp in sent:
            cp.wait_send()
        local.wait()

    return pl.pallas_call(
        body, name="gather_rows", out_shape=jax.ShapeDtypeStruct((N_DEV,) + row.shape, row.dtype),
        in_specs=[ANY], out_specs=ANY,
        scratch_shapes=[pltpu.SemaphoreType.DMA((N_DEV - 1,)), pltpu.SemaphoreType.DMA((N_DEV - 1,)),
                        pltpu.SemaphoreType.DMA],
    )(row)


def _add_pair(a, b, name):
    n, rows, cols = a.shape
    tr = _divisor_tile(rows, max(16, (2 * MIB) // (cols * 4)), 16)

    def body(a_ref, b_ref, o_ref):
        o_ref[...] = (a_ref[...].astype(F32) + b_ref[...].astype(F32)).astype(o_ref.dtype)

    spec = pl.BlockSpec((None, tr, cols), lambda i, r: (i, r, 0))
    return pl.pallas_call(body, name=name, grid=(n, rows // tr), in_specs=[spec, spec], out_specs=spec,
                          out_shape=jax.ShapeDtypeStruct(a.shape, BF16),
                          compiler_params=_params(("parallel", "parallel")))(a, b)


def _sum_slots(a, name):
    n, rows, cols = a.shape
    tr = _divisor_tile(rows, max(16, (2 * MIB) // (cols * 4 * n)), 16)

    def body(a_ref, o_ref):
        acc = a_ref[0].astype(F32)
        for i in range(1, n):
            acc = acc + a_ref[i].astype(F32)
        o_ref[...] = acc

    return pl.pallas_call(body, name=name, grid=(rows // tr,),
                          in_specs=[pl.BlockSpec((n, tr, cols), lambda r: (0, r, 0))],
                          out_specs=pl.BlockSpec((tr, cols), lambda r: (r, 0)),
                          out_shape=jax.ShapeDtypeStruct((rows, cols), F32),
                          compiler_params=_params(("parallel",)))(a)


def _adamw_math(w, g, m, v):
    m = ADAM_B1 * m + (1.0 - ADAM_B1) * g
    v = ADAM_B2 * v + (1.0 - ADAM_B2) * (g * g)
    m_hat = m / (1.0 - ADAM_B1 ** ADAM_STEP)
    v_hat = v / (1.0 - ADAM_B2 ** ADAM_STEP)
    delta = -ADAM_LR * (m_hat / (jnp.sqrt(v_hat) + ADAM_EPS) + ADAM_WD * w)
    return delta, m, v


def _adamw(w, g, m, v, name):
    rows, cols = w.shape
    tr = _divisor_tile(rows, max(8, MIB // (cols * 4)), 8)

    def body(w_ref, g_ref, m_ref, v_ref, d_ref, nm_ref, nv_ref):
        d, nm, nv = _adamw_math(w_ref[...], g_ref[...], m_ref[...], v_ref[...])
        d_ref[...] = d
        nm_ref[...] = nm
        nv_ref[...] = nv

    spec = pl.BlockSpec((tr, cols), lambda r: (r, 0))
    shape = jax.ShapeDtypeStruct((rows, cols), F32)
    return pl.pallas_call(body, name=name, grid=(rows // tr,), in_specs=[spec] * 4, out_specs=[spec] * 3,
                          out_shape=[shape] * 3, compiler_params=_params(("parallel",)))(w, g, m, v)


def _sum_rows_adamw(parts, w, m, v):
    def body(p_ref, w_ref, m_ref, v_ref, g_ref, d_ref, nm_ref, nv_ref):
        g = p_ref[0]
        for i in range(1, N_DEV):
            g = g + p_ref[i]
        d, nm, nv = _adamw_math(w_ref[...], g, m_ref[...], v_ref[...])
        g_ref[...] = g
        d_ref[...] = d
        nm_ref[...] = nm
        nv_ref[...] = nv

    shape = jax.ShapeDtypeStruct(w.shape, F32)
    return pl.pallas_call(body, name="sum_rows_adamw", out_shape=[shape] * 4)(parts, w, m, v)


LANES = 128


def _permute_scratch(rows, width):
    return pltpu.VMEM((width // LANES, rows, LANES), F32)


def _split_rows(value, scratch, dil):
    if dil == 1:
        return [value]
    rows = value.shape[0] // dil
    slabs = value.shape[1] // LANES
    for c in range(slabs):
        scratch[c] = value[:, c * LANES:(c + 1) * LANES]
    return [jnp.concatenate([scratch[c, pl.ds(r, rows, stride=dil), :] for c in range(slabs)], axis=1)
            for r in range(dil)]


def _merge_rows(ref, scratch, dil):
    if dil == 1:
        return ref[0].astype(F32)
    rows = ref.shape[1]
    slabs = ref.shape[2] // LANES
    for r in range(dil):
        part = ref[r].astype(F32)
        for c in range(slabs):
            scratch[c, pl.ds(r, rows, stride=dil), :] = part[:, c * LANES:(c + 1) * LANES]
    return jnp.concatenate([scratch[c] for c in range(slabs)], axis=1)


def _grouped_view(t, dil):
    return t.reshape(dil, t.shape[0] // dil, t.shape[1])


def _grouped_spec(dil, rows, width, index):
    return pl.BlockSpec((dil, rows // dil, width), index)


def _in_proj(xb, w4, col0, ncols, tn, out_dtype, name):
    s, d = xb.shape
    per_chip = w4.shape[2] // tn
    j0 = col0 // tn
    tm = _divisor_tile(s, 512, 16)

    def body(a_ref, b_ref, o_ref):
        o_ref[...] = jnp.dot(a_ref[...], b_ref[...], preferred_element_type=F32).astype(o_ref.dtype)

    return pl.pallas_call(
        body, name=name, grid=(ncols // tn, s // tm),
        in_specs=[pl.BlockSpec((tm, d), lambda j, m: (m, 0)),
                  pl.BlockSpec((None, d, tn), lambda j, m: ((j0 + j) // per_chip, 0, (j0 + j) % per_chip))],
        out_specs=pl.BlockSpec((tm, tn), lambda j, m: (m, j)),
        out_shape=jax.ShapeDtypeStruct((s, ncols), out_dtype),
        compiler_params=_params(("parallel", "parallel")))(xb, w4)


def _in_proj_qkv(xb, w4, g, aw, tn):
    s, d = xb.shape
    dil = DILATIONS[g]
    per_chip = w4.shape[2] // tn
    sub = aw // tn
    tm = _divisor_tile(s, 512, 16 * dil)

    def body(a_ref, b_ref, o_ref, scratch):
        res = jnp.dot(a_ref[...], b_ref[...], preferred_element_type=F32)
        for r, part in enumerate(_split_rows(res, scratch, dil)):
            o_ref[r] = part.astype(BF16)

    def w_index(j, m):
        col = ((j // sub) * N_GROUPS + g) * sub + j % sub
        return col // per_chip, 0, col % per_chip

    out = pl.pallas_call(
        body, name=f"in_proj_qkv{g}", grid=(3 * sub, s // tm),
        in_specs=[pl.BlockSpec((tm, d), lambda j, m: (m, 0)), pl.BlockSpec((None, d, tn), w_index)],
        out_specs=pl.BlockSpec((None, dil, tm // dil, tn), lambda j, m: (j // sub, 0, m, j % sub)),
        out_shape=jax.ShapeDtypeStruct((3, dil, s // dil, aw), BF16),
        scratch_shapes=[_permute_scratch(tm, tn)],
        compiler_params=_params(("parallel", "parallel")))(xb, w4)
    return out.reshape(3, s, aw)


def _window_mask(first):
    qi = lax.broadcasted_iota(jnp.int32, (STEPS, 2 * STEPS), 0)
    kj = lax.broadcasted_iota(jnp.int32, (STEPS, 2 * STEPS), 1)
    lowest = jnp.where(first, STEPS, 0)
    return (kj >= qi) & (kj <= qi + STEPS) & (kj >= lowest)


def _attn_fwd(qkv, g):
    _, s, aw = qkv.shape
    heads = aw // HEAD_DIM
    n_blocks = s // STEPS
    per_seq = n_blocks // DILATIONS[g]

    def body(q_ref, kc_ref, kp_ref, vc_ref, vp_ref, o_ref, l_ref):
        mask = _window_mask(lax.rem(pl.program_id(0), per_seq) == 0)
        for h in range(heads):
            hs = slice(h * HEAD_DIM, (h + 1) * HEAD_DIM)
            kk = jnp.concatenate([kp_ref[:, hs], kc_ref[:, hs]], axis=0)
            vv = jnp.concatenate([vp_ref[:, hs], vc_ref[:, hs]], axis=0)
            sc = lax.dot_general(q_ref[:, hs], kk, NT, preferred_element_type=F32) * SCORE_SCALE
            sc = jnp.where(mask, sc, NEG_INF)
            mx = jnp.max(sc, axis=1, keepdims=True)
            e = jnp.exp(sc - mx)
            den = jnp.sum(e, axis=1, keepdims=True)
            o_ref[:, hs] = jnp.dot(e.astype(BF16), vv, preferred_element_type=F32) / den
            l_ref[:, hs] = jnp.broadcast_to(mx + jnp.log(den), (STEPS, HEAD_DIM))

    def cur(which):
        return pl.BlockSpec((None, STEPS, aw), lambda b: (which, b, 0))

    def prev(which):
        return pl.BlockSpec((None, STEPS, aw), lambda b: (which, jnp.maximum(b - 1, 0), 0))

    out = pl.BlockSpec((STEPS, aw), lambda b: (b, 0))
    shape = jax.ShapeDtypeStruct((s, aw), F32)
    return pl.pallas_call(
        body, name=f"attn_fwd{g}", grid=(n_blocks,),
        in_specs=[cur(0), cur(1), prev(1), cur(2), prev(2)], out_specs=[out, out], out_shape=[shape, shape],
        compiler_params=_params(("parallel",)))(qkv, qkv, qkv, qkv, qkv)


def _combine_groups(os, ls, zuz, aw):
    s = zuz.shape[0]
    tr = _divisor_tile(s, 256, 8 * DILATIONS[-1])

    def body(*refs):
        o_refs, l_refs, z_ref = refs[0:3], refs[3:6], refs[6]
        oo_ref, y_ref, yt_ref = refs[7:10]
        lq_refs, scratch = refs[10:13], refs[13]
        ls_ = [_merge_rows(l_refs[g], scratch, dil) for g, dil in enumerate(DILATIONS)]
        mx = jnp.maximum(jnp.maximum(ls_[0], ls_[1]), ls_[2])
        ws = [jnp.exp(l - mx) for l in ls_]
        den = ws[0] + ws[1] + ws[2]
        o = ws[0] * _merge_rows(o_refs[0], scratch, DILATIONS[0])
        for g in range(1, N_GROUPS):
            o = o + ws[g] * _merge_rows(o_refs[g], scratch, DILATIONS[g])
        o = o / den
        z = z_ref[...]
        y = o * (z * _sigmoid(z))
        oo_ref[...] = o
        y_ref[...] = y.astype(BF16)
        yt_ref[...] = y.T.astype(BF16)
        for g, dil in enumerate(DILATIONS):
            for r, part in enumerate(_split_rows(mx + jnp.log(den), scratch, dil)):
                lq_refs[g][r] = part

    grouped = [_grouped_spec(dil, tr, aw, lambda r: (0, r, 0)) for dil in DILATIONS]
    one = pl.BlockSpec((tr, aw), lambda r: (r, 0))
    f = jax.ShapeDtypeStruct((s, aw), F32)
    out = pl.pallas_call(
        body, name="combine_groups", grid=(s // tr,),
        in_specs=grouped + grouped + [one],
        out_specs=[one, one, pl.BlockSpec((aw, tr), lambda r: (0, r))] + grouped,
        out_shape=[f, jax.ShapeDtypeStruct((s, aw), BF16), jax.ShapeDtypeStruct((aw, s), BF16)]
        + [jax.ShapeDtypeStruct((dil, s // dil, aw), F32) for dil in DILATIONS],
        scratch_shapes=[_permute_scratch(tr, aw)],
        compiler_params=_params(("parallel",)))(
            *[_grouped_view(t, dil) for t, dil in zip(os, DILATIONS)],
            *[_grouped_view(t, dil) for t, dil in zip(ls, DILATIONS)], zuz)
    return out[0], out[1], out[2], [t.reshape(s, aw) for t in out[3:]]


def _pool_counts(row0, rows, window):
    t = row0 + lax.broadcasted_iota(jnp.int32, (rows, 1), 0)
    return jnp.minimum(t + 1, window).astype(F32)


def _pool_fwd(zuz, w_pool, pool_scale, aw, pw):
    s = zuz.shape[0]
    pg = pw // len(POOL_WINDOWS)
    tr = _divisor_tile(s, 256, 128)
    u_col, z_col = aw // pw, aw // pw + 1
    assert aw % pw == 0

    def body(u_ref, up_ref, z_ref, w_ref, sc_ref, p_ref, l_ref, y_ref, yt_ref):
        r = pl.program_id(0)
        u = u_ref[...]
        halo = jnp.where(r > 0, up_ref[...], 0.0)
        ext = jnp.concatenate([halo, u], axis=0)
        pieces, lins = [], []
        for gi, window in enumerate(POOL_WINDOWS):
            cs = slice(gi * pg, (gi + 1) * pg)
            acc = ext[:, cs]
            shift = 1
            while shift < window:
                acc = acc + pltpu.roll(acc, shift, 0)
                shift *= 2
            p = acc[POOL_HALO:] / _pool_counts(r * tr, tr, window) - u[:, cs]
            pieces.append(p)
            lins.append(jnp.dot(p.astype(BF16), w_ref[gi], preferred_element_type=F32))
        p = jnp.concatenate(pieces, axis=1)
        lin = jnp.concatenate(lins, axis=1)
        z = z_ref[...]
        y = lin * sc_ref[...] * (z * _sigmoid(z))
        p_ref[...] = p.astype(BF16)
        l_ref[...] = lin
        y_ref[...] = y.astype(BF16)
        yt_ref[...] = y.T.astype(BF16)

    per = tr // POOL_HALO
    out = pl.BlockSpec((tr, pw), lambda r: (r, 0))
    return pl.pallas_call(
        body, name="pool_fwd", grid=(s // tr,),
        in_specs=[pl.BlockSpec((tr, pw), lambda r: (r, u_col)),
                  pl.BlockSpec((POOL_HALO, pw), lambda r: (jnp.maximum(r * per - 1, 0), u_col)),
                  pl.BlockSpec((tr, pw), lambda r: (r, z_col)),
                  pl.BlockSpec((len(POOL_WINDOWS), pg, pg), lambda r: (0, 0, 0)),
                  pl.BlockSpec((1, pw), lambda r: (0, 0))],
        out_specs=[out, out, out, pl.BlockSpec((pw, tr), lambda r: (0, r))],
        out_shape=[jax.ShapeDtypeStruct((s, pw), BF16), jax.ShapeDtypeStruct((s, pw), F32),
                   jax.ShapeDtypeStruct((s, pw), BF16), jax.ShapeDtypeStruct((pw, s), BF16)],
        compiler_params=_params(("parallel",)))(zuz, zuz, zuz, w_pool, pool_scale)


def _proj_merge(y_attn, y_pool, wpa4, wpp4, gpre, b_gate):
    s, aw = y_attn.shape
    pw = y_pool.shape[1]
    tn = wpa4.shape[2]
    d = N_CHIPS * tn
    tm = _divisor_tile(s, 512, 128)

    def body(ya_ref, yp_ref, wa_ref, wp_ref, ga_ref, gp_ref, ba_ref, bp_ref, a_ref, p_ref, m_ref, mt_ref):
        a = jnp.dot(ya_ref[...], wa_ref[...], preferred_element_type=F32)
        p = jnp.dot(yp_ref[...], wp_ref[...], preferred_element_type=F32)
        merged = _sigmoid(ga_ref[...] + ba_ref[...]) * a + _sigmoid(gp_ref[...] + bp_ref[...]) * p
        a_ref[...] = a
        p_ref[...] = p
        m_ref[...] = merged.astype(BF16)
        mt_ref[...] = merged.T.astype(BF16)

    out = pl.BlockSpec((tm, tn), lambda n, m: (m, n))
    f = jax.ShapeDtypeStruct((s, d), F32)
    return pl.pallas_call(
        body, name="proj_merge", grid=(N_CHIPS, s // tm),
        in_specs=[pl.BlockSpec((tm, aw), lambda n, m: (m, 0)), pl.BlockSpec((tm, pw), lambda n, m: (m, 0)),
                  pl.BlockSpec((None, aw, tn), lambda n, m: (n, 0, 0)),
                  pl.BlockSpec((None, pw, tn), lambda n, m: (n, 0, 0)),
                  pl.BlockSpec((tm, tn), lambda n, m: (m, n)), pl.BlockSpec((tm, tn), lambda n, m: (m, N_CHIPS + n)),
                  pl.BlockSpec((1, tn), lambda n, m: (0, n)), pl.BlockSpec((1, tn), lambda n, m: (0, N_CHIPS + n))],
        out_specs=[out, out, out, pl.BlockSpec((tn, tm), lambda n, m: (n, m))],
        out_shape=[f, f, jax.ShapeDtypeStruct((s, d), BF16), jax.ShapeDtypeStruct((d, s), BF16)],
        compiler_params=_params(("parallel", "parallel")))(y_attn, y_pool, wpa4, wpp4, gpre, gpre, b_gate, b_gate)


def _out_norm_loss(merged, w_out, x, target, gamma, beta):
    s, d = x.shape
    tm = _divisor_tile(s, 256, 16)

    def body(m_ref, w_ref, x_ref, t_ref, g_ref, b_ref, dr_ref, drb_ref, loss_ref, dg_ref, db_ref):
        @pl.when(pl.program_id(0) == 0)
        def _():
            loss_ref[...] = jnp.zeros_like(loss_ref)
            dg_ref[...] = jnp.zeros_like(dg_ref)
            db_ref[...] = jnp.zeros_like(db_ref)

        r = ALPHA * x_ref[...] + jnp.dot(m_ref[...], w_ref[...], preferred_element_type=F32)
        mu = jnp.mean(r, axis=1, keepdims=True)
        rc = r - mu
        rstd = lax.rsqrt(jnp.mean(rc * rc, axis=1, keepdims=True) + LN_EPS)
        xhat = rc * rstd
        diff = xhat * g_ref[...] + b_ref[...] - t_ref[...]
        dy = diff / d
        loss_ref[...] += jnp.sum(diff * diff, axis=0, keepdims=True)
        dg_ref[...] += jnp.sum(dy * xhat, axis=0, keepdims=True)
        db_ref[...] += jnp.sum(dy, axis=0, keepdims=True)
        dxhat = dy * g_ref[...]
        dr = rstd * (dxhat - jnp.mean(dxhat, axis=1, keepdims=True)
                     - xhat * jnp.mean(dxhat * xhat, axis=1, keepdims=True))
        dr_ref[...] = dr
        drb_ref[...] = dr.astype(BF16)

    row = pl.BlockSpec((tm, d), lambda m: (m, 0))
    vec = pl.BlockSpec((1, d), lambda m: (0, 0))
    v = jax.ShapeDtypeStruct((1, d), F32)
    return pl.pallas_call(
        body, name="out_norm_loss", grid=(s // tm,),
        in_specs=[row, pl.BlockSpec((d, d), lambda m: (0, 0)), row, row, vec, vec],
        out_specs=[row, row, vec, vec, vec],
        out_shape=[jax.ShapeDtypeStruct((s, d), F32), jax.ShapeDtypeStruct((s, d), BF16), v, v, v],
        compiler_params=_params(("arbitrary",), vmem_mib=56))(merged, w_out, x, target, gamma, beta)


def _merge_bwd(drb, w_out, a, p, gpre, b_gate):
    s, d = drb.shape
    tm = _divisor_tile(s, 128, 16)

    def body(dr_ref, w_ref, a_ref, p_ref, ga_ref, gp_ref, ba_ref, bp_ref, da_ref, dp_ref, dg_ref, db_ref):
        @pl.when(pl.program_id(0) == 0)
        def _():
            db_ref[...] = jnp.zeros_like(db_ref)

        dm = lax.dot_general(dr_ref[...], w_ref[...], NT, preferred_element_type=F32)
        sa = _sigmoid(ga_ref[...] + ba_ref[...])
        sp = _sigmoid(gp_ref[...] + bp_ref[...])
        da_ref[...] = (dm * sa).astype(BF16)
        dp_ref[...] = (dm * sp).astype(BF16)
        dga = dm * a_ref[...] * sa * (1.0 - sa)
        dgp = dm * p_ref[...] * sp * (1.0 - sp)
        dg_ref[:, :d] = dga.astype(BF16)
        dg_ref[:, d:] = dgp.astype(BF16)
        db_ref[:, :d] += jnp.sum(dga, axis=0, keepdims=True)
        db_ref[:, d:] += jnp.sum(dgp, axis=0, keepdims=True)

    row = pl.BlockSpec((tm, d), lambda m: (m, 0))
    row1 = pl.BlockSpec((tm, d), lambda m: (m, 1))
    return pl.pallas_call(
        body, name="merge_bwd", grid=(s // tm,),
        in_specs=[row, pl.BlockSpec((d, d), lambda m: (0, 0)), row, row, row, row1,
                  pl.BlockSpec((1, d), lambda m: (0, 0)), pl.BlockSpec((1, d), lambda m: (0, 1))],
        out_specs=[row, row, pl.BlockSpec((tm, 2 * d), lambda m: (m, 0)), pl.BlockSpec((1, 2 * d), lambda m: (0, 0))],
        out_shape=[jax.ShapeDtypeStruct((s, d), BF16), jax.ShapeDtypeStruct((s, d), BF16),
                   jax.ShapeDtypeStruct((s, 2 * d), BF16), jax.ShapeDtypeStruct((1, 2 * d), F32)],
        compiler_params=_params(("arbitrary",)))(drb, w_out, a, p, gpre, gpre, b_gate, b_gate)


def _proj_t(dy_ref, w_ref, tn):
    acc = None
    for n in range(N_CHIPS):
        t = lax.dot_general(dy_ref[:, n * tn:(n + 1) * tn], w_ref[n], NT, preferred_element_type=F32)
        acc = t if acc is None else acc + t
    return acc


def _attn_gate_bwd(da, wpa4, zuz, o):
    s, d = da.shape
    aw, tn = wpa4.shape[1], wpa4.shape[2]
    heads = aw // HEAD_DIM
    tm = _divisor_tile(s, 256, 16 * DILATIONS[-1])

    def body(*refs):
        da_ref, w_ref, z_ref, o_ref, dz_ref = refs[:5]
        do_refs, dd_refs, scratch = refs[5:8], refs[8:11], refs[11]
        dy = _proj_t(da_ref, w_ref, tn)
        z, o = z_ref[...], o_ref[...]
        sg = _sigmoid(z)
        do = dy * (z * sg)
        dz_ref[...] = (dy * o * _dsilu(z, sg)).astype(BF16)
        prod = do * o
        dd = jnp.concatenate(
            [jnp.broadcast_to(jnp.sum(prod[:, h * HEAD_DIM:(h + 1) * HEAD_DIM], axis=1, keepdims=True),
                              (tm, HEAD_DIM)) for h in range(heads)], axis=1)
        for g, dil in enumerate(DILATIONS):
            for r, part in enumerate(_split_rows(do, scratch, dil)):
                do_refs[g][r] = part.astype(BF16)
            for r, part in enumerate(_split_rows(dd, scratch, dil)):
                dd_refs[g][r] = part

    row = pl.BlockSpec((tm, aw), lambda m: (m, 0))
    grouped = [_grouped_spec(dil, tm, aw, lambda m: (0, m, 0)) for dil in DILATIONS]
    out = pl.pallas_call(
        body, name="attn_gate_bwd", grid=(s // tm,),
        in_specs=[pl.BlockSpec((tm, d), lambda m: (m, 0)), pl.BlockSpec((N_CHIPS, aw, tn), lambda m: (0, 0, 0)),
                  row, row],
        out_specs=[row] + grouped + grouped,
        out_shape=[jax.ShapeDtypeStruct((s, aw), BF16)]
        + [jax.ShapeDtypeStruct((dil, s // dil, aw), BF16) for dil in DILATIONS]
        + [jax.ShapeDtypeStruct((dil, s // dil, aw), F32) for dil in DILATIONS],
        scratch_shapes=[_permute_scratch(tm, aw)],
        compiler_params=_params(("parallel",)))(da, wpa4, zuz, o)
    return out[0], [t.reshape(s, aw) for t in out[1:4]], [t.reshape(s, aw) for t in out[4:7]]


def _pool_gate_bwd(dp_in, wpp4, zuz, lin, pooled, w_pool, pool_scale, aw):
    s, d = dp_in.shape
    pw, tn = wpp4.shape[1], wpp4.shape[2]
    n_win = len(POOL_WINDOWS)
    pg = pw // n_win
    tm = _divisor_tile(s, 256, 16)
    z_col = aw // pw + 1

    def body(dp_ref, w_ref, z_ref, l_ref, p_ref, wp_ref, sc_ref, dz_ref, dpo_ref, dw_ref, ds_ref):
        @pl.when(pl.program_id(0) == 0)
        def _():
            dw_ref[...] = jnp.zeros_like(dw_ref)
            ds_ref[...] = jnp.zeros_like(ds_ref)

        dy = _proj_t(dp_ref, w_ref, tn)
        z, lin_ = z_ref[...], l_ref[...]
        sg = _sigmoid(z)
        dypp = dy * (z * sg)
        dz_ref[...] = (dy * (lin_ * sc_ref[...]) * _dsilu(z, sg)).astype(BF16)
        ds_ref[...] += jnp.sum(dypp * lin_, axis=0, keepdims=True)
        dlin = (dypp * sc_ref[...]).astype(BF16)
        for gi in range(n_win):
            cs = slice(gi * pg, (gi + 1) * pg)
            pt = p_ref[:, cs].astype(F32).T.astype(BF16)
            dw_ref[gi] += jnp.dot(pt, dlin[:, cs], preferred_element_type=F32)
            dpo_ref[:, cs] = lax.dot_general(dlin[:, cs], wp_ref[gi], NT, preferred_element_type=F32)

    row = pl.BlockSpec((tm, pw), lambda m: (m, 0))
    return pl.pallas_call(
        body, name="pool_gate_bwd", grid=(s // tm,),
        in_specs=[pl.BlockSpec((tm, d), lambda m: (m, 0)), pl.BlockSpec((N_CHIPS, pw, tn), lambda m: (0, 0, 0)),
                  pl.BlockSpec((tm, pw), lambda m: (m, z_col)), row, row,
                  pl.BlockSpec((n_win, pg, pg), lambda m: (0, 0, 0)), pl.BlockSpec((1, pw), lambda m: (0, 0))],
        out_specs=[row, row, pl.BlockSpec((n_win, pg, pg), lambda m: (0, 0, 0)),
                   pl.BlockSpec((1, pw), lambda m: (0, 0))],
        out_shape=[jax.ShapeDtypeStruct((s, pw), BF16), jax.ShapeDtypeStruct((s, pw), F32),
                   jax.ShapeDtypeStruct((n_win, pg, pg), F32), jax.ShapeDtypeStruct((1, pw), F32)],
        compiler_params=_params(("arbitrary",)))(dp_in, wpp4, zuz, lin, pooled, w_pool, pool_scale)


def _pool_bwd(dpooled):
    s, pw = dpooled.shape
    pg = pw // len(POOL_WINDOWS)
    tr = _divisor_tile(s, 256, POOL_HALO)
    per = tr // POOL_HALO
    n_tiles = s // tr

    def body(c_ref, n_ref, du_ref):
        r = pl.program_id(0)
        cur = c_ref[...]
        halo = jnp.where(r < n_tiles - 1, n_ref[...], 0.0)
        ext = jnp.concatenate([cur, halo], axis=0)
        rows = tr + POOL_HALO
        for gi, window in enumerate(POOL_WINDOWS):
            cs = slice(gi * pg, (gi + 1) * pg)
            acc = ext[:, cs] / _pool_counts(r * tr, rows, window)
            shift = 1
            while shift < window:
                acc = acc + pltpu.roll(acc, rows - shift, 0)
                shift *= 2
            du_ref[:, cs] = (acc[:tr] - cur[:, cs]).astype(BF16)

    return pl.pallas_call(
        body, name="pool_bwd", grid=(n_tiles,),
        in_specs=[pl.BlockSpec((tr, pw), lambda r: (r, 0)),
                  pl.BlockSpec((POOL_HALO, pw), lambda r: (jnp.minimum((r + 1) * per, s // POOL_HALO - 1), 0))],
        out_specs=pl.BlockSpec((tr, pw), lambda r: (r, 0)),
        out_shape=jax.ShapeDtypeStruct((s, pw), BF16), compiler_params=_params(("parallel",)))(dpooled, dpooled)


def _attn_bwd(qkv, do, lse, dd, g):
    _, s, aw = qkv.shape
    heads = aw // HEAD_DIM
    n_blocks = s // STEPS
    per_seq = n_blocks // DILATIONS[g]

    def body(q_ref, do_ref, l_ref, dd_ref, kc_ref, kp_ref, vc_ref, vp_ref, out_ref, cq_ref, ck_ref, cv_ref):
        b = pl.program_id(0)

        @pl.when(b == 0)
        def _():
            cq_ref[...] = jnp.zeros_like(cq_ref)
            ck_ref[...] = jnp.zeros_like(ck_ref)
            cv_ref[...] = jnp.zeros_like(cv_ref)

        out_ref[0] = cq_ref[...].astype(BF16)

        @pl.when(b < n_blocks)
        def _():
            mask = _window_mask(lax.rem(b, per_seq) == 0)
            for h in range(heads):
                hs = slice(h * HEAD_DIM, (h + 1) * HEAD_DIM)
                q, do_ = q_ref[:, hs], do_ref[:, hs]
                kk = jnp.concatenate([kp_ref[:, hs], kc_ref[:, hs]], axis=0)
                vv = jnp.concatenate([vp_ref[:, hs], vc_ref[:, hs]], axis=0)
                lse_ = jnp.concatenate([l_ref[:, hs], l_ref[:, hs]], axis=1)
                dd_ = jnp.concatenate([dd_ref[:, hs], dd_ref[:, hs]], axis=1)
                sc = lax.dot_general(q, kk, NT, preferred_element_type=F32) * SCORE_SCALE
                prob = jnp.where(mask, jnp.exp(sc - lse_), 0.0)
                dprob = lax.dot_general(do_, vv, NT, preferred_element_type=F32)
                dsc = prob * (dprob - dd_) * SCORE_SCALE
                cq_ref[:, hs] = jnp.dot(dsc.astype(BF16), kk, preferred_element_type=F32)
                dkk = jnp.dot(dsc.T.astype(BF16), q, preferred_element_type=F32)
                dvv = jnp.dot(prob.T.astype(BF16), do_, preferred_element_type=F32)
                out_ref[1, :, hs] = (ck_ref[:, hs] + dkk[:STEPS]).astype(BF16)
                out_ref[2, :, hs] = (cv_ref[:, hs] + dvv[:STEPS]).astype(BF16)
                ck_ref[:, hs] = dkk[STEPS:]
                cv_ref[:, hs] = dvv[STEPS:]

        @pl.when(b == n_blocks)
        def _():
            out_ref[1] = ck_ref[...].astype(BF16)
            out_ref[2] = cv_ref[...].astype(BF16)

    last = n_blocks - 1

    def cur(which):
        return pl.BlockSpec((None, STEPS, aw), lambda b: (which, jnp.minimum(b, last), 0))

    def prev(which):
        return pl.BlockSpec((None, STEPS, aw), lambda b: (which, jnp.clip(b - 1, 0, last), 0))

    row = pl.BlockSpec((STEPS, aw), lambda b: (jnp.minimum(b, last), 0))
    return pl.pallas_call(
        body, name=f"attn_bwd{g}", grid=(n_blocks + 1,),
        in_specs=[cur(0), row, row, row, cur(1), prev(1), cur(2), prev(2)],
        out_specs=pl.BlockSpec((3, STEPS, aw), lambda b: (0, jnp.clip(b - 1, 0, last), 0)),
        out_shape=jax.ShapeDtypeStruct((3, s, aw), BF16),
        scratch_shapes=[pltpu.VMEM((STEPS, aw), F32)] * 3,
        compiler_params=_params(("arbitrary",)))(qkv, do, lse, dd, qkv, qkv, qkv, qkv)


def _weight_grad(at, b, tn, col_blocks, name):
    m, k = at.shape
    n = b.shape[1]
    tm = _divisor_tile(m, 1024, 16)
    tk = _divisor_tile(k, 2048, 128)
    nk = k // tk

    def body(a_ref, b_ref, o_ref, acc_ref):
        kk = pl.program_id(2)

        @pl.when(kk == 0)
        def _():
            acc_ref[...] = jnp.zeros_like(acc_ref)

        acc_ref[...] += jnp.dot(a_ref[...], b_ref[...], preferred_element_type=F32)

        @pl.when(kk == nk - 1)
        def _():
            o_ref[...] = acc_ref[...].astype(BF16)

    if col_blocks:
        out_spec = pl.BlockSpec((None, tm, tn), lambda i, j, kk: (j, i, 0))
        out_shape = jax.ShapeDtypeStruct((n // tn, m, tn), BF16)
    else:
        out_spec = pl.BlockSpec((tm, tn), lambda i, j, kk: (i, j))
        out_shape = jax.ShapeDtypeStruct((m, n), BF16)
    return pl.pallas_call(
        body, name=name, grid=(m // tm, n // tn, nk),
        in_specs=[pl.BlockSpec((tm, tk), lambda i, j, kk: (i, kk)), pl.BlockSpec((tk, tn), lambda i, j, kk: (kk, j))],
        out_specs=out_spec, out_shape=out_shape, scratch_shapes=[pltpu.VMEM((tm, tn), F32)],
        compiler_params=_params(("parallel", "parallel", "arbitrary")))(at, b)


def _w_in_grad_part(xt, b, col_of, n_local, tn, w_shape, prev, name):
    d, s = xt.shape
    per_chip = w_shape[2] // tn
    tm = _divisor_tile(d, 1024, 16)
    tk = _divisor_tile(s, 2048, 128)
    nk = s // tk

    def body(*refs):
        a_ref, b_ref, o_ref, acc_ref = refs[0], refs[1], refs[-2], refs[-1]
        kk = pl.program_id(2)

        @pl.when(kk == 0)
        def _():
            acc_ref[...] = jnp.zeros_like(acc_ref)

        acc_ref[...] += jnp.dot(a_ref[...], b_ref[...], preferred_element_type=F32)

        @pl.when(kk == nk - 1)
        def _():
            o_ref[...] = acc_ref[...].astype(BF16)

    if b.ndim == 3:
        sub = b.shape[2] // tn
        b_spec = pl.BlockSpec((None, tk, tn), lambda j, i, kk: (j // sub, kk, j % sub))
    else:
        b_spec = pl.BlockSpec((tk, tn), lambda j, i, kk: (kk, j))
    in_specs = [pl.BlockSpec((tm, tk), lambda j, i, kk: (i, kk)), b_spec]
    args = [xt, b]
    aliases = {}
    if prev is not None:
        in_specs.append(ANY)
        args.append(prev)
        aliases = {2: 0}
    return pl.pallas_call(
        body, name=name, grid=(n_local, d // tm, nk), in_specs=in_specs,
        out_specs=pl.BlockSpec((None, tm, tn), lambda j, i, kk: (col_of(j) // per_chip, i, col_of(j) % per_chip)),
        out_shape=jax.ShapeDtypeStruct(w_shape, BF16), scratch_shapes=[pltpu.VMEM((tm, tn), F32)],
        input_output_aliases=aliases,
        compiler_params=_params(("parallel", "parallel", "arbitrary")))(*args)


def _x_grad(dqkv, rest, w4, dr, aw, tn):
    s, d = dr.shape
    sub = aw // tn
    n_qkv = 3 * N_GROUPS * sub
    los, lo = [], n_qkv
    for p in rest:
        los.append(lo)
        lo += p.shape[1] // tn
    n_blocks = lo
    per_chip = n_blocks // N_CHIPS
    tm = _divisor_tile(s, 512, 16 * DILATIONS[-1])

    def body(*refs):
        q_refs, r_refs = refs[:N_GROUPS], refs[N_GROUPS:N_GROUPS + len(rest)]
        w_ref, dr_ref, o_ref, acc_ref, scratch = refs[-5:]
        j = pl.program_id(1)

        @pl.when(j == 0)
        def _():
            acc_ref[...] = ALPHA * dr_ref[...]

        for g, dil in enumerate(DILATIONS):
            @pl.when((j < n_qkv) & (lax.rem(j // sub, N_GROUPS) == g))
            def _(g=g, dil=dil):
                rows = _merge_rows(q_refs[g], scratch, dil).astype(BF16)
                acc_ref[...] += lax.dot_general(rows, w_ref[...], NT, preferred_element_type=F32)

        for p_ref, lo_, piece in zip(r_refs, los, rest):
            @pl.when((j >= lo_) & (j < lo_ + piece.shape[1] // tn))
            def _(p_ref=p_ref):
                acc_ref[...] += lax.dot_general(p_ref[...], w_ref[...], NT, preferred_element_type=F32)

        @pl.when(j == n_blocks - 1)
        def _():
            o_ref[...] = acc_ref[...]

    def qkv_spec(dil):
        def index(i, j):
            region = jnp.minimum(j // sub, 3 * N_GROUPS - 1)
            return region // N_GROUPS, 0, i, jnp.where(j < n_qkv, j % sub, 0)

        return pl.BlockSpec((None, dil, tm // dil, tn), index)

    def rest_spec(lo_, piece):
        n = piece.shape[1] // tn
        return pl.BlockSpec((tm, tn), lambda i, j: (i, jnp.clip(j - lo_, 0, n - 1)))

    row = pl.BlockSpec((tm, d), lambda i, j: (i, 0))
    return pl.pallas_call(
        body, name="x_grad", grid=(s // tm, n_blocks),
        in_specs=[qkv_spec(dil) for dil in DILATIONS] + [rest_spec(lo_, p) for lo_, p in zip(los, rest)]
        + [pl.BlockSpec((None, d, tn), lambda i, j: (j // per_chip, 0, j % per_chip)), row],
        out_specs=row, out_shape=jax.ShapeDtypeStruct((s, d), F32),
        scratch_shapes=[pltpu.VMEM((tm, d), F32), _permute_scratch(tm, tn)],
        compiler_params=_params(("parallel", "arbitrary"), vmem_mib=56))(
            *[t.reshape(3, dil, s // dil, aw) for t, dil in zip(dqkv, DILATIONS)], *rest, w4, dr)


def _to_subsequences(t, dilation):
    s, w = t.shape
    return t.reshape(s // dilation, dilation, w).transpose(1, 0, 2).reshape(s, w)


def _local_step(x, target, w_in4, b_gate, w_pool, pool_scale, wpa4, wpp4, w_out, gamma, beta):
    s, d = x.shape
    aw, pw = wpa4.shape[1], wpp4.shape[1]
    tn = _col_tile(aw, pw, w_in4.shape[2])
    sub = aw // tn
    qkv_w = 3 * N_GROUPS * aw

    xb = x.astype(BF16)
    qkv = [_in_proj_qkv(xb, w_in4, g, aw, tn) for g in range(N_GROUPS)]
    zuz = _in_proj(xb, w_in4, qkv_w, aw + 2 * pw, tn, F32, "in_proj_zuz")
    gpre = _in_proj(xb, w_in4, qkv_w + aw + 2 * pw, 2 * d, tn, F32, "in_proj_gates")

    attn = [_attn_fwd(qkv[g], g) for g in range(N_GROUPS)]
    o, y_attn, y_attn_t, lse = _combine_groups([a[0] for a in attn], [a[1] for a in attn], zuz, aw)
    pooled, lin, y_pool, y_pool_t = _pool_fwd(zuz, w_pool, pool_scale, aw, pw)
    a, p, merged, merged_t = _proj_merge(y_attn, y_pool, wpa4, wpp4, gpre, b_gate)
    dr, drb, loss_lanes, d_gamma, d_beta = _out_norm_loss(merged, w_out, x, target, gamma, beta)

    da, dp, d_gpre, d_b_gate = _merge_bwd(drb, w_out, a, p, gpre, b_gate)
    d_w_out = _weight_grad(merged_t, drb, d // N_CHIPS, False, "w_out_grad")
    d_wpa4 = _weight_grad(y_attn_t, da, d // N_CHIPS, True, "w_proj_attn_grad")
    d_wpp4 = _weight_grad(y_pool_t, dp, d // N_CHIPS, True, "w_proj_pool_grad")
    d_z_attn, d_o, dd = _attn_gate_bwd(da, wpa4, zuz, o)
    d_z_pool, d_pooled, d_w_pool, d_pool_scale = _pool_gate_bwd(dp, wpp4, zuz, lin, pooled, w_pool, pool_scale, aw)
    d_u = _pool_bwd(d_pooled)
    dqkv = [_attn_bwd(qkv[g], d_o[g], lse[g], dd[g], g) for g in range(N_GROUPS)]

    rest = [d_z_attn, d_u, d_z_pool, d_gpre]
    d_w_in4 = None
    for g, dil in enumerate(DILATIONS):
        xt = _to_subsequences(xb, dil).T
        d_w_in4 = _w_in_grad_part(xt, dqkv[g], lambda j, g=g: ((j // sub) * N_GROUPS + g) * sub + j % sub, 3 * sub,
                                  tn, w_in4.shape, d_w_in4, f"w_in_grad_qkv{g}")
    lo = qkv_w // tn
    for i, piece in enumerate(rest):
        n_local = piece.shape[1] // tn
        d_w_in4 = _w_in_grad_part(xb.T, piece, lambda j, lo=lo: lo + j, n_local, tn, w_in4.shape, d_w_in4,
                                  f"w_in_grad_rest{i}")
        lo += n_local
    d_x = _x_grad(dqkv, rest, w_in4, dr, aw, tn)
    return dict(d_x=d_x, loss_lanes=loss_lanes, w_in=d_w_in4, b_gate=d_b_gate, w_pool=d_w_pool,
                pool_scale=d_pool_scale, w_proj_attn=d_wpa4, w_proj_pool=d_wpp4, w_out=d_w_out,
                ln_gamma=d_gamma, ln_beta=d_beta)


def _pack_small(wpa, wpp, w_out, w_pool):
    width = wpa.shape[1]
    return jnp.concatenate([wpa, wpp, w_out.reshape(-1, width), w_pool.reshape(-1, width)], axis=0)


def _unpack_small(packed, aw, pw, d, pg):
    lead = packed.shape[:-2]
    width = d // N_CHIPS
    r0, r1, r2 = aw, aw + pw, aw + pw + d
    return (packed[..., :r0, :], packed[..., r0:r1, :], packed[..., r1:r2, :].reshape(lead + (width, d)),
            packed[..., r2:, :].reshape(lead + (len(POOL_WINDOWS), pg // N_CHIPS, pg)))


def _pack_rows(vectors, rows):
    flat = jnp.concatenate([v.reshape(-1) for v in vectors])
    return jnp.pad(flat, (0, rows * 128 - flat.shape[0])).reshape(rows, 128)


def _unpack_rows(packed, sizes):
    flat, out, lo = packed.reshape(-1), [], 0
    for n in sizes:
        out.append(flat[lo:lo + n].reshape(1, n))
        lo += n
    return out


def kernel(x, w_in, b_gate, w_pool, pool_scale, w_proj_attn, w_proj_pool, w_out, ln_gamma, ln_beta, loss_target, m_w_in, m_b_gate, m_w_pool, m_pool_scale, m_w_proj_attn, m_w_proj_pool, m_w_out, m_ln_gamma, m_ln_beta, v_w_in, v_b_gate, v_w_pool, v_pool_scale, v_w_proj_attn, v_w_proj_pool, v_w_out, v_ln_gamma, v_ln_beta):
    s, d = x.shape[1], x.shape[2]
    aw, pw = w_proj_attn.shape[1], w_proj_pool.shape[1]
    pg = w_pool.shape[3]
    n_win = len(POOL_WINDOWS)

    def small(wpa, wpp, wo, wpl):
        return _pack_small(wpa[0], wpp[0], wo[0], wpl[0])

    w_small = small(w_proj_attn, w_proj_pool, w_out, w_pool)
    w_in4, small4 = _gather_weights([w_in[0].astype(BF16), w_small.astype(BF16)])
    wpa4, wpp4, w_out4, w_pool4 = _unpack_small(small4, aw, pw, d, pg)
    w_out_full = w_out4.reshape(d, d)
    w_pool_full = w_pool4.transpose(1, 0, 2, 3).reshape(n_win, pg, pg)

    g = _local_step(x[0], loss_target[0], w_in4, b_gate, w_pool_full, pool_scale, wpa4, wpp4, w_out_full,
                    ln_gamma, ln_beta)

    g_pool4 = g["w_pool"].reshape(n_win, N_CHIPS, pg // N_CHIPS, pg).transpose(1, 0, 2, 3).astype(BF16)
    g_out4 = g["w_out"].reshape(N_CHIPS, d // N_CHIPS, d)
    g_small4 = jnp.concatenate([g["w_proj_attn"], g["w_proj_pool"], g_out4.reshape(N_CHIPS, -1, d // N_CHIPS),
                                g_pool4.reshape(N_CHIPS, -1, d // N_CHIPS)], axis=1)
    mine_big, mine_small, theirs_big, theirs_small = _swap_halves([g["w_in"], g_small4])
    chip_big = _add_pair(mine_big, theirs_big, "add_cores_big")
    chip_small = _add_pair(mine_small, theirs_small, "add_cores_small")
    got_big, got_small = _scatter_to_chips([chip_big, chip_small])
    half_big = _sum_slots(got_big, "sum_chips_big")
    half_small = _sum_slots(got_small, "sum_chips_small")
    grad_w_in, grad_small = _join_halves([half_big, half_small])

    sizes = [b_gate.shape[1], pool_scale.shape[1], d, d, 1]
    rows = -(-sum(sizes) // (8 * 128)) * 8
    loss_part = (0.5 / d) * jnp.sum(g["loss_lanes"]).reshape(1, 1)
    parts = _gather_rows(_pack_rows([g["b_gate"], g["pool_scale"], g["ln_gamma"], g["ln_beta"], loss_part], rows))
    zero = jnp.zeros((1, 1), F32)
    packed = [_pack_rows(vs, rows) for vs in ([b_gate, pool_scale, ln_gamma, ln_beta, zero],
                                              [m_b_gate, m_pool_scale, m_ln_gamma, m_ln_beta, zero],
                                              [v_b_gate, v_pool_scale, v_ln_gamma, v_ln_beta, zero])]
    rep = [_unpack_rows(t, sizes) for t in _sum_rows_adamw(parts, *packed)]
    loss = rep[0][4].reshape(())

    upd_in = _adamw(w_in[0], grad_w_in, m_w_in[0], v_w_in[0], "adamw_w_in")
    upd_small = _adamw(w_small, grad_small, small(m_w_proj_attn, m_w_proj_pool, m_w_out, m_w_pool),
                       small(v_w_proj_attn, v_w_proj_pool, v_w_out, v_w_pool), "adamw_small")

    def leaves(big, packed_small, replicated):
        wpa_, wpp_, wo_, wpl_ = _unpack_small(packed_small, aw, pw, d, pg)
        return [big[None], replicated[0], wpl_[None], replicated[1], wpa_[None], wpp_[None], wo_[None],
                replicated[2], replicated[3]]

    out = [loss, g["d_x"][None]]
    out += leaves(grad_w_in, grad_small, rep[0])
    for i in range(3):
        out += leaves(upd_in[i], upd_small[i], rep[1 + i])
    return tuple(out)
```

```python
import math

import jax
import jax.numpy as jnp
from jax import lax
from jax.experimental import pallas as pl
from jax.experimental.pallas import tpu as pltpu

F32 = jnp.float32
BF16 = jnp.bfloat16
MESH = pl.DeviceIdType.MESH
ANY = pl.BlockSpec(memory_space=pl.ANY)

HEAD_DIM = 128
STEPS = 128
DILATIONS = (1, 4, 16)
N_GROUPS = len(DILATIONS)
POOL_WINDOWS = (2, 4, 8, 16)
POOL_HALO = 16
N_CHIPS = 4
N_DEV = 8
ALPHA = 2.0 ** 0.25
LN_EPS = 1e-5
NEG_INF = -1e30
SCORE_SCALE = HEAD_DIM ** -0.5
ADAM_LR = 0.001
ADAM_B1 = 0.9
ADAM_B2 = 0.999
ADAM_EPS = 1e-08
ADAM_WD = 0.01
ADAM_STEP = 10
MIB = 2 ** 20
NT = (((1,), (1,)), ((), ()))
DMA_STREAMS = 8


def _params(semantics=None, vmem_mib=48):
    return pltpu.CompilerParams(dimension_semantics=semantics, vmem_limit_bytes=vmem_mib * MIB)


def _divisor_tile(n, target, multiple):
    best = None
    for t in range(multiple, min(n, target) + 1, multiple):
        if n % t == 0:
            best = t
    assert best is not None, (n, target, multiple)
    return best


def _col_tile(*widths):
    g = 0
    for w in widths:
        g = math.gcd(g, w)
    return _divisor_tile(g, 1024, 128)


def _sigmoid(z):
    return jax.nn.sigmoid(z)


def _dsilu(z, sg):
    return sg * (1.0 + z * (1.0 - sg))


def _place():
    x, y, c = lax.axis_index("x"), lax.axis_index("y"), lax.axis_index("c")
    others = [(1 - x, y), (x, 1 - y), (1 - x, 1 - y)]
    return x, y, c, (x, y, 1 - c), others


def _remote(src, dst, send_sem, recv_sem, dev):
    return pltpu.make_async_remote_copy(src_ref=src, dst_ref=dst, send_sem=send_sem, recv_sem=recv_sem,
                                        device_id=dev, device_id_type=MESH)


def _row_pieces(n_rows, streams=DMA_STREAMS, multiple=16):
    size = -(-n_rows // (streams * multiple)) * multiple
    return [(lo, min(size, n_rows - lo)) for lo in range(0, n_rows, size)]


def _start_streams(make, n_rows):
    for lo, size in _row_pieces(n_rows):
        make(pl.ds(lo, size)).start()


def _gather_weights(placed):
    n = len(placed)

    def body(*refs):
        dst = refs[n:2 * n]
        send_sems, recv_sems = refs[2 * n:]
        x, y, c, sibling, others = _place()
        me = 2 * x + y
        sent = []
        for i in range(n):
            half = dst[i].shape[1] // 2
            mine = c * half
            for j, (ox, oy) in enumerate(others):
                slab = dst[i].at[me, pl.ds(mine, half)]
                cp = _remote(slab, slab, send_sems.at[6 * i + j], recv_sems.at[6 * i + j], (ox, oy, c))
                cp.start()
                sent.append(cp)
        for i in range(n):
            half = dst[i].shape[1] // 2
            mine = c * half
            for j, (ox, oy) in enumerate(others):
                blk = dst[i].at[2 * ox + oy]
                slab = blk.at[pl.ds(mine, half)]
                _remote(slab, slab, send_sems.at[6 * i + j], recv_sems.at[6 * i + j], (ox, oy, c)).wait_recv()
                k = 6 * i + 3 + j
                _start_streams(lambda r, blk=blk, k=k: _remote(blk.at[pl.ds(mine + r.start, r.size)],
                                                               blk.at[pl.ds(mine + r.start, r.size)],
                                                               send_sems.at[k], recv_sems.at[k], sibling), half)
                sent.append(_remote(slab, slab, send_sems.at[k], recv_sems.at[k], sibling))
        for i in range(n):
            half = dst[i].shape[1] // 2
            for j, (ox, oy) in enumerate(others):
                slab = dst[i].at[2 * ox + oy, pl.ds((1 - c) * half, half)]
                _remote(slab, slab, send_sems.at[6 * i + 3 + j], recv_sems.at[6 * i + 3 + j], sibling).wait_recv()
        for cp in sent:
            cp.wait_send()

    return pl.pallas_call(
        body, name="gather_weights", out_shape=[jax.ShapeDtypeStruct(s.shape, s.dtype) for s in placed],
        in_specs=[ANY] * n, out_specs=[ANY] * n, input_output_aliases={i: i for i in range(n)},
        scratch_shapes=[pltpu.SemaphoreType.DMA((6 * n,)), pltpu.SemaphoreType.DMA((6 * n,))],
    )(*placed)


def _swap_halves(grads):
    n = len(grads)

    def body(*refs):
        g, theirs = refs[:n], refs[n:2 * n]
        send_sems, recv_sems = refs[2 * n:]
        x, y, c, sibling, _ = _place()
        for i in range(n):
            half = g[i].shape[1] // 2
            give = (1 - c) * half
            for b in range(N_CHIPS):
                _start_streams(lambda r, i=i, b=b: _remote(
                    g[i].at[b, pl.ds(give + r.start, r.size)], theirs[i].at[b, r], send_sems.at[i], recv_sems.at[i],
                    sibling), half)
        for i in range(n):
            _remote(theirs[i], theirs[i], send_sems.at[i], recv_sems.at[i], sibling).wait()

    return pl.pallas_call(
        body, name="swap_halves",
        out_shape=[jax.ShapeDtypeStruct((s.shape[0], s.shape[1] // 2) + s.shape[2:], s.dtype) for s in grads],
        in_specs=[ANY] * n, out_specs=[ANY] * n,
        scratch_shapes=[pltpu.SemaphoreType.DMA((n,)), pltpu.SemaphoreType.DMA((n,))],
    )(*grads)


def _scatter_to_chips(sums, placed):
    n = len(sums)

    def body(*refs):
        s, got = refs[:n], refs[2 * n:3 * n]
        send_sems, recv_sems = refs[3 * n:]
        x, y, c, _, others = _place()
        me = 2 * x + y
        sent = []
        for i in range(n):
            for j, (ox, oy) in enumerate(others):
                cp = _remote(s[i].at[2 * ox + oy], got[i].at[me], send_sems.at[3 * i + j], recv_sems.at[3 * i + j],
                             (ox, oy, c))
                cp.start()
                sent.append(cp)
        for i in range(n):
            for j, (ox, oy) in enumerate(others):
                slot = got[i].at[2 * ox + oy]
                _remote(slot, slot, send_sems.at[3 * i + j], recv_sems.at[3 * i + j], (ox, oy, c)).wait_recv()
        for cp in sent:
            cp.wait_send()

    return pl.pallas_call(
        body, name="scatter_to_chips", out_shape=[jax.ShapeDtypeStruct(s.shape, s.dtype) for s in sums],
        in_specs=[ANY] * (2 * n), out_specs=[ANY] * n, input_output_aliases={n + i: i for i in range(n)},
        scratch_shapes=[pltpu.SemaphoreType.DMA((3 * n,)), pltpu.SemaphoreType.DMA((3 * n,))],
    )(*sums, *placed)


def _join_halves(placed):
    n = len(placed)

    def body(*refs):
        full = refs[n:2 * n]
        send_sems, recv_sems = refs[2 * n:]
        x, y, c, sibling, _ = _place()
        for i in range(n):
            _start_streams(lambda r, i=i: _remote(full[i].at[c, r], full[i].at[c, r], send_sems.at[i],
                                                  recv_sems.at[i], sibling), full[i].shape[1])
        for i in range(n):
            cp = _remote(full[i].at[c], full[i].at[1 - c], send_sems.at[i], recv_sems.at[i], sibling)
            cp.wait_recv()
            cp.wait_send()

    return pl.pallas_call(
        body, name="join_halves", out_shape=[jax.ShapeDtypeStruct(s.shape, s.dtype) for s in placed],
        in_specs=[ANY] * n, out_specs=[ANY] * n, input_output_aliases={i: i for i in range(n)},
        scratch_shapes=[pltpu.SemaphoreType.DMA((n,)), pltpu.SemaphoreType.DMA((n,))],
    )(*placed)


def _gather_rows(row):
    def body(row_ref, out_ref, send_sems, recv_sems, local_sem):
        x, y, c = lax.axis_index("x"), lax.axis_index("y"), lax.axis_index("c")
        me = 4 * x + 2 * y + c
        local = pltpu.make_async_copy(row_ref, out_ref.at[me], local_sem)
        local.start()
        sent = []
        peers = []
        for k in range(1, N_DEV):
            px, py, pc = x ^ (k >> 2), y ^ ((k >> 1) & 1), c ^ (k & 1)
            peers.append((k, px, py, pc))
            cp = _remote(row_ref, out_ref.at[me], send_sems.at[k - 1], recv_sems.at[k - 1], (px, py, pc))
            cp.start()
            sent.append(cp)
        for k, px, py, pc in peers:
            slot = out_ref.at[4 * px + 2 * py + pc]
            _remote(slot, slot, send_sems.at[k - 1], recv_sems.at[k - 1], (px, py, pc)).wait_recv()
        for cp in sent:
            cp.wait_send()
        local.wait()

    return pl.pallas_call(
        body, name="gather_rows", out_shape=jax.ShapeDtypeStruct((N_DEV,) + row.shape, row.dtype),
        in_specs=[ANY], out_specs=ANY,
        scratch_shapes=[pltpu.SemaphoreType.DMA((N_DEV - 1,)), pltpu.SemaphoreType.DMA((N_DEV - 1,)),
                        pltpu.SemaphoreType.DMA],
    )(row)


def _scalar(i):
    return jnp.reshape(i, (1,)).astype(jnp.int32)


def _place_block(src, n_slots, slot, out_dtype, name, src_slot=None):
    rows, cols = src.shape[-2:]
    tr = _divisor_tile(rows, max(16, (2 * MIB) // (cols * 4)), 16)

    def body(slot_ref, s_ref, o_ref):
        o_ref[...] = s_ref[...].astype(o_ref.dtype)

    if src_slot is None:
        in_spec = pl.BlockSpec((tr, cols), lambda r, sl: (r, 0))
        slots = _scalar(slot)
    else:
        in_spec = pl.BlockSpec((None, tr, cols), lambda r, sl: (sl[1], r, 0))
        slots = jnp.concatenate([_scalar(slot), _scalar(src_slot)])
    return pl.pallas_call(
        body, name=name, out_shape=jax.ShapeDtypeStruct((n_slots, rows, cols), out_dtype),
        grid_spec=pltpu.PrefetchScalarGridSpec(
            num_scalar_prefetch=1, grid=(rows // tr,), in_specs=[in_spec],
            out_specs=pl.BlockSpec((None, tr, cols), lambda r, sl: (sl[0], r, 0))),
        compiler_params=_params(("parallel",)))(slots, src)


def _add_halves(g, theirs, core, name):
    n, half, cols = theirs.shape
    tr = _divisor_tile(half, max(16, (2 * MIB) // (cols * 4)), 16)
    per = half // tr

    def body(c_ref, a_ref, b_ref, o_ref):
        o_ref[...] = (a_ref[...].astype(F32) + b_ref[...].astype(F32)).astype(o_ref.dtype)

    spec = pl.BlockSpec((None, tr, cols), lambda i, r, c: (i, r, 0))
    return pl.pallas_call(
        body, name=name, out_shape=jax.ShapeDtypeStruct(theirs.shape, BF16),
        grid_spec=pltpu.PrefetchScalarGridSpec(
            num_scalar_prefetch=1, grid=(n, per),
            in_specs=[pl.BlockSpec((None, tr, cols), lambda i, r, c: (i, c[0] * per + r, 0)), spec], out_specs=spec),
        compiler_params=_params(("parallel", "parallel")))(_scalar(core), g, theirs)


def _sum_slots(a, name):
    n, rows, cols = a.shape
    tr = _divisor_tile(rows, max(16, (2 * MIB) // (cols * 4 * n)), 16)

    def body(a_ref, o_ref):
        acc = a_ref[0].astype(F32)
        for i in range(1, n):
            acc = acc + a_ref[i].astype(F32)
        o_ref[...] = acc

    return pl.pallas_call(body, name=name, grid=(rows // tr,),
                          in_specs=[pl.BlockSpec((n, tr, cols), lambda r: (0, r, 0))],
                          out_specs=pl.BlockSpec((tr, cols), lambda r: (r, 0)),
                          out_shape=jax.ShapeDtypeStruct((rows, cols), F32),
                          compiler_params=_params(("parallel",)))(a)


def _adamw_math(w, g, m, v):
    m = ADAM_B1 * m + (1.0 - ADAM_B1) * g
    v = ADAM_B2 * v + (1.0 - ADAM_B2) * (g * g)
    m_hat = m / (1.0 - ADAM_B1 ** ADAM_STEP)
    v_hat = v / (1.0 - ADAM_B2 ** ADAM_STEP)
    delta = -ADAM_LR * (m_hat / (jnp.sqrt(v_hat) + ADAM_EPS) + ADAM_WD * w)
    return delta, m, v


def _adamw(w, g, m, v, name):
    rows, cols = w.shape
    tr = _divisor_tile(rows, max(8, MIB // (cols * 4)), 8)

    def body(w_ref, g_ref, m_ref, v_ref, d_ref, nm_ref, nv_ref):
        d, nm, nv = _adamw_math(w_ref[...], g_ref[...], m_ref[...], v_ref[...])
        d_ref[...] = d
        nm_ref[...] = nm
        nv_ref[...] = nv

    spec = pl.BlockSpec((tr, cols), lambda r: (r, 0))
    shape = jax.ShapeDtypeStruct((rows, cols), F32)
    return pl.pallas_call(body, name=name, grid=(rows // tr,), in_specs=[spec] * 4, out_specs=[spec] * 3,
                          out_shape=[shape] * 3, compiler_params=_params(("parallel",)))(w, g, m, v)


def _sum_rows_adamw(parts, w, m, v):
    def body(p_ref, w_ref, m_ref, v_ref, g_ref, d_ref, nm_ref, nv_ref):
        g = p_ref[0]
        for i in range(1, N_DEV):
            g = g + p_ref[i]
        d, nm, nv = _adamw_math(w_ref[...], g, m_ref[...], v_ref[...])
        g_ref[...] = g
        d_ref[...] = d
        nm_ref[...] = nm
        nv_ref[...] = nv

    shape = jax.ShapeDtypeStruct(w.shape, F32)
    return pl.pallas_call(body, name="sum_rows_adamw", out_shape=[shape] * 4)(parts, w, m, v)


LANES = 128


def _permute_scratch(rows, width):
    return pltpu.VMEM((width // LANES, rows, LANES), F32)


def _split_rows(value, scratch, dil):
    if dil == 1:
        return [value]
    rows = value.shape[0] // dil
    slabs = value.shape[1] // LANES
    for c in range(slabs):
        scratch[c] = value[:, c * LANES:(c + 1) * LANES]
    return [jnp.concatenate([scratch[c, pl.ds(r, rows, stride=dil), :] for c in range(slabs)], axis=1)
            for r in range(dil)]


def _merge_rows(ref, scratch, dil):
    if dil == 1:
        return ref[0].astype(F32)
    rows = ref.shape[1]
    slabs = ref.shape[2] // LANES
    for r in range(dil):
        part = ref[r].astype(F32)
        for c in range(slabs):
            scratch[c, pl.ds(r, rows, stride=dil), :] = part[:, c * LANES:(c + 1) * LANES]
    return jnp.concatenate([scratch[c] for c in range(slabs)], axis=1)


def _grouped_view(t, dil):
    return t.reshape(dil, t.shape[0] // dil, t.shape[1])


def _grouped_spec(dil, rows, width, index):
    return pl.BlockSpec((dil, rows // dil, width), index)


def _in_proj(xb, w4, col0, ncols, tn, out_dtype, name):
    s, d = xb.shape
    per_chip = w4.shape[2] // tn
    j0 = col0 // tn
    tm = _divisor_tile(s, 512, 16)

    def body(a_ref, b_ref, o_ref):
        o_ref[...] = jnp.dot(a_ref[...], b_ref[...], preferred_element_type=F32).astype(o_ref.dtype)

    return pl.pallas_call(
        body, name=name, grid=(ncols // tn, s // tm),
        in_specs=[pl.BlockSpec((tm, d), lambda j, m: (m, 0)),
                  pl.BlockSpec((None, d, tn), lambda j, m: ((j0 + j) // per_chip, 0, (j0 + j) % per_chip))],
        out_specs=pl.BlockSpec((tm, tn), lambda j, m: (m, j)),
        out_shape=jax.ShapeDtypeStruct((s, ncols), out_dtype),
        compiler_params=_params(("parallel", "parallel")))(xb, w4)


def _in_proj_qkv(xb, w4, g, aw, tn):
    s, d = xb.shape
    dil = DILATIONS[g]
    per_chip = w4.shape[2] // tn
    sub = aw // tn
    tm = _divisor_tile(s, 512, 16 * dil)

    def body(a_ref, b_ref, o_ref, scratch):
        res = jnp.dot(a_ref[...], b_ref[...], preferred_element_type=F32)
        for r, part in enumerate(_split_rows(res, scratch, dil)):
            o_ref[r] = part.astype(BF16)

    def w_index(j, m):
        col = ((j // sub) * N_GROUPS + g) * sub + j % sub
        return col // per_chip, 0, col % per_chip

    out = pl.pallas_call(
        body, name=f"in_proj_qkv{g}", grid=(3 * sub, s // tm),
        in_specs=[pl.BlockSpec((tm, d), lambda j, m: (m, 0)), pl.BlockSpec((None, d, tn), w_index)],
        out_specs=pl.BlockSpec((None, dil, tm // dil, tn), lambda j, m: (j // sub, 0, m, j % sub)),
        out_shape=jax.ShapeDtypeStruct((3, dil, s // dil, aw), BF16),
        scratch_shapes=[_permute_scratch(tm, tn)],
        compiler_params=_params(("parallel", "parallel")))(xb, w4)
    return out.reshape(3, s, aw)


def _window_mask(first):
    qi = lax.broadcasted_iota(jnp.int32, (STEPS, 2 * STEPS), 0)
    kj = lax.broadcasted_iota(jnp.int32, (STEPS, 2 * STEPS), 1)
    lowest = jnp.where(first, STEPS, 0)
    return (kj >= qi) & (kj <= qi + STEPS) & (kj >= lowest)


def _attn_fwd(qkv, g):
    _, s, aw = qkv.shape
    heads = aw // HEAD_DIM
    n_blocks = s // STEPS
    per_seq = n_blocks // DILATIONS[g]

    def body(q_ref, kc_ref, kp_ref, vc_ref, vp_ref, o_ref, l_ref):
        mask = _window_mask(lax.rem(pl.program_id(0), per_seq) == 0)
        for h in range(heads):
            hs = slice(h * HEAD_DIM, (h + 1) * HEAD_DIM)
            kk = jnp.concatenate([kp_ref[:, hs], kc_ref[:, hs]], axis=0)
            vv = jnp.concatenate([vp_ref[:, hs], vc_ref[:, hs]], axis=0)
            sc = lax.dot_general(q_ref[:, hs], kk, NT, preferred_element_type=F32) * SCORE_SCALE
            sc = jnp.where(mask, sc, NEG_INF)
            mx = jnp.max(sc, axis=1, keepdims=True)
            e = jnp.exp(sc - mx)
            den = jnp.sum(e, axis=1, keepdims=True)
            o_ref[:, hs] = jnp.dot(e.astype(BF16), vv, preferred_element_type=F32) / den
            l_ref[:, hs] = jnp.broadcast_to(mx + jnp.log(den), (STEPS, HEAD_DIM))

    def cur(which):
        return pl.BlockSpec((None, STEPS, aw), lambda b: (which, b, 0))

    def prev(which):
        return pl.BlockSpec((None, STEPS, aw), lambda b: (which, jnp.maximum(b - 1, 0), 0))

    out = pl.BlockSpec((STEPS, aw), lambda b: (b, 0))
    shape = jax.ShapeDtypeStruct((s, aw), F32)
    return pl.pallas_call(
        body, name=f"attn_fwd{g}", grid=(n_blocks,),
        in_specs=[cur(0), cur(1), prev(1), cur(2), prev(2)], out_specs=[out, out], out_shape=[shape, shape],
        compiler_params=_params(("parallel",)))(qkv, qkv, qkv, qkv, qkv)


def _combine_groups(os, ls, zuz, aw):
    s = zuz.shape[0]
    tr = _divisor_tile(s, 256, 8 * DILATIONS[-1])

    def body(*refs):
        o_refs, l_refs, z_ref = refs[0:3], refs[3:6], refs[6]
        oo_ref, y_ref, yt_ref = refs[7:10]
        lq_refs, scratch = refs[10:13], refs[13]
        ls_ = [_merge_rows(l_refs[g], scratch, dil) for g, dil in enumerate(DILATIONS)]
        mx = jnp.maximum(jnp.maximum(ls_[0], ls_[1]), ls_[2])
        ws = [jnp.exp(l - mx) for l in ls_]
        den = ws[0] + ws[1] + ws[2]
        o = ws[0] * _merge_rows(o_refs[0], scratch, DILATIONS[0])
        for g in range(1, N_GROUPS):
            o = o + ws[g] * _merge_rows(o_refs[g], scratch, DILATIONS[g])
        o = o / den
        z = z_ref[...]
        y = o * (z * _sigmoid(z))
        oo_ref[...] = o
        y_ref[...] = y.astype(BF16)
        yt_ref[...] = y.T.astype(BF16)
        for g, dil in enumerate(DILATIONS):
            for r, part in enumerate(_split_rows(mx + jnp.log(den), scratch, dil)):
                lq_refs[g][r] = part

    grouped = [_grouped_spec(dil, tr, aw, lambda r: (0, r, 0)) for dil in DILATIONS]
    one = pl.BlockSpec((tr, aw), lambda r: (r, 0))
    f = jax.ShapeDtypeStruct((s, aw), F32)
    out = pl.pallas_call(
        body, name="combine_groups", grid=(s // tr,),
        in_specs=grouped + grouped + [one],
        out_specs=[one, one, pl.BlockSpec((aw, tr), lambda r: (0, r))] + grouped,
        out_shape=[f, jax.ShapeDtypeStruct((s, aw), BF16), jax.ShapeDtypeStruct((aw, s), BF16)]
        + [jax.ShapeDtypeStruct((dil, s // dil, aw), F32) for dil in DILATIONS],
        scratch_shapes=[_permute_scratch(tr, aw)],
        compiler_params=_params(("parallel",)))(
            *[_grouped_view(t, dil) for t, dil in zip(os, DILATIONS)],
            *[_grouped_view(t, dil) for t, dil in zip(ls, DILATIONS)], zuz)
    return out[0], out[1], out[2], [t.reshape(s, aw) for t in out[3:]]


def _pool_counts(row0, rows, window):
    t = row0 + lax.broadcasted_iota(jnp.int32, (rows, 1), 0)
    return jnp.minimum(t + 1, window).astype(F32)


def _pool_fwd(zuz, w_pool, pool_scale, aw, pw):
    s = zuz.shape[0]
    pg = pw // len(POOL_WINDOWS)
    tr = _divisor_tile(s, 256, 128)
    u_col, z_col = aw // pw, aw // pw + 1
    assert aw % pw == 0

    def body(u_ref, up_ref, z_ref, w_ref, sc_ref, p_ref, l_ref, y_ref, yt_ref):
        r = pl.program_id(0)
        u = u_ref[...]
        halo = jnp.where(r > 0, up_ref[...], 0.0)
        ext = jnp.concatenate([halo, u], axis=0)
        pieces, lins = [], []
        for gi, window in enumerate(POOL_WINDOWS):
            cs = slice(gi * pg, (gi + 1) * pg)
            acc = ext[:, cs]
            shift = 1
            while shift < window:
                acc = acc + pltpu.roll(acc, shift, 0)
                shift *= 2
            p = acc[POOL_HALO:] / _pool_counts(r * tr, tr, window) - u[:, cs]
            pieces.append(p)
            lins.append(jnp.dot(p.astype(BF16), w_ref[gi], preferred_element_type=F32))
        p = jnp.concatenate(pieces, axis=1)
        lin = jnp.concatenate(lins, axis=1)
        z = z_ref[...]
        y = lin * sc_ref[...] * (z * _sigmoid(z))
        p_ref[...] = p.astype(BF16)
        l_ref[...] = lin
        y_ref[...] = y.astype(BF16)
        yt_ref[...] = y.T.astype(BF16)

    per = tr // POOL_HALO
    out = pl.BlockSpec((tr, pw), lambda r: (r, 0))
    return pl.pallas_call(
        body, name="pool_fwd", grid=(s // tr,),
        in_specs=[pl.BlockSpec((tr, pw), lambda r: (r, u_col)),
                  pl.BlockSpec((POOL_HALO, pw), lambda r: (jnp.maximum(r * per - 1, 0), u_col)),
                  pl.BlockSpec((tr, pw), lambda r: (r, z_col)),
                  pl.BlockSpec((len(POOL_WINDOWS), pg, pg), lambda r: (0, 0, 0)),
                  pl.BlockSpec((1, pw), lambda r: (0, 0))],
        out_specs=[out, out, out, pl.BlockSpec((pw, tr), lambda r: (0, r))],
        out_shape=[jax.ShapeDtypeStruct((s, pw), BF16), jax.ShapeDtypeStruct((s, pw), F32),
                   jax.ShapeDtypeStruct((s, pw), BF16), jax.ShapeDtypeStruct((pw, s), BF16)],
        compiler_params=_params(("parallel",)))(zuz, zuz, zuz, w_pool, pool_scale)


def _proj_merge(y_attn, y_pool, wpa4, wpp4, gpre, b_gate):
    s, aw = y_attn.shape
    pw = y_pool.shape[1]
    tn = wpa4.shape[2]
    d = N_CHIPS * tn
    tm = _divisor_tile(s, 512, 128)

    def body(ya_ref, yp_ref, wa_ref, wp_ref, ga_ref, gp_ref, ba_ref, bp_ref, a_ref, p_ref, m_ref, mt_ref):
        a = jnp.dot(ya_ref[...], wa_ref[...], preferred_element_type=F32)
        p = jnp.dot(yp_ref[...], wp_ref[...], preferred_element_type=F32)
        merged = _sigmoid(ga_ref[...] + ba_ref[...]) * a + _sigmoid(gp_ref[...] + bp_ref[...]) * p
        a_ref[...] = a
        p_ref[...] = p
        m_ref[...] = merged.astype(BF16)
        mt_ref[...] = merged.T.astype(BF16)

    out = pl.BlockSpec((tm, tn), lambda n, m: (m, n))
    f = jax.ShapeDtypeStruct((s, d), F32)
    return pl.pallas_call(
        body, name="proj_merge", grid=(N_CHIPS, s // tm),
        in_specs=[pl.BlockSpec((tm, aw), lambda n, m: (m, 0)), pl.BlockSpec((tm, pw), lambda n, m: (m, 0)),
                  pl.BlockSpec((None, aw, tn), lambda n, m: (n, 0, 0)),
                  pl.BlockSpec((None, pw, tn), lambda n, m: (n, 0, 0)),
                  pl.BlockSpec((tm, tn), lambda n, m: (m, n)), pl.BlockSpec((tm, tn), lambda n, m: (m, N_CHIPS + n)),
                  pl.BlockSpec((1, tn), lambda n, m: (0, n)), pl.BlockSpec((1, tn), lambda n, m: (0, N_CHIPS + n))],
        out_specs=[out, out, out, pl.BlockSpec((tn, tm), lambda n, m: (n, m))],
        out_shape=[f, f, jax.ShapeDtypeStruct((s, d), BF16), jax.ShapeDtypeStruct((d, s), BF16)],
        compiler_params=_params(("parallel", "parallel")))(y_attn, y_pool, wpa4, wpp4, gpre, gpre, b_gate, b_gate)


def _out_norm_loss(merged, w_out, x, target, gamma, beta):
    s, d = x.shape
    tm = _divisor_tile(s, 256, 16)

    def body(m_ref, w_ref, x_ref, t_ref, g_ref, b_ref, dr_ref, drb_ref, loss_ref, dg_ref, db_ref):
        @pl.when(pl.program_id(0) == 0)
        def _():
            loss_ref[...] = jnp.zeros_like(loss_ref)
            dg_ref[...] = jnp.zeros_like(dg_ref)
            db_ref[...] = jnp.zeros_like(db_ref)

        r = ALPHA * x_ref[...] + jnp.dot(m_ref[...], w_ref[...], preferred_element_type=F32)
        mu = jnp.mean(r, axis=1, keepdims=True)
        rc = r - mu
        rstd = lax.rsqrt(jnp.mean(rc * rc, axis=1, keepdims=True) + LN_EPS)
        xhat = rc * rstd
        diff = xhat * g_ref[...] + b_ref[...] - t_ref[...]
        dy = diff / d
        loss_ref[...] += jnp.sum(diff * diff, axis=0, keepdims=True)
        dg_ref[...] += jnp.sum(dy * xhat, axis=0, keepdims=True)
        db_ref[...] += jnp.sum(dy, axis=0, keepdims=True)
        dxhat = dy * g_ref[...]
        dr = rstd * (dxhat - jnp.mean(dxhat, axis=1, keepdims=True)
                     - xhat * jnp.mean(dxhat * xhat, axis=1, keepdims=True))
        dr_ref[...] = dr
        drb_ref[...] = dr.astype(BF16)

    row = pl.BlockSpec((tm, d), lambda m: (m, 0))
    vec = pl.BlockSpec((1, d), lambda m: (0, 0))
    v = jax.ShapeDtypeStruct((1, d), F32)
    return pl.pallas_call(
        body, name="out_norm_loss", grid=(s // tm,),
        in_specs=[row, pl.BlockSpec((d, d), lambda m: (0, 0)), row, row, vec, vec],
        out_specs=[row, row, vec, vec, vec],
        out_shape=[jax.ShapeDtypeStruct((s, d), F32), jax.ShapeDtypeStruct((s, d), BF16), v, v, v],
        compiler_params=_params(("arbitrary",), vmem_mib=56))(merged, w_out, x, target, gamma, beta)


def _merge_bwd(drb, w_out, a, p, gpre, b_gate):
    s, d = drb.shape
    tm = _divisor_tile(s, 512, 16)
    tn = d // N_CHIPS

    def body(dr_ref, w_ref, a_ref, p_ref, ga_ref, gp_ref, ba_ref, bp_ref, da_ref, dp_ref, dga_ref, dgp_ref,
             dba_ref, dbp_ref):
        @pl.when(pl.program_id(1) == 0)
        def _():
            dba_ref[...] = jnp.zeros_like(dba_ref)
            dbp_ref[...] = jnp.zeros_like(dbp_ref)

        dm = lax.dot_general(dr_ref[...], w_ref[...], NT, preferred_element_type=F32)
        sa = _sigmoid(ga_ref[...] + ba_ref[...])
        sp = _sigmoid(gp_ref[...] + bp_ref[...])
        da_ref[...] = (dm * sa).astype(BF16)
        dp_ref[...] = (dm * sp).astype(BF16)
        dga = dm * a_ref[...] * sa * (1.0 - sa)
        dgp = dm * p_ref[...] * sp * (1.0 - sp)
        dga_ref[...] = dga.astype(BF16)
        dgp_ref[...] = dgp.astype(BF16)
        dba_ref[...] += jnp.sum(dga, axis=0, keepdims=True)
        dbp_ref[...] += jnp.sum(dgp, axis=0, keepdims=True)

    blk = pl.BlockSpec((tm, tn), lambda n, m: (m, n))
    blk1 = pl.BlockSpec((tm, tn), lambda n, m: (m, N_CHIPS + n))
    vec = pl.BlockSpec((1, tn), lambda n, m: (0, n))
    vec1 = pl.BlockSpec((1, tn), lambda n, m: (0, N_CHIPS + n))
    b16 = jax.ShapeDtypeStruct((s, d), BF16)
    v = jax.ShapeDtypeStruct((1, d), F32)
    return pl.pallas_call(
        body, name="merge_bwd", grid=(N_CHIPS, s // tm),
        in_specs=[pl.BlockSpec((tm, d), lambda n, m: (m, 0)), pl.BlockSpec((tn, d), lambda n, m: (n, 0)),
                  blk, blk, blk, blk1, vec, vec1],
        out_specs=[blk, blk, blk, blk, vec, vec], out_shape=[b16, b16, b16, b16, v, v],
        compiler_params=_params(("parallel", "arbitrary")))(drb, w_out, a, p, gpre, gpre, b_gate, b_gate)


def _proj_t(dy_ref, w_ref, tn):
    acc = None
    for n in range(N_CHIPS):
        t = lax.dot_general(dy_ref[:, n * tn:(n + 1) * tn], w_ref[n], NT, preferred_element_type=F32)
        acc = t if acc is None else acc + t
    return acc


def _attn_gate_bwd(da, wpa4, zuz, o):
    s, d = da.shape
    aw, tn = wpa4.shape[1], wpa4.shape[2]
    heads = aw // HEAD_DIM
    tm = _divisor_tile(s, 256, 16 * DILATIONS[-1])

    def body(*refs):
        da_ref, w_ref, z_ref, o_ref, dz_ref = refs[:5]
        do_refs, dd_refs, scratch = refs[5:8], refs[8:11], refs[11]
        dy = _proj_t(da_ref, w_ref, tn)
        z, o = z_ref[...], o_ref[...]
        sg = _sigmoid(z)
        do = dy * (z * sg)
        dz_ref[...] = (dy * o * _dsilu(z, sg)).astype(BF16)
        prod = do * o
        dd = jnp.concatenate(
            [jnp.broadcast_to(jnp.sum(prod[:, h * HEAD_DIM:(h + 1) * HEAD_DIM], axis=1, keepdims=True),
                              (tm, HEAD_DIM)) for h in range(heads)], axis=1)
        for g, dil in enumerate(DILATIONS):
            for r, part in enumerate(_split_rows(do, scratch, dil)):
                do_refs[g][r] = part.astype(BF16)
            for r, part in enumerate(_split_rows(dd, scratch, dil)):
                dd_refs[g][r] = part

    row = pl.BlockSpec((tm, aw), lambda m: (m, 0))
    grouped = [_grouped_spec(dil, tm, aw, lambda m: (0, m, 0)) for dil in DILATIONS]
    out = pl.pallas_call(
        body, name="attn_gate_bwd", grid=(s // tm,),
        in_specs=[pl.BlockSpec((tm, d), lambda m: (m, 0)), pl.BlockSpec((N_CHIPS, aw, tn), lambda m: (0, 0, 0)),
                  row, row],
        out_specs=[row] + grouped + grouped,
        out_shape=[jax.ShapeDtypeStruct((s, aw), BF16)]
        + [jax.ShapeDtypeStruct((dil, s // dil, aw), BF16) for dil in DILATIONS]
        + [jax.ShapeDtypeStruct((dil, s // dil, aw), F32) for dil in DILATIONS],
        scratch_shapes=[_permute_scratch(tm, aw)],
        compiler_params=_params(("parallel",)))(da, wpa4, zuz, o)
    return out[0], [t.reshape(s, aw) for t in out[1:4]], [t.reshape(s, aw) for t in out[4:7]]


def _pool_gate_bwd(dp_in, wpp4, zuz, lin, pooled, w_pool, pool_scale, aw):
    s, d = dp_in.shape
    pw, tn = wpp4.shape[1], wpp4.shape[2]
    n_win = len(POOL_WINDOWS)
    pg = pw // n_win
    tm = _divisor_tile(s, 256, 16)
    z_col = aw // pw + 1

    def body(dp_ref, w_ref, z_ref, l_ref, p_ref, wp_ref, sc_ref, dz_ref, dpo_ref, dw_ref, ds_ref):
        @pl.when(pl.program_id(0) == 0)
        def _():
            dw_ref[...] = jnp.zeros_like(dw_ref)
            ds_ref[...] = jnp.zeros_like(ds_ref)

        dy = _proj_t(dp_ref, w_ref, tn)
        z, lin_ = z_ref[...], l_ref[...]
        sg = _sigmoid(z)
        dypp = dy * (z * sg)
        dz_ref[...] = (dy * (lin_ * sc_ref[...]) * _dsilu(z, sg)).astype(BF16)
        ds_ref[...] += jnp.sum(dypp * lin_, axis=0, keepdims=True)
        dlin = (dypp * sc_ref[...]).astype(BF16)
        for gi in range(n_win):
            cs = slice(gi * pg, (gi + 1) * pg)
            pt = p_ref[:, cs].astype(F32).T.astype(BF16)
            dw_ref[gi] += jnp.dot(pt, dlin[:, cs], preferred_element_type=F32)
            dpo_ref[:, cs] = lax.dot_general(dlin[:, cs], wp_ref[gi], NT, preferred_element_type=F32)

    row = pl.BlockSpec((tm, pw), lambda m: (m, 0))
    return pl.pallas_call(
        body, name="pool_gate_bwd", grid=(s // tm,),
        in_specs=[pl.BlockSpec((tm, d), lambda m: (m, 0)), pl.BlockSpec((N_CHIPS, pw, tn), lambda m: (0, 0, 0)),
                  pl.BlockSpec((tm, pw), lambda m: (m, z_col)), row, row,
                  pl.BlockSpec((n_win, pg, pg), lambda m: (0, 0, 0)), pl.BlockSpec((1, pw), lambda m: (0, 0))],
        out_specs=[row, row, pl.BlockSpec((n_win, pg, pg), lambda m: (0, 0, 0)),
                   pl.BlockSpec((1, pw), lambda m: (0, 0))],
        out_shape=[jax.ShapeDtypeStruct((s, pw), BF16), jax.ShapeDtypeStruct((s, pw), F32),
                   jax.ShapeDtypeStruct((n_win, pg, pg), F32), jax.ShapeDtypeStruct((1, pw), F32)],
        compiler_params=_params(("arbitrary",)))(dp_in, wpp4, zuz, lin, pooled, w_pool, pool_scale)


def _pool_bwd(dpooled):
    s, pw = dpooled.shape
    pg = pw // len(POOL_WINDOWS)
    tr = _divisor_tile(s, 256, POOL_HALO)
    per = tr // POOL_HALO
    n_tiles = s // tr

    def body(c_ref, n_ref, du_ref):
        r = pl.program_id(0)
        cur = c_ref[...]
        halo = jnp.where(r < n_tiles - 1, n_ref[...], 0.0)
        ext = jnp.concatenate([cur, halo], axis=0)
        rows = tr + POOL_HALO
        for gi, window in enumerate(POOL_WINDOWS):
            cs = slice(gi * pg, (gi + 1) * pg)
            acc = ext[:, cs] / _pool_counts(r * tr, rows, window)
            shift = 1
            while shift < window:
                acc = acc + pltpu.roll(acc, rows - shift, 0)
                shift *= 2
            du_ref[:, cs] = (acc[:tr] - cur[:, cs]).astype(BF16)

    return pl.pallas_call(
        body, name="pool_bwd", grid=(n_tiles,),
        in_specs=[pl.BlockSpec((tr, pw), lambda r: (r, 0)),
                  pl.BlockSpec((POOL_HALO, pw), lambda r: (jnp.minimum((r + 1) * per, s // POOL_HALO - 1), 0))],
        out_specs=pl.BlockSpec((tr, pw), lambda r: (r, 0)),
        out_shape=jax.ShapeDtypeStruct((s, pw), BF16), compiler_params=_params(("parallel",)))(dpooled, dpooled)


def _attn_bwd(qkv, do, lse, dd, g):
    _, s, aw = qkv.shape
    heads = aw // HEAD_DIM
    n_blocks = s // STEPS
    per_seq = n_blocks // DILATIONS[g]

    def body(q_ref, do_ref, l_ref, dd_ref, kc_ref, kp_ref, vc_ref, vp_ref, out_ref, cq_ref, ck_ref, cv_ref):
        b = pl.program_id(0)

        @pl.when(b == 0)
        def _():
            cq_ref[...] = jnp.zeros_like(cq_ref)
            ck_ref[...] = jnp.zeros_like(ck_ref)
            cv_ref[...] = jnp.zeros_like(cv_ref)

        out_ref[0] = cq_ref[...].astype(BF16)

        @pl.when(b < n_blocks)
        def _():
            mask = _window_mask(lax.rem(b, per_seq) == 0)
            for h in range(heads):
                hs = slice(h * HEAD_DIM, (h + 1) * HEAD_DIM)
                q, do_ = q_ref[:, hs], do_ref[:, hs]
                kk = jnp.concatenate([kp_ref[:, hs], kc_ref[:, hs]], axis=0)
                vv = jnp.concatenate([vp_ref[:, hs], vc_ref[:, hs]], axis=0)
                lse_ = jnp.concatenate([l_ref[:, hs], l_ref[:, hs]], axis=1)
                dd_ = jnp.concatenate([dd_ref[:, hs], dd_ref[:, hs]], axis=1)
                sc = lax.dot_general(q, kk, NT, preferred_element_type=F32) * SCORE_SCALE
                prob = jnp.where(mask, jnp.exp(sc - lse_), 0.0)
                dprob = lax.dot_general(do_, vv, NT, preferred_element_type=F32)
                dsc = prob * (dprob - dd_) * SCORE_SCALE
                cq_ref[:, hs] = jnp.dot(dsc.astype(BF16), kk, preferred_element_type=F32)
                dkk = jnp.dot(dsc.T.astype(BF16), q, preferred_element_type=F32)
                dvv = jnp.dot(prob.T.astype(BF16), do_, preferred_element_type=F32)
                out_ref[1, :, hs] = (ck_ref[:, hs] + dkk[:STEPS]).astype(BF16)
                out_ref[2, :, hs] = (cv_ref[:, hs] + dvv[:STEPS]).astype(BF16)
                ck_ref[:, hs] = dkk[STEPS:]
                cv_ref[:, hs] = dvv[STEPS:]

        @pl.when(b == n_blocks)
        def _():
            out_ref[1] = ck_ref[...].astype(BF16)
            out_ref[2] = cv_ref[...].astype(BF16)

    last = n_blocks - 1

    def cur(which):
        return pl.BlockSpec((None, STEPS, aw), lambda b: (which, jnp.minimum(b, last), 0))

    def prev(which):
        return pl.BlockSpec((None, STEPS, aw), lambda b: (which, jnp.clip(b - 1, 0, last), 0))

    row = pl.BlockSpec((STEPS, aw), lambda b: (jnp.minimum(b, last), 0))
    return pl.pallas_call(
        body, name=f"attn_bwd{g}", grid=(n_blocks + 1,),
        in_specs=[cur(0), row, row, row, cur(1), prev(1), cur(2), prev(2)],
        out_specs=pl.BlockSpec((3, STEPS, aw), lambda b: (0, jnp.clip(b - 1, 0, last), 0)),
        out_shape=jax.ShapeDtypeStruct((3, s, aw), BF16),
        scratch_shapes=[pltpu.VMEM((STEPS, aw), F32)] * 3,
        compiler_params=_params(("arbitrary",)))(qkv, do, lse, dd, qkv, qkv, qkv, qkv)


def _weight_grad(at, b, tn, col_blocks, name):
    m, k = at.shape
    n = b.shape[1]
    tm = _divisor_tile(m, 1024, 16)
    tk = _divisor_tile(k, 2048, 128)
    nk = k // tk

    def body(a_ref, b_ref, o_ref, acc_ref):
        kk = pl.program_id(2)

        @pl.when(kk == 0)
        def _():
            acc_ref[...] = jnp.zeros_like(acc_ref)

        acc_ref[...] += jnp.dot(a_ref[...], b_ref[...], preferred_element_type=F32)

        @pl.when(kk == nk - 1)
        def _():
            o_ref[...] = acc_ref[...].astype(BF16)

    if col_blocks:
        out_spec = pl.BlockSpec((None, tm, tn), lambda i, j, kk: (j, i, 0))
        out_shape = jax.ShapeDtypeStruct((n // tn, m, tn), BF16)
    else:
        out_spec = pl.BlockSpec((tm, tn), lambda i, j, kk: (i, j))
        out_shape = jax.ShapeDtypeStruct((m, n), BF16)
    return pl.pallas_call(
        body, name=name, grid=(m // tm, n // tn, nk),
        in_specs=[pl.BlockSpec((tm, tk), lambda i, j, kk: (i, kk)), pl.BlockSpec((tk, tn), lambda i, j, kk: (kk, j))],
        out_specs=out_spec, out_shape=out_shape, scratch_shapes=[pltpu.VMEM((tm, tn), F32)],
        compiler_params=_params(("parallel", "parallel", "arbitrary")))(at, b)


def _w_in_grad_part(xt, b, col_of, n_local, tn, w_shape, prev, name):
    d, s = xt.shape
    per_chip = w_shape[2] // tn
    tm = _divisor_tile(d, 1024, 16)
    tk = _divisor_tile(s, 2048, 128)
    nk = s // tk

    def body(*refs):
        a_ref, b_ref, o_ref, acc_ref = refs[0], refs[1], refs[-2], refs[-1]
        kk = pl.program_id(2)

        @pl.when(kk == 0)
        def _():
            acc_ref[...] = jnp.zeros_like(acc_ref)

        acc_ref[...] += jnp.dot(a_ref[...], b_ref[...], preferred_element_type=F32)

        @pl.when(kk == nk - 1)
        def _():
            o_ref[...] = acc_ref[...].astype(BF16)

    if b.ndim == 3:
        sub = b.shape[2] // tn
        b_spec = pl.BlockSpec((None, tk, tn), lambda j, i, kk: (j // sub, kk, j % sub))
    else:
        b_spec = pl.BlockSpec((tk, tn), lambda j, i, kk: (kk, j))
    in_specs = [pl.BlockSpec((tm, tk), lambda j, i, kk: (i, kk)), b_spec]
    args = [xt, b]
    aliases = {}
    if prev is not None:
        in_specs.append(ANY)
        args.append(prev)
        aliases = {2: 0}
    return pl.pallas_call(
        body, name=name, grid=(n_local, d // tm, nk), in_specs=in_specs,
        out_specs=pl.BlockSpec((None, tm, tn), lambda j, i, kk: (col_of(j) // per_chip, i, col_of(j) % per_chip)),
        out_shape=jax.ShapeDtypeStruct(w_shape, BF16), scratch_shapes=[pltpu.VMEM((tm, tn), F32)],
        input_output_aliases=aliases,
        compiler_params=_params(("parallel", "parallel", "arbitrary")))(*args)


def _x_grad(dqkv, rest, w4, dr, aw, tn):
    s, d = dr.shape
    sub = aw // tn
    n_qkv = 3 * N_GROUPS * sub
    los, lo = [], n_qkv
    for p in rest:
        los.append(lo)
        lo += p.shape[1] // tn
    n_blocks = lo
    per_chip = n_blocks // N_CHIPS
    tm = _divisor_tile(s, 512, 16 * DILATIONS[-1])

    def body(*refs):
        q_refs, r_refs = refs[:N_GROUPS], refs[N_GROUPS:N_GROUPS + len(rest)]
        w_ref, dr_ref, o_ref, acc_ref, scratch = refs[-5:]
        j = pl.program_id(1)

        @pl.when(j == 0)
        def _():
            acc_ref[...] = ALPHA * dr_ref[...]

        for g, dil in enumerate(DILATIONS):
            @pl.when((j < n_qkv) & (lax.rem(j // sub, N_GROUPS) == g))
            def _(g=g, dil=dil):
                rows = _merge_rows(q_refs[g], scratch, dil).astype(BF16)
                acc_ref[...] += lax.dot_general(rows, w_ref[...], NT, preferred_element_type=F32)

        for p_ref, lo_, piece in zip(r_refs, los, rest):
            @pl.when((j >= lo_) & (j < lo_ + piece.shape[1] // tn))
            def _(p_ref=p_ref):
                acc_ref[...] += lax.dot_general(p_ref[...], w_ref[...], NT, preferred_element_type=F32)

        @pl.when(j == n_blocks - 1)
        def _():
            o_ref[...] = acc_ref[...]

    def qkv_spec(dil):
        def index(i, j):
            region = jnp.minimum(j // sub, 3 * N_GROUPS - 1)
            return region // N_GROUPS, 0, i, jnp.where(j < n_qkv, j % sub, 0)

        return pl.BlockSpec((None, dil, tm // dil, tn), index)

    def rest_spec(lo_, piece):
        n = piece.shape[1] // tn
        return pl.BlockSpec((tm, tn), lambda i, j: (i, jnp.clip(j - lo_, 0, n - 1)))

    row = pl.BlockSpec((tm, d), lambda i, j: (i, 0))
    return pl.pallas_call(
        body, name="x_grad", grid=(s // tm, n_blocks),
        in_specs=[qkv_spec(dil) for dil in DILATIONS] + [rest_spec(lo_, p) for lo_, p in zip(los, rest)]
        + [pl.BlockSpec((None, d, tn), lambda i, j: (j // per_chip, 0, j % per_chip)), row],
        out_specs=row, out_shape=jax.ShapeDtypeStruct((s, d), F32),
        scratch_shapes=[pltpu.VMEM((tm, d), F32), _permute_scratch(tm, tn)],
        compiler_params=_params(("parallel", "arbitrary"), vmem_mib=56))(
            *[t.reshape(3, dil, s // dil, aw) for t, dil in zip(dqkv, DILATIONS)], *rest, w4, dr)


def _to_subsequences(t, dilation):
    s, w = t.shape
    return t.reshape(s // dilation, dilation, w).transpose(1, 0, 2).reshape(s, w)


def _local_step(x, target, w_in4, b_gate, w_pool, pool_scale, wpa4, wpp4, w_out, gamma, beta):
    s, d = x.shape
    aw, pw = wpa4.shape[1], wpp4.shape[1]
    tn = _col_tile(aw, pw, w_in4.shape[2])
    sub = aw // tn
    qkv_w = 3 * N_GROUPS * aw

    xb = x.astype(BF16)
    qkv = [_in_proj_qkv(xb, w_in4, g, aw, tn) for g in range(N_GROUPS)]
    zuz = _in_proj(xb, w_in4, qkv_w, aw + 2 * pw, tn, F32, "in_proj_zuz")
    gpre = _in_proj(xb, w_in4, qkv_w + aw + 2 * pw, 2 * d, tn, F32, "in_proj_gates")

    attn = [_attn_fwd(qkv[g], g) for g in range(N_GROUPS)]
    o, y_attn, y_attn_t, lse = _combine_groups([a[0] for a in attn], [a[1] for a in attn], zuz, aw)
    pooled, lin, y_pool, y_pool_t = _pool_fwd(zuz, w_pool, pool_scale, aw, pw)
    a, p, merged, merged_t = _proj_merge(y_attn, y_pool, wpa4, wpp4, gpre, b_gate)
    dr, drb, loss_lanes, d_gamma, d_beta = _out_norm_loss(merged, w_out, x, target, gamma, beta)

    da, dp, d_gpre_a, d_gpre_p, d_b_a, d_b_p = _merge_bwd(drb, w_out, a, p, gpre, b_gate)
    d_b_gate = jnp.concatenate([d_b_a, d_b_p], axis=1)
    d_w_out = _weight_grad(merged_t, drb, d // N_CHIPS, False, "w_out_grad")
    d_wpa4 = _weight_grad(y_attn_t, da, d // N_CHIPS, True, "w_proj_attn_grad")
    d_wpp4 = _weight_grad(y_pool_t, dp, d // N_CHIPS, True, "w_proj_pool_grad")
    d_z_attn, d_o, dd = _attn_gate_bwd(da, wpa4, zuz, o)
    d_z_pool, d_pooled, d_w_pool, d_pool_scale = _pool_gate_bwd(dp, wpp4, zuz, lin, pooled, w_pool, pool_scale, aw)
    d_u = _pool_bwd(d_pooled)
    dqkv = [_attn_bwd(qkv[g], d_o[g], lse[g], dd[g], g) for g in range(N_GROUPS)]

    rest = [d_z_attn, d_u, d_z_pool, d_gpre_a, d_gpre_p]
    d_w_in4 = None
    for g, dil in enumerate(DILATIONS):
        xt = _to_subsequences(xb, dil).T
        d_w_in4 = _w_in_grad_part(xt, dqkv[g], lambda j, g=g: ((j // sub) * N_GROUPS + g) * sub + j % sub, 3 * sub,
                                  tn, w_in4.shape, d_w_in4, f"w_in_grad_qkv{g}")
    lo = qkv_w // tn
    for i, piece in enumerate(rest):
        n_local = piece.shape[1] // tn
        d_w_in4 = _w_in_grad_part(xb.T, piece, lambda j, lo=lo: lo + j, n_local, tn, w_in4.shape, d_w_in4,
                                  f"w_in_grad_rest{i}")
        lo += n_local
    d_x = _x_grad(dqkv, rest, w_in4, dr, aw, tn)
    return dict(d_x=d_x, loss_lanes=loss_lanes, w_in=d_w_in4, b_gate=d_b_gate, w_pool=d_w_pool,
                pool_scale=d_pool_scale, w_proj_attn=d_wpa4, w_proj_pool=d_wpp4, w_out=d_w_out,
                ln_gamma=d_gamma, ln_beta=d_beta)


def _pack_small(wpa, wpp, w_out, w_pool):
    width = wpa.shape[1]
    return jnp.concatenate([wpa, wpp, w_out.reshape(-1, width), w_pool.reshape(-1, width)], axis=0)


def _unpack_small(packed, aw, pw, d, pg):
    lead = packed.shape[:-2]
    width = d // N_CHIPS
    r0, r1, r2 = aw, aw + pw, aw + pw + d
    return (packed[..., :r0, :], packed[..., r0:r1, :], packed[..., r1:r2, :].reshape(lead + (width, d)),
            packed[..., r2:, :].reshape(lead + (len(POOL_WINDOWS), pg // N_CHIPS, pg)))


def _pack_rows(vectors, rows):
    flat = jnp.concatenate([v.reshape(-1) for v in vectors])
    return jnp.pad(flat, (0, rows * 128 - flat.shape[0])).reshape(rows, 128)


def _unpack_rows(packed, sizes):
    flat, out, lo = packed.reshape(-1), [], 0
    for n in sizes:
        out.append(flat[lo:lo + n].reshape(1, n))
        lo += n
    return out


def kernel(x, w_in, b_gate, w_pool, pool_scale, w_proj_attn, w_proj_pool, w_out, ln_gamma, ln_beta, loss_target, m_w_in, m_b_gate, m_w_pool, m_pool_scale, m_w_proj_attn, m_w_proj_pool, m_w_out, m_ln_gamma, m_ln_beta, v_w_in, v_b_gate, v_w_pool, v_pool_scale, v_w_proj_attn, v_w_proj_pool, v_w_out, v_ln_gamma, v_ln_beta):
    s, d = x.shape[1], x.shape[2]
    aw, pw = w_proj_attn.shape[1], w_proj_pool.shape[1]
    pg = w_pool.shape[3]
    n_win = len(POOL_WINDOWS)

    def small(wpa, wpp, wo, wpl):
        return _pack_small(wpa[0], wpp[0], wo[0], wpl[0])

    chip = 2 * lax.axis_index("x") + lax.axis_index("y")
    core = lax.axis_index("c")

    w_small = small(w_proj_attn, w_proj_pool, w_out, w_pool)
    w_in4, small4 = _gather_weights([_place_block(w_in[0], N_CHIPS, chip, BF16, "place_w_in"),
                                     _place_block(w_small, N_CHIPS, chip, BF16, "place_w_small")])
    wpa4, wpp4, w_out4, w_pool4 = _unpack_small(small4, aw, pw, d, pg)
    w_out_full = w_out4.reshape(d, d)
    w_pool_full = w_pool4.transpose(1, 0, 2, 3).reshape(n_win, pg, pg)

    g = _local_step(x[0], loss_target[0], w_in4, b_gate, w_pool_full, pool_scale, wpa4, wpp4, w_out_full,
                    ln_gamma, ln_beta)

    g_pool4 = g["w_pool"].reshape(n_win, N_CHIPS, pg // N_CHIPS, pg).transpose(1, 0, 2, 3).astype(BF16)
    g_out4 = g["w_out"].reshape(N_CHIPS, d // N_CHIPS, d)
    g_small4 = jnp.concatenate([g["w_proj_attn"], g["w_proj_pool"], g_out4.reshape(N_CHIPS, -1, d // N_CHIPS),
                                g_pool4.reshape(N_CHIPS, -1, d // N_CHIPS)], axis=1)
    theirs_big, theirs_small = _swap_halves([g["w_in"], g_small4])
    chip_big = _add_halves(g["w_in"], theirs_big, core, "add_cores_big")
    chip_small = _add_halves(g_small4, theirs_small, core, "add_cores_small")
    got_big, got_small = _scatter_to_chips(
        [chip_big, chip_small],
        [_place_block(chip_big, N_CHIPS, chip, BF16, "place_sum_big", src_slot=chip),
         _place_block(chip_small, N_CHIPS, chip, BF16, "place_sum_small", src_slot=chip)])
    half_big = _sum_slots(got_big, "sum_chips_big")
    half_small = _sum_slots(got_small, "sum_chips_small")
    grad_w_in, grad_small = _join_halves([_place_block(half_big, 2, core, F32, "place_half_big"),
                                          _place_block(half_small, 2, core, F32, "place_half_small")])
    grad_w_in = grad_w_in.reshape(-1, grad_w_in.shape[2])
    grad_small = grad_small.reshape(-1, grad_small.shape[2])

    sizes = [b_gate.shape[1], pool_scale.shape[1], d, d, 1]
    rows = -(-sum(sizes) // (8 * 128)) * 8
    loss_part = (0.5 / d) * jnp.sum(g["loss_lanes"]).reshape(1, 1)
    parts = _gather_rows(_pack_rows([g["b_gate"], g["pool_scale"], g["ln_gamma"], g["ln_beta"], loss_part], rows))
    zero = jnp.zeros((1, 1), F32)
    packed = [_pack_rows(vs, rows) for vs in ([b_gate, pool_scale, ln_gamma, ln_beta, zero],
                                              [m_b_gate, m_pool_scale, m_ln_gamma, m_ln_beta, zero],
                                              [v_b_gate, v_pool_scale, v_ln_gamma, v_ln_beta, zero])]
    rep = [_unpack_rows(t, sizes) for t in _sum_rows_adamw(parts, *packed)]
    loss = rep[0][4].reshape(())

    upd_in = _adamw(w_in[0], grad_w_in, m_w_in[0], v_w_in[0], "adamw_w_in")
    upd_small = _adamw(w_small, grad_small, small(m_w_proj_attn, m_w_proj_pool, m_w_out, m_w_pool),
                       small(v_w_proj_attn, v_w_proj_pool, v_w_out, v_w_pool), "adamw_small")

    def leaves(big, packed_small, replicated):
        wpa_, wpp_, wo_, wpl_ = _unpack_small(packed_small, aw, pw, d, pg)
        return [big[None], replicated[0], wpl_[None], replicated[1], wpa_[None], wpp_[None], wo_[None],
                replicated[2], replicated[3]]

    out = [loss, g["d_x"][None]]
    out += leaves(grad_w_in, grad_small, rep[0])
    for i in range(3):
        out += leaves(upd_in[i], upd_small[i], rep[1 + i])
    return tuple(out)
```

```python
import math

import jax
import jax.numpy as jnp
from jax import lax
from jax.experimental import pallas as pl
from jax.experimental.pallas import tpu as pltpu

F32 = jnp.float32
BF16 = jnp.bfloat16
MESH = pl.DeviceIdType.MESH
ANY = pl.BlockSpec(memory_space=pl.ANY)

HEAD_DIM = 128
STEPS = 128
DILATIONS = (1, 4, 16)
N_GROUPS = len(DILATIONS)
POOL_WINDOWS = (2, 4, 8, 16)
POOL_HALO = 16
N_CHIPS = 4
N_DEV = 8
ALPHA = 2.0 ** 0.25
LN_EPS = 1e-5
NEG_INF = -1e30
SCORE_SCALE = HEAD_DIM ** -0.5
ADAM_LR = 0.001
ADAM_B1 = 0.9
ADAM_B2 = 0.999
ADAM_EPS = 1e-08
ADAM_WD = 0.01
ADAM_STEP = 10
MIB = 2 ** 20
NT = (((1,), (1,)), ((), ()))
DMA_STREAMS = 8


def _params(semantics=None, vmem_mib=48):
    return pltpu.CompilerParams(dimension_semantics=semantics, vmem_limit_bytes=vmem_mib * MIB)


def _divisor_tile(n, target, multiple):
    best = None
    for t in range(multiple, min(n, target) + 1, multiple):
        if n % t == 0:
            best = t
    assert best is not None, (n, target, multiple)
    return best


def _col_tile(*widths):
    g = 0
    for w in widths:
        g = math.gcd(g, w)
    return _divisor_tile(g, 1024, 128)


def _sigmoid(z):
    return jax.nn.sigmoid(z)


def _dsilu(z, sg):
    return sg * (1.0 + z * (1.0 - sg))


def _place():
    x, y, c = lax.axis_index("x"), lax.axis_index("y"), lax.axis_index("c")
    others = [(1 - x, y), (x, 1 - y), (1 - x, 1 - y)]
    return x, y, c, (x, y, 1 - c), others


def _remote(src, dst, send_sem, recv_sem, dev):
    return pltpu.make_async_remote_copy(src_ref=src, dst_ref=dst, send_sem=send_sem, recv_sem=recv_sem,
                                        device_id=dev, device_id_type=MESH)


def _row_pieces(n_rows, streams=DMA_STREAMS, multiple=16):
    size = -(-n_rows // (streams * multiple)) * multiple
    return [(lo, min(size, n_rows - lo)) for lo in range(0, n_rows, size)]


def _start_streams(make, n_rows):
    for lo, size in _row_pieces(n_rows):
        make(pl.ds(lo, size)).start()


def _gather_weights(placed):
    n = len(placed)

    def body(*refs):
        dst = refs[n:2 * n]
        send_sems, recv_sems = refs[2 * n:]
        x, y, c, sibling, others = _place()
        me = 2 * x + y
        sent = []
        for i in range(n):
            half = dst[i].shape[1] // 2
            mine = c * half
            for j, (ox, oy) in enumerate(others):
                slab = dst[i].at[me, pl.ds(mine, half)]
                cp = _remote(slab, slab, send_sems.at[6 * i + j], recv_sems.at[6 * i + j], (ox, oy, c))
                cp.start()
                sent.append(cp)
        for i in range(n):
            half = dst[i].shape[1] // 2
            mine = c * half
            for j, (ox, oy) in enumerate(others):
                blk = dst[i].at[2 * ox + oy]
                slab = blk.at[pl.ds(mine, half)]
                _remote(slab, slab, send_sems.at[6 * i + j], recv_sems.at[6 * i + j], (ox, oy, c)).wait_recv()
                k = 6 * i + 3 + j
                _start_streams(lambda r, blk=blk, k=k: _remote(blk.at[pl.ds(mine + r.start, r.size)],
                                                               blk.at[pl.ds(mine + r.start, r.size)],
                                                               send_sems.at[k], recv_sems.at[k], sibling), half)
                sent.append(_remote(slab, slab, send_sems.at[k], recv_sems.at[k], sibling))
        for i in range(n):
            half = dst[i].shape[1] // 2
            for j, (ox, oy) in enumerate(others):
                slab = dst[i].at[2 * ox + oy, pl.ds((1 - c) * half, half)]
                _remote(slab, slab, send_sems.at[6 * i + 3 + j], recv_sems.at[6 * i + 3 + j], sibling).wait_recv()
        for cp in sent:
            cp.wait_send()

    return pl.pallas_call(
        body, name="gather_weights", out_shape=[jax.ShapeDtypeStruct(s.shape, s.dtype) for s in placed],
        in_specs=[ANY] * n, out_specs=[ANY] * n, input_output_aliases={i: i for i in range(n)},
        scratch_shapes=[pltpu.SemaphoreType.DMA((6 * n,)), pltpu.SemaphoreType.DMA((6 * n,))],
    )(*placed)


def _swap_halves(grads):
    n = len(grads)

    def body(*refs):
        g, theirs = refs[:n], refs[n:2 * n]
        send_sems, recv_sems = refs[2 * n:]
        x, y, c, sibling, _ = _place()
        for i in range(n):
            half = g[i].shape[1] // 2
            give = (1 - c) * half
            for b in range(N_CHIPS):
                _start_streams(lambda r, i=i, b=b: _remote(
                    g[i].at[b, pl.ds(give + r.start, r.size)], theirs[i].at[b, r], send_sems.at[i], recv_sems.at[i],
                    sibling), half)
        for i in range(n):
            _remote(theirs[i], theirs[i], send_sems.at[i], recv_sems.at[i], sibling).wait()

    return pl.pallas_call(
        body, name="swap_halves",
        out_shape=[jax.ShapeDtypeStruct((s.shape[0], s.shape[1] // 2) + s.shape[2:], s.dtype) for s in grads],
        in_specs=[ANY] * n, out_specs=[ANY] * n,
        scratch_shapes=[pltpu.SemaphoreType.DMA((n,)), pltpu.SemaphoreType.DMA((n,))],
    )(*grads)


HBM = pl.BlockSpec(memory_space=pltpu.HBM)
SEM = pl.BlockSpec(memory_space=pltpu.SEMAPHORE)
DATAFLOW = pltpu.SideEffectType.DATAFLOW_SIDE_EFFECTING


def _scatter_copies(s, got, send_sems, recv_sems):
    x, y, c, _, others = _place()
    me = 2 * x + y
    n = len(s)
    return [_remote(s[i].at[2 * ox + oy], got[i].at[me], send_sems[3 * i + j], recv_sems[3 * i + j], (ox, oy, c))
            for i in range(n) for j, (ox, oy) in enumerate(others)]


def _scatter_start(sums, placed):
    n = len(sums)
    k = 3 * n

    def body(*refs):
        s, got, token = refs[:n], refs[n:2 * n], refs[-1]
        send_sems, recv_sems = refs[2 * n:2 * n + k], refs[2 * n + k:2 * n + 2 * k]
        for cp in _scatter_copies(s, got, send_sems, recv_sems):
            cp.start()
        token[...] = jnp.zeros_like(token)

    hbm = [pltpu.HBM(a.shape, a.dtype) for a in list(sums) + list(placed)]
    out = pl.pallas_call(
        body, name="scatter_start",
        out_shape=[pltpu.SemaphoreType.DMA(())] * (2 * k) + hbm + [jax.ShapeDtypeStruct((8, 128), F32)],
        in_specs=[HBM] * (2 * n), out_specs=[SEM] * (2 * k) + [HBM] * (2 * n) + [pl.BlockSpec(memory_space=pltpu.VMEM)],
        input_output_aliases={i: 2 * k + i for i in range(2 * n)},
        compiler_params=pltpu.CompilerParams(has_side_effects=DATAFLOW),
    )(*[pltpu.with_memory_space_constraint(a, pltpu.HBM) for a in list(sums) + list(placed)])
    return out[:2 * k], out[2 * k:2 * k + n], out[2 * k + n:2 * k + 2 * n], out[-1]


def _scatter_wait(sems, sums, placed, after):
    n = len(sums)
    k = 3 * n

    def body(*refs):
        s, got = refs[:n], refs[n:2 * n]
        send_sems, recv_sems = refs[2 * n:2 * n + k], refs[2 * n + k:2 * n + 2 * k]
        for cp in _scatter_copies(s, got, send_sems, recv_sems):
            cp.wait_send()
            cp.wait_recv()

    hbm = [pltpu.HBM(a.shape, a.dtype) for a in list(sums) + list(placed)]
    out = pl.pallas_call(
        body, name="scatter_wait", out_shape=hbm,
        in_specs=[HBM] * (2 * n) + [SEM] * (2 * k) + [ANY], out_specs=[HBM] * (2 * n),
        input_output_aliases={i: i for i in range(2 * n)},
        compiler_params=pltpu.CompilerParams(has_side_effects=DATAFLOW),
    )(*sums, *placed, *sems, after)
    return out[n:]


def _join_halves(placed):
    n = len(placed)

    def body(*refs):
        full = refs[n:2 * n]
        send_sems, recv_sems = refs[2 * n:]
        x, y, c, sibling, _ = _place()
        for i in range(n):
            _start_streams(lambda r, i=i: _remote(full[i].at[c, r], full[i].at[c, r], send_sems.at[i],
                                                  recv_sems.at[i], sibling), full[i].shape[1])
        for i in range(n):
            cp = _remote(full[i].at[c], full[i].at[1 - c], send_sems.at[i], recv_sems.at[i], sibling)
            cp.wait_recv()
            cp.wait_send()

    return pl.pallas_call(
        body, name="join_halves", out_shape=[jax.ShapeDtypeStruct(s.shape, s.dtype) for s in placed],
        in_specs=[ANY] * n, out_specs=[ANY] * n, input_output_aliases={i: i for i in range(n)},
        scratch_shapes=[pltpu.SemaphoreType.DMA((n,)), pltpu.SemaphoreType.DMA((n,))],
    )(*placed)


def _gather_rows(row):
    def body(row_ref, out_ref, send_sems, recv_sems, local_sem):
        x, y, c = lax.axis_index("x"), lax.axis_index("y"), lax.axis_index("c")
        me = 4 * x + 2 * y + c
        local = pltpu.make_async_copy(row_ref, out_ref.at[me], local_sem)
        local.start()
        sent = []
        peers = []
        for k in range(1, N_DEV):
            px, py, pc = x ^ (k >> 2), y ^ ((k >> 1) & 1), c ^ (k & 1)
            peers.append((k, px, py, pc))
            cp = _remote(row_ref, out_ref.at[me], send_sems.at[k - 1], recv_sems.at[k - 1], (px, py, pc))
            cp.start()
            sent.append(cp)
        for k, px, py, pc in peers:
            slot = out_ref.at[4 * px + 2 * py + pc]
            _remote(slot, slot, send_sems.at[k - 1], recv_sems.at[k - 1], (px, py, pc)).wait_recv()
        for cp in sent:
            cp.wait_send()
        local.wait()

    return pl.pallas_call(
        body, name="gather_rows", out_shape=jax.ShapeDtypeStruct((N_DEV,) + row.shape, row.dtype),
        in_specs=[ANY], out_specs=ANY,
        scratch_shapes=[pltpu.SemaphoreType.DMA((N_DEV - 1,)), pltpu.SemaphoreType.DMA((N_DEV - 1,)),
                        pltpu.SemaphoreType.DMA],
    )(row)


def _scalar(i):
    return jnp.reshape(i, (1,)).astype(jnp.int32)


def _place_block(src, n_slots, slot, out_dtype, name, src_slot=None):
    rows, cols = src.shape[-2:]
    tr = _divisor_tile(rows, max(16, (2 * MIB) // (cols * 4)), 16)

    def body(slot_ref, s_ref, o_ref):
        o_ref[...] = s_ref[...].astype(o_ref.dtype)

    if src_slot is None:
        in_spec = pl.BlockSpec((tr, cols), lambda r, sl: (r, 0))
        slots = _scalar(slot)
    else:
        in_spec = pl.BlockSpec((None, tr, cols), lambda r, sl: (sl[1], r, 0))
        slots = jnp.concatenate([_scalar(slot), _scalar(src_slot)])
    return pl.pallas_call(
        body, name=name, out_shape=jax.ShapeDtypeStruct((n_slots, rows, cols), out_dtype),
        grid_spec=pltpu.PrefetchScalarGridSpec(
            num_scalar_prefetch=1, grid=(rows // tr,), in_specs=[in_spec],
            out_specs=pl.BlockSpec((None, tr, cols), lambda r, sl: (sl[0], r, 0))),
        compiler_params=_params(("parallel",)))(slots, src)


def _add_halves(g, theirs, core, name):
    n, half, cols = theirs.shape
    tr = _divisor_tile(half, max(16, (2 * MIB) // (cols * 4)), 16)
    per = half // tr

    def body(c_ref, a_ref, b_ref, o_ref):
        o_ref[...] = (a_ref[...].astype(F32) + b_ref[...].astype(F32)).astype(o_ref.dtype)

    spec = pl.BlockSpec((None, tr, cols), lambda i, r, c: (i, r, 0))
    return pl.pallas_call(
        body, name=name, out_shape=jax.ShapeDtypeStruct(theirs.shape, BF16),
        grid_spec=pltpu.PrefetchScalarGridSpec(
            num_scalar_prefetch=1, grid=(n, per),
            in_specs=[pl.BlockSpec((None, tr, cols), lambda i, r, c: (i, c[0] * per + r, 0)), spec], out_specs=spec),
        compiler_params=_params(("parallel", "parallel")))(_scalar(core), g, theirs)


def _sum_slots(a, name):
    n, rows, cols = a.shape
    tr = _divisor_tile(rows, max(16, (2 * MIB) // (cols * 4 * n)), 16)

    def body(a_ref, o_ref):
        acc = a_ref[0].astype(F32)
        for i in range(1, n):
            acc = acc + a_ref[i].astype(F32)
        o_ref[...] = acc

    return pl.pallas_call(body, name=name, grid=(rows // tr,),
                          in_specs=[pl.BlockSpec((n, tr, cols), lambda r: (0, r, 0))],
                          out_specs=pl.BlockSpec((tr, cols), lambda r: (r, 0)),
                          out_shape=jax.ShapeDtypeStruct((rows, cols), F32),
                          compiler_params=_params(("parallel",)))(a)


def _adamw_math(w, g, m, v):
    m = ADAM_B1 * m + (1.0 - ADAM_B1) * g
    v = ADAM_B2 * v + (1.0 - ADAM_B2) * (g * g)
    m_hat = m / (1.0 - ADAM_B1 ** ADAM_STEP)
    v_hat = v / (1.0 - ADAM_B2 ** ADAM_STEP)
    delta = -ADAM_LR * (m_hat / (jnp.sqrt(v_hat) + ADAM_EPS) + ADAM_WD * w)
    return delta, m, v


def _adamw(w, g, m, v, name):
    rows, cols = w.shape
    tr = _divisor_tile(rows, max(8, MIB // (cols * 4)), 8)

    def body(w_ref, g_ref, m_ref, v_ref, d_ref, nm_ref, nv_ref):
        d, nm, nv = _adamw_math(w_ref[...], g_ref[...], m_ref[...], v_ref[...])
        d_ref[...] = d
        nm_ref[...] = nm
        nv_ref[...] = nv

    spec = pl.BlockSpec((tr, cols), lambda r: (r, 0))
    shape = jax.ShapeDtypeStruct((rows, cols), F32)
    return pl.pallas_call(body, name=name, grid=(rows // tr,), in_specs=[spec] * 4, out_specs=[spec] * 3,
                          out_shape=[shape] * 3, compiler_params=_params(("parallel",)))(w, g, m, v)


def _sum_rows_adamw(parts, w, m, v):
    def body(p_ref, w_ref, m_ref, v_ref, g_ref, d_ref, nm_ref, nv_ref):
        g = p_ref[0]
        for i in range(1, N_DEV):
            g = g + p_ref[i]
        d, nm, nv = _adamw_math(w_ref[...], g, m_ref[...], v_ref[...])
        g_ref[...] = g
        d_ref[...] = d
        nm_ref[...] = nm
        nv_ref[...] = nv

    shape = jax.ShapeDtypeStruct(w.shape, F32)
    return pl.pallas_call(body, name="sum_rows_adamw", out_shape=[shape] * 4)(parts, w, m, v)


LANES = 128


def _permute_scratch(rows, width):
    return pltpu.VMEM((width // LANES, rows, LANES), F32)


def _split_rows(value, scratch, dil):
    if dil == 1:
        return [value]
    rows = value.shape[0] // dil
    slabs = value.shape[1] // LANES
    for c in range(slabs):
        scratch[c] = value[:, c * LANES:(c + 1) * LANES]
    return [jnp.concatenate([scratch[c, pl.ds(r, rows, stride=dil), :] for c in range(slabs)], axis=1)
            for r in range(dil)]


def _merge_rows(ref, scratch, dil):
    if dil == 1:
        return ref[0].astype(F32)
    rows = ref.shape[1]
    slabs = ref.shape[2] // LANES
    for r in range(dil):
        part = ref[r].astype(F32)
        for c in range(slabs):
            scratch[c, pl.ds(r, rows, stride=dil), :] = part[:, c * LANES:(c + 1) * LANES]
    return jnp.concatenate([scratch[c] for c in range(slabs)], axis=1)


def _grouped_view(t, dil):
    return t.reshape(dil, t.shape[0] // dil, t.shape[1])


def _grouped_spec(dil, rows, width, index):
    return pl.BlockSpec((dil, rows // dil, width), index)


def _in_proj(xb, w4, col0, ncols, tn, out_dtype, name):
    s, d = xb.shape
    per_chip = w4.shape[2] // tn
    j0 = col0 // tn
    tm = _divisor_tile(s, 512, 16)

    def body(a_ref, b_ref, o_ref):
        o_ref[...] = jnp.dot(a_ref[...], b_ref[...], preferred_element_type=F32).astype(o_ref.dtype)

    return pl.pallas_call(
        body, name=name, grid=(ncols // tn, s // tm),
        in_specs=[pl.BlockSpec((tm, d), lambda j, m: (m, 0)),
                  pl.BlockSpec((None, d, tn), lambda j, m: ((j0 + j) // per_chip, 0, (j0 + j) % per_chip))],
        out_specs=pl.BlockSpec((tm, tn), lambda j, m: (m, j)),
        out_shape=jax.ShapeDtypeStruct((s, ncols), out_dtype),
        compiler_params=_params(("parallel", "parallel")))(xb, w4)


def _in_proj_qkv(xb, w4, g, aw, tn):
    s, d = xb.shape
    dil = DILATIONS[g]
    per_chip = w4.shape[2] // tn
    sub = aw // tn
    tm = _divisor_tile(s, 512, 16 * dil)

    def body(a_ref, b_ref, o_ref, scratch):
        res = jnp.dot(a_ref[...], b_ref[...], preferred_element_type=F32)
        for r, part in enumerate(_split_rows(res, scratch, dil)):
            o_ref[r] = part.astype(BF16)

    def w_index(j, m):
        col = ((j // sub) * N_GROUPS + g) * sub + j % sub
        return col // per_chip, 0, col % per_chip

    out = pl.pallas_call(
        body, name=f"in_proj_qkv{g}", grid=(3 * sub, s // tm),
        in_specs=[pl.BlockSpec((tm, d), lambda j, m: (m, 0)), pl.BlockSpec((None, d, tn), w_index)],
        out_specs=pl.BlockSpec((None, dil, tm // dil, tn), lambda j, m: (j // sub, 0, m, j % sub)),
        out_shape=jax.ShapeDtypeStruct((3, dil, s // dil, aw), BF16),
        scratch_shapes=[_permute_scratch(tm, tn)],
        compiler_params=_params(("parallel", "parallel")))(xb, w4)
    return out.reshape(3, s, aw)


def _window_mask(first):
    qi = lax.broadcasted_iota(jnp.int32, (STEPS, 2 * STEPS), 0)
    kj = lax.broadcasted_iota(jnp.int32, (STEPS, 2 * STEPS), 1)
    lowest = jnp.where(first, STEPS, 0)
    return (kj >= qi) & (kj <= qi + STEPS) & (kj >= lowest)


def _attn_fwd(qkv, g):
    _, s, aw = qkv.shape
    heads = aw // HEAD_DIM
    n_blocks = s // STEPS
    per_seq = n_blocks // DILATIONS[g]

    def body(q_ref, kc_ref, kp_ref, vc_ref, vp_ref, o_ref, l_ref):
        mask = _window_mask(lax.rem(pl.program_id(0), per_seq) == 0)
        for h in range(heads):
            hs = slice(h * HEAD_DIM, (h + 1) * HEAD_DIM)
            kk = jnp.concatenate([kp_ref[:, hs], kc_ref[:, hs]], axis=0)
            vv = jnp.concatenate([vp_ref[:, hs], vc_ref[:, hs]], axis=0)
            sc = lax.dot_general(q_ref[:, hs], kk, NT, preferred_element_type=F32) * SCORE_SCALE
            sc = jnp.where(mask, sc, NEG_INF)
            mx = jnp.max(sc, axis=1, keepdims=True)
            e = jnp.exp(sc - mx)
            den = jnp.sum(e, axis=1, keepdims=True)
            o_ref[:, hs] = jnp.dot(e.astype(BF16), vv, preferred_element_type=F32) / den
            l_ref[:, hs] = jnp.broadcast_to(mx + jnp.log(den), (STEPS, HEAD_DIM))

    def cur(which):
        return pl.BlockSpec((None, STEPS, aw), lambda b: (which, b, 0))

    def prev(which):
        return pl.BlockSpec((None, STEPS, aw), lambda b: (which, jnp.maximum(b - 1, 0), 0))

    out = pl.BlockSpec((STEPS, aw), lambda b: (b, 0))
    shape = jax.ShapeDtypeStruct((s, aw), F32)
    return pl.pallas_call(
        body, name=f"attn_fwd{g}", grid=(n_blocks,),
        in_specs=[cur(0), cur(1), prev(1), cur(2), prev(2)], out_specs=[out, out], out_shape=[shape, shape],
        compiler_params=_params(("parallel",)))(qkv, qkv, qkv, qkv, qkv)


def _combine_groups(os, ls, zuz, aw):
    s = zuz.shape[0]
    tr = _divisor_tile(s, 256, 8 * DILATIONS[-1])

    def body(*refs):
        o_refs, l_refs, z_ref = refs[0:3], refs[3:6], refs[6]
        oo_ref, y_ref, yt_ref = refs[7:10]
        lq_refs, scratch = refs[10:13], refs[13]
        ls_ = [_merge_rows(l_refs[g], scratch, dil) for g, dil in enumerate(DILATIONS)]
        mx = jnp.maximum(jnp.maximum(ls_[0], ls_[1]), ls_[2])
        ws = [jnp.exp(l - mx) for l in ls_]
        den = ws[0] + ws[1] + ws[2]
        o = ws[0] * _merge_rows(o_refs[0], scratch, DILATIONS[0])
        for g in range(1, N_GROUPS):
            o = o + ws[g] * _merge_rows(o_refs[g], scratch, DILATIONS[g])
        o = o / den
        z = z_ref[...]
        y = o * (z * _sigmoid(z))
        oo_ref[...] = o
        y_ref[...] = y.astype(BF16)
        yt_ref[...] = y.T.astype(BF16)
        for g, dil in enumerate(DILATIONS):
            for r, part in enumerate(_split_rows(mx + jnp.log(den), scratch, dil)):
                lq_refs[g][r] = part

    grouped = [_grouped_spec(dil, tr, aw, lambda r: (0, r, 0)) for dil in DILATIONS]
    one = pl.BlockSpec((tr, aw), lambda r: (r, 0))
    f = jax.ShapeDtypeStruct((s, aw), F32)
    out = pl.pallas_call(
        body, name="combine_groups", grid=(s // tr,),
        in_specs=grouped + grouped + [one],
        out_specs=[one, one, pl.BlockSpec((aw, tr), lambda r: (0, r))] + grouped,
        out_shape=[f, jax.ShapeDtypeStruct((s, aw), BF16), jax.ShapeDtypeStruct((aw, s), BF16)]
        + [jax.ShapeDtypeStruct((dil, s // dil, aw), F32) for dil in DILATIONS],
        scratch_shapes=[_permute_scratch(tr, aw)],
        compiler_params=_params(("parallel",)))(
            *[_grouped_view(t, dil) for t, dil in zip(os, DILATIONS)],
            *[_grouped_view(t, dil) for t, dil in zip(ls, DILATIONS)], zuz)
    return out[0], out[1], out[2], [t.reshape(s, aw) for t in out[3:]]


def _pool_counts(row0, rows, window):
    t = row0 + lax.broadcasted_iota(jnp.int32, (rows, 1), 0)
    return jnp.minimum(t + 1, window).astype(F32)


def _pool_fwd(zuz, w_pool, pool_scale, aw, pw):
    s = zuz.shape[0]
    pg = pw // len(POOL_WINDOWS)
    tr = _divisor_tile(s, 256, 128)
    u_col, z_col = aw // pw, aw // pw + 1
    assert aw % pw == 0

    def body(u_ref, up_ref, z_ref, w_ref, sc_ref, p_ref, l_ref, y_ref, yt_ref):
        r = pl.program_id(0)
        u = u_ref[...]
        halo = jnp.where(r > 0, up_ref[...], 0.0)
        ext = jnp.concatenate([halo, u], axis=0)
        pieces, lins = [], []
        for gi, window in enumerate(POOL_WINDOWS):
            cs = slice(gi * pg, (gi + 1) * pg)
            acc = ext[:, cs]
            shift = 1
            while shift < window:
                acc = acc + pltpu.roll(acc, shift, 0)
                shift *= 2
            p = acc[POOL_HALO:] / _pool_counts(r * tr, tr, window) - u[:, cs]
            pieces.append(p)
            lins.append(jnp.dot(p.astype(BF16), w_ref[gi], preferred_element_type=F32))
        p = jnp.concatenate(pieces, axis=1)
        lin = jnp.concatenate(lins, axis=1)
        z = z_ref[...]
        y = lin * sc_ref[...] * (z * _sigmoid(z))
        p_ref[...] = p.astype(BF16)
        l_ref[...] = lin
        y_ref[...] = y.astype(BF16)
        yt_ref[...] = y.T.astype(BF16)

    per = tr // POOL_HALO
    out = pl.BlockSpec((tr, pw), lambda r: (r, 0))
    return pl.pallas_call(
        body, name="pool_fwd", grid=(s // tr,),
        in_specs=[pl.BlockSpec((tr, pw), lambda r: (r, u_col)),
                  pl.BlockSpec((POOL_HALO, pw), lambda r: (jnp.maximum(r * per - 1, 0), u_col)),
                  pl.BlockSpec((tr, pw), lambda r: (r, z_col)),
                  pl.BlockSpec((len(POOL_WINDOWS), pg, pg), lambda r: (0, 0, 0)),
                  pl.BlockSpec((1, pw), lambda r: (0, 0))],
        out_specs=[out, out, out, pl.BlockSpec((pw, tr), lambda r: (0, r))],
        out_shape=[jax.ShapeDtypeStruct((s, pw), BF16), jax.ShapeDtypeStruct((s, pw), F32),
                   jax.ShapeDtypeStruct((s, pw), BF16), jax.ShapeDtypeStruct((pw, s), BF16)],
        compiler_params=_params(("parallel",)))(zuz, zuz, zuz, w_pool, pool_scale)


def _proj_merge(y_attn, y_pool, wpa4, wpp4, gpre, b_gate):
    s, aw = y_attn.shape
    pw = y_pool.shape[1]
    tn = wpa4.shape[2]
    d = N_CHIPS * tn
    tm = _divisor_tile(s, 512, 128)

    def body(ya_ref, yp_ref, wa_ref, wp_ref, ga_ref, gp_ref, ba_ref, bp_ref, a_ref, p_ref, m_ref, mt_ref):
        a = jnp.dot(ya_ref[...], wa_ref[...], preferred_element_type=F32)
        p = jnp.dot(yp_ref[...], wp_ref[...], preferred_element_type=F32)
        merged = _sigmoid(ga_ref[...] + ba_ref[...]) * a + _sigmoid(gp_ref[...] + bp_ref[...]) * p
        a_ref[...] = a
        p_ref[...] = p
        m_ref[...] = merged.astype(BF16)
        mt_ref[...] = merged.T.astype(BF16)

    out = pl.BlockSpec((tm, tn), lambda n, m: (m, n))
    f = jax.ShapeDtypeStruct((s, d), F32)
    return pl.pallas_call(
        body, name="proj_merge", grid=(N_CHIPS, s // tm),
        in_specs=[pl.BlockSpec((tm, aw), lambda n, m: (m, 0)), pl.BlockSpec((tm, pw), lambda n, m: (m, 0)),
                  pl.BlockSpec((None, aw, tn), lambda n, m: (n, 0, 0)),
                  pl.BlockSpec((None, pw, tn), lambda n, m: (n, 0, 0)),
                  pl.BlockSpec((tm, tn), lambda n, m: (m, n)), pl.BlockSpec((tm, tn), lambda n, m: (m, N_CHIPS + n)),
                  pl.BlockSpec((1, tn), lambda n, m: (0, n)), pl.BlockSpec((1, tn), lambda n, m: (0, N_CHIPS + n))],
        out_specs=[out, out, out, pl.BlockSpec((tn, tm), lambda n, m: (n, m))],
        out_shape=[f, f, jax.ShapeDtypeStruct((s, d), BF16), jax.ShapeDtypeStruct((d, s), BF16)],
        compiler_params=_params(("parallel", "parallel")))(y_attn, y_pool, wpa4, wpp4, gpre, gpre, b_gate, b_gate)


def _out_norm_loss(merged, w_out, x, target, gamma, beta):
    s, d = x.shape
    tm = _divisor_tile(s, 256, 16)

    def body(m_ref, w_ref, x_ref, t_ref, g_ref, b_ref, dr_ref, drb_ref, loss_ref, dg_ref, db_ref):
        @pl.when(pl.program_id(0) == 0)
        def _():
            loss_ref[...] = jnp.zeros_like(loss_ref)
            dg_ref[...] = jnp.zeros_like(dg_ref)
            db_ref[...] = jnp.zeros_like(db_ref)

        r = ALPHA * x_ref[...] + jnp.dot(m_ref[...], w_ref[...], preferred_element_type=F32)
        mu = jnp.mean(r, axis=1, keepdims=True)
        rc = r - mu
        rstd = lax.rsqrt(jnp.mean(rc * rc, axis=1, keepdims=True) + LN_EPS)
        xhat = rc * rstd
        diff = xhat * g_ref[...] + b_ref[...] - t_ref[...]
        dy = diff / d
        loss_ref[...] += jnp.sum(diff * diff, axis=0, keepdims=True)
        dg_ref[...] += jnp.sum(dy * xhat, axis=0, keepdims=True)
        db_ref[...] += jnp.sum(dy, axis=0, keepdims=True)
        dxhat = dy * g_ref[...]
        dr = rstd * (dxhat - jnp.mean(dxhat, axis=1, keepdims=True)
                     - xhat * jnp.mean(dxhat * xhat, axis=1, keepdims=True))
        dr_ref[...] = dr
        drb_ref[...] = dr.astype(BF16)

    row = pl.BlockSpec((tm, d), lambda m: (m, 0))
    vec = pl.BlockSpec((1, d), lambda m: (0, 0))
    v = jax.ShapeDtypeStruct((1, d), F32)
    return pl.pallas_call(
        body, name="out_norm_loss", grid=(s // tm,),
        in_specs=[row, pl.BlockSpec((d, d), lambda m: (0, 0)), row, row, vec, vec],
        out_specs=[row, row, vec, vec, vec],
        out_shape=[jax.ShapeDtypeStruct((s, d), F32), jax.ShapeDtypeStruct((s, d), BF16), v, v, v],
        compiler_params=_params(("arbitrary",), vmem_mib=56))(merged, w_out, x, target, gamma, beta)


def _merge_bwd(drb, w_out, a, p, gpre, b_gate):
    s, d = drb.shape
    tm = _divisor_tile(s, 512, 16)
    tn = d // N_CHIPS

    def body(dr_ref, w_ref, a_ref, p_ref, ga_ref, gp_ref, ba_ref, bp_ref, da_ref, dp_ref, dga_ref, dgp_ref,
             dba_ref, dbp_ref):
        @pl.when(pl.program_id(1) == 0)
        def _():
            dba_ref[...] = jnp.zeros_like(dba_ref)
            dbp_ref[...] = jnp.zeros_like(dbp_ref)

        dm = lax.dot_general(dr_ref[...], w_ref[...], NT, preferred_element_type=F32)
        sa = _sigmoid(ga_ref[...] + ba_ref[...])
        sp = _sigmoid(gp_ref[...] + bp_ref[...])
        da_ref[...] = (dm * sa).astype(BF16)
        dp_ref[...] = (dm * sp).astype(BF16)
        dga = dm * a_ref[...] * sa * (1.0 - sa)
        dgp = dm * p_ref[...] * sp * (1.0 - sp)
        dga_ref[...] = dga.astype(BF16)
        dgp_ref[...] = dgp.astype(BF16)
        dba_ref[...] += jnp.sum(dga, axis=0, keepdims=True)
        dbp_ref[...] += jnp.sum(dgp, axis=0, keepdims=True)

    blk = pl.BlockSpec((tm, tn), lambda n, m: (m, n))
    blk1 = pl.BlockSpec((tm, tn), lambda n, m: (m, N_CHIPS + n))
    vec = pl.BlockSpec((1, tn), lambda n, m: (0, n))
    vec1 = pl.BlockSpec((1, tn), lambda n, m: (0, N_CHIPS + n))
    b16 = jax.ShapeDtypeStruct((s, d), BF16)
    v = jax.ShapeDtypeStruct((1, d), F32)
    return pl.pallas_call(
        body, name="merge_bwd", grid=(N_CHIPS, s // tm),
        in_specs=[pl.BlockSpec((tm, d), lambda n, m: (m, 0)), pl.BlockSpec((tn, d), lambda n, m: (n, 0)),
                  blk, blk, blk, blk1, vec, vec1],
        out_specs=[blk, blk, blk, blk, vec, vec], out_shape=[b16, b16, b16, b16, v, v],
        compiler_params=_params(("parallel", "arbitrary")))(drb, w_out, a, p, gpre, gpre, b_gate, b_gate)


def _proj_t(dy_ref, w_ref, tn):
    acc = None
    for n in range(N_CHIPS):
        t = lax.dot_general(dy_ref[:, n * tn:(n + 1) * tn], w_ref[n], NT, preferred_element_type=F32)
        acc = t if acc is None else acc + t
    return acc


def _attn_gate_bwd(da, wpa4, zuz, o):
    s, d = da.shape
    aw, tn = wpa4.shape[1], wpa4.shape[2]
    heads = aw // HEAD_DIM
    tm = _divisor_tile(s, 256, 16 * DILATIONS[-1])

    def body(*refs):
        da_ref, w_ref, z_ref, o_ref, dz_ref = refs[:5]
        do_refs, dd_refs, scratch = refs[5:8], refs[8:11], refs[11]
        dy = _proj_t(da_ref, w_ref, tn)
        z, o = z_ref[...], o_ref[...]
        sg = _sigmoid(z)
        do = dy * (z * sg)
        dz_ref[...] = (dy * o * _dsilu(z, sg)).astype(BF16)
        prod = do * o
        dd = jnp.concatenate(
            [jnp.broadcast_to(jnp.sum(prod[:, h * HEAD_DIM:(h + 1) * HEAD_DIM], axis=1, keepdims=True),
                              (tm, HEAD_DIM)) for h in range(heads)], axis=1)
        for g, dil in enumerate(DILATIONS):
            for r, part in enumerate(_split_rows(do, scratch, dil)):
                do_refs[g][r] = part.astype(BF16)
            for r, part in enumerate(_split_rows(dd, scratch, dil)):
                dd_refs[g][r] = part

    row = pl.BlockSpec((tm, aw), lambda m: (m, 0))
    grouped = [_grouped_spec(dil, tm, aw, lambda m: (0, m, 0)) for dil in DILATIONS]
    out = pl.pallas_call(
        body, name="attn_gate_bwd", grid=(s // tm,),
        in_specs=[pl.BlockSpec((tm, d), lambda m: (m, 0)), pl.BlockSpec((N_CHIPS, aw, tn), lambda m: (0, 0, 0)),
                  row, row],
        out_specs=[row] + grouped + grouped,
        out_shape=[jax.ShapeDtypeStruct((s, aw), BF16)]
        + [jax.ShapeDtypeStruct((dil, s // dil, aw), BF16) for dil in DILATIONS]
        + [jax.ShapeDtypeStruct((dil, s // dil, aw), F32) for dil in DILATIONS],
        scratch_shapes=[_permute_scratch(tm, aw)],
        compiler_params=_params(("parallel",)))(da, wpa4, zuz, o)
    return out[0], [t.reshape(s, aw) for t in out[1:4]], [t.reshape(s, aw) for t in out[4:7]]


def _pool_gate_bwd(dp_in, wpp4, zuz, lin, pooled, w_pool, pool_scale, aw):
    s, d = dp_in.shape
    pw, tn = wpp4.shape[1], wpp4.shape[2]
    n_win = len(POOL_WINDOWS)
    pg = pw // n_win
    tm = _divisor_tile(s, 256, 16)
    z_col = aw // pw + 1

    def body(dp_ref, w_ref, z_ref, l_ref, p_ref, wp_ref, sc_ref, dz_ref, dpo_ref, dw_ref, ds_ref):
        @pl.when(pl.program_id(0) == 0)
        def _():
            dw_ref[...] = jnp.zeros_like(dw_ref)
            ds_ref[...] = jnp.zeros_like(ds_ref)

        dy = _proj_t(dp_ref, w_ref, tn)
        z, lin_ = z_ref[...], l_ref[...]
        sg = _sigmoid(z)
        dypp = dy * (z * sg)
        dz_ref[...] = (dy * (lin_ * sc_ref[...]) * _dsilu(z, sg)).astype(BF16)
        ds_ref[...] += jnp.sum(dypp * lin_, axis=0, keepdims=True)
        dlin = (dypp * sc_ref[...]).astype(BF16)
        for gi in range(n_win):
            cs = slice(gi * pg, (gi + 1) * pg)
            pt = p_ref[:, cs].astype(F32).T.astype(BF16)
            dw_ref[gi] += jnp.dot(pt, dlin[:, cs], preferred_element_type=F32)
            dpo_ref[:, cs] = lax.dot_general(dlin[:, cs], wp_ref[gi], NT, preferred_element_type=F32)

    row = pl.BlockSpec((tm, pw), lambda m: (m, 0))
    return pl.pallas_call(
        body, name="pool_gate_bwd", grid=(s // tm,),
        in_specs=[pl.BlockSpec((tm, d), lambda m: (m, 0)), pl.BlockSpec((N_CHIPS, pw, tn), lambda m: (0, 0, 0)),
                  pl.BlockSpec((tm, pw), lambda m: (m, z_col)), row, row,
                  pl.BlockSpec((n_win, pg, pg), lambda m: (0, 0, 0)), pl.BlockSpec((1, pw), lambda m: (0, 0))],
        out_specs=[row, row, pl.BlockSpec((n_win, pg, pg), lambda m: (0, 0, 0)),
                   pl.BlockSpec((1, pw), lambda m: (0, 0))],
        out_shape=[jax.ShapeDtypeStruct((s, pw), BF16), jax.ShapeDtypeStruct((s, pw), F32),
                   jax.ShapeDtypeStruct((n_win, pg, pg), F32), jax.ShapeDtypeStruct((1, pw), F32)],
        compiler_params=_params(("arbitrary",)))(dp_in, wpp4, zuz, lin, pooled, w_pool, pool_scale)


def _pool_bwd(dpooled):
    s, pw = dpooled.shape
    pg = pw // len(POOL_WINDOWS)
    tr = _divisor_tile(s, 256, POOL_HALO)
    per = tr // POOL_HALO
    n_tiles = s // tr

    def body(c_ref, n_ref, du_ref):
        r = pl.program_id(0)
        cur = c_ref[...]
        halo = jnp.where(r < n_tiles - 1, n_ref[...], 0.0)
        ext = jnp.concatenate([cur, halo], axis=0)
        rows = tr + POOL_HALO
        for gi, window in enumerate(POOL_WINDOWS):
            cs = slice(gi * pg, (gi + 1) * pg)
            acc = ext[:, cs] / _pool_counts(r * tr, rows, window)
            shift = 1
            while shift < window:
                acc = acc + pltpu.roll(acc, rows - shift, 0)
                shift *= 2
            du_ref[:, cs] = (acc[:tr] - cur[:, cs]).astype(BF16)

    return pl.pallas_call(
        body, name="pool_bwd", grid=(n_tiles,),
        in_specs=[pl.BlockSpec((tr, pw), lambda r: (r, 0)),
                  pl.BlockSpec((POOL_HALO, pw), lambda r: (jnp.minimum((r + 1) * per, s // POOL_HALO - 1), 0))],
        out_specs=pl.BlockSpec((tr, pw), lambda r: (r, 0)),
        out_shape=jax.ShapeDtypeStruct((s, pw), BF16), compiler_params=_params(("parallel",)))(dpooled, dpooled)


def _attn_bwd(qkv, do, lse, dd, g):
    _, s, aw = qkv.shape
    heads = aw // HEAD_DIM
    n_blocks = s // STEPS
    per_seq = n_blocks // DILATIONS[g]

    def body(q_ref, do_ref, l_ref, dd_ref, kc_ref, kp_ref, vc_ref, vp_ref, out_ref, cq_ref, ck_ref, cv_ref):
        b = pl.program_id(0)

        @pl.when(b == 0)
        def _():
            cq_ref[...] = jnp.zeros_like(cq_ref)
            ck_ref[...] = jnp.zeros_like(ck_ref)
            cv_ref[...] = jnp.zeros_like(cv_ref)

        out_ref[0] = cq_ref[...].astype(BF16)

        @pl.when(b < n_blocks)
        def _():
            mask = _window_mask(lax.rem(b, per_seq) == 0)
            for h in range(heads):
                hs = slice(h * HEAD_DIM, (h + 1) * HEAD_DIM)
                q, do_ = q_ref[:, hs], do_ref[:, hs]
                kk = jnp.concatenate([kp_ref[:, hs], kc_ref[:, hs]], axis=0)
                vv = jnp.concatenate([vp_ref[:, hs], vc_ref[:, hs]], axis=0)
                lse_ = jnp.concatenate([l_ref[:, hs], l_ref[:, hs]], axis=1)
                dd_ = jnp.concatenate([dd_ref[:, hs], dd_ref[:, hs]], axis=1)
                sc = lax.dot_general(q, kk, NT, preferred_element_type=F32) * SCORE_SCALE
                prob = jnp.where(mask, jnp.exp(sc - lse_), 0.0)
                dprob = lax.dot_general(do_, vv, NT, preferred_element_type=F32)
                dsc = prob * (dprob - dd_) * SCORE_SCALE
                cq_ref[:, hs] = jnp.dot(dsc.astype(BF16), kk, preferred_element_type=F32)
                dkk = jnp.dot(dsc.T.astype(BF16), q, preferred_element_type=F32)
                dvv = jnp.dot(prob.T.astype(BF16), do_, preferred_element_type=F32)
                out_ref[1, :, hs] = (ck_ref[:, hs] + dkk[:STEPS]).astype(BF16)
                out_ref[2, :, hs] = (cv_ref[:, hs] + dvv[:STEPS]).astype(BF16)
                ck_ref[:, hs] = dkk[STEPS:]
                cv_ref[:, hs] = dvv[STEPS:]

        @pl.when(b == n_blocks)
        def _():
            out_ref[1] = ck_ref[...].astype(BF16)
            out_ref[2] = cv_ref[...].astype(BF16)

    last = n_blocks - 1

    def cur(which):
        return pl.BlockSpec((None, STEPS, aw), lambda b: (which, jnp.minimum(b, last), 0))

    def prev(which):
        return pl.BlockSpec((None, STEPS, aw), lambda b: (which, jnp.clip(b - 1, 0, last), 0))

    row = pl.BlockSpec((STEPS, aw), lambda b: (jnp.minimum(b, last), 0))
    return pl.pallas_call(
        body, name=f"attn_bwd{g}", grid=(n_blocks + 1,),
        in_specs=[cur(0), row, row, row, cur(1), prev(1), cur(2), prev(2)],
        out_specs=pl.BlockSpec((3, STEPS, aw), lambda b: (0, jnp.clip(b - 1, 0, last), 0)),
        out_shape=jax.ShapeDtypeStruct((3, s, aw), BF16),
        scratch_shapes=[pltpu.VMEM((STEPS, aw), F32)] * 3,
        compiler_params=_params(("arbitrary",)))(qkv, do, lse, dd, qkv, qkv, qkv, qkv)


def _weight_grad(at, b, tn, col_blocks, name):
    m, k = at.shape
    n = b.shape[1]
    tm = _divisor_tile(m, 1024, 16)
    tk = _divisor_tile(k, 2048, 128)
    nk = k // tk

    def body(a_ref, b_ref, o_ref, acc_ref):
        kk = pl.program_id(2)

        @pl.when(kk == 0)
        def _():
            acc_ref[...] = jnp.zeros_like(acc_ref)

        acc_ref[...] += jnp.dot(a_ref[...], b_ref[...], preferred_element_type=F32)

        @pl.when(kk == nk - 1)
        def _():
            o_ref[...] = acc_ref[...].astype(BF16)

    if col_blocks:
        out_spec = pl.BlockSpec((None, tm, tn), lambda i, j, kk: (j, i, 0))
        out_shape = jax.ShapeDtypeStruct((n // tn, m, tn), BF16)
    else:
        out_spec = pl.BlockSpec((tm, tn), lambda i, j, kk: (i, j))
        out_shape = jax.ShapeDtypeStruct((m, n), BF16)
    return pl.pallas_call(
        body, name=name, grid=(m // tm, n // tn, nk),
        in_specs=[pl.BlockSpec((tm, tk), lambda i, j, kk: (i, kk)), pl.BlockSpec((tk, tn), lambda i, j, kk: (kk, j))],
        out_specs=out_spec, out_shape=out_shape, scratch_shapes=[pltpu.VMEM((tm, tn), F32)],
        compiler_params=_params(("parallel", "parallel", "arbitrary")))(at, b)


def _w_in_grad_part(xt, b, col_of, n_local, tn, w_shape, prev, name):
    d, s = xt.shape
    per_chip = w_shape[2] // tn
    tm = _divisor_tile(d, 1024, 16)
    tk = _divisor_tile(s, 2048, 128)
    nk = s // tk

    def body(*refs):
        a_ref, b_ref, o_ref, acc_ref = refs[0], refs[1], refs[-2], refs[-1]
        kk = pl.program_id(2)

        @pl.when(kk == 0)
        def _():
            acc_ref[...] = jnp.zeros_like(acc_ref)

        acc_ref[...] += jnp.dot(a_ref[...], b_ref[...], preferred_element_type=F32)

        @pl.when(kk == nk - 1)
        def _():
            o_ref[...] = acc_ref[...].astype(BF16)

    if b.ndim == 3:
        sub = b.shape[2] // tn
        b_spec = pl.BlockSpec((None, tk, tn), lambda j, i, kk: (j // sub, kk, j % sub))
    else:
        b_spec = pl.BlockSpec((tk, tn), lambda j, i, kk: (kk, j))
    in_specs = [pl.BlockSpec((tm, tk), lambda j, i, kk: (i, kk)), b_spec]
    args = [xt, b]
    aliases = {}
    if prev is not None:
        in_specs.append(ANY)
        args.append(prev)
        aliases = {2: 0}
    return pl.pallas_call(
        body, name=name, grid=(n_local, d // tm, nk), in_specs=in_specs,
        out_specs=pl.BlockSpec((None, tm, tn), lambda j, i, kk: (col_of(j) // per_chip, i, col_of(j) % per_chip)),
        out_shape=jax.ShapeDtypeStruct(w_shape, BF16), scratch_shapes=[pltpu.VMEM((tm, tn), F32)],
        input_output_aliases=aliases,
        compiler_params=_params(("parallel", "parallel", "arbitrary")))(*args)


def _x_grad(dqkv, rest, w4, dr, aw, tn, after=None):
    s, d = dr.shape
    sub = aw // tn
    n_qkv = 3 * N_GROUPS * sub
    los, lo = [], n_qkv
    for p in rest:
        los.append(lo)
        lo += p.shape[1] // tn
    n_blocks = lo
    per_chip = n_blocks // N_CHIPS
    tm = _divisor_tile(s, 512, 16 * DILATIONS[-1])

    ordered = [] if after is None else [after]

    def body(*refs):
        refs = refs[len(ordered):]
        q_refs, r_refs = refs[:N_GROUPS], refs[N_GROUPS:N_GROUPS + len(rest)]
        w_ref, dr_ref, o_ref, acc_ref, scratch = refs[-5:]
        j = pl.program_id(1)

        @pl.when(j == 0)
        def _():
            acc_ref[...] = ALPHA * dr_ref[...]

        for g, dil in enumerate(DILATIONS):
            @pl.when((j < n_qkv) & (lax.rem(j // sub, N_GROUPS) == g))
            def _(g=g, dil=dil):
                rows = _merge_rows(q_refs[g], scratch, dil).astype(BF16)
                acc_ref[...] += lax.dot_general(rows, w_ref[...], NT, preferred_element_type=F32)

        for p_ref, lo_, piece in zip(r_refs, los, rest):
            @pl.when((j >= lo_) & (j < lo_ + piece.shape[1] // tn))
            def _(p_ref=p_ref):
                acc_ref[...] += lax.dot_general(p_ref[...], w_ref[...], NT, preferred_element_type=F32)

        @pl.when(j == n_blocks - 1)
        def _():
            o_ref[...] = acc_ref[...]

    def qkv_spec(dil):
        def index(i, j):
            region = jnp.minimum(j // sub, 3 * N_GROUPS - 1)
            return region // N_GROUPS, 0, i, jnp.where(j < n_qkv, j % sub, 0)

        return pl.BlockSpec((None, dil, tm // dil, tn), index)

    def rest_spec(lo_, piece):
        n = piece.shape[1] // tn
        return pl.BlockSpec((tm, tn), lambda i, j: (i, jnp.clip(j - lo_, 0, n - 1)))

    row = pl.BlockSpec((tm, d), lambda i, j: (i, 0))
    return pl.pallas_call(
        body, name="x_grad", grid=(s // tm, n_blocks),
        in_specs=[pl.BlockSpec(t.shape, lambda i, j: (0, 0)) for t in ordered]
        + [qkv_spec(dil) for dil in DILATIONS] + [rest_spec(lo_, p) for lo_, p in zip(los, rest)]
        + [pl.BlockSpec((None, d, tn), lambda i, j: (j // per_chip, 0, j % per_chip)), row],
        out_specs=row, out_shape=jax.ShapeDtypeStruct((s, d), F32),
        scratch_shapes=[pltpu.VMEM((tm, d), F32), _permute_scratch(tm, tn)],
        compiler_params=_params(("parallel", "arbitrary"), vmem_mib=56))(
            *ordered, *[t.reshape(3, dil, s // dil, aw) for t, dil in zip(dqkv, DILATIONS)], *rest, w4, dr)


def _to_subsequences(t, dilation):
    s, w = t.shape
    return t.reshape(s // dilation, dilation, w).transpose(1, 0, 2).reshape(s, w)


def _local_step(x, target, w_in4, b_gate, w_pool, pool_scale, wpa4, wpp4, w_out, gamma, beta, start_exchange=None):
    s, d = x.shape
    aw, pw = wpa4.shape[1], wpp4.shape[1]
    tn = _col_tile(aw, pw, w_in4.shape[2])
    sub = aw // tn
    qkv_w = 3 * N_GROUPS * aw

    xb = x.astype(BF16)
    qkv = [_in_proj_qkv(xb, w_in4, g, aw, tn) for g in range(N_GROUPS)]
    zuz = _in_proj(xb, w_in4, qkv_w, aw + 2 * pw, tn, F32, "in_proj_zuz")
    gpre = _in_proj(xb, w_in4, qkv_w + aw + 2 * pw, 2 * d, tn, F32, "in_proj_gates")

    attn = [_attn_fwd(qkv[g], g) for g in range(N_GROUPS)]
    o, y_attn, y_attn_t, lse = _combine_groups([a[0] for a in attn], [a[1] for a in attn], zuz, aw)
    pooled, lin, y_pool, y_pool_t = _pool_fwd(zuz, w_pool, pool_scale, aw, pw)
    a, p, merged, merged_t = _proj_merge(y_attn, y_pool, wpa4, wpp4, gpre, b_gate)
    dr, drb, loss_lanes, d_gamma, d_beta = _out_norm_loss(merged, w_out, x, target, gamma, beta)

    da, dp, d_gpre_a, d_gpre_p, d_b_a, d_b_p = _merge_bwd(drb, w_out, a, p, gpre, b_gate)
    d_b_gate = jnp.concatenate([d_b_a, d_b_p], axis=1)
    d_w_out = _weight_grad(merged_t, drb, d // N_CHIPS, False, "w_out_grad")
    d_wpa4 = _weight_grad(y_attn_t, da, d // N_CHIPS, True, "w_proj_attn_grad")
    d_wpp4 = _weight_grad(y_pool_t, dp, d // N_CHIPS, True, "w_proj_pool_grad")
    d_z_attn, d_o, dd = _attn_gate_bwd(da, wpa4, zuz, o)
    d_z_pool, d_pooled, d_w_pool, d_pool_scale = _pool_gate_bwd(dp, wpp4, zuz, lin, pooled, w_pool, pool_scale, aw)
    d_u = _pool_bwd(d_pooled)
    dqkv = [_attn_bwd(qkv[g], d_o[g], lse[g], dd[g], g) for g in range(N_GROUPS)]

    rest = [d_z_attn, d_u, d_z_pool, d_gpre_a, d_gpre_p]
    d_w_in4 = None
    for g, dil in enumerate(DILATIONS):
        xt = _to_subsequences(xb, dil).T
        d_w_in4 = _w_in_grad_part(xt, dqkv[g], lambda j, g=g: ((j // sub) * N_GROUPS + g) * sub + j % sub, 3 * sub,
                                  tn, w_in4.shape, d_w_in4, f"w_in_grad_qkv{g}")
    lo = qkv_w // tn
    for i, piece in enumerate(rest):
        n_local = piece.shape[1] // tn
        d_w_in4 = _w_in_grad_part(xb.T, piece, lambda j, lo=lo: lo + j, n_local, tn, w_in4.shape, d_w_in4,
                                  f"w_in_grad_rest{i}")
        lo += n_local
    grads = dict(loss_lanes=loss_lanes, w_in=d_w_in4, b_gate=d_b_gate, w_pool=d_w_pool,
                 pool_scale=d_pool_scale, w_proj_attn=d_wpa4, w_proj_pool=d_wpp4, w_out=d_w_out,
                 ln_gamma=d_gamma, ln_beta=d_beta)
    token = None if start_exchange is None else start_exchange(grads)
    grads["d_x"] = _x_grad(dqkv, rest, w_in4, dr, aw, tn, token)
    return grads


def _pack_small(wpa, wpp, w_out, w_pool):
    width = wpa.shape[1]
    return jnp.concatenate([wpa, wpp, w_out.reshape(-1, width), w_pool.reshape(-1, width)], axis=0)


def _unpack_small(packed, aw, pw, d, pg):
    lead = packed.shape[:-2]
    width = d // N_CHIPS
    r0, r1, r2 = aw, aw + pw, aw + pw + d
    return (packed[..., :r0, :], packed[..., r0:r1, :], packed[..., r1:r2, :].reshape(lead + (width, d)),
            packed[..., r2:, :].reshape(lead + (len(POOL_WINDOWS), pg // N_CHIPS, pg)))


def _pack_rows(vectors, rows):
    flat = jnp.concatenate([v.reshape(-1) for v in vectors])
    return jnp.pad(flat, (0, rows * 128 - flat.shape[0])).reshape(rows, 128)


def _unpack_rows(packed, sizes):
    flat, out, lo = packed.reshape(-1), [], 0
    for n in sizes:
        out.append(flat[lo:lo + n].reshape(1, n))
        lo += n
    return out


def kernel(x, w_in, b_gate, w_pool, pool_scale, w_proj_attn, w_proj_pool, w_out, ln_gamma, ln_beta, loss_target, m_w_in, m_b_gate, m_w_pool, m_pool_scale, m_w_proj_attn, m_w_proj_pool, m_w_out, m_ln_gamma, m_ln_beta, v_w_in, v_b_gate, v_w_pool, v_pool_scale, v_w_proj_attn, v_w_proj_pool, v_w_out, v_ln_gamma, v_ln_beta):
    s, d = x.shape[1], x.shape[2]
    aw, pw = w_proj_attn.shape[1], w_proj_pool.shape[1]
    pg = w_pool.shape[3]
    n_win = len(POOL_WINDOWS)

    def small(wpa, wpp, wo, wpl):
        return _pack_small(wpa[0], wpp[0], wo[0], wpl[0])

    chip = 2 * lax.axis_index("x") + lax.axis_index("y")
    core = lax.axis_index("c")

    w_small = small(w_proj_attn, w_proj_pool, w_out, w_pool)
    w_in4, small4 = _gather_weights([_place_block(w_in[0], N_CHIPS, chip, BF16, "place_w_in"),
                                     _place_block(w_small, N_CHIPS, chip, BF16, "place_w_small")])
    wpa4, wpp4, w_out4, w_pool4 = _unpack_small(small4, aw, pw, d, pg)
    w_out_full = w_out4.reshape(d, d)
    w_pool_full = w_pool4.transpose(1, 0, 2, 3).reshape(n_win, pg, pg)

    exchange = {}

    def start_exchange(g):
        g_pool4 = g["w_pool"].reshape(n_win, N_CHIPS, pg // N_CHIPS, pg).transpose(1, 0, 2, 3).astype(BF16)
        g_out4 = g["w_out"].reshape(N_CHIPS, d // N_CHIPS, d)
        g_small4 = jnp.concatenate([g["w_proj_attn"], g["w_proj_pool"], g_out4.reshape(N_CHIPS, -1, d // N_CHIPS),
                                    g_pool4.reshape(N_CHIPS, -1, d // N_CHIPS)], axis=1)
        theirs_big, theirs_small = _swap_halves([g["w_in"], g_small4])
        chip_big = _add_halves(g["w_in"], theirs_big, core, "add_cores_big")
        chip_small = _add_halves(g_small4, theirs_small, core, "add_cores_small")
        sems, sums, placed, token = _scatter_start(
            [chip_big, chip_small],
            [_place_block(chip_big, N_CHIPS, chip, BF16, "place_sum_big", src_slot=chip),
             _place_block(chip_small, N_CHIPS, chip, BF16, "place_sum_small", src_slot=chip)])
        exchange.update(sems=sems, sums=sums, placed=placed)
        return token

    g = _local_step(x[0], loss_target[0], w_in4, b_gate, w_pool_full, pool_scale, wpa4, wpp4, w_out_full,
                    ln_gamma, ln_beta, start_exchange)
    got_big, got_small = _scatter_wait(exchange["sems"], exchange["sums"], exchange["placed"], g["d_x"])
    half_big = _sum_slots(got_big, "sum_chips_big")
    half_small = _sum_slots(got_small, "sum_chips_small")
    grad_w_in, grad_small = _join_halves([_place_block(half_big, 2, core, F32, "place_half_big"),
                                          _place_block(half_small, 2, core, F32, "place_half_small")])
    grad_w_in = grad_w_in.reshape(-1, grad_w_in.shape[2])
    grad_small = grad_small.reshape(-1, grad_small.shape[2])

    sizes = [b_gate.shape[1], pool_scale.shape[1], d, d, 1]
    rows = -(-sum(sizes) // (8 * 128)) * 8
    loss_part = (0.5 / d) * jnp.sum(g["loss_lanes"]).reshape(1, 1)
    parts = _gather_rows(_pack_rows([g["b_gate"], g["pool_scale"], g["ln_gamma"], g["ln_beta"], loss_part], rows))
    zero = jnp.zeros((1, 1), F32)
    packed = [_pack_rows(vs, rows) for vs in ([b_gate, pool_scale, ln_gamma, ln_beta, zero],
                                              [m_b_gate, m_pool_scale, m_ln_gamma, m_ln_beta, zero],
                                              [v_b_gate, v_pool_scale, v_ln_gamma, v_ln_beta, zero])]
    rep = [_unpack_rows(t, sizes) for t in _sum_rows_adamw(parts, *packed)]
    loss = rep[0][4].reshape(())

    upd_in = _adamw(w_in[0], grad_w_in, m_w_in[0], v_w_in[0], "adamw_w_in")
    upd_small = _adamw(w_small, grad_small, small(m_w_proj_attn, m_w_proj_pool, m_w_out, m_w_pool),
                       small(v_w_proj_attn, v_w_proj_pool, v_w_out, v_w_pool), "adamw_small")

    def leaves(big, packed_small, replicated):
        wpa_, wpp_, wo_, wpl_ = _unpack_small(packed_small, aw, pw, d, pg)
        return [big[None], replicated[0], wpl_[None], replicated[1], wpa_[None], wpp_[None], wo_[None],
                replicated[2], replicated[3]]

    out = [loss, g["d_x"][None]]
    out += leaves(grad_w_in, grad_small, rep[0])
    for i in range(3):
        out += leaves(upd_in[i], upd_small[i], rep[1 + i])
    return tuple(out)
```

```python
import math

import jax
import jax.numpy as jnp
from jax import lax
from jax.experimental import pallas as pl
from jax.experimental.pallas import tpu as pltpu

F32 = jnp.float32
BF16 = jnp.bfloat16
MESH = pl.DeviceIdType.MESH
ANY = pl.BlockSpec(memory_space=pl.ANY)

HEAD_DIM = 128
STEPS = 128
DILATIONS = (1, 4, 16)
N_GROUPS = len(DILATIONS)
POOL_WINDOWS = (2, 4, 8, 16)
POOL_HALO = 16
N_CHIPS = 4
N_DEV = 8
ALPHA = 2.0 ** 0.25
LN_EPS = 1e-5
NEG_INF = -1e30
SCORE_SCALE = HEAD_DIM ** -0.5
ADAM_LR = 0.001
ADAM_B1 = 0.9
ADAM_B2 = 0.999
ADAM_EPS = 1e-08
ADAM_WD = 0.01
ADAM_STEP = 10
MIB = 2 ** 20
NT = (((1,), (1,)), ((), ()))
DMA_STREAMS = 8


def _params(semantics=None, vmem_mib=48):
    return pltpu.CompilerParams(dimension_semantics=semantics, vmem_limit_bytes=vmem_mib * MIB)


def _divisor_tile(n, target, multiple):
    best = None
    for t in range(multiple, min(n, target) + 1, multiple):
        if n % t == 0:
            best = t
    assert best is not None, (n, target, multiple)
    return best


def _col_tile(*widths):
    g = 0
    for w in widths:
        g = math.gcd(g, w)
    return _divisor_tile(g, 1024, 128)


def _sigmoid(z):
    return jax.nn.sigmoid(z)


def _dsilu(z, sg):
    return sg * (1.0 + z * (1.0 - sg))


def _place():
    x, y, c = lax.axis_index("x"), lax.axis_index("y"), lax.axis_index("c")
    others = [(1 - x, y), (x, 1 - y), (1 - x, 1 - y)]
    return x, y, c, (x, y, 1 - c), others


def _remote(src, dst, send_sem, recv_sem, dev):
    return pltpu.make_async_remote_copy(src_ref=src, dst_ref=dst, send_sem=send_sem, recv_sem=recv_sem,
                                        device_id=dev, device_id_type=MESH)


def _row_pieces(n_rows, streams=DMA_STREAMS, multiple=16):
    size = -(-n_rows // (streams * multiple)) * multiple
    return [(lo, min(size, n_rows - lo)) for lo in range(0, n_rows, size)]


def _start_streams(make, n_rows):
    for lo, size in _row_pieces(n_rows):
        make(pl.ds(lo, size)).start()


def _gather_weights(placed):
    n = len(placed)

    def body(*refs):
        dst = refs[n:2 * n]
        send_sems, recv_sems = refs[2 * n:]
        x, y, c, sibling, others = _place()
        me = 2 * x + y
        sent = []
        for i in range(n):
            half = dst[i].shape[1] // 2
            mine = c * half
            for j, (ox, oy) in enumerate(others):
                slab = dst[i].at[me, pl.ds(mine, half)]
                cp = _remote(slab, slab, send_sems.at[6 * i + j], recv_sems.at[6 * i + j], (ox, oy, c))
                cp.start()
                sent.append(cp)
        for i in range(n):
            half = dst[i].shape[1] // 2
            mine = c * half
            for j, (ox, oy) in enumerate(others):
                blk = dst[i].at[2 * ox + oy]
                slab = blk.at[pl.ds(mine, half)]
                _remote(slab, slab, send_sems.at[6 * i + j], recv_sems.at[6 * i + j], (ox, oy, c)).wait_recv()
                k = 6 * i + 3 + j
                _start_streams(lambda r, blk=blk, k=k: _remote(blk.at[pl.ds(mine + r.start, r.size)],
                                                               blk.at[pl.ds(mine + r.start, r.size)],
                                                               send_sems.at[k], recv_sems.at[k], sibling), half)
                sent.append(_remote(slab, slab, send_sems.at[k], recv_sems.at[k], sibling))
        for i in range(n):
            half = dst[i].shape[1] // 2
            for j, (ox, oy) in enumerate(others):
                slab = dst[i].at[2 * ox + oy, pl.ds((1 - c) * half, half)]
                _remote(slab, slab, send_sems.at[6 * i + 3 + j], recv_sems.at[6 * i + 3 + j], sibling).wait_recv()
        for cp in sent:
            cp.wait_send()

    return pl.pallas_call(
        body, name="gather_weights", out_shape=[jax.ShapeDtypeStruct(s.shape, s.dtype) for s in placed],
        in_specs=[ANY] * n, out_specs=[ANY] * n, input_output_aliases={i: i for i in range(n)},
        scratch_shapes=[pltpu.SemaphoreType.DMA((6 * n,)), pltpu.SemaphoreType.DMA((6 * n,))],
    )(*placed)


def _swap_halves(grads):
    n = len(grads)

    def body(*refs):
        g, theirs = refs[:n], refs[n:2 * n]
        send_sems, recv_sems = refs[2 * n:]
        x, y, c, sibling, _ = _place()
        for i in range(n):
            half = g[i].shape[1] // 2
            give = (1 - c) * half
            for b in range(N_CHIPS):
                _start_streams(lambda r, i=i, b=b: _remote(
                    g[i].at[b, pl.ds(give + r.start, r.size)], theirs[i].at[b, r], send_sems.at[i], recv_sems.at[i],
                    sibling), half)
        for i in range(n):
            _remote(theirs[i], theirs[i], send_sems.at[i], recv_sems.at[i], sibling).wait()

    return pl.pallas_call(
        body, name="swap_halves",
        out_shape=[jax.ShapeDtypeStruct((s.shape[0], s.shape[1] // 2) + s.shape[2:], s.dtype) for s in grads],
        in_specs=[ANY] * n, out_specs=[ANY] * n,
        scratch_shapes=[pltpu.SemaphoreType.DMA((n,)), pltpu.SemaphoreType.DMA((n,))],
    )(*grads)


HBM = pl.BlockSpec(memory_space=pltpu.HBM)
SEM = pl.BlockSpec(memory_space=pltpu.SEMAPHORE)
DATAFLOW = pltpu.SideEffectType.DATAFLOW_SIDE_EFFECTING


def _broadcast_copies(buf, send_sems, recv_sems):
    x, y, c, _, others = _place()
    mine = buf.at[2 * x + y]
    return [_remote(mine, mine, send_sems[j], recv_sems[j], (ox, oy, c)) for j, (ox, oy) in enumerate(others)]


def _broadcast_start(placed, name):
    k = N_CHIPS - 1

    def body(buf, *refs):
        send_sems, recv_sems, token = refs[:k], refs[k:2 * k], refs[-1]
        for cp in _broadcast_copies(buf, send_sems, recv_sems):
            cp.start()
        token[...] = jnp.zeros_like(token)

    out = pl.pallas_call(
        body, name=name,
        out_shape=[pltpu.SemaphoreType.DMA(())] * (2 * k) + [pltpu.HBM(placed.shape, placed.dtype),
                                                             jax.ShapeDtypeStruct((8, 128), F32)],
        in_specs=[HBM], out_specs=[SEM] * (2 * k) + [HBM, pl.BlockSpec(memory_space=pltpu.VMEM)],
        input_output_aliases={0: 2 * k},
        compiler_params=pltpu.CompilerParams(has_side_effects=DATAFLOW),
    )(pltpu.with_memory_space_constraint(placed, pltpu.HBM))
    return out[:2 * k], out[2 * k], out[-1]


def _broadcast_wait(sems, placed, after, name):
    k = N_CHIPS - 1

    def body(buf, *refs):
        send_sems, recv_sems = refs[:k], refs[k:2 * k]
        for cp in _broadcast_copies(buf, send_sems, recv_sems):
            cp.wait_send()
            cp.wait_recv()

    return pl.pallas_call(
        body, name=name, out_shape=pltpu.HBM(placed.shape, placed.dtype),
        in_specs=[HBM] + [SEM] * (2 * k) + [ANY], out_specs=HBM, input_output_aliases={0: 0},
        compiler_params=pltpu.CompilerParams(has_side_effects=DATAFLOW),
    )(placed, *sems, after)


def _scatter_copies(s, got, send_sems, recv_sems):
    x, y, c, _, others = _place()
    me = 2 * x + y
    n = len(s)
    return [_remote(s[i].at[2 * ox + oy], got[i].at[me], send_sems[3 * i + j], recv_sems[3 * i + j], (ox, oy, c))
            for i in range(n) for j, (ox, oy) in enumerate(others)]


def _scatter_start(sums, placed):
    n = len(sums)
    k = 3 * n

    def body(*refs):
        s, got, token = refs[:n], refs[n:2 * n], refs[-1]
        send_sems, recv_sems = refs[2 * n:2 * n + k], refs[2 * n + k:2 * n + 2 * k]
        for cp in _scatter_copies(s, got, send_sems, recv_sems):
            cp.start()
        token[...] = jnp.zeros_like(token)

    hbm = [pltpu.HBM(a.shape, a.dtype) for a in list(sums) + list(placed)]
    out = pl.pallas_call(
        body, name="scatter_start",
        out_shape=[pltpu.SemaphoreType.DMA(())] * (2 * k) + hbm + [jax.ShapeDtypeStruct((8, 128), F32)],
        in_specs=[HBM] * (2 * n), out_specs=[SEM] * (2 * k) + [HBM] * (2 * n) + [pl.BlockSpec(memory_space=pltpu.VMEM)],
        input_output_aliases={i: 2 * k + i for i in range(2 * n)},
        compiler_params=pltpu.CompilerParams(has_side_effects=DATAFLOW),
    )(*[pltpu.with_memory_space_constraint(a, pltpu.HBM) for a in list(sums) + list(placed)])
    return out[:2 * k], out[2 * k:2 * k + n], out[2 * k + n:2 * k + 2 * n], out[-1]


def _scatter_wait(sems, sums, placed, after):
    n = len(sums)
    k = 3 * n

    def body(*refs):
        s, got = refs[:n], refs[n:2 * n]
        send_sems, recv_sems = refs[2 * n:2 * n + k], refs[2 * n + k:2 * n + 2 * k]
        for cp in _scatter_copies(s, got, send_sems, recv_sems):
            cp.wait_send()
            cp.wait_recv()

    hbm = [pltpu.HBM(a.shape, a.dtype) for a in list(sums) + list(placed)]
    out = pl.pallas_call(
        body, name="scatter_wait", out_shape=hbm,
        in_specs=[HBM] * (2 * n) + [SEM] * (2 * k) + [ANY], out_specs=[HBM] * (2 * n),
        input_output_aliases={i: i for i in range(2 * n)},
        compiler_params=pltpu.CompilerParams(has_side_effects=DATAFLOW),
    )(*sums, *placed, *sems, after)
    return out[n:]


def _join_halves(placed):
    n = len(placed)

    def body(*refs):
        full = refs[n:2 * n]
        send_sems, recv_sems = refs[2 * n:]
        x, y, c, sibling, _ = _place()
        for i in range(n):
            _start_streams(lambda r, i=i: _remote(full[i].at[c, r], full[i].at[c, r], send_sems.at[i],
                                                  recv_sems.at[i], sibling), full[i].shape[1])
        for i in range(n):
            cp = _remote(full[i].at[c], full[i].at[1 - c], send_sems.at[i], recv_sems.at[i], sibling)
            cp.wait_recv()
            cp.wait_send()

    return pl.pallas_call(
        body, name="join_halves", out_shape=[jax.ShapeDtypeStruct(s.shape, s.dtype) for s in placed],
        in_specs=[ANY] * n, out_specs=[ANY] * n, input_output_aliases={i: i for i in range(n)},
        scratch_shapes=[pltpu.SemaphoreType.DMA((n,)), pltpu.SemaphoreType.DMA((n,))],
    )(*placed)


def _gather_rows(row):
    def body(row_ref, out_ref, send_sems, recv_sems, local_sem):
        x, y, c = lax.axis_index("x"), lax.axis_index("y"), lax.axis_index("c")
        me = 4 * x + 2 * y + c
        local = pltpu.make_async_copy(row_ref, out_ref.at[me], local_sem)
        local.start()
        sent = []
        peers = []
        for k in range(1, N_DEV):
            px, py, pc = x ^ (k >> 2), y ^ ((k >> 1) & 1), c ^ (k & 1)
            peers.append((k, px, py, pc))
            cp = _remote(row_ref, out_ref.at[me], send_sems.at[k - 1], recv_sems.at[k - 1], (px, py, pc))
            cp.start()
            sent.append(cp)
        for k, px, py, pc in peers:
            slot = out_ref.at[4 * px + 2 * py + pc]
            _remote(slot, slot, send_sems.at[k - 1], recv_sems.at[k - 1], (px, py, pc)).wait_recv()
        for cp in sent:
            cp.wait_send()
        local.wait()

    return pl.pallas_call(
        body, name="gather_rows", out_shape=jax.ShapeDtypeStruct((N_DEV,) + row.shape, row.dtype),
        in_specs=[ANY], out_specs=ANY,
        scratch_shapes=[pltpu.SemaphoreType.DMA((N_DEV - 1,)), pltpu.SemaphoreType.DMA((N_DEV - 1,)),
                        pltpu.SemaphoreType.DMA],
    )(row)


def _scalar(i):
    return jnp.reshape(i, (1,)).astype(jnp.int32)


def _place_block(src, n_slots, slot, out_dtype, name, src_slot=None):
    rows, cols = src.shape[-2:]
    tr = _divisor_tile(rows, max(16, (2 * MIB) // (cols * 4)), 16)

    def body(slot_ref, s_ref, o_ref):
        o_ref[...] = s_ref[...].astype(o_ref.dtype)

    if src_slot is None:
        in_spec = pl.BlockSpec((tr, cols), lambda r, sl: (r, 0))
        slots = _scalar(slot)
    else:
        in_spec = pl.BlockSpec((None, tr, cols), lambda r, sl: (sl[1], r, 0))
        slots = jnp.concatenate([_scalar(slot), _scalar(src_slot)])
    return pl.pallas_call(
        body, name=name, out_shape=jax.ShapeDtypeStruct((n_slots, rows, cols), out_dtype),
        grid_spec=pltpu.PrefetchScalarGridSpec(
            num_scalar_prefetch=1, grid=(rows // tr,), in_specs=[in_spec],
            out_specs=pl.BlockSpec((None, tr, cols), lambda r, sl: (sl[0], r, 0))),
        compiler_params=_params(("parallel",)))(slots, src)


def _add_halves(g, theirs, core, chip, name):
    n, half, cols = theirs.shape
    tr = _divisor_tile(half, max(16, (2 * MIB) // (cols * 4)), 16)
    per = half // tr

    def body(at_ref, a_ref, b_ref, o_ref, own_ref):
        total = (a_ref[...].astype(F32) + b_ref[...].astype(F32)).astype(o_ref.dtype)
        o_ref[...] = total

        @pl.when(pl.program_id(1) == at_ref[1])
        def _():
            own_ref[...] = total

    spec = pl.BlockSpec((None, tr, cols), lambda r, i, at: (i, r, 0))
    shape = jax.ShapeDtypeStruct(theirs.shape, BF16)
    return pl.pallas_call(
        body, name=name, out_shape=[shape, shape],
        grid_spec=pltpu.PrefetchScalarGridSpec(
            num_scalar_prefetch=1, grid=(per, n),
            in_specs=[pl.BlockSpec((None, tr, cols), lambda r, i, at: (i, at[0] * per + r, 0)), spec],
            out_specs=[spec, pl.BlockSpec((None, tr, cols), lambda r, i, at: (at[1], r, 0))]),
        compiler_params=_params(("parallel", "arbitrary")))(jnp.concatenate([_scalar(core), _scalar(chip)]), g, theirs)


def _sum_slots(a, core, name):
    n, rows, cols = a.shape
    tr = _divisor_tile(rows, max(16, (2 * MIB) // (cols * 4 * n)), 16)

    def body(c_ref, a_ref, o_ref):
        acc = a_ref[0].astype(F32)
        for i in range(1, n):
            acc = acc + a_ref[i].astype(F32)
        o_ref[...] = acc

    return pl.pallas_call(
        body, name=name, out_shape=jax.ShapeDtypeStruct((2, rows, cols), F32),
        grid_spec=pltpu.PrefetchScalarGridSpec(
            num_scalar_prefetch=1, grid=(rows // tr,),
            in_specs=[pl.BlockSpec((n, tr, cols), lambda r, c: (0, r, 0))],
            out_specs=pl.BlockSpec((None, tr, cols), lambda r, c: (c[0], r, 0))),
        compiler_params=_params(("parallel",)))(_scalar(core), a)


def _adamw_math(w, g, m, v):
    m = ADAM_B1 * m + (1.0 - ADAM_B1) * g
    v = ADAM_B2 * v + (1.0 - ADAM_B2) * (g * g)
    m_hat = m / (1.0 - ADAM_B1 ** ADAM_STEP)
    v_hat = v / (1.0 - ADAM_B2 ** ADAM_STEP)
    delta = -ADAM_LR * (m_hat / (jnp.sqrt(v_hat) + ADAM_EPS) + ADAM_WD * w)
    return delta, m, v


def _adamw(w, g, m, v, name):
    rows, cols = w.shape
    tr = _divisor_tile(rows, max(8, MIB // (cols * 4)), 8)

    def body(w_ref, g_ref, m_ref, v_ref, d_ref, nm_ref, nv_ref):
        d, nm, nv = _adamw_math(w_ref[...], g_ref[...], m_ref[...], v_ref[...])
        d_ref[...] = d
        nm_ref[...] = nm
        nv_ref[...] = nv

    spec = pl.BlockSpec((tr, cols), lambda r: (r, 0))
    shape = jax.ShapeDtypeStruct((rows, cols), F32)
    return pl.pallas_call(body, name=name, grid=(rows // tr,), in_specs=[spec] * 4, out_specs=[spec] * 3,
                          out_shape=[shape] * 3, compiler_params=_params(("parallel",)))(w, g, m, v)


def _sum_rows_adamw(parts, w, m, v):
    def body(p_ref, w_ref, m_ref, v_ref, g_ref, d_ref, nm_ref, nv_ref):
        g = p_ref[0]
        for i in range(1, N_DEV):
            g = g + p_ref[i]
        d, nm, nv = _adamw_math(w_ref[...], g, m_ref[...], v_ref[...])
        g_ref[...] = g
        d_ref[...] = d
        nm_ref[...] = nm
        nv_ref[...] = nv

    shape = jax.ShapeDtypeStruct(w.shape, F32)
    return pl.pallas_call(body, name="sum_rows_adamw", out_shape=[shape] * 4)(parts, w, m, v)


LANES = 128


def _permute_scratch(rows, width):
    return pltpu.VMEM((width // LANES, rows, LANES), F32)


def _split_rows(value, scratch, dil):
    if dil == 1:
        return [value]
    rows = value.shape[0] // dil
    slabs = value.shape[1] // LANES
    for c in range(slabs):
        scratch[c] = value[:, c * LANES:(c + 1) * LANES]
    return [jnp.concatenate([scratch[c, pl.ds(r, rows, stride=dil), :] for c in range(slabs)], axis=1)
            for r in range(dil)]


def _merge_rows(ref, scratch, dil):
    if dil == 1:
        return ref[0].astype(F32)
    rows = ref.shape[1]
    slabs = ref.shape[2] // LANES
    for r in range(dil):
        part = ref[r].astype(F32)
        for c in range(slabs):
            scratch[c, pl.ds(r, rows, stride=dil), :] = part[:, c * LANES:(c + 1) * LANES]
    return jnp.concatenate([scratch[c] for c in range(slabs)], axis=1)


def _grouped_view(t, dil):
    return t.reshape(dil, t.shape[0] // dil, t.shape[1])


def _grouped_spec(dil, rows, width, index):
    return pl.BlockSpec((dil, rows // dil, width), index)


def _in_proj(xb, w4, col0, ncols, tn, out_dtype, name):
    s, d = xb.shape
    per_chip = w4.shape[2] // tn
    j0 = col0 // tn
    tm = _divisor_tile(s, 512, 16)

    def body(a_ref, b_ref, o_ref):
        o_ref[...] = jnp.dot(a_ref[...], b_ref[...], preferred_element_type=F32).astype(o_ref.dtype)

    return pl.pallas_call(
        body, name=name, grid=(ncols // tn, s // tm),
        in_specs=[pl.BlockSpec((tm, d), lambda j, m: (m, 0)),
                  pl.BlockSpec((None, d, tn), lambda j, m: ((j0 + j) // per_chip, 0, (j0 + j) % per_chip))],
        out_specs=pl.BlockSpec((tm, tn), lambda j, m: (m, j)),
        out_shape=jax.ShapeDtypeStruct((s, ncols), out_dtype),
        compiler_params=_params(("parallel", "parallel")))(xb, w4)


def _in_proj_qkv(xb, w4, g, aw, tn, after=None):
    s, d = xb.shape
    dil = DILATIONS[g]
    per_chip = w4.shape[2] // tn
    sub = aw // tn
    tm = _divisor_tile(s, 512, 16 * dil)
    ordered = [] if after is None else [after]

    def body(*refs):
        a_ref, b_ref, o_ref, scratch = refs[len(ordered):]
        res = jnp.dot(a_ref[...], b_ref[...], preferred_element_type=F32)
        for r, part in enumerate(_split_rows(res, scratch, dil)):
            o_ref[r] = part.astype(BF16)

    def w_index(j, m):
        col = ((j // sub) * N_GROUPS + g) * sub + j % sub
        return col // per_chip, 0, col % per_chip

    out = pl.pallas_call(
        body, name=f"in_proj_qkv{g}", grid=(3 * sub, s // tm),
        in_specs=[pl.BlockSpec(t.shape, lambda j, m: (0, 0)) for t in ordered]
        + [pl.BlockSpec((tm, d), lambda j, m: (m, 0)), pl.BlockSpec((None, d, tn), w_index)],
        out_specs=pl.BlockSpec((None, dil, tm // dil, tn), lambda j, m: (j // sub, 0, m, j % sub)),
        out_shape=jax.ShapeDtypeStruct((3, dil, s // dil, aw), BF16),
        scratch_shapes=[_permute_scratch(tm, tn)],
        compiler_params=_params(("parallel", "parallel")))(*ordered, xb, w4)
    return out.reshape(3, s, aw)


def _window_mask(first):
    qi = lax.broadcasted_iota(jnp.int32, (STEPS, 2 * STEPS), 0)
    kj = lax.broadcasted_iota(jnp.int32, (STEPS, 2 * STEPS), 1)
    lowest = jnp.where(first, STEPS, 0)
    return (kj >= qi) & (kj <= qi + STEPS) & (kj >= lowest)


def _attn_fwd(qkv, g):
    _, s, aw = qkv.shape
    heads = aw // HEAD_DIM
    n_blocks = s // STEPS
    per_seq = n_blocks // DILATIONS[g]

    def body(q_ref, kc_ref, kp_ref, vc_ref, vp_ref, o_ref, l_ref):
        mask = _window_mask(lax.rem(pl.program_id(0), per_seq) == 0)
        for h in range(heads):
            hs = slice(h * HEAD_DIM, (h + 1) * HEAD_DIM)
            kk = jnp.concatenate([kp_ref[:, hs], kc_ref[:, hs]], axis=0)
            vv = jnp.concatenate([vp_ref[:, hs], vc_ref[:, hs]], axis=0)
            sc = lax.dot_general(q_ref[:, hs], kk, NT, preferred_element_type=F32) * SCORE_SCALE
            sc = jnp.where(mask, sc, NEG_INF)
            mx = jnp.max(sc, axis=1, keepdims=True)
            e = jnp.exp(sc - mx)
            den = jnp.sum(e, axis=1, keepdims=True)
            o_ref[:, hs] = jnp.dot(e.astype(BF16), vv, preferred_element_type=F32) / den
            l_ref[:, hs] = jnp.broadcast_to(mx + jnp.log(den), (STEPS, HEAD_DIM))

    def cur(which):
        return pl.BlockSpec((None, STEPS, aw), lambda b: (which, b, 0))

    def prev(which):
        return pl.BlockSpec((None, STEPS, aw), lambda b: (which, jnp.maximum(b - 1, 0), 0))

    out = pl.BlockSpec((STEPS, aw), lambda b: (b, 0))
    shape = jax.ShapeDtypeStruct((s, aw), F32)
    return pl.pallas_call(
        body, name=f"attn_fwd{g}", grid=(n_blocks,),
        in_specs=[cur(0), cur(1), prev(1), cur(2), prev(2)], out_specs=[out, out], out_shape=[shape, shape],
        compiler_params=_params(("parallel",)))(qkv, qkv, qkv, qkv, qkv)


def _combine_groups(os, ls, zuz, aw):
    s = zuz.shape[0]
    tr = _divisor_tile(s, 256, 8 * DILATIONS[-1])

    def body(*refs):
        o_refs, l_refs, z_ref = refs[0:3], refs[3:6], refs[6]
        oo_ref, y_ref, yt_ref = refs[7:10]
        lq_refs, scratch = refs[10:13], refs[13]
        ls_ = [_merge_rows(l_refs[g], scratch, dil) for g, dil in enumerate(DILATIONS)]
        mx = jnp.maximum(jnp.maximum(ls_[0], ls_[1]), ls_[2])
        ws = [jnp.exp(l - mx) for l in ls_]
        den = ws[0] + ws[1] + ws[2]
        o = ws[0] * _merge_rows(o_refs[0], scratch, DILATIONS[0])
        for g in range(1, N_GROUPS):
            o = o + ws[g] * _merge_rows(o_refs[g], scratch, DILATIONS[g])
        o = o / den
        z = z_ref[...]
        y = o * (z * _sigmoid(z))
        oo_ref[...] = o
        y_ref[...] = y.astype(BF16)
        yt_ref[...] = y.T.astype(BF16)
        for g, dil in enumerate(DILATIONS):
            for r, part in enumerate(_split_rows(mx + jnp.log(den), scratch, dil)):
                lq_refs[g][r] = part

    grouped = [_grouped_spec(dil, tr, aw, lambda r: (0, r, 0)) for dil in DILATIONS]
    one = pl.BlockSpec((tr, aw), lambda r: (r, 0))
    f = jax.ShapeDtypeStruct((s, aw), F32)
    out = pl.pallas_call(
        body, name="combine_groups", grid=(s // tr,),
        in_specs=grouped + grouped + [one],
        out_specs=[one, one, pl.BlockSpec((aw, tr), lambda r: (0, r))] + grouped,
        out_shape=[f, jax.ShapeDtypeStruct((s, aw), BF16), jax.ShapeDtypeStruct((aw, s), BF16)]
        + [jax.ShapeDtypeStruct((dil, s // dil, aw), F32) for dil in DILATIONS],
        scratch_shapes=[_permute_scratch(tr, aw)],
        compiler_params=_params(("parallel",)))(
            *[_grouped_view(t, dil) for t, dil in zip(os, DILATIONS)],
            *[_grouped_view(t, dil) for t, dil in zip(ls, DILATIONS)], zuz)
    return out[0], out[1], out[2], [t.reshape(s, aw) for t in out[3:]]


def _pool_counts(row0, rows, window):
    t = row0 + lax.broadcasted_iota(jnp.int32, (rows, 1), 0)
    return jnp.minimum(t + 1, window).astype(F32)


def _pool_fwd(zuz, w_pool, pool_scale, aw, pw):
    s = zuz.shape[0]
    pg = pw // len(POOL_WINDOWS)
    tr = _divisor_tile(s, 256, 128)
    u_col, z_col = aw // pw, aw // pw + 1
    assert aw % pw == 0

    def body(u_ref, up_ref, z_ref, w_ref, sc_ref, p_ref, l_ref, y_ref, yt_ref):
        r = pl.program_id(0)
        u = u_ref[...]
        halo = jnp.where(r > 0, up_ref[...], 0.0)
        ext = jnp.concatenate([halo, u], axis=0)
        pieces, lins = [], []
        for gi, window in enumerate(POOL_WINDOWS):
            cs = slice(gi * pg, (gi + 1) * pg)
            acc = ext[:, cs]
            shift = 1
            while shift < window:
                acc = acc + pltpu.roll(acc, shift, 0)
                shift *= 2
            p = acc[POOL_HALO:] / _pool_counts(r * tr, tr, window) - u[:, cs]
            pieces.append(p)
            lins.append(jnp.dot(p.astype(BF16), w_ref[gi], preferred_element_type=F32))
        p = jnp.concatenate(pieces, axis=1)
        lin = jnp.concatenate(lins, axis=1)
        z = z_ref[...]
        y = lin * sc_ref[...] * (z * _sigmoid(z))
        p_ref[...] = p.astype(BF16)
        l_ref[...] = lin
        y_ref[...] = y.astype(BF16)
        yt_ref[...] = y.T.astype(BF16)

    per = tr // POOL_HALO
    out = pl.BlockSpec((tr, pw), lambda r: (r, 0))
    return pl.pallas_call(
        body, name="pool_fwd", grid=(s // tr,),
        in_specs=[pl.BlockSpec((tr, pw), lambda r: (r, u_col)),
                  pl.BlockSpec((POOL_HALO, pw), lambda r: (jnp.maximum(r * per - 1, 0), u_col)),
                  pl.BlockSpec((tr, pw), lambda r: (r, z_col)),
                  pl.BlockSpec((len(POOL_WINDOWS), pg, pg), lambda r: (0, 0, 0)),
                  pl.BlockSpec((1, pw), lambda r: (0, 0))],
        out_specs=[out, out, out, pl.BlockSpec((pw, tr), lambda r: (0, r))],
        out_shape=[jax.ShapeDtypeStruct((s, pw), BF16), jax.ShapeDtypeStruct((s, pw), F32),
                   jax.ShapeDtypeStruct((s, pw), BF16), jax.ShapeDtypeStruct((pw, s), BF16)],
        compiler_params=_params(("parallel",)))(zuz, zuz, zuz, w_pool, pool_scale)


def _proj_merge(y_attn, y_pool, wpa4, wpp4, gpre, b_gate):
    s, aw = y_attn.shape
    pw = y_pool.shape[1]
    tn = wpa4.shape[2]
    d = N_CHIPS * tn
    tm = _divisor_tile(s, 512, 128)

    def body(ya_ref, yp_ref, wa_ref, wp_ref, ga_ref, gp_ref, ba_ref, bp_ref, a_ref, p_ref, m_ref, mt_ref):
        a = jnp.dot(ya_ref[...], wa_ref[...], preferred_element_type=F32)
        p = jnp.dot(yp_ref[...], wp_ref[...], preferred_element_type=F32)
        merged = _sigmoid(ga_ref[...] + ba_ref[...]) * a + _sigmoid(gp_ref[...] + bp_ref[...]) * p
        a_ref[...] = a
        p_ref[...] = p
        m_ref[...] = merged.astype(BF16)
        mt_ref[...] = merged.T.astype(BF16)

    out = pl.BlockSpec((tm, tn), lambda n, m: (m, n))
    f = jax.ShapeDtypeStruct((s, d), F32)
    return pl.pallas_call(
        body, name="proj_merge", grid=(N_CHIPS, s // tm),
        in_specs=[pl.BlockSpec((tm, aw), lambda n, m: (m, 0)), pl.BlockSpec((tm, pw), lambda n, m: (m, 0)),
                  pl.BlockSpec((None, aw, tn), lambda n, m: (n, 0, 0)),
                  pl.BlockSpec((None, pw, tn), lambda n, m: (n, 0, 0)),
                  pl.BlockSpec((tm, tn), lambda n, m: (m, n)), pl.BlockSpec((tm, tn), lambda n, m: (m, N_CHIPS + n)),
                  pl.BlockSpec((1, tn), lambda n, m: (0, n)), pl.BlockSpec((1, tn), lambda n, m: (0, N_CHIPS + n))],
        out_specs=[out, out, out, pl.BlockSpec((tn, tm), lambda n, m: (n, m))],
        out_shape=[f, f, jax.ShapeDtypeStruct((s, d), BF16), jax.ShapeDtypeStruct((d, s), BF16)],
        compiler_params=_params(("parallel", "parallel")))(y_attn, y_pool, wpa4, wpp4, gpre, gpre, b_gate, b_gate)


def _out_norm_loss(merged, w_out, x, target, gamma, beta):
    s, d = x.shape
    tm = _divisor_tile(s, 256, 16)

    def body(m_ref, w_ref, x_ref, t_ref, g_ref, b_ref, dr_ref, drb_ref, loss_ref, dg_ref, db_ref):
        @pl.when(pl.program_id(0) == 0)
        def _():
            loss_ref[...] = jnp.zeros_like(loss_ref)
            dg_ref[...] = jnp.zeros_like(dg_ref)
            db_ref[...] = jnp.zeros_like(db_ref)

        r = ALPHA * x_ref[...] + jnp.dot(m_ref[...], w_ref[...], preferred_element_type=F32)
        mu = jnp.mean(r, axis=1, keepdims=True)
        rc = r - mu
        rstd = lax.rsqrt(jnp.mean(rc * rc, axis=1, keepdims=True) + LN_EPS)
        xhat = rc * rstd
        diff = xhat * g_ref[...] + b_ref[...] - t_ref[...]
        dy = diff / d
        loss_ref[...] += jnp.sum(diff * diff, axis=0, keepdims=True)
        dg_ref[...] += jnp.sum(dy * xhat, axis=0, keepdims=True)
        db_ref[...] += jnp.sum(dy, axis=0, keepdims=True)
        dxhat = dy * g_ref[...]
        dr = rstd * (dxhat - jnp.mean(dxhat, axis=1, keepdims=True)
                     - xhat * jnp.mean(dxhat * xhat, axis=1, keepdims=True))
        dr_ref[...] = dr
        drb_ref[...] = dr.astype(BF16)

    row = pl.BlockSpec((tm, d), lambda m: (m, 0))
    vec = pl.BlockSpec((1, d), lambda m: (0, 0))
    v = jax.ShapeDtypeStruct((1, d), F32)
    return pl.pallas_call(
        body, name="out_norm_loss", grid=(s // tm,),
        in_specs=[row, pl.BlockSpec((d, d), lambda m: (0, 0)), row, row, vec, vec],
        out_specs=[row, row, vec, vec, vec],
        out_shape=[jax.ShapeDtypeStruct((s, d), F32), jax.ShapeDtypeStruct((s, d), BF16), v, v, v],
        compiler_params=_params(("arbitrary",), vmem_mib=56))(merged, w_out, x, target, gamma, beta)


def _merge_bwd(drb, w_out, a, p, gpre, b_gate):
    s, d = drb.shape
    tm = _divisor_tile(s, 512, 16)
    tn = d // N_CHIPS

    def body(dr_ref, w_ref, a_ref, p_ref, ga_ref, gp_ref, ba_ref, bp_ref, da_ref, dp_ref, dga_ref, dgp_ref,
             dba_ref, dbp_ref):
        @pl.when(pl.program_id(1) == 0)
        def _():
            dba_ref[...] = jnp.zeros_like(dba_ref)
            dbp_ref[...] = jnp.zeros_like(dbp_ref)

        dm = lax.dot_general(dr_ref[...], w_ref[...], NT, preferred_element_type=F32)
        sa = _sigmoid(ga_ref[...] + ba_ref[...])
        sp = _sigmoid(gp_ref[...] + bp_ref[...])
        da_ref[...] = (dm * sa).astype(BF16)
        dp_ref[...] = (dm * sp).astype(BF16)
        dga = dm * a_ref[...] * sa * (1.0 - sa)
        dgp = dm * p_ref[...] * sp * (1.0 - sp)
        dga_ref[...] = dga.astype(BF16)
        dgp_ref[...] = dgp.astype(BF16)
        dba_ref[...] += jnp.sum(dga, axis=0, keepdims=True)
        dbp_ref[...] += jnp.sum(dgp, axis=0, keepdims=True)

    blk = pl.BlockSpec((tm, tn), lambda n, m: (m, n))
    blk1 = pl.BlockSpec((tm, tn), lambda n, m: (m, N_CHIPS + n))
    vec = pl.BlockSpec((1, tn), lambda n, m: (0, n))
    vec1 = pl.BlockSpec((1, tn), lambda n, m: (0, N_CHIPS + n))
    b16 = jax.ShapeDtypeStruct((s, d), BF16)
    v = jax.ShapeDtypeStruct((1, d), F32)
    return pl.pallas_call(
        body, name="merge_bwd", grid=(N_CHIPS, s // tm),
        in_specs=[pl.BlockSpec((tm, d), lambda n, m: (m, 0)), pl.BlockSpec((tn, d), lambda n, m: (n, 0)),
                  blk, blk, blk, blk1, vec, vec1],
        out_specs=[blk, blk, blk, blk, vec, vec], out_shape=[b16, b16, b16, b16, v, v],
        compiler_params=_params(("parallel", "arbitrary")))(drb, w_out, a, p, gpre, gpre, b_gate, b_gate)


def _proj_t(dy_ref, w_ref, tn):
    acc = None
    for n in range(N_CHIPS):
        t = lax.dot_general(dy_ref[:, n * tn:(n + 1) * tn], w_ref[n], NT, preferred_element_type=F32)
        acc = t if acc is None else acc + t
    return acc


def _attn_gate_bwd(da, wpa4, zuz, o):
    s, d = da.shape
    aw, tn = wpa4.shape[1], wpa4.shape[2]
    heads = aw // HEAD_DIM
    tm = _divisor_tile(s, 256, 16 * DILATIONS[-1])

    def body(*refs):
        da_ref, w_ref, z_ref, o_ref, dz_ref = refs[:5]
        do_refs, dd_refs, scratch = refs[5:8], refs[8:11], refs[11]
        dy = _proj_t(da_ref, w_ref, tn)
        z, o = z_ref[...], o_ref[...]
        sg = _sigmoid(z)
        do = dy * (z * sg)
        dz_ref[...] = (dy * o * _dsilu(z, sg)).astype(BF16)
        prod = do * o
        dd = jnp.concatenate(
            [jnp.broadcast_to(jnp.sum(prod[:, h * HEAD_DIM:(h + 1) * HEAD_DIM], axis=1, keepdims=True),
                              (tm, HEAD_DIM)) for h in range(heads)], axis=1)
        for g, dil in enumerate(DILATIONS):
            for r, part in enumerate(_split_rows(do, scratch, dil)):
                do_refs[g][r] = part.astype(BF16)
            for r, part in enumerate(_split_rows(dd, scratch, dil)):
                dd_refs[g][r] = part

    row = pl.BlockSpec((tm, aw), lambda m: (m, 0))
    grouped = [_grouped_spec(dil, tm, aw, lambda m: (0, m, 0)) for dil in DILATIONS]
    out = pl.pallas_call(
        body, name="attn_gate_bwd", grid=(s // tm,),
        in_specs=[pl.BlockSpec((tm, d), lambda m: (m, 0)), pl.BlockSpec((N_CHIPS, aw, tn), lambda m: (0, 0, 0)),
                  row, row],
        out_specs=[row] + grouped + grouped,
        out_shape=[jax.ShapeDtypeStruct((s, aw), BF16)]
        + [jax.ShapeDtypeStruct((dil, s // dil, aw), BF16) for dil in DILATIONS]
        + [jax.ShapeDtypeStruct((dil, s // dil, aw), F32) for dil in DILATIONS],
        scratch_shapes=[_permute_scratch(tm, aw)],
        compiler_params=_params(("parallel",)))(da, wpa4, zuz, o)
    return out[0], [t.reshape(s, aw) for t in out[1:4]], [t.reshape(s, aw) for t in out[4:7]]


def _pool_gate_bwd(dp_in, wpp4, zuz, lin, pooled, w_pool, pool_scale, aw):
    s, d = dp_in.shape
    pw, tn = wpp4.shape[1], wpp4.shape[2]
    n_win = len(POOL_WINDOWS)
    pg = pw // n_win
    tm = _divisor_tile(s, 256, 16)
    z_col = aw // pw + 1

    def body(dp_ref, w_ref, z_ref, l_ref, p_ref, wp_ref, sc_ref, dz_ref, dpo_ref, dw_ref, ds_ref):
        @pl.when(pl.program_id(0) == 0)
        def _():
            dw_ref[...] = jnp.zeros_like(dw_ref)
            ds_ref[...] = jnp.zeros_like(ds_ref)

        dy = _proj_t(dp_ref, w_ref, tn)
        z, lin_ = z_ref[...], l_ref[...]
        sg = _sigmoid(z)
        dypp = dy * (z * sg)
        dz_ref[...] = (dy * (lin_ * sc_ref[...]) * _dsilu(z, sg)).astype(BF16)
        ds_ref[...] += jnp.sum(dypp * lin_, axis=0, keepdims=True)
        dlin = (dypp * sc_ref[...]).astype(BF16)
        for gi in range(n_win):
            cs = slice(gi * pg, (gi + 1) * pg)
            pt = p_ref[:, cs].astype(F32).T.astype(BF16)
            dw_ref[gi] += jnp.dot(pt, dlin[:, cs], preferred_element_type=F32)
            dpo_ref[:, cs] = lax.dot_general(dlin[:, cs], wp_ref[gi], NT, preferred_element_type=F32)

    row = pl.BlockSpec((tm, pw), lambda m: (m, 0))
    return pl.pallas_call(
        body, name="pool_gate_bwd", grid=(s // tm,),
        in_specs=[pl.BlockSpec((tm, d), lambda m: (m, 0)), pl.BlockSpec((N_CHIPS, pw, tn), lambda m: (0, 0, 0)),
                  pl.BlockSpec((tm, pw), lambda m: (m, z_col)), row, row,
                  pl.BlockSpec((n_win, pg, pg), lambda m: (0, 0, 0)), pl.BlockSpec((1, pw), lambda m: (0, 0))],
        out_specs=[row, row, pl.BlockSpec((n_win, pg, pg), lambda m: (0, 0, 0)),
                   pl.BlockSpec((1, pw), lambda m: (0, 0))],
        out_shape=[jax.ShapeDtypeStruct((s, pw), BF16), jax.ShapeDtypeStruct((s, pw), F32),
                   jax.ShapeDtypeStruct((n_win, pg, pg), F32), jax.ShapeDtypeStruct((1, pw), F32)],
        compiler_params=_params(("arbitrary",)))(dp_in, wpp4, zuz, lin, pooled, w_pool, pool_scale)


def _pool_bwd(dpooled):
    s, pw = dpooled.shape
    pg = pw // len(POOL_WINDOWS)
    tr = _divisor_tile(s, 256, POOL_HALO)
    per = tr // POOL_HALO
    n_tiles = s // tr

    def body(c_ref, n_ref, du_ref):
        r = pl.program_id(0)
        cur = c_ref[...]
        halo = jnp.where(r < n_tiles - 1, n_ref[...], 0.0)
        ext = jnp.concatenate([cur, halo], axis=0)
        rows = tr + POOL_HALO
        for gi, window in enumerate(POOL_WINDOWS):
            cs = slice(gi * pg, (gi + 1) * pg)
            acc = ext[:, cs] / _pool_counts(r * tr, rows, window)
            shift = 1
            while shift < window:
                acc = acc + pltpu.roll(acc, rows - shift, 0)
                shift *= 2
            du_ref[:, cs] = (acc[:tr] - cur[:, cs]).astype(BF16)

    return pl.pallas_call(
        body, name="pool_bwd", grid=(n_tiles,),
        in_specs=[pl.BlockSpec((tr, pw), lambda r: (r, 0)),
                  pl.BlockSpec((POOL_HALO, pw), lambda r: (jnp.minimum((r + 1) * per, s // POOL_HALO - 1), 0))],
        out_specs=pl.BlockSpec((tr, pw), lambda r: (r, 0)),
        out_shape=jax.ShapeDtypeStruct((s, pw), BF16), compiler_params=_params(("parallel",)))(dpooled, dpooled)


def _attn_bwd(qkv, do, lse, dd, g):
    _, s, aw = qkv.shape
    heads = aw // HEAD_DIM
    n_blocks = s // STEPS
    per_seq = n_blocks // DILATIONS[g]

    def body(q_ref, do_ref, l_ref, dd_ref, kc_ref, kp_ref, vc_ref, vp_ref, out_ref, cq_ref, ck_ref, cv_ref):
        b = pl.program_id(0)

        @pl.when(b == 0)
        def _():
            cq_ref[...] = jnp.zeros_like(cq_ref)
            ck_ref[...] = jnp.zeros_like(ck_ref)
            cv_ref[...] = jnp.zeros_like(cv_ref)

        out_ref[0] = cq_ref[...].astype(BF16)

        @pl.when(b < n_blocks)
        def _():
            mask = _window_mask(lax.rem(b, per_seq) == 0)
            for h in range(heads):
                hs = slice(h * HEAD_DIM, (h + 1) * HEAD_DIM)
                q, do_ = q_ref[:, hs], do_ref[:, hs]
                kk = jnp.concatenate([kp_ref[:, hs], kc_ref[:, hs]], axis=0)
                vv = jnp.concatenate([vp_ref[:, hs], vc_ref[:, hs]], axis=0)
                lse_ = jnp.concatenate([l_ref[:, hs], l_ref[:, hs]], axis=1)
                dd_ = jnp.concatenate([dd_ref[:, hs], dd_ref[:, hs]], axis=1)
                sc = lax.dot_general(q, kk, NT, preferred_element_type=F32) * SCORE_SCALE
                prob = jnp.where(mask, jnp.exp(sc - lse_), 0.0)
                dprob = lax.dot_general(do_, vv, NT, preferred_element_type=F32)
                dsc = prob * (dprob - dd_) * SCORE_SCALE
                cq_ref[:, hs] = jnp.dot(dsc.astype(BF16), kk, preferred_element_type=F32)
                dkk = jnp.dot(dsc.T.astype(BF16), q, preferred_element_type=F32)
                dvv = jnp.dot(prob.T.astype(BF16), do_, preferred_element_type=F32)
                out_ref[1, :, hs] = (ck_ref[:, hs] + dkk[:STEPS]).astype(BF16)
                out_ref[2, :, hs] = (cv_ref[:, hs] + dvv[:STEPS]).astype(BF16)
                ck_ref[:, hs] = dkk[STEPS:]
                cv_ref[:, hs] = dvv[STEPS:]

        @pl.when(b == n_blocks)
        def _():
            out_ref[1] = ck_ref[...].astype(BF16)
            out_ref[2] = cv_ref[...].astype(BF16)

    last = n_blocks - 1

    def cur(which):
        return pl.BlockSpec((None, STEPS, aw), lambda b: (which, jnp.minimum(b, last), 0))

    def prev(which):
        return pl.BlockSpec((None, STEPS, aw), lambda b: (which, jnp.clip(b - 1, 0, last), 0))

    row = pl.BlockSpec((STEPS, aw), lambda b: (jnp.minimum(b, last), 0))
    return pl.pallas_call(
        body, name=f"attn_bwd{g}", grid=(n_blocks + 1,),
        in_specs=[cur(0), row, row, row, cur(1), prev(1), cur(2), prev(2)],
        out_specs=pl.BlockSpec((3, STEPS, aw), lambda b: (0, jnp.clip(b - 1, 0, last), 0)),
        out_shape=jax.ShapeDtypeStruct((3, s, aw), BF16),
        scratch_shapes=[pltpu.VMEM((STEPS, aw), F32)] * 3,
        compiler_params=_params(("arbitrary",)))(qkv, do, lse, dd, qkv, qkv, qkv, qkv)


def _weight_grad(at, b, tn, col_blocks, name):
    m, k = at.shape
    n = b.shape[1]
    tm = _divisor_tile(m, 1024, 16)
    tk = _divisor_tile(k, 2048, 128)
    nk = k // tk

    def body(a_ref, b_ref, o_ref, acc_ref):
        kk = pl.program_id(2)

        @pl.when(kk == 0)
        def _():
            acc_ref[...] = jnp.zeros_like(acc_ref)

        acc_ref[...] += jnp.dot(a_ref[...], b_ref[...], preferred_element_type=F32)

        @pl.when(kk == nk - 1)
        def _():
            o_ref[...] = acc_ref[...].astype(BF16)

    if col_blocks:
        out_spec = pl.BlockSpec((None, tm, tn), lambda i, j, kk: (j, i, 0))
        out_shape = jax.ShapeDtypeStruct((n // tn, m, tn), BF16)
    else:
        out_spec = pl.BlockSpec((tm, tn), lambda i, j, kk: (i, j))
        out_shape = jax.ShapeDtypeStruct((m, n), BF16)
    return pl.pallas_call(
        body, name=name, grid=(m // tm, n // tn, nk),
        in_specs=[pl.BlockSpec((tm, tk), lambda i, j, kk: (i, kk)), pl.BlockSpec((tk, tn), lambda i, j, kk: (kk, j))],
        out_specs=out_spec, out_shape=out_shape, scratch_shapes=[pltpu.VMEM((tm, tn), F32)],
        compiler_params=_params(("parallel", "parallel", "arbitrary")))(at, b)


def _w_in_grad_part(xt, b, col_of, n_local, tn, w_shape, prev, name):
    d, s = xt.shape
    per_chip = w_shape[2] // tn
    tm = _divisor_tile(d, 1024, 16)
    tk = _divisor_tile(s, 2048, 128)
    nk = s // tk

    def body(*refs):
        a_ref, b_ref, o_ref, acc_ref = refs[0], refs[1], refs[-2], refs[-1]
        kk = pl.program_id(2)

        @pl.when(kk == 0)
        def _():
            acc_ref[...] = jnp.zeros_like(acc_ref)

        acc_ref[...] += jnp.dot(a_ref[...], b_ref[...], preferred_element_type=F32)

        @pl.when(kk == nk - 1)
        def _():
            o_ref[...] = acc_ref[...].astype(BF16)

    if b.ndim == 3:
        sub = b.shape[2] // tn
        b_spec = pl.BlockSpec((None, tk, tn), lambda j, i, kk: (j // sub, kk, j % sub))
    else:
        b_spec = pl.BlockSpec((tk, tn), lambda j, i, kk: (kk, j))
    in_specs = [pl.BlockSpec((tm, tk), lambda j, i, kk: (i, kk)), b_spec]
    args = [xt, b]
    aliases = {}
    if prev is not None:
        in_specs.append(ANY)
        args.append(prev)
        aliases = {2: 0}
    return pl.pallas_call(
        body, name=name, grid=(n_local, d // tm, nk), in_specs=in_specs,
        out_specs=pl.BlockSpec((None, tm, tn), lambda j, i, kk: (col_of(j) // per_chip, i, col_of(j) % per_chip)),
        out_shape=jax.ShapeDtypeStruct(w_shape, BF16), scratch_shapes=[pltpu.VMEM((tm, tn), F32)],
        input_output_aliases=aliases,
        compiler_params=_params(("parallel", "parallel", "arbitrary")))(*args)


def _x_grad(dqkv, rest, w4, dr, aw, tn, after=None):
    s, d = dr.shape
    sub = aw // tn
    n_qkv = 3 * N_GROUPS * sub
    los, lo = [], n_qkv
    for p in rest:
        los.append(lo)
        lo += p.shape[1] // tn
    n_blocks = lo
    per_chip = n_blocks // N_CHIPS
    tm = _divisor_tile(s, 512, 16 * DILATIONS[-1])

    ordered = [] if after is None else [after]

    def body(*refs):
        refs = refs[len(ordered):]
        q_refs, r_refs = refs[:N_GROUPS], refs[N_GROUPS:N_GROUPS + len(rest)]
        w_ref, dr_ref, o_ref, acc_ref, scratch = refs[-5:]
        j = pl.program_id(1)

        @pl.when(j == 0)
        def _():
            acc_ref[...] = ALPHA * dr_ref[...]

        for g, dil in enumerate(DILATIONS):
            @pl.when((j < n_qkv) & (lax.rem(j // sub, N_GROUPS) == g))
            def _(g=g, dil=dil):
                rows = _merge_rows(q_refs[g], scratch, dil).astype(BF16)
                acc_ref[...] += lax.dot_general(rows, w_ref[...], NT, preferred_element_type=F32)

        for p_ref, lo_, piece in zip(r_refs, los, rest):
            @pl.when((j >= lo_) & (j < lo_ + piece.shape[1] // tn))
            def _(p_ref=p_ref):
                acc_ref[...] += lax.dot_general(p_ref[...], w_ref[...], NT, preferred_element_type=F32)

        @pl.when(j == n_blocks - 1)
        def _():
            o_ref[...] = acc_ref[...]

    def qkv_spec(dil):
        def index(i, j):
            region = jnp.minimum(j // sub, 3 * N_GROUPS - 1)
            return region // N_GROUPS, 0, i, jnp.where(j < n_qkv, j % sub, 0)

        return pl.BlockSpec((None, dil, tm // dil, tn), index)

    def rest_spec(lo_, piece):
        n = piece.shape[1] // tn
        return pl.BlockSpec((tm, tn), lambda i, j: (i, jnp.clip(j - lo_, 0, n - 1)))

    row = pl.BlockSpec((tm, d), lambda i, j: (i, 0))
    return pl.pallas_call(
        body, name="x_grad", grid=(s // tm, n_blocks),
        in_specs=[pl.BlockSpec(t.shape, lambda i, j: (0, 0)) for t in ordered]
        + [qkv_spec(dil) for dil in DILATIONS] + [rest_spec(lo_, p) for lo_, p in zip(los, rest)]
        + [pl.BlockSpec((None, d, tn), lambda i, j: (j // per_chip, 0, j % per_chip)), row],
        out_specs=row, out_shape=jax.ShapeDtypeStruct((s, d), F32),
        scratch_shapes=[pltpu.VMEM((tm, d), F32), _permute_scratch(tm, tn)],
        compiler_params=_params(("parallel", "arbitrary"), vmem_mib=56))(
            *ordered, *[t.reshape(3, dil, s // dil, aw) for t, dil in zip(dqkv, DILATIONS)], *rest, w4, dr)


def _to_subsequences(t, dilation):
    s, w = t.shape
    return t.reshape(s // dilation, dilation, w).transpose(1, 0, 2).reshape(s, w)


def _prepare_x(x):
    s, d = x.shape
    tc = 2 * LANES
    slabs = tc // LANES

    def body(x_ref, xb_ref, *refs):
        xt_refs, scratch = refs[:N_GROUPS], refs[-1]
        t = x_ref[...]
        xb_ref[...] = t.astype(BF16)
        for c in range(slabs):
            scratch[c] = t[:, c * LANES:(c + 1) * LANES]
        for g, dil in enumerate(DILATIONS):
            length = s // dil
            for r in range(dil):
                part = t if dil == 1 else jnp.concatenate(
                    [scratch[c, pl.ds(r, length, stride=dil), :] for c in range(slabs)], axis=1)
                xt_refs[g][:, r * length:(r + 1) * length] = part.T.astype(BF16)

    col = pl.BlockSpec((s, tc), lambda j: (0, j))
    row = pl.BlockSpec((tc, s), lambda j: (j, 0))
    t_shape = jax.ShapeDtypeStruct((d, s), BF16)
    out = pl.pallas_call(
        body, name="prepare_x", grid=(d // tc,), in_specs=[col], out_specs=[col] + [row] * N_GROUPS,
        out_shape=[jax.ShapeDtypeStruct((s, d), BF16)] + [t_shape] * N_GROUPS,
        scratch_shapes=[_permute_scratch(s, tc)], compiler_params=_params(("parallel",)))(x)
    return out[0], out[1:]


def _local_step(x, target, w_in4, b_gate, pool_scale, gamma, beta, aw, pw, small_weights, start_exchange=None,
                after=None):
    s, d = x.shape
    tn = _col_tile(aw, pw, w_in4.shape[2])
    sub = aw // tn
    qkv_w = 3 * N_GROUPS * aw

    xb, xts = _prepare_x(x)
    qkv = [_in_proj_qkv(xb, w_in4, g, aw, tn, after if g == 0 else None) for g in range(N_GROUPS)]
    zuz = _in_proj(xb, w_in4, qkv_w, aw + 2 * pw, tn, F32, "in_proj_zuz")
    gpre = _in_proj(xb, w_in4, qkv_w + aw + 2 * pw, 2 * d, tn, F32, "in_proj_gates")

    attn = [_attn_fwd(qkv[g], g) for g in range(N_GROUPS)]
    o, y_attn, y_attn_t, lse = _combine_groups([a[0] for a in attn], [a[1] for a in attn], zuz, aw)
    w_pool, wpa4, wpp4, w_out = small_weights(o)
    pooled, lin, y_pool, y_pool_t = _pool_fwd(zuz, w_pool, pool_scale, aw, pw)
    a, p, merged, merged_t = _proj_merge(y_attn, y_pool, wpa4, wpp4, gpre, b_gate)
    dr, drb, loss_lanes, d_gamma, d_beta = _out_norm_loss(merged, w_out, x, target, gamma, beta)

    da, dp, d_gpre_a, d_gpre_p, d_b_a, d_b_p = _merge_bwd(drb, w_out, a, p, gpre, b_gate)
    d_b_gate = jnp.concatenate([d_b_a, d_b_p], axis=1)
    d_w_out = _weight_grad(merged_t, drb, d // N_CHIPS, False, "w_out_grad")
    d_wpa4 = _weight_grad(y_attn_t, da, d // N_CHIPS, True, "w_proj_attn_grad")
    d_wpp4 = _weight_grad(y_pool_t, dp, d // N_CHIPS, True, "w_proj_pool_grad")
    d_z_attn, d_o, dd = _attn_gate_bwd(da, wpa4, zuz, o)
    d_z_pool, d_pooled, d_w_pool, d_pool_scale = _pool_gate_bwd(dp, wpp4, zuz, lin, pooled, w_pool, pool_scale, aw)
    d_u = _pool_bwd(d_pooled)
    dqkv = [_attn_bwd(qkv[g], d_o[g], lse[g], dd[g], g) for g in range(N_GROUPS)]

    rest = [d_z_attn, d_u, d_z_pool, d_gpre_a, d_gpre_p]
    d_w_in4 = None
    for g in range(N_GROUPS):
        d_w_in4 = _w_in_grad_part(xts[g], dqkv[g], lambda j, g=g: ((j // sub) * N_GROUPS + g) * sub + j % sub,
                                  3 * sub, tn, w_in4.shape, d_w_in4, f"w_in_grad_qkv{g}")
    lo = qkv_w // tn
    for i, piece in enumerate(rest):
        n_local = piece.shape[1] // tn
        d_w_in4 = _w_in_grad_part(xts[0], piece, lambda j, lo=lo: lo + j, n_local, tn, w_in4.shape, d_w_in4,
                                  f"w_in_grad_rest{i}")
        lo += n_local
    grads = dict(loss_lanes=loss_lanes, w_in=d_w_in4, b_gate=d_b_gate, w_pool=d_w_pool,
                 pool_scale=d_pool_scale, w_proj_attn=d_wpa4, w_proj_pool=d_wpp4, w_out=d_w_out,
                 ln_gamma=d_gamma, ln_beta=d_beta)
    token = None if start_exchange is None else start_exchange(grads)
    grads["d_x"] = _x_grad(dqkv, rest, w_in4, dr, aw, tn, token)
    return grads


def _pack_small(wpa, wpp, w_out, w_pool):
    width = wpa.shape[1]
    return jnp.concatenate([wpa, wpp, w_out.reshape(-1, width), w_pool.reshape(-1, width)], axis=0)


def _unpack_small(packed, aw, pw, d, pg):
    lead = packed.shape[:-2]
    width = d // N_CHIPS
    r0, r1, r2 = aw, aw + pw, aw + pw + d
    return (packed[..., :r0, :], packed[..., r0:r1, :], packed[..., r1:r2, :].reshape(lead + (width, d)),
            packed[..., r2:, :].reshape(lead + (len(POOL_WINDOWS), pg // N_CHIPS, pg)))


def _pack_rows(vectors, rows):
    flat = jnp.concatenate([v.reshape(-1) for v in vectors])
    return jnp.pad(flat, (0, rows * 128 - flat.shape[0])).reshape(rows, 128)


def _unpack_rows(packed, sizes):
    flat, out, lo = packed.reshape(-1), [], 0
    for n in sizes:
        out.append(flat[lo:lo + n].reshape(1, n))
        lo += n
    return out


def kernel(x, w_in, b_gate, w_pool, pool_scale, w_proj_attn, w_proj_pool, w_out, ln_gamma, ln_beta, loss_target, m_w_in, m_b_gate, m_w_pool, m_pool_scale, m_w_proj_attn, m_w_proj_pool, m_w_out, m_ln_gamma, m_ln_beta, v_w_in, v_b_gate, v_w_pool, v_pool_scale, v_w_proj_attn, v_w_proj_pool, v_w_out, v_ln_gamma, v_ln_beta):
    s, d = x.shape[1], x.shape[2]
    aw, pw = w_proj_attn.shape[1], w_proj_pool.shape[1]
    pg = w_pool.shape[3]
    n_win = len(POOL_WINDOWS)

    def small(wpa, wpp, wo, wpl):
        return _pack_small(wpa[0], wpp[0], wo[0], wpl[0])

    chip = 2 * lax.axis_index("x") + lax.axis_index("y")
    core = lax.axis_index("c")

    w_small = small(w_proj_attn, w_proj_pool, w_out, w_pool)
    w_in4, = _gather_weights([_place_block(w_in[0], N_CHIPS, chip, BF16, "place_w_in")])
    small_sems, small_thru, small_token = _broadcast_start(
        _place_block(w_small, N_CHIPS, chip, BF16, "place_w_small"), "gather_small_start")

    def small_weights(after):
        small4 = _broadcast_wait(small_sems, small_thru, after, "gather_small_wait")
        wpa4, wpp4, w_out4, w_pool4 = _unpack_small(small4, aw, pw, d, pg)
        return w_pool4.transpose(1, 0, 2, 3).reshape(n_win, pg, pg), wpa4, wpp4, w_out4.reshape(d, d)

    exchange = {}

    def start_exchange(g):
        g_pool4 = g["w_pool"].reshape(n_win, N_CHIPS, pg // N_CHIPS, pg).transpose(1, 0, 2, 3).astype(BF16)
        g_out4 = g["w_out"].reshape(N_CHIPS, d // N_CHIPS, d)
        g_small4 = jnp.concatenate([g["w_proj_attn"], g["w_proj_pool"], g_out4.reshape(N_CHIPS, -1, d // N_CHIPS),
                                    g_pool4.reshape(N_CHIPS, -1, d // N_CHIPS)], axis=1)
        theirs_big, theirs_small = _swap_halves([g["w_in"], g_small4])
        chip_big, placed_big = _add_halves(g["w_in"], theirs_big, core, chip, "add_cores_big")
        chip_small, placed_small = _add_halves(g_small4, theirs_small, core, chip, "add_cores_small")
        sems, sums, placed, token = _scatter_start([chip_big, chip_small], [placed_big, placed_small])
        exchange.update(sems=sems, sums=sums, placed=placed)
        return token

    g = _local_step(x[0], loss_target[0], w_in4, b_gate, pool_scale, ln_gamma, ln_beta, aw, pw, small_weights,
                    start_exchange, small_token)
    got_big, got_small = _scatter_wait(exchange["sems"], exchange["sums"], exchange["placed"], g["d_x"])
    grad_w_in, grad_small = _join_halves([_sum_slots(got_big, core, "sum_chips_big"),
                                          _sum_slots(got_small, core, "sum_chips_small")])
    grad_w_in = grad_w_in.reshape(-1, grad_w_in.shape[2])
    grad_small = grad_small.reshape(-1, grad_small.shape[2])

    sizes = [b_gate.shape[1], pool_scale.shape[1], d, d, 1]
    rows = -(-sum(sizes) // (8 * 128)) * 8
    loss_part = (0.5 / d) * jnp.sum(g["loss_lanes"]).reshape(1, 1)
    parts = _gather_rows(_pack_rows([g["b_gate"], g["pool_scale"], g["ln_gamma"], g["ln_beta"], loss_part], rows))
    zero = jnp.zeros((1, 1), F32)
    packed = [_pack_rows(vs, rows) for vs in ([b_gate, pool_scale, ln_gamma, ln_beta, zero],
                                              [m_b_gate, m_pool_scale, m_ln_gamma, m_ln_beta, zero],
                                              [v_b_gate, v_pool_scale, v_ln_gamma, v_ln_beta, zero])]
    rep = [_unpack_rows(t, sizes) for t in _sum_rows_adamw(parts, *packed)]
    loss = rep[0][4].reshape(())

    upd_in = _adamw(w_in[0], grad_w_in, m_w_in[0], v_w_in[0], "adamw_w_in")
    upd_small = _adamw(w_small, grad_small, small(m_w_proj_attn, m_w_proj_pool, m_w_out, m_w_pool),
                       small(v_w_proj_attn, v_w_proj_pool, v_w_out, v_w_pool), "adamw_small")

    def leaves(big, packed_small, replicated):
        wpa_, wpp_, wo_, wpl_ = _unpack_small(packed_small, aw, pw, d, pg)
        return [big[None], replicated[0], wpl_[None], replicated[1], wpa_[None], wpp_[None], wo_[None],
                replicated[2], replicated[3]]

    out = [loss, g["d_x"][None]]
    out += leaves(grad_w_in, grad_small, rep[0])
    for i in range(3):
        out += leaves(upd_in[i], upd_small[i], rep[1 + i])
    return tuple(out)
```

```python
import math

import jax
import jax.numpy as jnp
from jax import lax
from jax.experimental import pallas as pl
from jax.experimental.pallas import tpu as pltpu

F32 = jnp.float32
BF16 = jnp.bfloat16
MESH = pl.DeviceIdType.MESH
ANY = pl.BlockSpec(memory_space=pl.ANY)

HEAD_DIM = 128
STEPS = 128
DILATIONS = (1, 4, 16)
N_GROUPS = len(DILATIONS)
POOL_WINDOWS = (2, 4, 8, 16)
POOL_HALO = 16
N_CHIPS = 4
N_DEV = 8
ALPHA = 2.0 ** 0.25
LN_EPS = 1e-5
NEG_INF = -1e30
SCORE_SCALE = HEAD_DIM ** -0.5
ADAM_LR = 0.001
ADAM_B1 = 0.9
ADAM_B2 = 0.999
ADAM_EPS = 1e-08
ADAM_WD = 0.01
ADAM_STEP = 10
MIB = 2 ** 20
NT = (((1,), (1,)), ((), ()))
DMA_STREAMS = 8


def _params(semantics=None, vmem_mib=48):
    return pltpu.CompilerParams(dimension_semantics=semantics, vmem_limit_bytes=vmem_mib * MIB)


def _divisor_tile(n, target, multiple):
    best = None
    for t in range(multiple, min(n, target) + 1, multiple):
        if n % t == 0:
            best = t
    assert best is not None, (n, target, multiple)
    return best


def _col_tile(*widths):
    g = 0
    for w in widths:
        g = math.gcd(g, w)
    return _divisor_tile(g, 1024, 128)


def _sigmoid(z):
    return jax.nn.sigmoid(z)


def _dsilu(z, sg):
    return sg * (1.0 + z * (1.0 - sg))


def _place():
    x, y, c = lax.axis_index("x"), lax.axis_index("y"), lax.axis_index("c")
    others = [(1 - x, y), (x, 1 - y), (1 - x, 1 - y)]
    return x, y, c, (x, y, 1 - c), others


def _remote(src, dst, send_sem, recv_sem, dev):
    return pltpu.make_async_remote_copy(src_ref=src, dst_ref=dst, send_sem=send_sem, recv_sem=recv_sem,
                                        device_id=dev, device_id_type=MESH)


def _row_pieces(n_rows, streams=DMA_STREAMS, multiple=16):
    size = -(-n_rows // (streams * multiple)) * multiple
    return [(lo, min(size, n_rows - lo)) for lo in range(0, n_rows, size)]


def _start_streams(make, n_rows):
    for lo, size in _row_pieces(n_rows):
        make(pl.ds(lo, size)).start()


def _gather_weights(placed):
    n = len(placed)

    def body(*refs):
        dst = refs[n:2 * n]
        send_sems, recv_sems = refs[2 * n:]
        x, y, c, sibling, others = _place()
        me = 2 * x + y
        sent = []
        for i in range(n):
            half = dst[i].shape[1] // 2
            mine = c * half
            for j, (ox, oy) in enumerate(others):
                slab = dst[i].at[me, pl.ds(mine, half)]
                cp = _remote(slab, slab, send_sems.at[6 * i + j], recv_sems.at[6 * i + j], (ox, oy, c))
                cp.start()
                sent.append(cp)
        for i in range(n):
            half = dst[i].shape[1] // 2
            mine = c * half
            for j, (ox, oy) in enumerate(others):
                blk = dst[i].at[2 * ox + oy]
                slab = blk.at[pl.ds(mine, half)]
                _remote(slab, slab, send_sems.at[6 * i + j], recv_sems.at[6 * i + j], (ox, oy, c)).wait_recv()
                k = 6 * i + 3 + j
                _start_streams(lambda r, blk=blk, k=k: _remote(blk.at[pl.ds(mine + r.start, r.size)],
                                                               blk.at[pl.ds(mine + r.start, r.size)],
                                                               send_sems.at[k], recv_sems.at[k], sibling), half)
                sent.append(_remote(slab, slab, send_sems.at[k], recv_sems.at[k], sibling))
        for i in range(n):
            half = dst[i].shape[1] // 2
            for j, (ox, oy) in enumerate(others):
                slab = dst[i].at[2 * ox + oy, pl.ds((1 - c) * half, half)]
                _remote(slab, slab, send_sems.at[6 * i + 3 + j], recv_sems.at[6 * i + 3 + j], sibling).wait_recv()
        for cp in sent:
            cp.wait_send()

    return pl.pallas_call(
        body, name="gather_weights", out_shape=[jax.ShapeDtypeStruct(s.shape, s.dtype) for s in placed],
        in_specs=[ANY] * n, out_specs=[ANY] * n, input_output_aliases={i: i for i in range(n)},
        scratch_shapes=[pltpu.SemaphoreType.DMA((6 * n,)), pltpu.SemaphoreType.DMA((6 * n,))],
    )(*placed)


def _half_copies(buf, send_sems, recv_sems):
    x, y, c, _, others = _place()
    half = buf.shape[1] // 2
    slab = buf.at[2 * x + y, pl.ds(c * half, half)]
    return [_remote(slab, slab, send_sems[j], recv_sems[j], (ox, oy, c)) for j, (ox, oy) in enumerate(others)]


def _halves_start(placed, after, name):
    k = N_CHIPS - 1

    def body(buf, after_ref, *refs):
        send_sems, recv_sems, token = refs[:k], refs[k:2 * k], refs[-1]
        for cp in _half_copies(buf, send_sems, recv_sems):
            cp.start()
        token[...] = jnp.zeros_like(token)

    out = pl.pallas_call(
        body, name=name,
        out_shape=[pltpu.SemaphoreType.DMA(())] * (2 * k) + [pltpu.HBM(placed.shape, placed.dtype),
                                                             jax.ShapeDtypeStruct((8, 128), F32)],
        in_specs=[HBM, ANY], out_specs=[SEM] * (2 * k) + [HBM, pl.BlockSpec(memory_space=pltpu.VMEM)],
        input_output_aliases={0: 2 * k},
        compiler_params=pltpu.CompilerParams(has_side_effects=DATAFLOW),
    )(pltpu.with_memory_space_constraint(placed, pltpu.HBM), after)
    return out[:2 * k], out[2 * k], out[-1]


def _halves_wait(sems, placed, after, name):
    k = N_CHIPS - 1

    def body(buf, *refs):
        send_sems, recv_sems = refs[:k], refs[k:2 * k]
        for cp in _half_copies(buf, send_sems, recv_sems):
            cp.wait_send()
            cp.wait_recv()

    return pl.pallas_call(
        body, name=name, out_shape=pltpu.HBM(placed.shape, placed.dtype),
        in_specs=[HBM] + [SEM] * (2 * k) + [ANY], out_specs=HBM, input_output_aliases={0: 0},
        compiler_params=pltpu.CompilerParams(has_side_effects=DATAFLOW),
    )(placed, *sems, after)


def _forward_halves(buf, name):
    def body(_, dst, send_sems, recv_sems):
        x, y, c, sibling, others = _place()
        half = dst.shape[1] // 2
        for j, (ox, oy) in enumerate(others):
            slab = dst.at[2 * ox + oy, pl.ds(c * half, half)]
            _remote(slab, slab, send_sems.at[j], recv_sems.at[j], sibling).start()
        for j, (ox, oy) in enumerate(others):
            mine = dst.at[2 * ox + oy, pl.ds(c * half, half)]
            theirs = dst.at[2 * ox + oy, pl.ds((1 - c) * half, half)]
            cp = _remote(mine, theirs, send_sems.at[j], recv_sems.at[j], sibling)
            cp.wait_recv()
            cp.wait_send()

    return pl.pallas_call(
        body, name=name, out_shape=jax.ShapeDtypeStruct(buf.shape, buf.dtype),
        in_specs=[ANY], out_specs=ANY, input_output_aliases={0: 0},
        scratch_shapes=[pltpu.SemaphoreType.DMA((N_CHIPS - 1,)), pltpu.SemaphoreType.DMA((N_CHIPS - 1,))],
    )(buf)


def _swap_halves(grads):
    n = len(grads)

    def body(*refs):
        g, theirs = refs[:n], refs[n:2 * n]
        send_sems, recv_sems = refs[2 * n:]
        x, y, c, sibling, _ = _place()
        for i in range(n):
            half = g[i].shape[1] // 2
            give = (1 - c) * half
            for b in range(N_CHIPS):
                _start_streams(lambda r, i=i, b=b: _remote(
                    g[i].at[b, pl.ds(give + r.start, r.size)], theirs[i].at[b, r], send_sems.at[i], recv_sems.at[i],
                    sibling), half)
        for i in range(n):
            _remote(theirs[i], theirs[i], send_sems.at[i], recv_sems.at[i], sibling).wait()

    return pl.pallas_call(
        body, name="swap_halves",
        out_shape=[jax.ShapeDtypeStruct((s.shape[0], s.shape[1] // 2) + s.shape[2:], s.dtype) for s in grads],
        in_specs=[ANY] * n, out_specs=[ANY] * n,
        scratch_shapes=[pltpu.SemaphoreType.DMA((n,)), pltpu.SemaphoreType.DMA((n,))],
    )(*grads)


HBM = pl.BlockSpec(memory_space=pltpu.HBM)
SEM = pl.BlockSpec(memory_space=pltpu.SEMAPHORE)
DATAFLOW = pltpu.SideEffectType.DATAFLOW_SIDE_EFFECTING


def _broadcast_copies(buf, send_sems, recv_sems):
    x, y, c, _, others = _place()
    mine = buf.at[2 * x + y]
    return [_remote(mine, mine, send_sems[j], recv_sems[j], (ox, oy, c)) for j, (ox, oy) in enumerate(others)]


def _broadcast_start(placed, after, name):
    k = N_CHIPS - 1

    def body(buf, after_ref, *refs):
        send_sems, recv_sems, token = refs[:k], refs[k:2 * k], refs[-1]
        for cp in _broadcast_copies(buf, send_sems, recv_sems):
            cp.start()
        token[...] = jnp.zeros_like(token)

    out = pl.pallas_call(
        body, name=name,
        out_shape=[pltpu.SemaphoreType.DMA(())] * (2 * k) + [pltpu.HBM(placed.shape, placed.dtype),
                                                             jax.ShapeDtypeStruct((8, 128), F32)],
        in_specs=[HBM, ANY], out_specs=[SEM] * (2 * k) + [HBM, pl.BlockSpec(memory_space=pltpu.VMEM)],
        input_output_aliases={0: 2 * k},
        compiler_params=pltpu.CompilerParams(has_side_effects=DATAFLOW),
    )(pltpu.with_memory_space_constraint(placed, pltpu.HBM), after)
    return out[:2 * k], out[2 * k], out[-1]


def _broadcast_wait(sems, placed, after, name):
    k = N_CHIPS - 1

    def body(buf, *refs):
        send_sems, recv_sems = refs[:k], refs[k:2 * k]
        for cp in _broadcast_copies(buf, send_sems, recv_sems):
            cp.wait_send()
            cp.wait_recv()

    return pl.pallas_call(
        body, name=name, out_shape=pltpu.HBM(placed.shape, placed.dtype),
        in_specs=[HBM] + [SEM] * (2 * k) + [ANY], out_specs=HBM, input_output_aliases={0: 0},
        compiler_params=pltpu.CompilerParams(has_side_effects=DATAFLOW),
    )(placed, *sems, after)


def _scatter_copies(s, got, send_sems, recv_sems):
    x, y, c, _, others = _place()
    me = 2 * x + y
    n = len(s)
    return [_remote(s[i].at[2 * ox + oy], got[i].at[me], send_sems[3 * i + j], recv_sems[3 * i + j], (ox, oy, c))
            for i in range(n) for j, (ox, oy) in enumerate(others)]


def _scatter_start(sums, placed):
    n = len(sums)
    k = 3 * n

    def body(*refs):
        s, got, token = refs[:n], refs[n:2 * n], refs[-1]
        send_sems, recv_sems = refs[2 * n:2 * n + k], refs[2 * n + k:2 * n + 2 * k]
        for cp in _scatter_copies(s, got, send_sems, recv_sems):
            cp.start()
        token[...] = jnp.zeros_like(token)

    hbm = [pltpu.HBM(a.shape, a.dtype) for a in list(sums) + list(placed)]
    out = pl.pallas_call(
        body, name="scatter_start",
        out_shape=[pltpu.SemaphoreType.DMA(())] * (2 * k) + hbm + [jax.ShapeDtypeStruct((8, 128), F32)],
        in_specs=[HBM] * (2 * n), out_specs=[SEM] * (2 * k) + [HBM] * (2 * n) + [pl.BlockSpec(memory_space=pltpu.VMEM)],
        input_output_aliases={i: 2 * k + i for i in range(2 * n)},
        compiler_params=pltpu.CompilerParams(has_side_effects=DATAFLOW),
    )(*[pltpu.with_memory_space_constraint(a, pltpu.HBM) for a in list(sums) + list(placed)])
    return out[:2 * k], out[2 * k:2 * k + n], out[2 * k + n:2 * k + 2 * n], out[-1]


def _scatter_wait(sems, sums, placed, after):
    n = len(sums)
    k = 3 * n

    def body(*refs):
        s, got = refs[:n], refs[n:2 * n]
        send_sems, recv_sems = refs[2 * n:2 * n + k], refs[2 * n + k:2 * n + 2 * k]
        for cp in _scatter_copies(s, got, send_sems, recv_sems):
            cp.wait_send()
            cp.wait_recv()

    hbm = [pltpu.HBM(a.shape, a.dtype) for a in list(sums) + list(placed)]
    out = pl.pallas_call(
        body, name="scatter_wait", out_shape=hbm,
        in_specs=[HBM] * (2 * n) + [SEM] * (2 * k) + [ANY], out_specs=[HBM] * (2 * n),
        input_output_aliases={i: i for i in range(2 * n)},
        compiler_params=pltpu.CompilerParams(has_side_effects=DATAFLOW),
    )(*sums, *placed, *sems, after)
    return out[n:]


def _join_halves(placed):
    n = len(placed)

    def body(*refs):
        full = refs[n:2 * n]
        send_sems, recv_sems = refs[2 * n:]
        x, y, c, sibling, _ = _place()
        for i in range(n):
            _start_streams(lambda r, i=i: _remote(full[i].at[c, r], full[i].at[c, r], send_sems.at[i],
                                                  recv_sems.at[i], sibling), full[i].shape[1])
        for i in range(n):
            cp = _remote(full[i].at[c], full[i].at[1 - c], send_sems.at[i], recv_sems.at[i], sibling)
            cp.wait_recv()
            cp.wait_send()

    return pl.pallas_call(
        body, name="join_halves", out_shape=[jax.ShapeDtypeStruct(s.shape, s.dtype) for s in placed],
        in_specs=[ANY] * n, out_specs=[ANY] * n, input_output_aliases={i: i for i in range(n)},
        scratch_shapes=[pltpu.SemaphoreType.DMA((n,)), pltpu.SemaphoreType.DMA((n,))],
    )(*placed)


def _gather_rows(row):
    def body(row_ref, out_ref, send_sems, recv_sems, local_sem):
        x, y, c = lax.axis_index("x"), lax.axis_index("y"), lax.axis_index("c")
        me = 4 * x + 2 * y + c
        local = pltpu.make_async_copy(row_ref, out_ref.at[me], local_sem)
        local.start()
        sent = []
        peers = []
        for k in range(1, N_DEV):
            px, py, pc = x ^ (k >> 2), y ^ ((k >> 1) & 1), c ^ (k & 1)
            peers.append((k, px, py, pc))
            cp = _remote(row_ref, out_ref.at[me], send_sems.at[k - 1], recv_sems.at[k - 1], (px, py, pc))
            cp.start()
            sent.append(cp)
        for k, px, py, pc in peers:
            slot = out_ref.at[4 * px + 2 * py + pc]
            _remote(slot, slot, send_sems.at[k - 1], recv_sems.at[k - 1], (px, py, pc)).wait_recv()
        for cp in sent:
            cp.wait_send()
        local.wait()

    return pl.pallas_call(
        body, name="gather_rows", out_shape=jax.ShapeDtypeStruct((N_DEV,) + row.shape, row.dtype),
        in_specs=[ANY], out_specs=ANY,
        scratch_shapes=[pltpu.SemaphoreType.DMA((N_DEV - 1,)), pltpu.SemaphoreType.DMA((N_DEV - 1,)),
                        pltpu.SemaphoreType.DMA],
    )(row)


def _scalar(i):
    return jnp.reshape(i, (1,)).astype(jnp.int32)


def _place_block(src, n_slots, slot, out_dtype, name, window=0, n_windows=1):
    rows, cols = src.shape[0], src.shape[1] // n_windows
    tr = _divisor_tile(rows, max(16, (2 * MIB) // (cols * 4)), 16)

    def body(slot_ref, s_ref, o_ref):
        o_ref[...] = s_ref[...].astype(o_ref.dtype)

    return pl.pallas_call(
        body, name=name, out_shape=jax.ShapeDtypeStruct((n_slots, rows, cols), out_dtype),
        grid_spec=pltpu.PrefetchScalarGridSpec(
            num_scalar_prefetch=1, grid=(rows // tr,), in_specs=[pl.BlockSpec((tr, cols), lambda r, sl: (r, window))],
            out_specs=pl.BlockSpec((None, tr, cols), lambda r, sl: (sl[0], r, 0))),
        compiler_params=_params(("parallel",)))(_scalar(slot), src)


def _add_halves(g, theirs, core, chip, name):
    n, half, cols = theirs.shape
    tr = _divisor_tile(half, max(16, (2 * MIB) // (cols * 4)), 16)
    per = half // tr

    def body(at_ref, a_ref, b_ref, o_ref, own_ref):
        total = (a_ref[...].astype(F32) + b_ref[...].astype(F32)).astype(o_ref.dtype)
        o_ref[...] = total

        @pl.when(pl.program_id(1) == at_ref[1])
        def _():
            own_ref[...] = total

    spec = pl.BlockSpec((None, tr, cols), lambda r, i, at: (i, r, 0))
    shape = jax.ShapeDtypeStruct(theirs.shape, BF16)
    return pl.pallas_call(
        body, name=name, out_shape=[shape, shape],
        grid_spec=pltpu.PrefetchScalarGridSpec(
            num_scalar_prefetch=1, grid=(per, n),
            in_specs=[pl.BlockSpec((None, tr, cols), lambda r, i, at: (i, at[0] * per + r, 0)), spec],
            out_specs=[spec, pl.BlockSpec((None, tr, cols), lambda r, i, at: (at[1], r, 0))]),
        compiler_params=_params(("parallel", "arbitrary")))(jnp.concatenate([_scalar(core), _scalar(chip)]), g, theirs)


def _sum_slots(a, core, name):
    n, rows, cols = a.shape
    tr = _divisor_tile(rows, max(16, (2 * MIB) // (cols * 4 * n)), 16)

    def body(c_ref, a_ref, o_ref):
        acc = a_ref[0].astype(F32)
        for i in range(1, n):
            acc = acc + a_ref[i].astype(F32)
        o_ref[...] = acc

    return pl.pallas_call(
        body, name=name, out_shape=jax.ShapeDtypeStruct((2, rows, cols), F32),
        grid_spec=pltpu.PrefetchScalarGridSpec(
            num_scalar_prefetch=1, grid=(rows // tr,),
            in_specs=[pl.BlockSpec((n, tr, cols), lambda r, c: (0, r, 0))],
            out_specs=pl.BlockSpec((None, tr, cols), lambda r, c: (c[0], r, 0))),
        compiler_params=_params(("parallel",)))(_scalar(core), a)


def _adamw_math(w, g, m, v):
    m = ADAM_B1 * m + (1.0 - ADAM_B1) * g
    v = ADAM_B2 * v + (1.0 - ADAM_B2) * (g * g)
    m_hat = m / (1.0 - ADAM_B1 ** ADAM_STEP)
    v_hat = v / (1.0 - ADAM_B2 ** ADAM_STEP)
    delta = -ADAM_LR * (m_hat / (jnp.sqrt(v_hat) + ADAM_EPS) + ADAM_WD * w)
    return delta, m, v


def _adamw(w, g, m, v, name):
    rows, cols = w.shape
    tr = _divisor_tile(rows, max(8, MIB // (cols * 4)), 8)

    def body(w_ref, g_ref, m_ref, v_ref, d_ref, nm_ref, nv_ref):
        d, nm, nv = _adamw_math(w_ref[...], g_ref[...], m_ref[...], v_ref[...])
        d_ref[...] = d
        nm_ref[...] = nm
        nv_ref[...] = nv

    spec = pl.BlockSpec((tr, cols), lambda r: (r, 0))
    shape = jax.ShapeDtypeStruct((rows, cols), F32)
    return pl.pallas_call(body, name=name, grid=(rows // tr,), in_specs=[spec] * 4, out_specs=[spec] * 3,
                          out_shape=[shape] * 3, compiler_params=_params(("parallel",)))(w, g, m, v)


def _sum_rows_adamw(parts, w, m, v):
    def body(p_ref, w_ref, m_ref, v_ref, g_ref, d_ref, nm_ref, nv_ref):
        g = p_ref[0]
        for i in range(1, N_DEV):
            g = g + p_ref[i]
        d, nm, nv = _adamw_math(w_ref[...], g, m_ref[...], v_ref[...])
        g_ref[...] = g
        d_ref[...] = d
        nm_ref[...] = nm
        nv_ref[...] = nv

    shape = jax.ShapeDtypeStruct(w.shape, F32)
    return pl.pallas_call(body, name="sum_rows_adamw", out_shape=[shape] * 4)(parts, w, m, v)


LANES = 128


def _permute_scratch(rows, width):
    return pltpu.VMEM((width // LANES, rows, LANES), F32)


def _split_rows(value, scratch, dil):
    if dil == 1:
        return [value]
    rows = value.shape[0] // dil
    slabs = value.shape[1] // LANES
    for c in range(slabs):
        scratch[c] = value[:, c * LANES:(c + 1) * LANES]
    return [jnp.concatenate([scratch[c, pl.ds(r, rows, stride=dil), :] for c in range(slabs)], axis=1)
            for r in range(dil)]


def _merge_rows(ref, scratch, dil):
    if dil == 1:
        return ref[0].astype(F32)
    rows = ref.shape[1]
    slabs = ref.shape[2] // LANES
    for r in range(dil):
        part = ref[r].astype(F32)
        for c in range(slabs):
            scratch[c, pl.ds(r, rows, stride=dil), :] = part[:, c * LANES:(c + 1) * LANES]
    return jnp.concatenate([scratch[c] for c in range(slabs)], axis=1)


def _grouped_view(t, dil):
    return t.reshape(dil, t.shape[0] // dil, t.shape[1])


def _grouped_spec(dil, rows, width, index):
    return pl.BlockSpec((dil, rows // dil, width), index)


W_CHUNKS = 2


def _pick(values, j):
    out = values[-1]
    for i in range(len(values) - 2, -1, -1):
        out = jnp.where(j == i, values[i], out)
    return out


def _chunk_of(col, per_chip):
    return (col % per_chip) // (per_chip // W_CHUNKS)


def _w_block(col, per_chip):
    return col // per_chip, 0, (col % per_chip) % (per_chip // W_CHUNKS)


def _in_proj(xb, wc, blocks, j0, ncols, tn, out_dtype, prev, after, name):
    s, d = xb.shape
    per_chip = wc.shape[2] * W_CHUNKS // tn
    tm = _divisor_tile(s, 512, 16)
    extra = [t for t in (after,) if t is not None]

    def body(*refs):
        a_ref, b_ref = refs[len(extra):len(extra) + 2]
        o_ref = refs[-1]
        o_ref[...] = jnp.dot(a_ref[...], b_ref[...], preferred_element_type=F32).astype(o_ref.dtype)

    in_specs = [pl.BlockSpec(t.shape, lambda j, m: (0, 0)) for t in extra] + [
        pl.BlockSpec((tm, d), lambda j, m: (m, 0)),
        pl.BlockSpec((None, d, tn), lambda j, m: _w_block(_pick(blocks, j), per_chip))]
    args = extra + [xb, wc]
    aliases = {}
    if prev is not None:
        aliases = {len(args): 0}
        in_specs.append(ANY)
        args.append(prev)
    return pl.pallas_call(
        body, name=name, grid=(len(blocks), s // tm), in_specs=in_specs,
        out_specs=pl.BlockSpec((tm, tn), lambda j, m: (m, _pick(blocks, j) - j0)),
        out_shape=jax.ShapeDtypeStruct((s, ncols), out_dtype), input_output_aliases=aliases,
        compiler_params=_params(("parallel", "parallel")))(*args)


def _in_proj_qkv(xb, wc, g, blocks, aw, tn, prev, after, name):
    s, d = xb.shape
    dil = DILATIONS[g]
    per_chip = wc.shape[2] * W_CHUNKS // tn
    sub = aw // tn
    tm = _divisor_tile(s, 512, 16 * dil)
    extra = [t for t in (after,) if t is not None]

    def body(*refs):
        a_ref, b_ref = refs[len(extra):len(extra) + 2]
        o_ref, scratch = refs[-2:]
        res = jnp.dot(a_ref[...], b_ref[...], preferred_element_type=F32)
        for r, part in enumerate(_split_rows(res, scratch, dil)):
            o_ref[r] = part.astype(BF16)

    def out_index(j, m):
        col = _pick(blocks, j)
        return (col // sub) // N_GROUPS, 0, m, col % sub

    in_specs = [pl.BlockSpec(t.shape, lambda j, m: (0, 0)) for t in extra] + [
        pl.BlockSpec((tm, d), lambda j, m: (m, 0)),
        pl.BlockSpec((None, d, tn), lambda j, m: _w_block(_pick(blocks, j), per_chip))]
    args = extra + [xb, wc]
    aliases = {}
    if prev is not None:
        aliases = {len(args): 0}
        in_specs.append(ANY)
        args.append(prev)
    return pl.pallas_call(
        body, name=name, grid=(len(blocks), s // tm), in_specs=in_specs,
        out_specs=pl.BlockSpec((None, dil, tm // dil, tn), out_index),
        out_shape=jax.ShapeDtypeStruct((3, dil, s // dil, aw), BF16), input_output_aliases=aliases,
        scratch_shapes=[_permute_scratch(tm, tn)],
        compiler_params=_params(("parallel", "parallel")))(*args)


def _window_mask(first):
    qi = lax.broadcasted_iota(jnp.int32, (STEPS, 2 * STEPS), 0)
    kj = lax.broadcasted_iota(jnp.int32, (STEPS, 2 * STEPS), 1)
    lowest = jnp.where(first, STEPS, 0)
    return (kj >= qi) & (kj <= qi + STEPS) & (kj >= lowest)


def _attn_fwd(qkv, g):
    _, s, aw = qkv.shape
    heads = aw // HEAD_DIM
    n_blocks = s // STEPS
    per_seq = n_blocks // DILATIONS[g]

    def body(q_ref, kc_ref, kp_ref, vc_ref, vp_ref, o_ref, l_ref):
        mask = _window_mask(lax.rem(pl.program_id(0), per_seq) == 0)
        for h in range(heads):
            hs = slice(h * HEAD_DIM, (h + 1) * HEAD_DIM)
            kk = jnp.concatenate([kp_ref[:, hs], kc_ref[:, hs]], axis=0)
            vv = jnp.concatenate([vp_ref[:, hs], vc_ref[:, hs]], axis=0)
            sc = lax.dot_general(q_ref[:, hs], kk, NT, preferred_element_type=F32) * SCORE_SCALE
            sc = jnp.where(mask, sc, NEG_INF)
            mx = jnp.max(sc, axis=1, keepdims=True)
            e = jnp.exp(sc - mx)
            den = jnp.sum(e, axis=1, keepdims=True)
            o_ref[:, hs] = jnp.dot(e.astype(BF16), vv, preferred_element_type=F32) / den
            l_ref[:, hs] = jnp.broadcast_to(mx + jnp.log(den), (STEPS, HEAD_DIM))

    def cur(which):
        return pl.BlockSpec((None, STEPS, aw), lambda b: (which, b, 0))

    def prev(which):
        return pl.BlockSpec((None, STEPS, aw), lambda b: (which, jnp.maximum(b - 1, 0), 0))

    out = pl.BlockSpec((STEPS, aw), lambda b: (b, 0))
    shape = jax.ShapeDtypeStruct((s, aw), F32)
    return pl.pallas_call(
        body, name=f"attn_fwd{g}", grid=(n_blocks,),
        in_specs=[cur(0), cur(1), prev(1), cur(2), prev(2)], out_specs=[out, out], out_shape=[shape, shape],
        compiler_params=_params(("parallel",)))(qkv, qkv, qkv, qkv, qkv)


def _combine_groups(os, ls, zuz, aw):
    s = zuz.shape[0]
    tr = _divisor_tile(s, 256, 8 * DILATIONS[-1])

    def body(*refs):
        o_refs, l_refs, z_ref = refs[0:3], refs[3:6], refs[6]
        oo_ref, y_ref, yt_ref = refs[7:10]
        lq_refs, scratch = refs[10:13], refs[13]
        ls_ = [_merge_rows(l_refs[g], scratch, dil) for g, dil in enumerate(DILATIONS)]
        mx = jnp.maximum(jnp.maximum(ls_[0], ls_[1]), ls_[2])
        ws = [jnp.exp(l - mx) for l in ls_]
        den = ws[0] + ws[1] + ws[2]
        o = ws[0] * _merge_rows(o_refs[0], scratch, DILATIONS[0])
        for g in range(1, N_GROUPS):
            o = o + ws[g] * _merge_rows(o_refs[g], scratch, DILATIONS[g])
        o = o / den
        z = z_ref[...]
        y = o * (z * _sigmoid(z))
        oo_ref[...] = o
        y_ref[...] = y.astype(BF16)
        yt_ref[...] = y.T.astype(BF16)
        for g, dil in enumerate(DILATIONS):
            for r, part in enumerate(_split_rows(mx + jnp.log(den), scratch, dil)):
                lq_refs[g][r] = part

    grouped = [_grouped_spec(dil, tr, aw, lambda r: (0, r, 0)) for dil in DILATIONS]
    one = pl.BlockSpec((tr, aw), lambda r: (r, 0))
    f = jax.ShapeDtypeStruct((s, aw), F32)
    out = pl.pallas_call(
        body, name="combine_groups", grid=(s // tr,),
        in_specs=grouped + grouped + [one],
        out_specs=[one, one, pl.BlockSpec((aw, tr), lambda r: (0, r))] + grouped,
        out_shape=[f, jax.ShapeDtypeStruct((s, aw), BF16), jax.ShapeDtypeStruct((aw, s), BF16)]
        + [jax.ShapeDtypeStruct((dil, s // dil, aw), F32) for dil in DILATIONS],
        scratch_shapes=[_permute_scratch(tr, aw)],
        compiler_params=_params(("parallel",)))(
            *[_grouped_view(t, dil) for t, dil in zip(os, DILATIONS)],
            *[_grouped_view(t, dil) for t, dil in zip(ls, DILATIONS)], zuz)
    return out[0], out[1], out[2], [t.reshape(s, aw) for t in out[3:]]


def _pool_counts(row0, rows, window):
    t = row0 + lax.broadcasted_iota(jnp.int32, (rows, 1), 0)
    return jnp.minimum(t + 1, window).astype(F32)


def _pool_fwd(zuz, w_pool, pool_scale, aw, pw):
    s = zuz.shape[0]
    pg = pw // len(POOL_WINDOWS)
    tr = _divisor_tile(s, 256, 128)
    u_col, z_col = aw // pw, aw // pw + 1
    assert aw % pw == 0

    def body(u_ref, up_ref, z_ref, w_ref, sc_ref, p_ref, l_ref, y_ref, yt_ref):
        r = pl.program_id(0)
        u = u_ref[...]
        halo = jnp.where(r > 0, up_ref[...], 0.0)
        ext = jnp.concatenate([halo, u], axis=0)
        pieces, lins = [], []
        for gi, window in enumerate(POOL_WINDOWS):
            cs = slice(gi * pg, (gi + 1) * pg)
            acc = ext[:, cs]
            shift = 1
            while shift < window:
                acc = acc + pltpu.roll(acc, shift, 0)
                shift *= 2
            p = acc[POOL_HALO:] / _pool_counts(r * tr, tr, window) - u[:, cs]
            pieces.append(p)
            lins.append(jnp.dot(p.astype(BF16), w_ref[gi], preferred_element_type=F32))
        p = jnp.concatenate(pieces, axis=1)
        lin = jnp.concatenate(lins, axis=1)
        z = z_ref[...]
        y = lin * sc_ref[...] * (z * _sigmoid(z))
        p_ref[...] = p.astype(BF16)
        l_ref[...] = lin
        y_ref[...] = y.astype(BF16)
        yt_ref[...] = y.T.astype(BF16)

    per = tr // POOL_HALO
    out = pl.BlockSpec((tr, pw), lambda r: (r, 0))
    return pl.pallas_call(
        body, name="pool_fwd", grid=(s // tr,),
        in_specs=[pl.BlockSpec((tr, pw), lambda r: (r, u_col)),
                  pl.BlockSpec((POOL_HALO, pw), lambda r: (jnp.maximum(r * per - 1, 0), u_col)),
                  pl.BlockSpec((tr, pw), lambda r: (r, z_col)),
                  pl.BlockSpec((len(POOL_WINDOWS), pg, pg), lambda r: (0, 0, 0)),
                  pl.BlockSpec((1, pw), lambda r: (0, 0))],
        out_specs=[out, out, out, pl.BlockSpec((pw, tr), lambda r: (0, r))],
        out_shape=[jax.ShapeDtypeStruct((s, pw), BF16), jax.ShapeDtypeStruct((s, pw), F32),
                   jax.ShapeDtypeStruct((s, pw), BF16), jax.ShapeDtypeStruct((pw, s), BF16)],
        compiler_params=_params(("parallel",)))(zuz, zuz, zuz, w_pool, pool_scale)


def _proj_merge(y_attn, y_pool, wpa4, wpp4, gpre, b_gate):
    s, aw = y_attn.shape
    pw = y_pool.shape[1]
    tn = wpa4.shape[2]
    d = N_CHIPS * tn
    tm = _divisor_tile(s, 512, 128)

    def body(ya_ref, yp_ref, wa_ref, wp_ref, ga_ref, gp_ref, ba_ref, bp_ref, a_ref, p_ref, m_ref, mt_ref):
        a = jnp.dot(ya_ref[...], wa_ref[...], preferred_element_type=F32)
        p = jnp.dot(yp_ref[...], wp_ref[...], preferred_element_type=F32)
        merged = _sigmoid(ga_ref[...] + ba_ref[...]) * a + _sigmoid(gp_ref[...] + bp_ref[...]) * p
        a_ref[...] = a
        p_ref[...] = p
        m_ref[...] = merged.astype(BF16)
        mt_ref[...] = merged.T.astype(BF16)

    out = pl.BlockSpec((tm, tn), lambda n, m: (m, n))
    f = jax.ShapeDtypeStruct((s, d), F32)
    return pl.pallas_call(
        body, name="proj_merge", grid=(N_CHIPS, s // tm),
        in_specs=[pl.BlockSpec((tm, aw), lambda n, m: (m, 0)), pl.BlockSpec((tm, pw), lambda n, m: (m, 0)),
                  pl.BlockSpec((None, aw, tn), lambda n, m: (n, 0, 0)),
                  pl.BlockSpec((None, pw, tn), lambda n, m: (n, 0, 0)),
                  pl.BlockSpec((tm, tn), lambda n, m: (m, n)), pl.BlockSpec((tm, tn), lambda n, m: (m, N_CHIPS + n)),
                  pl.BlockSpec((1, tn), lambda n, m: (0, n)), pl.BlockSpec((1, tn), lambda n, m: (0, N_CHIPS + n))],
        out_specs=[out, out, out, pl.BlockSpec((tn, tm), lambda n, m: (n, m))],
        out_shape=[f, f, jax.ShapeDtypeStruct((s, d), BF16), jax.ShapeDtypeStruct((d, s), BF16)],
        compiler_params=_params(("parallel", "parallel")))(y_attn, y_pool, wpa4, wpp4, gpre, gpre, b_gate, b_gate)


def _out_norm_loss(merged, w_out, x, target, gamma, beta):
    s, d = x.shape
    tm = _divisor_tile(s, 256, 16)

    def body(m_ref, w_ref, x_ref, t_ref, g_ref, b_ref, dr_ref, drb_ref, loss_ref, dg_ref, db_ref):
        @pl.when(pl.program_id(0) == 0)
        def _():
            loss_ref[...] = jnp.zeros_like(loss_ref)
            dg_ref[...] = jnp.zeros_like(dg_ref)
            db_ref[...] = jnp.zeros_like(db_ref)

        r = ALPHA * x_ref[...] + jnp.dot(m_ref[...], w_ref[...], preferred_element_type=F32)
        mu = jnp.mean(r, axis=1, keepdims=True)
        rc = r - mu
        rstd = lax.rsqrt(jnp.mean(rc * rc, axis=1, keepdims=True) + LN_EPS)
        xhat = rc * rstd
        diff = xhat * g_ref[...] + b_ref[...] - t_ref[...]
        dy = diff / d
        loss_ref[...] += jnp.sum(diff * diff, axis=0, keepdims=True)
        dg_ref[...] += jnp.sum(dy * xhat, axis=0, keepdims=True)
        db_ref[...] += jnp.sum(dy, axis=0, keepdims=True)
        dxhat = dy * g_ref[...]
        dr = rstd * (dxhat - jnp.mean(dxhat, axis=1, keepdims=True)
                     - xhat * jnp.mean(dxhat * xhat, axis=1, keepdims=True))
        dr_ref[...] = dr
        drb_ref[...] = dr.astype(BF16)

    row = pl.BlockSpec((tm, d), lambda m: (m, 0))
    vec = pl.BlockSpec((1, d), lambda m: (0, 0))
    v = jax.ShapeDtypeStruct((1, d), F32)
    return pl.pallas_call(
        body, name="out_norm_loss", grid=(s // tm,),
        in_specs=[row, pl.BlockSpec((d, d), lambda m: (0, 0)), row, row, vec, vec],
        out_specs=[row, row, vec, vec, vec],
        out_shape=[jax.ShapeDtypeStruct((s, d), F32), jax.ShapeDtypeStruct((s, d), BF16), v, v, v],
        compiler_params=_params(("arbitrary",), vmem_mib=56))(merged, w_out, x, target, gamma, beta)


def _merge_bwd(drb, w_out, a, p, gpre, b_gate):
    s, d = drb.shape
    tm = _divisor_tile(s, 512, 16)
    tn = d // N_CHIPS

    def body(dr_ref, w_ref, a_ref, p_ref, ga_ref, gp_ref, ba_ref, bp_ref, da_ref, dp_ref, dga_ref, dgp_ref,
             dba_ref, dbp_ref):
        @pl.when(pl.program_id(1) == 0)
        def _():
            dba_ref[...] = jnp.zeros_like(dba_ref)
            dbp_ref[...] = jnp.zeros_like(dbp_ref)

        dm = lax.dot_general(dr_ref[...], w_ref[...], NT, preferred_element_type=F32)
        sa = _sigmoid(ga_ref[...] + ba_ref[...])
        sp = _sigmoid(gp_ref[...] + bp_ref[...])
        da_ref[...] = (dm * sa).astype(BF16)
        dp_ref[...] = (dm * sp).astype(BF16)
        dga = dm * a_ref[...] * sa * (1.0 - sa)
        dgp = dm * p_ref[...] * sp * (1.0 - sp)
        dga_ref[...] = dga.astype(BF16)
        dgp_ref[...] = dgp.astype(BF16)
        dba_ref[...] += jnp.sum(dga, axis=0, keepdims=True)
        dbp_ref[...] += jnp.sum(dgp, axis=0, keepdims=True)

    blk = pl.BlockSpec((tm, tn), lambda n, m: (m, n))
    blk1 = pl.BlockSpec((tm, tn), lambda n, m: (m, N_CHIPS + n))
    vec = pl.BlockSpec((1, tn), lambda n, m: (0, n))
    vec1 = pl.BlockSpec((1, tn), lambda n, m: (0, N_CHIPS + n))
    b16 = jax.ShapeDtypeStruct((s, d), BF16)
    v = jax.ShapeDtypeStruct((1, d), F32)
    return pl.pallas_call(
        body, name="merge_bwd", grid=(N_CHIPS, s // tm),
        in_specs=[pl.BlockSpec((tm, d), lambda n, m: (m, 0)), pl.BlockSpec((tn, d), lambda n, m: (n, 0)),
                  blk, blk, blk, blk1, vec, vec1],
        out_specs=[blk, blk, blk, blk, vec, vec], out_shape=[b16, b16, b16, b16, v, v],
        compiler_params=_params(("parallel", "arbitrary")))(drb, w_out, a, p, gpre, gpre, b_gate, b_gate)


def _proj_t(dy_ref, w_ref, tn):
    acc = None
    for n in range(N_CHIPS):
        t = lax.dot_general(dy_ref[:, n * tn:(n + 1) * tn], w_ref[n], NT, preferred_element_type=F32)
        acc = t if acc is None else acc + t
    return acc


def _attn_gate_bwd(da, wpa4, zuz, o):
    s, d = da.shape
    aw, tn = wpa4.shape[1], wpa4.shape[2]
    heads = aw // HEAD_DIM
    tm = _divisor_tile(s, 256, 16 * DILATIONS[-1])

    def body(*refs):
        da_ref, w_ref, z_ref, o_ref, dz_ref = refs[:5]
        do_refs, dd_refs, scratch = refs[5:8], refs[8:11], refs[11]
        dy = _proj_t(da_ref, w_ref, tn)
        z, o = z_ref[...], o_ref[...]
        sg = _sigmoid(z)
        do = dy * (z * sg)
        dz_ref[...] = (dy * o * _dsilu(z, sg)).astype(BF16)
        prod = do * o
        dd = jnp.concatenate(
            [jnp.broadcast_to(jnp.sum(prod[:, h * HEAD_DIM:(h + 1) * HEAD_DIM], axis=1, keepdims=True),
                              (tm, HEAD_DIM)) for h in range(heads)], axis=1)
        for g, dil in enumerate(DILATIONS):
            for r, part in enumerate(_split_rows(do, scratch, dil)):
                do_refs[g][r] = part.astype(BF16)
            for r, part in enumerate(_split_rows(dd, scratch, dil)):
                dd_refs[g][r] = part

    row = pl.BlockSpec((tm, aw), lambda m: (m, 0))
    grouped = [_grouped_spec(dil, tm, aw, lambda m: (0, m, 0)) for dil in DILATIONS]
    out = pl.pallas_call(
        body, name="attn_gate_bwd", grid=(s // tm,),
        in_specs=[pl.BlockSpec((tm, d), lambda m: (m, 0)), pl.BlockSpec((N_CHIPS, aw, tn), lambda m: (0, 0, 0)),
                  row, row],
        out_specs=[row] + grouped + grouped,
        out_shape=[jax.ShapeDtypeStruct((s, aw), BF16)]
        + [jax.ShapeDtypeStruct((dil, s // dil, aw), BF16) for dil in DILATIONS]
        + [jax.ShapeDtypeStruct((dil, s // dil, aw), F32) for dil in DILATIONS],
        scratch_shapes=[_permute_scratch(tm, aw)],
        compiler_params=_params(("parallel",)))(da, wpa4, zuz, o)
    return out[0], [t.reshape(s, aw) for t in out[1:4]], [t.reshape(s, aw) for t in out[4:7]]


def _pool_gate_bwd(dp_in, wpp4, zuz, lin, pooled, w_pool, pool_scale, aw):
    s, d = dp_in.shape
    pw, tn = wpp4.shape[1], wpp4.shape[2]
    n_win = len(POOL_WINDOWS)
    pg = pw // n_win
    tm = _divisor_tile(s, 256, 16)
    z_col = aw // pw + 1

    def body(dp_ref, w_ref, z_ref, l_ref, p_ref, wp_ref, sc_ref, dz_ref, dpo_ref, dw_ref, ds_ref):
        @pl.when(pl.program_id(0) == 0)
        def _():
            dw_ref[...] = jnp.zeros_like(dw_ref)
            ds_ref[...] = jnp.zeros_like(ds_ref)

        dy = _proj_t(dp_ref, w_ref, tn)
        z, lin_ = z_ref[...], l_ref[...]
        sg = _sigmoid(z)
        dypp = dy * (z * sg)
        dz_ref[...] = (dy * (lin_ * sc_ref[...]) * _dsilu(z, sg)).astype(BF16)
        ds_ref[...] += jnp.sum(dypp * lin_, axis=0, keepdims=True)
        dlin = (dypp * sc_ref[...]).astype(BF16)
        for gi in range(n_win):
            cs = slice(gi * pg, (gi + 1) * pg)
            pt = p_ref[:, cs].astype(F32).T.astype(BF16)
            dw_ref[gi] += jnp.dot(pt, dlin[:, cs], preferred_element_type=F32)
            dpo_ref[:, cs] = lax.dot_general(dlin[:, cs], wp_ref[gi], NT, preferred_element_type=F32)

    row = pl.BlockSpec((tm, pw), lambda m: (m, 0))
    return pl.pallas_call(
        body, name="pool_gate_bwd", grid=(s // tm,),
        in_specs=[pl.BlockSpec((tm, d), lambda m: (m, 0)), pl.BlockSpec((N_CHIPS, pw, tn), lambda m: (0, 0, 0)),
                  pl.BlockSpec((tm, pw), lambda m: (m, z_col)), row, row,
                  pl.BlockSpec((n_win, pg, pg), lambda m: (0, 0, 0)), pl.BlockSpec((1, pw), lambda m: (0, 0))],
        out_specs=[row, row, pl.BlockSpec((n_win, pg, pg), lambda m: (0, 0, 0)),
                   pl.BlockSpec((1, pw), lambda m: (0, 0))],
        out_shape=[jax.ShapeDtypeStruct((s, pw), BF16), jax.ShapeDtypeStruct((s, pw), F32),
                   jax.ShapeDtypeStruct((n_win, pg, pg), F32), jax.ShapeDtypeStruct((1, pw), F32)],
        compiler_params=_params(("arbitrary",)))(dp_in, wpp4, zuz, lin, pooled, w_pool, pool_scale)


def _pool_bwd(dpooled):
    s, pw = dpooled.shape
    pg = pw // len(POOL_WINDOWS)
    tr = _divisor_tile(s, 256, POOL_HALO)
    per = tr // POOL_HALO
    n_tiles = s // tr

    def body(c_ref, n_ref, du_ref):
        r = pl.program_id(0)
        cur = c_ref[...]
        halo = jnp.where(r < n_tiles - 1, n_ref[...], 0.0)
        ext = jnp.concatenate([cur, halo], axis=0)
        rows = tr + POOL_HALO
        for gi, window in enumerate(POOL_WINDOWS):
            cs = slice(gi * pg, (gi + 1) * pg)
            acc = ext[:, cs] / _pool_counts(r * tr, rows, window)
            shift = 1
            while shift < window:
                acc = acc + pltpu.roll(acc, rows - shift, 0)
                shift *= 2
            du_ref[:, cs] = (acc[:tr] - cur[:, cs]).astype(BF16)

    return pl.pallas_call(
        body, name="pool_bwd", grid=(n_tiles,),
        in_specs=[pl.BlockSpec((tr, pw), lambda r: (r, 0)),
                  pl.BlockSpec((POOL_HALO, pw), lambda r: (jnp.minimum((r + 1) * per, s // POOL_HALO - 1), 0))],
        out_specs=pl.BlockSpec((tr, pw), lambda r: (r, 0)),
        out_shape=jax.ShapeDtypeStruct((s, pw), BF16), compiler_params=_params(("parallel",)))(dpooled, dpooled)


def _attn_bwd(qkv, do, lse, dd, g):
    _, s, aw = qkv.shape
    heads = aw // HEAD_DIM
    n_blocks = s // STEPS
    per_seq = n_blocks // DILATIONS[g]

    def body(q_ref, do_ref, l_ref, dd_ref, kc_ref, kp_ref, vc_ref, vp_ref, out_ref, cq_ref, ck_ref, cv_ref):
        b = pl.program_id(0)

        @pl.when(b == 0)
        def _():
            cq_ref[...] = jnp.zeros_like(cq_ref)
            ck_ref[...] = jnp.zeros_like(ck_ref)
            cv_ref[...] = jnp.zeros_like(cv_ref)

        out_ref[0] = cq_ref[...].astype(BF16)

        @pl.when(b < n_blocks)
        def _():
            mask = _window_mask(lax.rem(b, per_seq) == 0)
            for h in range(heads):
                hs = slice(h * HEAD_DIM, (h + 1) * HEAD_DIM)
                q, do_ = q_ref[:, hs], do_ref[:, hs]
                kk = jnp.concatenate([kp_ref[:, hs], kc_ref[:, hs]], axis=0)
                vv = jnp.concatenate([vp_ref[:, hs], vc_ref[:, hs]], axis=0)
                lse_ = jnp.concatenate([l_ref[:, hs], l_ref[:, hs]], axis=1)
                dd_ = jnp.concatenate([dd_ref[:, hs], dd_ref[:, hs]], axis=1)
                sc = lax.dot_general(q, kk, NT, preferred_element_type=F32) * SCORE_SCALE
                prob = jnp.where(mask, jnp.exp(sc - lse_), 0.0)
                dprob = lax.dot_general(do_, vv, NT, preferred_element_type=F32)
                dsc = prob * (dprob - dd_) * SCORE_SCALE
                cq_ref[:, hs] = jnp.dot(dsc.astype(BF16), kk, preferred_element_type=F32)
                dkk = jnp.dot(dsc.T.astype(BF16), q, preferred_element_type=F32)
                dvv = jnp.dot(prob.T.astype(BF16), do_, preferred_element_type=F32)
                out_ref[1, :, hs] = (ck_ref[:, hs] + dkk[:STEPS]).astype(BF16)
                out_ref[2, :, hs] = (cv_ref[:, hs] + dvv[:STEPS]).astype(BF16)
                ck_ref[:, hs] = dkk[STEPS:]
                cv_ref[:, hs] = dvv[STEPS:]

        @pl.when(b == n_blocks)
        def _():
            out_ref[1] = ck_ref[...].astype(BF16)
            out_ref[2] = cv_ref[...].astype(BF16)

    last = n_blocks - 1

    def cur(which):
        return pl.BlockSpec((None, STEPS, aw), lambda b: (which, jnp.minimum(b, last), 0))

    def prev(which):
        return pl.BlockSpec((None, STEPS, aw), lambda b: (which, jnp.clip(b - 1, 0, last), 0))

    row = pl.BlockSpec((STEPS, aw), lambda b: (jnp.minimum(b, last), 0))
    return pl.pallas_call(
        body, name=f"attn_bwd{g}", grid=(n_blocks + 1,),
        in_specs=[cur(0), row, row, row, cur(1), prev(1), cur(2), prev(2)],
        out_specs=pl.BlockSpec((3, STEPS, aw), lambda b: (0, jnp.clip(b - 1, 0, last), 0)),
        out_shape=jax.ShapeDtypeStruct((3, s, aw), BF16),
        scratch_shapes=[pltpu.VMEM((STEPS, aw), F32)] * 3,
        compiler_params=_params(("arbitrary",)))(qkv, do, lse, dd, qkv, qkv, qkv, qkv)


def _weight_grad(at, b, tn, col_blocks, name):
    m, k = at.shape
    n = b.shape[1]
    tm = _divisor_tile(m, 1024, 16)
    tk = _divisor_tile(k, 2048, 128)
    nk = k // tk

    def body(a_ref, b_ref, o_ref, acc_ref):
        kk = pl.program_id(2)

        @pl.when(kk == 0)
        def _():
            acc_ref[...] = jnp.zeros_like(acc_ref)

        acc_ref[...] += jnp.dot(a_ref[...], b_ref[...], preferred_element_type=F32)

        @pl.when(kk == nk - 1)
        def _():
            o_ref[...] = acc_ref[...].astype(BF16)

    if col_blocks:
        out_spec = pl.BlockSpec((None, tm, tn), lambda i, j, kk: (j, i, 0))
        out_shape = jax.ShapeDtypeStruct((n // tn, m, tn), BF16)
    else:
        out_spec = pl.BlockSpec((tm, tn), lambda i, j, kk: (i, j))
        out_shape = jax.ShapeDtypeStruct((m, n), BF16)
    return pl.pallas_call(
        body, name=name, grid=(m // tm, n // tn, nk),
        in_specs=[pl.BlockSpec((tm, tk), lambda i, j, kk: (i, kk)), pl.BlockSpec((tk, tn), lambda i, j, kk: (kk, j))],
        out_specs=out_spec, out_shape=out_shape, scratch_shapes=[pltpu.VMEM((tm, tn), F32)],
        compiler_params=_params(("parallel", "parallel", "arbitrary")))(at, b)


def _w_in_grad_part(xt, b, col_of, n_local, tn, w_shape, prev, name):
    d, s = xt.shape
    per_chip = w_shape[2] // tn
    tm = _divisor_tile(d, 1024, 16)
    tk = _divisor_tile(s, 2048, 128)
    nk = s // tk

    def body(*refs):
        a_ref, b_ref, o_ref, acc_ref = refs[0], refs[1], refs[-2], refs[-1]
        kk = pl.program_id(2)

        @pl.when(kk == 0)
        def _():
            acc_ref[...] = jnp.zeros_like(acc_ref)

        acc_ref[...] += jnp.dot(a_ref[...], b_ref[...], preferred_element_type=F32)

        @pl.when(kk == nk - 1)
        def _():
            o_ref[...] = acc_ref[...].astype(BF16)

    if b.ndim == 3:
        sub = b.shape[2] // tn
        b_spec = pl.BlockSpec((None, tk, tn), lambda j, i, kk: (j // sub, kk, j % sub))
    else:
        b_spec = pl.BlockSpec((tk, tn), lambda j, i, kk: (kk, j))
    in_specs = [pl.BlockSpec((tm, tk), lambda j, i, kk: (i, kk)), b_spec]
    args = [xt, b]
    aliases = {}
    if prev is not None:
        in_specs.append(ANY)
        args.append(prev)
        aliases = {2: 0}
    return pl.pallas_call(
        body, name=name, grid=(n_local, d // tm, nk), in_specs=in_specs,
        out_specs=pl.BlockSpec((None, tm, tn), lambda j, i, kk: (col_of(j) // per_chip, i, col_of(j) % per_chip)),
        out_shape=jax.ShapeDtypeStruct(w_shape, BF16), scratch_shapes=[pltpu.VMEM((tm, tn), F32)],
        input_output_aliases=aliases,
        compiler_params=_params(("parallel", "parallel", "arbitrary")))(*args)


def _x_grad(dqkv, rest, wc, chunk, init, init_scale, aw, tn, after=None):
    s, d = init.shape
    sub = aw // tn
    n_qkv = 3 * N_GROUPS * sub
    los, lo = [], n_qkv
    for p in rest:
        los.append(lo)
        lo += p.shape[1] // tn
    per_chip = lo // N_CHIPS
    per = per_chip // W_CHUNKS
    n_local = lo // W_CHUNKS
    tm = _divisor_tile(s, 512, 16 * DILATIONS[-1])

    def col(jl):
        return (jl // per) * per_chip + chunk * per + jl % per

    ordered = [] if after is None else [after]

    def body(*refs):
        refs = refs[len(ordered):]
        q_refs, r_refs = refs[:N_GROUPS], refs[N_GROUPS:N_GROUPS + len(rest)]
        w_ref, init_ref, o_ref, acc_ref, scratch = refs[-5:]
        jl = pl.program_id(1)
        j = col(jl)

        @pl.when(jl == 0)
        def _():
            acc_ref[...] = init_scale * init_ref[...]

        for g, dil in enumerate(DILATIONS):
            @pl.when((j < n_qkv) & (lax.rem(j // sub, N_GROUPS) == g))
            def _(g=g, dil=dil):
                rows = _merge_rows(q_refs[g], scratch, dil).astype(BF16)
                acc_ref[...] += lax.dot_general(rows, w_ref[...], NT, preferred_element_type=F32)

        for p_ref, lo_, piece in zip(r_refs, los, rest):
            @pl.when((j >= lo_) & (j < lo_ + piece.shape[1] // tn))
            def _(p_ref=p_ref):
                acc_ref[...] += lax.dot_general(p_ref[...], w_ref[...], NT, preferred_element_type=F32)

        @pl.when(jl == n_local - 1)
        def _():
            o_ref[...] = acc_ref[...]

    def qkv_spec(dil):
        def index(i, jl):
            j = col(jl)
            region = jnp.minimum(j // sub, 3 * N_GROUPS - 1)
            return region // N_GROUPS, 0, i, jnp.where(j < n_qkv, j % sub, 0)

        return pl.BlockSpec((None, dil, tm // dil, tn), index)

    def rest_spec(lo_, piece):
        n = piece.shape[1] // tn
        return pl.BlockSpec((tm, tn), lambda i, jl: (i, jnp.clip(col(jl) - lo_, 0, n - 1)))

    row = pl.BlockSpec((tm, d), lambda i, jl: (i, 0))
    return pl.pallas_call(
        body, name=f"x_grad{chunk}", grid=(s // tm, n_local),
        in_specs=[pl.BlockSpec(t.shape, lambda i, jl: (0, 0)) for t in ordered]
        + [qkv_spec(dil) for dil in DILATIONS] + [rest_spec(lo_, p) for lo_, p in zip(los, rest)]
        + [pl.BlockSpec((None, d, tn), lambda i, jl: (jl // per, 0, jl % per)), row],
        out_specs=row, out_shape=jax.ShapeDtypeStruct((s, d), F32),
        scratch_shapes=[pltpu.VMEM((tm, d), F32), _permute_scratch(tm, tn)],
        compiler_params=_params(("parallel", "arbitrary"), vmem_mib=56))(
            *ordered, *[t.reshape(3, dil, s // dil, aw) for t, dil in zip(dqkv, DILATIONS)], *rest, wc, init)


def _to_subsequences(t, dilation):
    s, w = t.shape
    return t.reshape(s // dilation, dilation, w).transpose(1, 0, 2).reshape(s, w)


def _prepare_x(x):
    s, d = x.shape
    tc = 2 * LANES
    slabs = tc // LANES

    def body(x_ref, xb_ref, *refs):
        xt_refs, scratch = refs[:N_GROUPS], refs[-1]
        t = x_ref[...]
        xb_ref[...] = t.astype(BF16)
        for c in range(slabs):
            scratch[c] = t[:, c * LANES:(c + 1) * LANES]
        for g, dil in enumerate(DILATIONS):
            length = s // dil
            for r in range(dil):
                part = t if dil == 1 else jnp.concatenate(
                    [scratch[c, pl.ds(r, length, stride=dil), :] for c in range(slabs)], axis=1)
                xt_refs[g][:, r * length:(r + 1) * length] = part.T.astype(BF16)

    col = pl.BlockSpec((s, tc), lambda j: (0, j))
    row = pl.BlockSpec((tc, s), lambda j: (j, 0))
    t_shape = jax.ShapeDtypeStruct((d, s), BF16)
    out = pl.pallas_call(
        body, name="prepare_x", grid=(d // tc,), in_specs=[col], out_specs=[col] + [row] * N_GROUPS,
        out_shape=[jax.ShapeDtypeStruct((s, d), BF16)] + [t_shape] * N_GROUPS,
        scratch_shapes=[_permute_scratch(s, tc)], compiler_params=_params(("parallel",)))(x)
    return out[0], out[1:]


def _local_step(x, target, w_chunk, w_width, b_gate, pool_scale, gamma, beta, aw, pw, small_weights,
                start_exchange=None):
    s, d = x.shape
    tn = _col_tile(aw, pw, w_width)
    sub = aw // tn
    per_chip = w_width // tn
    qkv_w = 3 * N_GROUPS * aw
    w_shape = (N_CHIPS, d, w_width)

    xb, xts = _prepare_x(x)
    regions = [dict(kind=g, blocks=[(which * N_GROUPS + g) * sub + i for which in range(3) for i in range(sub)])
               for g in range(N_GROUPS)]
    lo = qkv_w // tn
    for name, width in (("zuz", aw + 2 * pw), ("gates", 2 * d)):
        regions.append(dict(kind=name, blocks=list(range(lo, lo + width // tn)), j0=lo, width=width))
        lo += width // tn
    results = [None] * len(regions)
    wcs, after = [], None
    for ch in range(W_CHUNKS):
        wc, token = w_chunk(ch, after)
        wcs.append(wc)
        for i, region in enumerate(regions):
            blocks = [b for b in region["blocks"] if _chunk_of(b, per_chip) == ch]
            if not blocks:
                continue
            if region["kind"] in range(N_GROUPS):
                results[i] = _in_proj_qkv(xb, wc, region["kind"], blocks, aw, tn, results[i], token,
                                          f"in_proj_qkv{region['kind']}_{ch}")
            else:
                results[i] = _in_proj(xb, wc, blocks, region["j0"], region["width"], tn, F32, results[i], token,
                                      f"in_proj_{region['kind']}_{ch}")
            token = None
            after = results[i]
    qkv = [results[g].reshape(3, s, aw) for g in range(N_GROUPS)]
    zuz, gpre = results[N_GROUPS], results[N_GROUPS + 1]

    attn = [_attn_fwd(qkv[g], g) for g in range(N_GROUPS)]
    o, y_attn, y_attn_t, lse = _combine_groups([a[0] for a in attn], [a[1] for a in attn], zuz, aw)
    w_pool, wpa4, wpp4, w_out = small_weights(o)
    pooled, lin, y_pool, y_pool_t = _pool_fwd(zuz, w_pool, pool_scale, aw, pw)
    a, p, merged, merged_t = _proj_merge(y_attn, y_pool, wpa4, wpp4, gpre, b_gate)
    dr, drb, loss_lanes, d_gamma, d_beta = _out_norm_loss(merged, w_out, x, target, gamma, beta)

    da, dp, d_gpre_a, d_gpre_p, d_b_a, d_b_p = _merge_bwd(drb, w_out, a, p, gpre, b_gate)
    d_b_gate = jnp.concatenate([d_b_a, d_b_p], axis=1)
    d_w_out = _weight_grad(merged_t, drb, d // N_CHIPS, False, "w_out_grad")
    d_wpa4 = _weight_grad(y_attn_t, da, d // N_CHIPS, True, "w_proj_attn_grad")
    d_wpp4 = _weight_grad(y_pool_t, dp, d // N_CHIPS, True, "w_proj_pool_grad")
    d_z_attn, d_o, dd = _attn_gate_bwd(da, wpa4, zuz, o)
    d_z_pool, d_pooled, d_w_pool, d_pool_scale = _pool_gate_bwd(dp, wpp4, zuz, lin, pooled, w_pool, pool_scale, aw)
    d_u = _pool_bwd(d_pooled)
    dqkv = [_attn_bwd(qkv[g], d_o[g], lse[g], dd[g], g) for g in range(N_GROUPS)]

    rest = [d_z_attn, d_u, d_z_pool, d_gpre_a, d_gpre_p]
    d_w_in4 = None
    for g in range(N_GROUPS):
        d_w_in4 = _w_in_grad_part(xts[g], dqkv[g], lambda j, g=g: ((j // sub) * N_GROUPS + g) * sub + j % sub,
                                  3 * sub, tn, w_shape, d_w_in4, f"w_in_grad_qkv{g}")
    lo = qkv_w // tn
    for i, piece in enumerate(rest):
        n_local = piece.shape[1] // tn
        d_w_in4 = _w_in_grad_part(xts[0], piece, lambda j, lo=lo: lo + j, n_local, tn, w_shape, d_w_in4,
                                  f"w_in_grad_rest{i}")
        lo += n_local
    grads = dict(loss_lanes=loss_lanes, w_in=d_w_in4, b_gate=d_b_gate, w_pool=d_w_pool,
                 pool_scale=d_pool_scale, w_proj_attn=d_wpa4, w_proj_pool=d_wpp4, w_out=d_w_out,
                 ln_gamma=d_gamma, ln_beta=d_beta)
    token = None if start_exchange is None else start_exchange(grads)
    d_x, scale = dr, ALPHA
    for ch in range(W_CHUNKS):
        d_x = _x_grad(dqkv, rest, wcs[ch], ch, d_x, scale, aw, tn, token)
        token, scale = None, 1.0
    grads["d_x"] = d_x
    return grads


def _pack_small(wpa, wpp, w_out, w_pool):
    width = wpa.shape[1]
    return jnp.concatenate([wpa, wpp, w_out.reshape(-1, width), w_pool.reshape(-1, width)], axis=0)


def _unpack_small(packed, aw, pw, d, pg):
    lead = packed.shape[:-2]
    width = d // N_CHIPS
    r0, r1, r2 = aw, aw + pw, aw + pw + d
    return (packed[..., :r0, :], packed[..., r0:r1, :], packed[..., r1:r2, :].reshape(lead + (width, d)),
            packed[..., r2:, :].reshape(lead + (len(POOL_WINDOWS), pg // N_CHIPS, pg)))


def _pack_rows(vectors, rows):
    flat = jnp.concatenate([v.reshape(-1) for v in vectors])
    return jnp.pad(flat, (0, rows * 128 - flat.shape[0])).reshape(rows, 128)


def _unpack_rows(packed, sizes):
    flat, out, lo = packed.reshape(-1), [], 0
    for n in sizes:
        out.append(flat[lo:lo + n].reshape(1, n))
        lo += n
    return out


def kernel(x, w_in, b_gate, w_pool, pool_scale, w_proj_attn, w_proj_pool, w_out, ln_gamma, ln_beta, loss_target, m_w_in, m_b_gate, m_w_pool, m_pool_scale, m_w_proj_attn, m_w_proj_pool, m_w_out, m_ln_gamma, m_ln_beta, v_w_in, v_b_gate, v_w_pool, v_pool_scale, v_w_proj_attn, v_w_proj_pool, v_w_out, v_ln_gamma, v_ln_beta):
    s, d = x.shape[1], x.shape[2]
    aw, pw = w_proj_attn.shape[1], w_proj_pool.shape[1]
    pg = w_pool.shape[3]
    n_win = len(POOL_WINDOWS)

    def small(wpa, wpp, wo, wpl):
        return _pack_small(wpa[0], wpp[0], wo[0], wpl[0])

    chip = 2 * lax.axis_index("x") + lax.axis_index("y")
    core = lax.axis_index("c")

    w_small = small(w_proj_attn, w_proj_pool, w_out, w_pool)
    placed = [_place_block(w_in[0], N_CHIPS, chip, BF16, f"place_w_in{ch}", ch, W_CHUNKS) for ch in range(W_CHUNKS)]
    placed_small = _place_block(w_small, N_CHIPS, chip, BF16, "place_w_small")
    gathered, = _gather_weights([placed[0]])
    flight = {"chunk": _halves_start(placed[1], gathered, "gather_w_in1_start")}

    def w_chunk(ch, after):
        if ch == 0:
            return gathered, flight["chunk"][2]
        sems, thru, _ = flight["chunk"]
        landed = _halves_wait(sems, thru, after, f"gather_w_in{ch}_wait")
        if ch + 1 < W_CHUNKS:
            flight["chunk"] = _halves_start(placed[ch + 1], landed, f"gather_w_in{ch + 1}_start")
            token = flight["chunk"][2]
        else:
            flight["small"] = _broadcast_start(placed_small, landed, "gather_small_start")
            token = flight["small"][2]
        return _forward_halves(landed, f"forward_w_in{ch}"), token

    def small_weights(after):
        sems, thru, _ = flight["small"]
        small4 = _broadcast_wait(sems, thru, after, "gather_small_wait")
        wpa4, wpp4, w_out4, w_pool4 = _unpack_small(small4, aw, pw, d, pg)
        return w_pool4.transpose(1, 0, 2, 3).reshape(n_win, pg, pg), wpa4, wpp4, w_out4.reshape(d, d)

    exchange = {}

    def start_exchange(g):
        g_pool4 = g["w_pool"].reshape(n_win, N_CHIPS, pg // N_CHIPS, pg).transpose(1, 0, 2, 3).astype(BF16)
        g_out4 = g["w_out"].reshape(N_CHIPS, d // N_CHIPS, d)
        g_small4 = jnp.concatenate([g["w_proj_attn"], g["w_proj_pool"], g_out4.reshape(N_CHIPS, -1, d // N_CHIPS),
                                    g_pool4.reshape(N_CHIPS, -1, d // N_CHIPS)], axis=1)
        theirs_big, theirs_small = _swap_halves([g["w_in"], g_small4])
        chip_big, placed_big = _add_halves(g["w_in"], theirs_big, core, chip, "add_cores_big")
        chip_small, placed_small = _add_halves(g_small4, theirs_small, core, chip, "add_cores_small")
        sems, sums, placed, token = _scatter_start([chip_big, chip_small], [placed_big, placed_small])
        exchange.update(sems=sems, sums=sums, placed=placed)
        return token

    g = _local_step(x[0], loss_target[0], w_chunk, w_in.shape[2], b_gate, pool_scale, ln_gamma, ln_beta, aw, pw,
                    small_weights, start_exchange)
    got_big, got_small = _scatter_wait(exchange["sems"], exchange["sums"], exchange["placed"], g["d_x"])
    grad_w_in, grad_small = _join_halves([_sum_slots(got_big, core, "sum_chips_big"),
                                          _sum_slots(got_small, core, "sum_chips_small")])
    grad_w_in = grad_w_in.reshape(-1, grad_w_in.shape[2])
    grad_small = grad_small.reshape(-1, grad_small.shape[2])

    sizes = [b_gate.shape[1], pool_scale.shape[1], d, d, 1]
    rows = -(-sum(sizes) // (8 * 128)) * 8
    loss_part = (0.5 / d) * jnp.sum(g["loss_lanes"]).reshape(1, 1)
    parts = _gather_rows(_pack_rows([g["b_gate"], g["pool_scale"], g["ln_gamma"], g["ln_beta"], loss_part], rows))
    zero = jnp.zeros((1, 1), F32)
    packed = [_pack_rows(vs, rows) for vs in ([b_gate, pool_scale, ln_gamma, ln_beta, zero],
                                              [m_b_gate, m_pool_scale, m_ln_gamma, m_ln_beta, zero],
                                              [v_b_gate, v_pool_scale, v_ln_gamma, v_ln_beta, zero])]
    rep = [_unpack_rows(t, sizes) for t in _sum_rows_adamw(parts, *packed)]
    loss = rep[0][4].reshape(())

    upd_in = _adamw(w_in[0], grad_w_in, m_w_in[0], v_w_in[0], "adamw_w_in")
    upd_small = _adamw(w_small, grad_small, small(m_w_proj_attn, m_w_proj_pool, m_w_out, m_w_pool),
                       small(v_w_proj_attn, v_w_proj_pool, v_w_out, v_w_pool), "adamw_small")

    def leaves(big, packed_small, replicated):
        wpa_, wpp_, wo_, wpl_ = _unpack_small(packed_small, aw, pw, d, pg)
        return [big[None], replicated[0], wpl_[None], replicated[1], wpa_[None], wpp_[None], wo_[None],
                replicated[2], replicated[3]]

    out = [loss, g["d_x"][None]]
    out += leaves(grad_w_in, grad_small, rep[0])
    for i in range(3):
        out += leaves(upd_in[i], upd_small[i], rep[1 + i])
    return tuple(out)
```

```python
import math

import jax
import jax.numpy as jnp
from jax import lax
from jax.experimental import pallas as pl
from jax.experimental.pallas import tpu as pltpu

F32 = jnp.float32
BF16 = jnp.bfloat16
MESH = pl.DeviceIdType.MESH
ANY = pl.BlockSpec(memory_space=pl.ANY)

HEAD_DIM = 128
STEPS = 128
DILATIONS = (1, 4, 16)
N_GROUPS = len(DILATIONS)
POOL_WINDOWS = (2, 4, 8, 16)
POOL_HALO = 16
N_CHIPS = 4
N_DEV = 8
ALPHA = 2.0 ** 0.25
LN_EPS = 1e-5
NEG_INF = -1e30
SCORE_SCALE = HEAD_DIM ** -0.5
ADAM_LR = 0.001
ADAM_B1 = 0.9
ADAM_B2 = 0.999
ADAM_EPS = 1e-08
ADAM_WD = 0.01
ADAM_STEP = 10
MIB = 2 ** 20
NT = (((1,), (1,)), ((), ()))
DMA_STREAMS = 8


def _params(semantics=None, vmem_mib=48):
    return pltpu.CompilerParams(dimension_semantics=semantics, vmem_limit_bytes=vmem_mib * MIB)


def _divisor_tile(n, target, multiple):
    best = None
    for t in range(multiple, min(n, target) + 1, multiple):
        if n % t == 0:
            best = t
    assert best is not None, (n, target, multiple)
    return best


def _col_tile(*widths):
    g = 0
    for w in widths:
        g = math.gcd(g, w)
    return _divisor_tile(g, 1024, 128)


def _sigmoid(z):
    return jax.nn.sigmoid(z)


def _dsilu(z, sg):
    return sg * (1.0 + z * (1.0 - sg))


def _place():
    x, y, c = lax.axis_index("x"), lax.axis_index("y"), lax.axis_index("c")
    others = [(1 - x, y), (x, 1 - y), (1 - x, 1 - y)]
    return x, y, c, (x, y, 1 - c), others


def _remote(src, dst, send_sem, recv_sem, dev):
    return pltpu.make_async_remote_copy(src_ref=src, dst_ref=dst, send_sem=send_sem, recv_sem=recv_sem,
                                        device_id=dev, device_id_type=MESH)


def _row_pieces(n_rows, streams=DMA_STREAMS, multiple=16):
    size = -(-n_rows // (streams * multiple)) * multiple
    return [(lo, min(size, n_rows - lo)) for lo in range(0, n_rows, size)]


def _start_streams(make, n_rows):
    for lo, size in _row_pieces(n_rows):
        make(pl.ds(lo, size)).start()


def _gather_weights(placed):
    n = len(placed)

    def body(*refs):
        dst = refs[n:2 * n]
        send_sems, recv_sems = refs[2 * n:]
        x, y, c, sibling, others = _place()
        me = 2 * x + y
        sent = []
        for i in range(n):
            half = dst[i].shape[1] // 2
            mine = c * half
            for j, (ox, oy) in enumerate(others):
                slab = dst[i].at[me, pl.ds(mine, half)]
                cp = _remote(slab, slab, send_sems.at[6 * i + j], recv_sems.at[6 * i + j], (ox, oy, c))
                cp.start()
                sent.append(cp)
        for i in range(n):
            half = dst[i].shape[1] // 2
            mine = c * half
            for j, (ox, oy) in enumerate(others):
                blk = dst[i].at[2 * ox + oy]
                slab = blk.at[pl.ds(mine, half)]
                _remote(slab, slab, send_sems.at[6 * i + j], recv_sems.at[6 * i + j], (ox, oy, c)).wait_recv()
                k = 6 * i + 3 + j
                _start_streams(lambda r, blk=blk, k=k: _remote(blk.at[pl.ds(mine + r.start, r.size)],
                                                               blk.at[pl.ds(mine + r.start, r.size)],
                                                               send_sems.at[k], recv_sems.at[k], sibling), half)
                sent.append(_remote(slab, slab, send_sems.at[k], recv_sems.at[k], sibling))
        for i in range(n):
            half = dst[i].shape[1] // 2
            for j, (ox, oy) in enumerate(others):
                slab = dst[i].at[2 * ox + oy, pl.ds((1 - c) * half, half)]
                _remote(slab, slab, send_sems.at[6 * i + 3 + j], recv_sems.at[6 * i + 3 + j], sibling).wait_recv()
        for cp in sent:
            cp.wait_send()

    return pl.pallas_call(
        body, name="gather_weights", out_shape=[jax.ShapeDtypeStruct(s.shape, s.dtype) for s in placed],
        in_specs=[ANY] * n, out_specs=[ANY] * n, input_output_aliases={i: i for i in range(n)},
        scratch_shapes=[pltpu.SemaphoreType.DMA((6 * n,)), pltpu.SemaphoreType.DMA((6 * n,))],
    )(*placed)


def _half_copies(buf, send_sems, recv_sems):
    x, y, c, _, others = _place()
    half = buf.shape[1] // 2
    slab = buf.at[2 * x + y, pl.ds(c * half, half)]
    return [_remote(slab, slab, send_sems[j], recv_sems[j], (ox, oy, c)) for j, (ox, oy) in enumerate(others)]


def _halves_start(placed, after, name):
    k = N_CHIPS - 1

    def body(buf, after_ref, *refs):
        send_sems, recv_sems, token = refs[:k], refs[k:2 * k], refs[-1]
        for cp in _half_copies(buf, send_sems, recv_sems):
            cp.start()
        token[...] = jnp.zeros_like(token)

    out = pl.pallas_call(
        body, name=name,
        out_shape=[pltpu.SemaphoreType.DMA(())] * (2 * k) + [pltpu.HBM(placed.shape, placed.dtype),
                                                             jax.ShapeDtypeStruct((8, 128), F32)],
        in_specs=[HBM, ANY], out_specs=[SEM] * (2 * k) + [HBM, pl.BlockSpec(memory_space=pltpu.VMEM)],
        input_output_aliases={0: 2 * k},
        compiler_params=pltpu.CompilerParams(has_side_effects=DATAFLOW),
    )(pltpu.with_memory_space_constraint(placed, pltpu.HBM), after)
    return out[:2 * k], out[2 * k], out[-1]


def _halves_wait(sems, placed, after, name):
    k = N_CHIPS - 1

    def body(buf, *refs):
        send_sems, recv_sems = refs[:k], refs[k:2 * k]
        for cp in _half_copies(buf, send_sems, recv_sems):
            cp.wait_send()
            cp.wait_recv()

    return pl.pallas_call(
        body, name=name, out_shape=pltpu.HBM(placed.shape, placed.dtype),
        in_specs=[HBM] + [SEM] * (2 * k) + [ANY], out_specs=HBM, input_output_aliases={0: 0},
        compiler_params=pltpu.CompilerParams(has_side_effects=DATAFLOW),
    )(placed, *sems, after)


def _forward_halves(buf, name):
    def body(_, dst, send_sems, recv_sems):
        x, y, c, sibling, others = _place()
        half = dst.shape[1] // 2
        for j, (ox, oy) in enumerate(others):
            slab = dst.at[2 * ox + oy, pl.ds(c * half, half)]
            _remote(slab, slab, send_sems.at[j], recv_sems.at[j], sibling).start()
        for j, (ox, oy) in enumerate(others):
            mine = dst.at[2 * ox + oy, pl.ds(c * half, half)]
            theirs = dst.at[2 * ox + oy, pl.ds((1 - c) * half, half)]
            cp = _remote(mine, theirs, send_sems.at[j], recv_sems.at[j], sibling)
            cp.wait_recv()
            cp.wait_send()

    return pl.pallas_call(
        body, name=name, out_shape=jax.ShapeDtypeStruct(buf.shape, buf.dtype),
        in_specs=[ANY], out_specs=ANY, input_output_aliases={0: 0},
        scratch_shapes=[pltpu.SemaphoreType.DMA((N_CHIPS - 1,)), pltpu.SemaphoreType.DMA((N_CHIPS - 1,))],
    )(buf)


def _swap_halves(grads):
    n = len(grads)

    def body(*refs):
        g, theirs = refs[:n], refs[n:2 * n]
        send_sems, recv_sems = refs[2 * n:]
        x, y, c, sibling, _ = _place()
        for i in range(n):
            half = g[i].shape[1] // 2
            give = (1 - c) * half
            for b in range(N_CHIPS):
                _start_streams(lambda r, i=i, b=b: _remote(
                    g[i].at[b, pl.ds(give + r.start, r.size)], theirs[i].at[b, r], send_sems.at[i], recv_sems.at[i],
                    sibling), half)
        for i in range(n):
            _remote(theirs[i], theirs[i], send_sems.at[i], recv_sems.at[i], sibling).wait()

    return pl.pallas_call(
        body, name="swap_halves",
        out_shape=[jax.ShapeDtypeStruct((s.shape[0], s.shape[1] // 2) + s.shape[2:], s.dtype) for s in grads],
        in_specs=[ANY] * n, out_specs=[ANY] * n,
        scratch_shapes=[pltpu.SemaphoreType.DMA((n,)), pltpu.SemaphoreType.DMA((n,))],
    )(*grads)


HBM = pl.BlockSpec(memory_space=pltpu.HBM)
SEM = pl.BlockSpec(memory_space=pltpu.SEMAPHORE)
DATAFLOW = pltpu.SideEffectType.DATAFLOW_SIDE_EFFECTING


def _broadcast_copies(buf, send_sems, recv_sems):
    x, y, c, _, others = _place()
    mine = buf.at[2 * x + y]
    return [_remote(mine, mine, send_sems[j], recv_sems[j], (ox, oy, c)) for j, (ox, oy) in enumerate(others)]


def _broadcast_start(placed, after, name):
    k = N_CHIPS - 1

    def body(buf, after_ref, *refs):
        send_sems, recv_sems, token = refs[:k], refs[k:2 * k], refs[-1]
        for cp in _broadcast_copies(buf, send_sems, recv_sems):
            cp.start()
        token[...] = jnp.zeros_like(token)

    out = pl.pallas_call(
        body, name=name,
        out_shape=[pltpu.SemaphoreType.DMA(())] * (2 * k) + [pltpu.HBM(placed.shape, placed.dtype),
                                                             jax.ShapeDtypeStruct((8, 128), F32)],
        in_specs=[HBM, ANY], out_specs=[SEM] * (2 * k) + [HBM, pl.BlockSpec(memory_space=pltpu.VMEM)],
        input_output_aliases={0: 2 * k},
        compiler_params=pltpu.CompilerParams(has_side_effects=DATAFLOW),
    )(pltpu.with_memory_space_constraint(placed, pltpu.HBM), after)
    return out[:2 * k], out[2 * k], out[-1]


def _broadcast_wait(sems, placed, after, name):
    k = N_CHIPS - 1

    def body(buf, *refs):
        send_sems, recv_sems = refs[:k], refs[k:2 * k]
        for cp in _broadcast_copies(buf, send_sems, recv_sems):
            cp.wait_send()
            cp.wait_recv()

    return pl.pallas_call(
        body, name=name, out_shape=pltpu.HBM(placed.shape, placed.dtype),
        in_specs=[HBM] + [SEM] * (2 * k) + [ANY], out_specs=HBM, input_output_aliases={0: 0},
        compiler_params=pltpu.CompilerParams(has_side_effects=DATAFLOW),
    )(placed, *sems, after)


def _scatter_copies(s, got, send_sems, recv_sems):
    x, y, c, _, others = _place()
    me = 2 * x + y
    n = len(s)
    return [_remote(s[i].at[2 * ox + oy], got[i].at[me], send_sems[3 * i + j], recv_sems[3 * i + j], (ox, oy, c))
            for i in range(n) for j, (ox, oy) in enumerate(others)]


def _scatter_start(sums, placed):
    n = len(sums)
    k = 3 * n

    def body(*refs):
        s, got, token = refs[:n], refs[n:2 * n], refs[-1]
        send_sems, recv_sems = refs[2 * n:2 * n + k], refs[2 * n + k:2 * n + 2 * k]
        for cp in _scatter_copies(s, got, send_sems, recv_sems):
            cp.start()
        token[...] = jnp.zeros_like(token)

    hbm = [pltpu.HBM(a.shape, a.dtype) for a in list(sums) + list(placed)]
    out = pl.pallas_call(
        body, name="scatter_start",
        out_shape=[pltpu.SemaphoreType.DMA(())] * (2 * k) + hbm + [jax.ShapeDtypeStruct((8, 128), F32)],
        in_specs=[HBM] * (2 * n), out_specs=[SEM] * (2 * k) + [HBM] * (2 * n) + [pl.BlockSpec(memory_space=pltpu.VMEM)],
        input_output_aliases={i: 2 * k + i for i in range(2 * n)},
        compiler_params=pltpu.CompilerParams(has_side_effects=DATAFLOW),
    )(*[pltpu.with_memory_space_constraint(a, pltpu.HBM) for a in list(sums) + list(placed)])
    return out[:2 * k], out[2 * k:2 * k + n], out[2 * k + n:2 * k + 2 * n], out[-1]


def _scatter_wait(sems, sums, placed, after):
    n = len(sums)
    k = 3 * n

    def body(*refs):
        s, got = refs[:n], refs[n:2 * n]
        send_sems, recv_sems = refs[2 * n:2 * n + k], refs[2 * n + k:2 * n + 2 * k]
        for cp in _scatter_copies(s, got, send_sems, recv_sems):
            cp.wait_send()
            cp.wait_recv()

    hbm = [pltpu.HBM(a.shape, a.dtype) for a in list(sums) + list(placed)]
    out = pl.pallas_call(
        body, name="scatter_wait", out_shape=hbm,
        in_specs=[HBM] * (2 * n) + [SEM] * (2 * k) + [ANY], out_specs=[HBM] * (2 * n),
        input_output_aliases={i: i for i in range(2 * n)},
        compiler_params=pltpu.CompilerParams(has_side_effects=DATAFLOW),
    )(*sums, *placed, *sems, after)
    return out[n:]


def _join_halves(placed):
    n = len(placed)

    def body(*refs):
        full = refs[n:2 * n]
        send_sems, recv_sems = refs[2 * n:]
        x, y, c, sibling, _ = _place()
        for i in range(n):
            _start_streams(lambda r, i=i: _remote(full[i].at[c, r], full[i].at[c, r], send_sems.at[i],
                                                  recv_sems.at[i], sibling), full[i].shape[1])
        for i in range(n):
            cp = _remote(full[i].at[c], full[i].at[1 - c], send_sems.at[i], recv_sems.at[i], sibling)
            cp.wait_recv()
            cp.wait_send()

    return pl.pallas_call(
        body, name="join_halves", out_shape=[jax.ShapeDtypeStruct(s.shape, s.dtype) for s in placed],
        in_specs=[ANY] * n, out_specs=[ANY] * n, input_output_aliases={i: i for i in range(n)},
        scratch_shapes=[pltpu.SemaphoreType.DMA((n,)), pltpu.SemaphoreType.DMA((n,))],
    )(*placed)


def _gather_rows(row):
    def body(row_ref, out_ref, send_sems, recv_sems, local_sem):
        x, y, c = lax.axis_index("x"), lax.axis_index("y"), lax.axis_index("c")
        me = 4 * x + 2 * y + c
        local = pltpu.make_async_copy(row_ref, out_ref.at[me], local_sem)
        local.start()
        sent = []
        peers = []
        for k in range(1, N_DEV):
            px, py, pc = x ^ (k >> 2), y ^ ((k >> 1) & 1), c ^ (k & 1)
            peers.append((k, px, py, pc))
            cp = _remote(row_ref, out_ref.at[me], send_sems.at[k - 1], recv_sems.at[k - 1], (px, py, pc))
            cp.start()
            sent.append(cp)
        for k, px, py, pc in peers:
            slot = out_ref.at[4 * px + 2 * py + pc]
            _remote(slot, slot, send_sems.at[k - 1], recv_sems.at[k - 1], (px, py, pc)).wait_recv()
        for cp in sent:
            cp.wait_send()
        local.wait()

    return pl.pallas_call(
        body, name="gather_rows", out_shape=jax.ShapeDtypeStruct((N_DEV,) + row.shape, row.dtype),
        in_specs=[ANY], out_specs=ANY,
        scratch_shapes=[pltpu.SemaphoreType.DMA((N_DEV - 1,)), pltpu.SemaphoreType.DMA((N_DEV - 1,)),
                        pltpu.SemaphoreType.DMA],
    )(row)


def _scalar(i):
    return jnp.reshape(i, (1,)).astype(jnp.int32)


def _place_block(src, n_slots, slot, out_dtype, name, window=0, n_windows=1):
    rows, cols = src.shape[0], src.shape[1] // n_windows
    tr = _divisor_tile(rows, max(16, (2 * MIB) // (cols * 4)), 16)

    def body(slot_ref, s_ref, o_ref):
        o_ref[...] = s_ref[...].astype(o_ref.dtype)

    return pl.pallas_call(
        body, name=name, out_shape=jax.ShapeDtypeStruct((n_slots, rows, cols), out_dtype),
        grid_spec=pltpu.PrefetchScalarGridSpec(
            num_scalar_prefetch=1, grid=(rows // tr,), in_specs=[pl.BlockSpec((tr, cols), lambda r, sl: (r, window))],
            out_specs=pl.BlockSpec((None, tr, cols), lambda r, sl: (sl[0], r, 0))),
        compiler_params=_params(("parallel",)))(_scalar(slot), src)


def _add_halves(g, theirs, core, chip, name):
    n, half, cols = theirs.shape
    tr = _divisor_tile(half, max(16, (2 * MIB) // (cols * 4)), 16)
    per = half // tr

    def body(at_ref, a_ref, b_ref, o_ref, own_ref):
        total = (a_ref[...].astype(F32) + b_ref[...].astype(F32)).astype(o_ref.dtype)
        o_ref[...] = total

        @pl.when(pl.program_id(1) == at_ref[1])
        def _():
            own_ref[...] = total

    spec = pl.BlockSpec((None, tr, cols), lambda r, i, at: (i, r, 0))
    shape = jax.ShapeDtypeStruct(theirs.shape, BF16)
    return pl.pallas_call(
        body, name=name, out_shape=[shape, shape],
        grid_spec=pltpu.PrefetchScalarGridSpec(
            num_scalar_prefetch=1, grid=(per, n),
            in_specs=[pl.BlockSpec((None, tr, cols), lambda r, i, at: (i, at[0] * per + r, 0)), spec],
            out_specs=[spec, pl.BlockSpec((None, tr, cols), lambda r, i, at: (at[1], r, 0))]),
        compiler_params=_params(("parallel", "arbitrary")))(jnp.concatenate([_scalar(core), _scalar(chip)]), g, theirs)


def _sum_slots(a, core, name):
    n, rows, cols = a.shape
    tr = _divisor_tile(rows, max(16, (2 * MIB) // (cols * 4 * n)), 16)

    def body(c_ref, a_ref, o_ref):
        acc = a_ref[0].astype(F32)
        for i in range(1, n):
            acc = acc + a_ref[i].astype(F32)
        o_ref[...] = acc

    return pl.pallas_call(
        body, name=name, out_shape=jax.ShapeDtypeStruct((2, rows, cols), F32),
        grid_spec=pltpu.PrefetchScalarGridSpec(
            num_scalar_prefetch=1, grid=(rows // tr,),
            in_specs=[pl.BlockSpec((n, tr, cols), lambda r, c: (0, r, 0))],
            out_specs=pl.BlockSpec((None, tr, cols), lambda r, c: (c[0], r, 0))),
        compiler_params=_params(("parallel",)))(_scalar(core), a)


def _adamw_math(w, g, m, v):
    m = ADAM_B1 * m + (1.0 - ADAM_B1) * g
    v = ADAM_B2 * v + (1.0 - ADAM_B2) * (g * g)
    m_hat = m / (1.0 - ADAM_B1 ** ADAM_STEP)
    v_hat = v / (1.0 - ADAM_B2 ** ADAM_STEP)
    delta = -ADAM_LR * (m_hat / (jnp.sqrt(v_hat) + ADAM_EPS) + ADAM_WD * w)
    return delta, m, v


def _adamw(w, g, m, v, name):
    rows, cols = w.shape
    tr = _divisor_tile(rows, max(8, MIB // (cols * 4)), 8)

    def body(w_ref, g_ref, m_ref, v_ref, d_ref, nm_ref, nv_ref):
        d, nm, nv = _adamw_math(w_ref[...], g_ref[...], m_ref[...], v_ref[...])
        d_ref[...] = d
        nm_ref[...] = nm
        nv_ref[...] = nv

    spec = pl.BlockSpec((tr, cols), lambda r: (r, 0))
    shape = jax.ShapeDtypeStruct((rows, cols), F32)
    return pl.pallas_call(body, name=name, grid=(rows // tr,), in_specs=[spec] * 4, out_specs=[spec] * 3,
                          out_shape=[shape] * 3, compiler_params=_params(("parallel",)))(w, g, m, v)


def _sum_rows_adamw(parts, w, m, v):
    def body(p_ref, w_ref, m_ref, v_ref, g_ref, d_ref, nm_ref, nv_ref):
        g = p_ref[0]
        for i in range(1, N_DEV):
            g = g + p_ref[i]
        d, nm, nv = _adamw_math(w_ref[...], g, m_ref[...], v_ref[...])
        g_ref[...] = g
        d_ref[...] = d
        nm_ref[...] = nm
        nv_ref[...] = nv

    shape = jax.ShapeDtypeStruct(w.shape, F32)
    return pl.pallas_call(body, name="sum_rows_adamw", out_shape=[shape] * 4)(parts, w, m, v)


LANES = 128


def _permute_scratch(rows, width):
    return pltpu.VMEM((width // LANES, rows, LANES), F32)


def _split_rows(value, scratch, dil):
    if dil == 1:
        return [value]
    rows = value.shape[0] // dil
    slabs = value.shape[1] // LANES
    for c in range(slabs):
        scratch[c] = value[:, c * LANES:(c + 1) * LANES]
    return [jnp.concatenate([scratch[c, pl.ds(r, rows, stride=dil), :] for c in range(slabs)], axis=1)
            for r in range(dil)]


def _merge_rows(ref, scratch, dil):
    if dil == 1:
        return ref[0].astype(F32)
    rows = ref.shape[1]
    slabs = ref.shape[2] // LANES
    for r in range(dil):
        part = ref[r].astype(F32)
        for c in range(slabs):
            scratch[c, pl.ds(r, rows, stride=dil), :] = part[:, c * LANES:(c + 1) * LANES]
    return jnp.concatenate([scratch[c] for c in range(slabs)], axis=1)


def _grouped_view(t, dil):
    return t.reshape(dil, t.shape[0] // dil, t.shape[1])


def _grouped_spec(dil, rows, width, index):
    return pl.BlockSpec((dil, rows // dil, width), index)


W_CHUNKS = 2


def _pick(values, j):
    out = values[-1]
    for i in range(len(values) - 2, -1, -1):
        out = jnp.where(j == i, values[i], out)
    return out


def _chunk_of(col, per_chip):
    return (col % per_chip) // (per_chip // W_CHUNKS)


def _w_block(col, per_chip):
    return col // per_chip, 0, (col % per_chip) % (per_chip // W_CHUNKS)


def _in_proj(xb, wc, blocks, j0, ncols, tn, out_dtype, prev, after, name):
    s, d = xb.shape
    per_chip = wc.shape[2] * W_CHUNKS // tn
    tm = _divisor_tile(s, 1024, 16)
    extra = [t for t in (after,) if t is not None]

    def body(*refs):
        a_ref, b_ref = refs[len(extra):len(extra) + 2]
        o_ref = refs[-1]
        o_ref[...] = jnp.dot(a_ref[...], b_ref[...], preferred_element_type=F32).astype(o_ref.dtype)

    in_specs = [pl.BlockSpec(t.shape, lambda j, m: (0, 0)) for t in extra] + [
        pl.BlockSpec((tm, d), lambda j, m: (m, 0)),
        pl.BlockSpec((None, d, tn), lambda j, m: _w_block(_pick(blocks, j), per_chip))]
    args = extra + [xb, wc]
    aliases = {}
    if prev is not None:
        aliases = {len(args): 0}
        in_specs.append(ANY)
        args.append(prev)
    return pl.pallas_call(
        body, name=name, grid=(len(blocks), s // tm), in_specs=in_specs,
        out_specs=pl.BlockSpec((tm, tn), lambda j, m: (m, _pick(blocks, j) - j0)),
        out_shape=jax.ShapeDtypeStruct((s, ncols), out_dtype), input_output_aliases=aliases,
        compiler_params=_params(("parallel", "parallel")))(*args)


def _in_proj_qkv(xb, wc, g, blocks, aw, tn, prev, after, name):
    s, d = xb.shape
    dil = DILATIONS[g]
    per_chip = wc.shape[2] * W_CHUNKS // tn
    sub = aw // tn
    tm = _divisor_tile(s, 1024, 16 * dil)
    extra = [t for t in (after,) if t is not None]

    def body(*refs):
        a_ref, b_ref = refs[len(extra):len(extra) + 2]
        o_ref, scratch = refs[-2:]
        res = jnp.dot(a_ref[...], b_ref[...], preferred_element_type=F32)
        for r, part in enumerate(_split_rows(res, scratch, dil)):
            o_ref[r] = part.astype(BF16)

    def out_index(j, m):
        col = _pick(blocks, j)
        return (col // sub) // N_GROUPS, 0, m, col % sub

    in_specs = [pl.BlockSpec(t.shape, lambda j, m: (0, 0)) for t in extra] + [
        pl.BlockSpec((tm, d), lambda j, m: (m, 0)),
        pl.BlockSpec((None, d, tn), lambda j, m: _w_block(_pick(blocks, j), per_chip))]
    args = extra + [xb, wc]
    aliases = {}
    if prev is not None:
        aliases = {len(args): 0}
        in_specs.append(ANY)
        args.append(prev)
    return pl.pallas_call(
        body, name=name, grid=(len(blocks), s // tm), in_specs=in_specs,
        out_specs=pl.BlockSpec((None, dil, tm // dil, tn), out_index),
        out_shape=jax.ShapeDtypeStruct((3, dil, s // dil, aw), BF16), input_output_aliases=aliases,
        scratch_shapes=[_permute_scratch(tm, tn)],
        compiler_params=_params(("parallel", "parallel")))(*args)


def _window_mask(first):
    qi = lax.broadcasted_iota(jnp.int32, (STEPS, 2 * STEPS), 0)
    kj = lax.broadcasted_iota(jnp.int32, (STEPS, 2 * STEPS), 1)
    lowest = jnp.where(first, STEPS, 0)
    return (kj >= qi) & (kj <= qi + STEPS) & (kj >= lowest)


def _attn_fwd(qkv, g):
    _, s, aw = qkv.shape
    heads = aw // HEAD_DIM
    n_blocks = s // STEPS
    per_seq = n_blocks // DILATIONS[g]

    def body(q_ref, kc_ref, kp_ref, vc_ref, vp_ref, o_ref, l_ref):
        mask = _window_mask(lax.rem(pl.program_id(0), per_seq) == 0)
        for h in range(heads):
            hs = slice(h * HEAD_DIM, (h + 1) * HEAD_DIM)
            kk = jnp.concatenate([kp_ref[:, hs], kc_ref[:, hs]], axis=0)
            vv = jnp.concatenate([vp_ref[:, hs], vc_ref[:, hs]], axis=0)
            sc = lax.dot_general(q_ref[:, hs], kk, NT, preferred_element_type=F32) * SCORE_SCALE
            sc = jnp.where(mask, sc, NEG_INF)
            mx = jnp.max(sc, axis=1, keepdims=True)
            e = jnp.exp(sc - mx)
            den = jnp.sum(e, axis=1, keepdims=True)
            o_ref[:, hs] = jnp.dot(e.astype(BF16), vv, preferred_element_type=F32) / den
            l_ref[:, hs] = jnp.broadcast_to(mx + jnp.log(den), (STEPS, HEAD_DIM))

    def cur(which):
        return pl.BlockSpec((None, STEPS, aw), lambda b: (which, b, 0))

    def prev(which):
        return pl.BlockSpec((None, STEPS, aw), lambda b: (which, jnp.maximum(b - 1, 0), 0))

    out = pl.BlockSpec((STEPS, aw), lambda b: (b, 0))
    shape = jax.ShapeDtypeStruct((s, aw), F32)
    return pl.pallas_call(
        body, name=f"attn_fwd{g}", grid=(n_blocks,),
        in_specs=[cur(0), cur(1), prev(1), cur(2), prev(2)], out_specs=[out, out], out_shape=[shape, shape],
        compiler_params=_params(("parallel",)))(qkv, qkv, qkv, qkv, qkv)


def _combine_groups(os, ls, zuz, aw):
    s = zuz.shape[0]
    tr = _divisor_tile(s, 256, 8 * DILATIONS[-1])

    def body(*refs):
        o_refs, l_refs, z_ref = refs[0:3], refs[3:6], refs[6]
        oo_ref, y_ref, yt_ref = refs[7:10]
        lq_refs, scratch = refs[10:13], refs[13]
        ls_ = [_merge_rows(l_refs[g], scratch, dil) for g, dil in enumerate(DILATIONS)]
        mx = jnp.maximum(jnp.maximum(ls_[0], ls_[1]), ls_[2])
        ws = [jnp.exp(l - mx) for l in ls_]
        den = ws[0] + ws[1] + ws[2]
        o = ws[0] * _merge_rows(o_refs[0], scratch, DILATIONS[0])
        for g in range(1, N_GROUPS):
            o = o + ws[g] * _merge_rows(o_refs[g], scratch, DILATIONS[g])
        o = o / den
        z = z_ref[...]
        y = o * (z * _sigmoid(z))
        oo_ref[...] = o
        y_ref[...] = y.astype(BF16)
        yt_ref[...] = y.T.astype(BF16)
        for g, dil in enumerate(DILATIONS):
            for r, part in enumerate(_split_rows(mx + jnp.log(den), scratch, dil)):
                lq_refs[g][r] = part

    grouped = [_grouped_spec(dil, tr, aw, lambda r: (0, r, 0)) for dil in DILATIONS]
    one = pl.BlockSpec((tr, aw), lambda r: (r, 0))
    f = jax.ShapeDtypeStruct((s, aw), F32)
    out = pl.pallas_call(
        body, name="combine_groups", grid=(s // tr,),
        in_specs=grouped + grouped + [one],
        out_specs=[one, one, pl.BlockSpec((aw, tr), lambda r: (0, r))] + grouped,
        out_shape=[f, jax.ShapeDtypeStruct((s, aw), BF16), jax.ShapeDtypeStruct((aw, s), BF16)]
        + [jax.ShapeDtypeStruct((dil, s // dil, aw), F32) for dil in DILATIONS],
        scratch_shapes=[_permute_scratch(tr, aw)],
        compiler_params=_params(("parallel",)))(
            *[_grouped_view(t, dil) for t, dil in zip(os, DILATIONS)],
            *[_grouped_view(t, dil) for t, dil in zip(ls, DILATIONS)], zuz)
    return out[0], out[1], out[2], [t.reshape(s, aw) for t in out[3:]]


def _pool_counts(row0, rows, window):
    t = row0 + lax.broadcasted_iota(jnp.int32, (rows, 1), 0)
    return jnp.minimum(t + 1, window).astype(F32)


def _pool_fwd(zuz, w_pool, pool_scale, aw, pw):
    s = zuz.shape[0]
    pg = pw // len(POOL_WINDOWS)
    tr = _divisor_tile(s, 256, 128)
    u_col, z_col = aw // pw, aw // pw + 1
    assert aw % pw == 0

    def body(u_ref, up_ref, z_ref, w_ref, sc_ref, p_ref, l_ref, y_ref, yt_ref):
        r = pl.program_id(0)
        u = u_ref[...]
        halo = jnp.where(r > 0, up_ref[...], 0.0)
        ext = jnp.concatenate([halo, u], axis=0)
        pieces, lins = [], []
        for gi, window in enumerate(POOL_WINDOWS):
            cs = slice(gi * pg, (gi + 1) * pg)
            acc = ext[:, cs]
            shift = 1
            while shift < window:
                acc = acc + pltpu.roll(acc, shift, 0)
                shift *= 2
            p = acc[POOL_HALO:] / _pool_counts(r * tr, tr, window) - u[:, cs]
            pieces.append(p)
            lins.append(jnp.dot(p.astype(BF16), w_ref[gi], preferred_element_type=F32))
        p = jnp.concatenate(pieces, axis=1)
        lin = jnp.concatenate(lins, axis=1)
        z = z_ref[...]
        y = lin * sc_ref[...] * (z * _sigmoid(z))
        p_ref[...] = p.astype(BF16)
        l_ref[...] = lin
        y_ref[...] = y.astype(BF16)
        yt_ref[...] = y.T.astype(BF16)

    per = tr // POOL_HALO
    out = pl.BlockSpec((tr, pw), lambda r: (r, 0))
    return pl.pallas_call(
        body, name="pool_fwd", grid=(s // tr,),
        in_specs=[pl.BlockSpec((tr, pw), lambda r: (r, u_col)),
                  pl.BlockSpec((POOL_HALO, pw), lambda r: (jnp.maximum(r * per - 1, 0), u_col)),
                  pl.BlockSpec((tr, pw), lambda r: (r, z_col)),
                  pl.BlockSpec((len(POOL_WINDOWS), pg, pg), lambda r: (0, 0, 0)),
                  pl.BlockSpec((1, pw), lambda r: (0, 0))],
        out_specs=[out, out, out, pl.BlockSpec((pw, tr), lambda r: (0, r))],
        out_shape=[jax.ShapeDtypeStruct((s, pw), BF16), jax.ShapeDtypeStruct((s, pw), F32),
                   jax.ShapeDtypeStruct((s, pw), BF16), jax.ShapeDtypeStruct((pw, s), BF16)],
        compiler_params=_params(("parallel",)))(zuz, zuz, zuz, w_pool, pool_scale)


def _proj_merge(y_attn, y_pool, wpa4, wpp4, gpre, b_gate):
    s, aw = y_attn.shape
    pw = y_pool.shape[1]
    tn = wpa4.shape[2]
    d = N_CHIPS * tn
    tm = _divisor_tile(s, 512, 128)

    def body(ya_ref, yp_ref, wa_ref, wp_ref, ga_ref, gp_ref, ba_ref, bp_ref, a_ref, p_ref, sa_ref, sp_ref, m_ref,
             mt_ref):
        a = jnp.dot(ya_ref[...], wa_ref[...], preferred_element_type=F32)
        p = jnp.dot(yp_ref[...], wp_ref[...], preferred_element_type=F32)
        sa = _sigmoid(ga_ref[...] + ba_ref[...])
        sp = _sigmoid(gp_ref[...] + bp_ref[...])
        merged = sa * a + sp * p
        a_ref[...] = a.astype(BF16)
        p_ref[...] = p.astype(BF16)
        sa_ref[...] = sa.astype(BF16)
        sp_ref[...] = sp.astype(BF16)
        m_ref[...] = merged.astype(BF16)
        mt_ref[...] = merged.T.astype(BF16)

    out = pl.BlockSpec((tm, tn), lambda n, m: (m, n))
    f = jax.ShapeDtypeStruct((s, d), BF16)
    return pl.pallas_call(
        body, name="proj_merge", grid=(N_CHIPS, s // tm),
        in_specs=[pl.BlockSpec((tm, aw), lambda n, m: (m, 0)), pl.BlockSpec((tm, pw), lambda n, m: (m, 0)),
                  pl.BlockSpec((None, aw, tn), lambda n, m: (n, 0, 0)),
                  pl.BlockSpec((None, pw, tn), lambda n, m: (n, 0, 0)),
                  pl.BlockSpec((tm, tn), lambda n, m: (m, n)), pl.BlockSpec((tm, tn), lambda n, m: (m, N_CHIPS + n)),
                  pl.BlockSpec((1, tn), lambda n, m: (0, n)), pl.BlockSpec((1, tn), lambda n, m: (0, N_CHIPS + n))],
        out_specs=[out] * 5 + [pl.BlockSpec((tn, tm), lambda n, m: (n, m))],
        out_shape=[f] * 5 + [jax.ShapeDtypeStruct((d, s), BF16)],
        compiler_params=_params(("parallel", "parallel")))(y_attn, y_pool, wpa4, wpp4, gpre, gpre, b_gate, b_gate)


def _out_norm_loss(merged, w_out, x, target, gamma, beta):
    s, d = x.shape
    tm = _divisor_tile(s, 256, 16)

    def body(m_ref, w_ref, x_ref, t_ref, g_ref, b_ref, dr_ref, drb_ref, loss_ref, dg_ref, db_ref):
        @pl.when(pl.program_id(0) == 0)
        def _():
            loss_ref[...] = jnp.zeros_like(loss_ref)
            dg_ref[...] = jnp.zeros_like(dg_ref)
            db_ref[...] = jnp.zeros_like(db_ref)

        r = ALPHA * x_ref[...] + jnp.dot(m_ref[...], w_ref[...], preferred_element_type=F32)
        mu = jnp.mean(r, axis=1, keepdims=True)
        rc = r - mu
        rstd = lax.rsqrt(jnp.mean(rc * rc, axis=1, keepdims=True) + LN_EPS)
        xhat = rc * rstd
        diff = xhat * g_ref[...] + b_ref[...] - t_ref[...]
        dy = diff / d
        loss_ref[...] += jnp.sum(diff * diff, axis=0, keepdims=True)
        dg_ref[...] += jnp.sum(dy * xhat, axis=0, keepdims=True)
        db_ref[...] += jnp.sum(dy, axis=0, keepdims=True)
        dxhat = dy * g_ref[...]
        dr = rstd * (dxhat - jnp.mean(dxhat, axis=1, keepdims=True)
                     - xhat * jnp.mean(dxhat * xhat, axis=1, keepdims=True))
        dr_ref[...] = dr
        drb_ref[...] = dr.astype(BF16)

    row = pl.BlockSpec((tm, d), lambda m: (m, 0))
    vec = pl.BlockSpec((1, d), lambda m: (0, 0))
    v = jax.ShapeDtypeStruct((1, d), F32)
    return pl.pallas_call(
        body, name="out_norm_loss", grid=(s // tm,),
        in_specs=[row, pl.BlockSpec((d, d), lambda m: (0, 0)), row, row, vec, vec],
        out_specs=[row, row, vec, vec, vec],
        out_shape=[jax.ShapeDtypeStruct((s, d), F32), jax.ShapeDtypeStruct((s, d), BF16), v, v, v],
        compiler_params=_params(("arbitrary",), vmem_mib=56))(merged, w_out, x, target, gamma, beta)


def _merge_bwd(drb, w_out, a, p, sa, sp):
    s, d = drb.shape
    tm = _divisor_tile(s, 512, 16)
    tn = d // N_CHIPS

    def body(dr_ref, w_ref, a_ref, p_ref, sa_ref, sp_ref, da_ref, dp_ref, dga_ref, dgp_ref, dba_ref, dbp_ref):
        @pl.when(pl.program_id(1) == 0)
        def _():
            dba_ref[...] = jnp.zeros_like(dba_ref)
            dbp_ref[...] = jnp.zeros_like(dbp_ref)

        dm = lax.dot_general(dr_ref[...], w_ref[...], NT, preferred_element_type=F32)
        sa = sa_ref[...].astype(F32)
        sp = sp_ref[...].astype(F32)
        da_ref[...] = (dm * sa).astype(BF16)
        dp_ref[...] = (dm * sp).astype(BF16)
        dga = dm * a_ref[...].astype(F32) * sa * (1.0 - sa)
        dgp = dm * p_ref[...].astype(F32) * sp * (1.0 - sp)
        dga_ref[...] = dga.astype(BF16)
        dgp_ref[...] = dgp.astype(BF16)
        dba_ref[...] += jnp.sum(dga, axis=0, keepdims=True)
        dbp_ref[...] += jnp.sum(dgp, axis=0, keepdims=True)

    blk = pl.BlockSpec((tm, tn), lambda n, m: (m, n))
    vec = pl.BlockSpec((1, tn), lambda n, m: (0, n))
    b16 = jax.ShapeDtypeStruct((s, d), BF16)
    v = jax.ShapeDtypeStruct((1, d), F32)
    return pl.pallas_call(
        body, name="merge_bwd", grid=(N_CHIPS, s // tm),
        in_specs=[pl.BlockSpec((tm, d), lambda n, m: (m, 0)), pl.BlockSpec((tn, d), lambda n, m: (n, 0)),
                  blk, blk, blk, blk],
        out_specs=[blk, blk, blk, blk, vec, vec], out_shape=[b16, b16, b16, b16, v, v],
        compiler_params=_params(("parallel", "arbitrary")))(drb, w_out, a, p, sa, sp)


def _proj_t(dy_ref, w_ref, tn):
    acc = None
    for n in range(N_CHIPS):
        t = lax.dot_general(dy_ref[:, n * tn:(n + 1) * tn], w_ref[n], NT, preferred_element_type=F32)
        acc = t if acc is None else acc + t
    return acc


def _attn_gate_bwd(da, wpa4, zuz, o):
    s, d = da.shape
    aw, tn = wpa4.shape[1], wpa4.shape[2]
    heads = aw // HEAD_DIM
    tm = _divisor_tile(s, 256, 16 * DILATIONS[-1])

    def body(*refs):
        da_ref, w_ref, z_ref, o_ref, dz_ref = refs[:5]
        do_refs, dd_refs, scratch = refs[5:8], refs[8:11], refs[11]
        dy = _proj_t(da_ref, w_ref, tn)
        z, o = z_ref[...], o_ref[...]
        sg = _sigmoid(z)
        do = dy * (z * sg)
        dz_ref[...] = (dy * o * _dsilu(z, sg)).astype(BF16)
        prod = do * o
        dd = jnp.concatenate(
            [jnp.broadcast_to(jnp.sum(prod[:, h * HEAD_DIM:(h + 1) * HEAD_DIM], axis=1, keepdims=True),
                              (tm, HEAD_DIM)) for h in range(heads)], axis=1)
        for g, dil in enumerate(DILATIONS):
            for r, part in enumerate(_split_rows(do, scratch, dil)):
                do_refs[g][r] = part.astype(BF16)
            for r, part in enumerate(_split_rows(dd, scratch, dil)):
                dd_refs[g][r] = part

    row = pl.BlockSpec((tm, aw), lambda m: (m, 0))
    grouped = [_grouped_spec(dil, tm, aw, lambda m: (0, m, 0)) for dil in DILATIONS]
    out = pl.pallas_call(
        body, name="attn_gate_bwd", grid=(s // tm,),
        in_specs=[pl.BlockSpec((tm, d), lambda m: (m, 0)), pl.BlockSpec((N_CHIPS, aw, tn), lambda m: (0, 0, 0)),
                  row, row],
        out_specs=[row] + grouped + grouped,
        out_shape=[jax.ShapeDtypeStruct((s, aw), BF16)]
        + [jax.ShapeDtypeStruct((dil, s // dil, aw), BF16) for dil in DILATIONS]
        + [jax.ShapeDtypeStruct((dil, s // dil, aw), F32) for dil in DILATIONS],
        scratch_shapes=[_permute_scratch(tm, aw)],
        compiler_params=_params(("parallel",)))(da, wpa4, zuz, o)
    return out[0], [t.reshape(s, aw) for t in out[1:4]], [t.reshape(s, aw) for t in out[4:7]]


def _pool_gate_bwd(dp_in, wpp4, zuz, lin, pooled, w_pool, pool_scale, aw):
    s, d = dp_in.shape
    pw, tn = wpp4.shape[1], wpp4.shape[2]
    n_win = len(POOL_WINDOWS)
    pg = pw // n_win
    tm = _divisor_tile(s, 256, 16)
    z_col = aw // pw + 1

    def body(dp_ref, w_ref, z_ref, l_ref, p_ref, wp_ref, sc_ref, dz_ref, dpo_ref, dw_ref, ds_ref):
        @pl.when(pl.program_id(0) == 0)
        def _():
            dw_ref[...] = jnp.zeros_like(dw_ref)
            ds_ref[...] = jnp.zeros_like(ds_ref)

        dy = _proj_t(dp_ref, w_ref, tn)
        z, lin_ = z_ref[...], l_ref[...]
        sg = _sigmoid(z)
        dypp = dy * (z * sg)
        dz_ref[...] = (dy * (lin_ * sc_ref[...]) * _dsilu(z, sg)).astype(BF16)
        ds_ref[...] += jnp.sum(dypp * lin_, axis=0, keepdims=True)
        dlin = (dypp * sc_ref[...]).astype(BF16)
        for gi in range(n_win):
            cs = slice(gi * pg, (gi + 1) * pg)
            pt = p_ref[:, cs].astype(F32).T.astype(BF16)
            dw_ref[gi] += jnp.dot(pt, dlin[:, cs], preferred_element_type=F32)
            dpo_ref[:, cs] = lax.dot_general(dlin[:, cs], wp_ref[gi], NT, preferred_element_type=F32)

    row = pl.BlockSpec((tm, pw), lambda m: (m, 0))
    return pl.pallas_call(
        body, name="pool_gate_bwd", grid=(s // tm,),
        in_specs=[pl.BlockSpec((tm, d), lambda m: (m, 0)), pl.BlockSpec((N_CHIPS, pw, tn), lambda m: (0, 0, 0)),
                  pl.BlockSpec((tm, pw), lambda m: (m, z_col)), row, row,
                  pl.BlockSpec((n_win, pg, pg), lambda m: (0, 0, 0)), pl.BlockSpec((1, pw), lambda m: (0, 0))],
        out_specs=[row, row, pl.BlockSpec((n_win, pg, pg), lambda m: (0, 0, 0)),
                   pl.BlockSpec((1, pw), lambda m: (0, 0))],
        out_shape=[jax.ShapeDtypeStruct((s, pw), BF16), jax.ShapeDtypeStruct((s, pw), F32),
                   jax.ShapeDtypeStruct((n_win, pg, pg), F32), jax.ShapeDtypeStruct((1, pw), F32)],
        compiler_params=_params(("arbitrary",)))(dp_in, wpp4, zuz, lin, pooled, w_pool, pool_scale)


def _pool_bwd(dpooled):
    s, pw = dpooled.shape
    pg = pw // len(POOL_WINDOWS)
    tr = _divisor_tile(s, 256, POOL_HALO)
    per = tr // POOL_HALO
    n_tiles = s // tr

    def body(c_ref, n_ref, du_ref):
        r = pl.program_id(0)
        cur = c_ref[...]
        halo = jnp.where(r < n_tiles - 1, n_ref[...], 0.0)
        ext = jnp.concatenate([cur, halo], axis=0)
        rows = tr + POOL_HALO
        for gi, window in enumerate(POOL_WINDOWS):
            cs = slice(gi * pg, (gi + 1) * pg)
            acc = ext[:, cs] / _pool_counts(r * tr, rows, window)
            shift = 1
            while shift < window:
                acc = acc + pltpu.roll(acc, rows - shift, 0)
                shift *= 2
            du_ref[:, cs] = (acc[:tr] - cur[:, cs]).astype(BF16)

    return pl.pallas_call(
        body, name="pool_bwd", grid=(n_tiles,),
        in_specs=[pl.BlockSpec((tr, pw), lambda r: (r, 0)),
                  pl.BlockSpec((POOL_HALO, pw), lambda r: (jnp.minimum((r + 1) * per, s // POOL_HALO - 1), 0))],
        out_specs=pl.BlockSpec((tr, pw), lambda r: (r, 0)),
        out_shape=jax.ShapeDtypeStruct((s, pw), BF16), compiler_params=_params(("parallel",)))(dpooled, dpooled)


def _attn_bwd(qkv, do, lse, dd, g):
    _, s, aw = qkv.shape
    heads = aw // HEAD_DIM
    n_blocks = s // STEPS
    per_seq = n_blocks // DILATIONS[g]

    def body(q_ref, do_ref, l_ref, dd_ref, kc_ref, kp_ref, vc_ref, vp_ref, out_ref, cq_ref, ck_ref, cv_ref):
        b = pl.program_id(0)

        @pl.when(b == 0)
        def _():
            cq_ref[...] = jnp.zeros_like(cq_ref)
            ck_ref[...] = jnp.zeros_like(ck_ref)
            cv_ref[...] = jnp.zeros_like(cv_ref)

        out_ref[0] = cq_ref[...].astype(BF16)

        @pl.when(b < n_blocks)
        def _():
            mask = _window_mask(lax.rem(b, per_seq) == 0)
            for h in range(heads):
                hs = slice(h * HEAD_DIM, (h + 1) * HEAD_DIM)
                q, do_ = q_ref[:, hs], do_ref[:, hs]
                kk = jnp.concatenate([kp_ref[:, hs], kc_ref[:, hs]], axis=0)
                vv = jnp.concatenate([vp_ref[:, hs], vc_ref[:, hs]], axis=0)
                lse_ = jnp.concatenate([l_ref[:, hs], l_ref[:, hs]], axis=1)
                dd_ = jnp.concatenate([dd_ref[:, hs], dd_ref[:, hs]], axis=1)
                sc = lax.dot_general(q, kk, NT, preferred_element_type=F32) * SCORE_SCALE
                prob = jnp.where(mask, jnp.exp(sc - lse_), 0.0)
                dprob = lax.dot_general(do_, vv, NT, preferred_element_type=F32)
                dsc = prob * (dprob - dd_) * SCORE_SCALE
                cq_ref[:, hs] = jnp.dot(dsc.astype(BF16), kk, preferred_element_type=F32)
                dkk = jnp.dot(dsc.T.astype(BF16), q, preferred_element_type=F32)
                dvv = jnp.dot(prob.T.astype(BF16), do_, preferred_element_type=F32)
                out_ref[1, :, hs] = (ck_ref[:, hs] + dkk[:STEPS]).astype(BF16)
                out_ref[2, :, hs] = (cv_ref[:, hs] + dvv[:STEPS]).astype(BF16)
                ck_ref[:, hs] = dkk[STEPS:]
                cv_ref[:, hs] = dvv[STEPS:]

        @pl.when(b == n_blocks)
        def _():
            out_ref[1] = ck_ref[...].astype(BF16)
            out_ref[2] = cv_ref[...].astype(BF16)

    last = n_blocks - 1

    def cur(which):
        return pl.BlockSpec((None, STEPS, aw), lambda b: (which, jnp.minimum(b, last), 0))

    def prev(which):
        return pl.BlockSpec((None, STEPS, aw), lambda b: (which, jnp.clip(b - 1, 0, last), 0))

    row = pl.BlockSpec((STEPS, aw), lambda b: (jnp.minimum(b, last), 0))
    return pl.pallas_call(
        body, name=f"attn_bwd{g}", grid=(n_blocks + 1,),
        in_specs=[cur(0), row, row, row, cur(1), prev(1), cur(2), prev(2)],
        out_specs=pl.BlockSpec((3, STEPS, aw), lambda b: (0, jnp.clip(b - 1, 0, last), 0)),
        out_shape=jax.ShapeDtypeStruct((3, s, aw), BF16),
        scratch_shapes=[pltpu.VMEM((STEPS, aw), F32)] * 3,
        compiler_params=_params(("arbitrary",)))(qkv, do, lse, dd, qkv, qkv, qkv, qkv)


def _weight_grad(at, b, tn, col_blocks, name):
    m, k = at.shape
    n = b.shape[1]
    tm = _divisor_tile(m, 1024, 16)
    tk = _divisor_tile(k, 2048, 128)
    nk = k // tk

    def body(a_ref, b_ref, o_ref, acc_ref):
        kk = pl.program_id(2)

        @pl.when(kk == 0)
        def _():
            acc_ref[...] = jnp.zeros_like(acc_ref)

        acc_ref[...] += jnp.dot(a_ref[...], b_ref[...], preferred_element_type=F32)

        @pl.when(kk == nk - 1)
        def _():
            o_ref[...] = acc_ref[...].astype(BF16)

    if col_blocks:
        out_spec = pl.BlockSpec((None, tm, tn), lambda i, j, kk: (j, i, 0))
        out_shape = jax.ShapeDtypeStruct((n // tn, m, tn), BF16)
    else:
        out_spec = pl.BlockSpec((tm, tn), lambda i, j, kk: (i, j))
        out_shape = jax.ShapeDtypeStruct((m, n), BF16)
    return pl.pallas_call(
        body, name=name, grid=(m // tm, n // tn, nk),
        in_specs=[pl.BlockSpec((tm, tk), lambda i, j, kk: (i, kk)), pl.BlockSpec((tk, tn), lambda i, j, kk: (kk, j))],
        out_specs=out_spec, out_shape=out_shape, scratch_shapes=[pltpu.VMEM((tm, tn), F32)],
        compiler_params=_params(("parallel", "parallel", "arbitrary")))(at, b)


def _w_in_grad_part(xt, b, col_of, n_local, tn, w_shape, prev, name):
    d, s = xt.shape
    per_chip = w_shape[2] // tn
    tm = _divisor_tile(d, 1024, 16)
    tk = _divisor_tile(s, 2048, 128)
    nk = s // tk

    def body(*refs):
        a_ref, b_ref, o_ref, acc_ref = refs[0], refs[1], refs[-2], refs[-1]
        kk = pl.program_id(2)

        @pl.when(kk == 0)
        def _():
            acc_ref[...] = jnp.zeros_like(acc_ref)

        acc_ref[...] += jnp.dot(a_ref[...], b_ref[...], preferred_element_type=F32)

        @pl.when(kk == nk - 1)
        def _():
            o_ref[...] = acc_ref[...].astype(BF16)

    if b.ndim == 3:
        sub = b.shape[2] // tn
        b_spec = pl.BlockSpec((None, tk, tn), lambda j, i, kk: (j // sub, kk, j % sub))
    else:
        b_spec = pl.BlockSpec((tk, tn), lambda j, i, kk: (kk, j))
    in_specs = [pl.BlockSpec((tm, tk), lambda j, i, kk: (i, kk)), b_spec]
    args = [xt, b]
    aliases = {}
    if prev is not None:
        in_specs.append(ANY)
        args.append(prev)
        aliases = {2: 0}
    return pl.pallas_call(
        body, name=name, grid=(n_local, d // tm, nk), in_specs=in_specs,
        out_specs=pl.BlockSpec((None, tm, tn), lambda j, i, kk: (col_of(j) // per_chip, i, col_of(j) % per_chip)),
        out_shape=jax.ShapeDtypeStruct(w_shape, BF16), scratch_shapes=[pltpu.VMEM((tm, tn), F32)],
        input_output_aliases=aliases,
        compiler_params=_params(("parallel", "parallel", "arbitrary")))(*args)


def _x_grad(dqkv, rest, wc, chunk, init, init_scale, aw, tn, after=None):
    s, d = init.shape
    sub = aw // tn
    n_qkv = 3 * N_GROUPS * sub
    los, lo = [], n_qkv
    for p in rest:
        los.append(lo)
        lo += p.shape[1] // tn
    per_chip = lo // N_CHIPS
    per = per_chip // W_CHUNKS
    n_local = lo // W_CHUNKS
    tm = _divisor_tile(s, 512, 16 * DILATIONS[-1])

    def col(jl):
        return (jl // per) * per_chip + chunk * per + jl % per

    ordered = [] if after is None else [after]

    def body(*refs):
        refs = refs[len(ordered):]
        q_refs, r_refs = refs[:N_GROUPS], refs[N_GROUPS:N_GROUPS + len(rest)]
        w_ref, init_ref, o_ref, acc_ref, scratch = refs[-5:]
        jl = pl.program_id(1)
        j = col(jl)

        @pl.when(jl == 0)
        def _():
            acc_ref[...] = init_scale * init_ref[...]

        for g, dil in enumerate(DILATIONS):
            @pl.when((j < n_qkv) & (lax.rem(j // sub, N_GROUPS) == g))
            def _(g=g, dil=dil):
                rows = _merge_rows(q_refs[g], scratch, dil).astype(BF16)
                acc_ref[...] += lax.dot_general(rows, w_ref[...], NT, preferred_element_type=F32)

        for p_ref, lo_, piece in zip(r_refs, los, rest):
            @pl.when((j >= lo_) & (j < lo_ + piece.shape[1] // tn))
            def _(p_ref=p_ref):
                acc_ref[...] += lax.dot_general(p_ref[...], w_ref[...], NT, preferred_element_type=F32)

        @pl.when(jl == n_local - 1)
        def _():
            o_ref[...] = acc_ref[...]

    def qkv_spec(dil):
        def index(i, jl):
            j = col(jl)
            region = jnp.minimum(j // sub, 3 * N_GROUPS - 1)
            return region // N_GROUPS, 0, i, jnp.where(j < n_qkv, j % sub, 0)

        return pl.BlockSpec((None, dil, tm // dil, tn), index)

    def rest_spec(lo_, piece):
        n = piece.shape[1] // tn
        return pl.BlockSpec((tm, tn), lambda i, jl: (i, jnp.clip(col(jl) - lo_, 0, n - 1)))

    row = pl.BlockSpec((tm, d), lambda i, jl: (i, 0))
    return pl.pallas_call(
        body, name=f"x_grad{chunk}", grid=(s // tm, n_local),
        in_specs=[pl.BlockSpec(t.shape, lambda i, jl: (0, 0)) for t in ordered]
        + [qkv_spec(dil) for dil in DILATIONS] + [rest_spec(lo_, p) for lo_, p in zip(los, rest)]
        + [pl.BlockSpec((None, d, tn), lambda i, jl: (jl // per, 0, jl % per)), row],
        out_specs=row, out_shape=jax.ShapeDtypeStruct((s, d), F32),
        scratch_shapes=[pltpu.VMEM((tm, d), F32), _permute_scratch(tm, tn)],
        compiler_params=_params(("parallel", "arbitrary"), vmem_mib=56))(
            *ordered, *[t.reshape(3, dil, s // dil, aw) for t, dil in zip(dqkv, DILATIONS)], *rest, wc, init)


def _to_subsequences(t, dilation):
    s, w = t.shape
    return t.reshape(s // dilation, dilation, w).transpose(1, 0, 2).reshape(s, w)


def _prepare_x(x, after=None):
    s, d = x.shape
    tc = 2 * LANES
    slabs = tc // LANES
    ordered = [] if after is None else [after]

    def body(*refs):
        x_ref, xb_ref = refs[len(ordered):len(ordered) + 2]
        xt_refs, scratch = refs[len(ordered) + 2:len(ordered) + 2 + N_GROUPS], refs[-1]
        t = x_ref[...]
        xb_ref[...] = t.astype(BF16)
        for c in range(slabs):
            scratch[c] = t[:, c * LANES:(c + 1) * LANES]
        for g, dil in enumerate(DILATIONS):
            length = s // dil
            for r in range(dil):
                part = t if dil == 1 else jnp.concatenate(
                    [scratch[c, pl.ds(r, length, stride=dil), :] for c in range(slabs)], axis=1)
                xt_refs[g][:, r * length:(r + 1) * length] = part.T.astype(BF16)

    col = pl.BlockSpec((s, tc), lambda j: (0, j))
    row = pl.BlockSpec((tc, s), lambda j: (j, 0))
    t_shape = jax.ShapeDtypeStruct((d, s), BF16)
    out = pl.pallas_call(
        body, name="prepare_x", grid=(d // tc,),
        in_specs=[pl.BlockSpec(t.shape, lambda j: (0, 0)) for t in ordered] + [col],
        out_specs=[col] + [row] * N_GROUPS,
        out_shape=[jax.ShapeDtypeStruct((s, d), BF16)] + [t_shape] * N_GROUPS,
        scratch_shapes=[_permute_scratch(s, tc)], compiler_params=_params(("parallel",)))(*ordered, x)
    return out[0], out[1:]


def _local_step(x, target, w_chunk, w_width, b_gate, pool_scale, gamma, beta, aw, pw, small_weights,
                start_exchange=None):
    s, d = x.shape
    tn = _col_tile(aw, pw, w_width)
    sub = aw // tn
    per_chip = w_width // tn
    qkv_w = 3 * N_GROUPS * aw
    w_shape = (N_CHIPS, d, w_width)

    regions = [dict(kind=g, blocks=[(which * N_GROUPS + g) * sub + i for which in range(3) for i in range(sub)])
               for g in range(N_GROUPS)]
    lo = qkv_w // tn
    for name, width in (("zuz", aw + 2 * pw), ("gates", 2 * d)):
        regions.append(dict(kind=name, blocks=list(range(lo, lo + width // tn)), j0=lo, width=width))
        lo += width // tn
    results = [None] * len(regions)
    wcs, after = [], None
    for ch in range(W_CHUNKS):
        wc, token = w_chunk(ch, after)
        wcs.append(wc)
        if ch == 0:
            xb, xts = _prepare_x(x, token)
            token = None
        for i, region in enumerate(regions):
            blocks = [b for b in region["blocks"] if _chunk_of(b, per_chip) == ch]
            if not blocks:
                continue
            if region["kind"] in range(N_GROUPS):
                results[i] = _in_proj_qkv(xb, wc, region["kind"], blocks, aw, tn, results[i], token,
                                          f"in_proj_qkv{region['kind']}_{ch}")
            else:
                results[i] = _in_proj(xb, wc, blocks, region["j0"], region["width"], tn, F32, results[i], token,
                                      f"in_proj_{region['kind']}_{ch}")
            token = None
            after = results[i]
    qkv = [results[g].reshape(3, s, aw) for g in range(N_GROUPS)]
    zuz, gpre = results[N_GROUPS], results[N_GROUPS + 1]

    attn = [_attn_fwd(qkv[g], g) for g in range(N_GROUPS)]
    o, y_attn, y_attn_t, lse = _combine_groups([a[0] for a in attn], [a[1] for a in attn], zuz, aw)
    w_pool, wpa4, wpp4, w_out = small_weights(o)
    pooled, lin, y_pool, y_pool_t = _pool_fwd(zuz, w_pool, pool_scale, aw, pw)
    a, p, sa, sp, merged, merged_t = _proj_merge(y_attn, y_pool, wpa4, wpp4, gpre, b_gate)
    dr, drb, loss_lanes, d_gamma, d_beta = _out_norm_loss(merged, w_out, x, target, gamma, beta)

    da, dp, d_gpre_a, d_gpre_p, d_b_a, d_b_p = _merge_bwd(drb, w_out, a, p, sa, sp)
    d_b_gate = jnp.concatenate([d_b_a, d_b_p], axis=1)
    d_w_out = _weight_grad(merged_t, drb, d // N_CHIPS, False, "w_out_grad")
    d_wpa4 = _weight_grad(y_attn_t, da, d // N_CHIPS, True, "w_proj_attn_grad")
    d_wpp4 = _weight_grad(y_pool_t, dp, d // N_CHIPS, True, "w_proj_pool_grad")
    d_z_attn, d_o, dd = _attn_gate_bwd(da, wpa4, zuz, o)
    d_z_pool, d_pooled, d_w_pool, d_pool_scale = _pool_gate_bwd(dp, wpp4, zuz, lin, pooled, w_pool, pool_scale, aw)
    d_u = _pool_bwd(d_pooled)
    dqkv = [_attn_bwd(qkv[g], d_o[g], lse[g], dd[g], g) for g in range(N_GROUPS)]

    rest = [d_z_attn, d_u, d_z_pool, d_gpre_a, d_gpre_p]
    d_w_in4 = None
    for g in range(N_GROUPS):
        d_w_in4 = _w_in_grad_part(xts[g], dqkv[g], lambda j, g=g: ((j // sub) * N_GROUPS + g) * sub + j % sub,
                                  3 * sub, tn, w_shape, d_w_in4, f"w_in_grad_qkv{g}")
    lo = qkv_w // tn
    for i, piece in enumerate(rest):
        n_local = piece.shape[1] // tn
        d_w_in4 = _w_in_grad_part(xts[0], piece, lambda j, lo=lo: lo + j, n_local, tn, w_shape, d_w_in4,
                                  f"w_in_grad_rest{i}")
        lo += n_local
    grads = dict(loss_lanes=loss_lanes, w_in=d_w_in4, b_gate=d_b_gate, w_pool=d_w_pool,
                 pool_scale=d_pool_scale, w_proj_attn=d_wpa4, w_proj_pool=d_wpp4, w_out=d_w_out,
                 ln_gamma=d_gamma, ln_beta=d_beta)
    token = None if start_exchange is None else start_exchange(grads)
    d_x, scale = dr, ALPHA
    for ch in range(W_CHUNKS):
        d_x = _x_grad(dqkv, rest, wcs[ch], ch, d_x, scale, aw, tn, token)
        token, scale = None, 1.0
    grads["d_x"] = d_x
    return grads


def _pack_small(wpa, wpp, w_out, w_pool):
    width = wpa.shape[1]
    return jnp.concatenate([wpa, wpp, w_out.reshape(-1, width), w_pool.reshape(-1, width)], axis=0)


def _unpack_small(packed, aw, pw, d, pg):
    lead = packed.shape[:-2]
    width = d // N_CHIPS
    r0, r1, r2 = aw, aw + pw, aw + pw + d
    return (packed[..., :r0, :], packed[..., r0:r1, :], packed[..., r1:r2, :].reshape(lead + (width, d)),
            packed[..., r2:, :].reshape(lead + (len(POOL_WINDOWS), pg // N_CHIPS, pg)))


def _pack_rows(vectors, rows):
    flat = jnp.concatenate([v.reshape(-1) for v in vectors])
    return jnp.pad(flat, (0, rows * 128 - flat.shape[0])).reshape(rows, 128)


def _unpack_rows(packed, sizes):
    flat, out, lo = packed.reshape(-1), [], 0
    for n in sizes:
        out.append(flat[lo:lo + n].reshape(1, n))
        lo += n
    return out


def kernel(x, w_in, b_gate, w_pool, pool_scale, w_proj_attn, w_proj_pool, w_out, ln_gamma, ln_beta, loss_target, m_w_in, m_b_gate, m_w_pool, m_pool_scale, m_w_proj_attn, m_w_proj_pool, m_w_out, m_ln_gamma, m_ln_beta, v_w_in, v_b_gate, v_w_pool, v_pool_scale, v_w_proj_attn, v_w_proj_pool, v_w_out, v_ln_gamma, v_ln_beta):
    s, d = x.shape[1], x.shape[2]
    aw, pw = w_proj_attn.shape[1], w_proj_pool.shape[1]
    pg = w_pool.shape[3]
    n_win = len(POOL_WINDOWS)

    def small(wpa, wpp, wo, wpl):
        return _pack_small(wpa[0], wpp[0], wo[0], wpl[0])

    chip = 2 * lax.axis_index("x") + lax.axis_index("y")
    core = lax.axis_index("c")

    w_small = small(w_proj_attn, w_proj_pool, w_out, w_pool)
    placed = [_place_block(w_in[0], N_CHIPS, chip, BF16, f"place_w_in{ch}", ch, W_CHUNKS) for ch in range(W_CHUNKS)]
    placed_small = _place_block(w_small, N_CHIPS, chip, BF16, "place_w_small")
    gathered, = _gather_weights([placed[0]])
    flight = {"chunk": _halves_start(placed[1], gathered, "gather_w_in1_start")}

    def w_chunk(ch, after):
        if ch == 0:
            return gathered, flight["chunk"][2]
        sems, thru, _ = flight["chunk"]
        landed = _halves_wait(sems, thru, after, f"gather_w_in{ch}_wait")
        if ch + 1 < W_CHUNKS:
            flight["chunk"] = _halves_start(placed[ch + 1], landed, f"gather_w_in{ch + 1}_start")
            token = flight["chunk"][2]
        else:
            flight["small"] = _broadcast_start(placed_small, landed, "gather_small_start")
            token = flight["small"][2]
        return _forward_halves(landed, f"forward_w_in{ch}"), token

    def small_weights(after):
        sems, thru, _ = flight["small"]
        small4 = _broadcast_wait(sems, thru, after, "gather_small_wait")
        wpa4, wpp4, w_out4, w_pool4 = _unpack_small(small4, aw, pw, d, pg)
        return w_pool4.transpose(1, 0, 2, 3).reshape(n_win, pg, pg), wpa4, wpp4, w_out4.reshape(d, d)

    exchange = {}

    def start_exchange(g):
        g_pool4 = g["w_pool"].reshape(n_win, N_CHIPS, pg // N_CHIPS, pg).transpose(1, 0, 2, 3).astype(BF16)
        g_out4 = g["w_out"].reshape(N_CHIPS, d // N_CHIPS, d)
        g_small4 = jnp.concatenate([g["w_proj_attn"], g["w_proj_pool"], g_out4.reshape(N_CHIPS, -1, d // N_CHIPS),
                                    g_pool4.reshape(N_CHIPS, -1, d // N_CHIPS)], axis=1)
        theirs_big, theirs_small = _swap_halves([g["w_in"], g_small4])
        chip_big, placed_big = _add_halves(g["w_in"], theirs_big, core, chip, "add_cores_big")
        chip_small, placed_small = _add_halves(g_small4, theirs_small, core, chip, "add_cores_small")
        sems, sums, placed, token = _scatter_start([chip_big, chip_small], [placed_big, placed_small])
        exchange.update(sems=sems, sums=sums, placed=placed)
        return token

    g = _local_step(x[0], loss_target[0], w_chunk, w_in.shape[2], b_gate, pool_scale, ln_gamma, ln_beta, aw, pw,
                    small_weights, start_exchange)
    got_big, got_small = _scatter_wait(exchange["sems"], exchange["sums"], exchange["placed"], g["d_x"])
    grad_w_in, grad_small = _join_halves([_sum_slots(got_big, core, "sum_chips_big"),
                                          _sum_slots(got_small, core, "sum_chips_small")])
    grad_w_in = grad_w_in.reshape(-1, grad_w_in.shape[2])
    grad_small = grad_small.reshape(-1, grad_small.shape[2])

    sizes = [b_gate.shape[1], pool_scale.shape[1], d, d, 1]
    rows = -(-sum(sizes) // (8 * 128)) * 8
    loss_part = (0.5 / d) * jnp.sum(g["loss_lanes"]).reshape(1, 1)
    parts = _gather_rows(_pack_rows([g["b_gate"], g["pool_scale"], g["ln_gamma"], g["ln_beta"], loss_part], rows))
    zero = jnp.zeros((1, 1), F32)
    packed = [_pack_rows(vs, rows) for vs in ([b_gate, pool_scale, ln_gamma, ln_beta, zero],
                                              [m_b_gate, m_pool_scale, m_ln_gamma, m_ln_beta, zero],
                                              [v_b_gate, v_pool_scale, v_ln_gamma, v_ln_beta, zero])]
    rep = [_unpack_rows(t, sizes) for t in _sum_rows_adamw(parts, *packed)]
    loss = rep[0][4].reshape(())

    upd_in = _adamw(w_in[0], grad_w_in, m_w_in[0], v_w_in[0], "adamw_w_in")
    upd_small = _adamw(w_small, grad_small, small(m_w_proj_attn, m_w_proj_pool, m_w_out, m_w_pool),
                       small(v_w_proj_attn, v_w_proj_pool, v_w_out, v_w_pool), "adamw_small")

    def leaves(big, packed_small, replicated):
        wpa_, wpp_, wo_, wpl_ = _unpack_small(packed_small, aw, pw, d, pg)
        return [big[None], replicated[0], wpl_[None], replicated[1], wpa_[None], wpp_[None], wo_[None],
                replicated[2], replicated[3]]

    out = [loss, g["d_x"][None]]
    out += leaves(grad_w_in, grad_small, rep[0])
    for i in range(3):
        out += leaves(upd_in[i], upd_small[i], rep[1 + i])
    return tuple(out)
```

```python
import math

import jax
import jax.numpy as jnp
from jax import lax
from jax.experimental import pallas as pl
from jax.experimental.pallas import tpu as pltpu

F32 = jnp.float32
BF16 = jnp.bfloat16
MESH = pl.DeviceIdType.MESH
ANY = pl.BlockSpec(memory_space=pl.ANY)

HEAD_DIM = 128
STEPS = 128
DILATIONS = (1, 4, 16)
N_GROUPS = len(DILATIONS)
POOL_WINDOWS = (2, 4, 8, 16)
POOL_HALO = 16
N_CHIPS = 4
N_DEV = 8
ALPHA = 2.0 ** 0.25
LN_EPS = 1e-5
NEG_INF = -1e30
SCORE_SCALE = HEAD_DIM ** -0.5
ADAM_LR = 0.001
ADAM_B1 = 0.9
ADAM_B2 = 0.999
ADAM_EPS = 1e-08
ADAM_WD = 0.01
ADAM_STEP = 10
MIB = 2 ** 20
NT = (((1,), (1,)), ((), ()))
DMA_STREAMS = 8


def _params(semantics=None, vmem_mib=48):
    return pltpu.CompilerParams(dimension_semantics=semantics, vmem_limit_bytes=vmem_mib * MIB)


def _divisor_tile(n, target, multiple):
    best = None
    for t in range(multiple, min(n, target) + 1, multiple):
        if n % t == 0:
            best = t
    assert best is not None, (n, target, multiple)
    return best


def _col_tile(*widths):
    g = 0
    for w in widths:
        g = math.gcd(g, w)
    return _divisor_tile(g, 1024, 128)


def _sigmoid(z):
    return jax.nn.sigmoid(z)


def _dsilu(z, sg):
    return sg * (1.0 + z * (1.0 - sg))


def _place():
    x, y, c = lax.axis_index("x"), lax.axis_index("y"), lax.axis_index("c")
    others = [(1 - x, y), (x, 1 - y), (1 - x, 1 - y)]
    return x, y, c, (x, y, 1 - c), others


def _remote(src, dst, send_sem, recv_sem, dev):
    return pltpu.make_async_remote_copy(src_ref=src, dst_ref=dst, send_sem=send_sem, recv_sem=recv_sem,
                                        device_id=dev, device_id_type=MESH)


def _row_pieces(n_rows, streams=DMA_STREAMS, multiple=16):
    size = -(-n_rows // (streams * multiple)) * multiple
    return [(lo, min(size, n_rows - lo)) for lo in range(0, n_rows, size)]


def _start_streams(make, n_rows):
    for lo, size in _row_pieces(n_rows):
        make(pl.ds(lo, size)).start()


def _gather_weights(placed):
    n = len(placed)

    def body(*refs):
        dst = refs[n:2 * n]
        send_sems, recv_sems = refs[2 * n:]
        x, y, c, sibling, others = _place()
        me = 2 * x + y
        sent = []
        for i in range(n):
            half = dst[i].shape[1] // 2
            mine = c * half
            for j, (ox, oy) in enumerate(others):
                slab = dst[i].at[me, pl.ds(mine, half)]
                cp = _remote(slab, slab, send_sems.at[6 * i + j], recv_sems.at[6 * i + j], (ox, oy, c))
                cp.start()
                sent.append(cp)
        for i in range(n):
            half = dst[i].shape[1] // 2
            mine = c * half
            for j, (ox, oy) in enumerate(others):
                blk = dst[i].at[2 * ox + oy]
                slab = blk.at[pl.ds(mine, half)]
                _remote(slab, slab, send_sems.at[6 * i + j], recv_sems.at[6 * i + j], (ox, oy, c)).wait_recv()
                k = 6 * i + 3 + j
                _start_streams(lambda r, blk=blk, k=k: _remote(blk.at[pl.ds(mine + r.start, r.size)],
                                                               blk.at[pl.ds(mine + r.start, r.size)],
                                                               send_sems.at[k], recv_sems.at[k], sibling), half)
                sent.append(_remote(slab, slab, send_sems.at[k], recv_sems.at[k], sibling))
        for i in range(n):
            half = dst[i].shape[1] // 2
            for j, (ox, oy) in enumerate(others):
                slab = dst[i].at[2 * ox + oy, pl.ds((1 - c) * half, half)]
                _remote(slab, slab, send_sems.at[6 * i + 3 + j], recv_sems.at[6 * i + 3 + j], sibling).wait_recv()
        for cp in sent:
            cp.wait_send()

    return pl.pallas_call(
        body, name="gather_weights", out_shape=[jax.ShapeDtypeStruct(s.shape, s.dtype) for s in placed],
        in_specs=[ANY] * n, out_specs=[ANY] * n, input_output_aliases={i: i for i in range(n)},
        scratch_shapes=[pltpu.SemaphoreType.DMA((6 * n,)), pltpu.SemaphoreType.DMA((6 * n,))],
    )(*placed)


def _half_copies(buf, send_sems, recv_sems):
    x, y, c, _, others = _place()
    half = buf.shape[1] // 2
    slab = buf.at[2 * x + y, pl.ds(c * half, half)]
    return [_remote(slab, slab, send_sems[j], recv_sems[j], (ox, oy, c)) for j, (ox, oy) in enumerate(others)]


def _halves_start(placed, after, name):
    k = N_CHIPS - 1

    def body(buf, after_ref, *refs):
        send_sems, recv_sems, token = refs[:k], refs[k:2 * k], refs[-1]
        for cp in _half_copies(buf, send_sems, recv_sems):
            cp.start()
        token[...] = jnp.zeros_like(token)

    out = pl.pallas_call(
        body, name=name,
        out_shape=[pltpu.SemaphoreType.DMA(())] * (2 * k) + [pltpu.HBM(placed.shape, placed.dtype),
                                                             jax.ShapeDtypeStruct((8, 128), F32)],
        in_specs=[HBM, ANY], out_specs=[SEM] * (2 * k) + [HBM, pl.BlockSpec(memory_space=pltpu.VMEM)],
        input_output_aliases={0: 2 * k},
        compiler_params=pltpu.CompilerParams(has_side_effects=DATAFLOW),
    )(pltpu.with_memory_space_constraint(placed, pltpu.HBM), after)
    return out[:2 * k], out[2 * k], out[-1]


def _halves_wait(sems, placed, after, name):
    k = N_CHIPS - 1

    def body(buf, *refs):
        send_sems, recv_sems = refs[:k], refs[k:2 * k]
        for cp in _half_copies(buf, send_sems, recv_sems):
            cp.wait_send()
            cp.wait_recv()

    return pl.pallas_call(
        body, name=name, out_shape=pltpu.HBM(placed.shape, placed.dtype),
        in_specs=[HBM] + [SEM] * (2 * k) + [ANY], out_specs=HBM, input_output_aliases={0: 0},
        compiler_params=pltpu.CompilerParams(has_side_effects=DATAFLOW),
    )(placed, *sems, after)


def _forward_halves(buf, name):
    def body(_, dst, send_sems, recv_sems):
        x, y, c, sibling, others = _place()
        half = dst.shape[1] // 2
        for j, (ox, oy) in enumerate(others):
            slab = dst.at[2 * ox + oy, pl.ds(c * half, half)]
            _remote(slab, slab, send_sems.at[j], recv_sems.at[j], sibling).start()
        for j, (ox, oy) in enumerate(others):
            mine = dst.at[2 * ox + oy, pl.ds(c * half, half)]
            theirs = dst.at[2 * ox + oy, pl.ds((1 - c) * half, half)]
            cp = _remote(mine, theirs, send_sems.at[j], recv_sems.at[j], sibling)
            cp.wait_recv()
            cp.wait_send()

    return pl.pallas_call(
        body, name=name, out_shape=jax.ShapeDtypeStruct(buf.shape, buf.dtype),
        in_specs=[ANY], out_specs=ANY, input_output_aliases={0: 0},
        scratch_shapes=[pltpu.SemaphoreType.DMA((N_CHIPS - 1,)), pltpu.SemaphoreType.DMA((N_CHIPS - 1,))],
    )(buf)


def _swap_halves(grads):
    n = len(grads)

    def body(*refs):
        g, theirs = refs[:n], refs[n:2 * n]
        send_sems, recv_sems = refs[2 * n:]
        x, y, c, sibling, _ = _place()
        for i in range(n):
            half = g[i].shape[1] // 2
            give = (1 - c) * half
            for b in range(N_CHIPS):
                _start_streams(lambda r, i=i, b=b: _remote(
                    g[i].at[b, pl.ds(give + r.start, r.size)], theirs[i].at[b, r], send_sems.at[i], recv_sems.at[i],
                    sibling), half)
        for i in range(n):
            _remote(theirs[i], theirs[i], send_sems.at[i], recv_sems.at[i], sibling).wait()

    return pl.pallas_call(
        body, name="swap_halves",
        out_shape=[jax.ShapeDtypeStruct((s.shape[0], s.shape[1] // 2) + s.shape[2:], s.dtype) for s in grads],
        in_specs=[ANY] * n, out_specs=[ANY] * n,
        scratch_shapes=[pltpu.SemaphoreType.DMA((n,)), pltpu.SemaphoreType.DMA((n,))],
    )(*grads)


HBM = pl.BlockSpec(memory_space=pltpu.HBM)
SEM = pl.BlockSpec(memory_space=pltpu.SEMAPHORE)
DATAFLOW = pltpu.SideEffectType.DATAFLOW_SIDE_EFFECTING


def _broadcast_copies(buf, send_sems, recv_sems):
    x, y, c, _, others = _place()
    mine = buf.at[2 * x + y]
    return [_remote(mine, mine, send_sems[j], recv_sems[j], (ox, oy, c)) for j, (ox, oy) in enumerate(others)]


def _broadcast_start(placed, after, name):
    k = N_CHIPS - 1

    def body(buf, after_ref, *refs):
        send_sems, recv_sems, token = refs[:k], refs[k:2 * k], refs[-1]
        for cp in _broadcast_copies(buf, send_sems, recv_sems):
            cp.start()
        token[...] = jnp.zeros_like(token)

    out = pl.pallas_call(
        body, name=name,
        out_shape=[pltpu.SemaphoreType.DMA(())] * (2 * k) + [pltpu.HBM(placed.shape, placed.dtype),
                                                             jax.ShapeDtypeStruct((8, 128), F32)],
        in_specs=[HBM, ANY], out_specs=[SEM] * (2 * k) + [HBM, pl.BlockSpec(memory_space=pltpu.VMEM)],
        input_output_aliases={0: 2 * k},
        compiler_params=pltpu.CompilerParams(has_side_effects=DATAFLOW),
    )(pltpu.with_memory_space_constraint(placed, pltpu.HBM), after)
    return out[:2 * k], out[2 * k], out[-1]


def _broadcast_wait(sems, placed, after, name):
    k = N_CHIPS - 1

    def body(buf, *refs):
        send_sems, recv_sems = refs[:k], refs[k:2 * k]
        for cp in _broadcast_copies(buf, send_sems, recv_sems):
            cp.wait_send()
            cp.wait_recv()

    return pl.pallas_call(
        body, name=name, out_shape=pltpu.HBM(placed.shape, placed.dtype),
        in_specs=[HBM] + [SEM] * (2 * k) + [ANY], out_specs=HBM, input_output_aliases={0: 0},
        compiler_params=pltpu.CompilerParams(has_side_effects=DATAFLOW),
    )(placed, *sems, after)


def _scatter_copies(s, got, send_sems, recv_sems):
    x, y, c, _, others = _place()
    me = 2 * x + y
    n = len(s)
    return [_remote(s[i].at[2 * ox + oy], got[i].at[me], send_sems[3 * i + j], recv_sems[3 * i + j], (ox, oy, c))
            for i in range(n) for j, (ox, oy) in enumerate(others)]


def _scatter_start(sums, placed):
    n = len(sums)
    k = 3 * n

    def body(*refs):
        s, got, token = refs[:n], refs[n:2 * n], refs[-1]
        send_sems, recv_sems = refs[2 * n:2 * n + k], refs[2 * n + k:2 * n + 2 * k]
        for cp in _scatter_copies(s, got, send_sems, recv_sems):
            cp.start()
        token[...] = jnp.zeros_like(token)

    hbm = [pltpu.HBM(a.shape, a.dtype) for a in list(sums) + list(placed)]
    out = pl.pallas_call(
        body, name="scatter_start",
        out_shape=[pltpu.SemaphoreType.DMA(())] * (2 * k) + hbm + [jax.ShapeDtypeStruct((8, 128), F32)],
        in_specs=[HBM] * (2 * n), out_specs=[SEM] * (2 * k) + [HBM] * (2 * n) + [pl.BlockSpec(memory_space=pltpu.VMEM)],
        input_output_aliases={i: 2 * k + i for i in range(2 * n)},
        compiler_params=pltpu.CompilerParams(has_side_effects=DATAFLOW),
    )(*[pltpu.with_memory_space_constraint(a, pltpu.HBM) for a in list(sums) + list(placed)])
    return out[:2 * k], out[2 * k:2 * k + n], out[2 * k + n:2 * k + 2 * n], out[-1]


def _scatter_wait(sems, sums, placed, after):
    n = len(sums)
    k = 3 * n

    def body(*refs):
        s, got = refs[:n], refs[n:2 * n]
        send_sems, recv_sems = refs[2 * n:2 * n + k], refs[2 * n + k:2 * n + 2 * k]
        for cp in _scatter_copies(s, got, send_sems, recv_sems):
            cp.wait_send()
            cp.wait_recv()

    hbm = [pltpu.HBM(a.shape, a.dtype) for a in list(sums) + list(placed)]
    out = pl.pallas_call(
        body, name="scatter_wait", out_shape=hbm,
        in_specs=[HBM] * (2 * n) + [SEM] * (2 * k) + [ANY], out_specs=[HBM] * (2 * n),
        input_output_aliases={i: i for i in range(2 * n)},
        compiler_params=pltpu.CompilerParams(has_side_effects=DATAFLOW),
    )(*sums, *placed, *sems, after)
    return out[n:]


def _join_halves(placed):
    n = len(placed)

    def body(*refs):
        full = refs[n:2 * n]
        send_sems, recv_sems = refs[2 * n:]
        x, y, c, sibling, _ = _place()
        for i in range(n):
            _start_streams(lambda r, i=i: _remote(full[i].at[c, r], full[i].at[c, r], send_sems.at[i],
                                                  recv_sems.at[i], sibling), full[i].shape[1])
        for i in range(n):
            cp = _remote(full[i].at[c], full[i].at[1 - c], send_sems.at[i], recv_sems.at[i], sibling)
            cp.wait_recv()
            cp.wait_send()

    return pl.pallas_call(
        body, name="join_halves", out_shape=[jax.ShapeDtypeStruct(s.shape, s.dtype) for s in placed],
        in_specs=[ANY] * n, out_specs=[ANY] * n, input_output_aliases={i: i for i in range(n)},
        scratch_shapes=[pltpu.SemaphoreType.DMA((n,)), pltpu.SemaphoreType.DMA((n,))],
    )(*placed)


def _gather_rows(row):
    def body(row_ref, out_ref, send_sems, recv_sems, local_sem):
        x, y, c = lax.axis_index("x"), lax.axis_index("y"), lax.axis_index("c")
        me = 4 * x + 2 * y + c
        local = pltpu.make_async_copy(row_ref, out_ref.at[me], local_sem)
        local.start()
        sent = []
        peers = []
        for k in range(1, N_DEV):
            px, py, pc = x ^ (k >> 2), y ^ ((k >> 1) & 1), c ^ (k & 1)
            peers.append((k, px, py, pc))
            cp = _remote(row_ref, out_ref.at[me], send_sems.at[k - 1], recv_sems.at[k - 1], (px, py, pc))
            cp.start()
            sent.append(cp)
        for k, px, py, pc in peers:
            slot = out_ref.at[4 * px + 2 * py + pc]
            _remote(slot, slot, send_sems.at[k - 1], recv_sems.at[k - 1], (px, py, pc)).wait_recv()
        for cp in sent:
            cp.wait_send()
        local.wait()

    return pl.pallas_call(
        body, name="gather_rows", out_shape=jax.ShapeDtypeStruct((N_DEV,) + row.shape, row.dtype),
        in_specs=[ANY], out_specs=ANY,
        scratch_shapes=[pltpu.SemaphoreType.DMA((N_DEV - 1,)), pltpu.SemaphoreType.DMA((N_DEV - 1,)),
                        pltpu.SemaphoreType.DMA],
    )(row)


def _scalar(i):
    return jnp.reshape(i, (1,)).astype(jnp.int32)


def _place_block(src, n_slots, slot, out_dtype, name, window=0, n_windows=1):
    rows, cols = src.shape[0], src.shape[1] // n_windows
    tr = _divisor_tile(rows, max(16, (2 * MIB) // (cols * 4)), 16)

    def body(slot_ref, s_ref, o_ref):
        o_ref[...] = s_ref[...].astype(o_ref.dtype)

    return pl.pallas_call(
        body, name=name, out_shape=jax.ShapeDtypeStruct((n_slots, rows, cols), out_dtype),
        grid_spec=pltpu.PrefetchScalarGridSpec(
            num_scalar_prefetch=1, grid=(rows // tr,), in_specs=[pl.BlockSpec((tr, cols), lambda r, sl: (r, window))],
            out_specs=pl.BlockSpec((None, tr, cols), lambda r, sl: (sl[0], r, 0))),
        compiler_params=_params(("parallel",)))(_scalar(slot), src)


def _add_halves(g, theirs, core, chip, name):
    n, half, cols = theirs.shape
    tr = _divisor_tile(half, max(16, (2 * MIB) // (cols * 4)), 16)
    per = half // tr

    def body(at_ref, a_ref, b_ref, o_ref, own_ref):
        total = (a_ref[...].astype(F32) + b_ref[...].astype(F32)).astype(o_ref.dtype)
        o_ref[...] = total

        @pl.when(pl.program_id(1) == at_ref[1])
        def _():
            own_ref[...] = total

    spec = pl.BlockSpec((None, tr, cols), lambda r, i, at: (i, r, 0))
    shape = jax.ShapeDtypeStruct(theirs.shape, BF16)
    return pl.pallas_call(
        body, name=name, out_shape=[shape, shape],
        grid_spec=pltpu.PrefetchScalarGridSpec(
            num_scalar_prefetch=1, grid=(per, n),
            in_specs=[pl.BlockSpec((None, tr, cols), lambda r, i, at: (i, at[0] * per + r, 0)), spec],
            out_specs=[spec, pl.BlockSpec((None, tr, cols), lambda r, i, at: (at[1], r, 0))]),
        compiler_params=_params(("parallel", "arbitrary")))(jnp.concatenate([_scalar(core), _scalar(chip)]), g, theirs)


def _sum_slots(a, core, name):
    n, rows, cols = a.shape
    tr = _divisor_tile(rows, max(16, (2 * MIB) // (cols * 4 * n)), 16)

    def body(c_ref, a_ref, o_ref):
        acc = a_ref[0].astype(F32)
        for i in range(1, n):
            acc = acc + a_ref[i].astype(F32)
        o_ref[...] = acc

    return pl.pallas_call(
        body, name=name, out_shape=jax.ShapeDtypeStruct((2, rows, cols), F32),
        grid_spec=pltpu.PrefetchScalarGridSpec(
            num_scalar_prefetch=1, grid=(rows // tr,),
            in_specs=[pl.BlockSpec((n, tr, cols), lambda r, c: (0, r, 0))],
            out_specs=pl.BlockSpec((None, tr, cols), lambda r, c: (c[0], r, 0))),
        compiler_params=_params(("parallel",)))(_scalar(core), a)


def _adamw_math(w, g, m, v):
    m = ADAM_B1 * m + (1.0 - ADAM_B1) * g
    v = ADAM_B2 * v + (1.0 - ADAM_B2) * (g * g)
    m_hat = m / (1.0 - ADAM_B1 ** ADAM_STEP)
    v_hat = v / (1.0 - ADAM_B2 ** ADAM_STEP)
    delta = -ADAM_LR * (m_hat / (jnp.sqrt(v_hat) + ADAM_EPS) + ADAM_WD * w)
    return delta, m, v


def _adamw(w, g, m, v, name):
    rows, cols = w.shape
    tr = _divisor_tile(rows, max(8, MIB // (cols * 4)), 8)

    def body(w_ref, g_ref, m_ref, v_ref, d_ref, nm_ref, nv_ref):
        d, nm, nv = _adamw_math(w_ref[...], g_ref[...], m_ref[...], v_ref[...])
        d_ref[...] = d
        nm_ref[...] = nm
        nv_ref[...] = nv

    spec = pl.BlockSpec((tr, cols), lambda r: (r, 0))
    shape = jax.ShapeDtypeStruct((rows, cols), F32)
    return pl.pallas_call(body, name=name, grid=(rows // tr,), in_specs=[spec] * 4, out_specs=[spec] * 3,
                          out_shape=[shape] * 3, compiler_params=_params(("parallel",)))(w, g, m, v)


def _sum_rows_adamw(parts, w, m, v):
    def body(p_ref, w_ref, m_ref, v_ref, g_ref, d_ref, nm_ref, nv_ref):
        g = p_ref[0]
        for i in range(1, N_DEV):
            g = g + p_ref[i]
        d, nm, nv = _adamw_math(w_ref[...], g, m_ref[...], v_ref[...])
        g_ref[...] = g
        d_ref[...] = d
        nm_ref[...] = nm
        nv_ref[...] = nv

    shape = jax.ShapeDtypeStruct(w.shape, F32)
    return pl.pallas_call(body, name="sum_rows_adamw", out_shape=[shape] * 4)(parts, w, m, v)


LANES = 128


def _permute_scratch(rows, width):
    return pltpu.VMEM((width // LANES, rows, LANES), F32)


def _split_rows(value, scratch, dil):
    if dil == 1:
        return [value]
    rows = value.shape[0] // dil
    slabs = value.shape[1] // LANES
    for c in range(slabs):
        scratch[c] = value[:, c * LANES:(c + 1) * LANES]
    return [jnp.concatenate([scratch[c, pl.ds(r, rows, stride=dil), :] for c in range(slabs)], axis=1)
            for r in range(dil)]


def _merge_rows(ref, scratch, dil):
    if dil == 1:
        return ref[0].astype(F32)
    rows = ref.shape[1]
    slabs = ref.shape[2] // LANES
    for r in range(dil):
        part = ref[r].astype(F32)
        for c in range(slabs):
            scratch[c, pl.ds(r, rows, stride=dil), :] = part[:, c * LANES:(c + 1) * LANES]
    return jnp.concatenate([scratch[c] for c in range(slabs)], axis=1)


def _grouped_view(t, dil):
    return t.reshape(dil, t.shape[0] // dil, t.shape[1])


def _grouped_spec(dil, rows, width, index):
    return pl.BlockSpec((dil, rows // dil, width), index)


W_CHUNKS = 2


def _pick(values, j):
    out = values[-1]
    for i in range(len(values) - 2, -1, -1):
        out = jnp.where(j == i, values[i], out)
    return out


def _chunk_of(col, per_chip):
    return (col % per_chip) // (per_chip // W_CHUNKS)


def _w_block(col, per_chip):
    return col // per_chip, 0, (col % per_chip) % (per_chip // W_CHUNKS)


def _in_proj(xb, wc, blocks, j0, ncols, tn, out_dtype, prev, after, name):
    s, d = xb.shape
    per_chip = wc.shape[2] * W_CHUNKS // tn
    tm = _divisor_tile(s, 1024, 16)
    extra = [t for t in (after,) if t is not None]

    def body(*refs):
        a_ref, b_ref = refs[len(extra):len(extra) + 2]
        o_ref = refs[-1]
        o_ref[...] = jnp.dot(a_ref[...], b_ref[...], preferred_element_type=F32).astype(o_ref.dtype)

    in_specs = [pl.BlockSpec(t.shape, lambda j, m: (0, 0)) for t in extra] + [
        pl.BlockSpec((tm, d), lambda j, m: (m, 0)),
        pl.BlockSpec((None, d, tn), lambda j, m: _w_block(_pick(blocks, j), per_chip))]
    args = extra + [xb, wc]
    aliases = {}
    if prev is not None:
        aliases = {len(args): 0}
        in_specs.append(ANY)
        args.append(prev)
    return pl.pallas_call(
        body, name=name, grid=(len(blocks), s // tm), in_specs=in_specs,
        out_specs=pl.BlockSpec((tm, tn), lambda j, m: (m, _pick(blocks, j) - j0)),
        out_shape=jax.ShapeDtypeStruct((s, ncols), out_dtype), input_output_aliases=aliases,
        compiler_params=_params(("parallel", "parallel")))(*args)


def _in_proj_qkv(xb, wc, g, blocks, aw, tn, prev, after, name):
    s, d = xb.shape
    dil = DILATIONS[g]
    per_chip = wc.shape[2] * W_CHUNKS // tn
    sub = aw // tn
    tm = _divisor_tile(s, 1024, 16 * dil)
    extra = [t for t in (after,) if t is not None]

    def body(*refs):
        a_ref, b_ref = refs[len(extra):len(extra) + 2]
        o_ref, scratch = refs[-2:]
        res = jnp.dot(a_ref[...], b_ref[...], preferred_element_type=F32)
        for r, part in enumerate(_split_rows(res, scratch, dil)):
            o_ref[r] = part.astype(BF16)

    def out_index(j, m):
        col = _pick(blocks, j)
        return (col // sub) // N_GROUPS, 0, m, col % sub

    in_specs = [pl.BlockSpec(t.shape, lambda j, m: (0, 0)) for t in extra] + [
        pl.BlockSpec((tm, d), lambda j, m: (m, 0)),
        pl.BlockSpec((None, d, tn), lambda j, m: _w_block(_pick(blocks, j), per_chip))]
    args = extra + [xb, wc]
    aliases = {}
    if prev is not None:
        aliases = {len(args): 0}
        in_specs.append(ANY)
        args.append(prev)
    return pl.pallas_call(
        body, name=name, grid=(len(blocks), s // tm), in_specs=in_specs,
        out_specs=pl.BlockSpec((None, dil, tm // dil, tn), out_index),
        out_shape=jax.ShapeDtypeStruct((3, dil, s // dil, aw), BF16), input_output_aliases=aliases,
        scratch_shapes=[_permute_scratch(tm, tn)],
        compiler_params=_params(("parallel", "parallel")))(*args)


def _window_mask(first):
    qi = lax.broadcasted_iota(jnp.int32, (STEPS, 2 * STEPS), 0)
    kj = lax.broadcasted_iota(jnp.int32, (STEPS, 2 * STEPS), 1)
    lowest = jnp.where(first, STEPS, 0)
    return (kj >= qi) & (kj <= qi + STEPS) & (kj >= lowest)


def _attn_fwd(qkv, g):
    _, s, aw = qkv.shape
    heads = aw // HEAD_DIM
    n_blocks = s // STEPS
    per_seq = n_blocks // DILATIONS[g]

    def body(q_ref, kc_ref, kp_ref, vc_ref, vp_ref, o_ref, l_ref):
        mask = _window_mask(lax.rem(pl.program_id(0), per_seq) == 0)
        for h in range(heads):
            hs = slice(h * HEAD_DIM, (h + 1) * HEAD_DIM)
            kk = jnp.concatenate([kp_ref[:, hs], kc_ref[:, hs]], axis=0)
            vv = jnp.concatenate([vp_ref[:, hs], vc_ref[:, hs]], axis=0)
            sc = lax.dot_general(q_ref[:, hs], kk, NT, preferred_element_type=F32) * SCORE_SCALE
            sc = jnp.where(mask, sc, NEG_INF)
            mx = jnp.max(sc, axis=1, keepdims=True)
            e = jnp.exp(sc - mx)
            den = jnp.sum(e, axis=1, keepdims=True)
            o_ref[:, hs] = (jnp.dot(e.astype(BF16), vv, preferred_element_type=F32) / den).astype(BF16)
            l_ref[:, hs] = jnp.broadcast_to(mx + jnp.log(den), (STEPS, HEAD_DIM))

    def cur(which):
        return pl.BlockSpec((None, STEPS, aw), lambda b: (which, b, 0))

    def prev(which):
        return pl.BlockSpec((None, STEPS, aw), lambda b: (which, jnp.maximum(b - 1, 0), 0))

    out = pl.BlockSpec((STEPS, aw), lambda b: (b, 0))
    return pl.pallas_call(
        body, name=f"attn_fwd{g}", grid=(n_blocks,),
        in_specs=[cur(0), cur(1), prev(1), cur(2), prev(2)], out_specs=[out, out],
        out_shape=[jax.ShapeDtypeStruct((s, aw), BF16), jax.ShapeDtypeStruct((s, aw), F32)],
        compiler_params=_params(("parallel",)))(qkv, qkv, qkv, qkv, qkv)


def _combine_groups(os, ls, zuz, aw):
    s = zuz.shape[0]
    tr = _divisor_tile(s, 256, 8 * DILATIONS[-1])

    def body(*refs):
        o_refs, l_refs, z_ref = refs[0:3], refs[3:6], refs[6]
        oo_ref, y_ref, yt_ref = refs[7:10]
        lq_refs, scratch = refs[10:13], refs[13]
        ls_ = [_merge_rows(l_refs[g], scratch, dil) for g, dil in enumerate(DILATIONS)]
        mx = jnp.maximum(jnp.maximum(ls_[0], ls_[1]), ls_[2])
        ws = [jnp.exp(l - mx) for l in ls_]
        den = ws[0] + ws[1] + ws[2]
        o = ws[0] * _merge_rows(o_refs[0], scratch, DILATIONS[0])
        for g in range(1, N_GROUPS):
            o = o + ws[g] * _merge_rows(o_refs[g], scratch, DILATIONS[g])
        o = o / den
        z = z_ref[...].astype(F32)
        y = o * (z * _sigmoid(z))
        oo_ref[...] = o.astype(BF16)
        y_ref[...] = y.astype(BF16)
        yt_ref[...] = y.T.astype(BF16)
        for g, dil in enumerate(DILATIONS):
            for r, part in enumerate(_split_rows(mx + jnp.log(den), scratch, dil)):
                lq_refs[g][r] = part

    grouped = [_grouped_spec(dil, tr, aw, lambda r: (0, r, 0)) for dil in DILATIONS]
    one = pl.BlockSpec((tr, aw), lambda r: (r, 0))
    b16 = jax.ShapeDtypeStruct((s, aw), BF16)
    out = pl.pallas_call(
        body, name="combine_groups", grid=(s // tr,),
        in_specs=grouped + grouped + [one],
        out_specs=[one, one, pl.BlockSpec((aw, tr), lambda r: (0, r))] + grouped,
        out_shape=[b16, b16, jax.ShapeDtypeStruct((aw, s), BF16)]
        + [jax.ShapeDtypeStruct((dil, s // dil, aw), F32) for dil in DILATIONS],
        scratch_shapes=[_permute_scratch(tr, aw)],
        compiler_params=_params(("parallel",)))(
            *[_grouped_view(t, dil) for t, dil in zip(os, DILATIONS)],
            *[_grouped_view(t, dil) for t, dil in zip(ls, DILATIONS)], zuz)
    return out[0], out[1], out[2], [t.reshape(s, aw) for t in out[3:]]


def _pool_counts(row0, rows, window):
    t = row0 + lax.broadcasted_iota(jnp.int32, (rows, 1), 0)
    return jnp.minimum(t + 1, window).astype(F32)


def _pool_fwd(zuz, w_pool, pool_scale, aw, pw):
    s = zuz.shape[0]
    pg = pw // len(POOL_WINDOWS)
    tr = _divisor_tile(s, 256, 128)
    u_col, z_col = aw // pw, aw // pw + 1
    assert aw % pw == 0

    def body(u_ref, up_ref, z_ref, w_ref, sc_ref, p_ref, l_ref, y_ref, yt_ref):
        r = pl.program_id(0)
        u = u_ref[...].astype(F32)
        halo = jnp.where(r > 0, up_ref[...].astype(F32), 0.0)
        ext = jnp.concatenate([halo, u], axis=0)
        pieces, lins = [], []
        for gi, window in enumerate(POOL_WINDOWS):
            cs = slice(gi * pg, (gi + 1) * pg)
            acc = ext[:, cs]
            shift = 1
            while shift < window:
                acc = acc + pltpu.roll(acc, shift, 0)
                shift *= 2
            p = acc[POOL_HALO:] / _pool_counts(r * tr, tr, window) - u[:, cs]
            pieces.append(p)
            lins.append(jnp.dot(p.astype(BF16), w_ref[gi], preferred_element_type=F32))
        p = jnp.concatenate(pieces, axis=1)
        lin = jnp.concatenate(lins, axis=1)
        z = z_ref[...].astype(F32)
        y = lin * sc_ref[...] * (z * _sigmoid(z))
        p_ref[...] = p.astype(BF16)
        l_ref[...] = lin
        y_ref[...] = y.astype(BF16)
        yt_ref[...] = y.T.astype(BF16)

    per = tr // POOL_HALO
    out = pl.BlockSpec((tr, pw), lambda r: (r, 0))
    return pl.pallas_call(
        body, name="pool_fwd", grid=(s // tr,),
        in_specs=[pl.BlockSpec((tr, pw), lambda r: (r, u_col)),
                  pl.BlockSpec((POOL_HALO, pw), lambda r: (jnp.maximum(r * per - 1, 0), u_col)),
                  pl.BlockSpec((tr, pw), lambda r: (r, z_col)),
                  pl.BlockSpec((len(POOL_WINDOWS), pg, pg), lambda r: (0, 0, 0)),
                  pl.BlockSpec((1, pw), lambda r: (0, 0))],
        out_specs=[out, out, out, pl.BlockSpec((pw, tr), lambda r: (0, r))],
        out_shape=[jax.ShapeDtypeStruct((s, pw), BF16), jax.ShapeDtypeStruct((s, pw), F32),
                   jax.ShapeDtypeStruct((s, pw), BF16), jax.ShapeDtypeStruct((pw, s), BF16)],
        compiler_params=_params(("parallel",)))(zuz, zuz, zuz, w_pool, pool_scale)


def _proj_merge(y_attn, y_pool, wpa4, wpp4, gpre, b_gate):
    s, aw = y_attn.shape
    pw = y_pool.shape[1]
    tn = wpa4.shape[2]
    d = N_CHIPS * tn
    tm = _divisor_tile(s, 512, 128)

    def body(ya_ref, yp_ref, wa_ref, wp_ref, ga_ref, gp_ref, ba_ref, bp_ref, a_ref, p_ref, sa_ref, sp_ref, m_ref,
             mt_ref):
        a = jnp.dot(ya_ref[...], wa_ref[...], preferred_element_type=F32)
        p = jnp.dot(yp_ref[...], wp_ref[...], preferred_element_type=F32)
        sa = _sigmoid(ga_ref[...].astype(F32) + ba_ref[...])
        sp = _sigmoid(gp_ref[...].astype(F32) + bp_ref[...])
        merged = sa * a + sp * p
        a_ref[...] = a.astype(BF16)
        p_ref[...] = p.astype(BF16)
        sa_ref[...] = sa.astype(BF16)
        sp_ref[...] = sp.astype(BF16)
        m_ref[...] = merged.astype(BF16)
        mt_ref[...] = merged.T.astype(BF16)

    out = pl.BlockSpec((tm, tn), lambda n, m: (m, n))
    f = jax.ShapeDtypeStruct((s, d), BF16)
    return pl.pallas_call(
        body, name="proj_merge", grid=(N_CHIPS, s // tm),
        in_specs=[pl.BlockSpec((tm, aw), lambda n, m: (m, 0)), pl.BlockSpec((tm, pw), lambda n, m: (m, 0)),
                  pl.BlockSpec((None, aw, tn), lambda n, m: (n, 0, 0)),
                  pl.BlockSpec((None, pw, tn), lambda n, m: (n, 0, 0)),
                  pl.BlockSpec((tm, tn), lambda n, m: (m, n)), pl.BlockSpec((tm, tn), lambda n, m: (m, N_CHIPS + n)),
                  pl.BlockSpec((1, tn), lambda n, m: (0, n)), pl.BlockSpec((1, tn), lambda n, m: (0, N_CHIPS + n))],
        out_specs=[out] * 5 + [pl.BlockSpec((tn, tm), lambda n, m: (n, m))],
        out_shape=[f] * 5 + [jax.ShapeDtypeStruct((d, s), BF16)],
        compiler_params=_params(("parallel", "parallel")))(y_attn, y_pool, wpa4, wpp4, gpre, gpre, b_gate, b_gate)


def _out_norm_loss(merged, w_out, x, target, gamma, beta):
    s, d = x.shape
    tm = _divisor_tile(s, 256, 16)

    def body(m_ref, w_ref, x_ref, t_ref, g_ref, b_ref, dr_ref, drb_ref, loss_ref, dg_ref, db_ref):
        @pl.when(pl.program_id(0) == 0)
        def _():
            loss_ref[...] = jnp.zeros_like(loss_ref)
            dg_ref[...] = jnp.zeros_like(dg_ref)
            db_ref[...] = jnp.zeros_like(db_ref)

        r = ALPHA * x_ref[...] + jnp.dot(m_ref[...], w_ref[...], preferred_element_type=F32)
        mu = jnp.mean(r, axis=1, keepdims=True)
        rc = r - mu
        rstd = lax.rsqrt(jnp.mean(rc * rc, axis=1, keepdims=True) + LN_EPS)
        xhat = rc * rstd
        diff = xhat * g_ref[...] + b_ref[...] - t_ref[...]
        dy = diff / d
        loss_ref[...] += jnp.sum(diff * diff, axis=0, keepdims=True)
        dg_ref[...] += jnp.sum(dy * xhat, axis=0, keepdims=True)
        db_ref[...] += jnp.sum(dy, axis=0, keepdims=True)
        dxhat = dy * g_ref[...]
        dr = rstd * (dxhat - jnp.mean(dxhat, axis=1, keepdims=True)
                     - xhat * jnp.mean(dxhat * xhat, axis=1, keepdims=True))
        dr_ref[...] = dr
        drb_ref[...] = dr.astype(BF16)

    row = pl.BlockSpec((tm, d), lambda m: (m, 0))
    vec = pl.BlockSpec((1, d), lambda m: (0, 0))
    v = jax.ShapeDtypeStruct((1, d), F32)
    return pl.pallas_call(
        body, name="out_norm_loss", grid=(s // tm,),
        in_specs=[row, pl.BlockSpec((d, d), lambda m: (0, 0)), row, row, vec, vec],
        out_specs=[row, row, vec, vec, vec],
        out_shape=[jax.ShapeDtypeStruct((s, d), F32), jax.ShapeDtypeStruct((s, d), BF16), v, v, v],
        compiler_params=_params(("arbitrary",), vmem_mib=56))(merged, w_out, x, target, gamma, beta)


def _merge_bwd(drb, w_out, a, p, sa, sp):
    s, d = drb.shape
    tm = _divisor_tile(s, 512, 16)
    tn = d // N_CHIPS

    def body(dr_ref, w_ref, a_ref, p_ref, sa_ref, sp_ref, da_ref, dp_ref, dga_ref, dgp_ref, dba_ref, dbp_ref):
        @pl.when(pl.program_id(1) == 0)
        def _():
            dba_ref[...] = jnp.zeros_like(dba_ref)
            dbp_ref[...] = jnp.zeros_like(dbp_ref)

        dm = lax.dot_general(dr_ref[...], w_ref[...], NT, preferred_element_type=F32)
        sa = sa_ref[...].astype(F32)
        sp = sp_ref[...].astype(F32)
        da_ref[...] = (dm * sa).astype(BF16)
        dp_ref[...] = (dm * sp).astype(BF16)
        dga = dm * a_ref[...].astype(F32) * sa * (1.0 - sa)
        dgp = dm * p_ref[...].astype(F32) * sp * (1.0 - sp)
        dga_ref[...] = dga.astype(BF16)
        dgp_ref[...] = dgp.astype(BF16)
        dba_ref[...] += jnp.sum(dga, axis=0, keepdims=True)
        dbp_ref[...] += jnp.sum(dgp, axis=0, keepdims=True)

    blk = pl.BlockSpec((tm, tn), lambda n, m: (m, n))
    vec = pl.BlockSpec((1, tn), lambda n, m: (0, n))
    b16 = jax.ShapeDtypeStruct((s, d), BF16)
    v = jax.ShapeDtypeStruct((1, d), F32)
    return pl.pallas_call(
        body, name="merge_bwd", grid=(N_CHIPS, s // tm),
        in_specs=[pl.BlockSpec((tm, d), lambda n, m: (m, 0)), pl.BlockSpec((tn, d), lambda n, m: (n, 0)),
                  blk, blk, blk, blk],
        out_specs=[blk, blk, blk, blk, vec, vec], out_shape=[b16, b16, b16, b16, v, v],
        compiler_params=_params(("parallel", "arbitrary")))(drb, w_out, a, p, sa, sp)


def _proj_t(dy_ref, w_ref, tn):
    acc = None
    for n in range(N_CHIPS):
        t = lax.dot_general(dy_ref[:, n * tn:(n + 1) * tn], w_ref[n], NT, preferred_element_type=F32)
        acc = t if acc is None else acc + t
    return acc


def _attn_gate_bwd(da, wpa4, zuz, o):
    s, d = da.shape
    aw, tn = wpa4.shape[1], wpa4.shape[2]
    heads = aw // HEAD_DIM
    tm = _divisor_tile(s, 256, 16 * DILATIONS[-1])

    def body(*refs):
        da_ref, w_ref, z_ref, o_ref, dz_ref = refs[:5]
        do_refs, dd_refs, scratch = refs[5:8], refs[8:11], refs[11]
        dy = _proj_t(da_ref, w_ref, tn)
        z, o = z_ref[...].astype(F32), o_ref[...].astype(F32)
        sg = _sigmoid(z)
        do = dy * (z * sg)
        dz_ref[...] = (dy * o * _dsilu(z, sg)).astype(BF16)
        prod = do * o
        dd = jnp.concatenate(
            [jnp.broadcast_to(jnp.sum(prod[:, h * HEAD_DIM:(h + 1) * HEAD_DIM], axis=1, keepdims=True),
                              (tm, HEAD_DIM)) for h in range(heads)], axis=1)
        for g, dil in enumerate(DILATIONS):
            for r, part in enumerate(_split_rows(do, scratch, dil)):
                do_refs[g][r] = part.astype(BF16)
            for r, part in enumerate(_split_rows(dd, scratch, dil)):
                dd_refs[g][r] = part

    row = pl.BlockSpec((tm, aw), lambda m: (m, 0))
    grouped = [_grouped_spec(dil, tm, aw, lambda m: (0, m, 0)) for dil in DILATIONS]
    out = pl.pallas_call(
        body, name="attn_gate_bwd", grid=(s // tm,),
        in_specs=[pl.BlockSpec((tm, d), lambda m: (m, 0)), pl.BlockSpec((N_CHIPS, aw, tn), lambda m: (0, 0, 0)),
                  row, row],
        out_specs=[row] + grouped + grouped,
        out_shape=[jax.ShapeDtypeStruct((s, aw), BF16)]
        + [jax.ShapeDtypeStruct((dil, s // dil, aw), BF16) for dil in DILATIONS]
        + [jax.ShapeDtypeStruct((dil, s // dil, aw), F32) for dil in DILATIONS],
        scratch_shapes=[_permute_scratch(tm, aw)],
        compiler_params=_params(("parallel",)))(da, wpa4, zuz, o)
    return out[0], [t.reshape(s, aw) for t in out[1:4]], [t.reshape(s, aw) for t in out[4:7]]


def _pool_gate_bwd(dp_in, wpp4, zuz, lin, pooled, w_pool, pool_scale, aw):
    s, d = dp_in.shape
    pw, tn = wpp4.shape[1], wpp4.shape[2]
    n_win = len(POOL_WINDOWS)
    pg = pw // n_win
    tm = _divisor_tile(s, 256, 16)
    z_col = aw // pw + 1

    def body(dp_ref, w_ref, z_ref, l_ref, p_ref, wp_ref, sc_ref, dz_ref, dpo_ref, dw_ref, ds_ref):
        @pl.when(pl.program_id(0) == 0)
        def _():
            dw_ref[...] = jnp.zeros_like(dw_ref)
            ds_ref[...] = jnp.zeros_like(ds_ref)

        dy = _proj_t(dp_ref, w_ref, tn)
        z, lin_ = z_ref[...].astype(F32), l_ref[...]
        sg = _sigmoid(z)
        dypp = dy * (z * sg)
        dz_ref[...] = (dy * (lin_ * sc_ref[...]) * _dsilu(z, sg)).astype(BF16)
        ds_ref[...] += jnp.sum(dypp * lin_, axis=0, keepdims=True)
        dlin = (dypp * sc_ref[...]).astype(BF16)
        for gi in range(n_win):
            cs = slice(gi * pg, (gi + 1) * pg)
            pt = p_ref[:, cs].astype(F32).T.astype(BF16)
            dw_ref[gi] += jnp.dot(pt, dlin[:, cs], preferred_element_type=F32)
            dpo_ref[:, cs] = lax.dot_general(dlin[:, cs], wp_ref[gi], NT, preferred_element_type=F32)

    row = pl.BlockSpec((tm, pw), lambda m: (m, 0))
    return pl.pallas_call(
        body, name="pool_gate_bwd", grid=(s // tm,),
        in_specs=[pl.BlockSpec((tm, d), lambda m: (m, 0)), pl.BlockSpec((N_CHIPS, pw, tn), lambda m: (0, 0, 0)),
                  pl.BlockSpec((tm, pw), lambda m: (m, z_col)), row, row,
                  pl.BlockSpec((n_win, pg, pg), lambda m: (0, 0, 0)), pl.BlockSpec((1, pw), lambda m: (0, 0))],
        out_specs=[row, row, pl.BlockSpec((n_win, pg, pg), lambda m: (0, 0, 0)),
                   pl.BlockSpec((1, pw), lambda m: (0, 0))],
        out_shape=[jax.ShapeDtypeStruct((s, pw), BF16), jax.ShapeDtypeStruct((s, pw), F32),
                   jax.ShapeDtypeStruct((n_win, pg, pg), F32), jax.ShapeDtypeStruct((1, pw), F32)],
        compiler_params=_params(("arbitrary",)))(dp_in, wpp4, zuz, lin, pooled, w_pool, pool_scale)


def _pool_bwd(dpooled):
    s, pw = dpooled.shape
    pg = pw // len(POOL_WINDOWS)
    tr = _divisor_tile(s, 256, POOL_HALO)
    per = tr // POOL_HALO
    n_tiles = s // tr

    def body(c_ref, n_ref, du_ref):
        r = pl.program_id(0)
        cur = c_ref[...]
        halo = jnp.where(r < n_tiles - 1, n_ref[...], 0.0)
        ext = jnp.concatenate([cur, halo], axis=0)
        rows = tr + POOL_HALO
        for gi, window in enumerate(POOL_WINDOWS):
            cs = slice(gi * pg, (gi + 1) * pg)
            acc = ext[:, cs] / _pool_counts(r * tr, rows, window)
            shift = 1
            while shift < window:
                acc = acc + pltpu.roll(acc, rows - shift, 0)
                shift *= 2
            du_ref[:, cs] = (acc[:tr] - cur[:, cs]).astype(BF16)

    return pl.pallas_call(
        body, name="pool_bwd", grid=(n_tiles,),
        in_specs=[pl.BlockSpec((tr, pw), lambda r: (r, 0)),
                  pl.BlockSpec((POOL_HALO, pw), lambda r: (jnp.minimum((r + 1) * per, s // POOL_HALO - 1), 0))],
        out_specs=pl.BlockSpec((tr, pw), lambda r: (r, 0)),
        out_shape=jax.ShapeDtypeStruct((s, pw), BF16), compiler_params=_params(("parallel",)))(dpooled, dpooled)


def _attn_bwd(qkv, do, lse, dd, g):
    _, s, aw = qkv.shape
    heads = aw // HEAD_DIM
    n_blocks = s // STEPS
    per_seq = n_blocks // DILATIONS[g]

    def body(q_ref, do_ref, l_ref, dd_ref, kc_ref, kp_ref, vc_ref, vp_ref, out_ref, cq_ref, ck_ref, cv_ref):
        b = pl.program_id(0)

        @pl.when(b == 0)
        def _():
            cq_ref[...] = jnp.zeros_like(cq_ref)
            ck_ref[...] = jnp.zeros_like(ck_ref)
            cv_ref[...] = jnp.zeros_like(cv_ref)

        out_ref[0] = cq_ref[...].astype(BF16)

        @pl.when(b < n_blocks)
        def _():
            mask = _window_mask(lax.rem(b, per_seq) == 0)
            for h in range(heads):
                hs = slice(h * HEAD_DIM, (h + 1) * HEAD_DIM)
                q, do_ = q_ref[:, hs], do_ref[:, hs]
                kk = jnp.concatenate([kp_ref[:, hs], kc_ref[:, hs]], axis=0)
                vv = jnp.concatenate([vp_ref[:, hs], vc_ref[:, hs]], axis=0)
                lse_ = jnp.concatenate([l_ref[:, hs], l_ref[:, hs]], axis=1)
                dd_ = jnp.concatenate([dd_ref[:, hs], dd_ref[:, hs]], axis=1)
                sc = lax.dot_general(q, kk, NT, preferred_element_type=F32) * SCORE_SCALE
                prob = jnp.where(mask, jnp.exp(sc - lse_), 0.0)
                dprob = lax.dot_general(do_, vv, NT, preferred_element_type=F32)
                dsc = prob * (dprob - dd_) * SCORE_SCALE
                cq_ref[:, hs] = jnp.dot(dsc.astype(BF16), kk, preferred_element_type=F32)
                dkk = jnp.dot(dsc.T.astype(BF16), q, preferred_element_type=F32)
                dvv = jnp.dot(prob.T.astype(BF16), do_, preferred_element_type=F32)
                out_ref[1, :, hs] = (ck_ref[:, hs] + dkk[:STEPS]).astype(BF16)
                out_ref[2, :, hs] = (cv_ref[:, hs] + dvv[:STEPS]).astype(BF16)
                ck_ref[:, hs] = dkk[STEPS:]
                cv_ref[:, hs] = dvv[STEPS:]

        @pl.when(b == n_blocks)
        def _():
            out_ref[1] = ck_ref[...].astype(BF16)
            out_ref[2] = cv_ref[...].astype(BF16)

    last = n_blocks - 1

    def cur(which):
        return pl.BlockSpec((None, STEPS, aw), lambda b: (which, jnp.minimum(b, last), 0))

    def prev(which):
        return pl.BlockSpec((None, STEPS, aw), lambda b: (which, jnp.clip(b - 1, 0, last), 0))

    row = pl.BlockSpec((STEPS, aw), lambda b: (jnp.minimum(b, last), 0))
    return pl.pallas_call(
        body, name=f"attn_bwd{g}", grid=(n_blocks + 1,),
        in_specs=[cur(0), row, row, row, cur(1), prev(1), cur(2), prev(2)],
        out_specs=pl.BlockSpec((3, STEPS, aw), lambda b: (0, jnp.clip(b - 1, 0, last), 0)),
        out_shape=jax.ShapeDtypeStruct((3, s, aw), BF16),
        scratch_shapes=[pltpu.VMEM((STEPS, aw), F32)] * 3,
        compiler_params=_params(("arbitrary",)))(qkv, do, lse, dd, qkv, qkv, qkv, qkv)


def _weight_grad(at, b, tn, col_blocks, name):
    m, k = at.shape
    n = b.shape[1]
    tm = _divisor_tile(m, 1024, 16)
    tk = _divisor_tile(k, 2048, 128)
    nk = k // tk

    def body(a_ref, b_ref, o_ref, acc_ref):
        kk = pl.program_id(2)

        @pl.when(kk == 0)
        def _():
            acc_ref[...] = jnp.zeros_like(acc_ref)

        acc_ref[...] += jnp.dot(a_ref[...], b_ref[...], preferred_element_type=F32)

        @pl.when(kk == nk - 1)
        def _():
            o_ref[...] = acc_ref[...].astype(BF16)

    if col_blocks:
        out_spec = pl.BlockSpec((None, tm, tn), lambda i, j, kk: (j, i, 0))
        out_shape = jax.ShapeDtypeStruct((n // tn, m, tn), BF16)
    else:
        out_spec = pl.BlockSpec((tm, tn), lambda i, j, kk: (i, j))
        out_shape = jax.ShapeDtypeStruct((m, n), BF16)
    return pl.pallas_call(
        body, name=name, grid=(m // tm, n // tn, nk),
        in_specs=[pl.BlockSpec((tm, tk), lambda i, j, kk: (i, kk)), pl.BlockSpec((tk, tn), lambda i, j, kk: (kk, j))],
        out_specs=out_spec, out_shape=out_shape, scratch_shapes=[pltpu.VMEM((tm, tn), F32)],
        compiler_params=_params(("parallel", "parallel", "arbitrary")))(at, b)


def _w_in_grad_part(xt, b, col_of, n_local, tn, w_shape, prev, name):
    d, s = xt.shape
    per_chip = w_shape[2] // tn
    tm = _divisor_tile(d, 1024, 16)
    tk = _divisor_tile(s, 2048, 128)
    nk = s // tk

    def body(*refs):
        a_ref, b_ref, o_ref, acc_ref = refs[0], refs[1], refs[-2], refs[-1]
        kk = pl.program_id(2)

        @pl.when(kk == 0)
        def _():
            acc_ref[...] = jnp.zeros_like(acc_ref)

        acc_ref[...] += jnp.dot(a_ref[...], b_ref[...], preferred_element_type=F32)

        @pl.when(kk == nk - 1)
        def _():
            o_ref[...] = acc_ref[...].astype(BF16)

    if b.ndim == 3:
        sub = b.shape[2] // tn
        b_spec = pl.BlockSpec((None, tk, tn), lambda j, i, kk: (j // sub, kk, j % sub))
    else:
        b_spec = pl.BlockSpec((tk, tn), lambda j, i, kk: (kk, j))
    in_specs = [pl.BlockSpec((tm, tk), lambda j, i, kk: (i, kk)), b_spec]
    args = [xt, b]
    aliases = {}
    if prev is not None:
        in_specs.append(ANY)
        args.append(prev)
        aliases = {2: 0}
    return pl.pallas_call(
        body, name=name, grid=(n_local, d // tm, nk), in_specs=in_specs,
        out_specs=pl.BlockSpec((None, tm, tn), lambda j, i, kk: (col_of(j) // per_chip, i, col_of(j) % per_chip)),
        out_shape=jax.ShapeDtypeStruct(w_shape, BF16), scratch_shapes=[pltpu.VMEM((tm, tn), F32)],
        input_output_aliases=aliases,
        compiler_params=_params(("parallel", "parallel", "arbitrary")))(*args)


def _x_grad(dqkv, rest, wc, chunk, init, init_scale, aw, tn, after=None):
    s, d = init.shape
    sub = aw // tn
    n_qkv = 3 * N_GROUPS * sub
    los, lo = [], n_qkv
    for p in rest:
        los.append(lo)
        lo += p.shape[1] // tn
    per_chip = lo // N_CHIPS
    per = per_chip // W_CHUNKS
    n_local = lo // W_CHUNKS
    tm = _divisor_tile(s, 512, 16 * DILATIONS[-1])

    def col(jl):
        return (jl // per) * per_chip + chunk * per + jl % per

    ordered = [] if after is None else [after]

    def body(*refs):
        refs = refs[len(ordered):]
        q_refs, r_refs = refs[:N_GROUPS], refs[N_GROUPS:N_GROUPS + len(rest)]
        w_ref, init_ref, o_ref, acc_ref, scratch = refs[-5:]
        jl = pl.program_id(1)
        j = col(jl)

        @pl.when(jl == 0)
        def _():
            acc_ref[...] = init_scale * init_ref[...]

        for g, dil in enumerate(DILATIONS):
            @pl.when((j < n_qkv) & (lax.rem(j // sub, N_GROUPS) == g))
            def _(g=g, dil=dil):
                rows = _merge_rows(q_refs[g], scratch, dil).astype(BF16)
                acc_ref[...] += lax.dot_general(rows, w_ref[...], NT, preferred_element_type=F32)

        for p_ref, lo_, piece in zip(r_refs, los, rest):
            @pl.when((j >= lo_) & (j < lo_ + piece.shape[1] // tn))
            def _(p_ref=p_ref):
                acc_ref[...] += lax.dot_general(p_ref[...], w_ref[...], NT, preferred_element_type=F32)

        @pl.when(jl == n_local - 1)
        def _():
            o_ref[...] = acc_ref[...]

    def qkv_spec(dil):
        def index(i, jl):
            j = col(jl)
            region = jnp.minimum(j // sub, 3 * N_GROUPS - 1)
            return region // N_GROUPS, 0, i, jnp.where(j < n_qkv, j % sub, 0)

        return pl.BlockSpec((None, dil, tm // dil, tn), index)

    def rest_spec(lo_, piece):
        n = piece.shape[1] // tn
        return pl.BlockSpec((tm, tn), lambda i, jl: (i, jnp.clip(col(jl) - lo_, 0, n - 1)))

    row = pl.BlockSpec((tm, d), lambda i, jl: (i, 0))
    return pl.pallas_call(
        body, name=f"x_grad{chunk}", grid=(s // tm, n_local),
        in_specs=[pl.BlockSpec(t.shape, lambda i, jl: (0, 0)) for t in ordered]
        + [qkv_spec(dil) for dil in DILATIONS] + [rest_spec(lo_, p) for lo_, p in zip(los, rest)]
        + [pl.BlockSpec((None, d, tn), lambda i, jl: (jl // per, 0, jl % per)), row],
        out_specs=row, out_shape=jax.ShapeDtypeStruct((s, d), F32),
        scratch_shapes=[pltpu.VMEM((tm, d), F32), _permute_scratch(tm, tn)],
        compiler_params=_params(("parallel", "arbitrary"), vmem_mib=56))(
            *ordered, *[t.reshape(3, dil, s // dil, aw) for t, dil in zip(dqkv, DILATIONS)], *rest, wc, init)


def _to_subsequences(t, dilation):
    s, w = t.shape
    return t.reshape(s // dilation, dilation, w).transpose(1, 0, 2).reshape(s, w)


def _prepare_x(x, after=None):
    s, d = x.shape
    tc = 2 * LANES
    slabs = tc // LANES
    ordered = [] if after is None else [after]

    def body(*refs):
        x_ref, xb_ref = refs[len(ordered):len(ordered) + 2]
        xt_refs, scratch = refs[len(ordered) + 2:len(ordered) + 2 + N_GROUPS], refs[-1]
        t = x_ref[...]
        xb_ref[...] = t.astype(BF16)
        for c in range(slabs):
            scratch[c] = t[:, c * LANES:(c + 1) * LANES]
        for g, dil in enumerate(DILATIONS):
            length = s // dil
            for r in range(dil):
                part = t if dil == 1 else jnp.concatenate(
                    [scratch[c, pl.ds(r, length, stride=dil), :] for c in range(slabs)], axis=1)
                xt_refs[g][:, r * length:(r + 1) * length] = part.T.astype(BF16)

    col = pl.BlockSpec((s, tc), lambda j: (0, j))
    row = pl.BlockSpec((tc, s), lambda j: (j, 0))
    t_shape = jax.ShapeDtypeStruct((d, s), BF16)
    out = pl.pallas_call(
        body, name="prepare_x", grid=(d // tc,),
        in_specs=[pl.BlockSpec(t.shape, lambda j: (0, 0)) for t in ordered] + [col],
        out_specs=[col] + [row] * N_GROUPS,
        out_shape=[jax.ShapeDtypeStruct((s, d), BF16)] + [t_shape] * N_GROUPS,
        scratch_shapes=[_permute_scratch(s, tc)], compiler_params=_params(("parallel",)))(*ordered, x)
    return out[0], out[1:]


def _local_step(x, target, w_chunk, w_width, b_gate, pool_scale, gamma, beta, aw, pw, small_weights,
                start_exchange=None):
    s, d = x.shape
    tn = _col_tile(aw, pw, w_width)
    sub = aw // tn
    per_chip = w_width // tn
    qkv_w = 3 * N_GROUPS * aw
    w_shape = (N_CHIPS, d, w_width)

    regions = [dict(kind=g, blocks=[(which * N_GROUPS + g) * sub + i for which in range(3) for i in range(sub)])
               for g in range(N_GROUPS)]
    lo = qkv_w // tn
    for name, width in (("zuz", aw + 2 * pw), ("gates", 2 * d)):
        regions.append(dict(kind=name, blocks=list(range(lo, lo + width // tn)), j0=lo, width=width))
        lo += width // tn
    results = [None] * len(regions)
    wcs, after = [], None
    for ch in range(W_CHUNKS):
        wc, token = w_chunk(ch, after)
        wcs.append(wc)
        if ch == 0:
            xb, xts = _prepare_x(x, token)
            token = None
        for i, region in enumerate(regions):
            blocks = [b for b in region["blocks"] if _chunk_of(b, per_chip) == ch]
            if not blocks:
                continue
            if region["kind"] in range(N_GROUPS):
                results[i] = _in_proj_qkv(xb, wc, region["kind"], blocks, aw, tn, results[i], token,
                                          f"in_proj_qkv{region['kind']}_{ch}")
            else:
                results[i] = _in_proj(xb, wc, blocks, region["j0"], region["width"], tn, BF16, results[i], token,
                                      f"in_proj_{region['kind']}_{ch}")
            token = None
            after = results[i]
    qkv = [results[g].reshape(3, s, aw) for g in range(N_GROUPS)]
    zuz, gpre = results[N_GROUPS], results[N_GROUPS + 1]

    attn = [_attn_fwd(qkv[g], g) for g in range(N_GROUPS)]
    o, y_attn, y_attn_t, lse = _combine_groups([a[0] for a in attn], [a[1] for a in attn], zuz, aw)
    w_pool, wpa4, wpp4, w_out = small_weights(o)
    pooled, lin, y_pool, y_pool_t = _pool_fwd(zuz, w_pool, pool_scale, aw, pw)
    a, p, sa, sp, merged, merged_t = _proj_merge(y_attn, y_pool, wpa4, wpp4, gpre, b_gate)
    dr, drb, loss_lanes, d_gamma, d_beta = _out_norm_loss(merged, w_out, x, target, gamma, beta)

    da, dp, d_gpre_a, d_gpre_p, d_b_a, d_b_p = _merge_bwd(drb, w_out, a, p, sa, sp)
    d_b_gate = jnp.concatenate([d_b_a, d_b_p], axis=1)
    d_w_out = _weight_grad(merged_t, drb, d // N_CHIPS, False, "w_out_grad")
    d_wpa4 = _weight_grad(y_attn_t, da, d // N_CHIPS, True, "w_proj_attn_grad")
    d_wpp4 = _weight_grad(y_pool_t, dp, d // N_CHIPS, True, "w_proj_pool_grad")
    d_z_attn, d_o, dd = _attn_gate_bwd(da, wpa4, zuz, o)
    d_z_pool, d_pooled, d_w_pool, d_pool_scale = _pool_gate_bwd(dp, wpp4, zuz, lin, pooled, w_pool, pool_scale, aw)
    d_u = _pool_bwd(d_pooled)
    dqkv = [_attn_bwd(qkv[g], d_o[g], lse[g], dd[g], g) for g in range(N_GROUPS)]

    rest = [d_z_attn, d_u, d_z_pool, d_gpre_a, d_gpre_p]
    d_w_in4 = None
    for g in range(N_GROUPS):
        d_w_in4 = _w_in_grad_part(xts[g], dqkv[g], lambda j, g=g: ((j // sub) * N_GROUPS + g) * sub + j % sub,
                                  3 * sub, tn, w_shape, d_w_in4, f"w_in_grad_qkv{g}")
    lo = qkv_w // tn
    for i, piece in enumerate(rest):
        n_local = piece.shape[1] // tn
        d_w_in4 = _w_in_grad_part(xts[0], piece, lambda j, lo=lo: lo + j, n_local, tn, w_shape, d_w_in4,
                                  f"w_in_grad_rest{i}")
        lo += n_local
    grads = dict(loss_lanes=loss_lanes, w_in=d_w_in4, b_gate=d_b_gate, w_pool=d_w_pool,
                 pool_scale=d_pool_scale, w_proj_attn=d_wpa4, w_proj_pool=d_wpp4, w_out=d_w_out,
                 ln_gamma=d_gamma, ln_beta=d_beta)
    token = None if start_exchange is None else start_exchange(grads)
    d_x, scale = dr, ALPHA
    for ch in range(W_CHUNKS):
        d_x = _x_grad(dqkv, rest, wcs[ch], ch, d_x, scale, aw, tn, token)
        token, scale = None, 1.0
    grads["d_x"] = d_x
    return grads


def _pack_small(wpa, wpp, w_out, w_pool):
    width = wpa.shape[1]
    return jnp.concatenate([wpa, wpp, w_out.reshape(-1, width), w_pool.reshape(-1, width)], axis=0)


def _unpack_small(packed, aw, pw, d, pg):
    lead = packed.shape[:-2]
    width = d // N_CHIPS
    r0, r1, r2 = aw, aw + pw, aw + pw + d
    return (packed[..., :r0, :], packed[..., r0:r1, :], packed[..., r1:r2, :].reshape(lead + (width, d)),
            packed[..., r2:, :].reshape(lead + (len(POOL_WINDOWS), pg // N_CHIPS, pg)))


def _pack_rows(vectors, rows):
    flat = jnp.concatenate([v.reshape(-1) for v in vectors])
    return jnp.pad(flat, (0, rows * 128 - flat.shape[0])).reshape(rows, 128)


def _unpack_rows(packed, sizes):
    flat, out, lo = packed.reshape(-1), [], 0
    for n in sizes:
        out.append(flat[lo:lo + n].reshape(1, n))
        lo += n
    return out


def kernel(x, w_in, b_gate, w_pool, pool_scale, w_proj_attn, w_proj_pool, w_out, ln_gamma, ln_beta, loss_target, m_w_in, m_b_gate, m_w_pool, m_pool_scale, m_w_proj_attn, m_w_proj_pool, m_w_out, m_ln_gamma, m_ln_beta, v_w_in, v_b_gate, v_w_pool, v_pool_scale, v_w_proj_attn, v_w_proj_pool, v_w_out, v_ln_gamma, v_ln_beta):
    s, d = x.shape[1], x.shape[2]
    aw, pw = w_proj_attn.shape[1], w_proj_pool.shape[1]
    pg = w_pool.shape[3]
    n_win = len(POOL_WINDOWS)

    def small(wpa, wpp, wo, wpl):
        return _pack_small(wpa[0], wpp[0], wo[0], wpl[0])

    chip = 2 * lax.axis_index("x") + lax.axis_index("y")
    core = lax.axis_index("c")

    w_small = small(w_proj_attn, w_proj_pool, w_out, w_pool)
    placed = [_place_block(w_in[0], N_CHIPS, chip, BF16, f"place_w_in{ch}", ch, W_CHUNKS) for ch in range(W_CHUNKS)]
    placed_small = _place_block(w_small, N_CHIPS, chip, BF16, "place_w_small")
    gathered, = _gather_weights([placed[0]])
    flight = {"chunk": _halves_start(placed[1], gathered, "gather_w_in1_start")}

    def w_chunk(ch, after):
        if ch == 0:
            return gathered, flight["chunk"][2]
        sems, thru, _ = flight["chunk"]
        landed = _halves_wait(sems, thru, after, f"gather_w_in{ch}_wait")
        if ch + 1 < W_CHUNKS:
            flight["chunk"] = _halves_start(placed[ch + 1], landed, f"gather_w_in{ch + 1}_start")
            token = flight["chunk"][2]
        else:
            flight["small"] = _broadcast_start(placed_small, landed, "gather_small_start")
            token = flight["small"][2]
        return _forward_halves(landed, f"forward_w_in{ch}"), token

    def small_weights(after):
        sems, thru, _ = flight["small"]
        small4 = _broadcast_wait(sems, thru, after, "gather_small_wait")
        wpa4, wpp4, w_out4, w_pool4 = _unpack_small(small4, aw, pw, d, pg)
        return w_pool4.transpose(1, 0, 2, 3).reshape(n_win, pg, pg), wpa4, wpp4, w_out4.reshape(d, d)

    exchange = {}

    def start_exchange(g):
        g_pool4 = g["w_pool"].reshape(n_win, N_CHIPS, pg // N_CHIPS, pg).transpose(1, 0, 2, 3).astype(BF16)
        g_out4 = g["w_out"].reshape(N_CHIPS, d // N_CHIPS, d)
        g_small4 = jnp.concatenate([g["w_proj_attn"], g["w_proj_pool"], g_out4.reshape(N_CHIPS, -1, d // N_CHIPS),
                                    g_pool4.reshape(N_CHIPS, -1, d // N_CHIPS)], axis=1)
        theirs_big, theirs_small = _swap_halves([g["w_in"], g_small4])
        chip_big, placed_big = _add_halves(g["w_in"], theirs_big, core, chip, "add_cores_big")
        chip_small, placed_small = _add_halves(g_small4, theirs_small, core, chip, "add_cores_small")
        sems, sums, placed, token = _scatter_start([chip_big, chip_small], [placed_big, placed_small])
        exchange.update(sems=sems, sums=sums, placed=placed)
        return token

    g = _local_step(x[0], loss_target[0], w_chunk, w_in.shape[2], b_gate, pool_scale, ln_gamma, ln_beta, aw, pw,
                    small_weights, start_exchange)
    got_big, got_small = _scatter_wait(exchange["sems"], exchange["sums"], exchange["placed"], g["d_x"])
    grad_w_in, grad_small = _join_halves([_sum_slots(got_big, core, "sum_chips_big"),
                                          _sum_slots(got_small, core, "sum_chips_small")])
    grad_w_in = grad_w_in.reshape(-1, grad_w_in.shape[2])
    grad_small = grad_small.reshape(-1, grad_small.shape[2])

    sizes = [b_gate.shape[1], pool_scale.shape[1], d, d, 1]
    rows = -(-sum(sizes) // (8 * 128)) * 8
    loss_part = (0.5 / d) * jnp.sum(g["loss_lanes"]).reshape(1, 1)
    parts = _gather_rows(_pack_rows([g["b_gate"], g["pool_scale"], g["ln_gamma"], g["ln_beta"], loss_part], rows))
    zero = jnp.zeros((1, 1), F32)
    packed = [_pack_rows(vs, rows) for vs in ([b_gate, pool_scale, ln_gamma, ln_beta, zero],
                                              [m_b_gate, m_pool_scale, m_ln_gamma, m_ln_beta, zero],
                                              [v_b_gate, v_pool_scale, v_ln_gamma, v_ln_beta, zero])]
    rep = [_unpack_rows(t, sizes) for t in _sum_rows_adamw(parts, *packed)]
    loss = rep[0][4].reshape(())

    upd_in = _adamw(w_in[0], grad_w_in, m_w_in[0], v_w_in[0], "adamw_w_in")
    upd_small = _adamw(w_small, grad_small, small(m_w_proj_attn, m_w_proj_pool, m_w_out, m_w_pool),
                       small(v_w_proj_attn, v_w_proj_pool, v_w_out, v_w_pool), "adamw_small")

    def leaves(big, packed_small, replicated):
        wpa_, wpp_, wo_, wpl_ = _unpack_small(packed_small, aw, pw, d, pg)
        return [big[None], replicated[0], wpl_[None], replicated[1], wpa_[None], wpp_[None], wo_[None],
                replicated[2], replicated[3]]

    out = [loss, g["d_x"][None]]
    out += leaves(grad_w_in, grad_small, rep[0])
    for i in range(3):
        out += leaves(upd_in[i], upd_small[i], rep[1 + i])
    return tuple(out)
```

```python
import math

import jax
import jax.numpy as jnp
from jax import lax
from jax.experimental import pallas as pl
from jax.experimental.pallas import tpu as pltpu

F32 = jnp.float32
BF16 = jnp.bfloat16
MESH = pl.DeviceIdType.MESH
ANY = pl.BlockSpec(memory_space=pl.ANY)

HEAD_DIM = 128
STEPS = 128
DILATIONS = (1, 4, 16)
N_GROUPS = len(DILATIONS)
POOL_WINDOWS = (2, 4, 8, 16)
POOL_HALO = 16
N_CHIPS = 4
N_DEV = 8
ALPHA = 2.0 ** 0.25
LN_EPS = 1e-5
NEG_INF = -1e30
SCORE_SCALE = HEAD_DIM ** -0.5
ADAM_LR = 0.001
ADAM_B1 = 0.9
ADAM_B2 = 0.999
ADAM_EPS = 1e-08
ADAM_WD = 0.01
ADAM_STEP = 10
MIB = 2 ** 20
NT = (((1,), (1,)), ((), ()))
DMA_STREAMS = 8


def _params(semantics=None, vmem_mib=48):
    return pltpu.CompilerParams(dimension_semantics=semantics, vmem_limit_bytes=vmem_mib * MIB)


def _divisor_tile(n, target, multiple):
    best = None
    for t in range(multiple, min(n, target) + 1, multiple):
        if n % t == 0:
            best = t
    assert best is not None, (n, target, multiple)
    return best


def _col_tile(*widths):
    g = 0
    for w in widths:
        g = math.gcd(g, w)
    return _divisor_tile(g, 1024, 128)


def _sigmoid(z):
    return jax.nn.sigmoid(z)


def _dsilu(z, sg):
    return sg * (1.0 + z * (1.0 - sg))


def _place():
    x, y, c = lax.axis_index("x"), lax.axis_index("y"), lax.axis_index("c")
    others = [(1 - x, y), (x, 1 - y), (1 - x, 1 - y)]
    return x, y, c, (x, y, 1 - c), others


def _remote(src, dst, send_sem, recv_sem, dev):
    return pltpu.make_async_remote_copy(src_ref=src, dst_ref=dst, send_sem=send_sem, recv_sem=recv_sem,
                                        device_id=dev, device_id_type=MESH)


def _row_pieces(n_rows, streams=DMA_STREAMS, multiple=16):
    size = -(-n_rows // (streams * multiple)) * multiple
    return [(lo, min(size, n_rows - lo)) for lo in range(0, n_rows, size)]


def _start_streams(make, n_rows):
    for lo, size in _row_pieces(n_rows):
        make(pl.ds(lo, size)).start()


def _half_copies(buf, send_sems, recv_sems):
    x, y, c, _, others = _place()
    half = buf.shape[1] // 2
    slab = buf.at[2 * x + y, pl.ds(c * half, half)]
    return [_remote(slab, slab, send_sems[j], recv_sems[j], (ox, oy, c)) for j, (ox, oy) in enumerate(others)]


def _halves_start(placed, after, name):
    k = N_CHIPS - 1

    def body(buf, after_ref, *refs):
        send_sems, recv_sems, token = refs[:k], refs[k:2 * k], refs[-1]
        for cp in _half_copies(buf, send_sems, recv_sems):
            cp.start()
        token[...] = jnp.zeros_like(token)

    out = pl.pallas_call(
        body, name=name,
        out_shape=[pltpu.SemaphoreType.DMA(())] * (2 * k) + [pltpu.HBM(placed.shape, placed.dtype),
                                                             jax.ShapeDtypeStruct((8, 128), F32)],
        in_specs=[HBM, ANY], out_specs=[SEM] * (2 * k) + [HBM, pl.BlockSpec(memory_space=pltpu.VMEM)],
        input_output_aliases={0: 2 * k},
        compiler_params=pltpu.CompilerParams(has_side_effects=DATAFLOW),
    )(pltpu.with_memory_space_constraint(placed, pltpu.HBM), after)
    return out[:2 * k], out[2 * k], out[-1]


def _halves_wait(sems, placed, after, name):
    k = N_CHIPS - 1

    def body(buf, *refs):
        send_sems, recv_sems = refs[:k], refs[k:2 * k]
        for cp in _half_copies(buf, send_sems, recv_sems):
            cp.wait_send()
            cp.wait_recv()

    return pl.pallas_call(
        body, name=name, out_shape=pltpu.HBM(placed.shape, placed.dtype),
        in_specs=[HBM] + [SEM] * (2 * k) + [ANY], out_specs=HBM, input_output_aliases={0: 0},
        compiler_params=pltpu.CompilerParams(has_side_effects=DATAFLOW),
    )(placed, *sems, after)


def _forward_halves(buf, name):
    def body(_, dst, send_sems, recv_sems):
        x, y, c, sibling, others = _place()
        half = dst.shape[1] // 2
        for j, (ox, oy) in enumerate(others):
            slab = dst.at[2 * ox + oy, pl.ds(c * half, half)]
            _remote(slab, slab, send_sems.at[j], recv_sems.at[j], sibling).start()
        for j, (ox, oy) in enumerate(others):
            mine = dst.at[2 * ox + oy, pl.ds(c * half, half)]
            theirs = dst.at[2 * ox + oy, pl.ds((1 - c) * half, half)]
            cp = _remote(mine, theirs, send_sems.at[j], recv_sems.at[j], sibling)
            cp.wait_recv()
            cp.wait_send()

    return pl.pallas_call(
        body, name=name, out_shape=jax.ShapeDtypeStruct(buf.shape, buf.dtype),
        in_specs=[ANY], out_specs=ANY, input_output_aliases={0: 0},
        scratch_shapes=[pltpu.SemaphoreType.DMA((N_CHIPS - 1,)), pltpu.SemaphoreType.DMA((N_CHIPS - 1,))],
    )(buf)


def _swap_halves(grads):
    n = len(grads)

    def body(*refs):
        g, theirs = refs[:n], refs[n:2 * n]
        send_sems, recv_sems = refs[2 * n:]
        x, y, c, sibling, _ = _place()
        for i in range(n):
            half = g[i].shape[1] // 2
            give = (1 - c) * half
            for b in range(N_CHIPS):
                _start_streams(lambda r, i=i, b=b: _remote(
                    g[i].at[b, pl.ds(give + r.start, r.size)], theirs[i].at[b, r], send_sems.at[i], recv_sems.at[i],
                    sibling), half)
        for i in range(n):
            _remote(theirs[i], theirs[i], send_sems.at[i], recv_sems.at[i], sibling).wait()

    return pl.pallas_call(
        body, name="swap_halves",
        out_shape=[jax.ShapeDtypeStruct((s.shape[0], s.shape[1] // 2) + s.shape[2:], s.dtype) for s in grads],
        in_specs=[ANY] * n, out_specs=[ANY] * n,
        scratch_shapes=[pltpu.SemaphoreType.DMA((n,)), pltpu.SemaphoreType.DMA((n,))],
    )(*grads)


HBM = pl.BlockSpec(memory_space=pltpu.HBM)
SEM = pl.BlockSpec(memory_space=pltpu.SEMAPHORE)
DATAFLOW = pltpu.SideEffectType.DATAFLOW_SIDE_EFFECTING


def _broadcast_copies(buf, send_sems, recv_sems):
    x, y, c, _, others = _place()
    mine = buf.at[2 * x + y]
    return [_remote(mine, mine, send_sems[j], recv_sems[j], (ox, oy, c)) for j, (ox, oy) in enumerate(others)]


def _broadcast_start(placed, after, name):
    k = N_CHIPS - 1

    def body(buf, after_ref, *refs):
        send_sems, recv_sems, token = refs[:k], refs[k:2 * k], refs[-1]
        for cp in _broadcast_copies(buf, send_sems, recv_sems):
            cp.start()
        token[...] = jnp.zeros_like(token)

    out = pl.pallas_call(
        body, name=name,
        out_shape=[pltpu.SemaphoreType.DMA(())] * (2 * k) + [pltpu.HBM(placed.shape, placed.dtype),
                                                             jax.ShapeDtypeStruct((8, 128), F32)],
        in_specs=[HBM, ANY], out_specs=[SEM] * (2 * k) + [HBM, pl.BlockSpec(memory_space=pltpu.VMEM)],
        input_output_aliases={0: 2 * k},
        compiler_params=pltpu.CompilerParams(has_side_effects=DATAFLOW),
    )(pltpu.with_memory_space_constraint(placed, pltpu.HBM), after)
    return out[:2 * k], out[2 * k], out[-1]


def _broadcast_wait(sems, placed, after, name):
    k = N_CHIPS - 1

    def body(buf, *refs):
        send_sems, recv_sems = refs[:k], refs[k:2 * k]
        for cp in _broadcast_copies(buf, send_sems, recv_sems):
            cp.wait_send()
            cp.wait_recv()

    return pl.pallas_call(
        body, name=name, out_shape=pltpu.HBM(placed.shape, placed.dtype),
        in_specs=[HBM] + [SEM] * (2 * k) + [ANY], out_specs=HBM, input_output_aliases={0: 0},
        compiler_params=pltpu.CompilerParams(has_side_effects=DATAFLOW),
    )(placed, *sems, after)


def _scatter_copies(s, got, send_sems, recv_sems):
    x, y, c, _, others = _place()
    me = 2 * x + y
    n = len(s)
    return [_remote(s[i].at[2 * ox + oy], got[i].at[me], send_sems[3 * i + j], recv_sems[3 * i + j], (ox, oy, c))
            for i in range(n) for j, (ox, oy) in enumerate(others)]


def _scatter_start(sums, placed):
    n = len(sums)
    k = 3 * n

    def body(*refs):
        s, got, token = refs[:n], refs[n:2 * n], refs[-1]
        send_sems, recv_sems = refs[2 * n:2 * n + k], refs[2 * n + k:2 * n + 2 * k]
        for cp in _scatter_copies(s, got, send_sems, recv_sems):
            cp.start()
        token[...] = jnp.zeros_like(token)

    hbm = [pltpu.HBM(a.shape, a.dtype) for a in list(sums) + list(placed)]
    out = pl.pallas_call(
        body, name="scatter_start",
        out_shape=[pltpu.SemaphoreType.DMA(())] * (2 * k) + hbm + [jax.ShapeDtypeStruct((8, 128), F32)],
        in_specs=[HBM] * (2 * n), out_specs=[SEM] * (2 * k) + [HBM] * (2 * n) + [pl.BlockSpec(memory_space=pltpu.VMEM)],
        input_output_aliases={i: 2 * k + i for i in range(2 * n)},
        compiler_params=pltpu.CompilerParams(has_side_effects=DATAFLOW),
    )(*[pltpu.with_memory_space_constraint(a, pltpu.HBM) for a in list(sums) + list(placed)])
    return out[:2 * k], out[2 * k:2 * k + n], out[2 * k + n:2 * k + 2 * n], out[-1]


def _scatter_wait(sems, sums, placed, after):
    n = len(sums)
    k = 3 * n

    def body(*refs):
        s, got = refs[:n], refs[n:2 * n]
        send_sems, recv_sems = refs[2 * n:2 * n + k], refs[2 * n + k:2 * n + 2 * k]
        for cp in _scatter_copies(s, got, send_sems, recv_sems):
            cp.wait_send()
            cp.wait_recv()

    hbm = [pltpu.HBM(a.shape, a.dtype) for a in list(sums) + list(placed)]
    out = pl.pallas_call(
        body, name="scatter_wait", out_shape=hbm,
        in_specs=[HBM] * (2 * n) + [SEM] * (2 * k) + [ANY], out_specs=[HBM] * (2 * n),
        input_output_aliases={i: i for i in range(2 * n)},
        compiler_params=pltpu.CompilerParams(has_side_effects=DATAFLOW),
    )(*sums, *placed, *sems, after)
    return out[n:]


def _join_copies(bufs, send_sems, recv_sems):
    x, y, c, sibling, _ = _place()
    return [_remote(b.at[c], b.at[c], send_sems[i], recv_sems[i], sibling) for i, b in enumerate(bufs)]


def _join_start(placed):
    n = len(placed)

    def body(*refs):
        bufs, send_sems, recv_sems = refs[:n], refs[n:2 * n], refs[2 * n:3 * n]
        for cp in _join_copies(bufs, send_sems, recv_sems):
            cp.start()

    hbm = [pltpu.HBM(a.shape, a.dtype) for a in placed]
    out = pl.pallas_call(
        body, name="join_start", out_shape=[pltpu.SemaphoreType.DMA(())] * (2 * n) + hbm,
        in_specs=[HBM] * n, out_specs=[SEM] * (2 * n) + [HBM] * n,
        input_output_aliases={i: 2 * n + i for i in range(n)},
        compiler_params=pltpu.CompilerParams(has_side_effects=DATAFLOW),
    )(*[pltpu.with_memory_space_constraint(a, pltpu.HBM) for a in placed])
    return out[:2 * n], out[2 * n:]


def _join_wait(sems, placed, after):
    n = len(placed)

    def body(*refs):
        bufs, send_sems, recv_sems = refs[:n], refs[n:2 * n], refs[2 * n:3 * n]
        for cp in _join_copies(bufs, send_sems, recv_sems):
            cp.wait_send()
            cp.wait_recv()

    return pl.pallas_call(
        body, name="join_wait", out_shape=[pltpu.HBM(a.shape, a.dtype) for a in placed],
        in_specs=[HBM] * n + [SEM] * (2 * n) + [ANY], out_specs=[HBM] * n,
        input_output_aliases={i: i for i in range(n)},
        compiler_params=pltpu.CompilerParams(has_side_effects=DATAFLOW),
    )(*placed, *sems, after)


def _gather_rows(row):
    def body(row_ref, out_ref, send_sems, recv_sems, local_sem):
        x, y, c = lax.axis_index("x"), lax.axis_index("y"), lax.axis_index("c")
        me = 4 * x + 2 * y + c
        local = pltpu.make_async_copy(row_ref, out_ref.at[me], local_sem)
        local.start()
        sent = []
        peers = []
        for k in range(1, N_DEV):
            px, py, pc = x ^ (k >> 2), y ^ ((k >> 1) & 1), c ^ (k & 1)
            peers.append((k, px, py, pc))
            cp = _remote(row_ref, out_ref.at[me], send_sems.at[k - 1], recv_sems.at[k - 1], (px, py, pc))
            cp.start()
            sent.append(cp)
        for k, px, py, pc in peers:
            slot = out_ref.at[4 * px + 2 * py + pc]
            _remote(slot, slot, send_sems.at[k - 1], recv_sems.at[k - 1], (px, py, pc)).wait_recv()
        for cp in sent:
            cp.wait_send()
        local.wait()

    return pl.pallas_call(
        body, name="gather_rows", out_shape=jax.ShapeDtypeStruct((N_DEV,) + row.shape, row.dtype),
        in_specs=[ANY], out_specs=ANY,
        scratch_shapes=[pltpu.SemaphoreType.DMA((N_DEV - 1,)), pltpu.SemaphoreType.DMA((N_DEV - 1,)),
                        pltpu.SemaphoreType.DMA],
    )(row)


def _scalar(i):
    return jnp.reshape(i, (1,)).astype(jnp.int32)


def _place_block(src, n_slots, slot, out_dtype, name, window=0, n_windows=1):
    rows, cols = src.shape[0], src.shape[1] // n_windows
    tr = _divisor_tile(rows, max(16, (2 * MIB) // (cols * 4)), 16)

    def body(slot_ref, s_ref, o_ref):
        o_ref[...] = s_ref[...].astype(o_ref.dtype)

    return pl.pallas_call(
        body, name=name, out_shape=jax.ShapeDtypeStruct((n_slots, rows, cols), out_dtype),
        grid_spec=pltpu.PrefetchScalarGridSpec(
            num_scalar_prefetch=1, grid=(rows // tr,), in_specs=[pl.BlockSpec((tr, cols), lambda r, sl: (r, window))],
            out_specs=pl.BlockSpec((None, tr, cols), lambda r, sl: (sl[0], r, 0))),
        compiler_params=_params(("parallel",)))(_scalar(slot), src)


def _add_halves(g, theirs, core, chip, name):
    n, half, cols = theirs.shape
    tr = _divisor_tile(half, max(16, (2 * MIB) // (cols * 4)), 16)
    per = half // tr

    def body(at_ref, a_ref, b_ref, o_ref, own_ref):
        total = (a_ref[...].astype(F32) + b_ref[...].astype(F32)).astype(o_ref.dtype)
        o_ref[...] = total

        @pl.when(pl.program_id(1) == at_ref[1])
        def _():
            own_ref[...] = total

    spec = pl.BlockSpec((None, tr, cols), lambda r, i, at: (i, r, 0))
    shape = jax.ShapeDtypeStruct(theirs.shape, BF16)
    return pl.pallas_call(
        body, name=name, out_shape=[shape, shape],
        grid_spec=pltpu.PrefetchScalarGridSpec(
            num_scalar_prefetch=1, grid=(per, n),
            in_specs=[pl.BlockSpec((None, tr, cols), lambda r, i, at: (i, at[0] * per + r, 0)), spec],
            out_specs=[spec, pl.BlockSpec((None, tr, cols), lambda r, i, at: (at[1], r, 0))]),
        compiler_params=_params(("parallel", "arbitrary")))(jnp.concatenate([_scalar(core), _scalar(chip)]), g, theirs)


def _sum_slots(a, core, name):
    n, rows, cols = a.shape
    tr = _divisor_tile(rows, max(16, (2 * MIB) // (cols * 4 * n)), 16)

    def body(c_ref, a_ref, o_ref):
        acc = a_ref[0].astype(F32)
        for i in range(1, n):
            acc = acc + a_ref[i].astype(F32)
        o_ref[...] = acc

    return pl.pallas_call(
        body, name=name, out_shape=jax.ShapeDtypeStruct((2, rows, cols), F32),
        grid_spec=pltpu.PrefetchScalarGridSpec(
            num_scalar_prefetch=1, grid=(rows // tr,),
            in_specs=[pl.BlockSpec((n, tr, cols), lambda r, c: (0, r, 0))],
            out_specs=pl.BlockSpec((None, tr, cols), lambda r, c: (c[0], r, 0))),
        compiler_params=_params(("parallel",)))(_scalar(core), a)


def _adamw_math(w, g, m, v):
    m = ADAM_B1 * m + (1.0 - ADAM_B1) * g
    v = ADAM_B2 * v + (1.0 - ADAM_B2) * (g * g)
    m_hat = m / (1.0 - ADAM_B1 ** ADAM_STEP)
    v_hat = v / (1.0 - ADAM_B2 ** ADAM_STEP)
    delta = -ADAM_LR * (m_hat / (jnp.sqrt(v_hat) + ADAM_EPS) + ADAM_WD * w)
    return delta, m, v


def _adamw_half(w, g2, m, v, which, prev, name):
    rows, cols = w.shape
    half = rows // 2
    tr = _divisor_tile(half, max(8, MIB // (cols * 4)), 8)
    per = half // tr

    def body(h_ref, w_ref, g_ref, m_ref, v_ref, *refs):
        d_ref, nm_ref, nv_ref = refs[-3:]
        d, nm, nv = _adamw_math(w_ref[...], g_ref[...], m_ref[...], v_ref[...])
        d_ref[...] = d
        nm_ref[...] = nm
        nv_ref[...] = nv

    spec = pl.BlockSpec((tr, cols), lambda r, h: (h[0] * per + r, 0))
    in_specs = [spec, pl.BlockSpec((None, tr, cols), lambda r, h: (h[0], r, 0)), spec, spec]
    args = [_scalar(which), w, g2, m, v]
    aliases = {}
    if prev is not None:
        aliases = {len(args) + i: i for i in range(3)}
        in_specs += [ANY] * 3
        args += list(prev)
    return pl.pallas_call(
        body, name=name, out_shape=[jax.ShapeDtypeStruct((rows, cols), F32)] * 3,
        grid_spec=pltpu.PrefetchScalarGridSpec(num_scalar_prefetch=1, grid=(per,), in_specs=in_specs,
                                               out_specs=[spec] * 3),
        input_output_aliases=aliases, compiler_params=_params(("parallel",)))(*args)


def _sum_rows_adamw(parts, w, m, v):
    def body(p_ref, w_ref, m_ref, v_ref, g_ref, d_ref, nm_ref, nv_ref):
        g = p_ref[0]
        for i in range(1, N_DEV):
            g = g + p_ref[i]
        d, nm, nv = _adamw_math(w_ref[...], g, m_ref[...], v_ref[...])
        g_ref[...] = g
        d_ref[...] = d
        nm_ref[...] = nm
        nv_ref[...] = nv

    shape = jax.ShapeDtypeStruct(w.shape, F32)
    return pl.pallas_call(body, name="sum_rows_adamw", out_shape=[shape] * 4)(parts, w, m, v)


LANES = 128


def _permute_scratch(rows, width):
    return pltpu.VMEM((width // LANES, rows, LANES), F32)


def _split_rows(value, scratch, dil):
    if dil == 1:
        return [value]
    rows = value.shape[0] // dil
    slabs = value.shape[1] // LANES
    for c in range(slabs):
        scratch[c] = value[:, c * LANES:(c + 1) * LANES]
    return [jnp.concatenate([scratch[c, pl.ds(r, rows, stride=dil), :] for c in range(slabs)], axis=1)
            for r in range(dil)]


def _merge_rows(ref, scratch, dil):
    if dil == 1:
        return ref[0].astype(F32)
    rows = ref.shape[1]
    slabs = ref.shape[2] // LANES
    for r in range(dil):
        part = ref[r].astype(F32)
        for c in range(slabs):
            scratch[c, pl.ds(r, rows, stride=dil), :] = part[:, c * LANES:(c + 1) * LANES]
    return jnp.concatenate([scratch[c] for c in range(slabs)], axis=1)


def _grouped_view(t, dil):
    return t.reshape(dil, t.shape[0] // dil, t.shape[1])


def _grouped_spec(dil, rows, width, index):
    return pl.BlockSpec((dil, rows // dil, width), index)


W_CHUNKS = 2


def _pick(values, j):
    out = values[-1]
    for i in range(len(values) - 2, -1, -1):
        out = jnp.where(j == i, values[i], out)
    return out


def _chunk_of(col, per_chip):
    return (col % per_chip) // (per_chip // W_CHUNKS)


def _w_block(col, per_chip):
    return col // per_chip, 0, (col % per_chip) % (per_chip // W_CHUNKS)


def _in_proj(xb, wc, blocks, j0, ncols, tn, out_dtype, prev, after, name):
    s, d = xb.shape
    per_chip = wc.shape[2] * W_CHUNKS // tn
    tm = _divisor_tile(s, 1024, 16)
    extra = [t for t in (after,) if t is not None]

    def body(*refs):
        a_ref, b_ref = refs[len(extra):len(extra) + 2]
        o_ref = refs[-1]
        o_ref[...] = jnp.dot(a_ref[...], b_ref[...], preferred_element_type=F32).astype(o_ref.dtype)

    in_specs = [pl.BlockSpec(t.shape, lambda j, m: (0, 0)) for t in extra] + [
        pl.BlockSpec((tm, d), lambda j, m: (m, 0)),
        pl.BlockSpec((None, d, tn), lambda j, m: _w_block(_pick(blocks, j), per_chip))]
    args = extra + [xb, wc]
    aliases = {}
    if prev is not None:
        aliases = {len(args): 0}
        in_specs.append(ANY)
        args.append(prev)
    return pl.pallas_call(
        body, name=name, grid=(len(blocks), s // tm), in_specs=in_specs,
        out_specs=pl.BlockSpec((tm, tn), lambda j, m: (m, _pick(blocks, j) - j0)),
        out_shape=jax.ShapeDtypeStruct((s, ncols), out_dtype), input_output_aliases=aliases,
        compiler_params=_params(("parallel", "parallel")))(*args)


def _in_proj_qkv(xb, wc, g, blocks, aw, tn, prev, after, name):
    s, d = xb.shape
    dil = DILATIONS[g]
    per_chip = wc.shape[2] * W_CHUNKS // tn
    sub = aw // tn
    tm = _divisor_tile(s, 1024, 16 * dil)
    extra = [t for t in (after,) if t is not None]

    def body(*refs):
        a_ref, b_ref = refs[len(extra):len(extra) + 2]
        o_ref, scratch = refs[-2:]
        res = jnp.dot(a_ref[...], b_ref[...], preferred_element_type=F32)
        for r, part in enumerate(_split_rows(res, scratch, dil)):
            o_ref[r] = part.astype(BF16)

    def out_index(j, m):
        col = _pick(blocks, j)
        return (col // sub) // N_GROUPS, 0, m, col % sub

    in_specs = [pl.BlockSpec(t.shape, lambda j, m: (0, 0)) for t in extra] + [
        pl.BlockSpec((tm, d), lambda j, m: (m, 0)),
        pl.BlockSpec((None, d, tn), lambda j, m: _w_block(_pick(blocks, j), per_chip))]
    args = extra + [xb, wc]
    aliases = {}
    if prev is not None:
        aliases = {len(args): 0}
        in_specs.append(ANY)
        args.append(prev)
    return pl.pallas_call(
        body, name=name, grid=(len(blocks), s // tm), in_specs=in_specs,
        out_specs=pl.BlockSpec((None, dil, tm // dil, tn), out_index),
        out_shape=jax.ShapeDtypeStruct((3, dil, s // dil, aw), BF16), input_output_aliases=aliases,
        scratch_shapes=[_permute_scratch(tm, tn)],
        compiler_params=_params(("parallel", "parallel")))(*args)


def _window_mask(first):
    qi = lax.broadcasted_iota(jnp.int32, (STEPS, 2 * STEPS), 0)
    kj = lax.broadcasted_iota(jnp.int32, (STEPS, 2 * STEPS), 1)
    lowest = jnp.where(first, STEPS, 0)
    return (kj >= qi) & (kj <= qi + STEPS) & (kj >= lowest)


def _attn_fwd(qkv, g):
    _, s, aw = qkv.shape
    heads = aw // HEAD_DIM
    n_blocks = s // STEPS
    per_seq = n_blocks // DILATIONS[g]

    def body(q_ref, kc_ref, kp_ref, vc_ref, vp_ref, o_ref, l_ref):
        mask = _window_mask(lax.rem(pl.program_id(0), per_seq) == 0)
        for h in range(heads):
            hs = slice(h * HEAD_DIM, (h + 1) * HEAD_DIM)
            kk = jnp.concatenate([kp_ref[:, hs], kc_ref[:, hs]], axis=0)
            vv = jnp.concatenate([vp_ref[:, hs], vc_ref[:, hs]], axis=0)
            sc = lax.dot_general(q_ref[:, hs], kk, NT, preferred_element_type=F32) * SCORE_SCALE
            sc = jnp.where(mask, sc, NEG_INF)
            mx = jnp.max(sc, axis=1, keepdims=True)
            e = jnp.exp(sc - mx)
            den = jnp.sum(e, axis=1, keepdims=True)
            o_ref[:, hs] = (jnp.dot(e.astype(BF16), vv, preferred_element_type=F32) / den).astype(BF16)
            l_ref[:, hs] = jnp.broadcast_to(mx + jnp.log(den), (STEPS, HEAD_DIM))

    def cur(which):
        return pl.BlockSpec((None, STEPS, aw), lambda b: (which, b, 0))

    def prev(which):
        return pl.BlockSpec((None, STEPS, aw), lambda b: (which, jnp.maximum(b - 1, 0), 0))

    out = pl.BlockSpec((STEPS, aw), lambda b: (b, 0))
    return pl.pallas_call(
        body, name=f"attn_fwd{g}", grid=(n_blocks,),
        in_specs=[cur(0), cur(1), prev(1), cur(2), prev(2)], out_specs=[out, out],
        out_shape=[jax.ShapeDtypeStruct((s, aw), BF16), jax.ShapeDtypeStruct((s, aw), F32)],
        compiler_params=_params(("parallel",)))(qkv, qkv, qkv, qkv, qkv)


def _combine_groups(os, ls, zuz, aw):
    s = zuz.shape[0]
    tr = _divisor_tile(s, 256, 8 * DILATIONS[-1])

    def body(*refs):
        o_refs, l_refs, z_ref = refs[0:3], refs[3:6], refs[6]
        oo_ref, y_ref, yt_ref = refs[7:10]
        lq_refs, scratch = refs[10:13], refs[13]
        ls_ = [_merge_rows(l_refs[g], scratch, dil) for g, dil in enumerate(DILATIONS)]
        mx = jnp.maximum(jnp.maximum(ls_[0], ls_[1]), ls_[2])
        ws = [jnp.exp(l - mx) for l in ls_]
        den = ws[0] + ws[1] + ws[2]
        o = ws[0] * _merge_rows(o_refs[0], scratch, DILATIONS[0])
        for g in range(1, N_GROUPS):
            o = o + ws[g] * _merge_rows(o_refs[g], scratch, DILATIONS[g])
        o = o / den
        z = z_ref[...].astype(F32)
        y = o * (z * _sigmoid(z))
        oo_ref[...] = o.astype(BF16)
        y_ref[...] = y.astype(BF16)
        yt_ref[...] = y.T.astype(BF16)
        for g, dil in enumerate(DILATIONS):
            for r, part in enumerate(_split_rows(mx + jnp.log(den), scratch, dil)):
                lq_refs[g][r] = part

    grouped = [_grouped_spec(dil, tr, aw, lambda r: (0, r, 0)) for dil in DILATIONS]
    one = pl.BlockSpec((tr, aw), lambda r: (r, 0))
    b16 = jax.ShapeDtypeStruct((s, aw), BF16)
    out = pl.pallas_call(
        body, name="combine_groups", grid=(s // tr,),
        in_specs=grouped + grouped + [one],
        out_specs=[one, one, pl.BlockSpec((aw, tr), lambda r: (0, r))] + grouped,
        out_shape=[b16, b16, jax.ShapeDtypeStruct((aw, s), BF16)]
        + [jax.ShapeDtypeStruct((dil, s // dil, aw), F32) for dil in DILATIONS],
        scratch_shapes=[_permute_scratch(tr, aw)],
        compiler_params=_params(("parallel",)))(
            *[_grouped_view(t, dil) for t, dil in zip(os, DILATIONS)],
            *[_grouped_view(t, dil) for t, dil in zip(ls, DILATIONS)], zuz)
    return out[0], out[1], out[2], [t.reshape(s, aw) for t in out[3:]]


def _pool_counts(row0, rows, window):
    t = row0 + lax.broadcasted_iota(jnp.int32, (rows, 1), 0)
    return jnp.minimum(t + 1, window).astype(F32)


def _pool_fwd(zuz, w_pool, pool_scale, aw, pw):
    s = zuz.shape[0]
    pg = pw // len(POOL_WINDOWS)
    tr = _divisor_tile(s, 256, 128)
    u_col, z_col = aw // pw, aw // pw + 1
    assert aw % pw == 0

    def body(u_ref, up_ref, z_ref, w_ref, sc_ref, p_ref, l_ref, y_ref, yt_ref):
        r = pl.program_id(0)
        u = u_ref[...].astype(F32)
        halo = jnp.where(r > 0, up_ref[...].astype(F32), 0.0)
        ext = jnp.concatenate([halo, u], axis=0)
        pieces, lins = [], []
        for gi, window in enumerate(POOL_WINDOWS):
            cs = slice(gi * pg, (gi + 1) * pg)
            acc = ext[:, cs]
            shift = 1
            while shift < window:
                acc = acc + pltpu.roll(acc, shift, 0)
                shift *= 2
            p = acc[POOL_HALO:] / _pool_counts(r * tr, tr, window) - u[:, cs]
            pieces.append(p)
            lins.append(jnp.dot(p.astype(BF16), w_ref[gi], preferred_element_type=F32))
        p = jnp.concatenate(pieces, axis=1)
        lin = jnp.concatenate(lins, axis=1)
        z = z_ref[...].astype(F32)
        y = lin * sc_ref[...] * (z * _sigmoid(z))
        p_ref[...] = p.astype(BF16)
        l_ref[...] = lin
        y_ref[...] = y.astype(BF16)
        yt_ref[...] = y.T.astype(BF16)

    per = tr // POOL_HALO
    out = pl.BlockSpec((tr, pw), lambda r: (r, 0))
    return pl.pallas_call(
        body, name="pool_fwd", grid=(s // tr,),
        in_specs=[pl.BlockSpec((tr, pw), lambda r: (r, u_col)),
                  pl.BlockSpec((POOL_HALO, pw), lambda r: (jnp.maximum(r * per - 1, 0), u_col)),
                  pl.BlockSpec((tr, pw), lambda r: (r, z_col)),
                  pl.BlockSpec((len(POOL_WINDOWS), pg, pg), lambda r: (0, 0, 0)),
                  pl.BlockSpec((1, pw), lambda r: (0, 0))],
        out_specs=[out, out, out, pl.BlockSpec((pw, tr), lambda r: (0, r))],
        out_shape=[jax.ShapeDtypeStruct((s, pw), BF16), jax.ShapeDtypeStruct((s, pw), F32),
                   jax.ShapeDtypeStruct((s, pw), BF16), jax.ShapeDtypeStruct((pw, s), BF16)],
        compiler_params=_params(("parallel",)))(zuz, zuz, zuz, w_pool, pool_scale)


def _proj_merge(y_attn, y_pool, wpa4, wpp4, gpre, b_gate):
    s, aw = y_attn.shape
    pw = y_pool.shape[1]
    tn = wpa4.shape[2]
    d = N_CHIPS * tn
    tm = _divisor_tile(s, 512, 128)

    def body(ya_ref, yp_ref, wa_ref, wp_ref, ga_ref, gp_ref, ba_ref, bp_ref, a_ref, p_ref, sa_ref, sp_ref, m_ref,
             mt_ref):
        a = jnp.dot(ya_ref[...], wa_ref[...], preferred_element_type=F32)
        p = jnp.dot(yp_ref[...], wp_ref[...], preferred_element_type=F32)
        sa = _sigmoid(ga_ref[...].astype(F32) + ba_ref[...])
        sp = _sigmoid(gp_ref[...].astype(F32) + bp_ref[...])
        merged = sa * a + sp * p
        a_ref[...] = a.astype(BF16)
        p_ref[...] = p.astype(BF16)
        sa_ref[...] = sa.astype(BF16)
        sp_ref[...] = sp.astype(BF16)
        m_ref[...] = merged.astype(BF16)
        mt_ref[...] = merged.T.astype(BF16)

    out = pl.BlockSpec((tm, tn), lambda n, m: (m, n))
    f = jax.ShapeDtypeStruct((s, d), BF16)
    return pl.pallas_call(
        body, name="proj_merge", grid=(N_CHIPS, s // tm),
        in_specs=[pl.BlockSpec((tm, aw), lambda n, m: (m, 0)), pl.BlockSpec((tm, pw), lambda n, m: (m, 0)),
                  pl.BlockSpec((None, aw, tn), lambda n, m: (n, 0, 0)),
                  pl.BlockSpec((None, pw, tn), lambda n, m: (n, 0, 0)),
                  pl.BlockSpec((tm, tn), lambda n, m: (m, n)), pl.BlockSpec((tm, tn), lambda n, m: (m, N_CHIPS + n)),
                  pl.BlockSpec((1, tn), lambda n, m: (0, n)), pl.BlockSpec((1, tn), lambda n, m: (0, N_CHIPS + n))],
        out_specs=[out] * 5 + [pl.BlockSpec((tn, tm), lambda n, m: (n, m))],
        out_shape=[f] * 5 + [jax.ShapeDtypeStruct((d, s), BF16)],
        compiler_params=_params(("parallel", "parallel")))(y_attn, y_pool, wpa4, wpp4, gpre, gpre, b_gate, b_gate)


def _out_norm_loss(merged, w_out, x, target, gamma, beta):
    s, d = x.shape
    tm = _divisor_tile(s, 256, 16)

    def body(m_ref, w_ref, x_ref, t_ref, g_ref, b_ref, dr_ref, drb_ref, loss_ref, dg_ref, db_ref):
        @pl.when(pl.program_id(0) == 0)
        def _():
            loss_ref[...] = jnp.zeros_like(loss_ref)
            dg_ref[...] = jnp.zeros_like(dg_ref)
            db_ref[...] = jnp.zeros_like(db_ref)

        r = ALPHA * x_ref[...] + jnp.dot(m_ref[...], w_ref[...], preferred_element_type=F32)
        mu = jnp.mean(r, axis=1, keepdims=True)
        rc = r - mu
        rstd = lax.rsqrt(jnp.mean(rc * rc, axis=1, keepdims=True) + LN_EPS)
        xhat = rc * rstd
        diff = xhat * g_ref[...] + b_ref[...] - t_ref[...]
        dy = diff / d
        loss_ref[...] += jnp.sum(diff * diff, axis=0, keepdims=True)
        dg_ref[...] += jnp.sum(dy * xhat, axis=0, keepdims=True)
        db_ref[...] += jnp.sum(dy, axis=0, keepdims=True)
        dxhat = dy * g_ref[...]
        dr = rstd * (dxhat - jnp.mean(dxhat, axis=1, keepdims=True)
                     - xhat * jnp.mean(dxhat * xhat, axis=1, keepdims=True))
        dr_ref[...] = dr
        drb_ref[...] = dr.astype(BF16)

    row = pl.BlockSpec((tm, d), lambda m: (m, 0))
    vec = pl.BlockSpec((1, d), lambda m: (0, 0))
    v = jax.ShapeDtypeStruct((1, d), F32)
    return pl.pallas_call(
        body, name="out_norm_loss", grid=(s // tm,),
        in_specs=[row, pl.BlockSpec((d, d), lambda m: (0, 0)), row, row, vec, vec],
        out_specs=[row, row, vec, vec, vec],
        out_shape=[jax.ShapeDtypeStruct((s, d), F32), jax.ShapeDtypeStruct((s, d), BF16), v, v, v],
        compiler_params=_params(("arbitrary",), vmem_mib=56))(merged, w_out, x, target, gamma, beta)


def _merge_bwd(drb, w_out, a, p, sa, sp):
    s, d = drb.shape
    tm = _divisor_tile(s, 512, 16)
    tn = d // N_CHIPS

    def body(dr_ref, w_ref, a_ref, p_ref, sa_ref, sp_ref, da_ref, dp_ref, dga_ref, dgp_ref, dba_ref, dbp_ref):
        @pl.when(pl.program_id(1) == 0)
        def _():
            dba_ref[...] = jnp.zeros_like(dba_ref)
            dbp_ref[...] = jnp.zeros_like(dbp_ref)

        dm = lax.dot_general(dr_ref[...], w_ref[...], NT, preferred_element_type=F32)
        sa = sa_ref[...].astype(F32)
        sp = sp_ref[...].astype(F32)
        da_ref[...] = (dm * sa).astype(BF16)
        dp_ref[...] = (dm * sp).astype(BF16)
        dga = dm * a_ref[...].astype(F32) * sa * (1.0 - sa)
        dgp = dm * p_ref[...].astype(F32) * sp * (1.0 - sp)
        dga_ref[...] = dga.astype(BF16)
        dgp_ref[...] = dgp.astype(BF16)
        dba_ref[...] += jnp.sum(dga, axis=0, keepdims=True)
        dbp_ref[...] += jnp.sum(dgp, axis=0, keepdims=True)

    blk = pl.BlockSpec((tm, tn), lambda n, m: (m, n))
    vec = pl.BlockSpec((1, tn), lambda n, m: (0, n))
    b16 = jax.ShapeDtypeStruct((s, d), BF16)
    v = jax.ShapeDtypeStruct((1, d), F32)
    return pl.pallas_call(
        body, name="merge_bwd", grid=(N_CHIPS, s // tm),
        in_specs=[pl.BlockSpec((tm, d), lambda n, m: (m, 0)), pl.BlockSpec((tn, d), lambda n, m: (n, 0)),
                  blk, blk, blk, blk],
        out_specs=[blk, blk, blk, blk, vec, vec], out_shape=[b16, b16, b16, b16, v, v],
        compiler_params=_params(("parallel", "arbitrary")))(drb, w_out, a, p, sa, sp)


def _proj_t(dy_ref, w_ref, tn):
    acc = None
    for n in range(N_CHIPS):
        t = lax.dot_general(dy_ref[:, n * tn:(n + 1) * tn], w_ref[n], NT, preferred_element_type=F32)
        acc = t if acc is None else acc + t
    return acc


def _attn_gate_bwd(da, wpa4, zuz, o):
    s, d = da.shape
    aw, tn = wpa4.shape[1], wpa4.shape[2]
    heads = aw // HEAD_DIM
    tm = _divisor_tile(s, 256, 16 * DILATIONS[-1])

    def body(*refs):
        da_ref, w_ref, z_ref, o_ref, dz_ref = refs[:5]
        do_refs, dd_refs, scratch = refs[5:8], refs[8:11], refs[11]
        dy = _proj_t(da_ref, w_ref, tn)
        z, o = z_ref[...].astype(F32), o_ref[...].astype(F32)
        sg = _sigmoid(z)
        do = dy * (z * sg)
        dz_ref[...] = (dy * o * _dsilu(z, sg)).astype(BF16)
        prod = do * o
        dd = jnp.concatenate(
            [jnp.broadcast_to(jnp.sum(prod[:, h * HEAD_DIM:(h + 1) * HEAD_DIM], axis=1, keepdims=True),
                              (tm, HEAD_DIM)) for h in range(heads)], axis=1)
        for g, dil in enumerate(DILATIONS):
            for r, part in enumerate(_split_rows(do, scratch, dil)):
                do_refs[g][r] = part.astype(BF16)
            for r, part in enumerate(_split_rows(dd, scratch, dil)):
                dd_refs[g][r] = part

    row = pl.BlockSpec((tm, aw), lambda m: (m, 0))
    grouped = [_grouped_spec(dil, tm, aw, lambda m: (0, m, 0)) for dil in DILATIONS]
    out = pl.pallas_call(
        body, name="attn_gate_bwd", grid=(s // tm,),
        in_specs=[pl.BlockSpec((tm, d), lambda m: (m, 0)), pl.BlockSpec((N_CHIPS, aw, tn), lambda m: (0, 0, 0)),
                  row, row],
        out_specs=[row] + grouped + grouped,
        out_shape=[jax.ShapeDtypeStruct((s, aw), BF16)]
        + [jax.ShapeDtypeStruct((dil, s // dil, aw), BF16) for dil in DILATIONS]
        + [jax.ShapeDtypeStruct((dil, s // dil, aw), F32) for dil in DILATIONS],
        scratch_shapes=[_permute_scratch(tm, aw)],
        compiler_params=_params(("parallel",)))(da, wpa4, zuz, o)
    return out[0], [t.reshape(s, aw) for t in out[1:4]], [t.reshape(s, aw) for t in out[4:7]]


def _pool_gate_bwd(dp_in, wpp4, zuz, lin, pooled, w_pool, pool_scale, aw):
    s, d = dp_in.shape
    pw, tn = wpp4.shape[1], wpp4.shape[2]
    n_win = len(POOL_WINDOWS)
    pg = pw // n_win
    tm = _divisor_tile(s, 256, 16)
    z_col = aw // pw + 1

    def body(dp_ref, w_ref, z_ref, l_ref, p_ref, wp_ref, sc_ref, dz_ref, dpo_ref, dw_ref, ds_ref):
        @pl.when(pl.program_id(0) == 0)
        def _():
            dw_ref[...] = jnp.zeros_like(dw_ref)
            ds_ref[...] = jnp.zeros_like(ds_ref)

        dy = _proj_t(dp_ref, w_ref, tn)
        z, lin_ = z_ref[...].astype(F32), l_ref[...]
        sg = _sigmoid(z)
        dypp = dy * (z * sg)
        dz_ref[...] = (dy * (lin_ * sc_ref[...]) * _dsilu(z, sg)).astype(BF16)
        ds_ref[...] += jnp.sum(dypp * lin_, axis=0, keepdims=True)
        dlin = (dypp * sc_ref[...]).astype(BF16)
        for gi in range(n_win):
            cs = slice(gi * pg, (gi + 1) * pg)
            pt = p_ref[:, cs].astype(F32).T.astype(BF16)
            dw_ref[gi] += jnp.dot(pt, dlin[:, cs], preferred_element_type=F32)
            dpo_ref[:, cs] = lax.dot_general(dlin[:, cs], wp_ref[gi], NT, preferred_element_type=F32)

    row = pl.BlockSpec((tm, pw), lambda m: (m, 0))
    return pl.pallas_call(
        body, name="pool_gate_bwd", grid=(s // tm,),
        in_specs=[pl.BlockSpec((tm, d), lambda m: (m, 0)), pl.BlockSpec((N_CHIPS, pw, tn), lambda m: (0, 0, 0)),
                  pl.BlockSpec((tm, pw), lambda m: (m, z_col)), row, row,
                  pl.BlockSpec((n_win, pg, pg), lambda m: (0, 0, 0)), pl.BlockSpec((1, pw), lambda m: (0, 0))],
        out_specs=[row, row, pl.BlockSpec((n_win, pg, pg), lambda m: (0, 0, 0)),
                   pl.BlockSpec((1, pw), lambda m: (0, 0))],
        out_shape=[jax.ShapeDtypeStruct((s, pw), BF16), jax.ShapeDtypeStruct((s, pw), F32),
                   jax.ShapeDtypeStruct((n_win, pg, pg), F32), jax.ShapeDtypeStruct((1, pw), F32)],
        compiler_params=_params(("arbitrary",)))(dp_in, wpp4, zuz, lin, pooled, w_pool, pool_scale)


def _pool_bwd(dpooled):
    s, pw = dpooled.shape
    pg = pw // len(POOL_WINDOWS)
    tr = _divisor_tile(s, 256, POOL_HALO)
    per = tr // POOL_HALO
    n_tiles = s // tr

    def body(c_ref, n_ref, du_ref):
        r = pl.program_id(0)
        cur = c_ref[...]
        halo = jnp.where(r < n_tiles - 1, n_ref[...], 0.0)
        ext = jnp.concatenate([cur, halo], axis=0)
        rows = tr + POOL_HALO
        for gi, window in enumerate(POOL_WINDOWS):
            cs = slice(gi * pg, (gi + 1) * pg)
            acc = ext[:, cs] / _pool_counts(r * tr, rows, window)
            shift = 1
            while shift < window:
                acc = acc + pltpu.roll(acc, rows - shift, 0)
                shift *= 2
            du_ref[:, cs] = (acc[:tr] - cur[:, cs]).astype(BF16)

    return pl.pallas_call(
        body, name="pool_bwd", grid=(n_tiles,),
        in_specs=[pl.BlockSpec((tr, pw), lambda r: (r, 0)),
                  pl.BlockSpec((POOL_HALO, pw), lambda r: (jnp.minimum((r + 1) * per, s // POOL_HALO - 1), 0))],
        out_specs=pl.BlockSpec((tr, pw), lambda r: (r, 0)),
        out_shape=jax.ShapeDtypeStruct((s, pw), BF16), compiler_params=_params(("parallel",)))(dpooled, dpooled)


def _attn_bwd(qkv, do, lse, dd, g):
    _, s, aw = qkv.shape
    heads = aw // HEAD_DIM
    n_blocks = s // STEPS
    per_seq = n_blocks // DILATIONS[g]

    def body(q_ref, do_ref, l_ref, dd_ref, kc_ref, kp_ref, vc_ref, vp_ref, out_ref, cq_ref, ck_ref, cv_ref):
        b = pl.program_id(0)

        @pl.when(b == 0)
        def _():
            cq_ref[...] = jnp.zeros_like(cq_ref)
            ck_ref[...] = jnp.zeros_like(ck_ref)
            cv_ref[...] = jnp.zeros_like(cv_ref)

        out_ref[0] = cq_ref[...].astype(BF16)

        @pl.when(b < n_blocks)
        def _():
            mask = _window_mask(lax.rem(b, per_seq) == 0)
            for h in range(heads):
                hs = slice(h * HEAD_DIM, (h + 1) * HEAD_DIM)
                q, do_ = q_ref[:, hs], do_ref[:, hs]
                kk = jnp.concatenate([kp_ref[:, hs], kc_ref[:, hs]], axis=0)
                vv = jnp.concatenate([vp_ref[:, hs], vc_ref[:, hs]], axis=0)
                lse_ = jnp.concatenate([l_ref[:, hs], l_ref[:, hs]], axis=1)
                dd_ = jnp.concatenate([dd_ref[:, hs], dd_ref[:, hs]], axis=1)
                sc = lax.dot_general(q, kk, NT, preferred_element_type=F32) * SCORE_SCALE
                prob = jnp.where(mask, jnp.exp(sc - lse_), 0.0)
                dprob = lax.dot_general(do_, vv, NT, preferred_element_type=F32)
                dsc = prob * (dprob - dd_) * SCORE_SCALE
                cq_ref[:, hs] = jnp.dot(dsc.astype(BF16), kk, preferred_element_type=F32)
                dkk = jnp.dot(dsc.T.astype(BF16), q, preferred_element_type=F32)
                dvv = jnp.dot(prob.T.astype(BF16), do_, preferred_element_type=F32)
                out_ref[1, :, hs] = (ck_ref[:, hs] + dkk[:STEPS]).astype(BF16)
                out_ref[2, :, hs] = (cv_ref[:, hs] + dvv[:STEPS]).astype(BF16)
                ck_ref[:, hs] = dkk[STEPS:]
                cv_ref[:, hs] = dvv[STEPS:]

        @pl.when(b == n_blocks)
        def _():
            out_ref[1] = ck_ref[...].astype(BF16)
            out_ref[2] = cv_ref[...].astype(BF16)

    last = n_blocks - 1

    def cur(which):
        return pl.BlockSpec((None, STEPS, aw), lambda b: (which, jnp.minimum(b, last), 0))

    def prev(which):
        return pl.BlockSpec((None, STEPS, aw), lambda b: (which, jnp.clip(b - 1, 0, last), 0))

    row = pl.BlockSpec((STEPS, aw), lambda b: (jnp.minimum(b, last), 0))
    return pl.pallas_call(
        body, name=f"attn_bwd{g}", grid=(n_blocks + 1,),
        in_specs=[cur(0), row, row, row, cur(1), prev(1), cur(2), prev(2)],
        out_specs=pl.BlockSpec((3, STEPS, aw), lambda b: (0, jnp.clip(b - 1, 0, last), 0)),
        out_shape=jax.ShapeDtypeStruct((3, s, aw), BF16),
        scratch_shapes=[pltpu.VMEM((STEPS, aw), F32)] * 3,
        compiler_params=_params(("arbitrary",)))(qkv, do, lse, dd, qkv, qkv, qkv, qkv)


def _weight_grad(at, b, tn, col_blocks, name):
    m, k = at.shape
    n = b.shape[1]
    tm = _divisor_tile(m, 1024, 16)
    tk = _divisor_tile(k, 2048, 128)
    nk = k // tk

    def body(a_ref, b_ref, o_ref, acc_ref):
        kk = pl.program_id(2)

        @pl.when(kk == 0)
        def _():
            acc_ref[...] = jnp.zeros_like(acc_ref)

        acc_ref[...] += jnp.dot(a_ref[...], b_ref[...], preferred_element_type=F32)

        @pl.when(kk == nk - 1)
        def _():
            o_ref[...] = acc_ref[...].astype(BF16)

    if col_blocks:
        out_spec = pl.BlockSpec((None, tm, tn), lambda i, j, kk: (j, i, 0))
        out_shape = jax.ShapeDtypeStruct((n // tn, m, tn), BF16)
    else:
        out_spec = pl.BlockSpec((tm, tn), lambda i, j, kk: (i, j))
        out_shape = jax.ShapeDtypeStruct((m, n), BF16)
    return pl.pallas_call(
        body, name=name, grid=(m // tm, n // tn, nk),
        in_specs=[pl.BlockSpec((tm, tk), lambda i, j, kk: (i, kk)), pl.BlockSpec((tk, tn), lambda i, j, kk: (kk, j))],
        out_specs=out_spec, out_shape=out_shape, scratch_shapes=[pltpu.VMEM((tm, tn), F32)],
        compiler_params=_params(("parallel", "parallel", "arbitrary")))(at, b)


def _w_in_grad_part(xt, b, col_of, n_local, tn, w_shape, prev, name):
    d, s = xt.shape
    per_chip = w_shape[2] // tn
    tm = _divisor_tile(d, 1024, 16)
    tk = _divisor_tile(s, 2048, 128)
    nk = s // tk

    def body(*refs):
        a_ref, b_ref, o_ref, acc_ref = refs[0], refs[1], refs[-2], refs[-1]
        kk = pl.program_id(2)

        @pl.when(kk == 0)
        def _():
            acc_ref[...] = jnp.zeros_like(acc_ref)

        acc_ref[...] += jnp.dot(a_ref[...], b_ref[...], preferred_element_type=F32)

        @pl.when(kk == nk - 1)
        def _():
            o_ref[...] = acc_ref[...].astype(BF16)

    if b.ndim == 3:
        sub = b.shape[2] // tn
        b_spec = pl.BlockSpec((None, tk, tn), lambda j, i, kk: (j // sub, kk, j % sub))
    else:
        b_spec = pl.BlockSpec((tk, tn), lambda j, i, kk: (kk, j))
    in_specs = [pl.BlockSpec((tm, tk), lambda j, i, kk: (i, kk)), b_spec]
    args = [xt, b]
    aliases = {}
    if prev is not None:
        in_specs.append(ANY)
        args.append(prev)
        aliases = {2: 0}
    return pl.pallas_call(
        body, name=name, grid=(n_local, d // tm, nk), in_specs=in_specs,
        out_specs=pl.BlockSpec((None, tm, tn), lambda j, i, kk: (col_of(j) // per_chip, i, col_of(j) % per_chip)),
        out_shape=jax.ShapeDtypeStruct(w_shape, BF16), scratch_shapes=[pltpu.VMEM((tm, tn), F32)],
        input_output_aliases=aliases,
        compiler_params=_params(("parallel", "parallel", "arbitrary")))(*args)


def _x_grad(dqkv, rest, wc, chunk, init, init_scale, aw, tn, after=None):
    s, d = init.shape
    sub = aw // tn
    n_qkv = 3 * N_GROUPS * sub
    los, lo = [], n_qkv
    for p in rest:
        los.append(lo)
        lo += p.shape[1] // tn
    per_chip = lo // N_CHIPS
    per = per_chip // W_CHUNKS
    n_local = lo // W_CHUNKS
    tm = _divisor_tile(s, 512, 16 * DILATIONS[-1])

    def col(jl):
        return (jl // per) * per_chip + chunk * per + jl % per

    ordered = [] if after is None else [after]

    def body(*refs):
        refs = refs[len(ordered):]
        q_refs, r_refs = refs[:N_GROUPS], refs[N_GROUPS:N_GROUPS + len(rest)]
        w_ref, init_ref, o_ref, acc_ref, scratch = refs[-5:]
        jl = pl.program_id(1)
        j = col(jl)

        @pl.when(jl == 0)
        def _():
            acc_ref[...] = init_scale * init_ref[...]

        for g, dil in enumerate(DILATIONS):
            @pl.when((j < n_qkv) & (lax.rem(j // sub, N_GROUPS) == g))
            def _(g=g, dil=dil):
                rows = _merge_rows(q_refs[g], scratch, dil).astype(BF16)
                acc_ref[...] += lax.dot_general(rows, w_ref[...], NT, preferred_element_type=F32)

        for p_ref, lo_, piece in zip(r_refs, los, rest):
            @pl.when((j >= lo_) & (j < lo_ + piece.shape[1] // tn))
            def _(p_ref=p_ref):
                acc_ref[...] += lax.dot_general(p_ref[...], w_ref[...], NT, preferred_element_type=F32)

        @pl.when(jl == n_local - 1)
        def _():
            o_ref[...] = acc_ref[...]

    def qkv_spec(dil):
        def index(i, jl):
            j = col(jl)
            region = jnp.minimum(j // sub, 3 * N_GROUPS - 1)
            return region // N_GROUPS, 0, i, jnp.where(j < n_qkv, j % sub, 0)

        return pl.BlockSpec((None, dil, tm // dil, tn), index)

    def rest_spec(lo_, piece):
        n = piece.shape[1] // tn
        return pl.BlockSpec((tm, tn), lambda i, jl: (i, jnp.clip(col(jl) - lo_, 0, n - 1)))

    row = pl.BlockSpec((tm, d), lambda i, jl: (i, 0))
    return pl.pallas_call(
        body, name=f"x_grad{chunk}", grid=(s // tm, n_local),
        in_specs=[pl.BlockSpec(t.shape, lambda i, jl: (0, 0)) for t in ordered]
        + [qkv_spec(dil) for dil in DILATIONS] + [rest_spec(lo_, p) for lo_, p in zip(los, rest)]
        + [pl.BlockSpec((None, d, tn), lambda i, jl: (jl // per, 0, jl % per)), row],
        out_specs=row, out_shape=jax.ShapeDtypeStruct((s, d), F32),
        scratch_shapes=[pltpu.VMEM((tm, d), F32), _permute_scratch(tm, tn)],
        compiler_params=_params(("parallel", "arbitrary"), vmem_mib=56))(
            *ordered, *[t.reshape(3, dil, s // dil, aw) for t, dil in zip(dqkv, DILATIONS)], *rest, wc, init)


def _prepare_x(x, after=None):
    s, d = x.shape
    tc = 2 * LANES
    slabs = tc // LANES
    ordered = [] if after is None else [after]

    def body(*refs):
        x_ref, xb_ref = refs[len(ordered):len(ordered) + 2]
        xt_refs, scratch = refs[len(ordered) + 2:len(ordered) + 2 + N_GROUPS], refs[-1]
        t = x_ref[...]
        xb_ref[...] = t.astype(BF16)
        for c in range(slabs):
            scratch[c] = t[:, c * LANES:(c + 1) * LANES]
        for g, dil in enumerate(DILATIONS):
            length = s // dil
            for r in range(dil):
                part = t if dil == 1 else jnp.concatenate(
                    [scratch[c, pl.ds(r, length, stride=dil), :] for c in range(slabs)], axis=1)
                xt_refs[g][:, r * length:(r + 1) * length] = part.T.astype(BF16)

    col = pl.BlockSpec((s, tc), lambda j: (0, j))
    row = pl.BlockSpec((tc, s), lambda j: (j, 0))
    t_shape = jax.ShapeDtypeStruct((d, s), BF16)
    out = pl.pallas_call(
        body, name="prepare_x", grid=(d // tc,),
        in_specs=[pl.BlockSpec(t.shape, lambda j: (0, 0)) for t in ordered] + [col],
        out_specs=[col] + [row] * N_GROUPS,
        out_shape=[jax.ShapeDtypeStruct((s, d), BF16)] + [t_shape] * N_GROUPS,
        scratch_shapes=[_permute_scratch(s, tc)], compiler_params=_params(("parallel",)))(*ordered, x)
    return out[0], out[1:]


def _local_step(x, target, w_chunk, w_width, b_gate, pool_scale, gamma, beta, aw, pw, small_weights,
                start_exchange=None, first_token=None):
    s, d = x.shape
    tn = _col_tile(aw, pw, w_width)
    sub = aw // tn
    per_chip = w_width // tn
    qkv_w = 3 * N_GROUPS * aw
    w_shape = (N_CHIPS, d, w_width)

    regions = [dict(kind=g, blocks=[(which * N_GROUPS + g) * sub + i for which in range(3) for i in range(sub)])
               for g in range(N_GROUPS)]
    lo = qkv_w // tn
    for name, width in (("zuz", aw + 2 * pw), ("gates", 2 * d)):
        regions.append(dict(kind=name, blocks=list(range(lo, lo + width // tn)), j0=lo, width=width))
        lo += width // tn
    results = [None] * len(regions)
    xb, xts = _prepare_x(x, first_token)
    wcs, after = [], xb
    for ch in range(W_CHUNKS):
        wc, token = w_chunk(ch, after)
        wcs.append(wc)
        for i, region in enumerate(regions):
            blocks = [b for b in region["blocks"] if _chunk_of(b, per_chip) == ch]
            if not blocks:
                continue
            if region["kind"] in range(N_GROUPS):
                results[i] = _in_proj_qkv(xb, wc, region["kind"], blocks, aw, tn, results[i], token,
                                          f"in_proj_qkv{region['kind']}_{ch}")
            else:
                results[i] = _in_proj(xb, wc, blocks, region["j0"], region["width"], tn, BF16, results[i], token,
                                      f"in_proj_{region['kind']}_{ch}")
            token = None
            after = results[i]
    qkv = [results[g].reshape(3, s, aw) for g in range(N_GROUPS)]
    zuz, gpre = results[N_GROUPS], results[N_GROUPS + 1]

    attn = [_attn_fwd(qkv[g], g) for g in range(N_GROUPS)]
    o, y_attn, y_attn_t, lse = _combine_groups([a[0] for a in attn], [a[1] for a in attn], zuz, aw)
    w_pool, wpa4, wpp4, w_out = small_weights(o)
    pooled, lin, y_pool, y_pool_t = _pool_fwd(zuz, w_pool, pool_scale, aw, pw)
    a, p, sa, sp, merged, merged_t = _proj_merge(y_attn, y_pool, wpa4, wpp4, gpre, b_gate)
    dr, drb, loss_lanes, d_gamma, d_beta = _out_norm_loss(merged, w_out, x, target, gamma, beta)

    da, dp, d_gpre_a, d_gpre_p, d_b_a, d_b_p = _merge_bwd(drb, w_out, a, p, sa, sp)
    d_b_gate = jnp.concatenate([d_b_a, d_b_p], axis=1)
    d_w_out = _weight_grad(merged_t, drb, d // N_CHIPS, False, "w_out_grad")
    d_wpa4 = _weight_grad(y_attn_t, da, d // N_CHIPS, True, "w_proj_attn_grad")
    d_wpp4 = _weight_grad(y_pool_t, dp, d // N_CHIPS, True, "w_proj_pool_grad")
    d_z_attn, d_o, dd = _attn_gate_bwd(da, wpa4, zuz, o)
    d_z_pool, d_pooled, d_w_pool, d_pool_scale = _pool_gate_bwd(dp, wpp4, zuz, lin, pooled, w_pool, pool_scale, aw)
    d_u = _pool_bwd(d_pooled)
    dqkv = [_attn_bwd(qkv[g], d_o[g], lse[g], dd[g], g) for g in range(N_GROUPS)]

    rest = [d_z_attn, d_u, d_z_pool, d_gpre_a, d_gpre_p]
    d_w_in4 = None
    for g in range(N_GROUPS):
        d_w_in4 = _w_in_grad_part(xts[g], dqkv[g], lambda j, g=g: ((j // sub) * N_GROUPS + g) * sub + j % sub,
                                  3 * sub, tn, w_shape, d_w_in4, f"w_in_grad_qkv{g}")
    lo = qkv_w // tn
    for i, piece in enumerate(rest):
        n_local = piece.shape[1] // tn
        d_w_in4 = _w_in_grad_part(xts[0], piece, lambda j, lo=lo: lo + j, n_local, tn, w_shape, d_w_in4,
                                  f"w_in_grad_rest{i}")
        lo += n_local
    grads = dict(loss_lanes=loss_lanes, w_in=d_w_in4, b_gate=d_b_gate, w_pool=d_w_pool,
                 pool_scale=d_pool_scale, w_proj_attn=d_wpa4, w_proj_pool=d_wpp4, w_out=d_w_out,
                 ln_gamma=d_gamma, ln_beta=d_beta)
    token = None if start_exchange is None else start_exchange(grads)
    d_x, scale = dr, ALPHA
    for ch in range(W_CHUNKS):
        d_x = _x_grad(dqkv, rest, wcs[ch], ch, d_x, scale, aw, tn, token)
        token, scale = None, 1.0
    grads["d_x"] = d_x
    return grads


def _pack_small(wpa, wpp, w_out, w_pool):
    width = wpa.shape[1]
    return jnp.concatenate([wpa, wpp, w_out.reshape(-1, width), w_pool.reshape(-1, width)], axis=0)


def _unpack_small(packed, aw, pw, d, pg):
    lead = packed.shape[:-2]
    width = d // N_CHIPS
    r0, r1, r2 = aw, aw + pw, aw + pw + d
    return (packed[..., :r0, :], packed[..., r0:r1, :], packed[..., r1:r2, :].reshape(lead + (width, d)),
            packed[..., r2:, :].reshape(lead + (len(POOL_WINDOWS), pg // N_CHIPS, pg)))


def _pack_rows(vectors, rows):
    flat = jnp.concatenate([v.reshape(-1) for v in vectors])
    return jnp.pad(flat, (0, rows * 128 - flat.shape[0])).reshape(rows, 128)


def _unpack_rows(packed, sizes):
    flat, out, lo = packed.reshape(-1), [], 0
    for n in sizes:
        out.append(flat[lo:lo + n].reshape(1, n))
        lo += n
    return out


def kernel(x, w_in, b_gate, w_pool, pool_scale, w_proj_attn, w_proj_pool, w_out, ln_gamma, ln_beta, loss_target, m_w_in, m_b_gate, m_w_pool, m_pool_scale, m_w_proj_attn, m_w_proj_pool, m_w_out, m_ln_gamma, m_ln_beta, v_w_in, v_b_gate, v_w_pool, v_pool_scale, v_w_proj_attn, v_w_proj_pool, v_w_out, v_ln_gamma, v_ln_beta):
    s, d = x.shape[1], x.shape[2]
    aw, pw = w_proj_attn.shape[1], w_proj_pool.shape[1]
    pg = w_pool.shape[3]
    n_win = len(POOL_WINDOWS)

    def small(wpa, wpp, wo, wpl):
        return _pack_small(wpa[0], wpp[0], wo[0], wpl[0])

    chip = 2 * lax.axis_index("x") + lax.axis_index("y")
    core = lax.axis_index("c")

    w_small = small(w_proj_attn, w_proj_pool, w_out, w_pool)
    placed = [_place_block(w_in[0], N_CHIPS, chip, BF16, f"place_w_in{ch}", ch, W_CHUNKS) for ch in range(W_CHUNKS)]
    placed_small = _place_block(w_small, N_CHIPS, chip, BF16, "place_w_small")
    flight = {"chunk": _halves_start(placed[0], placed_small, "gather_w_in0_start")}
    first_token = flight["chunk"][2]

    def w_chunk(ch, after):
        sems, thru, _ = flight["chunk"]
        landed = _halves_wait(sems, thru, after, f"gather_w_in{ch}_wait")
        if ch + 1 < W_CHUNKS:
            flight["chunk"] = _halves_start(placed[ch + 1], landed, f"gather_w_in{ch + 1}_start")
            token = flight["chunk"][2]
        else:
            flight["small"] = _broadcast_start(placed_small, landed, "gather_small_start")
            token = flight["small"][2]
        return _forward_halves(landed, f"forward_w_in{ch}"), token

    def small_weights(after):
        sems, thru, _ = flight["small"]
        small4 = _broadcast_wait(sems, thru, after, "gather_small_wait")
        wpa4, wpp4, w_out4, w_pool4 = _unpack_small(small4, aw, pw, d, pg)
        return w_pool4.transpose(1, 0, 2, 3).reshape(n_win, pg, pg), wpa4, wpp4, w_out4.reshape(d, d)

    exchange = {}

    def start_exchange(g):
        g_pool4 = g["w_pool"].reshape(n_win, N_CHIPS, pg // N_CHIPS, pg).transpose(1, 0, 2, 3).astype(BF16)
        g_out4 = g["w_out"].reshape(N_CHIPS, d // N_CHIPS, d)
        g_small4 = jnp.concatenate([g["w_proj_attn"], g["w_proj_pool"], g_out4.reshape(N_CHIPS, -1, d // N_CHIPS),
                                    g_pool4.reshape(N_CHIPS, -1, d // N_CHIPS)], axis=1)
        theirs_big, theirs_small = _swap_halves([g["w_in"], g_small4])
        chip_big, placed_big = _add_halves(g["w_in"], theirs_big, core, chip, "add_cores_big")
        chip_small, placed_small = _add_halves(g_small4, theirs_small, core, chip, "add_cores_small")
        sems, sums, placed, token = _scatter_start([chip_big, chip_small], [placed_big, placed_small])
        exchange.update(sems=sems, sums=sums, placed=placed)
        return token

    g = _local_step(x[0], loss_target[0], w_chunk, w_in.shape[2], b_gate, pool_scale, ln_gamma, ln_beta, aw, pw,
                    small_weights, start_exchange, first_token)
    got_big, got_small = _scatter_wait(exchange["sems"], exchange["sums"], exchange["placed"], g["d_x"])
    join_sems, halves = _join_start([_sum_slots(got_big, core, "sum_chips_big"),
                                     _sum_slots(got_small, core, "sum_chips_small")])
    mv_small = (small(m_w_proj_attn, m_w_proj_pool, m_w_out, m_w_pool),
                small(v_w_proj_attn, v_w_proj_pool, v_w_out, v_w_pool))
    upd_in = _adamw_half(w_in[0], halves[0], m_w_in[0], v_w_in[0], core, None, "adamw_w_in_own")
    upd_small = _adamw_half(w_small, halves[1], *mv_small, core, None, "adamw_small_own")
    grad_w_in, grad_small = _join_wait(join_sems, halves, upd_small[0])
    upd_in = _adamw_half(w_in[0], grad_w_in, m_w_in[0], v_w_in[0], 1 - core, upd_in, "adamw_w_in_other")
    upd_small = _adamw_half(w_small, grad_small, *mv_small, 1 - core, upd_small, "adamw_small_other")
    grad_w_in = grad_w_in.reshape(-1, grad_w_in.shape[2])
    grad_small = grad_small.reshape(-1, grad_small.shape[2])

    sizes = [b_gate.shape[1], pool_scale.shape[1], d, d, 1]
    rows = -(-sum(sizes) // (8 * 128)) * 8
    loss_part = (0.5 / d) * jnp.sum(g["loss_lanes"]).reshape(1, 1)
    parts = _gather_rows(_pack_rows([g["b_gate"], g["pool_scale"], g["ln_gamma"], g["ln_beta"], loss_part], rows))
    zero = jnp.zeros((1, 1), F32)
    packed = [_pack_rows(vs, rows) for vs in ([b_gate, pool_scale, ln_gamma, ln_beta, zero],
                                              [m_b_gate, m_pool_scale, m_ln_gamma, m_ln_beta, zero],
                                              [v_b_gate, v_pool_scale, v_ln_gamma, v_ln_beta, zero])]
    rep = [_unpack_rows(t, sizes) for t in _sum_rows_adamw(parts, *packed)]
    loss = rep[0][4].reshape(())

    def leaves(big, packed_small, replicated):
        wpa_, wpp_, wo_, wpl_ = _unpack_small(packed_small, aw, pw, d, pg)
        return [big[None], replicated[0], wpl_[None], replicated[1], wpa_[None], wpp_[None], wo_[None],
                replicated[2], replicated[3]]

    out = [loss, g["d_x"][None]]
    out += leaves(grad_w_in, grad_small, rep[0])
    for i in range(3):
        out += leaves(upd_in[i], upd_small[i], rep[1 + i])
    return tuple(out)
```

```python
import math

import jax
import jax.numpy as jnp
from jax import lax
from jax.experimental import pallas as pl
from jax.experimental.pallas import tpu as pltpu

F32 = jnp.float32
BF16 = jnp.bfloat16
MESH = pl.DeviceIdType.MESH
ANY = pl.BlockSpec(memory_space=pl.ANY)

HEAD_DIM = 128
STEPS = 128
DILATIONS = (1, 4, 16)
N_GROUPS = len(DILATIONS)
POOL_WINDOWS = (2, 4, 8, 16)
POOL_HALO = 16
N_CHIPS = 4
N_DEV = 8
ALPHA = 2.0 ** 0.25
LN_EPS = 1e-5
NEG_INF = -1e30
SCORE_SCALE = HEAD_DIM ** -0.5
ADAM_LR = 0.001
ADAM_B1 = 0.9
ADAM_B2 = 0.999
ADAM_EPS = 1e-08
ADAM_WD = 0.01
ADAM_STEP = 10
MIB = 2 ** 20
NT = (((1,), (1,)), ((), ()))
DMA_STREAMS = 8


def _params(semantics=None, vmem_mib=48):
    return pltpu.CompilerParams(dimension_semantics=semantics, vmem_limit_bytes=vmem_mib * MIB)


def _divisor_tile(n, target, multiple):
    best = None
    for t in range(multiple, min(n, target) + 1, multiple):
        if n % t == 0:
            best = t
    assert best is not None, (n, target, multiple)
    return best


def _col_tile(*widths):
    g = 0
    for w in widths:
        g = math.gcd(g, w)
    return _divisor_tile(g, 1024, 128)


def _sigmoid(z):
    return jax.nn.sigmoid(z)


def _dsilu(z, sg):
    return sg * (1.0 + z * (1.0 - sg))


def _place():
    x, y, c = lax.axis_index("x"), lax.axis_index("y"), lax.axis_index("c")
    others = [(1 - x, y), (x, 1 - y), (1 - x, 1 - y)]
    return x, y, c, (x, y, 1 - c), others


def _remote(src, dst, send_sem, recv_sem, dev):
    return pltpu.make_async_remote_copy(src_ref=src, dst_ref=dst, send_sem=send_sem, recv_sem=recv_sem,
                                        device_id=dev, device_id_type=MESH)


def _row_pieces(n_rows, streams=DMA_STREAMS, multiple=16):
    size = -(-n_rows // (streams * multiple)) * multiple
    return [(lo, min(size, n_rows - lo)) for lo in range(0, n_rows, size)]


def _start_streams(make, n_rows):
    for lo, size in _row_pieces(n_rows):
        make(pl.ds(lo, size)).start()


def _half_copies(buf, send_sems, recv_sems):
    x, y, c, _, others = _place()
    half = buf.shape[1] // 2
    slab = buf.at[2 * x + y, pl.ds(c * half, half)]
    return [_remote(slab, slab, send_sems[j], recv_sems[j], (ox, oy, c)) for j, (ox, oy) in enumerate(others)]


def _halves_start(placed, after, name):
    k = N_CHIPS - 1

    def body(buf, after_ref, *refs):
        send_sems, recv_sems, token = refs[:k], refs[k:2 * k], refs[-1]
        for cp in _half_copies(buf, send_sems, recv_sems):
            cp.start()
        token[...] = jnp.zeros_like(token)

    out = pl.pallas_call(
        body, name=name,
        out_shape=[pltpu.SemaphoreType.DMA(())] * (2 * k) + [pltpu.HBM(placed.shape, placed.dtype),
                                                             jax.ShapeDtypeStruct((8, 128), F32)],
        in_specs=[HBM, ANY], out_specs=[SEM] * (2 * k) + [HBM, pl.BlockSpec(memory_space=pltpu.VMEM)],
        input_output_aliases={0: 2 * k},
        compiler_params=pltpu.CompilerParams(has_side_effects=DATAFLOW),
    )(pltpu.with_memory_space_constraint(placed, pltpu.HBM), after)
    return out[:2 * k], out[2 * k], out[-1]


def _halves_wait(sems, placed, after, name):
    k = N_CHIPS - 1

    def body(buf, *refs):
        send_sems, recv_sems = refs[:k], refs[k:2 * k]
        for cp in _half_copies(buf, send_sems, recv_sems):
            cp.wait_send()
            cp.wait_recv()

    return pl.pallas_call(
        body, name=name, out_shape=pltpu.HBM(placed.shape, placed.dtype),
        in_specs=[HBM] + [SEM] * (2 * k) + [ANY] * len(after), out_specs=HBM, input_output_aliases={0: 0},
        compiler_params=pltpu.CompilerParams(has_side_effects=DATAFLOW),
    )(placed, *sems, *after)


def _forward_halves(buf, name):
    def body(_, dst, send_sems, recv_sems):
        x, y, c, sibling, others = _place()
        half = dst.shape[1] // 2
        for j, (ox, oy) in enumerate(others):
            slab = dst.at[2 * ox + oy, pl.ds(c * half, half)]
            _remote(slab, slab, send_sems.at[j], recv_sems.at[j], sibling).start()
        for j, (ox, oy) in enumerate(others):
            mine = dst.at[2 * ox + oy, pl.ds(c * half, half)]
            theirs = dst.at[2 * ox + oy, pl.ds((1 - c) * half, half)]
            cp = _remote(mine, theirs, send_sems.at[j], recv_sems.at[j], sibling)
            cp.wait_recv()
            cp.wait_send()

    return pl.pallas_call(
        body, name=name, out_shape=jax.ShapeDtypeStruct(buf.shape, buf.dtype),
        in_specs=[ANY], out_specs=ANY, input_output_aliases={0: 0},
        scratch_shapes=[pltpu.SemaphoreType.DMA((N_CHIPS - 1,)), pltpu.SemaphoreType.DMA((N_CHIPS - 1,))],
    )(buf)


def _swap_halves(grads):
    n = len(grads)

    def body(*refs):
        g, theirs = refs[:n], refs[n:2 * n]
        send_sems, recv_sems = refs[2 * n:]
        x, y, c, sibling, _ = _place()
        for i in range(n):
            half = g[i].shape[1] // 2
            give = (1 - c) * half
            for b in range(N_CHIPS):
                _start_streams(lambda r, i=i, b=b: _remote(
                    g[i].at[b, pl.ds(give + r.start, r.size)], theirs[i].at[b, r], send_sems.at[i], recv_sems.at[i],
                    sibling), half)
        for i in range(n):
            _remote(theirs[i], theirs[i], send_sems.at[i], recv_sems.at[i], sibling).wait()

    return pl.pallas_call(
        body, name="swap_halves",
        out_shape=[jax.ShapeDtypeStruct((s.shape[0], s.shape[1] // 2) + s.shape[2:], s.dtype) for s in grads],
        in_specs=[ANY] * n, out_specs=[ANY] * n,
        scratch_shapes=[pltpu.SemaphoreType.DMA((n,)), pltpu.SemaphoreType.DMA((n,))],
    )(*grads)


HBM = pl.BlockSpec(memory_space=pltpu.HBM)
SEM = pl.BlockSpec(memory_space=pltpu.SEMAPHORE)
DATAFLOW = pltpu.SideEffectType.DATAFLOW_SIDE_EFFECTING


def _broadcast_copies(buf, send_sems, recv_sems):
    x, y, c, _, others = _place()
    mine = buf.at[2 * x + y]
    return [_remote(mine, mine, send_sems[j], recv_sems[j], (ox, oy, c)) for j, (ox, oy) in enumerate(others)]


def _broadcast_start(placed, after, name):
    k = N_CHIPS - 1

    def body(buf, after_ref, *refs):
        send_sems, recv_sems, token = refs[:k], refs[k:2 * k], refs[-1]
        for cp in _broadcast_copies(buf, send_sems, recv_sems):
            cp.start()
        token[...] = jnp.zeros_like(token)

    out = pl.pallas_call(
        body, name=name,
        out_shape=[pltpu.SemaphoreType.DMA(())] * (2 * k) + [pltpu.HBM(placed.shape, placed.dtype),
                                                             jax.ShapeDtypeStruct((8, 128), F32)],
        in_specs=[HBM, ANY], out_specs=[SEM] * (2 * k) + [HBM, pl.BlockSpec(memory_space=pltpu.VMEM)],
        input_output_aliases={0: 2 * k},
        compiler_params=pltpu.CompilerParams(has_side_effects=DATAFLOW),
    )(pltpu.with_memory_space_constraint(placed, pltpu.HBM), after)
    return out[:2 * k], out[2 * k], out[-1]


def _broadcast_wait(sems, placed, after, name):
    k = N_CHIPS - 1

    def body(buf, *refs):
        send_sems, recv_sems = refs[:k], refs[k:2 * k]
        for cp in _broadcast_copies(buf, send_sems, recv_sems):
            cp.wait_send()
            cp.wait_recv()

    return pl.pallas_call(
        body, name=name, out_shape=pltpu.HBM(placed.shape, placed.dtype),
        in_specs=[HBM] + [SEM] * (2 * k) + [ANY], out_specs=HBM, input_output_aliases={0: 0},
        compiler_params=pltpu.CompilerParams(has_side_effects=DATAFLOW),
    )(placed, *sems, after)


def _scatter_copies(s, got, send_sems, recv_sems):
    x, y, c, _, others = _place()
    me = 2 * x + y
    n = len(s)
    return [_remote(s[i].at[2 * ox + oy], got[i].at[me], send_sems[3 * i + j], recv_sems[3 * i + j], (ox, oy, c))
            for i in range(n) for j, (ox, oy) in enumerate(others)]


def _scatter_start(sums, placed):
    n = len(sums)
    k = 3 * n

    def body(*refs):
        s, got, token = refs[:n], refs[n:2 * n], refs[-1]
        send_sems, recv_sems = refs[2 * n:2 * n + k], refs[2 * n + k:2 * n + 2 * k]
        for cp in _scatter_copies(s, got, send_sems, recv_sems):
            cp.start()
        token[...] = jnp.zeros_like(token)

    hbm = [pltpu.HBM(a.shape, a.dtype) for a in list(sums) + list(placed)]
    out = pl.pallas_call(
        body, name="scatter_start",
        out_shape=[pltpu.SemaphoreType.DMA(())] * (2 * k) + hbm + [jax.ShapeDtypeStruct((8, 128), F32)],
        in_specs=[HBM] * (2 * n), out_specs=[SEM] * (2 * k) + [HBM] * (2 * n) + [pl.BlockSpec(memory_space=pltpu.VMEM)],
        input_output_aliases={i: 2 * k + i for i in range(2 * n)},
        compiler_params=pltpu.CompilerParams(has_side_effects=DATAFLOW),
    )(*[pltpu.with_memory_space_constraint(a, pltpu.HBM) for a in list(sums) + list(placed)])
    return out[:2 * k], out[2 * k:2 * k + n], out[2 * k + n:2 * k + 2 * n], out[-1]


def _scatter_wait(sems, sums, placed, after):
    n = len(sums)
    k = 3 * n

    def body(*refs):
        s, got = refs[:n], refs[n:2 * n]
        send_sems, recv_sems = refs[2 * n:2 * n + k], refs[2 * n + k:2 * n + 2 * k]
        for cp in _scatter_copies(s, got, send_sems, recv_sems):
            cp.wait_send()
            cp.wait_recv()

    hbm = [pltpu.HBM(a.shape, a.dtype) for a in list(sums) + list(placed)]
    out = pl.pallas_call(
        body, name="scatter_wait", out_shape=hbm,
        in_specs=[HBM] * (2 * n) + [SEM] * (2 * k) + [ANY], out_specs=[HBM] * (2 * n),
        input_output_aliases={i: i for i in range(2 * n)},
        compiler_params=pltpu.CompilerParams(has_side_effects=DATAFLOW),
    )(*sums, *placed, *sems, after)
    return out[n:]


def _join_copies(bufs, send_sems, recv_sems):
    x, y, c, sibling, _ = _place()
    return [_remote(b.at[c], b.at[c], send_sems[i], recv_sems[i], sibling) for i, b in enumerate(bufs)]


def _join_start(placed):
    n = len(placed)

    def body(*refs):
        bufs, send_sems, recv_sems = refs[:n], refs[n:2 * n], refs[2 * n:3 * n]
        for cp in _join_copies(bufs, send_sems, recv_sems):
            cp.start()

    hbm = [pltpu.HBM(a.shape, a.dtype) for a in placed]
    out = pl.pallas_call(
        body, name="join_start", out_shape=[pltpu.SemaphoreType.DMA(())] * (2 * n) + hbm,
        in_specs=[HBM] * n, out_specs=[SEM] * (2 * n) + [HBM] * n,
        input_output_aliases={i: 2 * n + i for i in range(n)},
        compiler_params=pltpu.CompilerParams(has_side_effects=DATAFLOW),
    )(*[pltpu.with_memory_space_constraint(a, pltpu.HBM) for a in placed])
    return out[:2 * n], out[2 * n:]


def _join_wait(sems, placed, after):
    n = len(placed)

    def body(*refs):
        bufs, send_sems, recv_sems = refs[:n], refs[n:2 * n], refs[2 * n:3 * n]
        for cp in _join_copies(bufs, send_sems, recv_sems):
            cp.wait_send()
            cp.wait_recv()

    return pl.pallas_call(
        body, name="join_wait", out_shape=[pltpu.HBM(a.shape, a.dtype) for a in placed],
        in_specs=[HBM] * n + [SEM] * (2 * n) + [ANY] * len(after), out_specs=[HBM] * n,
        input_output_aliases={i: i for i in range(n)},
        compiler_params=pltpu.CompilerParams(has_side_effects=DATAFLOW),
    )(*placed, *sems, *after)


def _gather_rows(row):
    def body(row_ref, out_ref, send_sems, recv_sems, local_sem):
        x, y, c = lax.axis_index("x"), lax.axis_index("y"), lax.axis_index("c")
        me = 4 * x + 2 * y + c
        local = pltpu.make_async_copy(row_ref, out_ref.at[me], local_sem)
        local.start()
        sent = []
        peers = []
        for k in range(1, N_DEV):
            px, py, pc = x ^ (k >> 2), y ^ ((k >> 1) & 1), c ^ (k & 1)
            peers.append((k, px, py, pc))
            cp = _remote(row_ref, out_ref.at[me], send_sems.at[k - 1], recv_sems.at[k - 1], (px, py, pc))
            cp.start()
            sent.append(cp)
        for k, px, py, pc in peers:
            slot = out_ref.at[4 * px + 2 * py + pc]
            _remote(slot, slot, send_sems.at[k - 1], recv_sems.at[k - 1], (px, py, pc)).wait_recv()
        for cp in sent:
            cp.wait_send()
        local.wait()

    return pl.pallas_call(
        body, name="gather_rows", out_shape=jax.ShapeDtypeStruct((N_DEV,) + row.shape, row.dtype),
        in_specs=[ANY], out_specs=ANY,
        scratch_shapes=[pltpu.SemaphoreType.DMA((N_DEV - 1,)), pltpu.SemaphoreType.DMA((N_DEV - 1,)),
                        pltpu.SemaphoreType.DMA],
    )(row)


def _scalar(i):
    return jnp.reshape(i, (1,)).astype(jnp.int32)


def _place_block(src, n_slots, slot, out_dtype, name, window=0, n_windows=1):
    rows, cols = src.shape[0], src.shape[1] // n_windows
    tr = _divisor_tile(rows, max(16, (2 * MIB) // (cols * 4)), 16)

    def body(slot_ref, s_ref, o_ref):
        o_ref[...] = s_ref[...].astype(o_ref.dtype)

    return pl.pallas_call(
        body, name=name, out_shape=jax.ShapeDtypeStruct((n_slots, rows, cols), out_dtype),
        grid_spec=pltpu.PrefetchScalarGridSpec(
            num_scalar_prefetch=1, grid=(rows // tr,), in_specs=[pl.BlockSpec((tr, cols), lambda r, sl: (r, window))],
            out_specs=pl.BlockSpec((None, tr, cols), lambda r, sl: (sl[0], r, 0))),
        compiler_params=_params(("parallel",)))(_scalar(slot), src)


def _add_halves(g, theirs, core, chip, name):
    n, half, cols = theirs.shape
    tr = _divisor_tile(half, max(16, (2 * MIB) // (cols * 4)), 16)
    per = half // tr

    def body(at_ref, a_ref, b_ref, o_ref, own_ref):
        total = (a_ref[...].astype(F32) + b_ref[...].astype(F32)).astype(o_ref.dtype)
        o_ref[...] = total

        @pl.when(pl.program_id(1) == at_ref[1])
        def _():
            own_ref[...] = total

    spec = pl.BlockSpec((None, tr, cols), lambda r, i, at: (i, r, 0))
    shape = jax.ShapeDtypeStruct(theirs.shape, BF16)
    return pl.pallas_call(
        body, name=name, out_shape=[shape, shape],
        grid_spec=pltpu.PrefetchScalarGridSpec(
            num_scalar_prefetch=1, grid=(per, n),
            in_specs=[pl.BlockSpec((None, tr, cols), lambda r, i, at: (i, at[0] * per + r, 0)), spec],
            out_specs=[spec, pl.BlockSpec((None, tr, cols), lambda r, i, at: (at[1], r, 0))]),
        compiler_params=_params(("parallel", "arbitrary")))(jnp.concatenate([_scalar(core), _scalar(chip)]), g, theirs)


def _sum_slots(a, core, name):
    n, rows, cols = a.shape
    tr = _divisor_tile(rows, max(16, (2 * MIB) // (cols * 4 * n)), 16)

    def body(c_ref, a_ref, o_ref):
        acc = a_ref[0].astype(F32)
        for i in range(1, n):
            acc = acc + a_ref[i].astype(F32)
        o_ref[...] = acc

    return pl.pallas_call(
        body, name=name, out_shape=jax.ShapeDtypeStruct((2, rows, cols), F32),
        grid_spec=pltpu.PrefetchScalarGridSpec(
            num_scalar_prefetch=1, grid=(rows // tr,),
            in_specs=[pl.BlockSpec((n, tr, cols), lambda r, c: (0, r, 0))],
            out_specs=pl.BlockSpec((None, tr, cols), lambda r, c: (c[0], r, 0))),
        compiler_params=_params(("parallel",)))(_scalar(core), a)


def _adamw_math(w, g, m, v):
    m = ADAM_B1 * m + (1.0 - ADAM_B1) * g
    v = ADAM_B2 * v + (1.0 - ADAM_B2) * (g * g)
    m_hat = m / (1.0 - ADAM_B1 ** ADAM_STEP)
    v_hat = v / (1.0 - ADAM_B2 ** ADAM_STEP)
    delta = -ADAM_LR * (m_hat / (jnp.sqrt(v_hat) + ADAM_EPS) + ADAM_WD * w)
    return delta, m, v


def _adamw_half(w, g2, m, v, which, prev, name):
    rows, cols = w.shape
    half = rows // 2
    tr = _divisor_tile(half, max(8, MIB // (cols * 4)), 8)
    per = half // tr

    def body(h_ref, w_ref, g_ref, m_ref, v_ref, *refs):
        d_ref, nm_ref, nv_ref = refs[-3:]
        d, nm, nv = _adamw_math(w_ref[...], g_ref[...], m_ref[...], v_ref[...])
        d_ref[...] = d
        nm_ref[...] = nm
        nv_ref[...] = nv

    spec = pl.BlockSpec((tr, cols), lambda r, h: (h[0] * per + r, 0))
    in_specs = [spec, pl.BlockSpec((None, tr, cols), lambda r, h: (h[0], r, 0)), spec, spec]
    args = [_scalar(which), w, g2, m, v]
    aliases = {}
    if prev is not None:
        aliases = {len(args) + i: i for i in range(3)}
        in_specs += [ANY] * 3
        args += list(prev)
    return pl.pallas_call(
        body, name=name, out_shape=[jax.ShapeDtypeStruct((rows, cols), F32)] * 3,
        grid_spec=pltpu.PrefetchScalarGridSpec(num_scalar_prefetch=1, grid=(per,), in_specs=in_specs,
                                               out_specs=[spec] * 3),
        input_output_aliases=aliases, compiler_params=_params(("parallel",)))(*args)


def _sum_rows_adamw(parts, w, m, v):
    def body(p_ref, w_ref, m_ref, v_ref, g_ref, d_ref, nm_ref, nv_ref):
        g = p_ref[0]
        for i in range(1, N_DEV):
            g = g + p_ref[i]
        d, nm, nv = _adamw_math(w_ref[...], g, m_ref[...], v_ref[...])
        g_ref[...] = g
        d_ref[...] = d
        nm_ref[...] = nm
        nv_ref[...] = nv

    shape = jax.ShapeDtypeStruct(w.shape, F32)
    return pl.pallas_call(body, name="sum_rows_adamw", out_shape=[shape] * 4)(parts, w, m, v)


LANES = 128


def _permute_scratch(rows, width):
    return pltpu.VMEM((width // LANES, rows, LANES), F32)


def _split_rows(value, scratch, dil):
    if dil == 1:
        return [value]
    rows = value.shape[0] // dil
    slabs = value.shape[1] // LANES
    for c in range(slabs):
        scratch[c] = value[:, c * LANES:(c + 1) * LANES]
    return [jnp.concatenate([scratch[c, pl.ds(r, rows, stride=dil), :] for c in range(slabs)], axis=1)
            for r in range(dil)]


def _merge_rows(ref, scratch, dil):
    if dil == 1:
        return ref[0].astype(F32)
    rows = ref.shape[1]
    slabs = ref.shape[2] // LANES
    for r in range(dil):
        part = ref[r].astype(F32)
        for c in range(slabs):
            scratch[c, pl.ds(r, rows, stride=dil), :] = part[:, c * LANES:(c + 1) * LANES]
    return jnp.concatenate([scratch[c] for c in range(slabs)], axis=1)


def _grouped_view(t, dil):
    return t.reshape(dil, t.shape[0] // dil, t.shape[1])


def _grouped_spec(dil, rows, width, index):
    return pl.BlockSpec((dil, rows // dil, width), index)


W_CHUNKS = 2


def _pick(values, j):
    out = values[-1]
    for i in range(len(values) - 2, -1, -1):
        out = jnp.where(j == i, values[i], out)
    return out


def _chunk_of(col, per_chip):
    return (col % per_chip) // (per_chip // W_CHUNKS)


def _w_block(col, per_chip):
    return col // per_chip, 0, (col % per_chip) % (per_chip // W_CHUNKS)


def _in_proj(xb, wc, blocks, j0, ncols, tn, out_dtype, prev, after, name):
    s, d = xb.shape
    per_chip = wc.shape[2] * W_CHUNKS // tn
    tm = _divisor_tile(s, 1024, 16)
    extra = [t for t in (after,) if t is not None]

    def body(*refs):
        a_ref, b_ref = refs[len(extra):len(extra) + 2]
        o_ref = refs[-1]
        o_ref[...] = jnp.dot(a_ref[...], b_ref[...], preferred_element_type=F32).astype(o_ref.dtype)

    in_specs = [pl.BlockSpec(t.shape, lambda j, m: (0, 0)) for t in extra] + [
        pl.BlockSpec((tm, d), lambda j, m: (m, 0)),
        pl.BlockSpec((None, d, tn), lambda j, m: _w_block(_pick(blocks, j), per_chip))]
    args = extra + [xb, wc]
    aliases = {}
    if prev is not None:
        aliases = {len(args): 0}
        in_specs.append(ANY)
        args.append(prev)
    return pl.pallas_call(
        body, name=name, grid=(len(blocks), s // tm), in_specs=in_specs,
        out_specs=pl.BlockSpec((tm, tn), lambda j, m: (m, _pick(blocks, j) - j0)),
        out_shape=jax.ShapeDtypeStruct((s, ncols), out_dtype), input_output_aliases=aliases,
        compiler_params=_params(("parallel", "parallel")))(*args)


def _in_proj_qkv(xb, wc, g, blocks, aw, tn, prev, after, name):
    s, d = xb.shape
    dil = DILATIONS[g]
    per_chip = wc.shape[2] * W_CHUNKS // tn
    sub = aw // tn
    tm = _divisor_tile(s, 1024, 16 * dil)
    extra = [t for t in (after,) if t is not None]

    def body(*refs):
        a_ref, b_ref = refs[len(extra):len(extra) + 2]
        o_ref, scratch = refs[-2:]
        res = jnp.dot(a_ref[...], b_ref[...], preferred_element_type=F32)
        for r, part in enumerate(_split_rows(res, scratch, dil)):
            o_ref[r] = part.astype(BF16)

    def out_index(j, m):
        col = _pick(blocks, j)
        return (col // sub) // N_GROUPS, 0, m, col % sub

    in_specs = [pl.BlockSpec(t.shape, lambda j, m: (0, 0)) for t in extra] + [
        pl.BlockSpec((tm, d), lambda j, m: (m, 0)),
        pl.BlockSpec((None, d, tn), lambda j, m: _w_block(_pick(blocks, j), per_chip))]
    args = extra + [xb, wc]
    aliases = {}
    if prev is not None:
        aliases = {len(args): 0}
        in_specs.append(ANY)
        args.append(prev)
    return pl.pallas_call(
        body, name=name, grid=(len(blocks), s // tm), in_specs=in_specs,
        out_specs=pl.BlockSpec((None, dil, tm // dil, tn), out_index),
        out_shape=jax.ShapeDtypeStruct((3, dil, s // dil, aw), BF16), input_output_aliases=aliases,
        scratch_shapes=[_permute_scratch(tm, tn)],
        compiler_params=_params(("parallel", "parallel")))(*args)


def _window_mask(first):
    qi = lax.broadcasted_iota(jnp.int32, (STEPS, 2 * STEPS), 0)
    kj = lax.broadcasted_iota(jnp.int32, (STEPS, 2 * STEPS), 1)
    lowest = jnp.where(first, STEPS, 0)
    return (kj >= qi) & (kj <= qi + STEPS) & (kj >= lowest)


def _attn_fwd(qkv, g):
    _, s, aw = qkv.shape
    heads = aw // HEAD_DIM
    n_blocks = s // STEPS
    per_seq = n_blocks // DILATIONS[g]

    def body(q_ref, kc_ref, kp_ref, vc_ref, vp_ref, o_ref, l_ref):
        mask = _window_mask(lax.rem(pl.program_id(0), per_seq) == 0)
        for h in range(heads):
            hs = slice(h * HEAD_DIM, (h + 1) * HEAD_DIM)
            kk = jnp.concatenate([kp_ref[:, hs], kc_ref[:, hs]], axis=0)
            vv = jnp.concatenate([vp_ref[:, hs], vc_ref[:, hs]], axis=0)
            sc = lax.dot_general(q_ref[:, hs], kk, NT, preferred_element_type=F32) * SCORE_SCALE
            sc = jnp.where(mask, sc, NEG_INF)
            mx = jnp.max(sc, axis=1, keepdims=True)
            e = jnp.exp(sc - mx)
            den = jnp.sum(e, axis=1, keepdims=True)
            o_ref[:, hs] = (jnp.dot(e.astype(BF16), vv, preferred_element_type=F32) / den).astype(BF16)
            l_ref[:, hs] = jnp.broadcast_to(mx + jnp.log(den), (STEPS, HEAD_DIM))

    def cur(which):
        return pl.BlockSpec((None, STEPS, aw), lambda b: (which, b, 0))

    def prev(which):
        return pl.BlockSpec((None, STEPS, aw), lambda b: (which, jnp.maximum(b - 1, 0), 0))

    out = pl.BlockSpec((STEPS, aw), lambda b: (b, 0))
    return pl.pallas_call(
        body, name=f"attn_fwd{g}", grid=(n_blocks,),
        in_specs=[cur(0), cur(1), prev(1), cur(2), prev(2)], out_specs=[out, out],
        out_shape=[jax.ShapeDtypeStruct((s, aw), BF16), jax.ShapeDtypeStruct((s, aw), F32)],
        compiler_params=_params(("parallel",)))(qkv, qkv, qkv, qkv, qkv)


def _combine_groups(os, ls, zuz, aw):
    s = zuz.shape[0]
    tr = _divisor_tile(s, 256, 8 * DILATIONS[-1])

    def body(*refs):
        o_refs, l_refs, z_ref = refs[0:3], refs[3:6], refs[6]
        oo_ref, y_ref, yt_ref = refs[7:10]
        lq_refs, scratch = refs[10:13], refs[13]
        ls_ = [_merge_rows(l_refs[g], scratch, dil) for g, dil in enumerate(DILATIONS)]
        mx = jnp.maximum(jnp.maximum(ls_[0], ls_[1]), ls_[2])
        ws = [jnp.exp(l - mx) for l in ls_]
        den = ws[0] + ws[1] + ws[2]
        o = ws[0] * _merge_rows(o_refs[0], scratch, DILATIONS[0])
        for g in range(1, N_GROUPS):
            o = o + ws[g] * _merge_rows(o_refs[g], scratch, DILATIONS[g])
        o = o / den
        z = z_ref[...].astype(F32)
        y = o * (z * _sigmoid(z))
        oo_ref[...] = o.astype(BF16)
        y_ref[...] = y.astype(BF16)
        yt_ref[...] = y.T.astype(BF16)
        for g, dil in enumerate(DILATIONS):
            for r, part in enumerate(_split_rows(mx + jnp.log(den), scratch, dil)):
                lq_refs[g][r] = part

    grouped = [_grouped_spec(dil, tr, aw, lambda r: (0, r, 0)) for dil in DILATIONS]
    one = pl.BlockSpec((tr, aw), lambda r: (r, 0))
    b16 = jax.ShapeDtypeStruct((s, aw), BF16)
    out = pl.pallas_call(
        body, name="combine_groups", grid=(s // tr,),
        in_specs=grouped + grouped + [one],
        out_specs=[one, one, pl.BlockSpec((aw, tr), lambda r: (0, r))] + grouped,
        out_shape=[b16, b16, jax.ShapeDtypeStruct((aw, s), BF16)]
        + [jax.ShapeDtypeStruct((dil, s // dil, aw), F32) for dil in DILATIONS],
        scratch_shapes=[_permute_scratch(tr, aw)],
        compiler_params=_params(("parallel",)))(
            *[_grouped_view(t, dil) for t, dil in zip(os, DILATIONS)],
            *[_grouped_view(t, dil) for t, dil in zip(ls, DILATIONS)], zuz)
    return out[0], out[1], out[2], [t.reshape(s, aw) for t in out[3:]]


def _pool_counts(row0, rows, window):
    t = row0 + lax.broadcasted_iota(jnp.int32, (rows, 1), 0)
    return jnp.minimum(t + 1, window).astype(F32)


def _pool_fwd(zuz, w_pool, pool_scale, aw, pw):
    s = zuz.shape[0]
    pg = pw // len(POOL_WINDOWS)
    tr = _divisor_tile(s, 256, 128)
    u_col, z_col = aw // pw, aw // pw + 1
    assert aw % pw == 0

    def body(u_ref, up_ref, z_ref, w_ref, sc_ref, p_ref, l_ref, y_ref, yt_ref):
        r = pl.program_id(0)
        u = u_ref[...].astype(F32)
        halo = jnp.where(r > 0, up_ref[...].astype(F32), 0.0)
        ext = jnp.concatenate([halo, u], axis=0)
        pieces, lins = [], []
        for gi, window in enumerate(POOL_WINDOWS):
            cs = slice(gi * pg, (gi + 1) * pg)
            acc = ext[:, cs]
            shift = 1
            while shift < window:
                acc = acc + pltpu.roll(acc, shift, 0)
                shift *= 2
            p = acc[POOL_HALO:] / _pool_counts(r * tr, tr, window) - u[:, cs]
            pieces.append(p)
            lins.append(jnp.dot(p.astype(BF16), w_ref[gi], preferred_element_type=F32))
        p = jnp.concatenate(pieces, axis=1)
        lin = jnp.concatenate(lins, axis=1)
        z = z_ref[...].astype(F32)
        y = lin * sc_ref[...] * (z * _sigmoid(z))
        p_ref[...] = p.astype(BF16)
        l_ref[...] = lin
        y_ref[...] = y.astype(BF16)
        yt_ref[...] = y.T.astype(BF16)

    per = tr // POOL_HALO
    out = pl.BlockSpec((tr, pw), lambda r: (r, 0))
    return pl.pallas_call(
        body, name="pool_fwd", grid=(s // tr,),
        in_specs=[pl.BlockSpec((tr, pw), lambda r: (r, u_col)),
                  pl.BlockSpec((POOL_HALO, pw), lambda r: (jnp.maximum(r * per - 1, 0), u_col)),
                  pl.BlockSpec((tr, pw), lambda r: (r, z_col)),
                  pl.BlockSpec((len(POOL_WINDOWS), pg, pg), lambda r: (0, 0, 0)),
                  pl.BlockSpec((1, pw), lambda r: (0, 0))],
        out_specs=[out, out, out, pl.BlockSpec((pw, tr), lambda r: (0, r))],
        out_shape=[jax.ShapeDtypeStruct((s, pw), BF16), jax.ShapeDtypeStruct((s, pw), F32),
                   jax.ShapeDtypeStruct((s, pw), BF16), jax.ShapeDtypeStruct((pw, s), BF16)],
        compiler_params=_params(("parallel",)))(zuz, zuz, zuz, w_pool, pool_scale)


def _proj_merge(y_attn, y_pool, wpa4, wpp4, gpre, b_gate):
    s, aw = y_attn.shape
    pw = y_pool.shape[1]
    tn = wpa4.shape[2]
    d = N_CHIPS * tn
    tm = _divisor_tile(s, 512, 128)

    def body(ya_ref, yp_ref, wa_ref, wp_ref, ga_ref, gp_ref, ba_ref, bp_ref, a_ref, p_ref, sa_ref, sp_ref, m_ref,
             mt_ref):
        a = jnp.dot(ya_ref[...], wa_ref[...], preferred_element_type=F32)
        p = jnp.dot(yp_ref[...], wp_ref[...], preferred_element_type=F32)
        sa = _sigmoid(ga_ref[...].astype(F32) + ba_ref[...])
        sp = _sigmoid(gp_ref[...].astype(F32) + bp_ref[...])
        merged = sa * a + sp * p
        a_ref[...] = a.astype(BF16)
        p_ref[...] = p.astype(BF16)
        sa_ref[...] = sa.astype(BF16)
        sp_ref[...] = sp.astype(BF16)
        m_ref[...] = merged.astype(BF16)
        mt_ref[...] = merged.T.astype(BF16)

    out = pl.BlockSpec((tm, tn), lambda n, m: (m, n))
    f = jax.ShapeDtypeStruct((s, d), BF16)
    return pl.pallas_call(
        body, name="proj_merge", grid=(N_CHIPS, s // tm),
        in_specs=[pl.BlockSpec((tm, aw), lambda n, m: (m, 0)), pl.BlockSpec((tm, pw), lambda n, m: (m, 0)),
                  pl.BlockSpec((None, aw, tn), lambda n, m: (n, 0, 0)),
                  pl.BlockSpec((None, pw, tn), lambda n, m: (n, 0, 0)),
                  pl.BlockSpec((tm, tn), lambda n, m: (m, n)), pl.BlockSpec((tm, tn), lambda n, m: (m, N_CHIPS + n)),
                  pl.BlockSpec((1, tn), lambda n, m: (0, n)), pl.BlockSpec((1, tn), lambda n, m: (0, N_CHIPS + n))],
        out_specs=[out] * 5 + [pl.BlockSpec((tn, tm), lambda n, m: (n, m))],
        out_shape=[f] * 5 + [jax.ShapeDtypeStruct((d, s), BF16)],
        compiler_params=_params(("parallel", "parallel")))(y_attn, y_pool, wpa4, wpp4, gpre, gpre, b_gate, b_gate)


def _out_norm_loss(merged, w_out, x, target, gamma, beta):
    s, d = x.shape
    tm = _divisor_tile(s, 256, 16)

    def body(m_ref, w_ref, x_ref, t_ref, g_ref, b_ref, dr_ref, drb_ref, loss_ref, dg_ref, db_ref):
        @pl.when(pl.program_id(0) == 0)
        def _():
            loss_ref[...] = jnp.zeros_like(loss_ref)
            dg_ref[...] = jnp.zeros_like(dg_ref)
            db_ref[...] = jnp.zeros_like(db_ref)

        r = ALPHA * x_ref[...] + jnp.dot(m_ref[...], w_ref[...], preferred_element_type=F32)
        mu = jnp.mean(r, axis=1, keepdims=True)
        rc = r - mu
        rstd = lax.rsqrt(jnp.mean(rc * rc, axis=1, keepdims=True) + LN_EPS)
        xhat = rc * rstd
        diff = xhat * g_ref[...] + b_ref[...] - t_ref[...]
        dy = diff / d
        loss_ref[...] += jnp.sum(diff * diff, axis=0, keepdims=True)
        dg_ref[...] += jnp.sum(dy * xhat, axis=0, keepdims=True)
        db_ref[...] += jnp.sum(dy, axis=0, keepdims=True)
        dxhat = dy * g_ref[...]
        dr = rstd * (dxhat - jnp.mean(dxhat, axis=1, keepdims=True)
                     - xhat * jnp.mean(dxhat * xhat, axis=1, keepdims=True))
        dr_ref[...] = dr
        drb_ref[...] = dr.astype(BF16)

    row = pl.BlockSpec((tm, d), lambda m: (m, 0))
    vec = pl.BlockSpec((1, d), lambda m: (0, 0))
    v = jax.ShapeDtypeStruct((1, d), F32)
    return pl.pallas_call(
        body, name="out_norm_loss", grid=(s // tm,),
        in_specs=[row, pl.BlockSpec((d, d), lambda m: (0, 0)), row, row, vec, vec],
        out_specs=[row, row, vec, vec, vec],
        out_shape=[jax.ShapeDtypeStruct((s, d), F32), jax.ShapeDtypeStruct((s, d), BF16), v, v, v],
        compiler_params=_params(("arbitrary",), vmem_mib=56))(merged, w_out, x, target, gamma, beta)


def _merge_bwd(drb, w_out, a, p, sa, sp):
    s, d = drb.shape
    tm = _divisor_tile(s, 512, 16)
    tn = d // N_CHIPS

    def body(dr_ref, w_ref, a_ref, p_ref, sa_ref, sp_ref, da_ref, dp_ref, dga_ref, dgp_ref, dba_ref, dbp_ref):
        @pl.when(pl.program_id(1) == 0)
        def _():
            dba_ref[...] = jnp.zeros_like(dba_ref)
            dbp_ref[...] = jnp.zeros_like(dbp_ref)

        dm = lax.dot_general(dr_ref[...], w_ref[...], NT, preferred_element_type=F32)
        sa = sa_ref[...].astype(F32)
        sp = sp_ref[...].astype(F32)
        da_ref[...] = (dm * sa).astype(BF16)
        dp_ref[...] = (dm * sp).astype(BF16)
        dga = dm * a_ref[...].astype(F32) * sa * (1.0 - sa)
        dgp = dm * p_ref[...].astype(F32) * sp * (1.0 - sp)
        dga_ref[...] = dga.astype(BF16)
        dgp_ref[...] = dgp.astype(BF16)
        dba_ref[...] += jnp.sum(dga, axis=0, keepdims=True)
        dbp_ref[...] += jnp.sum(dgp, axis=0, keepdims=True)

    blk = pl.BlockSpec((tm, tn), lambda n, m: (m, n))
    vec = pl.BlockSpec((1, tn), lambda n, m: (0, n))
    b16 = jax.ShapeDtypeStruct((s, d), BF16)
    v = jax.ShapeDtypeStruct((1, d), F32)
    return pl.pallas_call(
        body, name="merge_bwd", grid=(N_CHIPS, s // tm),
        in_specs=[pl.BlockSpec((tm, d), lambda n, m: (m, 0)), pl.BlockSpec((tn, d), lambda n, m: (n, 0)),
                  blk, blk, blk, blk],
        out_specs=[blk, blk, blk, blk, vec, vec], out_shape=[b16, b16, b16, b16, v, v],
        compiler_params=_params(("parallel", "arbitrary")))(drb, w_out, a, p, sa, sp)


def _proj_t(dy_ref, w_ref, tn):
    acc = None
    for n in range(N_CHIPS):
        t = lax.dot_general(dy_ref[:, n * tn:(n + 1) * tn], w_ref[n], NT, preferred_element_type=F32)
        acc = t if acc is None else acc + t
    return acc


def _attn_gate_bwd(da, wpa4, zuz, o):
    s, d = da.shape
    aw, tn = wpa4.shape[1], wpa4.shape[2]
    heads = aw // HEAD_DIM
    tm = _divisor_tile(s, 256, 16 * DILATIONS[-1])

    def body(*refs):
        da_ref, w_ref, z_ref, o_ref, dz_ref = refs[:5]
        do_refs, dd_refs, scratch = refs[5:8], refs[8:11], refs[11]
        dy = _proj_t(da_ref, w_ref, tn)
        z, o = z_ref[...].astype(F32), o_ref[...].astype(F32)
        sg = _sigmoid(z)
        do = dy * (z * sg)
        dz_ref[...] = (dy * o * _dsilu(z, sg)).astype(BF16)
        prod = do * o
        dd = jnp.concatenate(
            [jnp.broadcast_to(jnp.sum(prod[:, h * HEAD_DIM:(h + 1) * HEAD_DIM], axis=1, keepdims=True),
                              (tm, HEAD_DIM)) for h in range(heads)], axis=1)
        for g, dil in enumerate(DILATIONS):
            for r, part in enumerate(_split_rows(do, scratch, dil)):
                do_refs[g][r] = part.astype(BF16)
            for r, part in enumerate(_split_rows(dd, scratch, dil)):
                dd_refs[g][r] = part

    row = pl.BlockSpec((tm, aw), lambda m: (m, 0))
    grouped = [_grouped_spec(dil, tm, aw, lambda m: (0, m, 0)) for dil in DILATIONS]
    out = pl.pallas_call(
        body, name="attn_gate_bwd", grid=(s // tm,),
        in_specs=[pl.BlockSpec((tm, d), lambda m: (m, 0)), pl.BlockSpec((N_CHIPS, aw, tn), lambda m: (0, 0, 0)),
                  row, row],
        out_specs=[row] + grouped + grouped,
        out_shape=[jax.ShapeDtypeStruct((s, aw), BF16)]
        + [jax.ShapeDtypeStruct((dil, s // dil, aw), BF16) for dil in DILATIONS]
        + [jax.ShapeDtypeStruct((dil, s // dil, aw), F32) for dil in DILATIONS],
        scratch_shapes=[_permute_scratch(tm, aw)],
        compiler_params=_params(("parallel",)))(da, wpa4, zuz, o)
    return out[0], [t.reshape(s, aw) for t in out[1:4]], [t.reshape(s, aw) for t in out[4:7]]


def _pool_gate_bwd(dp_in, wpp4, zuz, lin, pooled, w_pool, pool_scale, aw):
    s, d = dp_in.shape
    pw, tn = wpp4.shape[1], wpp4.shape[2]
    n_win = len(POOL_WINDOWS)
    pg = pw // n_win
    tm = _divisor_tile(s, 256, 16)
    z_col = aw // pw + 1

    def body(dp_ref, w_ref, z_ref, l_ref, p_ref, wp_ref, sc_ref, dz_ref, dpo_ref, dw_ref, ds_ref):
        @pl.when(pl.program_id(0) == 0)
        def _():
            dw_ref[...] = jnp.zeros_like(dw_ref)
            ds_ref[...] = jnp.zeros_like(ds_ref)

        dy = _proj_t(dp_ref, w_ref, tn)
        z, lin_ = z_ref[...].astype(F32), l_ref[...]
        sg = _sigmoid(z)
        dypp = dy * (z * sg)
        dz_ref[...] = (dy * (lin_ * sc_ref[...]) * _dsilu(z, sg)).astype(BF16)
        ds_ref[...] += jnp.sum(dypp * lin_, axis=0, keepdims=True)
        dlin = (dypp * sc_ref[...]).astype(BF16)
        for gi in range(n_win):
            cs = slice(gi * pg, (gi + 1) * pg)
            pt = p_ref[:, cs].astype(F32).T.astype(BF16)
            dw_ref[gi] += jnp.dot(pt, dlin[:, cs], preferred_element_type=F32)
            dpo_ref[:, cs] = lax.dot_general(dlin[:, cs], wp_ref[gi], NT, preferred_element_type=F32)

    row = pl.BlockSpec((tm, pw), lambda m: (m, 0))
    return pl.pallas_call(
        body, name="pool_gate_bwd", grid=(s // tm,),
        in_specs=[pl.BlockSpec((tm, d), lambda m: (m, 0)), pl.BlockSpec((N_CHIPS, pw, tn), lambda m: (0, 0, 0)),
                  pl.BlockSpec((tm, pw), lambda m: (m, z_col)), row, row,
                  pl.BlockSpec((n_win, pg, pg), lambda m: (0, 0, 0)), pl.BlockSpec((1, pw), lambda m: (0, 0))],
        out_specs=[row, row, pl.BlockSpec((n_win, pg, pg), lambda m: (0, 0, 0)),
                   pl.BlockSpec((1, pw), lambda m: (0, 0))],
        out_shape=[jax.ShapeDtypeStruct((s, pw), BF16), jax.ShapeDtypeStruct((s, pw), F32),
                   jax.ShapeDtypeStruct((n_win, pg, pg), F32), jax.ShapeDtypeStruct((1, pw), F32)],
        compiler_params=_params(("arbitrary",)))(dp_in, wpp4, zuz, lin, pooled, w_pool, pool_scale)


def _pool_bwd(dpooled):
    s, pw = dpooled.shape
    pg = pw // len(POOL_WINDOWS)
    tr = _divisor_tile(s, 256, POOL_HALO)
    per = tr // POOL_HALO
    n_tiles = s // tr

    def body(c_ref, n_ref, du_ref):
        r = pl.program_id(0)
        cur = c_ref[...]
        halo = jnp.where(r < n_tiles - 1, n_ref[...], 0.0)
        ext = jnp.concatenate([cur, halo], axis=0)
        rows = tr + POOL_HALO
        for gi, window in enumerate(POOL_WINDOWS):
            cs = slice(gi * pg, (gi + 1) * pg)
            acc = ext[:, cs] / _pool_counts(r * tr, rows, window)
            shift = 1
            while shift < window:
                acc = acc + pltpu.roll(acc, rows - shift, 0)
                shift *= 2
            du_ref[:, cs] = (acc[:tr] - cur[:, cs]).astype(BF16)

    return pl.pallas_call(
        body, name="pool_bwd", grid=(n_tiles,),
        in_specs=[pl.BlockSpec((tr, pw), lambda r: (r, 0)),
                  pl.BlockSpec((POOL_HALO, pw), lambda r: (jnp.minimum((r + 1) * per, s // POOL_HALO - 1), 0))],
        out_specs=pl.BlockSpec((tr, pw), lambda r: (r, 0)),
        out_shape=jax.ShapeDtypeStruct((s, pw), BF16), compiler_params=_params(("parallel",)))(dpooled, dpooled)


def _attn_bwd(qkv, do, lse, dd, g):
    _, s, aw = qkv.shape
    heads = aw // HEAD_DIM
    n_blocks = s // STEPS
    per_seq = n_blocks // DILATIONS[g]

    def body(q_ref, do_ref, l_ref, dd_ref, kc_ref, kp_ref, vc_ref, vp_ref, out_ref, cq_ref, ck_ref, cv_ref):
        b = pl.program_id(0)

        @pl.when(b == 0)
        def _():
            cq_ref[...] = jnp.zeros_like(cq_ref)
            ck_ref[...] = jnp.zeros_like(ck_ref)
            cv_ref[...] = jnp.zeros_like(cv_ref)

        out_ref[0] = cq_ref[...].astype(BF16)

        @pl.when(b < n_blocks)
        def _():
            mask = _window_mask(lax.rem(b, per_seq) == 0)
            for h in range(heads):
                hs = slice(h * HEAD_DIM, (h + 1) * HEAD_DIM)
                q, do_ = q_ref[:, hs], do_ref[:, hs]
                kk = jnp.concatenate([kp_ref[:, hs], kc_ref[:, hs]], axis=0)
                vv = jnp.concatenate([vp_ref[:, hs], vc_ref[:, hs]], axis=0)
                lse_ = jnp.concatenate([l_ref[:, hs], l_ref[:, hs]], axis=1)
                dd_ = jnp.concatenate([dd_ref[:, hs], dd_ref[:, hs]], axis=1)
                sc = lax.dot_general(q, kk, NT, preferred_element_type=F32) * SCORE_SCALE
                prob = jnp.where(mask, jnp.exp(sc - lse_), 0.0)
                dprob = lax.dot_general(do_, vv, NT, preferred_element_type=F32)
                dsc = prob * (dprob - dd_) * SCORE_SCALE
                cq_ref[:, hs] = jnp.dot(dsc.astype(BF16), kk, preferred_element_type=F32)
                dkk = jnp.dot(dsc.T.astype(BF16), q, preferred_element_type=F32)
                dvv = jnp.dot(prob.T.astype(BF16), do_, preferred_element_type=F32)
                out_ref[1, :, hs] = (ck_ref[:, hs] + dkk[:STEPS]).astype(BF16)
                out_ref[2, :, hs] = (cv_ref[:, hs] + dvv[:STEPS]).astype(BF16)
                ck_ref[:, hs] = dkk[STEPS:]
                cv_ref[:, hs] = dvv[STEPS:]

        @pl.when(b == n_blocks)
        def _():
            out_ref[1] = ck_ref[...].astype(BF16)
            out_ref[2] = cv_ref[...].astype(BF16)

    last = n_blocks - 1

    def cur(which):
        return pl.BlockSpec((None, STEPS, aw), lambda b: (which, jnp.minimum(b, last), 0))

    def prev(which):
        return pl.BlockSpec((None, STEPS, aw), lambda b: (which, jnp.clip(b - 1, 0, last), 0))

    row = pl.BlockSpec((STEPS, aw), lambda b: (jnp.minimum(b, last), 0))
    return pl.pallas_call(
        body, name=f"attn_bwd{g}", grid=(n_blocks + 1,),
        in_specs=[cur(0), row, row, row, cur(1), prev(1), cur(2), prev(2)],
        out_specs=pl.BlockSpec((3, STEPS, aw), lambda b: (0, jnp.clip(b - 1, 0, last), 0)),
        out_shape=jax.ShapeDtypeStruct((3, s, aw), BF16),
        scratch_shapes=[pltpu.VMEM((STEPS, aw), F32)] * 3,
        compiler_params=_params(("arbitrary",)))(qkv, do, lse, dd, qkv, qkv, qkv, qkv)


def _weight_grad(at, b, tn, col_blocks, name):
    m, k = at.shape
    n = b.shape[1]
    tm = _divisor_tile(m, 1024, 16)
    tk = _divisor_tile(k, 2048, 128)
    nk = k // tk

    def body(a_ref, b_ref, o_ref, acc_ref):
        kk = pl.program_id(2)

        @pl.when(kk == 0)
        def _():
            acc_ref[...] = jnp.zeros_like(acc_ref)

        acc_ref[...] += jnp.dot(a_ref[...], b_ref[...], preferred_element_type=F32)

        @pl.when(kk == nk - 1)
        def _():
            o_ref[...] = acc_ref[...].astype(BF16)

    if col_blocks:
        out_spec = pl.BlockSpec((None, tm, tn), lambda i, j, kk: (j, i, 0))
        out_shape = jax.ShapeDtypeStruct((n // tn, m, tn), BF16)
    else:
        out_spec = pl.BlockSpec((tm, tn), lambda i, j, kk: (i, j))
        out_shape = jax.ShapeDtypeStruct((m, n), BF16)
    return pl.pallas_call(
        body, name=name, grid=(m // tm, n // tn, nk),
        in_specs=[pl.BlockSpec((tm, tk), lambda i, j, kk: (i, kk)), pl.BlockSpec((tk, tn), lambda i, j, kk: (kk, j))],
        out_specs=out_spec, out_shape=out_shape, scratch_shapes=[pltpu.VMEM((tm, tn), F32)],
        compiler_params=_params(("parallel", "parallel", "arbitrary")))(at, b)


def _w_in_grad_part(xt, b, col_of, n_local, tn, w_shape, prev, name):
    d, s = xt.shape
    per_chip = w_shape[2] // tn
    tm = _divisor_tile(d, 1024, 16)
    tk = _divisor_tile(s, 2048, 128)
    nk = s // tk

    def body(*refs):
        a_ref, b_ref, o_ref, acc_ref = refs[0], refs[1], refs[-2], refs[-1]
        kk = pl.program_id(2)

        @pl.when(kk == 0)
        def _():
            acc_ref[...] = jnp.zeros_like(acc_ref)

        acc_ref[...] += jnp.dot(a_ref[...], b_ref[...], preferred_element_type=F32)

        @pl.when(kk == nk - 1)
        def _():
            o_ref[...] = acc_ref[...].astype(BF16)

    if b.ndim == 3:
        sub = b.shape[2] // tn
        b_spec = pl.BlockSpec((None, tk, tn), lambda j, i, kk: (j // sub, kk, j % sub))
    else:
        b_spec = pl.BlockSpec((tk, tn), lambda j, i, kk: (kk, j))
    in_specs = [pl.BlockSpec((tm, tk), lambda j, i, kk: (i, kk)), b_spec]
    args = [xt, b]
    aliases = {}
    if prev is not None:
        in_specs.append(ANY)
        args.append(prev)
        aliases = {2: 0}
    return pl.pallas_call(
        body, name=name, grid=(n_local, d // tm, nk), in_specs=in_specs,
        out_specs=pl.BlockSpec((None, tm, tn), lambda j, i, kk: (col_of(j) // per_chip, i, col_of(j) % per_chip)),
        out_shape=jax.ShapeDtypeStruct(w_shape, BF16), scratch_shapes=[pltpu.VMEM((tm, tn), F32)],
        input_output_aliases=aliases,
        compiler_params=_params(("parallel", "parallel", "arbitrary")))(*args)


def _x_grad(dqkv, rest, wc, chunk, init, init_scale, aw, tn, after=None):
    s, d = init.shape
    sub = aw // tn
    n_qkv = 3 * N_GROUPS * sub
    los, lo = [], n_qkv
    for p in rest:
        los.append(lo)
        lo += p.shape[1] // tn
    per_chip = lo // N_CHIPS
    per = per_chip // W_CHUNKS
    n_local = lo // W_CHUNKS
    tm = _divisor_tile(s, 512, 16 * DILATIONS[-1])

    def col(jl):
        return (jl // per) * per_chip + chunk * per + jl % per

    ordered = [] if after is None else [after]

    def body(*refs):
        refs = refs[len(ordered):]
        q_refs, r_refs = refs[:N_GROUPS], refs[N_GROUPS:N_GROUPS + len(rest)]
        w_ref, init_ref, o_ref, acc_ref, scratch = refs[-5:]
        jl = pl.program_id(1)
        j = col(jl)

        @pl.when(jl == 0)
        def _():
            acc_ref[...] = init_scale * init_ref[...]

        for g, dil in enumerate(DILATIONS):
            @pl.when((j < n_qkv) & (lax.rem(j // sub, N_GROUPS) == g))
            def _(g=g, dil=dil):
                rows = _merge_rows(q_refs[g], scratch, dil).astype(BF16)
                acc_ref[...] += lax.dot_general(rows, w_ref[...], NT, preferred_element_type=F32)

        for p_ref, lo_, piece in zip(r_refs, los, rest):
            @pl.when((j >= lo_) & (j < lo_ + piece.shape[1] // tn))
            def _(p_ref=p_ref):
                acc_ref[...] += lax.dot_general(p_ref[...], w_ref[...], NT, preferred_element_type=F32)

        @pl.when(jl == n_local - 1)
        def _():
            o_ref[...] = acc_ref[...]

    def qkv_spec(dil):
        def index(i, jl):
            j = col(jl)
            region = jnp.minimum(j // sub, 3 * N_GROUPS - 1)
            return region // N_GROUPS, 0, i, jnp.where(j < n_qkv, j % sub, 0)

        return pl.BlockSpec((None, dil, tm // dil, tn), index)

    def rest_spec(lo_, piece):
        n = piece.shape[1] // tn
        return pl.BlockSpec((tm, tn), lambda i, jl: (i, jnp.clip(col(jl) - lo_, 0, n - 1)))

    row = pl.BlockSpec((tm, d), lambda i, jl: (i, 0))
    return pl.pallas_call(
        body, name=f"x_grad{chunk}", grid=(s // tm, n_local),
        in_specs=[pl.BlockSpec(t.shape, lambda i, jl: (0, 0)) for t in ordered]
        + [qkv_spec(dil) for dil in DILATIONS] + [rest_spec(lo_, p) for lo_, p in zip(los, rest)]
        + [pl.BlockSpec((None, d, tn), lambda i, jl: (jl // per, 0, jl % per)), row],
        out_specs=row, out_shape=jax.ShapeDtypeStruct((s, d), F32),
        scratch_shapes=[pltpu.VMEM((tm, d), F32), _permute_scratch(tm, tn)],
        compiler_params=_params(("parallel", "arbitrary"), vmem_mib=56))(
            *ordered, *[t.reshape(3, dil, s // dil, aw) for t, dil in zip(dqkv, DILATIONS)], *rest, wc, init)


def _prepare_x(x, after=None):
    s, d = x.shape
    tc = 2 * LANES
    slabs = tc // LANES
    ordered = [] if after is None else [after]

    def body(*refs):
        x_ref, xb_ref = refs[len(ordered):len(ordered) + 2]
        xt_refs, scratch = refs[len(ordered) + 2:len(ordered) + 2 + N_GROUPS], refs[-1]
        t = x_ref[...]
        xb_ref[...] = t.astype(BF16)
        for c in range(slabs):
            scratch[c] = t[:, c * LANES:(c + 1) * LANES]
        for g, dil in enumerate(DILATIONS):
            length = s // dil
            for r in range(dil):
                part = t if dil == 1 else jnp.concatenate(
                    [scratch[c, pl.ds(r, length, stride=dil), :] for c in range(slabs)], axis=1)
                xt_refs[g][:, r * length:(r + 1) * length] = part.T.astype(BF16)

    col = pl.BlockSpec((s, tc), lambda j: (0, j))
    row = pl.BlockSpec((tc, s), lambda j: (j, 0))
    t_shape = jax.ShapeDtypeStruct((d, s), BF16)
    out = pl.pallas_call(
        body, name="prepare_x", grid=(d // tc,),
        in_specs=[pl.BlockSpec(t.shape, lambda j: (0, 0)) for t in ordered] + [col],
        out_specs=[col] + [row] * N_GROUPS,
        out_shape=[jax.ShapeDtypeStruct((s, d), BF16)] + [t_shape] * N_GROUPS,
        scratch_shapes=[_permute_scratch(s, tc)], compiler_params=_params(("parallel",)))(*ordered, x)
    return out[0], out[1:]


def _local_step(x, target, w_chunk, w_width, b_gate, pool_scale, gamma, beta, aw, pw, small_weights,
                start_exchange=None, first_token=None):
    s, d = x.shape
    tn = _col_tile(aw, pw, w_width)
    sub = aw // tn
    per_chip = w_width // tn
    qkv_w = 3 * N_GROUPS * aw
    w_shape = (N_CHIPS, d, w_width)

    regions = [dict(kind=g, blocks=[(which * N_GROUPS + g) * sub + i for which in range(3) for i in range(sub)])
               for g in range(N_GROUPS)]
    lo = qkv_w // tn
    for name, width in (("zuz", aw + 2 * pw), ("gates", 2 * d)):
        regions.append(dict(kind=name, blocks=list(range(lo, lo + width // tn)), j0=lo, width=width))
        lo += width // tn
    results = [None] * len(regions)
    xb, xts = _prepare_x(x, first_token)
    wcs, after = [], [xb]
    for ch in range(W_CHUNKS):
        wc, token = w_chunk(ch, after)
        wcs.append(wc)
        after = []
        for i, region in enumerate(regions):
            blocks = [b for b in region["blocks"] if _chunk_of(b, per_chip) == ch]
            if not blocks:
                continue
            if region["kind"] in range(N_GROUPS):
                results[i] = _in_proj_qkv(xb, wc, region["kind"], blocks, aw, tn, results[i], token,
                                          f"in_proj_qkv{region['kind']}_{ch}")
            else:
                results[i] = _in_proj(xb, wc, blocks, region["j0"], region["width"], tn, BF16, results[i], token,
                                      f"in_proj_{region['kind']}_{ch}")
            after.append(results[i])
    qkv = [results[g].reshape(3, s, aw) for g in range(N_GROUPS)]
    zuz, gpre = results[N_GROUPS], results[N_GROUPS + 1]

    attn = [_attn_fwd(qkv[g], g) for g in range(N_GROUPS)]
    o, y_attn, y_attn_t, lse = _combine_groups([a[0] for a in attn], [a[1] for a in attn], zuz, aw)
    w_pool, wpa4, wpp4, w_out = small_weights(o)
    pooled, lin, y_pool, y_pool_t = _pool_fwd(zuz, w_pool, pool_scale, aw, pw)
    a, p, sa, sp, merged, merged_t = _proj_merge(y_attn, y_pool, wpa4, wpp4, gpre, b_gate)
    dr, drb, loss_lanes, d_gamma, d_beta = _out_norm_loss(merged, w_out, x, target, gamma, beta)

    da, dp, d_gpre_a, d_gpre_p, d_b_a, d_b_p = _merge_bwd(drb, w_out, a, p, sa, sp)
    d_b_gate = jnp.concatenate([d_b_a, d_b_p], axis=1)
    d_w_out = _weight_grad(merged_t, drb, d // N_CHIPS, False, "w_out_grad")
    d_wpa4 = _weight_grad(y_attn_t, da, d // N_CHIPS, True, "w_proj_attn_grad")
    d_wpp4 = _weight_grad(y_pool_t, dp, d // N_CHIPS, True, "w_proj_pool_grad")
    d_z_attn, d_o, dd = _attn_gate_bwd(da, wpa4, zuz, o)
    d_z_pool, d_pooled, d_w_pool, d_pool_scale = _pool_gate_bwd(dp, wpp4, zuz, lin, pooled, w_pool, pool_scale, aw)
    d_u = _pool_bwd(d_pooled)
    dqkv = [_attn_bwd(qkv[g], d_o[g], lse[g], dd[g], g) for g in range(N_GROUPS)]

    rest = [d_z_attn, d_u, d_z_pool, d_gpre_a, d_gpre_p]
    d_w_in4 = None
    for g in range(N_GROUPS):
        d_w_in4 = _w_in_grad_part(xts[g], dqkv[g], lambda j, g=g: ((j // sub) * N_GROUPS + g) * sub + j % sub,
                                  3 * sub, tn, w_shape, d_w_in4, f"w_in_grad_qkv{g}")
    lo = qkv_w // tn
    for i, piece in enumerate(rest):
        n_local = piece.shape[1] // tn
        d_w_in4 = _w_in_grad_part(xts[0], piece, lambda j, lo=lo: lo + j, n_local, tn, w_shape, d_w_in4,
                                  f"w_in_grad_rest{i}")
        lo += n_local
    grads = dict(loss_lanes=loss_lanes, w_in=d_w_in4, b_gate=d_b_gate, w_pool=d_w_pool,
                 pool_scale=d_pool_scale, w_proj_attn=d_wpa4, w_proj_pool=d_wpp4, w_out=d_w_out,
                 ln_gamma=d_gamma, ln_beta=d_beta)
    token = None if start_exchange is None else start_exchange(grads)
    d_x, scale = dr, ALPHA
    for ch in range(W_CHUNKS):
        d_x = _x_grad(dqkv, rest, wcs[ch], ch, d_x, scale, aw, tn, token)
        token, scale = None, 1.0
    grads["d_x"] = d_x
    return grads


def _pack_small(wpa, wpp, w_out, w_pool):
    width = wpa.shape[1]
    return jnp.concatenate([wpa, wpp, w_out.reshape(-1, width), w_pool.reshape(-1, width)], axis=0)


def _unpack_small(packed, aw, pw, d, pg):
    lead = packed.shape[:-2]
    width = d // N_CHIPS
    r0, r1, r2 = aw, aw + pw, aw + pw + d
    return (packed[..., :r0, :], packed[..., r0:r1, :], packed[..., r1:r2, :].reshape(lead + (width, d)),
            packed[..., r2:, :].reshape(lead + (len(POOL_WINDOWS), pg // N_CHIPS, pg)))


def _pack_rows(vectors, rows):
    flat = jnp.concatenate([v.reshape(-1) for v in vectors])
    return jnp.pad(flat, (0, rows * 128 - flat.shape[0])).reshape(rows, 128)


def _unpack_rows(packed, sizes):
    flat, out, lo = packed.reshape(-1), [], 0
    for n in sizes:
        out.append(flat[lo:lo + n].reshape(1, n))
        lo += n
    return out


def kernel(x, w_in, b_gate, w_pool, pool_scale, w_proj_attn, w_proj_pool, w_out, ln_gamma, ln_beta, loss_target, m_w_in, m_b_gate, m_w_pool, m_pool_scale, m_w_proj_attn, m_w_proj_pool, m_w_out, m_ln_gamma, m_ln_beta, v_w_in, v_b_gate, v_w_pool, v_pool_scale, v_w_proj_attn, v_w_proj_pool, v_w_out, v_ln_gamma, v_ln_beta):
    s, d = x.shape[1], x.shape[2]
    aw, pw = w_proj_attn.shape[1], w_proj_pool.shape[1]
    pg = w_pool.shape[3]
    n_win = len(POOL_WINDOWS)

    def small(wpa, wpp, wo, wpl):
        return _pack_small(wpa[0], wpp[0], wo[0], wpl[0])

    chip = 2 * lax.axis_index("x") + lax.axis_index("y")
    core = lax.axis_index("c")

    w_small = small(w_proj_attn, w_proj_pool, w_out, w_pool)
    placed = [_place_block(w_in[0], N_CHIPS, chip, BF16, f"place_w_in{ch}", ch, W_CHUNKS) for ch in range(W_CHUNKS)]
    placed_small = _place_block(w_small, N_CHIPS, chip, BF16, "place_w_small")
    flight = {"chunk": _halves_start(placed[0], placed_small, "gather_w_in0_start")}
    first_token = flight["chunk"][2]

    def w_chunk(ch, after):
        sems, thru, _ = flight["chunk"]
        landed = _halves_wait(sems, thru, after, f"gather_w_in{ch}_wait")
        if ch + 1 < W_CHUNKS:
            flight["chunk"] = _halves_start(placed[ch + 1], landed, f"gather_w_in{ch + 1}_start")
            token = flight["chunk"][2]
        else:
            flight["small"] = _broadcast_start(placed_small, landed, "gather_small_start")
            token = flight["small"][2]
        return _forward_halves(landed, f"forward_w_in{ch}"), token

    def small_weights(after):
        sems, thru, _ = flight["small"]
        small4 = _broadcast_wait(sems, thru, after, "gather_small_wait")
        wpa4, wpp4, w_out4, w_pool4 = _unpack_small(small4, aw, pw, d, pg)
        return w_pool4.transpose(1, 0, 2, 3).reshape(n_win, pg, pg), wpa4, wpp4, w_out4.reshape(d, d)

    exchange = {}

    def start_exchange(g):
        g_pool4 = g["w_pool"].reshape(n_win, N_CHIPS, pg // N_CHIPS, pg).transpose(1, 0, 2, 3).astype(BF16)
        g_out4 = g["w_out"].reshape(N_CHIPS, d // N_CHIPS, d)
        g_small4 = jnp.concatenate([g["w_proj_attn"], g["w_proj_pool"], g_out4.reshape(N_CHIPS, -1, d // N_CHIPS),
                                    g_pool4.reshape(N_CHIPS, -1, d // N_CHIPS)], axis=1)
        theirs_big, theirs_small = _swap_halves([g["w_in"], g_small4])
        chip_big, placed_big = _add_halves(g["w_in"], theirs_big, core, chip, "add_cores_big")
        chip_small, placed_small = _add_halves(g_small4, theirs_small, core, chip, "add_cores_small")
        sems, sums, placed, token = _scatter_start([chip_big, chip_small], [placed_big, placed_small])
        exchange.update(sems=sems, sums=sums, placed=placed)
        return token

    g = _local_step(x[0], loss_target[0], w_chunk, w_in.shape[2], b_gate, pool_scale, ln_gamma, ln_beta, aw, pw,
                    small_weights, start_exchange, first_token)
    got_big, got_small = _scatter_wait(exchange["sems"], exchange["sums"], exchange["placed"], g["d_x"])
    join_sems, halves = _join_start([_sum_slots(got_big, core, "sum_chips_big"),
                                     _sum_slots(got_small, core, "sum_chips_small")])
    mv_small = (small(m_w_proj_attn, m_w_proj_pool, m_w_out, m_w_pool),
                small(v_w_proj_attn, v_w_proj_pool, v_w_out, v_w_pool))
    upd_in = _adamw_half(w_in[0], halves[0], m_w_in[0], v_w_in[0], core, None, "adamw_w_in_own")
    upd_small = _adamw_half(w_small, halves[1], *mv_small, core, None, "adamw_small_own")
    grad_w_in, grad_small = _join_wait(join_sems, halves, [upd_in[0], upd_small[0]])
    upd_in = _adamw_half(w_in[0], grad_w_in, m_w_in[0], v_w_in[0], 1 - core, upd_in, "adamw_w_in_other")
    upd_small = _adamw_half(w_small, grad_small, *mv_small, 1 - core, upd_small, "adamw_small_other")
    grad_w_in = grad_w_in.reshape(-1, grad_w_in.shape[2])
    grad_small = grad_small.reshape(-1, grad_small.shape[2])

    sizes = [b_gate.shape[1], pool_scale.shape[1], d, d, 1]
    rows = -(-sum(sizes) // (8 * 128)) * 8
    loss_part = (0.5 / d) * jnp.sum(g["loss_lanes"]).reshape(1, 1)
    parts = _gather_rows(_pack_rows([g["b_gate"], g["pool_scale"], g["ln_gamma"], g["ln_beta"], loss_part], rows))
    zero = jnp.zeros((1, 1), F32)
    packed = [_pack_rows(vs, rows) for vs in ([b_gate, pool_scale, ln_gamma, ln_beta, zero],
                                              [m_b_gate, m_pool_scale, m_ln_gamma, m_ln_beta, zero],
                                              [v_b_gate, v_pool_scale, v_ln_gamma, v_ln_beta, zero])]
    rep = [_unpack_rows(t, sizes) for t in _sum_rows_adamw(parts, *packed)]
    loss = rep[0][4].reshape(())

    def leaves(big, packed_small, replicated):
        wpa_, wpp_, wo_, wpl_ = _unpack_small(packed_small, aw, pw, d, pg)
        return [big[None], replicated[0], wpl_[None], replicated[1], wpa_[None], wpp_[None], wo_[None],
                replicated[2], replicated[3]]

    out = [loss, g["d_x"][None]]
    out += leaves(grad_w_in, grad_small, rep[0])
    for i in range(3):
        out += leaves(upd_in[i], upd_small[i], rep[1 + i])
    return tuple(out)
```

```python
import math

import jax
import jax.numpy as jnp
from jax import lax
from jax.experimental import pallas as pl
from jax.experimental.pallas import tpu as pltpu

F32 = jnp.float32
BF16 = jnp.bfloat16
MESH = pl.DeviceIdType.MESH
ANY = pl.BlockSpec(memory_space=pl.ANY)

HEAD_DIM = 128
STEPS = 128
DILATIONS = (1, 4, 16)
N_GROUPS = len(DILATIONS)
POOL_WINDOWS = (2, 4, 8, 16)
POOL_HALO = 16
N_CHIPS = 4
N_DEV = 8
ALPHA = 2.0 ** 0.25
LN_EPS = 1e-5
NEG_INF = -1e30
SCORE_SCALE = HEAD_DIM ** -0.5
ADAM_LR = 0.001
ADAM_B1 = 0.9
ADAM_B2 = 0.999
ADAM_EPS = 1e-08
ADAM_WD = 0.01
ADAM_STEP = 10
MIB = 2 ** 20
NT = (((1,), (1,)), ((), ()))
TN = (((0,), (0,)), ((), ()))
DMA_STREAMS = 8


def _params(semantics=None, vmem_mib=48):
    return pltpu.CompilerParams(dimension_semantics=semantics, vmem_limit_bytes=vmem_mib * MIB)


def _divisor_tile(n, target, multiple):
    best = None
    for t in range(multiple, min(n, target) + 1, multiple):
        if n % t == 0:
            best = t
    assert best is not None, (n, target, multiple)
    return best


def _col_tile(*widths):
    g = 0
    for w in widths:
        g = math.gcd(g, w)
    return _divisor_tile(g, 1024, 128)


def _sigmoid(z):
    return jax.nn.sigmoid(z)


def _dsilu(z, sg):
    return sg * (1.0 + z * (1.0 - sg))


def _place():
    x, y, c = lax.axis_index("x"), lax.axis_index("y"), lax.axis_index("c")
    others = [(1 - x, y), (x, 1 - y), (1 - x, 1 - y)]
    return x, y, c, (x, y, 1 - c), others


def _remote(src, dst, send_sem, recv_sem, dev):
    return pltpu.make_async_remote_copy(src_ref=src, dst_ref=dst, send_sem=send_sem, recv_sem=recv_sem,
                                        device_id=dev, device_id_type=MESH)


def _row_pieces(n_rows, streams=DMA_STREAMS, multiple=16):
    size = -(-n_rows // (streams * multiple)) * multiple
    return [(lo, min(size, n_rows - lo)) for lo in range(0, n_rows, size)]


def _start_streams(make, n_rows):
    for lo, size in _row_pieces(n_rows):
        make(pl.ds(lo, size)).start()


def _half_copies(buf, send_sems, recv_sems):
    x, y, c, _, others = _place()
    half = buf.shape[1] // 2
    slab = buf.at[2 * x + y, pl.ds(c * half, half)]
    return [_remote(slab, slab, send_sems[j], recv_sems[j], (ox, oy, c)) for j, (ox, oy) in enumerate(others)]


def _halves_start(placed, after, name):
    k = N_CHIPS - 1

    def body(buf, after_ref, *refs):
        send_sems, recv_sems, token = refs[:k], refs[k:2 * k], refs[-1]
        for cp in _half_copies(buf, send_sems, recv_sems):
            cp.start()
        token[...] = jnp.zeros_like(token)

    out = pl.pallas_call(
        body, name=name,
        out_shape=[pltpu.SemaphoreType.DMA(())] * (2 * k) + [pltpu.HBM(placed.shape, placed.dtype),
                                                             jax.ShapeDtypeStruct((8, 128), F32)],
        in_specs=[HBM, ANY], out_specs=[SEM] * (2 * k) + [HBM, pl.BlockSpec(memory_space=pltpu.VMEM)],
        input_output_aliases={0: 2 * k},
        compiler_params=pltpu.CompilerParams(has_side_effects=DATAFLOW),
    )(pltpu.with_memory_space_constraint(placed, pltpu.HBM), after)
    return out[:2 * k], out[2 * k], out[-1]


def _halves_wait(sems, placed, after, name):
    k = N_CHIPS - 1

    def body(buf, *refs):
        send_sems, recv_sems = refs[:k], refs[k:2 * k]
        for cp in _half_copies(buf, send_sems, recv_sems):
            cp.wait_send()
            cp.wait_recv()

    return pl.pallas_call(
        body, name=name, out_shape=pltpu.HBM(placed.shape, placed.dtype),
        in_specs=[HBM] + [SEM] * (2 * k) + [ANY] * len(after), out_specs=HBM, input_output_aliases={0: 0},
        compiler_params=pltpu.CompilerParams(has_side_effects=DATAFLOW),
    )(placed, *sems, *after)


def _forward_halves(buf, name):
    def body(_, dst, send_sems, recv_sems):
        x, y, c, sibling, others = _place()
        half = dst.shape[1] // 2
        for j, (ox, oy) in enumerate(others):
            slab = dst.at[2 * ox + oy, pl.ds(c * half, half)]
            _remote(slab, slab, send_sems.at[j], recv_sems.at[j], sibling).start()
        for j, (ox, oy) in enumerate(others):
            mine = dst.at[2 * ox + oy, pl.ds(c * half, half)]
            theirs = dst.at[2 * ox + oy, pl.ds((1 - c) * half, half)]
            cp = _remote(mine, theirs, send_sems.at[j], recv_sems.at[j], sibling)
            cp.wait_recv()
            cp.wait_send()

    return pl.pallas_call(
        body, name=name, out_shape=jax.ShapeDtypeStruct(buf.shape, buf.dtype),
        in_specs=[ANY], out_specs=ANY, input_output_aliases={0: 0},
        scratch_shapes=[pltpu.SemaphoreType.DMA((N_CHIPS - 1,)), pltpu.SemaphoreType.DMA((N_CHIPS - 1,))],
    )(buf)


def _swap_halves(grads):
    n = len(grads)

    def body(*refs):
        g, theirs = refs[:n], refs[n:2 * n]
        send_sems, recv_sems = refs[2 * n:]
        x, y, c, sibling, _ = _place()
        for i in range(n):
            half = g[i].shape[1] // 2
            give = (1 - c) * half
            for b in range(N_CHIPS):
                _start_streams(lambda r, i=i, b=b: _remote(
                    g[i].at[b, pl.ds(give + r.start, r.size)], theirs[i].at[b, r], send_sems.at[i], recv_sems.at[i],
                    sibling), half)
        for i in range(n):
            _remote(theirs[i], theirs[i], send_sems.at[i], recv_sems.at[i], sibling).wait()

    return pl.pallas_call(
        body, name="swap_halves",
        out_shape=[jax.ShapeDtypeStruct((s.shape[0], s.shape[1] // 2) + s.shape[2:], s.dtype) for s in grads],
        in_specs=[ANY] * n, out_specs=[ANY] * n,
        scratch_shapes=[pltpu.SemaphoreType.DMA((n,)), pltpu.SemaphoreType.DMA((n,))],
    )(*grads)


HBM = pl.BlockSpec(memory_space=pltpu.HBM)
SEM = pl.BlockSpec(memory_space=pltpu.SEMAPHORE)
DATAFLOW = pltpu.SideEffectType.DATAFLOW_SIDE_EFFECTING


def _broadcast_copies(buf, send_sems, recv_sems):
    x, y, c, _, others = _place()
    mine = buf.at[2 * x + y]
    return [_remote(mine, mine, send_sems[j], recv_sems[j], (ox, oy, c)) for j, (ox, oy) in enumerate(others)]


def _broadcast_start(placed, after, name):
    k = N_CHIPS - 1

    def body(buf, after_ref, *refs):
        send_sems, recv_sems, token = refs[:k], refs[k:2 * k], refs[-1]
        for cp in _broadcast_copies(buf, send_sems, recv_sems):
            cp.start()
        token[...] = jnp.zeros_like(token)

    out = pl.pallas_call(
        body, name=name,
        out_shape=[pltpu.SemaphoreType.DMA(())] * (2 * k) + [pltpu.HBM(placed.shape, placed.dtype),
                                                             jax.ShapeDtypeStruct((8, 128), F32)],
        in_specs=[HBM, ANY], out_specs=[SEM] * (2 * k) + [HBM, pl.BlockSpec(memory_space=pltpu.VMEM)],
        input_output_aliases={0: 2 * k},
        compiler_params=pltpu.CompilerParams(has_side_effects=DATAFLOW),
    )(pltpu.with_memory_space_constraint(placed, pltpu.HBM), after)
    return out[:2 * k], out[2 * k], out[-1]


def _broadcast_wait(sems, placed, after, name):
    k = N_CHIPS - 1

    def body(buf, *refs):
        send_sems, recv_sems = refs[:k], refs[k:2 * k]
        for cp in _broadcast_copies(buf, send_sems, recv_sems):
            cp.wait_send()
            cp.wait_recv()

    return pl.pallas_call(
        body, name=name, out_shape=pltpu.HBM(placed.shape, placed.dtype),
        in_specs=[HBM] + [SEM] * (2 * k) + [ANY], out_specs=HBM, input_output_aliases={0: 0},
        compiler_params=pltpu.CompilerParams(has_side_effects=DATAFLOW),
    )(placed, *sems, after)


def _scatter_copies(s, got, send_sems, recv_sems):
    x, y, c, _, others = _place()
    me = 2 * x + y
    n = len(s)
    return [_remote(s[i].at[2 * ox + oy], got[i].at[me], send_sems[3 * i + j], recv_sems[3 * i + j], (ox, oy, c))
            for i in range(n) for j, (ox, oy) in enumerate(others)]


def _scatter_start(sums, placed):
    n = len(sums)
    k = 3 * n

    def body(*refs):
        s, got, token = refs[:n], refs[n:2 * n], refs[-1]
        send_sems, recv_sems = refs[2 * n:2 * n + k], refs[2 * n + k:2 * n + 2 * k]
        for cp in _scatter_copies(s, got, send_sems, recv_sems):
            cp.start()
        token[...] = jnp.zeros_like(token)

    hbm = [pltpu.HBM(a.shape, a.dtype) for a in list(sums) + list(placed)]
    out = pl.pallas_call(
        body, name="scatter_start",
        out_shape=[pltpu.SemaphoreType.DMA(())] * (2 * k) + hbm + [jax.ShapeDtypeStruct((8, 128), F32)],
        in_specs=[HBM] * (2 * n), out_specs=[SEM] * (2 * k) + [HBM] * (2 * n) + [pl.BlockSpec(memory_space=pltpu.VMEM)],
        input_output_aliases={i: 2 * k + i for i in range(2 * n)},
        compiler_params=pltpu.CompilerParams(has_side_effects=DATAFLOW),
    )(*[pltpu.with_memory_space_constraint(a, pltpu.HBM) for a in list(sums) + list(placed)])
    return out[:2 * k], out[2 * k:2 * k + n], out[2 * k + n:2 * k + 2 * n], out[-1]


def _scatter_wait(sems, sums, placed, after):
    n = len(sums)
    k = 3 * n

    def body(*refs):
        s, got = refs[:n], refs[n:2 * n]
        send_sems, recv_sems = refs[2 * n:2 * n + k], refs[2 * n + k:2 * n + 2 * k]
        for cp in _scatter_copies(s, got, send_sems, recv_sems):
            cp.wait_send()
            cp.wait_recv()

    hbm = [pltpu.HBM(a.shape, a.dtype) for a in list(sums) + list(placed)]
    out = pl.pallas_call(
        body, name="scatter_wait", out_shape=hbm,
        in_specs=[HBM] * (2 * n) + [SEM] * (2 * k) + [ANY], out_specs=[HBM] * (2 * n),
        input_output_aliases={i: i for i in range(2 * n)},
        compiler_params=pltpu.CompilerParams(has_side_effects=DATAFLOW),
    )(*sums, *placed, *sems, after)
    return out[n:]


def _join_copies(bufs, send_sems, recv_sems):
    x, y, c, sibling, _ = _place()
    return [_remote(b.at[c], b.at[c], send_sems[i], recv_sems[i], sibling) for i, b in enumerate(bufs)]


def _join_start(placed):
    n = len(placed)

    def body(*refs):
        bufs, send_sems, recv_sems = refs[:n], refs[n:2 * n], refs[2 * n:3 * n]
        for cp in _join_copies(bufs, send_sems, recv_sems):
            cp.start()

    hbm = [pltpu.HBM(a.shape, a.dtype) for a in placed]
    out = pl.pallas_call(
        body, name="join_start", out_shape=[pltpu.SemaphoreType.DMA(())] * (2 * n) + hbm,
        in_specs=[HBM] * n, out_specs=[SEM] * (2 * n) + [HBM] * n,
        input_output_aliases={i: 2 * n + i for i in range(n)},
        compiler_params=pltpu.CompilerParams(has_side_effects=DATAFLOW),
    )(*[pltpu.with_memory_space_constraint(a, pltpu.HBM) for a in placed])
    return out[:2 * n], out[2 * n:]


def _join_wait(sems, placed, after):
    n = len(placed)

    def body(*refs):
        bufs, send_sems, recv_sems = refs[:n], refs[n:2 * n], refs[2 * n:3 * n]
        for cp in _join_copies(bufs, send_sems, recv_sems):
            cp.wait_send()
            cp.wait_recv()

    return pl.pallas_call(
        body, name="join_wait", out_shape=[pltpu.HBM(a.shape, a.dtype) for a in placed],
        in_specs=[HBM] * n + [SEM] * (2 * n) + [ANY] * len(after), out_specs=[HBM] * n,
        input_output_aliases={i: i for i in range(n)},
        compiler_params=pltpu.CompilerParams(has_side_effects=DATAFLOW),
    )(*placed, *sems, *after)


def _gather_rows(row):
    def body(row_ref, out_ref, send_sems, recv_sems, local_sem):
        x, y, c = lax.axis_index("x"), lax.axis_index("y"), lax.axis_index("c")
        me = 4 * x + 2 * y + c
        local = pltpu.make_async_copy(row_ref, out_ref.at[me], local_sem)
        local.start()
        sent = []
        peers = []
        for k in range(1, N_DEV):
            px, py, pc = x ^ (k >> 2), y ^ ((k >> 1) & 1), c ^ (k & 1)
            peers.append((k, px, py, pc))
            cp = _remote(row_ref, out_ref.at[me], send_sems.at[k - 1], recv_sems.at[k - 1], (px, py, pc))
            cp.start()
            sent.append(cp)
        for k, px, py, pc in peers:
            slot = out_ref.at[4 * px + 2 * py + pc]
            _remote(slot, slot, send_sems.at[k - 1], recv_sems.at[k - 1], (px, py, pc)).wait_recv()
        for cp in sent:
            cp.wait_send()
        local.wait()

    return pl.pallas_call(
        body, name="gather_rows", out_shape=jax.ShapeDtypeStruct((N_DEV,) + row.shape, row.dtype),
        in_specs=[ANY], out_specs=ANY,
        scratch_shapes=[pltpu.SemaphoreType.DMA((N_DEV - 1,)), pltpu.SemaphoreType.DMA((N_DEV - 1,)),
                        pltpu.SemaphoreType.DMA],
    )(row)


def _scalar(i):
    return jnp.reshape(i, (1,)).astype(jnp.int32)


def _place_block(src, n_slots, slot, out_dtype, name, window=0, n_windows=1):
    rows, cols = src.shape[0], src.shape[1] // n_windows
    tr = _divisor_tile(rows, max(16, (2 * MIB) // (cols * 4)), 16)

    def body(slot_ref, s_ref, o_ref):
        o_ref[...] = s_ref[...].astype(o_ref.dtype)

    return pl.pallas_call(
        body, name=name, out_shape=jax.ShapeDtypeStruct((n_slots, rows, cols), out_dtype),
        grid_spec=pltpu.PrefetchScalarGridSpec(
            num_scalar_prefetch=1, grid=(rows // tr,), in_specs=[pl.BlockSpec((tr, cols), lambda r, sl: (r, window))],
            out_specs=pl.BlockSpec((None, tr, cols), lambda r, sl: (sl[0], r, 0))),
        compiler_params=_params(("parallel",)))(_scalar(slot), src)


def _add_halves(g, theirs, core, chip, name):
    n, half, cols = theirs.shape
    tr = _divisor_tile(half, max(16, (2 * MIB) // (cols * 4)), 16)
    per = half // tr

    def body(at_ref, a_ref, b_ref, o_ref, own_ref):
        total = (a_ref[...].astype(F32) + b_ref[...].astype(F32)).astype(o_ref.dtype)
        o_ref[...] = total

        @pl.when(pl.program_id(1) == at_ref[1])
        def _():
            own_ref[...] = total

    spec = pl.BlockSpec((None, tr, cols), lambda r, i, at: (i, r, 0))
    shape = jax.ShapeDtypeStruct(theirs.shape, BF16)
    return pl.pallas_call(
        body, name=name, out_shape=[shape, shape],
        grid_spec=pltpu.PrefetchScalarGridSpec(
            num_scalar_prefetch=1, grid=(per, n),
            in_specs=[pl.BlockSpec((None, tr, cols), lambda r, i, at: (i, at[0] * per + r, 0)), spec],
            out_specs=[spec, pl.BlockSpec((None, tr, cols), lambda r, i, at: (at[1], r, 0))]),
        compiler_params=_params(("parallel", "arbitrary")))(jnp.concatenate([_scalar(core), _scalar(chip)]), g, theirs)


def _sum_slots(a, core, name):
    n, rows, cols = a.shape
    tr = _divisor_tile(rows, max(16, (2 * MIB) // (cols * 4 * n)), 16)

    def body(c_ref, a_ref, o_ref):
        acc = a_ref[0].astype(F32)
        for i in range(1, n):
            acc = acc + a_ref[i].astype(F32)
        o_ref[...] = acc

    return pl.pallas_call(
        body, name=name, out_shape=jax.ShapeDtypeStruct((2, rows, cols), F32),
        grid_spec=pltpu.PrefetchScalarGridSpec(
            num_scalar_prefetch=1, grid=(rows // tr,),
            in_specs=[pl.BlockSpec((n, tr, cols), lambda r, c: (0, r, 0))],
            out_specs=pl.BlockSpec((None, tr, cols), lambda r, c: (c[0], r, 0))),
        compiler_params=_params(("parallel",)))(_scalar(core), a)


def _adamw_math(w, g, m, v):
    m = ADAM_B1 * m + (1.0 - ADAM_B1) * g
    v = ADAM_B2 * v + (1.0 - ADAM_B2) * (g * g)
    m_hat = m / (1.0 - ADAM_B1 ** ADAM_STEP)
    v_hat = v / (1.0 - ADAM_B2 ** ADAM_STEP)
    delta = -ADAM_LR * (m_hat / (jnp.sqrt(v_hat) + ADAM_EPS) + ADAM_WD * w)
    return delta, m, v


def _adamw_half(w, g2, m, v, which, prev, name):
    rows, cols = w.shape
    half = rows // 2
    tr = _divisor_tile(half, max(8, MIB // (cols * 4)), 8)
    per = half // tr

    def body(h_ref, w_ref, g_ref, m_ref, v_ref, *refs):
        d_ref, nm_ref, nv_ref = refs[-3:]
        d, nm, nv = _adamw_math(w_ref[...], g_ref[...], m_ref[...], v_ref[...])
        d_ref[...] = d
        nm_ref[...] = nm
        nv_ref[...] = nv

    spec = pl.BlockSpec((tr, cols), lambda r, h: (h[0] * per + r, 0))
    in_specs = [spec, pl.BlockSpec((None, tr, cols), lambda r, h: (h[0], r, 0)), spec, spec]
    args = [_scalar(which), w, g2, m, v]
    aliases = {}
    if prev is not None:
        aliases = {len(args) + i: i for i in range(3)}
        in_specs += [ANY] * 3
        args += list(prev)
    return pl.pallas_call(
        body, name=name, out_shape=[jax.ShapeDtypeStruct((rows, cols), F32)] * 3,
        grid_spec=pltpu.PrefetchScalarGridSpec(num_scalar_prefetch=1, grid=(per,), in_specs=in_specs,
                                               out_specs=[spec] * 3),
        input_output_aliases=aliases, compiler_params=_params(("parallel",)))(*args)


def _sum_rows_adamw(parts, w, m, v):
    def body(p_ref, w_ref, m_ref, v_ref, g_ref, d_ref, nm_ref, nv_ref):
        g = p_ref[0]
        for i in range(1, N_DEV):
            g = g + p_ref[i]
        d, nm, nv = _adamw_math(w_ref[...], g, m_ref[...], v_ref[...])
        g_ref[...] = g
        d_ref[...] = d
        nm_ref[...] = nm
        nv_ref[...] = nv

    shape = jax.ShapeDtypeStruct(w.shape, F32)
    return pl.pallas_call(body, name="sum_rows_adamw", out_shape=[shape] * 4)(parts, w, m, v)


LANES = 128


def _permute_scratch(rows, width):
    return pltpu.VMEM((width // LANES, rows, LANES), F32)


def _split_rows(value, scratch, dil):
    if dil == 1:
        return [value]
    rows = value.shape[0] // dil
    slabs = value.shape[1] // LANES
    for c in range(slabs):
        scratch[c] = value[:, c * LANES:(c + 1) * LANES]
    return [jnp.concatenate([scratch[c, pl.ds(r, rows, stride=dil), :] for c in range(slabs)], axis=1)
            for r in range(dil)]


def _merge_rows(ref, scratch, dil):
    if dil == 1:
        return ref[0].astype(F32)
    rows = ref.shape[1]
    slabs = ref.shape[2] // LANES
    for r in range(dil):
        part = ref[r].astype(F32)
        for c in range(slabs):
            scratch[c, pl.ds(r, rows, stride=dil), :] = part[:, c * LANES:(c + 1) * LANES]
    return jnp.concatenate([scratch[c] for c in range(slabs)], axis=1)


def _grouped_view(t, dil):
    return t.reshape(dil, t.shape[0] // dil, t.shape[1])


def _grouped_spec(dil, rows, width, index):
    return pl.BlockSpec((dil, rows // dil, width), index)


W_CHUNKS = 2


def _pick(values, j):
    out = values[-1]
    for i in range(len(values) - 2, -1, -1):
        out = jnp.where(j == i, values[i], out)
    return out


def _chunk_of(col, per_chip):
    return (col % per_chip) // (per_chip // W_CHUNKS)


def _w_block(col, per_chip):
    return col // per_chip, 0, (col % per_chip) % (per_chip // W_CHUNKS)


def _in_proj(xb, wc, blocks, j0, ncols, tn, out_dtype, prev, after, name):
    s, d = xb.shape
    per_chip = wc.shape[2] * W_CHUNKS // tn
    tm = _divisor_tile(s, 1024, 16)
    extra = [t for t in (after,) if t is not None]

    def body(*refs):
        a_ref, b_ref = refs[len(extra):len(extra) + 2]
        o_ref = refs[-1]
        o_ref[...] = jnp.dot(a_ref[...], b_ref[...], preferred_element_type=F32).astype(o_ref.dtype)

    in_specs = [pl.BlockSpec(t.shape, lambda j, m: (0, 0)) for t in extra] + [
        pl.BlockSpec((tm, d), lambda j, m: (m, 0)),
        pl.BlockSpec((None, d, tn), lambda j, m: _w_block(_pick(blocks, j), per_chip))]
    args = extra + [xb, wc]
    aliases = {}
    if prev is not None:
        aliases = {len(args): 0}
        in_specs.append(ANY)
        args.append(prev)
    return pl.pallas_call(
        body, name=name, grid=(len(blocks), s // tm), in_specs=in_specs,
        out_specs=pl.BlockSpec((tm, tn), lambda j, m: (m, _pick(blocks, j) - j0)),
        out_shape=jax.ShapeDtypeStruct((s, ncols), out_dtype), input_output_aliases=aliases,
        compiler_params=_params(("parallel", "parallel")))(*args)


def _in_proj_qkv(xb, wc, g, blocks, aw, tn, prev, after, name):
    s, d = xb.shape
    dil = DILATIONS[g]
    per_chip = wc.shape[2] * W_CHUNKS // tn
    sub = aw // tn
    tm = _divisor_tile(s, 1024, 16 * dil)
    extra = [t for t in (after,) if t is not None]

    def body(*refs):
        a_ref, b_ref = refs[len(extra):len(extra) + 2]
        o_ref, scratch = refs[-2:]
        res = jnp.dot(a_ref[...], b_ref[...], preferred_element_type=F32)
        for r, part in enumerate(_split_rows(res, scratch, dil)):
            o_ref[r] = part.astype(BF16)

    def out_index(j, m):
        col = _pick(blocks, j)
        return (col // sub) // N_GROUPS, 0, m, col % sub

    in_specs = [pl.BlockSpec(t.shape, lambda j, m: (0, 0)) for t in extra] + [
        pl.BlockSpec((tm, d), lambda j, m: (m, 0)),
        pl.BlockSpec((None, d, tn), lambda j, m: _w_block(_pick(blocks, j), per_chip))]
    args = extra + [xb, wc]
    aliases = {}
    if prev is not None:
        aliases = {len(args): 0}
        in_specs.append(ANY)
        args.append(prev)
    return pl.pallas_call(
        body, name=name, grid=(len(blocks), s // tm), in_specs=in_specs,
        out_specs=pl.BlockSpec((None, dil, tm // dil, tn), out_index),
        out_shape=jax.ShapeDtypeStruct((3, dil, s // dil, aw), BF16), input_output_aliases=aliases,
        scratch_shapes=[_permute_scratch(tm, tn)],
        compiler_params=_params(("parallel", "parallel")))(*args)


def _window_mask(first):
    qi = lax.broadcasted_iota(jnp.int32, (STEPS, 2 * STEPS), 0)
    kj = lax.broadcasted_iota(jnp.int32, (STEPS, 2 * STEPS), 1)
    lowest = jnp.where(first, STEPS, 0)
    return (kj >= qi) & (kj <= qi + STEPS) & (kj >= lowest)


def _attn_fwd(qkv, g):
    _, s, aw = qkv.shape
    heads = aw // HEAD_DIM
    n_blocks = s // STEPS
    per_seq = n_blocks // DILATIONS[g]
    pair = 2 if n_blocks % 2 == 0 else 1

    def body(q_ref, kc_ref, kp_ref, vc_ref, vp_ref, o_ref, l_ref):
        masks = [_window_mask(lax.rem(pl.program_id(0) * pair + j, per_seq) == 0) for j in range(pair)]
        for h in range(heads):
            hs = slice(h * HEAD_DIM, (h + 1) * HEAD_DIM)
            keys = jnp.concatenate([kp_ref[:, hs], kc_ref[:, hs]], axis=0)
            values = jnp.concatenate([vp_ref[:, hs], vc_ref[:, hs]], axis=0)
            for j in range(pair):
                rows = slice(j * STEPS, (j + 1) * STEPS)
                window = slice(j * STEPS, (j + 2) * STEPS)
                sc = lax.dot_general(q_ref[rows, hs], keys[window], NT, preferred_element_type=F32) * SCORE_SCALE
                sc = jnp.where(masks[j], sc, NEG_INF)
                mx = jnp.max(sc, axis=1, keepdims=True)
                e = jnp.exp(sc - mx)
                den = jnp.sum(e, axis=1, keepdims=True)
                o_ref[rows, hs] = (jnp.dot(e.astype(BF16), values[window], preferred_element_type=F32)
                                   / den).astype(BF16)
                l_ref[rows, hs] = jnp.broadcast_to(mx + jnp.log(den), (STEPS, HEAD_DIM))

    def cur(which):
        return pl.BlockSpec((None, pair * STEPS, aw), lambda b: (which, b, 0))

    def prev(which):
        return pl.BlockSpec((None, STEPS, aw), lambda b: (which, jnp.maximum(pair * b - 1, 0), 0))

    out = pl.BlockSpec((pair * STEPS, aw), lambda b: (b, 0))
    return pl.pallas_call(
        body, name=f"attn_fwd{g}", grid=(n_blocks // pair,),
        in_specs=[cur(0), cur(1), prev(1), cur(2), prev(2)], out_specs=[out, out],
        out_shape=[jax.ShapeDtypeStruct((s, aw), BF16), jax.ShapeDtypeStruct((s, aw), F32)],
        compiler_params=_params(("parallel",)))(qkv, qkv, qkv, qkv, qkv)


def _combine_groups(os, ls, zuz, aw):
    s = zuz.shape[0]
    tr = _divisor_tile(s, 256, 8 * DILATIONS[-1])

    def body(*refs):
        o_refs, l_refs, z_ref = refs[0:3], refs[3:6], refs[6]
        oo_ref, y_ref, yt_ref = refs[7:10]
        lq_refs, scratch = refs[10:13], refs[13]
        ls_ = [_merge_rows(l_refs[g], scratch, dil) for g, dil in enumerate(DILATIONS)]
        mx = jnp.maximum(jnp.maximum(ls_[0], ls_[1]), ls_[2])
        ws = [jnp.exp(l - mx) for l in ls_]
        den = ws[0] + ws[1] + ws[2]
        o = ws[0] * _merge_rows(o_refs[0], scratch, DILATIONS[0])
        for g in range(1, N_GROUPS):
            o = o + ws[g] * _merge_rows(o_refs[g], scratch, DILATIONS[g])
        o = o / den
        z = z_ref[...].astype(F32)
        y = o * (z * _sigmoid(z))
        oo_ref[...] = o.astype(BF16)
        y_ref[...] = y.astype(BF16)
        yt_ref[...] = y.T.astype(BF16)
        for g, dil in enumerate(DILATIONS):
            for r, part in enumerate(_split_rows(mx + jnp.log(den), scratch, dil)):
                lq_refs[g][r] = part

    grouped = [_grouped_spec(dil, tr, aw, lambda r: (0, r, 0)) for dil in DILATIONS]
    one = pl.BlockSpec((tr, aw), lambda r: (r, 0))
    b16 = jax.ShapeDtypeStruct((s, aw), BF16)
    out = pl.pallas_call(
        body, name="combine_groups", grid=(s // tr,),
        in_specs=grouped + grouped + [one],
        out_specs=[one, one, pl.BlockSpec((aw, tr), lambda r: (0, r))] + grouped,
        out_shape=[b16, b16, jax.ShapeDtypeStruct((aw, s), BF16)]
        + [jax.ShapeDtypeStruct((dil, s // dil, aw), F32) for dil in DILATIONS],
        scratch_shapes=[_permute_scratch(tr, aw)],
        compiler_params=_params(("parallel",)))(
            *[_grouped_view(t, dil) for t, dil in zip(os, DILATIONS)],
            *[_grouped_view(t, dil) for t, dil in zip(ls, DILATIONS)], zuz)
    return out[0], out[1], out[2], [t.reshape(s, aw) for t in out[3:]]


def _pool_counts(row0, rows, window):
    t = row0 + lax.broadcasted_iota(jnp.int32, (rows, 1), 0)
    return jnp.minimum(t + 1, window).astype(F32)


def _pool_fwd(zuz, w_pool, pool_scale, aw, pw):
    s = zuz.shape[0]
    pg = pw // len(POOL_WINDOWS)
    tr = _divisor_tile(s, 256, 128)
    u_col, z_col = aw // pw, aw // pw + 1
    assert aw % pw == 0

    def body(u_ref, up_ref, z_ref, w_ref, sc_ref, p_ref, l_ref, y_ref, yt_ref):
        r = pl.program_id(0)
        u = u_ref[...].astype(F32)
        halo = jnp.where(r > 0, up_ref[...].astype(F32), 0.0)
        ext = jnp.concatenate([halo, u], axis=0)
        pieces, lins = [], []
        for gi, window in enumerate(POOL_WINDOWS):
            cs = slice(gi * pg, (gi + 1) * pg)
            acc = ext[:, cs]
            shift = 1
            while shift < window:
                acc = acc + pltpu.roll(acc, shift, 0)
                shift *= 2
            p = acc[POOL_HALO:] / _pool_counts(r * tr, tr, window) - u[:, cs]
            pieces.append(p)
            lins.append(jnp.dot(p.astype(BF16), w_ref[gi], preferred_element_type=F32))
        p = jnp.concatenate(pieces, axis=1)
        lin = jnp.concatenate(lins, axis=1)
        z = z_ref[...].astype(F32)
        y = lin * sc_ref[...] * (z * _sigmoid(z))
        p_ref[...] = p.astype(BF16)
        l_ref[...] = lin
        y_ref[...] = y.astype(BF16)
        yt_ref[...] = y.T.astype(BF16)

    per = tr // POOL_HALO
    out = pl.BlockSpec((tr, pw), lambda r: (r, 0))
    return pl.pallas_call(
        body, name="pool_fwd", grid=(s // tr,),
        in_specs=[pl.BlockSpec((tr, pw), lambda r: (r, u_col)),
                  pl.BlockSpec((POOL_HALO, pw), lambda r: (jnp.maximum(r * per - 1, 0), u_col)),
                  pl.BlockSpec((tr, pw), lambda r: (r, z_col)),
                  pl.BlockSpec((len(POOL_WINDOWS), pg, pg), lambda r: (0, 0, 0)),
                  pl.BlockSpec((1, pw), lambda r: (0, 0))],
        out_specs=[out, out, out, pl.BlockSpec((pw, tr), lambda r: (0, r))],
        out_shape=[jax.ShapeDtypeStruct((s, pw), BF16), jax.ShapeDtypeStruct((s, pw), F32),
                   jax.ShapeDtypeStruct((s, pw), BF16), jax.ShapeDtypeStruct((pw, s), BF16)],
        compiler_params=_params(("parallel",)))(zuz, zuz, zuz, w_pool, pool_scale)


def _proj_merge(y_attn, y_pool, wpa4, wpp4, gpre, b_gate):
    s, aw = y_attn.shape
    pw = y_pool.shape[1]
    tn = wpa4.shape[2]
    d = N_CHIPS * tn
    tm = _divisor_tile(s, 512, 128)

    def body(ya_ref, yp_ref, wa_ref, wp_ref, ga_ref, gp_ref, ba_ref, bp_ref, a_ref, p_ref, sa_ref, sp_ref, m_ref,
             mt_ref):
        a = jnp.dot(ya_ref[...], wa_ref[...], preferred_element_type=F32)
        p = jnp.dot(yp_ref[...], wp_ref[...], preferred_element_type=F32)
        sa = _sigmoid(ga_ref[...].astype(F32) + ba_ref[...])
        sp = _sigmoid(gp_ref[...].astype(F32) + bp_ref[...])
        merged = sa * a + sp * p
        a_ref[...] = a.astype(BF16)
        p_ref[...] = p.astype(BF16)
        sa_ref[...] = sa.astype(BF16)
        sp_ref[...] = sp.astype(BF16)
        m_ref[...] = merged.astype(BF16)
        mt_ref[...] = merged.T.astype(BF16)

    out = pl.BlockSpec((tm, tn), lambda n, m: (m, n))
    f = jax.ShapeDtypeStruct((s, d), BF16)
    return pl.pallas_call(
        body, name="proj_merge", grid=(N_CHIPS, s // tm),
        in_specs=[pl.BlockSpec((tm, aw), lambda n, m: (m, 0)), pl.BlockSpec((tm, pw), lambda n, m: (m, 0)),
                  pl.BlockSpec((None, aw, tn), lambda n, m: (n, 0, 0)),
                  pl.BlockSpec((None, pw, tn), lambda n, m: (n, 0, 0)),
                  pl.BlockSpec((tm, tn), lambda n, m: (m, n)), pl.BlockSpec((tm, tn), lambda n, m: (m, N_CHIPS + n)),
                  pl.BlockSpec((1, tn), lambda n, m: (0, n)), pl.BlockSpec((1, tn), lambda n, m: (0, N_CHIPS + n))],
        out_specs=[out] * 5 + [pl.BlockSpec((tn, tm), lambda n, m: (n, m))],
        out_shape=[f] * 5 + [jax.ShapeDtypeStruct((d, s), BF16)],
        compiler_params=_params(("parallel", "parallel")))(y_attn, y_pool, wpa4, wpp4, gpre, gpre, b_gate, b_gate)


def _out_norm_loss(merged, w_out, x, target, gamma, beta):
    s, d = x.shape
    tm = _divisor_tile(s, 256, 16)

    def body(m_ref, w_ref, x_ref, t_ref, g_ref, b_ref, dr_ref, drb_ref, loss_ref, dg_ref, db_ref):
        @pl.when(pl.program_id(0) == 0)
        def _():
            loss_ref[...] = jnp.zeros_like(loss_ref)
            dg_ref[...] = jnp.zeros_like(dg_ref)
            db_ref[...] = jnp.zeros_like(db_ref)

        r = ALPHA * x_ref[...] + jnp.dot(m_ref[...], w_ref[...], preferred_element_type=F32)
        mu = jnp.mean(r, axis=1, keepdims=True)
        rc = r - mu
        rstd = lax.rsqrt(jnp.mean(rc * rc, axis=1, keepdims=True) + LN_EPS)
        xhat = rc * rstd
        diff = xhat * g_ref[...] + b_ref[...] - t_ref[...]
        dy = diff / d
        loss_ref[...] += jnp.sum(diff * diff, axis=0, keepdims=True)
        dg_ref[...] += jnp.sum(dy * xhat, axis=0, keepdims=True)
        db_ref[...] += jnp.sum(dy, axis=0, keepdims=True)
        dxhat = dy * g_ref[...]
        dr = rstd * (dxhat - jnp.mean(dxhat, axis=1, keepdims=True)
                     - xhat * jnp.mean(dxhat * xhat, axis=1, keepdims=True))
        dr_ref[...] = dr
        drb_ref[...] = dr.astype(BF16)

    row = pl.BlockSpec((tm, d), lambda m: (m, 0))
    vec = pl.BlockSpec((1, d), lambda m: (0, 0))
    v = jax.ShapeDtypeStruct((1, d), F32)
    return pl.pallas_call(
        body, name="out_norm_loss", grid=(s // tm,),
        in_specs=[row, pl.BlockSpec((d, d), lambda m: (0, 0)), row, row, vec, vec],
        out_specs=[row, row, vec, vec, vec],
        out_shape=[jax.ShapeDtypeStruct((s, d), F32), jax.ShapeDtypeStruct((s, d), BF16), v, v, v],
        compiler_params=_params(("arbitrary",), vmem_mib=56))(merged, w_out, x, target, gamma, beta)


def _merge_bwd(drb, w_out, a, p, sa, sp):
    s, d = drb.shape
    tm = _divisor_tile(s, 512, 16)
    tn = d // N_CHIPS

    def body(dr_ref, w_ref, a_ref, p_ref, sa_ref, sp_ref, da_ref, dp_ref, dga_ref, dgp_ref, dba_ref, dbp_ref):
        @pl.when(pl.program_id(1) == 0)
        def _():
            dba_ref[...] = jnp.zeros_like(dba_ref)
            dbp_ref[...] = jnp.zeros_like(dbp_ref)

        dm = lax.dot_general(dr_ref[...], w_ref[...], NT, preferred_element_type=F32)
        sa = sa_ref[...].astype(F32)
        sp = sp_ref[...].astype(F32)
        da_ref[...] = (dm * sa).astype(BF16)
        dp_ref[...] = (dm * sp).astype(BF16)
        dga = dm * a_ref[...].astype(F32) * sa * (1.0 - sa)
        dgp = dm * p_ref[...].astype(F32) * sp * (1.0 - sp)
        dga_ref[...] = dga.astype(BF16)
        dgp_ref[...] = dgp.astype(BF16)
        dba_ref[...] += jnp.sum(dga, axis=0, keepdims=True)
        dbp_ref[...] += jnp.sum(dgp, axis=0, keepdims=True)

    blk = pl.BlockSpec((tm, tn), lambda n, m: (m, n))
    vec = pl.BlockSpec((1, tn), lambda n, m: (0, n))
    b16 = jax.ShapeDtypeStruct((s, d), BF16)
    v = jax.ShapeDtypeStruct((1, d), F32)
    return pl.pallas_call(
        body, name="merge_bwd", grid=(N_CHIPS, s // tm),
        in_specs=[pl.BlockSpec((tm, d), lambda n, m: (m, 0)), pl.BlockSpec((tn, d), lambda n, m: (n, 0)),
                  blk, blk, blk, blk],
        out_specs=[blk, blk, blk, blk, vec, vec], out_shape=[b16, b16, b16, b16, v, v],
        compiler_params=_params(("parallel", "arbitrary")))(drb, w_out, a, p, sa, sp)


def _proj_t(dy_ref, w_ref, tn):
    acc = None
    for n in range(N_CHIPS):
        t = lax.dot_general(dy_ref[:, n * tn:(n + 1) * tn], w_ref[n], NT, preferred_element_type=F32)
        acc = t if acc is None else acc + t
    return acc


def _attn_gate_bwd(da, wpa4, zuz, o):
    s, d = da.shape
    aw, tn = wpa4.shape[1], wpa4.shape[2]
    heads = aw // HEAD_DIM
    tm = _divisor_tile(s, 256, 16 * DILATIONS[-1])

    def body(*refs):
        da_ref, w_ref, z_ref, o_ref, dz_ref = refs[:5]
        do_refs, dd_refs, scratch = refs[5:8], refs[8:11], refs[11]
        dy = _proj_t(da_ref, w_ref, tn)
        z, o = z_ref[...].astype(F32), o_ref[...].astype(F32)
        sg = _sigmoid(z)
        do = dy * (z * sg)
        dz_ref[...] = (dy * o * _dsilu(z, sg)).astype(BF16)
        prod = do * o
        dd = jnp.concatenate(
            [jnp.broadcast_to(jnp.sum(prod[:, h * HEAD_DIM:(h + 1) * HEAD_DIM], axis=1, keepdims=True),
                              (tm, HEAD_DIM)) for h in range(heads)], axis=1)
        for g, dil in enumerate(DILATIONS):
            for r, part in enumerate(_split_rows(do, scratch, dil)):
                do_refs[g][r] = part.astype(BF16)
            for r, part in enumerate(_split_rows(dd, scratch, dil)):
                dd_refs[g][r] = part

    row = pl.BlockSpec((tm, aw), lambda m: (m, 0))
    grouped = [_grouped_spec(dil, tm, aw, lambda m: (0, m, 0)) for dil in DILATIONS]
    out = pl.pallas_call(
        body, name="attn_gate_bwd", grid=(s // tm,),
        in_specs=[pl.BlockSpec((tm, d), lambda m: (m, 0)), pl.BlockSpec((N_CHIPS, aw, tn), lambda m: (0, 0, 0)),
                  row, row],
        out_specs=[row] + grouped + grouped,
        out_shape=[jax.ShapeDtypeStruct((s, aw), BF16)]
        + [jax.ShapeDtypeStruct((dil, s // dil, aw), BF16) for dil in DILATIONS]
        + [jax.ShapeDtypeStruct((dil, s // dil, aw), F32) for dil in DILATIONS],
        scratch_shapes=[_permute_scratch(tm, aw)],
        compiler_params=_params(("parallel",)))(da, wpa4, zuz, o)
    return out[0], [t.reshape(s, aw) for t in out[1:4]], [t.reshape(s, aw) for t in out[4:7]]


def _pool_gate_bwd(dp_in, wpp4, zuz, lin, pooled, w_pool, pool_scale, aw):
    s, d = dp_in.shape
    pw, tn = wpp4.shape[1], wpp4.shape[2]
    n_win = len(POOL_WINDOWS)
    pg = pw // n_win
    tm = _divisor_tile(s, 256, 16)
    z_col = aw // pw + 1

    def body(dp_ref, w_ref, z_ref, l_ref, p_ref, wp_ref, sc_ref, dz_ref, dpo_ref, dw_ref, ds_ref):
        @pl.when(pl.program_id(0) == 0)
        def _():
            dw_ref[...] = jnp.zeros_like(dw_ref)
            ds_ref[...] = jnp.zeros_like(ds_ref)

        dy = _proj_t(dp_ref, w_ref, tn)
        z, lin_ = z_ref[...].astype(F32), l_ref[...]
        sg = _sigmoid(z)
        dypp = dy * (z * sg)
        dz_ref[...] = (dy * (lin_ * sc_ref[...]) * _dsilu(z, sg)).astype(BF16)
        ds_ref[...] += jnp.sum(dypp * lin_, axis=0, keepdims=True)
        dlin = (dypp * sc_ref[...]).astype(BF16)
        for gi in range(n_win):
            cs = slice(gi * pg, (gi + 1) * pg)
            dw_ref[gi] += lax.dot_general(p_ref[:, cs], dlin[:, cs], TN, preferred_element_type=F32)
            dpo_ref[:, cs] = lax.dot_general(dlin[:, cs], wp_ref[gi], NT, preferred_element_type=F32)

    row = pl.BlockSpec((tm, pw), lambda m: (m, 0))
    return pl.pallas_call(
        body, name="pool_gate_bwd", grid=(s // tm,),
        in_specs=[pl.BlockSpec((tm, d), lambda m: (m, 0)), pl.BlockSpec((N_CHIPS, pw, tn), lambda m: (0, 0, 0)),
                  pl.BlockSpec((tm, pw), lambda m: (m, z_col)), row, row,
                  pl.BlockSpec((n_win, pg, pg), lambda m: (0, 0, 0)), pl.BlockSpec((1, pw), lambda m: (0, 0))],
        out_specs=[row, row, pl.BlockSpec((n_win, pg, pg), lambda m: (0, 0, 0)),
                   pl.BlockSpec((1, pw), lambda m: (0, 0))],
        out_shape=[jax.ShapeDtypeStruct((s, pw), BF16), jax.ShapeDtypeStruct((s, pw), F32),
                   jax.ShapeDtypeStruct((n_win, pg, pg), F32), jax.ShapeDtypeStruct((1, pw), F32)],
        compiler_params=_params(("arbitrary",)))(dp_in, wpp4, zuz, lin, pooled, w_pool, pool_scale)


def _pool_bwd(dpooled):
    s, pw = dpooled.shape
    pg = pw // len(POOL_WINDOWS)
    tr = _divisor_tile(s, 256, POOL_HALO)
    per = tr // POOL_HALO
    n_tiles = s // tr

    def body(c_ref, n_ref, du_ref):
        r = pl.program_id(0)
        cur = c_ref[...]
        halo = jnp.where(r < n_tiles - 1, n_ref[...], 0.0)
        ext = jnp.concatenate([cur, halo], axis=0)
        rows = tr + POOL_HALO
        for gi, window in enumerate(POOL_WINDOWS):
            cs = slice(gi * pg, (gi + 1) * pg)
            acc = ext[:, cs] / _pool_counts(r * tr, rows, window)
            shift = 1
            while shift < window:
                acc = acc + pltpu.roll(acc, rows - shift, 0)
                shift *= 2
            du_ref[:, cs] = (acc[:tr] - cur[:, cs]).astype(BF16)

    return pl.pallas_call(
        body, name="pool_bwd", grid=(n_tiles,),
        in_specs=[pl.BlockSpec((tr, pw), lambda r: (r, 0)),
                  pl.BlockSpec((POOL_HALO, pw), lambda r: (jnp.minimum((r + 1) * per, s // POOL_HALO - 1), 0))],
        out_specs=pl.BlockSpec((tr, pw), lambda r: (r, 0)),
        out_shape=jax.ShapeDtypeStruct((s, pw), BF16), compiler_params=_params(("parallel",)))(dpooled, dpooled)


def _attn_bwd(qkv, do, lse, dd, g):
    _, s, aw = qkv.shape
    heads = aw // HEAD_DIM
    n_blocks = s // STEPS
    per_seq = n_blocks // DILATIONS[g]

    def body(q_ref, do_ref, l_ref, dd_ref, kc_ref, kp_ref, vc_ref, vp_ref, out_ref, cq_ref, ck_ref, cv_ref):
        b = pl.program_id(0)

        @pl.when(b == 0)
        def _():
            cq_ref[...] = jnp.zeros_like(cq_ref)
            ck_ref[...] = jnp.zeros_like(ck_ref)
            cv_ref[...] = jnp.zeros_like(cv_ref)

        out_ref[0] = cq_ref[...].astype(BF16)

        @pl.when(b < n_blocks)
        def _():
            mask = _window_mask(lax.rem(b, per_seq) == 0)
            for h in range(heads):
                hs = slice(h * HEAD_DIM, (h + 1) * HEAD_DIM)
                q, do_ = q_ref[:, hs], do_ref[:, hs]
                kk = jnp.concatenate([kp_ref[:, hs], kc_ref[:, hs]], axis=0)
                vv = jnp.concatenate([vp_ref[:, hs], vc_ref[:, hs]], axis=0)
                lse_ = jnp.concatenate([l_ref[:, hs], l_ref[:, hs]], axis=1)
                dd_ = jnp.concatenate([dd_ref[:, hs], dd_ref[:, hs]], axis=1)
                sc = lax.dot_general(q, kk, NT, preferred_element_type=F32) * SCORE_SCALE
                prob = jnp.where(mask, jnp.exp(sc - lse_), 0.0)
                dprob = lax.dot_general(do_, vv, NT, preferred_element_type=F32)
                dsc = prob * (dprob - dd_) * SCORE_SCALE
                cq_ref[:, hs] = jnp.dot(dsc.astype(BF16), kk, preferred_element_type=F32)
                dkk = lax.dot_general(dsc.astype(BF16), q, TN, preferred_element_type=F32)
                dvv = lax.dot_general(prob.astype(BF16), do_, TN, preferred_element_type=F32)
                out_ref[1, :, hs] = (ck_ref[:, hs] + dkk[:STEPS]).astype(BF16)
                out_ref[2, :, hs] = (cv_ref[:, hs] + dvv[:STEPS]).astype(BF16)
                ck_ref[:, hs] = dkk[STEPS:]
                cv_ref[:, hs] = dvv[STEPS:]

        @pl.when(b == n_blocks)
        def _():
            out_ref[1] = ck_ref[...].astype(BF16)
            out_ref[2] = cv_ref[...].astype(BF16)

    last = n_blocks - 1

    def cur(which):
        return pl.BlockSpec((None, STEPS, aw), lambda b: (which, jnp.minimum(b, last), 0))

    def prev(which):
        return pl.BlockSpec((None, STEPS, aw), lambda b: (which, jnp.clip(b - 1, 0, last), 0))

    row = pl.BlockSpec((STEPS, aw), lambda b: (jnp.minimum(b, last), 0))
    return pl.pallas_call(
        body, name=f"attn_bwd{g}", grid=(n_blocks + 1,),
        in_specs=[cur(0), row, row, row, cur(1), prev(1), cur(2), prev(2)],
        out_specs=pl.BlockSpec((3, STEPS, aw), lambda b: (0, jnp.clip(b - 1, 0, last), 0)),
        out_shape=jax.ShapeDtypeStruct((3, s, aw), BF16),
        scratch_shapes=[pltpu.VMEM((STEPS, aw), F32)] * 3,
        compiler_params=_params(("arbitrary",)))(qkv, do, lse, dd, qkv, qkv, qkv, qkv)


def _weight_grad(at, b, tn, col_blocks, name):
    m, k = at.shape
    n = b.shape[1]
    tm = _divisor_tile(m, 1024, 16)
    tk = _divisor_tile(k, 2048, 128)
    nk = k // tk

    def body(a_ref, b_ref, o_ref, acc_ref):
        kk = pl.program_id(2)

        @pl.when(kk == 0)
        def _():
            acc_ref[...] = jnp.zeros_like(acc_ref)

        acc_ref[...] += jnp.dot(a_ref[...], b_ref[...], preferred_element_type=F32)

        @pl.when(kk == nk - 1)
        def _():
            o_ref[...] = acc_ref[...].astype(BF16)

    if col_blocks:
        out_spec = pl.BlockSpec((None, tm, tn), lambda i, j, kk: (j, i, 0))
        out_shape = jax.ShapeDtypeStruct((n // tn, m, tn), BF16)
    else:
        out_spec = pl.BlockSpec((tm, tn), lambda i, j, kk: (i, j))
        out_shape = jax.ShapeDtypeStruct((m, n), BF16)
    return pl.pallas_call(
        body, name=name, grid=(m // tm, n // tn, nk),
        in_specs=[pl.BlockSpec((tm, tk), lambda i, j, kk: (i, kk)), pl.BlockSpec((tk, tn), lambda i, j, kk: (kk, j))],
        out_specs=out_spec, out_shape=out_shape, scratch_shapes=[pltpu.VMEM((tm, tn), F32)],
        compiler_params=_params(("parallel", "parallel", "arbitrary")))(at, b)


def _w_in_grad_part(xt, b, col_of, n_local, tn, w_shape, prev, name):
    d, s = xt.shape
    per_chip = w_shape[2] // tn
    tm = _divisor_tile(d, 1024, 16)
    tk = _divisor_tile(s, 2048, 128)
    nk = s // tk

    def body(*refs):
        a_ref, b_ref, o_ref, acc_ref = refs[0], refs[1], refs[-2], refs[-1]
        kk = pl.program_id(2)

        @pl.when(kk == 0)
        def _():
            acc_ref[...] = jnp.zeros_like(acc_ref)

        acc_ref[...] += jnp.dot(a_ref[...], b_ref[...], preferred_element_type=F32)

        @pl.when(kk == nk - 1)
        def _():
            o_ref[...] = acc_ref[...].astype(BF16)

    if b.ndim == 3:
        sub = b.shape[2] // tn
        b_spec = pl.BlockSpec((None, tk, tn), lambda j, i, kk: (j // sub, kk, j % sub))
    else:
        b_spec = pl.BlockSpec((tk, tn), lambda j, i, kk: (kk, j))
    in_specs = [pl.BlockSpec((tm, tk), lambda j, i, kk: (i, kk)), b_spec]
    args = [xt, b]
    aliases = {}
    if prev is not None:
        in_specs.append(ANY)
        args.append(prev)
        aliases = {2: 0}
    return pl.pallas_call(
        body, name=name, grid=(n_local, d // tm, nk), in_specs=in_specs,
        out_specs=pl.BlockSpec((None, tm, tn), lambda j, i, kk: (col_of(j) // per_chip, i, col_of(j) % per_chip)),
        out_shape=jax.ShapeDtypeStruct(w_shape, BF16), scratch_shapes=[pltpu.VMEM((tm, tn), F32)],
        input_output_aliases=aliases,
        compiler_params=_params(("parallel", "parallel", "arbitrary")))(*args)


def _x_grad(dqkv, rest, wc, chunk, init, init_scale, aw, tn, after=None):
    s, d = init.shape
    sub = aw // tn
    n_qkv = 3 * N_GROUPS * sub
    los, lo = [], n_qkv
    for p in rest:
        los.append(lo)
        lo += p.shape[1] // tn
    per_chip = lo // N_CHIPS
    per = per_chip // W_CHUNKS
    n_local = lo // W_CHUNKS
    tm = _divisor_tile(s, 512, 16 * DILATIONS[-1])

    def col(jl):
        return (jl // per) * per_chip + chunk * per + jl % per

    ordered = [] if after is None else [after]

    def body(*refs):
        refs = refs[len(ordered):]
        q_refs, r_refs = refs[:N_GROUPS], refs[N_GROUPS:N_GROUPS + len(rest)]
        w_ref, init_ref, o_ref, acc_ref, scratch = refs[-5:]
        jl = pl.program_id(1)
        j = col(jl)

        @pl.when(jl == 0)
        def _():
            acc_ref[...] = init_scale * init_ref[...]

        for g, dil in enumerate(DILATIONS):
            @pl.when((j < n_qkv) & (lax.rem(j // sub, N_GROUPS) == g))
            def _(g=g, dil=dil):
                rows = _merge_rows(q_refs[g], scratch, dil).astype(BF16)
                acc_ref[...] += lax.dot_general(rows, w_ref[...], NT, preferred_element_type=F32)

        for p_ref, lo_, piece in zip(r_refs, los, rest):
            @pl.when((j >= lo_) & (j < lo_ + piece.shape[1] // tn))
            def _(p_ref=p_ref):
                acc_ref[...] += lax.dot_general(p_ref[...], w_ref[...], NT, preferred_element_type=F32)

        @pl.when(jl == n_local - 1)
        def _():
            o_ref[...] = acc_ref[...]

    def qkv_spec(dil):
        def index(i, jl):
            j = col(jl)
            region = jnp.minimum(j // sub, 3 * N_GROUPS - 1)
            return region // N_GROUPS, 0, i, jnp.where(j < n_qkv, j % sub, 0)

        return pl.BlockSpec((None, dil, tm // dil, tn), index)

    def rest_spec(lo_, piece):
        n = piece.shape[1] // tn
        return pl.BlockSpec((tm, tn), lambda i, jl: (i, jnp.clip(col(jl) - lo_, 0, n - 1)))

    row = pl.BlockSpec((tm, d), lambda i, jl: (i, 0))
    return pl.pallas_call(
        body, name=f"x_grad{chunk}", grid=(s // tm, n_local),
        in_specs=[pl.BlockSpec(t.shape, lambda i, jl: (0, 0)) for t in ordered]
        + [qkv_spec(dil) for dil in DILATIONS] + [rest_spec(lo_, p) for lo_, p in zip(los, rest)]
        + [pl.BlockSpec((None, d, tn), lambda i, jl: (jl // per, 0, jl % per)), row],
        out_specs=row, out_shape=jax.ShapeDtypeStruct((s, d), F32),
        scratch_shapes=[pltpu.VMEM((tm, d), F32), _permute_scratch(tm, tn)],
        compiler_params=_params(("parallel", "arbitrary"), vmem_mib=56))(
            *ordered, *[t.reshape(3, dil, s // dil, aw) for t, dil in zip(dqkv, DILATIONS)], *rest, wc, init)


def _prepare_x(x, after=None):
    s, d = x.shape
    tc = 2 * LANES
    slabs = tc // LANES
    ordered = [] if after is None else [after]

    def body(*refs):
        x_ref, xb_ref = refs[len(ordered):len(ordered) + 2]
        xt_refs, scratch = refs[len(ordered) + 2:len(ordered) + 2 + N_GROUPS], refs[-1]
        t = x_ref[...]
        xb_ref[...] = t.astype(BF16)
        for c in range(slabs):
            scratch[c] = t[:, c * LANES:(c + 1) * LANES]
        for g, dil in enumerate(DILATIONS):
            length = s // dil
            for r in range(dil):
                part = t if dil == 1 else jnp.concatenate(
                    [scratch[c, pl.ds(r, length, stride=dil), :] for c in range(slabs)], axis=1)
                xt_refs[g][:, r * length:(r + 1) * length] = part.T.astype(BF16)

    col = pl.BlockSpec((s, tc), lambda j: (0, j))
    row = pl.BlockSpec((tc, s), lambda j: (j, 0))
    t_shape = jax.ShapeDtypeStruct((d, s), BF16)
    out = pl.pallas_call(
        body, name="prepare_x", grid=(d // tc,),
        in_specs=[pl.BlockSpec(t.shape, lambda j: (0, 0)) for t in ordered] + [col],
        out_specs=[col] + [row] * N_GROUPS,
        out_shape=[jax.ShapeDtypeStruct((s, d), BF16)] + [t_shape] * N_GROUPS,
        scratch_shapes=[_permute_scratch(s, tc)], compiler_params=_params(("parallel",)))(*ordered, x)
    return out[0], out[1:]


def _local_step(x, target, w_chunk, w_width, b_gate, pool_scale, gamma, beta, aw, pw, small_weights,
                start_exchange=None, first_token=None):
    s, d = x.shape
    tn = _col_tile(aw, pw, w_width)
    sub = aw // tn
    per_chip = w_width // tn
    qkv_w = 3 * N_GROUPS * aw
    w_shape = (N_CHIPS, d, w_width)

    regions = [dict(kind=g, blocks=[(which * N_GROUPS + g) * sub + i for which in range(3) for i in range(sub)])
               for g in range(N_GROUPS)]
    lo = qkv_w // tn
    for name, width in (("zuz", aw + 2 * pw), ("gates", 2 * d)):
        regions.append(dict(kind=name, blocks=list(range(lo, lo + width // tn)), j0=lo, width=width))
        lo += width // tn
    results = [None] * len(regions)
    xb, xts = _prepare_x(x, first_token)
    wcs, after = [], [xb]
    for ch in range(W_CHUNKS):
        wc, token = w_chunk(ch, after)
        wcs.append(wc)
        after = []
        for i, region in enumerate(regions):
            blocks = [b for b in region["blocks"] if _chunk_of(b, per_chip) == ch]
            if not blocks:
                continue
            if region["kind"] in range(N_GROUPS):
                results[i] = _in_proj_qkv(xb, wc, region["kind"], blocks, aw, tn, results[i], token,
                                          f"in_proj_qkv{region['kind']}_{ch}")
            else:
                results[i] = _in_proj(xb, wc, blocks, region["j0"], region["width"], tn, BF16, results[i], token,
                                      f"in_proj_{region['kind']}_{ch}")
            after.append(results[i])
    qkv = [results[g].reshape(3, s, aw) for g in range(N_GROUPS)]
    zuz, gpre = results[N_GROUPS], results[N_GROUPS + 1]

    attn = [_attn_fwd(qkv[g], g) for g in range(N_GROUPS)]
    o, y_attn, y_attn_t, lse = _combine_groups([a[0] for a in attn], [a[1] for a in attn], zuz, aw)
    w_pool, wpa4, wpp4, w_out = small_weights(o)
    pooled, lin, y_pool, y_pool_t = _pool_fwd(zuz, w_pool, pool_scale, aw, pw)
    a, p, sa, sp, merged, merged_t = _proj_merge(y_attn, y_pool, wpa4, wpp4, gpre, b_gate)
    dr, drb, loss_lanes, d_gamma, d_beta = _out_norm_loss(merged, w_out, x, target, gamma, beta)

    da, dp, d_gpre_a, d_gpre_p, d_b_a, d_b_p = _merge_bwd(drb, w_out, a, p, sa, sp)
    d_b_gate = jnp.concatenate([d_b_a, d_b_p], axis=1)
    d_w_out = _weight_grad(merged_t, drb, d // N_CHIPS, False, "w_out_grad")
    d_wpa4 = _weight_grad(y_attn_t, da, d // N_CHIPS, True, "w_proj_attn_grad")
    d_wpp4 = _weight_grad(y_pool_t, dp, d // N_CHIPS, True, "w_proj_pool_grad")
    d_z_attn, d_o, dd = _attn_gate_bwd(da, wpa4, zuz, o)
    d_z_pool, d_pooled, d_w_pool, d_pool_scale = _pool_gate_bwd(dp, wpp4, zuz, lin, pooled, w_pool, pool_scale, aw)
    d_u = _pool_bwd(d_pooled)
    dqkv = [_attn_bwd(qkv[g], d_o[g], lse[g], dd[g], g) for g in range(N_GROUPS)]

    rest = [d_z_attn, d_u, d_z_pool, d_gpre_a, d_gpre_p]
    d_w_in4 = None
    for g in range(N_GROUPS):
        d_w_in4 = _w_in_grad_part(xts[g], dqkv[g], lambda j, g=g: ((j // sub) * N_GROUPS + g) * sub + j % sub,
                                  3 * sub, tn, w_shape, d_w_in4, f"w_in_grad_qkv{g}")
    lo = qkv_w // tn
    for i, piece in enumerate(rest):
        n_local = piece.shape[1] // tn
        d_w_in4 = _w_in_grad_part(xts[0], piece, lambda j, lo=lo: lo + j, n_local, tn, w_shape, d_w_in4,
                                  f"w_in_grad_rest{i}")
        lo += n_local
    grads = dict(loss_lanes=loss_lanes, w_in=d_w_in4, b_gate=d_b_gate, w_pool=d_w_pool,
                 pool_scale=d_pool_scale, w_proj_attn=d_wpa4, w_proj_pool=d_wpp4, w_out=d_w_out,
                 ln_gamma=d_gamma, ln_beta=d_beta)
    token = None if start_exchange is None else start_exchange(grads)
    d_x, scale = dr, ALPHA
    for ch in range(W_CHUNKS):
        d_x = _x_grad(dqkv, rest, wcs[ch], ch, d_x, scale, aw, tn, token)
        token, scale = None, 1.0
    grads["d_x"] = d_x
    return grads


def _pack_small(wpa, wpp, w_out, w_pool):
    width = wpa.shape[1]
    return jnp.concatenate([wpa, wpp, w_out.reshape(-1, width), w_pool.reshape(-1, width)], axis=0)


def _unpack_small(packed, aw, pw, d, pg):
    lead = packed.shape[:-2]
    width = d // N_CHIPS
    r0, r1, r2 = aw, aw + pw, aw + pw + d
    return (packed[..., :r0, :], packed[..., r0:r1, :], packed[..., r1:r2, :].reshape(lead + (width, d)),
            packed[..., r2:, :].reshape(lead + (len(POOL_WINDOWS), pg // N_CHIPS, pg)))


def _pack_rows(vectors, rows):
    flat = jnp.concatenate([v.reshape(-1) for v in vectors])
    return jnp.pad(flat, (0, rows * 128 - flat.shape[0])).reshape(rows, 128)


def _unpack_rows(packed, sizes):
    flat, out, lo = packed.reshape(-1), [], 0
    for n in sizes:
        out.append(flat[lo:lo + n].reshape(1, n))
        lo += n
    return out


def kernel(x, w_in, b_gate, w_pool, pool_scale, w_proj_attn, w_proj_pool, w_out, ln_gamma, ln_beta, loss_target, m_w_in, m_b_gate, m_w_pool, m_pool_scale, m_w_proj_attn, m_w_proj_pool, m_w_out, m_ln_gamma, m_ln_beta, v_w_in, v_b_gate, v_w_pool, v_pool_scale, v_w_proj_attn, v_w_proj_pool, v_w_out, v_ln_gamma, v_ln_beta):
    s, d = x.shape[1], x.shape[2]
    aw, pw = w_proj_attn.shape[1], w_proj_pool.shape[1]
    pg = w_pool.shape[3]
    n_win = len(POOL_WINDOWS)

    def small(wpa, wpp, wo, wpl):
        return _pack_small(wpa[0], wpp[0], wo[0], wpl[0])

    chip = 2 * lax.axis_index("x") + lax.axis_index("y")
    core = lax.axis_index("c")

    w_small = small(w_proj_attn, w_proj_pool, w_out, w_pool)
    placed = [_place_block(w_in[0], N_CHIPS, chip, BF16, f"place_w_in{ch}", ch, W_CHUNKS) for ch in range(W_CHUNKS)]
    placed_small = _place_block(w_small, N_CHIPS, chip, BF16, "place_w_small")
    flight = {"chunk": _halves_start(placed[0], placed_small, "gather_w_in0_start")}
    first_token = flight["chunk"][2]

    def w_chunk(ch, after):
        sems, thru, _ = flight["chunk"]
        landed = _halves_wait(sems, thru, after, f"gather_w_in{ch}_wait")
        if ch + 1 < W_CHUNKS:
            flight["chunk"] = _halves_start(placed[ch + 1], landed, f"gather_w_in{ch + 1}_start")
            token = flight["chunk"][2]
        else:
            flight["small"] = _broadcast_start(placed_small, landed, "gather_small_start")
            token = flight["small"][2]
        return _forward_halves(landed, f"forward_w_in{ch}"), token

    def small_weights(after):
        sems, thru, _ = flight["small"]
        small4 = _broadcast_wait(sems, thru, after, "gather_small_wait")
        wpa4, wpp4, w_out4, w_pool4 = _unpack_small(small4, aw, pw, d, pg)
        return w_pool4.transpose(1, 0, 2, 3).reshape(n_win, pg, pg), wpa4, wpp4, w_out4.reshape(d, d)

    exchange = {}

    def start_exchange(g):
        g_pool4 = g["w_pool"].reshape(n_win, N_CHIPS, pg // N_CHIPS, pg).transpose(1, 0, 2, 3).astype(BF16)
        g_out4 = g["w_out"].reshape(N_CHIPS, d // N_CHIPS, d)
        g_small4 = jnp.concatenate([g["w_proj_attn"], g["w_proj_pool"], g_out4.reshape(N_CHIPS, -1, d // N_CHIPS),
                                    g_pool4.reshape(N_CHIPS, -1, d // N_CHIPS)], axis=1)
        theirs_big, theirs_small = _swap_halves([g["w_in"], g_small4])
        chip_big, placed_big = _add_halves(g["w_in"], theirs_big, core, chip, "add_cores_big")
        chip_small, placed_small = _add_halves(g_small4, theirs_small, core, chip, "add_cores_small")
        sems, sums, placed, token = _scatter_start([chip_big, chip_small], [placed_big, placed_small])
        exchange.update(sems=sems, sums=sums, placed=placed)
        return token

    g = _local_step(x[0], loss_target[0], w_chunk, w_in.shape[2], b_gate, pool_scale, ln_gamma, ln_beta, aw, pw,
                    small_weights, start_exchange, first_token)
    got_big, got_small = _scatter_wait(exchange["sems"], exchange["sums"], exchange["placed"], g["d_x"])
    join_sems, halves = _join_start([_sum_slots(got_big, core, "sum_chips_big"),
                                     _sum_slots(got_small, core, "sum_chips_small")])
    mv_small = (small(m_w_proj_attn, m_w_proj_pool, m_w_out, m_w_pool),
                small(v_w_proj_attn, v_w_proj_pool, v_w_out, v_w_pool))
    upd_in = _adamw_half(w_in[0], halves[0], m_w_in[0], v_w_in[0], core, None, "adamw_w_in_own")
    upd_small = _adamw_half(w_small, halves[1], *mv_small, core, None, "adamw_small_own")
    grad_w_in, grad_small = _join_wait(join_sems, halves, [upd_in[0], upd_small[0]])
    upd_in = _adamw_half(w_in[0], grad_w_in, m_w_in[0], v_w_in[0], 1 - core, upd_in, "adamw_w_in_other")
    upd_small = _adamw_half(w_small, grad_small, *mv_small, 1 - core, upd_small, "adamw_small_other")
    grad_w_in = grad_w_in.reshape(-1, grad_w_in.shape[2])
    grad_small = grad_small.reshape(-1, grad_small.shape[2])

    sizes = [b_gate.shape[1], pool_scale.shape[1], d, d, 1]
    rows = -(-sum(sizes) // (8 * 128)) * 8
    loss_part = (0.5 / d) * jnp.sum(g["loss_lanes"]).reshape(1, 1)
    parts = _gather_rows(_pack_rows([g["b_gate"], g["pool_scale"], g["ln_gamma"], g["ln_beta"], loss_part], rows))
    zero = jnp.zeros((1, 1), F32)
    packed = [_pack_rows(vs, rows) for vs in ([b_gate, pool_scale, ln_gamma, ln_beta, zero],
                                              [m_b_gate, m_pool_scale, m_ln_gamma, m_ln_beta, zero],
                                              [v_b_gate, v_pool_scale, v_ln_gamma, v_ln_beta, zero])]
    rep = [_unpack_rows(t, sizes) for t in _sum_rows_adamw(parts, *packed)]
    loss = rep[0][4].reshape(())

    def leaves(big, packed_small, replicated):
        wpa_, wpp_, wo_, wpl_ = _unpack_small(packed_small, aw, pw, d, pg)
        return [big[None], replicated[0], wpl_[None], replicated[1], wpa_[None], wpp_[None], wo_[None],
                replicated[2], replicated[3]]

    out = [loss, g["d_x"][None]]
    out += leaves(grad_w_in, grad_small, rep[0])
    for i in range(3):
        out += leaves(upd_in[i], upd_small[i], rep[1 + i])
    return tuple(out)
```

```python
import math

import jax
import jax.numpy as jnp
from jax import lax
from jax.experimental import pallas as pl
from jax.experimental.pallas import tpu as pltpu

F32 = jnp.float32
BF16 = jnp.bfloat16
MESH = pl.DeviceIdType.MESH
ANY = pl.BlockSpec(memory_space=pl.ANY)

HEAD_DIM = 128
STEPS = 128
DILATIONS = (1, 4, 16)
N_GROUPS = len(DILATIONS)
POOL_WINDOWS = (2, 4, 8, 16)
POOL_HALO = 16
N_CHIPS = 4
N_DEV = 8
ALPHA = 2.0 ** 0.25
LN_EPS = 1e-5
NEG_INF = -1e30
SCORE_SCALE = HEAD_DIM ** -0.5
ADAM_LR = 0.001
ADAM_B1 = 0.9
ADAM_B2 = 0.999
ADAM_EPS = 1e-08
ADAM_WD = 0.01
ADAM_STEP = 10
MIB = 2 ** 20
NT = (((1,), (1,)), ((), ()))
TN = (((0,), (0,)), ((), ()))
DMA_STREAMS = 8


def _params(semantics=None, vmem_mib=48):
    return pltpu.CompilerParams(dimension_semantics=semantics, vmem_limit_bytes=vmem_mib * MIB)


def _divisor_tile(n, target, multiple):
    best = None
    for t in range(multiple, min(n, target) + 1, multiple):
        if n % t == 0:
            best = t
    assert best is not None, (n, target, multiple)
    return best


def _col_tile(*widths):
    g = 0
    for w in widths:
        g = math.gcd(g, w)
    return _divisor_tile(g, 1024, 128)


def _sigmoid(z):
    return jax.nn.sigmoid(z)


def _dsilu(z, sg):
    return sg * (1.0 + z * (1.0 - sg))


def _place():
    x, y, c = lax.axis_index("x"), lax.axis_index("y"), lax.axis_index("c")
    others = [(1 - x, y), (x, 1 - y), (1 - x, 1 - y)]
    return x, y, c, (x, y, 1 - c), others


def _remote(src, dst, send_sem, recv_sem, dev):
    return pltpu.make_async_remote_copy(src_ref=src, dst_ref=dst, send_sem=send_sem, recv_sem=recv_sem,
                                        device_id=dev, device_id_type=MESH)


def _row_pieces(n_rows, streams=DMA_STREAMS, multiple=16):
    size = -(-n_rows // (streams * multiple)) * multiple
    return [(lo, min(size, n_rows - lo)) for lo in range(0, n_rows, size)]


def _start_streams(make, n_rows):
    for lo, size in _row_pieces(n_rows):
        make(pl.ds(lo, size)).start()


def _half_copies(buf, send_sems, recv_sems):
    x, y, c, _, others = _place()
    half = buf.shape[1] // 2
    slab = buf.at[2 * x + y, pl.ds(c * half, half)]
    return [_remote(slab, slab, send_sems[j], recv_sems[j], (ox, oy, c)) for j, (ox, oy) in enumerate(others)]


def _halves_start(placed, after, name):
    k = N_CHIPS - 1

    def body(buf, after_ref, *refs):
        send_sems, recv_sems, token = refs[:k], refs[k:2 * k], refs[-1]
        for cp in _half_copies(buf, send_sems, recv_sems):
            cp.start()
        token[...] = jnp.zeros_like(token)

    out = pl.pallas_call(
        body, name=name,
        out_shape=[pltpu.SemaphoreType.DMA(())] * (2 * k) + [pltpu.HBM(placed.shape, placed.dtype),
                                                             jax.ShapeDtypeStruct((8, 128), F32)],
        in_specs=[HBM, ANY], out_specs=[SEM] * (2 * k) + [HBM, pl.BlockSpec(memory_space=pltpu.VMEM)],
        input_output_aliases={0: 2 * k},
        compiler_params=pltpu.CompilerParams(has_side_effects=DATAFLOW),
    )(pltpu.with_memory_space_constraint(placed, pltpu.HBM), after)
    return out[:2 * k], out[2 * k], out[-1]


def _halves_wait(sems, placed, after, name):
    k = N_CHIPS - 1

    def body(buf, *refs):
        send_sems, recv_sems = refs[:k], refs[k:2 * k]
        for cp in _half_copies(buf, send_sems, recv_sems):
            cp.wait_send()
            cp.wait_recv()

    return pl.pallas_call(
        body, name=name, out_shape=pltpu.HBM(placed.shape, placed.dtype),
        in_specs=[HBM] + [SEM] * (2 * k) + [ANY] * len(after), out_specs=HBM, input_output_aliases={0: 0},
        compiler_params=pltpu.CompilerParams(has_side_effects=DATAFLOW),
    )(placed, *sems, *after)


def _forward_halves(buf, name):
    def body(_, dst, send_sems, recv_sems):
        x, y, c, sibling, others = _place()
        half = dst.shape[1] // 2
        for j, (ox, oy) in enumerate(others):
            slab = dst.at[2 * ox + oy, pl.ds(c * half, half)]
            _remote(slab, slab, send_sems.at[j], recv_sems.at[j], sibling).start()
        for j, (ox, oy) in enumerate(others):
            mine = dst.at[2 * ox + oy, pl.ds(c * half, half)]
            theirs = dst.at[2 * ox + oy, pl.ds((1 - c) * half, half)]
            cp = _remote(mine, theirs, send_sems.at[j], recv_sems.at[j], sibling)
            cp.wait_recv()
            cp.wait_send()

    return pl.pallas_call(
        body, name=name, out_shape=jax.ShapeDtypeStruct(buf.shape, buf.dtype),
        in_specs=[ANY], out_specs=ANY, input_output_aliases={0: 0},
        scratch_shapes=[pltpu.SemaphoreType.DMA((N_CHIPS - 1,)), pltpu.SemaphoreType.DMA((N_CHIPS - 1,))],
    )(buf)


def _swap_halves(grads):
    n = len(grads)

    def body(*refs):
        g, theirs = refs[:n], refs[n:2 * n]
        send_sems, recv_sems = refs[2 * n:]
        x, y, c, sibling, _ = _place()
        for i in range(n):
            half = g[i].shape[1] // 2
            give = (1 - c) * half
            for b in range(N_CHIPS):
                _start_streams(lambda r, i=i, b=b: _remote(
                    g[i].at[b, pl.ds(give + r.start, r.size)], theirs[i].at[b, r], send_sems.at[i], recv_sems.at[i],
                    sibling), half)
        for i in range(n):
            _remote(theirs[i], theirs[i], send_sems.at[i], recv_sems.at[i], sibling).wait()

    return pl.pallas_call(
        body, name="swap_halves",
        out_shape=[jax.ShapeDtypeStruct((s.shape[0], s.shape[1] // 2) + s.shape[2:], s.dtype) for s in grads],
        in_specs=[ANY] * n, out_specs=[ANY] * n,
        scratch_shapes=[pltpu.SemaphoreType.DMA((n,)), pltpu.SemaphoreType.DMA((n,))],
    )(*grads)


HBM = pl.BlockSpec(memory_space=pltpu.HBM)
SEM = pl.BlockSpec(memory_space=pltpu.SEMAPHORE)
DATAFLOW = pltpu.SideEffectType.DATAFLOW_SIDE_EFFECTING


def _broadcast_copies(buf, send_sems, recv_sems):
    x, y, c, _, others = _place()
    mine = buf.at[2 * x + y]
    return [_remote(mine, mine, send_sems[j], recv_sems[j], (ox, oy, c)) for j, (ox, oy) in enumerate(others)]


def _broadcast_start(placed, after, name):
    k = N_CHIPS - 1

    def body(buf, after_ref, *refs):
        send_sems, recv_sems, token = refs[:k], refs[k:2 * k], refs[-1]
        for cp in _broadcast_copies(buf, send_sems, recv_sems):
            cp.start()
        token[...] = jnp.zeros_like(token)

    out = pl.pallas_call(
        body, name=name,
        out_shape=[pltpu.SemaphoreType.DMA(())] * (2 * k) + [pltpu.HBM(placed.shape, placed.dtype),
                                                             jax.ShapeDtypeStruct((8, 128), F32)],
        in_specs=[HBM, ANY], out_specs=[SEM] * (2 * k) + [HBM, pl.BlockSpec(memory_space=pltpu.VMEM)],
        input_output_aliases={0: 2 * k},
        compiler_params=pltpu.CompilerParams(has_side_effects=DATAFLOW),
    )(pltpu.with_memory_space_constraint(placed, pltpu.HBM), after)
    return out[:2 * k], out[2 * k], out[-1]


def _broadcast_wait(sems, placed, after, name):
    k = N_CHIPS - 1

    def body(buf, *refs):
        send_sems, recv_sems = refs[:k], refs[k:2 * k]
        for cp in _broadcast_copies(buf, send_sems, recv_sems):
            cp.wait_send()
            cp.wait_recv()

    return pl.pallas_call(
        body, name=name, out_shape=pltpu.HBM(placed.shape, placed.dtype),
        in_specs=[HBM] + [SEM] * (2 * k) + [ANY], out_specs=HBM, input_output_aliases={0: 0},
        compiler_params=pltpu.CompilerParams(has_side_effects=DATAFLOW),
    )(placed, *sems, after)


def _scatter_copies(s, got, send_sems, recv_sems):
    x, y, c, _, others = _place()
    me = 2 * x + y
    n = len(s)
    return [_remote(s[i].at[2 * ox + oy], got[i].at[me], send_sems[3 * i + j], recv_sems[3 * i + j], (ox, oy, c))
            for i in range(n) for j, (ox, oy) in enumerate(others)]


def _scatter_start(sums, placed):
    n = len(sums)
    k = 3 * n

    def body(*refs):
        s, got, token = refs[:n], refs[n:2 * n], refs[-1]
        send_sems, recv_sems = refs[2 * n:2 * n + k], refs[2 * n + k:2 * n + 2 * k]
        for cp in _scatter_copies(s, got, send_sems, recv_sems):
            cp.start()
        token[...] = jnp.zeros_like(token)

    hbm = [pltpu.HBM(a.shape, a.dtype) for a in list(sums) + list(placed)]
    out = pl.pallas_call(
        body, name="scatter_start",
        out_shape=[pltpu.SemaphoreType.DMA(())] * (2 * k) + hbm + [jax.ShapeDtypeStruct((8, 128), F32)],
        in_specs=[HBM] * (2 * n), out_specs=[SEM] * (2 * k) + [HBM] * (2 * n) + [pl.BlockSpec(memory_space=pltpu.VMEM)],
        input_output_aliases={i: 2 * k + i for i in range(2 * n)},
        compiler_params=pltpu.CompilerParams(has_side_effects=DATAFLOW),
    )(*[pltpu.with_memory_space_constraint(a, pltpu.HBM) for a in list(sums) + list(placed)])
    return out[:2 * k], out[2 * k:2 * k + n], out[2 * k + n:2 * k + 2 * n], out[-1]


def _scatter_wait(sems, sums, placed, after):
    n = len(sums)
    k = 3 * n

    def body(*refs):
        s, got = refs[:n], refs[n:2 * n]
        send_sems, recv_sems = refs[2 * n:2 * n + k], refs[2 * n + k:2 * n + 2 * k]
        for cp in _scatter_copies(s, got, send_sems, recv_sems):
            cp.wait_send()
            cp.wait_recv()

    hbm = [pltpu.HBM(a.shape, a.dtype) for a in list(sums) + list(placed)]
    out = pl.pallas_call(
        body, name="scatter_wait", out_shape=hbm,
        in_specs=[HBM] * (2 * n) + [SEM] * (2 * k) + [ANY], out_specs=[HBM] * (2 * n),
        input_output_aliases={i: i for i in range(2 * n)},
        compiler_params=pltpu.CompilerParams(has_side_effects=DATAFLOW),
    )(*sums, *placed, *sems, after)
    return out[n:]


def _join_copies(bufs, send_sems, recv_sems):
    x, y, c, sibling, _ = _place()
    return [_remote(b.at[c], b.at[c], send_sems[i], recv_sems[i], sibling) for i, b in enumerate(bufs)]


def _join_start(placed):
    n = len(placed)

    def body(*refs):
        bufs, send_sems, recv_sems = refs[:n], refs[n:2 * n], refs[2 * n:3 * n]
        for cp in _join_copies(bufs, send_sems, recv_sems):
            cp.start()

    hbm = [pltpu.HBM(a.shape, a.dtype) for a in placed]
    out = pl.pallas_call(
        body, name="join_start", out_shape=[pltpu.SemaphoreType.DMA(())] * (2 * n) + hbm,
        in_specs=[HBM] * n, out_specs=[SEM] * (2 * n) + [HBM] * n,
        input_output_aliases={i: 2 * n + i for i in range(n)},
        compiler_params=pltpu.CompilerParams(has_side_effects=DATAFLOW),
    )(*[pltpu.with_memory_space_constraint(a, pltpu.HBM) for a in placed])
    return out[:2 * n], out[2 * n:]


def _join_wait(sems, placed, after):
    n = len(placed)

    def body(*refs):
        bufs, send_sems, recv_sems = refs[:n], refs[n:2 * n], refs[2 * n:3 * n]
        for cp in _join_copies(bufs, send_sems, recv_sems):
            cp.wait_send()
            cp.wait_recv()

    return pl.pallas_call(
        body, name="join_wait", out_shape=[pltpu.HBM(a.shape, a.dtype) for a in placed],
        in_specs=[HBM] * n + [SEM] * (2 * n) + [ANY] * len(after), out_specs=[HBM] * n,
        input_output_aliases={i: i for i in range(n)},
        compiler_params=pltpu.CompilerParams(has_side_effects=DATAFLOW),
    )(*placed, *sems, *after)


def _gather_rows(row):
    def body(row_ref, out_ref, send_sems, recv_sems, local_sem):
        x, y, c = lax.axis_index("x"), lax.axis_index("y"), lax.axis_index("c")
        me = 4 * x + 2 * y + c
        local = pltpu.make_async_copy(row_ref, out_ref.at[me], local_sem)
        local.start()
        sent = []
        peers = []
        for k in range(1, N_DEV):
            px, py, pc = x ^ (k >> 2), y ^ ((k >> 1) & 1), c ^ (k & 1)
            peers.append((k, px, py, pc))
            cp = _remote(row_ref, out_ref.at[me], send_sems.at[k - 1], recv_sems.at[k - 1], (px, py, pc))
            cp.start()
            sent.append(cp)
        for k, px, py, pc in peers:
            slot = out_ref.at[4 * px + 2 * py + pc]
            _remote(slot, slot, send_sems.at[k - 1], recv_sems.at[k - 1], (px, py, pc)).wait_recv()
        for cp in sent:
            cp.wait_send()
        local.wait()

    return pl.pallas_call(
        body, name="gather_rows", out_shape=jax.ShapeDtypeStruct((N_DEV,) + row.shape, row.dtype),
        in_specs=[ANY], out_specs=ANY,
        scratch_shapes=[pltpu.SemaphoreType.DMA((N_DEV - 1,)), pltpu.SemaphoreType.DMA((N_DEV - 1,)),
                        pltpu.SemaphoreType.DMA],
    )(row)


def _scalar(i):
    return jnp.reshape(i, (1,)).astype(jnp.int32)


def _place_block(src, n_slots, slot, out_dtype, name, window=0, n_windows=1):
    rows, cols = src.shape[0], src.shape[1] // n_windows
    tr = _divisor_tile(rows, max(16, (2 * MIB) // (cols * 4)), 16)

    def body(slot_ref, s_ref, o_ref):
        o_ref[...] = s_ref[...].astype(o_ref.dtype)

    return pl.pallas_call(
        body, name=name, out_shape=jax.ShapeDtypeStruct((n_slots, rows, cols), out_dtype),
        grid_spec=pltpu.PrefetchScalarGridSpec(
            num_scalar_prefetch=1, grid=(rows // tr,), in_specs=[pl.BlockSpec((tr, cols), lambda r, sl: (r, window))],
            out_specs=pl.BlockSpec((None, tr, cols), lambda r, sl: (sl[0], r, 0))),
        compiler_params=_params(("parallel",)))(_scalar(slot), src)


def _add_halves(g, theirs, core, chip, name):
    n, half, cols = theirs.shape
    tr = _divisor_tile(half, max(16, (2 * MIB) // (cols * 4)), 16)
    per = half // tr

    def body(at_ref, a_ref, b_ref, o_ref, own_ref):
        total = (a_ref[...].astype(F32) + b_ref[...].astype(F32)).astype(o_ref.dtype)
        o_ref[...] = total

        @pl.when(pl.program_id(1) == at_ref[1])
        def _():
            own_ref[...] = total

    spec = pl.BlockSpec((None, tr, cols), lambda r, i, at: (i, r, 0))
    shape = jax.ShapeDtypeStruct(theirs.shape, BF16)
    return pl.pallas_call(
        body, name=name, out_shape=[shape, shape],
        grid_spec=pltpu.PrefetchScalarGridSpec(
            num_scalar_prefetch=1, grid=(per, n),
            in_specs=[pl.BlockSpec((None, tr, cols), lambda r, i, at: (i, at[0] * per + r, 0)), spec],
            out_specs=[spec, pl.BlockSpec((None, tr, cols), lambda r, i, at: (at[1], r, 0))]),
        compiler_params=_params(("parallel", "arbitrary")))(jnp.concatenate([_scalar(core), _scalar(chip)]), g, theirs)


def _sum_slots(a, core, name):
    n, rows, cols = a.shape
    tr = _divisor_tile(rows, max(16, (2 * MIB) // (cols * 4 * n)), 16)

    def body(c_ref, a_ref, o_ref):
        acc = a_ref[0].astype(F32)
        for i in range(1, n):
            acc = acc + a_ref[i].astype(F32)
        o_ref[...] = acc

    return pl.pallas_call(
        body, name=name, out_shape=jax.ShapeDtypeStruct((2, rows, cols), F32),
        grid_spec=pltpu.PrefetchScalarGridSpec(
            num_scalar_prefetch=1, grid=(rows // tr,),
            in_specs=[pl.BlockSpec((n, tr, cols), lambda r, c: (0, r, 0))],
            out_specs=pl.BlockSpec((None, tr, cols), lambda r, c: (c[0], r, 0))),
        compiler_params=_params(("parallel",)))(_scalar(core), a)


def _adamw_math(w, g, m, v):
    m = ADAM_B1 * m + (1.0 - ADAM_B1) * g
    v = ADAM_B2 * v + (1.0 - ADAM_B2) * (g * g)
    m_hat = m / (1.0 - ADAM_B1 ** ADAM_STEP)
    v_hat = v / (1.0 - ADAM_B2 ** ADAM_STEP)
    delta = -ADAM_LR * (m_hat / (jnp.sqrt(v_hat) + ADAM_EPS) + ADAM_WD * w)
    return delta, m, v


def _adamw_half(w, g2, m, v, which, prev, name):
    rows, cols = w.shape
    half = rows // 2
    tr = _divisor_tile(half, max(8, MIB // (cols * 4)), 8)
    per = half // tr

    def body(h_ref, w_ref, g_ref, m_ref, v_ref, *refs):
        d_ref, nm_ref, nv_ref = refs[-3:]
        d, nm, nv = _adamw_math(w_ref[...], g_ref[...], m_ref[...], v_ref[...])
        d_ref[...] = d
        nm_ref[...] = nm
        nv_ref[...] = nv

    spec = pl.BlockSpec((tr, cols), lambda r, h: (h[0] * per + r, 0))
    in_specs = [spec, pl.BlockSpec((None, tr, cols), lambda r, h: (h[0], r, 0)), spec, spec]
    args = [_scalar(which), w, g2, m, v]
    aliases = {}
    if prev is not None:
        aliases = {len(args) + i: i for i in range(3)}
        in_specs += [ANY] * 3
        args += list(prev)
    return pl.pallas_call(
        body, name=name, out_shape=[jax.ShapeDtypeStruct((rows, cols), F32)] * 3,
        grid_spec=pltpu.PrefetchScalarGridSpec(num_scalar_prefetch=1, grid=(per,), in_specs=in_specs,
                                               out_specs=[spec] * 3),
        input_output_aliases=aliases, compiler_params=_params(("parallel",)))(*args)


def _sum_rows_adamw(parts, w, m, v):
    def body(p_ref, w_ref, m_ref, v_ref, g_ref, d_ref, nm_ref, nv_ref):
        g = p_ref[0]
        for i in range(1, N_DEV):
            g = g + p_ref[i]
        d, nm, nv = _adamw_math(w_ref[...], g, m_ref[...], v_ref[...])
        g_ref[...] = g
        d_ref[...] = d
        nm_ref[...] = nm
        nv_ref[...] = nv

    shape = jax.ShapeDtypeStruct(w.shape, F32)
    return pl.pallas_call(body, name="sum_rows_adamw", out_shape=[shape] * 4)(parts, w, m, v)


LANES = 128


def _permute_scratch(rows, width):
    return pltpu.VMEM((width // LANES, rows, LANES), F32)


def _split_rows(value, scratch, dil):
    if dil == 1:
        return [value]
    rows = value.shape[0] // dil
    slabs = value.shape[1] // LANES
    for c in range(slabs):
        scratch[c] = value[:, c * LANES:(c + 1) * LANES]
    return [jnp.concatenate([scratch[c, pl.ds(r, rows, stride=dil), :] for c in range(slabs)], axis=1)
            for r in range(dil)]


def _merge_rows(ref, scratch, dil):
    if dil == 1:
        return ref[0].astype(F32)
    rows = ref.shape[1]
    slabs = ref.shape[2] // LANES
    for r in range(dil):
        part = ref[r].astype(F32)
        for c in range(slabs):
            scratch[c, pl.ds(r, rows, stride=dil), :] = part[:, c * LANES:(c + 1) * LANES]
    return jnp.concatenate([scratch[c] for c in range(slabs)], axis=1)


def _grouped_view(t, dil):
    return t.reshape(dil, t.shape[0] // dil, t.shape[1])


def _grouped_spec(dil, rows, width, index):
    return pl.BlockSpec((dil, rows // dil, width), index)


W_CHUNKS = 4


def _pick(values, j):
    out = values[-1]
    for i in range(len(values) - 2, -1, -1):
        out = jnp.where(j == i, values[i], out)
    return out


def _chunk_of(col, per_chip):
    return (col % per_chip) // (per_chip // W_CHUNKS)


def _w_block(col, per_chip):
    return col // per_chip, 0, (col % per_chip) % (per_chip // W_CHUNKS)


def _in_proj(xb, wc, blocks, j0, ncols, tn, out_dtype, prev, after, name):
    s, d = xb.shape
    per_chip = wc.shape[2] * W_CHUNKS // tn
    tm = _divisor_tile(s, 1024, 16)
    extra = [t for t in (after,) if t is not None]

    def body(*refs):
        a_ref, b_ref = refs[len(extra):len(extra) + 2]
        o_ref = refs[-1]
        o_ref[...] = jnp.dot(a_ref[...], b_ref[...], preferred_element_type=F32).astype(o_ref.dtype)

    in_specs = [pl.BlockSpec(t.shape, lambda j, m: (0, 0)) for t in extra] + [
        pl.BlockSpec((tm, d), lambda j, m: (m, 0)),
        pl.BlockSpec((None, d, tn), lambda j, m: _w_block(_pick(blocks, j), per_chip))]
    args = extra + [xb, wc]
    aliases = {}
    if prev is not None:
        aliases = {len(args): 0}
        in_specs.append(ANY)
        args.append(prev)
    return pl.pallas_call(
        body, name=name, grid=(len(blocks), s // tm), in_specs=in_specs,
        out_specs=pl.BlockSpec((tm, tn), lambda j, m: (m, _pick(blocks, j) - j0)),
        out_shape=jax.ShapeDtypeStruct((s, ncols), out_dtype), input_output_aliases=aliases,
        compiler_params=_params(("parallel", "parallel")))(*args)


def _in_proj_qkv(xb, wc, g, blocks, aw, tn, prev, after, name):
    s, d = xb.shape
    dil = DILATIONS[g]
    per_chip = wc.shape[2] * W_CHUNKS // tn
    sub = aw // tn
    tm = _divisor_tile(s, 1024, 16 * dil)
    extra = [t for t in (after,) if t is not None]

    def body(*refs):
        a_ref, b_ref = refs[len(extra):len(extra) + 2]
        o_ref, scratch = refs[-2:]
        res = jnp.dot(a_ref[...], b_ref[...], preferred_element_type=F32)
        for r, part in enumerate(_split_rows(res, scratch, dil)):
            o_ref[r] = part.astype(BF16)

    def out_index(j, m):
        col = _pick(blocks, j)
        return (col // sub) // N_GROUPS, 0, m, col % sub

    in_specs = [pl.BlockSpec(t.shape, lambda j, m: (0, 0)) for t in extra] + [
        pl.BlockSpec((tm, d), lambda j, m: (m, 0)),
        pl.BlockSpec((None, d, tn), lambda j, m: _w_block(_pick(blocks, j), per_chip))]
    args = extra + [xb, wc]
    aliases = {}
    if prev is not None:
        aliases = {len(args): 0}
        in_specs.append(ANY)
        args.append(prev)
    return pl.pallas_call(
        body, name=name, grid=(len(blocks), s // tm), in_specs=in_specs,
        out_specs=pl.BlockSpec((None, dil, tm // dil, tn), out_index),
        out_shape=jax.ShapeDtypeStruct((3, dil, s // dil, aw), BF16), input_output_aliases=aliases,
        scratch_shapes=[_permute_scratch(tm, tn)],
        compiler_params=_params(("parallel", "parallel")))(*args)


def _window_mask(first):
    qi = lax.broadcasted_iota(jnp.int32, (STEPS, 2 * STEPS), 0)
    kj = lax.broadcasted_iota(jnp.int32, (STEPS, 2 * STEPS), 1)
    lowest = jnp.where(first, STEPS, 0)
    return (kj >= qi) & (kj <= qi + STEPS) & (kj >= lowest)


def _attn_fwd(qkv, g):
    _, s, aw = qkv.shape
    heads = aw // HEAD_DIM
    n_blocks = s // STEPS
    per_seq = n_blocks // DILATIONS[g]
    pair = 2 if n_blocks % 2 == 0 else 1

    def body(q_ref, kc_ref, kp_ref, vc_ref, vp_ref, o_ref, l_ref):
        masks = [_window_mask(lax.rem(pl.program_id(0) * pair + j, per_seq) == 0) for j in range(pair)]
        for h in range(heads):
            hs = slice(h * HEAD_DIM, (h + 1) * HEAD_DIM)
            keys = jnp.concatenate([kp_ref[:, hs], kc_ref[:, hs]], axis=0)
            values = jnp.concatenate([vp_ref[:, hs], vc_ref[:, hs]], axis=0)
            for j in range(pair):
                rows = slice(j * STEPS, (j + 1) * STEPS)
                window = slice(j * STEPS, (j + 2) * STEPS)
                sc = lax.dot_general(q_ref[rows, hs], keys[window], NT, preferred_element_type=F32) * SCORE_SCALE
                sc = jnp.where(masks[j], sc, NEG_INF)
                mx = jnp.max(sc, axis=1, keepdims=True)
                e = jnp.exp(sc - mx)
                den = jnp.sum(e, axis=1, keepdims=True)
                o_ref[rows, hs] = (jnp.dot(e.astype(BF16), values[window], preferred_element_type=F32)
                                   / den).astype(BF16)
                l_ref[rows, hs] = jnp.broadcast_to(mx + jnp.log(den), (STEPS, HEAD_DIM))

    def cur(which):
        return pl.BlockSpec((None, pair * STEPS, aw), lambda b: (which, b, 0))

    def prev(which):
        return pl.BlockSpec((None, STEPS, aw), lambda b: (which, jnp.maximum(pair * b - 1, 0), 0))

    out = pl.BlockSpec((pair * STEPS, aw), lambda b: (b, 0))
    return pl.pallas_call(
        body, name=f"attn_fwd{g}", grid=(n_blocks // pair,),
        in_specs=[cur(0), cur(1), prev(1), cur(2), prev(2)], out_specs=[out, out],
        out_shape=[jax.ShapeDtypeStruct((s, aw), BF16), jax.ShapeDtypeStruct((s, aw), F32)],
        compiler_params=_params(("parallel",)))(qkv, qkv, qkv, qkv, qkv)


def _combine_groups(os, ls, zuz, aw):
    s = zuz.shape[0]
    tr = _divisor_tile(s, 256, 8 * DILATIONS[-1])

    def body(*refs):
        o_refs, l_refs, z_ref = refs[0:3], refs[3:6], refs[6]
        oo_ref, y_ref, yt_ref = refs[7:10]
        lq_refs, scratch = refs[10:13], refs[13]
        ls_ = [_merge_rows(l_refs[g], scratch, dil) for g, dil in enumerate(DILATIONS)]
        mx = jnp.maximum(jnp.maximum(ls_[0], ls_[1]), ls_[2])
        ws = [jnp.exp(l - mx) for l in ls_]
        den = ws[0] + ws[1] + ws[2]
        o = ws[0] * _merge_rows(o_refs[0], scratch, DILATIONS[0])
        for g in range(1, N_GROUPS):
            o = o + ws[g] * _merge_rows(o_refs[g], scratch, DILATIONS[g])
        o = o / den
        z = z_ref[...].astype(F32)
        y = o * (z * _sigmoid(z))
        oo_ref[...] = o.astype(BF16)
        y_ref[...] = y.astype(BF16)
        yt_ref[...] = y.T.astype(BF16)
        for g, dil in enumerate(DILATIONS):
            for r, part in enumerate(_split_rows(mx + jnp.log(den), scratch, dil)):
                lq_refs[g][r] = part

    grouped = [_grouped_spec(dil, tr, aw, lambda r: (0, r, 0)) for dil in DILATIONS]
    one = pl.BlockSpec((tr, aw), lambda r: (r, 0))
    b16 = jax.ShapeDtypeStruct((s, aw), BF16)
    out = pl.pallas_call(
        body, name="combine_groups", grid=(s // tr,),
        in_specs=grouped + grouped + [one],
        out_specs=[one, one, pl.BlockSpec((aw, tr), lambda r: (0, r))] + grouped,
        out_shape=[b16, b16, jax.ShapeDtypeStruct((aw, s), BF16)]
        + [jax.ShapeDtypeStruct((dil, s // dil, aw), F32) for dil in DILATIONS],
        scratch_shapes=[_permute_scratch(tr, aw)],
        compiler_params=_params(("parallel",)))(
            *[_grouped_view(t, dil) for t, dil in zip(os, DILATIONS)],
            *[_grouped_view(t, dil) for t, dil in zip(ls, DILATIONS)], zuz)
    return out[0], out[1], out[2], [t.reshape(s, aw) for t in out[3:]]


def _pool_counts(row0, rows, window):
    t = row0 + lax.broadcasted_iota(jnp.int32, (rows, 1), 0)
    return jnp.minimum(t + 1, window).astype(F32)


def _pool_fwd(zuz, w_pool, pool_scale, aw, pw):
    s = zuz.shape[0]
    pg = pw // len(POOL_WINDOWS)
    tr = _divisor_tile(s, 256, 128)
    u_col, z_col = aw // pw, aw // pw + 1
    assert aw % pw == 0

    def body(u_ref, up_ref, z_ref, w_ref, sc_ref, p_ref, l_ref, y_ref, yt_ref):
        r = pl.program_id(0)
        u = u_ref[...].astype(F32)
        halo = jnp.where(r > 0, up_ref[...].astype(F32), 0.0)
        ext = jnp.concatenate([halo, u], axis=0)
        pieces, lins = [], []
        for gi, window in enumerate(POOL_WINDOWS):
            cs = slice(gi * pg, (gi + 1) * pg)
            acc = ext[:, cs]
            shift = 1
            while shift < window:
                acc = acc + pltpu.roll(acc, shift, 0)
                shift *= 2
            p = acc[POOL_HALO:] / _pool_counts(r * tr, tr, window) - u[:, cs]
            pieces.append(p)
            lins.append(jnp.dot(p.astype(BF16), w_ref[gi], preferred_element_type=F32))
        p = jnp.concatenate(pieces, axis=1)
        lin = jnp.concatenate(lins, axis=1)
        z = z_ref[...].astype(F32)
        y = lin * sc_ref[...] * (z * _sigmoid(z))
        p_ref[...] = p.astype(BF16)
        l_ref[...] = lin
        y_ref[...] = y.astype(BF16)
        yt_ref[...] = y.T.astype(BF16)

    per = tr // POOL_HALO
    out = pl.BlockSpec((tr, pw), lambda r: (r, 0))
    return pl.pallas_call(
        body, name="pool_fwd", grid=(s // tr,),
        in_specs=[pl.BlockSpec((tr, pw), lambda r: (r, u_col)),
                  pl.BlockSpec((POOL_HALO, pw), lambda r: (jnp.maximum(r * per - 1, 0), u_col)),
                  pl.BlockSpec((tr, pw), lambda r: (r, z_col)),
                  pl.BlockSpec((len(POOL_WINDOWS), pg, pg), lambda r: (0, 0, 0)),
                  pl.BlockSpec((1, pw), lambda r: (0, 0))],
        out_specs=[out, out, out, pl.BlockSpec((pw, tr), lambda r: (0, r))],
        out_shape=[jax.ShapeDtypeStruct((s, pw), BF16), jax.ShapeDtypeStruct((s, pw), F32),
                   jax.ShapeDtypeStruct((s, pw), BF16), jax.ShapeDtypeStruct((pw, s), BF16)],
        compiler_params=_params(("parallel",)))(zuz, zuz, zuz, w_pool, pool_scale)


def _proj_merge(y_attn, y_pool, wpa4, wpp4, gpre, b_gate):
    s, aw = y_attn.shape
    pw = y_pool.shape[1]
    tn = wpa4.shape[2]
    d = N_CHIPS * tn
    tm = _divisor_tile(s, 512, 128)

    def body(ya_ref, yp_ref, wa_ref, wp_ref, ga_ref, gp_ref, ba_ref, bp_ref, a_ref, p_ref, sa_ref, sp_ref, m_ref,
             mt_ref):
        a = jnp.dot(ya_ref[...], wa_ref[...], preferred_element_type=F32)
        p = jnp.dot(yp_ref[...], wp_ref[...], preferred_element_type=F32)
        sa = _sigmoid(ga_ref[...].astype(F32) + ba_ref[...])
        sp = _sigmoid(gp_ref[...].astype(F32) + bp_ref[...])
        merged = sa * a + sp * p
        a_ref[...] = a.astype(BF16)
        p_ref[...] = p.astype(BF16)
        sa_ref[...] = sa.astype(BF16)
        sp_ref[...] = sp.astype(BF16)
        m_ref[...] = merged.astype(BF16)
        mt_ref[...] = merged.T.astype(BF16)

    out = pl.BlockSpec((tm, tn), lambda n, m: (m, n))
    f = jax.ShapeDtypeStruct((s, d), BF16)
    return pl.pallas_call(
        body, name="proj_merge", grid=(N_CHIPS, s // tm),
        in_specs=[pl.BlockSpec((tm, aw), lambda n, m: (m, 0)), pl.BlockSpec((tm, pw), lambda n, m: (m, 0)),
                  pl.BlockSpec((None, aw, tn), lambda n, m: (n, 0, 0)),
                  pl.BlockSpec((None, pw, tn), lambda n, m: (n, 0, 0)),
                  pl.BlockSpec((tm, tn), lambda n, m: (m, n)), pl.BlockSpec((tm, tn), lambda n, m: (m, N_CHIPS + n)),
                  pl.BlockSpec((1, tn), lambda n, m: (0, n)), pl.BlockSpec((1, tn), lambda n, m: (0, N_CHIPS + n))],
        out_specs=[out] * 5 + [pl.BlockSpec((tn, tm), lambda n, m: (n, m))],
        out_shape=[f] * 5 + [jax.ShapeDtypeStruct((d, s), BF16)],
        compiler_params=_params(("parallel", "parallel")))(y_attn, y_pool, wpa4, wpp4, gpre, gpre, b_gate, b_gate)


def _out_norm_loss(merged, w_out, x, target, gamma, beta):
    s, d = x.shape
    tm = _divisor_tile(s, 256, 16)

    def body(m_ref, w_ref, x_ref, t_ref, g_ref, b_ref, dr_ref, drb_ref, loss_ref, dg_ref, db_ref):
        @pl.when(pl.program_id(0) == 0)
        def _():
            loss_ref[...] = jnp.zeros_like(loss_ref)
            dg_ref[...] = jnp.zeros_like(dg_ref)
            db_ref[...] = jnp.zeros_like(db_ref)

        r = ALPHA * x_ref[...] + jnp.dot(m_ref[...], w_ref[...], preferred_element_type=F32)
        mu = jnp.mean(r, axis=1, keepdims=True)
        rc = r - mu
        rstd = lax.rsqrt(jnp.mean(rc * rc, axis=1, keepdims=True) + LN_EPS)
        xhat = rc * rstd
        diff = xhat * g_ref[...] + b_ref[...] - t_ref[...]
        dy = diff / d
        loss_ref[...] += jnp.sum(diff * diff, axis=0, keepdims=True)
        dg_ref[...] += jnp.sum(dy * xhat, axis=0, keepdims=True)
        db_ref[...] += jnp.sum(dy, axis=0, keepdims=True)
        dxhat = dy * g_ref[...]
        dr = rstd * (dxhat - jnp.mean(dxhat, axis=1, keepdims=True)
                     - xhat * jnp.mean(dxhat * xhat, axis=1, keepdims=True))
        dr_ref[...] = dr
        drb_ref[...] = dr.astype(BF16)

    row = pl.BlockSpec((tm, d), lambda m: (m, 0))
    vec = pl.BlockSpec((1, d), lambda m: (0, 0))
    v = jax.ShapeDtypeStruct((1, d), F32)
    return pl.pallas_call(
        body, name="out_norm_loss", grid=(s // tm,),
        in_specs=[row, pl.BlockSpec((d, d), lambda m: (0, 0)), row, row, vec, vec],
        out_specs=[row, row, vec, vec, vec],
        out_shape=[jax.ShapeDtypeStruct((s, d), F32), jax.ShapeDtypeStruct((s, d), BF16), v, v, v],
        compiler_params=_params(("arbitrary",), vmem_mib=56))(merged, w_out, x, target, gamma, beta)


def _merge_bwd(drb, w_out, a, p, sa, sp):
    s, d = drb.shape
    tm = _divisor_tile(s, 512, 16)
    tn = d // N_CHIPS

    def body(dr_ref, w_ref, a_ref, p_ref, sa_ref, sp_ref, da_ref, dp_ref, dga_ref, dgp_ref, dba_ref, dbp_ref):
        @pl.when(pl.program_id(1) == 0)
        def _():
            dba_ref[...] = jnp.zeros_like(dba_ref)
            dbp_ref[...] = jnp.zeros_like(dbp_ref)

        dm = lax.dot_general(dr_ref[...], w_ref[...], NT, preferred_element_type=F32)
        sa = sa_ref[...].astype(F32)
        sp = sp_ref[...].astype(F32)
        da_ref[...] = (dm * sa).astype(BF16)
        dp_ref[...] = (dm * sp).astype(BF16)
        dga = dm * a_ref[...].astype(F32) * sa * (1.0 - sa)
        dgp = dm * p_ref[...].astype(F32) * sp * (1.0 - sp)
        dga_ref[...] = dga.astype(BF16)
        dgp_ref[...] = dgp.astype(BF16)
        dba_ref[...] += jnp.sum(dga, axis=0, keepdims=True)
        dbp_ref[...] += jnp.sum(dgp, axis=0, keepdims=True)

    blk = pl.BlockSpec((tm, tn), lambda n, m: (m, n))
    vec = pl.BlockSpec((1, tn), lambda n, m: (0, n))
    b16 = jax.ShapeDtypeStruct((s, d), BF16)
    v = jax.ShapeDtypeStruct((1, d), F32)
    return pl.pallas_call(
        body, name="merge_bwd", grid=(N_CHIPS, s // tm),
        in_specs=[pl.BlockSpec((tm, d), lambda n, m: (m, 0)), pl.BlockSpec((tn, d), lambda n, m: (n, 0)),
                  blk, blk, blk, blk],
        out_specs=[blk, blk, blk, blk, vec, vec], out_shape=[b16, b16, b16, b16, v, v],
        compiler_params=_params(("parallel", "arbitrary")))(drb, w_out, a, p, sa, sp)


def _proj_t(dy_ref, w_ref, tn):
    acc = None
    for n in range(N_CHIPS):
        t = lax.dot_general(dy_ref[:, n * tn:(n + 1) * tn], w_ref[n], NT, preferred_element_type=F32)
        acc = t if acc is None else acc + t
    return acc


def _attn_gate_bwd(da, wpa4, zuz, o):
    s, d = da.shape
    aw, tn = wpa4.shape[1], wpa4.shape[2]
    heads = aw // HEAD_DIM
    tm = _divisor_tile(s, 256, 16 * DILATIONS[-1])

    def body(*refs):
        da_ref, w_ref, z_ref, o_ref, dz_ref = refs[:5]
        do_refs, dd_refs, scratch = refs[5:8], refs[8:11], refs[11]
        dy = _proj_t(da_ref, w_ref, tn)
        z, o = z_ref[...].astype(F32), o_ref[...].astype(F32)
        sg = _sigmoid(z)
        do = dy * (z * sg)
        dz_ref[...] = (dy * o * _dsilu(z, sg)).astype(BF16)
        prod = do * o
        dd = jnp.concatenate(
            [jnp.broadcast_to(jnp.sum(prod[:, h * HEAD_DIM:(h + 1) * HEAD_DIM], axis=1, keepdims=True),
                              (tm, HEAD_DIM)) for h in range(heads)], axis=1)
        for g, dil in enumerate(DILATIONS):
            for r, part in enumerate(_split_rows(do, scratch, dil)):
                do_refs[g][r] = part.astype(BF16)
            for r, part in enumerate(_split_rows(dd, scratch, dil)):
                dd_refs[g][r] = part

    row = pl.BlockSpec((tm, aw), lambda m: (m, 0))
    grouped = [_grouped_spec(dil, tm, aw, lambda m: (0, m, 0)) for dil in DILATIONS]
    out = pl.pallas_call(
        body, name="attn_gate_bwd", grid=(s // tm,),
        in_specs=[pl.BlockSpec((tm, d), lambda m: (m, 0)), pl.BlockSpec((N_CHIPS, aw, tn), lambda m: (0, 0, 0)),
                  row, row],
        out_specs=[row] + grouped + grouped,
        out_shape=[jax.ShapeDtypeStruct((s, aw), BF16)]
        + [jax.ShapeDtypeStruct((dil, s // dil, aw), BF16) for dil in DILATIONS]
        + [jax.ShapeDtypeStruct((dil, s // dil, aw), F32) for dil in DILATIONS],
        scratch_shapes=[_permute_scratch(tm, aw)],
        compiler_params=_params(("parallel",)))(da, wpa4, zuz, o)
    return out[0], [t.reshape(s, aw) for t in out[1:4]], [t.reshape(s, aw) for t in out[4:7]]


def _pool_gate_bwd(dp_in, wpp4, zuz, lin, pooled, w_pool, pool_scale, aw):
    s, d = dp_in.shape
    pw, tn = wpp4.shape[1], wpp4.shape[2]
    n_win = len(POOL_WINDOWS)
    pg = pw // n_win
    tm = _divisor_tile(s, 256, 16)
    z_col = aw // pw + 1

    def body(dp_ref, w_ref, z_ref, l_ref, p_ref, wp_ref, sc_ref, dz_ref, dpo_ref, dw_ref, ds_ref):
        @pl.when(pl.program_id(0) == 0)
        def _():
            dw_ref[...] = jnp.zeros_like(dw_ref)
            ds_ref[...] = jnp.zeros_like(ds_ref)

        dy = _proj_t(dp_ref, w_ref, tn)
        z, lin_ = z_ref[...].astype(F32), l_ref[...]
        sg = _sigmoid(z)
        dypp = dy * (z * sg)
        dz_ref[...] = (dy * (lin_ * sc_ref[...]) * _dsilu(z, sg)).astype(BF16)
        ds_ref[...] += jnp.sum(dypp * lin_, axis=0, keepdims=True)
        dlin = (dypp * sc_ref[...]).astype(BF16)
        for gi in range(n_win):
            cs = slice(gi * pg, (gi + 1) * pg)
            dw_ref[gi] += lax.dot_general(p_ref[:, cs], dlin[:, cs], TN, preferred_element_type=F32)
            dpo_ref[:, cs] = lax.dot_general(dlin[:, cs], wp_ref[gi], NT, preferred_element_type=F32)

    row = pl.BlockSpec((tm, pw), lambda m: (m, 0))
    return pl.pallas_call(
        body, name="pool_gate_bwd", grid=(s // tm,),
        in_specs=[pl.BlockSpec((tm, d), lambda m: (m, 0)), pl.BlockSpec((N_CHIPS, pw, tn), lambda m: (0, 0, 0)),
                  pl.BlockSpec((tm, pw), lambda m: (m, z_col)), row, row,
                  pl.BlockSpec((n_win, pg, pg), lambda m: (0, 0, 0)), pl.BlockSpec((1, pw), lambda m: (0, 0))],
        out_specs=[row, row, pl.BlockSpec((n_win, pg, pg), lambda m: (0, 0, 0)),
                   pl.BlockSpec((1, pw), lambda m: (0, 0))],
        out_shape=[jax.ShapeDtypeStruct((s, pw), BF16), jax.ShapeDtypeStruct((s, pw), F32),
                   jax.ShapeDtypeStruct((n_win, pg, pg), F32), jax.ShapeDtypeStruct((1, pw), F32)],
        compiler_params=_params(("arbitrary",)))(dp_in, wpp4, zuz, lin, pooled, w_pool, pool_scale)


def _pool_bwd(dpooled):
    s, pw = dpooled.shape
    pg = pw // len(POOL_WINDOWS)
    tr = _divisor_tile(s, 256, POOL_HALO)
    per = tr // POOL_HALO
    n_tiles = s // tr

    def body(c_ref, n_ref, du_ref):
        r = pl.program_id(0)
        cur = c_ref[...]
        halo = jnp.where(r < n_tiles - 1, n_ref[...], 0.0)
        ext = jnp.concatenate([cur, halo], axis=0)
        rows = tr + POOL_HALO
        for gi, window in enumerate(POOL_WINDOWS):
            cs = slice(gi * pg, (gi + 1) * pg)
            acc = ext[:, cs] / _pool_counts(r * tr, rows, window)
            shift = 1
            while shift < window:
                acc = acc + pltpu.roll(acc, rows - shift, 0)
                shift *= 2
            du_ref[:, cs] = (acc[:tr] - cur[:, cs]).astype(BF16)

    return pl.pallas_call(
        body, name="pool_bwd", grid=(n_tiles,),
        in_specs=[pl.BlockSpec((tr, pw), lambda r: (r, 0)),
                  pl.BlockSpec((POOL_HALO, pw), lambda r: (jnp.minimum((r + 1) * per, s // POOL_HALO - 1), 0))],
        out_specs=pl.BlockSpec((tr, pw), lambda r: (r, 0)),
        out_shape=jax.ShapeDtypeStruct((s, pw), BF16), compiler_params=_params(("parallel",)))(dpooled, dpooled)


def _attn_bwd(qkv, do, lse, dd, g):
    _, s, aw = qkv.shape
    heads = aw // HEAD_DIM
    n_blocks = s // STEPS
    per_seq = n_blocks // DILATIONS[g]
    pair = 2 if n_blocks % 2 == 0 else 1
    rows_ = pair * STEPS
    n_steps = n_blocks // pair
    tail = slice(rows_ - STEPS, rows_)

    def body(q_ref, do_ref, l_ref, dd_ref, kc_ref, kp_ref, vc_ref, vp_ref, out_ref, cq_ref, ck_ref, cv_ref):
        b = pl.program_id(0)

        @pl.when(b == 0)
        def _():
            cq_ref[...] = jnp.zeros_like(cq_ref)
            ck_ref[...] = jnp.zeros_like(ck_ref)
            cv_ref[...] = jnp.zeros_like(cv_ref)

        out_ref[0] = cq_ref[...].astype(BF16)

        @pl.when(b < n_steps)
        def _():
            masks = [_window_mask(lax.rem(b * pair + j, per_seq) == 0) for j in range(pair)]
            for h in range(heads):
                hs = slice(h * HEAD_DIM, (h + 1) * HEAD_DIM)
                keys = jnp.concatenate([kp_ref[:, hs], kc_ref[:, hs]], axis=0)
                values = jnp.concatenate([vp_ref[:, hs], vc_ref[:, hs]], axis=0)
                dks, dvs = [], []
                for j in range(pair):
                    rows = slice(j * STEPS, (j + 1) * STEPS)
                    window = slice(j * STEPS, (j + 2) * STEPS)
                    q, do_, kk, vv = q_ref[rows, hs], do_ref[rows, hs], keys[window], values[window]
                    lse_ = jnp.concatenate([l_ref[rows, hs], l_ref[rows, hs]], axis=1)
                    dd_ = jnp.concatenate([dd_ref[rows, hs], dd_ref[rows, hs]], axis=1)
                    sc = lax.dot_general(q, kk, NT, preferred_element_type=F32) * SCORE_SCALE
                    prob = jnp.where(masks[j], jnp.exp(sc - lse_), 0.0)
                    dprob = lax.dot_general(do_, vv, NT, preferred_element_type=F32)
                    dsc = prob * (dprob - dd_) * SCORE_SCALE
                    cq_ref[rows, hs] = jnp.dot(dsc.astype(BF16), kk, preferred_element_type=F32)
                    dks.append(lax.dot_general(dsc.astype(BF16), q, TN, preferred_element_type=F32))
                    dvs.append(lax.dot_general(prob.astype(BF16), do_, TN, preferred_element_type=F32))
                for which, carry, parts in ((1, ck_ref, dks), (2, cv_ref, dvs)):
                    out_ref[which, tail, hs] = (carry[tail, hs] + parts[0][:STEPS]).astype(BF16)
                    if pair > 1:
                        out_ref[which, :rows_ - STEPS, hs] = carry[:rows_ - STEPS, hs].astype(BF16)
                    for j in range(pair):
                        total = parts[j][STEPS:]
                        if j + 1 < pair:
                            total = total + parts[j + 1][:STEPS]
                        carry[j * STEPS:(j + 1) * STEPS, hs] = total

        @pl.when(b == n_steps)
        def _():
            out_ref[1] = ck_ref[...].astype(BF16)
            out_ref[2] = cv_ref[...].astype(BF16)

    last = n_steps - 1

    def cur(which):
        return pl.BlockSpec((None, rows_, aw), lambda b: (which, jnp.minimum(b, last), 0))

    def prev(which):
        return pl.BlockSpec((None, STEPS, aw), lambda b: (which, jnp.clip(b * pair - 1, 0, n_blocks - 1), 0))

    row = pl.BlockSpec((rows_, aw), lambda b: (jnp.minimum(b, last), 0))
    return pl.pallas_call(
        body, name=f"attn_bwd{g}", grid=(n_steps + 1,),
        in_specs=[cur(0), row, row, row, cur(1), prev(1), cur(2), prev(2)],
        out_specs=pl.BlockSpec((3, rows_, aw), lambda b: (0, jnp.clip(b - 1, 0, last), 0)),
        out_shape=jax.ShapeDtypeStruct((3, s, aw), BF16),
        scratch_shapes=[pltpu.VMEM((rows_, aw), F32)] * 3,
        compiler_params=_params(("arbitrary",)))(qkv, do, lse, dd, qkv, qkv, qkv, qkv)


def _weight_grad(at, b, tn, col_blocks, name):
    m, k = at.shape
    n = b.shape[1]
    tm = _divisor_tile(m, 1024, 16)
    tk = _divisor_tile(k, 2048, 128)
    nk = k // tk

    def body(a_ref, b_ref, o_ref, acc_ref):
        kk = pl.program_id(2)

        @pl.when(kk == 0)
        def _():
            acc_ref[...] = jnp.zeros_like(acc_ref)

        acc_ref[...] += jnp.dot(a_ref[...], b_ref[...], preferred_element_type=F32)

        @pl.when(kk == nk - 1)
        def _():
            o_ref[...] = acc_ref[...].astype(BF16)

    if col_blocks:
        out_spec = pl.BlockSpec((None, tm, tn), lambda i, j, kk: (j, i, 0))
        out_shape = jax.ShapeDtypeStruct((n // tn, m, tn), BF16)
    else:
        out_spec = pl.BlockSpec((tm, tn), lambda i, j, kk: (i, j))
        out_shape = jax.ShapeDtypeStruct((m, n), BF16)
    return pl.pallas_call(
        body, name=name, grid=(m // tm, n // tn, nk),
        in_specs=[pl.BlockSpec((tm, tk), lambda i, j, kk: (i, kk)), pl.BlockSpec((tk, tn), lambda i, j, kk: (kk, j))],
        out_specs=out_spec, out_shape=out_shape, scratch_shapes=[pltpu.VMEM((tm, tn), F32)],
        compiler_params=_params(("parallel", "parallel", "arbitrary")))(at, b)


def _w_in_grad_part(xt, b, col_of, n_local, tn, w_shape, prev, name):
    d, s = xt.shape
    per_chip = w_shape[2] // tn
    tm = _divisor_tile(d, 1024, 16)
    tk = _divisor_tile(s, 2048, 128)
    nk = s // tk

    def body(*refs):
        a_ref, b_ref, o_ref, acc_ref = refs[0], refs[1], refs[-2], refs[-1]
        kk = pl.program_id(2)

        @pl.when(kk == 0)
        def _():
            acc_ref[...] = jnp.zeros_like(acc_ref)

        acc_ref[...] += jnp.dot(a_ref[...], b_ref[...], preferred_element_type=F32)

        @pl.when(kk == nk - 1)
        def _():
            o_ref[...] = acc_ref[...].astype(BF16)

    if b.ndim == 3:
        sub = b.shape[2] // tn
        b_spec = pl.BlockSpec((None, tk, tn), lambda j, i, kk: (j // sub, kk, j % sub))
    else:
        b_spec = pl.BlockSpec((tk, tn), lambda j, i, kk: (kk, j))
    in_specs = [pl.BlockSpec((tm, tk), lambda j, i, kk: (i, kk)), b_spec]
    args = [xt, b]
    aliases = {}
    if prev is not None:
        in_specs.append(ANY)
        args.append(prev)
        aliases = {2: 0}
    return pl.pallas_call(
        body, name=name, grid=(n_local, d // tm, nk), in_specs=in_specs,
        out_specs=pl.BlockSpec((None, tm, tn), lambda j, i, kk: (col_of(j) // per_chip, i, col_of(j) % per_chip)),
        out_shape=jax.ShapeDtypeStruct(w_shape, BF16), scratch_shapes=[pltpu.VMEM((tm, tn), F32)],
        input_output_aliases=aliases,
        compiler_params=_params(("parallel", "parallel", "arbitrary")))(*args)


def _assemble_w(wcs, after):
    n, d, wc = wcs[0].shape
    tr = _divisor_tile(d, 256, 16)

    def body(after_ref, *refs):
        o_ref = refs[-1]
        for ch in range(W_CHUNKS):
            o_ref[:, ch * wc:(ch + 1) * wc] = refs[ch][...]

    return pl.pallas_call(
        body, name="assemble_w", grid=(n, d // tr),
        in_specs=[pl.BlockSpec(after.shape, lambda b, r: (0, 0))]
        + [pl.BlockSpec((None, tr, wc), lambda b, r: (b, r, 0))] * W_CHUNKS,
        out_specs=pl.BlockSpec((None, tr, W_CHUNKS * wc), lambda b, r: (b, r, 0)),
        out_shape=jax.ShapeDtypeStruct((n, d, W_CHUNKS * wc), wcs[0].dtype),
        compiler_params=_params(("parallel", "parallel")))(after, *wcs)


def _x_grad(dqkv, rest, w4, dr, aw, tn):
    s, d = dr.shape
    sub = aw // tn
    n_qkv = 3 * N_GROUPS * sub
    los, lo = [], n_qkv
    for p in rest:
        los.append(lo)
        lo += p.shape[1] // tn
    n_blocks = lo
    per_chip = n_blocks // N_CHIPS
    tm = _divisor_tile(s, 512, 16 * DILATIONS[-1])

    def body(*refs):
        q_refs, r_refs = refs[:N_GROUPS], refs[N_GROUPS:N_GROUPS + len(rest)]
        w_ref, dr_ref, o_ref, acc_ref, scratch = refs[-5:]
        j = pl.program_id(1)

        @pl.when(j == 0)
        def _():
            acc_ref[...] = ALPHA * dr_ref[...]

        for g, dil in enumerate(DILATIONS):
            @pl.when((j < n_qkv) & (lax.rem(j // sub, N_GROUPS) == g))
            def _(g=g, dil=dil):
                rows = _merge_rows(q_refs[g], scratch, dil).astype(BF16)
                acc_ref[...] += lax.dot_general(rows, w_ref[...], NT, preferred_element_type=F32)

        for p_ref, lo_, piece in zip(r_refs, los, rest):
            @pl.when((j >= lo_) & (j < lo_ + piece.shape[1] // tn))
            def _(p_ref=p_ref):
                acc_ref[...] += lax.dot_general(p_ref[...], w_ref[...], NT, preferred_element_type=F32)

        @pl.when(j == n_blocks - 1)
        def _():
            o_ref[...] = acc_ref[...]

    def qkv_spec(dil):
        def index(i, j):
            region = jnp.minimum(j // sub, 3 * N_GROUPS - 1)
            return region // N_GROUPS, 0, i, jnp.where(j < n_qkv, j % sub, 0)

        return pl.BlockSpec((None, dil, tm // dil, tn), index)

    def rest_spec(lo_, piece):
        n = piece.shape[1] // tn
        return pl.BlockSpec((tm, tn), lambda i, j: (i, jnp.clip(j - lo_, 0, n - 1)))

    row = pl.BlockSpec((tm, d), lambda i, j: (i, 0))
    return pl.pallas_call(
        body, name="x_grad", grid=(s // tm, n_blocks),
        in_specs=[qkv_spec(dil) for dil in DILATIONS] + [rest_spec(lo_, p) for lo_, p in zip(los, rest)]
        + [pl.BlockSpec((None, d, tn), lambda i, j: (j // per_chip, 0, j % per_chip)), row],
        out_specs=row, out_shape=jax.ShapeDtypeStruct((s, d), F32),
        scratch_shapes=[pltpu.VMEM((tm, d), F32), _permute_scratch(tm, tn)],
        compiler_params=_params(("parallel", "arbitrary"), vmem_mib=56))(
            *[t.reshape(3, dil, s // dil, aw) for t, dil in zip(dqkv, DILATIONS)], *rest, w4, dr)


def _prepare_x(x, after=None):
    s, d = x.shape
    tc = 2 * LANES
    slabs = tc // LANES
    ordered = [] if after is None else [after]

    def body(*refs):
        x_ref, xb_ref = refs[len(ordered):len(ordered) + 2]
        xt_refs, scratch = refs[len(ordered) + 2:len(ordered) + 2 + N_GROUPS], refs[-1]
        t = x_ref[...]
        xb_ref[...] = t.astype(BF16)
        for c in range(slabs):
            scratch[c] = t[:, c * LANES:(c + 1) * LANES]
        for g, dil in enumerate(DILATIONS):
            length = s // dil
            for r in range(dil):
                part = t if dil == 1 else jnp.concatenate(
                    [scratch[c, pl.ds(r, length, stride=dil), :] for c in range(slabs)], axis=1)
                xt_refs[g][:, r * length:(r + 1) * length] = part.T.astype(BF16)

    col = pl.BlockSpec((s, tc), lambda j: (0, j))
    row = pl.BlockSpec((tc, s), lambda j: (j, 0))
    t_shape = jax.ShapeDtypeStruct((d, s), BF16)
    out = pl.pallas_call(
        body, name="prepare_x", grid=(d // tc,),
        in_specs=[pl.BlockSpec(t.shape, lambda j: (0, 0)) for t in ordered] + [col],
        out_specs=[col] + [row] * N_GROUPS,
        out_shape=[jax.ShapeDtypeStruct((s, d), BF16)] + [t_shape] * N_GROUPS,
        scratch_shapes=[_permute_scratch(s, tc)], compiler_params=_params(("parallel",)))(*ordered, x)
    return out[0], out[1:]


def _local_step(x, target, w_chunk, w_width, b_gate, pool_scale, gamma, beta, aw, pw, small_weights,
                start_exchange=None, first_token=None):
    s, d = x.shape
    tn = _col_tile(aw, pw, w_width)
    sub = aw // tn
    per_chip = w_width // tn
    qkv_w = 3 * N_GROUPS * aw
    w_shape = (N_CHIPS, d, w_width)

    regions = [dict(kind=g, blocks=[(which * N_GROUPS + g) * sub + i for which in range(3) for i in range(sub)])
               for g in range(N_GROUPS)]
    lo = qkv_w // tn
    for name, width in (("zuz", aw + 2 * pw), ("gates", 2 * d)):
        regions.append(dict(kind=name, blocks=list(range(lo, lo + width // tn)), j0=lo, width=width))
        lo += width // tn
    results = [None] * len(regions)
    xb, xts = _prepare_x(x, first_token)
    wcs, after = [], [xb]
    for ch in range(W_CHUNKS):
        wc, token = w_chunk(ch, after)
        wcs.append(wc)
        after = []
        for i, region in enumerate(regions):
            blocks = [b for b in region["blocks"] if _chunk_of(b, per_chip) == ch]
            if not blocks:
                continue
            if region["kind"] in range(N_GROUPS):
                results[i] = _in_proj_qkv(xb, wc, region["kind"], blocks, aw, tn, results[i], token,
                                          f"in_proj_qkv{region['kind']}_{ch}")
            else:
                results[i] = _in_proj(xb, wc, blocks, region["j0"], region["width"], tn, BF16, results[i], token,
                                      f"in_proj_{region['kind']}_{ch}")
            after.append(results[i])
    qkv = [results[g].reshape(3, s, aw) for g in range(N_GROUPS)]
    zuz, gpre = results[N_GROUPS], results[N_GROUPS + 1]

    attn = [_attn_fwd(qkv[g], g) for g in range(N_GROUPS)]
    o, y_attn, y_attn_t, lse = _combine_groups([a[0] for a in attn], [a[1] for a in attn], zuz, aw)
    w_pool, wpa4, wpp4, w_out = small_weights(o)
    pooled, lin, y_pool, y_pool_t = _pool_fwd(zuz, w_pool, pool_scale, aw, pw)
    a, p, sa, sp, merged, merged_t = _proj_merge(y_attn, y_pool, wpa4, wpp4, gpre, b_gate)
    dr, drb, loss_lanes, d_gamma, d_beta = _out_norm_loss(merged, w_out, x, target, gamma, beta)

    da, dp, d_gpre_a, d_gpre_p, d_b_a, d_b_p = _merge_bwd(drb, w_out, a, p, sa, sp)
    d_b_gate = jnp.concatenate([d_b_a, d_b_p], axis=1)
    d_w_out = _weight_grad(merged_t, drb, d // N_CHIPS, False, "w_out_grad")
    d_wpa4 = _weight_grad(y_attn_t, da, d // N_CHIPS, True, "w_proj_attn_grad")
    d_wpp4 = _weight_grad(y_pool_t, dp, d // N_CHIPS, True, "w_proj_pool_grad")
    d_z_attn, d_o, dd = _attn_gate_bwd(da, wpa4, zuz, o)
    d_z_pool, d_pooled, d_w_pool, d_pool_scale = _pool_gate_bwd(dp, wpp4, zuz, lin, pooled, w_pool, pool_scale, aw)
    d_u = _pool_bwd(d_pooled)
    dqkv = [_attn_bwd(qkv[g], d_o[g], lse[g], dd[g], g) for g in range(N_GROUPS)]

    rest = [d_z_attn, d_u, d_z_pool, d_gpre_a, d_gpre_p]
    d_w_in4 = None
    for g in range(N_GROUPS):
        d_w_in4 = _w_in_grad_part(xts[g], dqkv[g], lambda j, g=g: ((j // sub) * N_GROUPS + g) * sub + j % sub,
                                  3 * sub, tn, w_shape, d_w_in4, f"w_in_grad_qkv{g}")
    lo = qkv_w // tn
    for i, piece in enumerate(rest):
        n_local = piece.shape[1] // tn
        d_w_in4 = _w_in_grad_part(xts[0], piece, lambda j, lo=lo: lo + j, n_local, tn, w_shape, d_w_in4,
                                  f"w_in_grad_rest{i}")
        lo += n_local
    grads = dict(loss_lanes=loss_lanes, w_in=d_w_in4, b_gate=d_b_gate, w_pool=d_w_pool,
                 pool_scale=d_pool_scale, w_proj_attn=d_wpa4, w_proj_pool=d_wpp4, w_out=d_w_out,
                 ln_gamma=d_gamma, ln_beta=d_beta)
    token = jnp.zeros((8, 128), F32) if start_exchange is None else start_exchange(grads)
    grads["d_x"] = _x_grad(dqkv, rest, _assemble_w(wcs, token), dr, aw, tn)
    return grads


def _pack_small(wpa, wpp, w_out, w_pool):
    width = wpa.shape[1]
    return jnp.concatenate([wpa, wpp, w_out.reshape(-1, width), w_pool.reshape(-1, width)], axis=0)


def _unpack_small(packed, aw, pw, d, pg):
    lead = packed.shape[:-2]
    width = d // N_CHIPS
    r0, r1, r2 = aw, aw + pw, aw + pw + d
    return (packed[..., :r0, :], packed[..., r0:r1, :], packed[..., r1:r2, :].reshape(lead + (width, d)),
            packed[..., r2:, :].reshape(lead + (len(POOL_WINDOWS), pg // N_CHIPS, pg)))


def _pack_rows(vectors, rows):
    flat = jnp.concatenate([v.reshape(-1) for v in vectors])
    return jnp.pad(flat, (0, rows * 128 - flat.shape[0])).reshape(rows, 128)


def _unpack_rows(packed, sizes):
    flat, out, lo = packed.reshape(-1), [], 0
    for n in sizes:
        out.append(flat[lo:lo + n].reshape(1, n))
        lo += n
    return out


def kernel(x, w_in, b_gate, w_pool, pool_scale, w_proj_attn, w_proj_pool, w_out, ln_gamma, ln_beta, loss_target, m_w_in, m_b_gate, m_w_pool, m_pool_scale, m_w_proj_attn, m_w_proj_pool, m_w_out, m_ln_gamma, m_ln_beta, v_w_in, v_b_gate, v_w_pool, v_pool_scale, v_w_proj_attn, v_w_proj_pool, v_w_out, v_ln_gamma, v_ln_beta):
    s, d = x.shape[1], x.shape[2]
    aw, pw = w_proj_attn.shape[1], w_proj_pool.shape[1]
    pg = w_pool.shape[3]
    n_win = len(POOL_WINDOWS)

    def small(wpa, wpp, wo, wpl):
        return _pack_small(wpa[0], wpp[0], wo[0], wpl[0])

    chip = 2 * lax.axis_index("x") + lax.axis_index("y")
    core = lax.axis_index("c")

    w_small = small(w_proj_attn, w_proj_pool, w_out, w_pool)
    placed = [_place_block(w_in[0], N_CHIPS, chip, BF16, f"place_w_in{ch}", ch, W_CHUNKS) for ch in range(W_CHUNKS)]
    placed_small = _place_block(w_small, N_CHIPS, chip, BF16, "place_w_small")
    flight = {"chunk": _halves_start(placed[0], placed_small, "gather_w_in0_start")}
    first_token = flight["chunk"][2]

    def w_chunk(ch, after):
        sems, thru, _ = flight["chunk"]
        landed = _halves_wait(sems, thru, after, f"gather_w_in{ch}_wait")
        if ch + 1 < W_CHUNKS:
            flight["chunk"] = _halves_start(placed[ch + 1], landed, f"gather_w_in{ch + 1}_start")
            token = flight["chunk"][2]
        else:
            flight["small"] = _broadcast_start(placed_small, landed, "gather_small_start")
            token = flight["small"][2]
        return _forward_halves(landed, f"forward_w_in{ch}"), token

    def small_weights(after):
        sems, thru, _ = flight["small"]
        small4 = _broadcast_wait(sems, thru, after, "gather_small_wait")
        wpa4, wpp4, w_out4, w_pool4 = _unpack_small(small4, aw, pw, d, pg)
        return w_pool4.transpose(1, 0, 2, 3).reshape(n_win, pg, pg), wpa4, wpp4, w_out4.reshape(d, d)

    exchange = {}

    def start_exchange(g):
        g_pool4 = g["w_pool"].reshape(n_win, N_CHIPS, pg // N_CHIPS, pg).transpose(1, 0, 2, 3).astype(BF16)
        g_out4 = g["w_out"].reshape(N_CHIPS, d // N_CHIPS, d)
        g_small4 = jnp.concatenate([g["w_proj_attn"], g["w_proj_pool"], g_out4.reshape(N_CHIPS, -1, d // N_CHIPS),
                                    g_pool4.reshape(N_CHIPS, -1, d // N_CHIPS)], axis=1)
        theirs_big, theirs_small = _swap_halves([g["w_in"], g_small4])
        chip_big, placed_big = _add_halves(g["w_in"], theirs_big, core, chip, "add_cores_big")
        chip_small, placed_small = _add_halves(g_small4, theirs_small, core, chip, "add_cores_small")
        sems, sums, placed, token = _scatter_start([chip_big, chip_small], [placed_big, placed_small])
        exchange.update(sems=sems, sums=sums, placed=placed)
        return token

    g = _local_step(x[0], loss_target[0], w_chunk, w_in.shape[2], b_gate, pool_scale, ln_gamma, ln_beta, aw, pw,
                    small_weights, start_exchange, first_token)
    got_big, got_small = _scatter_wait(exchange["sems"], exchange["sums"], exchange["placed"], g["d_x"])
    join_sems, halves = _join_start([_sum_slots(got_big, core, "sum_chips_big"),
                                     _sum_slots(got_small, core, "sum_chips_small")])
    mv_small = (small(m_w_proj_attn, m_w_proj_pool, m_w_out, m_w_pool),
                small(v_w_proj_attn, v_w_proj_pool, v_w_out, v_w_pool))
    upd_in = _adamw_half(w_in[0], halves[0], m_w_in[0], v_w_in[0], core, None, "adamw_w_in_own")
    upd_small = _adamw_half(w_small, halves[1], *mv_small, core, None, "adamw_small_own")
    grad_w_in, grad_small = _join_wait(join_sems, halves, [upd_in[0], upd_small[0]])
    upd_in = _adamw_half(w_in[0], grad_w_in, m_w_in[0], v_w_in[0], 1 - core, upd_in, "adamw_w_in_other")
    upd_small = _adamw_half(w_small, grad_small, *mv_small, 1 - core, upd_small, "adamw_small_other")
    grad_w_in = grad_w_in.reshape(-1, grad_w_in.shape[2])
    grad_small = grad_small.reshape(-1, grad_small.shape[2])

    sizes = [b_gate.shape[1], pool_scale.shape[1], d, d, 1]
    rows = -(-sum(sizes) // (8 * 128)) * 8
    loss_part = (0.5 / d) * jnp.sum(g["loss_lanes"]).reshape(1, 1)
    parts = _gather_rows(_pack_rows([g["b_gate"], g["pool_scale"], g["ln_gamma"], g["ln_beta"], loss_part], rows))
    zero = jnp.zeros((1, 1), F32)
    packed = [_pack_rows(vs, rows) for vs in ([b_gate, pool_scale, ln_gamma, ln_beta, zero],
                                              [m_b_gate, m_pool_scale, m_ln_gamma, m_ln_beta, zero],
                                              [v_b_gate, v_pool_scale, v_ln_gamma, v_ln_beta, zero])]
    rep = [_unpack_rows(t, sizes) for t in _sum_rows_adamw(parts, *packed)]
    loss = rep[0][4].reshape(())

    def leaves(big, packed_small, replicated):
        wpa_, wpp_, wo_, wpl_ = _unpack_small(packed_small, aw, pw, d, pg)
        return [big[None], replicated[0], wpl_[None], replicated[1], wpa_[None], wpp_[None], wo_[None],
                replicated[2], replicated[3]]

    out = [loss, g["d_x"][None]]
    out += leaves(grad_w_in, grad_small, rep[0])
    for i in range(3):
        out += leaves(upd_in[i], upd_small[i], rep[1 + i])
    return tuple(out)
```

```python
import math

import jax
import jax.numpy as jnp
from jax import lax
from jax.experimental import pallas as pl
from jax.experimental.pallas import tpu as pltpu

F32 = jnp.float32
BF16 = jnp.bfloat16
MESH = pl.DeviceIdType.MESH
ANY = pl.BlockSpec(memory_space=pl.ANY)

HEAD_DIM = 128
STEPS = 128
DILATIONS = (1, 4, 16)
N_GROUPS = len(DILATIONS)
POOL_WINDOWS = (2, 4, 8, 16)
POOL_HALO = 16
N_CHIPS = 4
N_DEV = 8
ALPHA = 2.0 ** 0.25
LN_EPS = 1e-5
NEG_INF = -1e30
SCORE_SCALE = HEAD_DIM ** -0.5
ADAM_LR = 0.001
ADAM_B1 = 0.9
ADAM_B2 = 0.999
ADAM_EPS = 1e-08
ADAM_WD = 0.01
ADAM_STEP = 10
MIB = 2 ** 20
NT = (((1,), (1,)), ((), ()))
TN = (((0,), (0,)), ((), ()))
DMA_STREAMS = 8


def _params(semantics=None, vmem_mib=48):
    return pltpu.CompilerParams(dimension_semantics=semantics, vmem_limit_bytes=vmem_mib * MIB)


def _divisor_tile(n, target, multiple):
    best = None
    for t in range(multiple, min(n, target) + 1, multiple):
        if n % t == 0:
            best = t
    assert best is not None, (n, target, multiple)
    return best


def _col_tile(*widths):
    g = 0
    for w in widths:
        g = math.gcd(g, w)
    return _divisor_tile(g, 1024, 128)


def _sigmoid(z):
    return jax.nn.sigmoid(z)


def _dsilu(z, sg):
    return sg * (1.0 + z * (1.0 - sg))


def _place():
    x, y, c = lax.axis_index("x"), lax.axis_index("y"), lax.axis_index("c")
    others = [(1 - x, y), (x, 1 - y), (1 - x, 1 - y)]
    return x, y, c, (x, y, 1 - c), others


def _remote(src, dst, send_sem, recv_sem, dev):
    return pltpu.make_async_remote_copy(src_ref=src, dst_ref=dst, send_sem=send_sem, recv_sem=recv_sem,
                                        device_id=dev, device_id_type=MESH)


def _row_pieces(n_rows, streams=DMA_STREAMS, multiple=16):
    size = -(-n_rows // (streams * multiple)) * multiple
    return [(lo, min(size, n_rows - lo)) for lo in range(0, n_rows, size)]


def _start_streams(make, n_rows):
    for lo, size in _row_pieces(n_rows):
        make(pl.ds(lo, size)).start()


def _half_copies(buf, send_sems, recv_sems):
    x, y, c, _, others = _place()
    half = buf.shape[1] // 2
    slab = buf.at[2 * x + y, pl.ds(c * half, half)]
    return [_remote(slab, slab, send_sems[j], recv_sems[j], (ox, oy, c)) for j, (ox, oy) in enumerate(others)]


def _halves_start(placed, after, name):
    k = N_CHIPS - 1

    def body(buf, after_ref, *refs):
        send_sems, recv_sems, token = refs[:k], refs[k:2 * k], refs[-1]
        for cp in _half_copies(buf, send_sems, recv_sems):
            cp.start()
        token[...] = jnp.zeros_like(token)

    out = pl.pallas_call(
        body, name=name,
        out_shape=[pltpu.SemaphoreType.DMA(())] * (2 * k) + [pltpu.HBM(placed.shape, placed.dtype),
                                                             jax.ShapeDtypeStruct((8, 128), F32)],
        in_specs=[HBM, ANY], out_specs=[SEM] * (2 * k) + [HBM, pl.BlockSpec(memory_space=pltpu.VMEM)],
        input_output_aliases={0: 2 * k},
        compiler_params=pltpu.CompilerParams(has_side_effects=DATAFLOW),
    )(pltpu.with_memory_space_constraint(placed, pltpu.HBM), after)
    return out[:2 * k], out[2 * k], out[-1]


def _halves_wait(sems, placed, after, name):
    k = N_CHIPS - 1

    def body(buf, *refs):
        send_sems, recv_sems = refs[:k], refs[k:2 * k]
        for cp in _half_copies(buf, send_sems, recv_sems):
            cp.wait_send()
            cp.wait_recv()

    return pl.pallas_call(
        body, name=name, out_shape=pltpu.HBM(placed.shape, placed.dtype),
        in_specs=[HBM] + [SEM] * (2 * k) + [ANY] * len(after), out_specs=HBM, input_output_aliases={0: 0},
        compiler_params=pltpu.CompilerParams(has_side_effects=DATAFLOW),
    )(placed, *sems, *after)


def _forward_halves(buf, name):
    def body(_, dst, send_sems, recv_sems):
        x, y, c, sibling, others = _place()
        half = dst.shape[1] // 2
        for j, (ox, oy) in enumerate(others):
            slab = dst.at[2 * ox + oy, pl.ds(c * half, half)]
            _remote(slab, slab, send_sems.at[j], recv_sems.at[j], sibling).start()
        for j, (ox, oy) in enumerate(others):
            mine = dst.at[2 * ox + oy, pl.ds(c * half, half)]
            theirs = dst.at[2 * ox + oy, pl.ds((1 - c) * half, half)]
            cp = _remote(mine, theirs, send_sems.at[j], recv_sems.at[j], sibling)
            cp.wait_recv()
            cp.wait_send()

    return pl.pallas_call(
        body, name=name, out_shape=jax.ShapeDtypeStruct(buf.shape, buf.dtype),
        in_specs=[ANY], out_specs=ANY, input_output_aliases={0: 0},
        scratch_shapes=[pltpu.SemaphoreType.DMA((N_CHIPS - 1,)), pltpu.SemaphoreType.DMA((N_CHIPS - 1,))],
    )(buf)


def _swap_halves(grads):
    n = len(grads)

    def body(*refs):
        g, theirs = refs[:n], refs[n:2 * n]
        send_sems, recv_sems = refs[2 * n:]
        x, y, c, sibling, _ = _place()
        for i in range(n):
            half = g[i].shape[1] // 2
            give = (1 - c) * half
            for b in range(N_CHIPS):
                _start_streams(lambda r, i=i, b=b: _remote(
                    g[i].at[b, pl.ds(give + r.start, r.size)], theirs[i].at[b, r], send_sems.at[i], recv_sems.at[i],
                    sibling), half)
        for i in range(n):
            _remote(theirs[i], theirs[i], send_sems.at[i], recv_sems.at[i], sibling).wait()

    return pl.pallas_call(
        body, name="swap_halves",
        out_shape=[jax.ShapeDtypeStruct((s.shape[0], s.shape[1] // 2) + s.shape[2:], s.dtype) for s in grads],
        in_specs=[ANY] * n, out_specs=[ANY] * n,
        scratch_shapes=[pltpu.SemaphoreType.DMA((n,)), pltpu.SemaphoreType.DMA((n,))],
    )(*grads)


HBM = pl.BlockSpec(memory_space=pltpu.HBM)
SEM = pl.BlockSpec(memory_space=pltpu.SEMAPHORE)
DATAFLOW = pltpu.SideEffectType.DATAFLOW_SIDE_EFFECTING


def _broadcast_copies(buf, send_sems, recv_sems):
    x, y, c, _, others = _place()
    mine = buf.at[2 * x + y]
    return [_remote(mine, mine, send_sems[j], recv_sems[j], (ox, oy, c)) for j, (ox, oy) in enumerate(others)]


def _broadcast_start(placed, after, name):
    k = N_CHIPS - 1

    def body(buf, after_ref, *refs):
        send_sems, recv_sems, token = refs[:k], refs[k:2 * k], refs[-1]
        for cp in _broadcast_copies(buf, send_sems, recv_sems):
            cp.start()
        token[...] = jnp.zeros_like(token)

    out = pl.pallas_call(
        body, name=name,
        out_shape=[pltpu.SemaphoreType.DMA(())] * (2 * k) + [pltpu.HBM(placed.shape, placed.dtype),
                                                             jax.ShapeDtypeStruct((8, 128), F32)],
        in_specs=[HBM, ANY], out_specs=[SEM] * (2 * k) + [HBM, pl.BlockSpec(memory_space=pltpu.VMEM)],
        input_output_aliases={0: 2 * k},
        compiler_params=pltpu.CompilerParams(has_side_effects=DATAFLOW),
    )(pltpu.with_memory_space_constraint(placed, pltpu.HBM), after)
    return out[:2 * k], out[2 * k], out[-1]


def _broadcast_wait(sems, placed, after, name):
    k = N_CHIPS - 1

    def body(buf, *refs):
        send_sems, recv_sems = refs[:k], refs[k:2 * k]
        for cp in _broadcast_copies(buf, send_sems, recv_sems):
            cp.wait_send()
            cp.wait_recv()

    return pl.pallas_call(
        body, name=name, out_shape=pltpu.HBM(placed.shape, placed.dtype),
        in_specs=[HBM] + [SEM] * (2 * k) + [ANY], out_specs=HBM, input_output_aliases={0: 0},
        compiler_params=pltpu.CompilerParams(has_side_effects=DATAFLOW),
    )(placed, *sems, after)


def _scatter_copies(s, got, send_sems, recv_sems):
    x, y, c, _, others = _place()
    me = 2 * x + y
    n = len(s)
    return [_remote(s[i].at[2 * ox + oy], got[i].at[me], send_sems[3 * i + j], recv_sems[3 * i + j], (ox, oy, c))
            for i in range(n) for j, (ox, oy) in enumerate(others)]


def _scatter_start(sums, placed):
    n = len(sums)
    k = 3 * n

    def body(*refs):
        s, got, token = refs[:n], refs[n:2 * n], refs[-1]
        send_sems, recv_sems = refs[2 * n:2 * n + k], refs[2 * n + k:2 * n + 2 * k]
        for cp in _scatter_copies(s, got, send_sems, recv_sems):
            cp.start()
        token[...] = jnp.zeros_like(token)

    hbm = [pltpu.HBM(a.shape, a.dtype) for a in list(sums) + list(placed)]
    out = pl.pallas_call(
        body, name="scatter_start",
        out_shape=[pltpu.SemaphoreType.DMA(())] * (2 * k) + hbm + [jax.ShapeDtypeStruct((8, 128), F32)],
        in_specs=[HBM] * (2 * n), out_specs=[SEM] * (2 * k) + [HBM] * (2 * n) + [pl.BlockSpec(memory_space=pltpu.VMEM)],
        input_output_aliases={i: 2 * k + i for i in range(2 * n)},
        compiler_params=pltpu.CompilerParams(has_side_effects=DATAFLOW),
    )(*[pltpu.with_memory_space_constraint(a, pltpu.HBM) for a in list(sums) + list(placed)])
    return out[:2 * k], out[2 * k:2 * k + n], out[2 * k + n:2 * k + 2 * n], out[-1]


def _scatter_wait(sems, sums, placed, after):
    n = len(sums)
    k = 3 * n

    def body(*refs):
        s, got = refs[:n], refs[n:2 * n]
        send_sems, recv_sems = refs[2 * n:2 * n + k], refs[2 * n + k:2 * n + 2 * k]
        for cp in _scatter_copies(s, got, send_sems, recv_sems):
            cp.wait_send()
            cp.wait_recv()

    hbm = [pltpu.HBM(a.shape, a.dtype) for a in list(sums) + list(placed)]
    out = pl.pallas_call(
        body, name="scatter_wait", out_shape=hbm,
        in_specs=[HBM] * (2 * n) + [SEM] * (2 * k) + [ANY], out_specs=[HBM] * (2 * n),
        input_output_aliases={i: i for i in range(2 * n)},
        compiler_params=pltpu.CompilerParams(has_side_effects=DATAFLOW),
    )(*sums, *placed, *sems, after)
    return out[n:]


def _join_copies(bufs, send_sems, recv_sems):
    x, y, c, sibling, _ = _place()
    return [_remote(b.at[c], b.at[c], send_sems[i], recv_sems[i], sibling) for i, b in enumerate(bufs)]


def _join_start(placed):
    n = len(placed)

    def body(*refs):
        bufs, send_sems, recv_sems = refs[:n], refs[n:2 * n], refs[2 * n:3 * n]
        for cp in _join_copies(bufs, send_sems, recv_sems):
            cp.start()

    hbm = [pltpu.HBM(a.shape, a.dtype) for a in placed]
    out = pl.pallas_call(
        body, name="join_start", out_shape=[pltpu.SemaphoreType.DMA(())] * (2 * n) + hbm,
        in_specs=[HBM] * n, out_specs=[SEM] * (2 * n) + [HBM] * n,
        input_output_aliases={i: 2 * n + i for i in range(n)},
        compiler_params=pltpu.CompilerParams(has_side_effects=DATAFLOW),
    )(*[pltpu.with_memory_space_constraint(a, pltpu.HBM) for a in placed])
    return out[:2 * n], out[2 * n:]


def _join_wait(sems, placed, after):
    n = len(placed)

    def body(*refs):
        bufs, send_sems, recv_sems = refs[:n], refs[n:2 * n], refs[2 * n:3 * n]
        for cp in _join_copies(bufs, send_sems, recv_sems):
            cp.wait_send()
            cp.wait_recv()

    return pl.pallas_call(
        body, name="join_wait", out_shape=[pltpu.HBM(a.shape, a.dtype) for a in placed],
        in_specs=[HBM] * n + [SEM] * (2 * n) + [ANY] * len(after), out_specs=[HBM] * n,
        input_output_aliases={i: i for i in range(n)},
        compiler_params=pltpu.CompilerParams(has_side_effects=DATAFLOW),
    )(*placed, *sems, *after)


def _gather_rows(row):
    def body(row_ref, out_ref, send_sems, recv_sems, local_sem):
        x, y, c = lax.axis_index("x"), lax.axis_index("y"), lax.axis_index("c")
        me = 4 * x + 2 * y + c
        local = pltpu.make_async_copy(row_ref, out_ref.at[me], local_sem)
        local.start()
        sent = []
        peers = []
        for k in range(1, N_DEV):
            px, py, pc = x ^ (k >> 2), y ^ ((k >> 1) & 1), c ^ (k & 1)
            peers.append((k, px, py, pc))
            cp = _remote(row_ref, out_ref.at[me], send_sems.at[k - 1], recv_sems.at[k - 1], (px, py, pc))
            cp.start()
            sent.append(cp)
        for k, px, py, pc in peers:
            slot = out_ref.at[4 * px + 2 * py + pc]
            _remote(slot, slot, send_sems.at[k - 1], recv_sems.at[k - 1], (px, py, pc)).wait_recv()
        for cp in sent:
            cp.wait_send()
        local.wait()

    return pl.pallas_call(
        body, name="gather_rows", out_shape=jax.ShapeDtypeStruct((N_DEV,) + row.shape, row.dtype),
        in_specs=[ANY], out_specs=ANY,
        scratch_shapes=[pltpu.SemaphoreType.DMA((N_DEV - 1,)), pltpu.SemaphoreType.DMA((N_DEV - 1,)),
                        pltpu.SemaphoreType.DMA],
    )(row)


def _scalar(i):
    return jnp.reshape(i, (1,)).astype(jnp.int32)


def _place_block(src, n_slots, slot, out_dtype, name, window=0, n_windows=1, after=None):
    rows, cols = src.shape[0], src.shape[1] // n_windows
    tr = _divisor_tile(rows, max(16, (2 * MIB) // (cols * 4)), 16)
    ordered = [] if after is None else [after]

    def body(slot_ref, *refs):
        s_ref, o_ref = refs[len(ordered):]
        o_ref[...] = s_ref[...].astype(o_ref.dtype)

    return pl.pallas_call(
        body, name=name, out_shape=jax.ShapeDtypeStruct((n_slots, rows, cols), out_dtype),
        grid_spec=pltpu.PrefetchScalarGridSpec(
            num_scalar_prefetch=1, grid=(rows // tr,),
            in_specs=[pl.BlockSpec(t.shape, lambda r, sl: (0, 0)) for t in ordered]
            + [pl.BlockSpec((tr, cols), lambda r, sl: (r, window))],
            out_specs=pl.BlockSpec((None, tr, cols), lambda r, sl: (sl[0], r, 0))),
        compiler_params=_params(("parallel",)))(_scalar(slot), *ordered, src)


def _add_halves(g, theirs, core, chip, name):
    n, half, cols = theirs.shape
    tr = _divisor_tile(half, max(16, (2 * MIB) // (cols * 4)), 16)
    per = half // tr

    def body(at_ref, a_ref, b_ref, o_ref, own_ref):
        total = (a_ref[...].astype(F32) + b_ref[...].astype(F32)).astype(o_ref.dtype)
        o_ref[...] = total

        @pl.when(pl.program_id(1) == at_ref[1])
        def _():
            own_ref[...] = total

    spec = pl.BlockSpec((None, tr, cols), lambda r, i, at: (i, r, 0))
    shape = jax.ShapeDtypeStruct(theirs.shape, BF16)
    return pl.pallas_call(
        body, name=name, out_shape=[shape, shape],
        grid_spec=pltpu.PrefetchScalarGridSpec(
            num_scalar_prefetch=1, grid=(per, n),
            in_specs=[pl.BlockSpec((None, tr, cols), lambda r, i, at: (i, at[0] * per + r, 0)), spec],
            out_specs=[spec, pl.BlockSpec((None, tr, cols), lambda r, i, at: (at[1], r, 0))]),
        compiler_params=_params(("parallel", "arbitrary")))(jnp.concatenate([_scalar(core), _scalar(chip)]), g, theirs)


def _sum_slots(a, core, name):
    n, rows, cols = a.shape
    tr = _divisor_tile(rows, max(16, (2 * MIB) // (cols * 4 * n)), 16)

    def body(c_ref, a_ref, o_ref):
        acc = a_ref[0].astype(F32)
        for i in range(1, n):
            acc = acc + a_ref[i].astype(F32)
        o_ref[...] = acc

    return pl.pallas_call(
        body, name=name, out_shape=jax.ShapeDtypeStruct((2, rows, cols), F32),
        grid_spec=pltpu.PrefetchScalarGridSpec(
            num_scalar_prefetch=1, grid=(rows // tr,),
            in_specs=[pl.BlockSpec((n, tr, cols), lambda r, c: (0, r, 0))],
            out_specs=pl.BlockSpec((None, tr, cols), lambda r, c: (c[0], r, 0))),
        compiler_params=_params(("parallel",)))(_scalar(core), a)


def _adamw_math(w, g, m, v):
    m = ADAM_B1 * m + (1.0 - ADAM_B1) * g
    v = ADAM_B2 * v + (1.0 - ADAM_B2) * (g * g)
    m_hat = m / (1.0 - ADAM_B1 ** ADAM_STEP)
    v_hat = v / (1.0 - ADAM_B2 ** ADAM_STEP)
    delta = -ADAM_LR * (m_hat / (jnp.sqrt(v_hat) + ADAM_EPS) + ADAM_WD * w)
    return delta, m, v


def _adamw_half(w, g2, m, v, which, prev, name):
    rows, cols = w.shape
    half = rows // 2
    tr = _divisor_tile(half, max(8, MIB // (cols * 4)), 8)
    per = half // tr

    def body(h_ref, w_ref, g_ref, m_ref, v_ref, *refs):
        d_ref, nm_ref, nv_ref = refs[-3:]
        d, nm, nv = _adamw_math(w_ref[...], g_ref[...], m_ref[...], v_ref[...])
        d_ref[...] = d
        nm_ref[...] = nm
        nv_ref[...] = nv

    spec = pl.BlockSpec((tr, cols), lambda r, h: (h[0] * per + r, 0))
    in_specs = [spec, pl.BlockSpec((None, tr, cols), lambda r, h: (h[0], r, 0)), spec, spec]
    args = [_scalar(which), w, g2, m, v]
    aliases = {}
    if prev is not None:
        aliases = {len(args) + i: i for i in range(3)}
        in_specs += [ANY] * 3
        args += list(prev)
    return pl.pallas_call(
        body, name=name, out_shape=[jax.ShapeDtypeStruct((rows, cols), F32)] * 3,
        grid_spec=pltpu.PrefetchScalarGridSpec(num_scalar_prefetch=1, grid=(per,), in_specs=in_specs,
                                               out_specs=[spec] * 3),
        input_output_aliases=aliases, compiler_params=_params(("parallel",)))(*args)


def _sum_rows_adamw(parts, w, m, v):
    def body(p_ref, w_ref, m_ref, v_ref, g_ref, d_ref, nm_ref, nv_ref):
        g = p_ref[0]
        for i in range(1, N_DEV):
            g = g + p_ref[i]
        d, nm, nv = _adamw_math(w_ref[...], g, m_ref[...], v_ref[...])
        g_ref[...] = g
        d_ref[...] = d
        nm_ref[...] = nm
        nv_ref[...] = nv

    shape = jax.ShapeDtypeStruct(w.shape, F32)
    return pl.pallas_call(body, name="sum_rows_adamw", out_shape=[shape] * 4)(parts, w, m, v)


LANES = 128


def _permute_scratch(rows, width):
    return pltpu.VMEM((width // LANES, rows, LANES), F32)


def _split_rows(value, scratch, dil):
    if dil == 1:
        return [value]
    rows = value.shape[0] // dil
    slabs = value.shape[1] // LANES
    for c in range(slabs):
        scratch[c] = value[:, c * LANES:(c + 1) * LANES]
    return [jnp.concatenate([scratch[c, pl.ds(r, rows, stride=dil), :] for c in range(slabs)], axis=1)
            for r in range(dil)]


def _merge_rows(ref, scratch, dil):
    if dil == 1:
        return ref[0].astype(F32)
    rows = ref.shape[1]
    slabs = ref.shape[2] // LANES
    for r in range(dil):
        part = ref[r].astype(F32)
        for c in range(slabs):
            scratch[c, pl.ds(r, rows, stride=dil), :] = part[:, c * LANES:(c + 1) * LANES]
    return jnp.concatenate([scratch[c] for c in range(slabs)], axis=1)


def _grouped_view(t, dil):
    return t.reshape(dil, t.shape[0] // dil, t.shape[1])


def _grouped_spec(dil, rows, width, index):
    return pl.BlockSpec((dil, rows // dil, width), index)


W_CHUNKS = 4


def _pick(values, j):
    out = values[-1]
    for i in range(len(values) - 2, -1, -1):
        out = jnp.where(j == i, values[i], out)
    return out


def _chunk_of(col, per_chip):
    return (col % per_chip) // (per_chip // W_CHUNKS)


def _w_block(col, per_chip):
    return col // per_chip, 0, (col % per_chip) % (per_chip // W_CHUNKS)


def _in_proj(xb, wc, blocks, j0, ncols, tn, out_dtype, prev, after, name):
    s, d = xb.shape
    per_chip = wc.shape[2] * W_CHUNKS // tn
    tm = _divisor_tile(s, 1024, 16)
    extra = [t for t in (after,) if t is not None]

    def body(*refs):
        a_ref, b_ref = refs[len(extra):len(extra) + 2]
        o_ref = refs[-1]
        o_ref[...] = jnp.dot(a_ref[...], b_ref[...], preferred_element_type=F32).astype(o_ref.dtype)

    in_specs = [pl.BlockSpec(t.shape, lambda j, m: (0, 0)) for t in extra] + [
        pl.BlockSpec((tm, d), lambda j, m: (m, 0)),
        pl.BlockSpec((None, d, tn), lambda j, m: _w_block(_pick(blocks, j), per_chip))]
    args = extra + [xb, wc]
    aliases = {}
    if prev is not None:
        aliases = {len(args): 0}
        in_specs.append(ANY)
        args.append(prev)
    return pl.pallas_call(
        body, name=name, grid=(len(blocks), s // tm), in_specs=in_specs,
        out_specs=pl.BlockSpec((tm, tn), lambda j, m: (m, _pick(blocks, j) - j0)),
        out_shape=jax.ShapeDtypeStruct((s, ncols), out_dtype), input_output_aliases=aliases,
        compiler_params=_params(("parallel", "parallel")))(*args)


def _in_proj_qkv(xb, wc, g, blocks, aw, tn, prev, after, name):
    s, d = xb.shape
    dil = DILATIONS[g]
    per_chip = wc.shape[2] * W_CHUNKS // tn
    sub = aw // tn
    tm = _divisor_tile(s, 1024, 16 * dil)
    extra = [t for t in (after,) if t is not None]

    def body(*refs):
        a_ref, b_ref = refs[len(extra):len(extra) + 2]
        o_ref, scratch = refs[-2:]
        res = jnp.dot(a_ref[...], b_ref[...], preferred_element_type=F32)
        for r, part in enumerate(_split_rows(res, scratch, dil)):
            o_ref[r] = part.astype(BF16)

    def out_index(j, m):
        col = _pick(blocks, j)
        return (col // sub) // N_GROUPS, 0, m, col % sub

    in_specs = [pl.BlockSpec(t.shape, lambda j, m: (0, 0)) for t in extra] + [
        pl.BlockSpec((tm, d), lambda j, m: (m, 0)),
        pl.BlockSpec((None, d, tn), lambda j, m: _w_block(_pick(blocks, j), per_chip))]
    args = extra + [xb, wc]
    aliases = {}
    if prev is not None:
        aliases = {len(args): 0}
        in_specs.append(ANY)
        args.append(prev)
    return pl.pallas_call(
        body, name=name, grid=(len(blocks), s // tm), in_specs=in_specs,
        out_specs=pl.BlockSpec((None, dil, tm // dil, tn), out_index),
        out_shape=jax.ShapeDtypeStruct((3, dil, s // dil, aw), BF16), input_output_aliases=aliases,
        scratch_shapes=[_permute_scratch(tm, tn)],
        compiler_params=_params(("parallel", "parallel")))(*args)


def _window_mask(first):
    qi = lax.broadcasted_iota(jnp.int32, (STEPS, 2 * STEPS), 0)
    kj = lax.broadcasted_iota(jnp.int32, (STEPS, 2 * STEPS), 1)
    lowest = jnp.where(first, STEPS, 0)
    return (kj >= qi) & (kj <= qi + STEPS) & (kj >= lowest)


def _attn_fwd(qkv, g):
    _, s, aw = qkv.shape
    heads = aw // HEAD_DIM
    n_blocks = s // STEPS
    per_seq = n_blocks // DILATIONS[g]
    pair = 2 if n_blocks % 2 == 0 else 1

    def body(q_ref, kc_ref, kp_ref, vc_ref, vp_ref, o_ref, l_ref):
        masks = [_window_mask(lax.rem(pl.program_id(0) * pair + j, per_seq) == 0) for j in range(pair)]
        for h in range(heads):
            hs = slice(h * HEAD_DIM, (h + 1) * HEAD_DIM)
            keys = jnp.concatenate([kp_ref[:, hs], kc_ref[:, hs]], axis=0)
            values = jnp.concatenate([vp_ref[:, hs], vc_ref[:, hs]], axis=0)
            for j in range(pair):
                rows = slice(j * STEPS, (j + 1) * STEPS)
                window = slice(j * STEPS, (j + 2) * STEPS)
                sc = lax.dot_general(q_ref[rows, hs], keys[window], NT, preferred_element_type=F32) * SCORE_SCALE
                sc = jnp.where(masks[j], sc, NEG_INF)
                mx = jnp.max(sc, axis=1, keepdims=True)
                e = jnp.exp(sc - mx)
                den = jnp.sum(e, axis=1, keepdims=True)
                o_ref[rows, hs] = (jnp.dot(e.astype(BF16), values[window], preferred_element_type=F32)
                                   / den).astype(BF16)
                l_ref[rows, hs] = jnp.broadcast_to(mx + jnp.log(den), (STEPS, HEAD_DIM))

    def cur(which):
        return pl.BlockSpec((None, pair * STEPS, aw), lambda b: (which, b, 0))

    def prev(which):
        return pl.BlockSpec((None, STEPS, aw), lambda b: (which, jnp.maximum(pair * b - 1, 0), 0))

    out = pl.BlockSpec((pair * STEPS, aw), lambda b: (b, 0))
    return pl.pallas_call(
        body, name=f"attn_fwd{g}", grid=(n_blocks // pair,),
        in_specs=[cur(0), cur(1), prev(1), cur(2), prev(2)], out_specs=[out, out],
        out_shape=[jax.ShapeDtypeStruct((s, aw), BF16), jax.ShapeDtypeStruct((s, aw), F32)],
        compiler_params=_params(("parallel",)))(qkv, qkv, qkv, qkv, qkv)


def _combine_groups(os, ls, zuz, aw):
    s = zuz.shape[0]
    tr = _divisor_tile(s, 256, 8 * DILATIONS[-1])

    def body(*refs):
        o_refs, l_refs, z_ref = refs[0:3], refs[3:6], refs[6]
        oo_ref, y_ref, yt_ref = refs[7:10]
        lq_refs, scratch = refs[10:13], refs[13]
        ls_ = [_merge_rows(l_refs[g], scratch, dil) for g, dil in enumerate(DILATIONS)]
        mx = jnp.maximum(jnp.maximum(ls_[0], ls_[1]), ls_[2])
        ws = [jnp.exp(l - mx) for l in ls_]
        den = ws[0] + ws[1] + ws[2]
        o = ws[0] * _merge_rows(o_refs[0], scratch, DILATIONS[0])
        for g in range(1, N_GROUPS):
            o = o + ws[g] * _merge_rows(o_refs[g], scratch, DILATIONS[g])
        o = o / den
        z = z_ref[...].astype(F32)
        y = o * (z * _sigmoid(z))
        oo_ref[...] = o.astype(BF16)
        y_ref[...] = y.astype(BF16)
        yt_ref[...] = y.T.astype(BF16)
        for g, dil in enumerate(DILATIONS):
            for r, part in enumerate(_split_rows(mx + jnp.log(den), scratch, dil)):
                lq_refs[g][r] = part

    grouped = [_grouped_spec(dil, tr, aw, lambda r: (0, r, 0)) for dil in DILATIONS]
    one = pl.BlockSpec((tr, aw), lambda r: (r, 0))
    b16 = jax.ShapeDtypeStruct((s, aw), BF16)
    out = pl.pallas_call(
        body, name="combine_groups", grid=(s // tr,),
        in_specs=grouped + grouped + [one],
        out_specs=[one, one, pl.BlockSpec((aw, tr), lambda r: (0, r))] + grouped,
        out_shape=[b16, b16, jax.ShapeDtypeStruct((aw, s), BF16)]
        + [jax.ShapeDtypeStruct((dil, s // dil, aw), F32) for dil in DILATIONS],
        scratch_shapes=[_permute_scratch(tr, aw)],
        compiler_params=_params(("parallel",)))(
            *[_grouped_view(t, dil) for t, dil in zip(os, DILATIONS)],
            *[_grouped_view(t, dil) for t, dil in zip(ls, DILATIONS)], zuz)
    return out[0], out[1], out[2], [t.reshape(s, aw) for t in out[3:]]


def _pool_counts(row0, rows, window):
    t = row0 + lax.broadcasted_iota(jnp.int32, (rows, 1), 0)
    return jnp.minimum(t + 1, window).astype(F32)


def _pool_fwd(zuz, w_pool, pool_scale, aw, pw):
    s = zuz.shape[0]
    pg = pw // len(POOL_WINDOWS)
    tr = _divisor_tile(s, 256, 128)
    u_col, z_col = aw // pw, aw // pw + 1
    assert aw % pw == 0

    def body(u_ref, up_ref, z_ref, w_ref, sc_ref, p_ref, l_ref, y_ref, yt_ref):
        r = pl.program_id(0)
        u = u_ref[...].astype(F32)
        halo = jnp.where(r > 0, up_ref[...].astype(F32), 0.0)
        ext = jnp.concatenate([halo, u], axis=0)
        pieces, lins = [], []
        for gi, window in enumerate(POOL_WINDOWS):
            cs = slice(gi * pg, (gi + 1) * pg)
            acc = ext[:, cs]
            shift = 1
            while shift < window:
                acc = acc + pltpu.roll(acc, shift, 0)
                shift *= 2
            p = acc[POOL_HALO:] / _pool_counts(r * tr, tr, window) - u[:, cs]
            pieces.append(p)
            lins.append(jnp.dot(p.astype(BF16), w_ref[gi], preferred_element_type=F32))
        p = jnp.concatenate(pieces, axis=1)
        lin = jnp.concatenate(lins, axis=1)
        z = z_ref[...].astype(F32)
        y = lin * sc_ref[...] * (z * _sigmoid(z))
        p_ref[...] = p.astype(BF16)
        l_ref[...] = lin
        y_ref[...] = y.astype(BF16)
        yt_ref[...] = y.T.astype(BF16)

    per = tr // POOL_HALO
    out = pl.BlockSpec((tr, pw), lambda r: (r, 0))
    return pl.pallas_call(
        body, name="pool_fwd", grid=(s // tr,),
        in_specs=[pl.BlockSpec((tr, pw), lambda r: (r, u_col)),
                  pl.BlockSpec((POOL_HALO, pw), lambda r: (jnp.maximum(r * per - 1, 0), u_col)),
                  pl.BlockSpec((tr, pw), lambda r: (r, z_col)),
                  pl.BlockSpec((len(POOL_WINDOWS), pg, pg), lambda r: (0, 0, 0)),
                  pl.BlockSpec((1, pw), lambda r: (0, 0))],
        out_specs=[out, out, out, pl.BlockSpec((pw, tr), lambda r: (0, r))],
        out_shape=[jax.ShapeDtypeStruct((s, pw), BF16), jax.ShapeDtypeStruct((s, pw), F32),
                   jax.ShapeDtypeStruct((s, pw), BF16), jax.ShapeDtypeStruct((pw, s), BF16)],
        compiler_params=_params(("parallel",)))(zuz, zuz, zuz, w_pool, pool_scale)


def _proj_merge(y_attn, y_pool, wpa4, wpp4, gpre, b_gate):
    s, aw = y_attn.shape
    pw = y_pool.shape[1]
    tn = wpa4.shape[2]
    d = N_CHIPS * tn
    tm = _divisor_tile(s, 512, 128)

    def body(ya_ref, yp_ref, wa_ref, wp_ref, ga_ref, gp_ref, ba_ref, bp_ref, a_ref, p_ref, sa_ref, sp_ref, m_ref,
             mt_ref):
        a = jnp.dot(ya_ref[...], wa_ref[...], preferred_element_type=F32)
        p = jnp.dot(yp_ref[...], wp_ref[...], preferred_element_type=F32)
        sa = _sigmoid(ga_ref[...].astype(F32) + ba_ref[...])
        sp = _sigmoid(gp_ref[...].astype(F32) + bp_ref[...])
        merged = sa * a + sp * p
        a_ref[...] = a.astype(BF16)
        p_ref[...] = p.astype(BF16)
        sa_ref[...] = sa.astype(BF16)
        sp_ref[...] = sp.astype(BF16)
        m_ref[...] = merged.astype(BF16)
        mt_ref[...] = merged.T.astype(BF16)

    out = pl.BlockSpec((tm, tn), lambda n, m: (m, n))
    f = jax.ShapeDtypeStruct((s, d), BF16)
    return pl.pallas_call(
        body, name="proj_merge", grid=(N_CHIPS, s // tm),
        in_specs=[pl.BlockSpec((tm, aw), lambda n, m: (m, 0)), pl.BlockSpec((tm, pw), lambda n, m: (m, 0)),
                  pl.BlockSpec((None, aw, tn), lambda n, m: (n, 0, 0)),
                  pl.BlockSpec((None, pw, tn), lambda n, m: (n, 0, 0)),
                  pl.BlockSpec((tm, tn), lambda n, m: (m, n)), pl.BlockSpec((tm, tn), lambda n, m: (m, N_CHIPS + n)),
                  pl.BlockSpec((1, tn), lambda n, m: (0, n)), pl.BlockSpec((1, tn), lambda n, m: (0, N_CHIPS + n))],
        out_specs=[out] * 5 + [pl.BlockSpec((tn, tm), lambda n, m: (n, m))],
        out_shape=[f] * 5 + [jax.ShapeDtypeStruct((d, s), BF16)],
        compiler_params=_params(("parallel", "parallel")))(y_attn, y_pool, wpa4, wpp4, gpre, gpre, b_gate, b_gate)


def _out_norm_loss(merged, w_out, x, target, gamma, beta):
    s, d = x.shape
    tm = _divisor_tile(s, 256, 16)

    def body(m_ref, w_ref, x_ref, t_ref, g_ref, b_ref, dr_ref, drb_ref, loss_ref, dg_ref, db_ref):
        @pl.when(pl.program_id(0) == 0)
        def _():
            loss_ref[...] = jnp.zeros_like(loss_ref)
            dg_ref[...] = jnp.zeros_like(dg_ref)
            db_ref[...] = jnp.zeros_like(db_ref)

        r = ALPHA * x_ref[...] + jnp.dot(m_ref[...], w_ref[...], preferred_element_type=F32)
        mu = jnp.mean(r, axis=1, keepdims=True)
        rc = r - mu
        rstd = lax.rsqrt(jnp.mean(rc * rc, axis=1, keepdims=True) + LN_EPS)
        xhat = rc * rstd
        diff = xhat * g_ref[...] + b_ref[...] - t_ref[...]
        dy = diff / d
        loss_ref[...] += jnp.sum(diff * diff, axis=0, keepdims=True)
        dg_ref[...] += jnp.sum(dy * xhat, axis=0, keepdims=True)
        db_ref[...] += jnp.sum(dy, axis=0, keepdims=True)
        dxhat = dy * g_ref[...]
        dr = rstd * (dxhat - jnp.mean(dxhat, axis=1, keepdims=True)
                     - xhat * jnp.mean(dxhat * xhat, axis=1, keepdims=True))
        dr_ref[...] = dr
        drb_ref[...] = dr.astype(BF16)

    row = pl.BlockSpec((tm, d), lambda m: (m, 0))
    vec = pl.BlockSpec((1, d), lambda m: (0, 0))
    v = jax.ShapeDtypeStruct((1, d), F32)
    return pl.pallas_call(
        body, name="out_norm_loss", grid=(s // tm,),
        in_specs=[row, pl.BlockSpec((d, d), lambda m: (0, 0)), row, row, vec, vec],
        out_specs=[row, row, vec, vec, vec],
        out_shape=[jax.ShapeDtypeStruct((s, d), F32), jax.ShapeDtypeStruct((s, d), BF16), v, v, v],
        compiler_params=_params(("arbitrary",), vmem_mib=56))(merged, w_out, x, target, gamma, beta)


def _merge_bwd(drb, w_out, a, p, sa, sp):
    s, d = drb.shape
    tm = _divisor_tile(s, 512, 16)
    tn = d // N_CHIPS

    def body(dr_ref, w_ref, a_ref, p_ref, sa_ref, sp_ref, da_ref, dp_ref, dga_ref, dgp_ref, dba_ref, dbp_ref):
        @pl.when(pl.program_id(1) == 0)
        def _():
            dba_ref[...] = jnp.zeros_like(dba_ref)
            dbp_ref[...] = jnp.zeros_like(dbp_ref)

        dm = lax.dot_general(dr_ref[...], w_ref[...], NT, preferred_element_type=F32)
        sa = sa_ref[...].astype(F32)
        sp = sp_ref[...].astype(F32)
        da_ref[...] = (dm * sa).astype(BF16)
        dp_ref[...] = (dm * sp).astype(BF16)
        dga = dm * a_ref[...].astype(F32) * sa * (1.0 - sa)
        dgp = dm * p_ref[...].astype(F32) * sp * (1.0 - sp)
        dga_ref[...] = dga.astype(BF16)
        dgp_ref[...] = dgp.astype(BF16)
        dba_ref[...] += jnp.sum(dga, axis=0, keepdims=True)
        dbp_ref[...] += jnp.sum(dgp, axis=0, keepdims=True)

    blk = pl.BlockSpec((tm, tn), lambda n, m: (m, n))
    vec = pl.BlockSpec((1, tn), lambda n, m: (0, n))
    b16 = jax.ShapeDtypeStruct((s, d), BF16)
    v = jax.ShapeDtypeStruct((1, d), F32)
    return pl.pallas_call(
        body, name="merge_bwd", grid=(N_CHIPS, s // tm),
        in_specs=[pl.BlockSpec((tm, d), lambda n, m: (m, 0)), pl.BlockSpec((tn, d), lambda n, m: (n, 0)),
                  blk, blk, blk, blk],
        out_specs=[blk, blk, blk, blk, vec, vec], out_shape=[b16, b16, b16, b16, v, v],
        compiler_params=_params(("parallel", "arbitrary")))(drb, w_out, a, p, sa, sp)


def _proj_t(dy_ref, w_ref, tn):
    acc = None
    for n in range(N_CHIPS):
        t = lax.dot_general(dy_ref[:, n * tn:(n + 1) * tn], w_ref[n], NT, preferred_element_type=F32)
        acc = t if acc is None else acc + t
    return acc


def _attn_gate_bwd(da, wpa4, zuz, o):
    s, d = da.shape
    aw, tn = wpa4.shape[1], wpa4.shape[2]
    heads = aw // HEAD_DIM
    tm = _divisor_tile(s, 256, 16 * DILATIONS[-1])

    def body(*refs):
        da_ref, w_ref, z_ref, o_ref, dz_ref = refs[:5]
        do_refs, dd_refs, scratch = refs[5:8], refs[8:11], refs[11]
        dy = _proj_t(da_ref, w_ref, tn)
        z, o = z_ref[...].astype(F32), o_ref[...].astype(F32)
        sg = _sigmoid(z)
        do = dy * (z * sg)
        dz_ref[...] = (dy * o * _dsilu(z, sg)).astype(BF16)
        prod = do * o
        dd = jnp.concatenate(
            [jnp.broadcast_to(jnp.sum(prod[:, h * HEAD_DIM:(h + 1) * HEAD_DIM], axis=1, keepdims=True),
                              (tm, HEAD_DIM)) for h in range(heads)], axis=1)
        for g, dil in enumerate(DILATIONS):
            for r, part in enumerate(_split_rows(do, scratch, dil)):
                do_refs[g][r] = part.astype(BF16)
            for r, part in enumerate(_split_rows(dd, scratch, dil)):
                dd_refs[g][r] = part

    row = pl.BlockSpec((tm, aw), lambda m: (m, 0))
    grouped = [_grouped_spec(dil, tm, aw, lambda m: (0, m, 0)) for dil in DILATIONS]
    out = pl.pallas_call(
        body, name="attn_gate_bwd", grid=(s // tm,),
        in_specs=[pl.BlockSpec((tm, d), lambda m: (m, 0)), pl.BlockSpec((N_CHIPS, aw, tn), lambda m: (0, 0, 0)),
                  row, row],
        out_specs=[row] + grouped + grouped,
        out_shape=[jax.ShapeDtypeStruct((s, aw), BF16)]
        + [jax.ShapeDtypeStruct((dil, s // dil, aw), BF16) for dil in DILATIONS]
        + [jax.ShapeDtypeStruct((dil, s // dil, aw), F32) for dil in DILATIONS],
        scratch_shapes=[_permute_scratch(tm, aw)],
        compiler_params=_params(("parallel",)))(da, wpa4, zuz, o)
    return out[0], [t.reshape(s, aw) for t in out[1:4]], [t.reshape(s, aw) for t in out[4:7]]


def _pool_gate_bwd(dp_in, wpp4, zuz, lin, pooled, w_pool, pool_scale, aw):
    s, d = dp_in.shape
    pw, tn = wpp4.shape[1], wpp4.shape[2]
    n_win = len(POOL_WINDOWS)
    pg = pw // n_win
    tm = _divisor_tile(s, 256, 16)
    z_col = aw // pw + 1

    def body(dp_ref, w_ref, z_ref, l_ref, p_ref, wp_ref, sc_ref, dz_ref, dpo_ref, dw_ref, ds_ref):
        @pl.when(pl.program_id(0) == 0)
        def _():
            dw_ref[...] = jnp.zeros_like(dw_ref)
            ds_ref[...] = jnp.zeros_like(ds_ref)

        dy = _proj_t(dp_ref, w_ref, tn)
        z, lin_ = z_ref[...].astype(F32), l_ref[...]
        sg = _sigmoid(z)
        dypp = dy * (z * sg)
        dz_ref[...] = (dy * (lin_ * sc_ref[...]) * _dsilu(z, sg)).astype(BF16)
        ds_ref[...] += jnp.sum(dypp * lin_, axis=0, keepdims=True)
        dlin = (dypp * sc_ref[...]).astype(BF16)
        for gi in range(n_win):
            cs = slice(gi * pg, (gi + 1) * pg)
            dw_ref[gi] += lax.dot_general(p_ref[:, cs], dlin[:, cs], TN, preferred_element_type=F32)
            dpo_ref[:, cs] = lax.dot_general(dlin[:, cs], wp_ref[gi], NT, preferred_element_type=F32)

    row = pl.BlockSpec((tm, pw), lambda m: (m, 0))
    return pl.pallas_call(
        body, name="pool_gate_bwd", grid=(s // tm,),
        in_specs=[pl.BlockSpec((tm, d), lambda m: (m, 0)), pl.BlockSpec((N_CHIPS, pw, tn), lambda m: (0, 0, 0)),
                  pl.BlockSpec((tm, pw), lambda m: (m, z_col)), row, row,
                  pl.BlockSpec((n_win, pg, pg), lambda m: (0, 0, 0)), pl.BlockSpec((1, pw), lambda m: (0, 0))],
        out_specs=[row, row, pl.BlockSpec((n_win, pg, pg), lambda m: (0, 0, 0)),
                   pl.BlockSpec((1, pw), lambda m: (0, 0))],
        out_shape=[jax.ShapeDtypeStruct((s, pw), BF16), jax.ShapeDtypeStruct((s, pw), F32),
                   jax.ShapeDtypeStruct((n_win, pg, pg), F32), jax.ShapeDtypeStruct((1, pw), F32)],
        compiler_params=_params(("arbitrary",)))(dp_in, wpp4, zuz, lin, pooled, w_pool, pool_scale)


def _pool_bwd(dpooled):
    s, pw = dpooled.shape
    pg = pw // len(POOL_WINDOWS)
    tr = _divisor_tile(s, 256, POOL_HALO)
    per = tr // POOL_HALO
    n_tiles = s // tr

    def body(c_ref, n_ref, du_ref):
        r = pl.program_id(0)
        cur = c_ref[...]
        halo = jnp.where(r < n_tiles - 1, n_ref[...], 0.0)
        ext = jnp.concatenate([cur, halo], axis=0)
        rows = tr + POOL_HALO
        for gi, window in enumerate(POOL_WINDOWS):
            cs = slice(gi * pg, (gi + 1) * pg)
            acc = ext[:, cs] / _pool_counts(r * tr, rows, window)
            shift = 1
            while shift < window:
                acc = acc + pltpu.roll(acc, rows - shift, 0)
                shift *= 2
            du_ref[:, cs] = (acc[:tr] - cur[:, cs]).astype(BF16)

    return pl.pallas_call(
        body, name="pool_bwd", grid=(n_tiles,),
        in_specs=[pl.BlockSpec((tr, pw), lambda r: (r, 0)),
                  pl.BlockSpec((POOL_HALO, pw), lambda r: (jnp.minimum((r + 1) * per, s // POOL_HALO - 1), 0))],
        out_specs=pl.BlockSpec((tr, pw), lambda r: (r, 0)),
        out_shape=jax.ShapeDtypeStruct((s, pw), BF16), compiler_params=_params(("parallel",)))(dpooled, dpooled)


def _attn_bwd(qkv, do, lse, dd, g):
    _, s, aw = qkv.shape
    heads = aw // HEAD_DIM
    n_blocks = s // STEPS
    per_seq = n_blocks // DILATIONS[g]
    pair = 2 if n_blocks % 2 == 0 else 1
    rows_ = pair * STEPS
    n_steps = n_blocks // pair
    tail = slice(rows_ - STEPS, rows_)

    def body(q_ref, do_ref, l_ref, dd_ref, kc_ref, kp_ref, vc_ref, vp_ref, out_ref, cq_ref, ck_ref, cv_ref):
        b = pl.program_id(0)

        @pl.when(b == 0)
        def _():
            cq_ref[...] = jnp.zeros_like(cq_ref)
            ck_ref[...] = jnp.zeros_like(ck_ref)
            cv_ref[...] = jnp.zeros_like(cv_ref)

        out_ref[0] = cq_ref[...].astype(BF16)

        @pl.when(b < n_steps)
        def _():
            masks = [_window_mask(lax.rem(b * pair + j, per_seq) == 0) for j in range(pair)]
            for h in range(heads):
                hs = slice(h * HEAD_DIM, (h + 1) * HEAD_DIM)
                keys = jnp.concatenate([kp_ref[:, hs], kc_ref[:, hs]], axis=0)
                values = jnp.concatenate([vp_ref[:, hs], vc_ref[:, hs]], axis=0)
                dks, dvs = [], []
                for j in range(pair):
                    rows = slice(j * STEPS, (j + 1) * STEPS)
                    window = slice(j * STEPS, (j + 2) * STEPS)
                    q, do_, kk, vv = q_ref[rows, hs], do_ref[rows, hs], keys[window], values[window]
                    lse_ = jnp.concatenate([l_ref[rows, hs], l_ref[rows, hs]], axis=1)
                    dd_ = jnp.concatenate([dd_ref[rows, hs], dd_ref[rows, hs]], axis=1)
                    sc = lax.dot_general(q, kk, NT, preferred_element_type=F32) * SCORE_SCALE
                    prob = jnp.where(masks[j], jnp.exp(sc - lse_), 0.0)
                    dprob = lax.dot_general(do_, vv, NT, preferred_element_type=F32)
                    dsc = prob * (dprob - dd_) * SCORE_SCALE
                    cq_ref[rows, hs] = jnp.dot(dsc.astype(BF16), kk, preferred_element_type=F32)
                    dks.append(lax.dot_general(dsc.astype(BF16), q, TN, preferred_element_type=F32))
                    dvs.append(lax.dot_general(prob.astype(BF16), do_, TN, preferred_element_type=F32))
                for which, carry, parts in ((1, ck_ref, dks), (2, cv_ref, dvs)):
                    out_ref[which, tail, hs] = (carry[tail, hs] + parts[0][:STEPS]).astype(BF16)
                    if pair > 1:
                        out_ref[which, :rows_ - STEPS, hs] = carry[:rows_ - STEPS, hs].astype(BF16)
                    for j in range(pair):
                        total = parts[j][STEPS:]
                        if j + 1 < pair:
                            total = total + parts[j + 1][:STEPS]
                        carry[j * STEPS:(j + 1) * STEPS, hs] = total

        @pl.when(b == n_steps)
        def _():
            out_ref[1] = ck_ref[...].astype(BF16)
            out_ref[2] = cv_ref[...].astype(BF16)

    last = n_steps - 1

    def cur(which):
        return pl.BlockSpec((None, rows_, aw), lambda b: (which, jnp.minimum(b, last), 0))

    def prev(which):
        return pl.BlockSpec((None, STEPS, aw), lambda b: (which, jnp.clip(b * pair - 1, 0, n_blocks - 1), 0))

    row = pl.BlockSpec((rows_, aw), lambda b: (jnp.minimum(b, last), 0))
    return pl.pallas_call(
        body, name=f"attn_bwd{g}", grid=(n_steps + 1,),
        in_specs=[cur(0), row, row, row, cur(1), prev(1), cur(2), prev(2)],
        out_specs=pl.BlockSpec((3, rows_, aw), lambda b: (0, jnp.clip(b - 1, 0, last), 0)),
        out_shape=jax.ShapeDtypeStruct((3, s, aw), BF16),
        scratch_shapes=[pltpu.VMEM((rows_, aw), F32)] * 3,
        compiler_params=_params(("arbitrary",)))(qkv, do, lse, dd, qkv, qkv, qkv, qkv)


def _weight_grad(at, b, tn, col_blocks, name):
    m, k = at.shape
    n = b.shape[1]
    tm = _divisor_tile(m, 1024, 16)
    tk = _divisor_tile(k, 2048, 128)
    nk = k // tk

    def body(a_ref, b_ref, o_ref, acc_ref):
        kk = pl.program_id(2)

        @pl.when(kk == 0)
        def _():
            acc_ref[...] = jnp.zeros_like(acc_ref)

        acc_ref[...] += jnp.dot(a_ref[...], b_ref[...], preferred_element_type=F32)

        @pl.when(kk == nk - 1)
        def _():
            o_ref[...] = acc_ref[...].astype(BF16)

    if col_blocks:
        out_spec = pl.BlockSpec((None, tm, tn), lambda i, j, kk: (j, i, 0))
        out_shape = jax.ShapeDtypeStruct((n // tn, m, tn), BF16)
    else:
        out_spec = pl.BlockSpec((tm, tn), lambda i, j, kk: (i, j))
        out_shape = jax.ShapeDtypeStruct((m, n), BF16)
    return pl.pallas_call(
        body, name=name, grid=(m // tm, n // tn, nk),
        in_specs=[pl.BlockSpec((tm, tk), lambda i, j, kk: (i, kk)), pl.BlockSpec((tk, tn), lambda i, j, kk: (kk, j))],
        out_specs=out_spec, out_shape=out_shape, scratch_shapes=[pltpu.VMEM((tm, tn), F32)],
        compiler_params=_params(("parallel", "parallel", "arbitrary")))(at, b)


def _w_in_grad_part(xt, b, col_of, n_local, tn, w_shape, prev, name):
    d, s = xt.shape
    per_chip = w_shape[2] // tn
    tm = _divisor_tile(d, 1024, 16)
    tk = _divisor_tile(s, 2048, 128)
    nk = s // tk

    def body(*refs):
        a_ref, b_ref, o_ref, acc_ref = refs[0], refs[1], refs[-2], refs[-1]
        kk = pl.program_id(2)

        @pl.when(kk == 0)
        def _():
            acc_ref[...] = jnp.zeros_like(acc_ref)

        acc_ref[...] += jnp.dot(a_ref[...], b_ref[...], preferred_element_type=F32)

        @pl.when(kk == nk - 1)
        def _():
            o_ref[...] = acc_ref[...].astype(BF16)

    if b.ndim == 3:
        sub = b.shape[2] // tn
        b_spec = pl.BlockSpec((None, tk, tn), lambda j, i, kk: (j // sub, kk, j % sub))
    else:
        b_spec = pl.BlockSpec((tk, tn), lambda j, i, kk: (kk, j))
    in_specs = [pl.BlockSpec((tm, tk), lambda j, i, kk: (i, kk)), b_spec]
    args = [xt, b]
    aliases = {}
    if prev is not None:
        in_specs.append(ANY)
        args.append(prev)
        aliases = {2: 0}
    return pl.pallas_call(
        body, name=name, grid=(n_local, d // tm, nk), in_specs=in_specs,
        out_specs=pl.BlockSpec((None, tm, tn), lambda j, i, kk: (col_of(j) // per_chip, i, col_of(j) % per_chip)),
        out_shape=jax.ShapeDtypeStruct(w_shape, BF16), scratch_shapes=[pltpu.VMEM((tm, tn), F32)],
        input_output_aliases=aliases,
        compiler_params=_params(("parallel", "parallel", "arbitrary")))(*args)


def _assemble_w(wcs, after):
    n, d, wc = wcs[0].shape
    tr = _divisor_tile(d, 256, 16)

    def body(after_ref, *refs):
        o_ref = refs[-1]
        for ch in range(W_CHUNKS):
            o_ref[:, ch * wc:(ch + 1) * wc] = refs[ch][...]

    return pl.pallas_call(
        body, name="assemble_w", grid=(n, d // tr),
        in_specs=[pl.BlockSpec(after.shape, lambda b, r: (0, 0))]
        + [pl.BlockSpec((None, tr, wc), lambda b, r: (b, r, 0))] * W_CHUNKS,
        out_specs=pl.BlockSpec((None, tr, W_CHUNKS * wc), lambda b, r: (b, r, 0)),
        out_shape=jax.ShapeDtypeStruct((n, d, W_CHUNKS * wc), wcs[0].dtype),
        compiler_params=_params(("parallel", "parallel")))(after, *wcs)


def _x_grad(dqkv, rest, w4, dr, aw, tn):
    s, d = dr.shape
    sub = aw // tn
    n_qkv = 3 * N_GROUPS * sub
    los, lo = [], n_qkv
    for p in rest:
        los.append(lo)
        lo += p.shape[1] // tn
    n_blocks = lo
    per_chip = n_blocks // N_CHIPS
    tm = _divisor_tile(s, 512, 16 * DILATIONS[-1])

    def body(*refs):
        q_refs, r_refs = refs[:N_GROUPS], refs[N_GROUPS:N_GROUPS + len(rest)]
        w_ref, dr_ref, o_ref, acc_ref, scratch = refs[-5:]
        j = pl.program_id(1)

        @pl.when(j == 0)
        def _():
            acc_ref[...] = ALPHA * dr_ref[...]

        for g, dil in enumerate(DILATIONS):
            @pl.when((j < n_qkv) & (lax.rem(j // sub, N_GROUPS) == g))
            def _(g=g, dil=dil):
                rows = _merge_rows(q_refs[g], scratch, dil).astype(BF16)
                acc_ref[...] += lax.dot_general(rows, w_ref[...], NT, preferred_element_type=F32)

        for p_ref, lo_, piece in zip(r_refs, los, rest):
            @pl.when((j >= lo_) & (j < lo_ + piece.shape[1] // tn))
            def _(p_ref=p_ref):
                acc_ref[...] += lax.dot_general(p_ref[...], w_ref[...], NT, preferred_element_type=F32)

        @pl.when(j == n_blocks - 1)
        def _():
            o_ref[...] = acc_ref[...]

    def qkv_spec(dil):
        def index(i, j):
            region = jnp.minimum(j // sub, 3 * N_GROUPS - 1)
            return region // N_GROUPS, 0, i, jnp.where(j < n_qkv, j % sub, 0)

        return pl.BlockSpec((None, dil, tm // dil, tn), index)

    def rest_spec(lo_, piece):
        n = piece.shape[1] // tn
        return pl.BlockSpec((tm, tn), lambda i, j: (i, jnp.clip(j - lo_, 0, n - 1)))

    row = pl.BlockSpec((tm, d), lambda i, j: (i, 0))
    return pl.pallas_call(
        body, name="x_grad", grid=(s // tm, n_blocks),
        in_specs=[qkv_spec(dil) for dil in DILATIONS] + [rest_spec(lo_, p) for lo_, p in zip(los, rest)]
        + [pl.BlockSpec((None, d, tn), lambda i, j: (j // per_chip, 0, j % per_chip)), row],
        out_specs=row, out_shape=jax.ShapeDtypeStruct((s, d), F32),
        scratch_shapes=[pltpu.VMEM((tm, d), F32), _permute_scratch(tm, tn)],
        compiler_params=_params(("parallel", "arbitrary"), vmem_mib=56))(
            *[t.reshape(3, dil, s // dil, aw) for t, dil in zip(dqkv, DILATIONS)], *rest, w4, dr)


def _prepare_x(x, after=None):
    s, d = x.shape
    tc = 2 * LANES
    slabs = tc // LANES
    ordered = [] if after is None else [after]

    def body(*refs):
        x_ref, xb_ref = refs[len(ordered):len(ordered) + 2]
        xt_refs, scratch = refs[len(ordered) + 2:len(ordered) + 2 + N_GROUPS], refs[-1]
        t = x_ref[...]
        xb_ref[...] = t.astype(BF16)
        for c in range(slabs):
            scratch[c] = t[:, c * LANES:(c + 1) * LANES]
        for g, dil in enumerate(DILATIONS):
            length = s // dil
            for r in range(dil):
                part = t if dil == 1 else jnp.concatenate(
                    [scratch[c, pl.ds(r, length, stride=dil), :] for c in range(slabs)], axis=1)
                xt_refs[g][:, r * length:(r + 1) * length] = part.T.astype(BF16)

    col = pl.BlockSpec((s, tc), lambda j: (0, j))
    row = pl.BlockSpec((tc, s), lambda j: (j, 0))
    t_shape = jax.ShapeDtypeStruct((d, s), BF16)
    out = pl.pallas_call(
        body, name="prepare_x", grid=(d // tc,),
        in_specs=[pl.BlockSpec(t.shape, lambda j: (0, 0)) for t in ordered] + [col],
        out_specs=[col] + [row] * N_GROUPS,
        out_shape=[jax.ShapeDtypeStruct((s, d), BF16)] + [t_shape] * N_GROUPS,
        scratch_shapes=[_permute_scratch(s, tc)], compiler_params=_params(("parallel",)))(*ordered, x)
    return out[0], out[1:]


def _local_step(x, target, w_chunk, w_width, b_gate, pool_scale, gamma, beta, aw, pw, small_weights,
                start_exchange=None, first_token=None):
    s, d = x.shape
    tn = _col_tile(aw, pw, w_width)
    sub = aw // tn
    per_chip = w_width // tn
    qkv_w = 3 * N_GROUPS * aw
    w_shape = (N_CHIPS, d, w_width)

    regions = [dict(kind=g, blocks=[(which * N_GROUPS + g) * sub + i for which in range(3) for i in range(sub)])
               for g in range(N_GROUPS)]
    lo = qkv_w // tn
    for name, width in (("zuz", aw + 2 * pw), ("gates", 2 * d)):
        regions.append(dict(kind=name, blocks=list(range(lo, lo + width // tn)), j0=lo, width=width))
        lo += width // tn
    results = [None] * len(regions)
    xb, xts = _prepare_x(x, first_token)
    wcs, after = [], [xb]
    for ch in range(W_CHUNKS):
        wc, token = w_chunk(ch, after)
        wcs.append(wc)
        after = []
        for i, region in enumerate(regions):
            blocks = [b for b in region["blocks"] if _chunk_of(b, per_chip) == ch]
            if not blocks:
                continue
            if region["kind"] in range(N_GROUPS):
                results[i] = _in_proj_qkv(xb, wc, region["kind"], blocks, aw, tn, results[i], token,
                                          f"in_proj_qkv{region['kind']}_{ch}")
            else:
                results[i] = _in_proj(xb, wc, blocks, region["j0"], region["width"], tn, BF16, results[i], token,
                                      f"in_proj_{region['kind']}_{ch}")
            after.append(results[i])
    qkv = [results[g].reshape(3, s, aw) for g in range(N_GROUPS)]
    zuz, gpre = results[N_GROUPS], results[N_GROUPS + 1]

    attn = [_attn_fwd(qkv[g], g) for g in range(N_GROUPS)]
    o, y_attn, y_attn_t, lse = _combine_groups([a[0] for a in attn], [a[1] for a in attn], zuz, aw)
    w_pool, wpa4, wpp4, w_out = small_weights(o)
    pooled, lin, y_pool, y_pool_t = _pool_fwd(zuz, w_pool, pool_scale, aw, pw)
    a, p, sa, sp, merged, merged_t = _proj_merge(y_attn, y_pool, wpa4, wpp4, gpre, b_gate)
    dr, drb, loss_lanes, d_gamma, d_beta = _out_norm_loss(merged, w_out, x, target, gamma, beta)

    da, dp, d_gpre_a, d_gpre_p, d_b_a, d_b_p = _merge_bwd(drb, w_out, a, p, sa, sp)
    d_b_gate = jnp.concatenate([d_b_a, d_b_p], axis=1)
    d_w_out = _weight_grad(merged_t, drb, d // N_CHIPS, False, "w_out_grad")
    d_wpa4 = _weight_grad(y_attn_t, da, d // N_CHIPS, True, "w_proj_attn_grad")
    d_wpp4 = _weight_grad(y_pool_t, dp, d // N_CHIPS, True, "w_proj_pool_grad")
    d_z_attn, d_o, dd = _attn_gate_bwd(da, wpa4, zuz, o)
    d_z_pool, d_pooled, d_w_pool, d_pool_scale = _pool_gate_bwd(dp, wpp4, zuz, lin, pooled, w_pool, pool_scale, aw)
    d_u = _pool_bwd(d_pooled)
    dqkv = [_attn_bwd(qkv[g], d_o[g], lse[g], dd[g], g) for g in range(N_GROUPS)]

    rest = [d_z_attn, d_u, d_z_pool, d_gpre_a, d_gpre_p]
    d_w_in4 = None
    for g in range(N_GROUPS):
        d_w_in4 = _w_in_grad_part(xts[g], dqkv[g], lambda j, g=g: ((j // sub) * N_GROUPS + g) * sub + j % sub,
                                  3 * sub, tn, w_shape, d_w_in4, f"w_in_grad_qkv{g}")
    lo = qkv_w // tn
    for i, piece in enumerate(rest):
        n_local = piece.shape[1] // tn
        d_w_in4 = _w_in_grad_part(xts[0], piece, lambda j, lo=lo: lo + j, n_local, tn, w_shape, d_w_in4,
                                  f"w_in_grad_rest{i}")
        lo += n_local
    grads = dict(loss_lanes=loss_lanes, w_in=d_w_in4, b_gate=d_b_gate, w_pool=d_w_pool,
                 pool_scale=d_pool_scale, w_proj_attn=d_wpa4, w_proj_pool=d_wpp4, w_out=d_w_out,
                 ln_gamma=d_gamma, ln_beta=d_beta)
    token = jnp.zeros((8, 128), F32) if start_exchange is None else start_exchange(grads)
    grads["d_x"] = _x_grad(dqkv, rest, _assemble_w(wcs, token), dr, aw, tn)
    return grads


def _pack_small(wpa, wpp, w_out, w_pool):
    width = wpa.shape[1]
    return jnp.concatenate([wpa, wpp, w_out.reshape(-1, width), w_pool.reshape(-1, width)], axis=0)


def _unpack_small(packed, aw, pw, d, pg):
    lead = packed.shape[:-2]
    width = d // N_CHIPS
    r0, r1, r2 = aw, aw + pw, aw + pw + d
    return (packed[..., :r0, :], packed[..., r0:r1, :], packed[..., r1:r2, :].reshape(lead + (width, d)),
            packed[..., r2:, :].reshape(lead + (len(POOL_WINDOWS), pg // N_CHIPS, pg)))


def _pack_rows(vectors, rows):
    flat = jnp.concatenate([v.reshape(-1) for v in vectors])
    return jnp.pad(flat, (0, rows * 128 - flat.shape[0])).reshape(rows, 128)


def _unpack_rows(packed, sizes):
    flat, out, lo = packed.reshape(-1), [], 0
    for n in sizes:
        out.append(flat[lo:lo + n].reshape(1, n))
        lo += n
    return out


def kernel(x, w_in, b_gate, w_pool, pool_scale, w_proj_attn, w_proj_pool, w_out, ln_gamma, ln_beta, loss_target, m_w_in, m_b_gate, m_w_pool, m_pool_scale, m_w_proj_attn, m_w_proj_pool, m_w_out, m_ln_gamma, m_ln_beta, v_w_in, v_b_gate, v_w_pool, v_pool_scale, v_w_proj_attn, v_w_proj_pool, v_w_out, v_ln_gamma, v_ln_beta):
    s, d = x.shape[1], x.shape[2]
    aw, pw = w_proj_attn.shape[1], w_proj_pool.shape[1]
    pg = w_pool.shape[3]
    n_win = len(POOL_WINDOWS)

    def small(wpa, wpp, wo, wpl):
        return _pack_small(wpa[0], wpp[0], wo[0], wpl[0])

    chip = 2 * lax.axis_index("x") + lax.axis_index("y")
    core = lax.axis_index("c")

    flight = {"chunk": _halves_start(_place_block(w_in[0], N_CHIPS, chip, BF16, "place_w_in0", 0, W_CHUNKS), x,
                                     "gather_w_in0_start")}
    first_token = flight["chunk"][2]
    w_small = small(w_proj_attn, w_proj_pool, w_out, w_pool) + first_token[0, 0]
    placed = [None] + [_place_block(w_in[0], N_CHIPS, chip, BF16, f"place_w_in{ch}", ch, W_CHUNKS, first_token)
                       for ch in range(1, W_CHUNKS)]
    placed_small = _place_block(w_small, N_CHIPS, chip, BF16, "place_w_small", after=first_token)

    def w_chunk(ch, after):
        sems, thru, _ = flight["chunk"]
        if ch == 0:
            after = after + placed[1:] + [placed_small]
        landed = _halves_wait(sems, thru, after, f"gather_w_in{ch}_wait")
        if ch + 1 < W_CHUNKS:
            flight["chunk"] = _halves_start(placed[ch + 1], landed, f"gather_w_in{ch + 1}_start")
            token = flight["chunk"][2]
        else:
            flight["small"] = _broadcast_start(placed_small, landed, "gather_small_start")
            token = flight["small"][2]
        return _forward_halves(landed, f"forward_w_in{ch}"), token

    def small_weights(after):
        sems, thru, _ = flight["small"]
        small4 = _broadcast_wait(sems, thru, after, "gather_small_wait")
        wpa4, wpp4, w_out4, w_pool4 = _unpack_small(small4, aw, pw, d, pg)
        return w_pool4.transpose(1, 0, 2, 3).reshape(n_win, pg, pg), wpa4, wpp4, w_out4.reshape(d, d)

    exchange = {}

    def start_exchange(g):
        g_pool4 = g["w_pool"].reshape(n_win, N_CHIPS, pg // N_CHIPS, pg).transpose(1, 0, 2, 3).astype(BF16)
        g_out4 = g["w_out"].reshape(N_CHIPS, d // N_CHIPS, d)
        g_small4 = jnp.concatenate([g["w_proj_attn"], g["w_proj_pool"], g_out4.reshape(N_CHIPS, -1, d // N_CHIPS),
                                    g_pool4.reshape(N_CHIPS, -1, d // N_CHIPS)], axis=1)
        theirs_big, theirs_small = _swap_halves([g["w_in"], g_small4])
        chip_big, placed_big = _add_halves(g["w_in"], theirs_big, core, chip, "add_cores_big")
        chip_small, placed_small = _add_halves(g_small4, theirs_small, core, chip, "add_cores_small")
        sems, sums, placed, token = _scatter_start([chip_big, chip_small], [placed_big, placed_small])
        exchange.update(sems=sems, sums=sums, placed=placed)
        return token

    g = _local_step(x[0], loss_target[0], w_chunk, w_in.shape[2], b_gate, pool_scale, ln_gamma, ln_beta, aw, pw,
                    small_weights, start_exchange, first_token)
    got_big, got_small = _scatter_wait(exchange["sems"], exchange["sums"], exchange["placed"], g["d_x"])
    join_sems, halves = _join_start([_sum_slots(got_big, core, "sum_chips_big"),
                                     _sum_slots(got_small, core, "sum_chips_small")])
    mv_small = (small(m_w_proj_attn, m_w_proj_pool, m_w_out, m_w_pool),
                small(v_w_proj_attn, v_w_proj_pool, v_w_out, v_w_pool))
    upd_in = _adamw_half(w_in[0], halves[0], m_w_in[0], v_w_in[0], core, None, "adamw_w_in_own")
    upd_small = _adamw_half(w_small, halves[1], *mv_small, core, None, "adamw_small_own")
    grad_w_in, grad_small = _join_wait(join_sems, halves, [upd_in[0], upd_small[0]])
    upd_in = _adamw_half(w_in[0], grad_w_in, m_w_in[0], v_w_in[0], 1 - core, upd_in, "adamw_w_in_other")
    upd_small = _adamw_half(w_small, grad_small, *mv_small, 1 - core, upd_small, "adamw_small_other")
    grad_w_in = grad_w_in.reshape(-1, grad_w_in.shape[2])
    grad_small = grad_small.reshape(-1, grad_small.shape[2])

    sizes = [b_gate.shape[1], pool_scale.shape[1], d, d, 1]
    rows = -(-sum(sizes) // (8 * 128)) * 8
    loss_part = (0.5 / d) * jnp.sum(g["loss_lanes"]).reshape(1, 1)
    parts = _gather_rows(_pack_rows([g["b_gate"], g["pool_scale"], g["ln_gamma"], g["ln_beta"], loss_part], rows))
    zero = jnp.zeros((1, 1), F32)
    packed = [_pack_rows(vs, rows) for vs in ([b_gate, pool_scale, ln_gamma, ln_beta, zero],
                                              [m_b_gate, m_pool_scale, m_ln_gamma, m_ln_beta, zero],
                                              [v_b_gate, v_pool_scale, v_ln_gamma, v_ln_beta, zero])]
    rep = [_unpack_rows(t, sizes) for t in _sum_rows_adamw(parts, *packed)]
    loss = rep[0][4].reshape(())

    def leaves(big, packed_small, replicated):
        wpa_, wpp_, wo_, wpl_ = _unpack_small(packed_small, aw, pw, d, pg)
        return [big[None], replicated[0], wpl_[None], replicated[1], wpa_[None], wpp_[None], wo_[None],
                replicated[2], replicated[3]]

    out = [loss, g["d_x"][None]]
    out += leaves(grad_w_in, grad_small, rep[0])
    for i in range(3):
        out += leaves(upd_in[i], upd_small[i], rep[1 + i])
    return tuple(out)
```

```python
import math

import jax
import jax.numpy as jnp
from jax import lax
from jax.experimental import pallas as pl
from jax.experimental.pallas import tpu as pltpu

F32 = jnp.float32
BF16 = jnp.bfloat16
MESH = pl.DeviceIdType.MESH
ANY = pl.BlockSpec(memory_space=pl.ANY)

HEAD_DIM = 128
STEPS = 128
DILATIONS = (1, 4, 16)
N_GROUPS = len(DILATIONS)
POOL_WINDOWS = (2, 4, 8, 16)
POOL_HALO = 16
N_CHIPS = 4
N_DEV = 8
ALPHA = 2.0 ** 0.25
LN_EPS = 1e-5
NEG_INF = -1e30
SCORE_SCALE = HEAD_DIM ** -0.5
ADAM_LR = 0.001
ADAM_B1 = 0.9
ADAM_B2 = 0.999
ADAM_EPS = 1e-08
ADAM_WD = 0.01
ADAM_STEP = 10
MIB = 2 ** 20
NT = (((1,), (1,)), ((), ()))
TN = (((0,), (0,)), ((), ()))
DMA_STREAMS = 8


def _params(semantics=None, vmem_mib=48):
    return pltpu.CompilerParams(dimension_semantics=semantics, vmem_limit_bytes=vmem_mib * MIB)


def _divisor_tile(n, target, multiple):
    best = None
    for t in range(multiple, min(n, target) + 1, multiple):
        if n % t == 0:
            best = t
    assert best is not None, (n, target, multiple)
    return best


def _col_tile(*widths):
    g = 0
    for w in widths:
        g = math.gcd(g, w)
    return _divisor_tile(g, 1024, 128)


def _sigmoid(z):
    return jax.nn.sigmoid(z)


def _dsilu(z, sg):
    return sg * (1.0 + z * (1.0 - sg))


def _place():
    x, y, c = lax.axis_index("x"), lax.axis_index("y"), lax.axis_index("c")
    others = [(1 - x, y), (x, 1 - y), (1 - x, 1 - y)]
    return x, y, c, (x, y, 1 - c), others


def _remote(src, dst, send_sem, recv_sem, dev):
    return pltpu.make_async_remote_copy(src_ref=src, dst_ref=dst, send_sem=send_sem, recv_sem=recv_sem,
                                        device_id=dev, device_id_type=MESH)


def _row_pieces(n_rows, streams=DMA_STREAMS, multiple=16):
    size = -(-n_rows // (streams * multiple)) * multiple
    return [(lo, min(size, n_rows - lo)) for lo in range(0, n_rows, size)]


def _start_streams(make, n_rows):
    for lo, size in _row_pieces(n_rows):
        make(pl.ds(lo, size)).start()


def _half_copies(buf, send_sems, recv_sems):
    x, y, c, _, others = _place()
    half = buf.shape[1] // 2
    slab = buf.at[2 * x + y, pl.ds(c * half, half)]
    return [_remote(slab, slab, send_sems[j], recv_sems[j], (ox, oy, c)) for j, (ox, oy) in enumerate(others)]


def _halves_start(placed, after, name):
    k = N_CHIPS - 1

    def body(buf, after_ref, *refs):
        send_sems, recv_sems, token = refs[:k], refs[k:2 * k], refs[-1]
        for cp in _half_copies(buf, send_sems, recv_sems):
            cp.start()
        token[...] = jnp.zeros_like(token)

    out = pl.pallas_call(
        body, name=name,
        out_shape=[pltpu.SemaphoreType.DMA(())] * (2 * k) + [pltpu.HBM(placed.shape, placed.dtype),
                                                             jax.ShapeDtypeStruct((8, 128), F32)],
        in_specs=[HBM, ANY], out_specs=[SEM] * (2 * k) + [HBM, pl.BlockSpec(memory_space=pltpu.VMEM)],
        input_output_aliases={0: 2 * k},
        compiler_params=pltpu.CompilerParams(has_side_effects=DATAFLOW),
    )(pltpu.with_memory_space_constraint(placed, pltpu.HBM), after)
    return out[:2 * k], out[2 * k], out[-1]


def _halves_wait(sems, placed, after, name):
    k = N_CHIPS - 1

    def body(buf, *refs):
        send_sems, recv_sems = refs[:k], refs[k:2 * k]
        for cp in _half_copies(buf, send_sems, recv_sems):
            cp.wait_send()
            cp.wait_recv()

    return pl.pallas_call(
        body, name=name, out_shape=pltpu.HBM(placed.shape, placed.dtype),
        in_specs=[HBM] + [SEM] * (2 * k) + [ANY] * len(after), out_specs=HBM, input_output_aliases={0: 0},
        compiler_params=pltpu.CompilerParams(has_side_effects=DATAFLOW),
    )(placed, *sems, *after)


def _forward_copies(buf, send_sems, recv_sems):
    x, y, c, sibling, others = _place()
    half = buf.shape[1] // 2
    copies = []
    for j, (ox, oy) in enumerate(others):
        slab = buf.at[2 * ox + oy, pl.ds(c * half, half)]
        copies.append(_remote(slab, slab, send_sems[j], recv_sems[j], sibling))
    return copies


def _forward_start(buf, name):
    k = N_CHIPS - 1

    def body(b, *refs):
        send_sems, recv_sems, token = refs[:k], refs[k:2 * k], refs[-1]
        for cp in _forward_copies(b, send_sems, recv_sems):
            cp.start()
        token[...] = jnp.zeros_like(token)

    out = pl.pallas_call(
        body, name=name,
        out_shape=[pltpu.SemaphoreType.DMA(())] * (2 * k) + [pltpu.HBM(buf.shape, buf.dtype),
                                                             jax.ShapeDtypeStruct((8, 128), F32)],
        in_specs=[HBM], out_specs=[SEM] * (2 * k) + [HBM, pl.BlockSpec(memory_space=pltpu.VMEM)],
        input_output_aliases={0: 2 * k},
        compiler_params=pltpu.CompilerParams(has_side_effects=DATAFLOW),
    )(pltpu.with_memory_space_constraint(buf, pltpu.HBM))
    return out[:2 * k], out[2 * k], out[-1]


def _forward_wait(sems, buf, after, name):
    k = N_CHIPS - 1

    def body(b, *refs):
        send_sems, recv_sems = refs[:k], refs[k:2 * k]
        for cp in _forward_copies(b, send_sems, recv_sems):
            cp.wait_send()
            cp.wait_recv()

    return pl.pallas_call(
        body, name=name, out_shape=pltpu.HBM(buf.shape, buf.dtype),
        in_specs=[HBM] + [SEM] * (2 * k) + [ANY] * len(after), out_specs=HBM, input_output_aliases={0: 0},
        compiler_params=pltpu.CompilerParams(has_side_effects=DATAFLOW),
    )(buf, *sems, *after)


def _swap_halves(grads):
    n = len(grads)

    def body(*refs):
        g, theirs = refs[:n], refs[n:2 * n]
        send_sems, recv_sems = refs[2 * n:]
        x, y, c, sibling, _ = _place()
        for i in range(n):
            half = g[i].shape[1] // 2
            give = (1 - c) * half
            for b in range(N_CHIPS):
                _start_streams(lambda r, i=i, b=b: _remote(
                    g[i].at[b, pl.ds(give + r.start, r.size)], theirs[i].at[b, r], send_sems.at[i], recv_sems.at[i],
                    sibling), half)
        for i in range(n):
            _remote(theirs[i], theirs[i], send_sems.at[i], recv_sems.at[i], sibling).wait()

    return pl.pallas_call(
        body, name="swap_halves",
        out_shape=[jax.ShapeDtypeStruct((s.shape[0], s.shape[1] // 2) + s.shape[2:], s.dtype) for s in grads],
        in_specs=[ANY] * n, out_specs=[ANY] * n,
        scratch_shapes=[pltpu.SemaphoreType.DMA((n,)), pltpu.SemaphoreType.DMA((n,))],
    )(*grads)


HBM = pl.BlockSpec(memory_space=pltpu.HBM)
SEM = pl.BlockSpec(memory_space=pltpu.SEMAPHORE)
DATAFLOW = pltpu.SideEffectType.DATAFLOW_SIDE_EFFECTING


def _broadcast_copies(buf, send_sems, recv_sems):
    x, y, c, _, others = _place()
    mine = buf.at[2 * x + y]
    return [_remote(mine, mine, send_sems[j], recv_sems[j], (ox, oy, c)) for j, (ox, oy) in enumerate(others)]


def _broadcast_start(placed, after, name):
    k = N_CHIPS - 1

    def body(buf, after_ref, *refs):
        send_sems, recv_sems, token = refs[:k], refs[k:2 * k], refs[-1]
        for cp in _broadcast_copies(buf, send_sems, recv_sems):
            cp.start()
        token[...] = jnp.zeros_like(token)

    out = pl.pallas_call(
        body, name=name,
        out_shape=[pltpu.SemaphoreType.DMA(())] * (2 * k) + [pltpu.HBM(placed.shape, placed.dtype),
                                                             jax.ShapeDtypeStruct((8, 128), F32)],
        in_specs=[HBM, ANY], out_specs=[SEM] * (2 * k) + [HBM, pl.BlockSpec(memory_space=pltpu.VMEM)],
        input_output_aliases={0: 2 * k},
        compiler_params=pltpu.CompilerParams(has_side_effects=DATAFLOW),
    )(pltpu.with_memory_space_constraint(placed, pltpu.HBM), after)
    return out[:2 * k], out[2 * k], out[-1]


def _broadcast_wait(sems, placed, after, name):
    k = N_CHIPS - 1

    def body(buf, *refs):
        send_sems, recv_sems = refs[:k], refs[k:2 * k]
        for cp in _broadcast_copies(buf, send_sems, recv_sems):
            cp.wait_send()
            cp.wait_recv()

    return pl.pallas_call(
        body, name=name, out_shape=pltpu.HBM(placed.shape, placed.dtype),
        in_specs=[HBM] + [SEM] * (2 * k) + [ANY], out_specs=HBM, input_output_aliases={0: 0},
        compiler_params=pltpu.CompilerParams(has_side_effects=DATAFLOW),
    )(placed, *sems, after)


def _scatter_copies(s, got, send_sems, recv_sems):
    x, y, c, _, others = _place()
    me = 2 * x + y
    n = len(s)
    return [_remote(s[i].at[2 * ox + oy], got[i].at[me], send_sems[3 * i + j], recv_sems[3 * i + j], (ox, oy, c))
            for i in range(n) for j, (ox, oy) in enumerate(others)]


def _scatter_start(sums, placed):
    n = len(sums)
    k = 3 * n

    def body(*refs):
        s, got, token = refs[:n], refs[n:2 * n], refs[-1]
        send_sems, recv_sems = refs[2 * n:2 * n + k], refs[2 * n + k:2 * n + 2 * k]
        for cp in _scatter_copies(s, got, send_sems, recv_sems):
            cp.start()
        token[...] = jnp.zeros_like(token)

    hbm = [pltpu.HBM(a.shape, a.dtype) for a in list(sums) + list(placed)]
    out = pl.pallas_call(
        body, name="scatter_start",
        out_shape=[pltpu.SemaphoreType.DMA(())] * (2 * k) + hbm + [jax.ShapeDtypeStruct((8, 128), F32)],
        in_specs=[HBM] * (2 * n), out_specs=[SEM] * (2 * k) + [HBM] * (2 * n) + [pl.BlockSpec(memory_space=pltpu.VMEM)],
        input_output_aliases={i: 2 * k + i for i in range(2 * n)},
        compiler_params=pltpu.CompilerParams(has_side_effects=DATAFLOW),
    )(*[pltpu.with_memory_space_constraint(a, pltpu.HBM) for a in list(sums) + list(placed)])
    return out[:2 * k], out[2 * k:2 * k + n], out[2 * k + n:2 * k + 2 * n], out[-1]


def _scatter_wait(sems, sums, placed, after):
    n = len(sums)
    k = 3 * n

    def body(*refs):
        s, got = refs[:n], refs[n:2 * n]
        send_sems, recv_sems = refs[2 * n:2 * n + k], refs[2 * n + k:2 * n + 2 * k]
        for cp in _scatter_copies(s, got, send_sems, recv_sems):
            cp.wait_send()
            cp.wait_recv()

    hbm = [pltpu.HBM(a.shape, a.dtype) for a in list(sums) + list(placed)]
    out = pl.pallas_call(
        body, name="scatter_wait", out_shape=hbm,
        in_specs=[HBM] * (2 * n) + [SEM] * (2 * k) + [ANY] * len(after), out_specs=[HBM] * (2 * n),
        input_output_aliases={i: i for i in range(2 * n)},
        compiler_params=pltpu.CompilerParams(has_side_effects=DATAFLOW),
    )(*sums, *placed, *sems, *after)
    return out[n:]


def _join_copies(bufs, send_sems, recv_sems):
    x, y, c, sibling, _ = _place()
    return [_remote(b.at[c], b.at[c], send_sems[i], recv_sems[i], sibling) for i, b in enumerate(bufs)]


def _join_start(placed):
    n = len(placed)

    def body(*refs):
        bufs, send_sems, recv_sems = refs[:n], refs[n:2 * n], refs[2 * n:3 * n]
        for cp in _join_copies(bufs, send_sems, recv_sems):
            cp.start()

    hbm = [pltpu.HBM(a.shape, a.dtype) for a in placed]
    out = pl.pallas_call(
        body, name="join_start", out_shape=[pltpu.SemaphoreType.DMA(())] * (2 * n) + hbm,
        in_specs=[HBM] * n, out_specs=[SEM] * (2 * n) + [HBM] * n,
        input_output_aliases={i: 2 * n + i for i in range(n)},
        compiler_params=pltpu.CompilerParams(has_side_effects=DATAFLOW),
    )(*[pltpu.with_memory_space_constraint(a, pltpu.HBM) for a in placed])
    return out[:2 * n], out[2 * n:]


def _join_wait(sems, placed, after):
    n = len(placed)

    def body(*refs):
        bufs, send_sems, recv_sems = refs[:n], refs[n:2 * n], refs[2 * n:3 * n]
        for cp in _join_copies(bufs, send_sems, recv_sems):
            cp.wait_send()
            cp.wait_recv()

    return pl.pallas_call(
        body, name="join_wait", out_shape=[pltpu.HBM(a.shape, a.dtype) for a in placed],
        in_specs=[HBM] * n + [SEM] * (2 * n) + [ANY] * len(after), out_specs=[HBM] * n,
        input_output_aliases={i: i for i in range(n)},
        compiler_params=pltpu.CompilerParams(has_side_effects=DATAFLOW),
    )(*placed, *sems, *after)


def _gather_rows(row):
    def body(row_ref, out_ref, send_sems, recv_sems, local_sem):
        x, y, c = lax.axis_index("x"), lax.axis_index("y"), lax.axis_index("c")
        me = 4 * x + 2 * y + c
        local = pltpu.make_async_copy(row_ref, out_ref.at[me], local_sem)
        local.start()
        sent = []
        peers = []
        for k in range(1, N_DEV):
            px, py, pc = x ^ (k >> 2), y ^ ((k >> 1) & 1), c ^ (k & 1)
            peers.append((k, px, py, pc))
            cp = _remote(row_ref, out_ref.at[me], send_sems.at[k - 1], recv_sems.at[k - 1], (px, py, pc))
            cp.start()
            sent.append(cp)
        for k, px, py, pc in peers:
            slot = out_ref.at[4 * px + 2 * py + pc]
            _remote(slot, slot, send_sems.at[k - 1], recv_sems.at[k - 1], (px, py, pc)).wait_recv()
        for cp in sent:
            cp.wait_send()
        local.wait()

    return pl.pallas_call(
        body, name="gather_rows", out_shape=jax.ShapeDtypeStruct((N_DEV,) + row.shape, row.dtype),
        in_specs=[ANY], out_specs=ANY,
        scratch_shapes=[pltpu.SemaphoreType.DMA((N_DEV - 1,)), pltpu.SemaphoreType.DMA((N_DEV - 1,)),
                        pltpu.SemaphoreType.DMA],
    )(row)


def _scalar(i):
    return jnp.reshape(i, (1,)).astype(jnp.int32)


def _place_block(src, n_slots, slot, out_dtype, name, window=0, n_windows=1, after=None):
    rows, cols = src.shape[0], src.shape[1] // n_windows
    tr = _divisor_tile(rows, max(16, (2 * MIB) // (cols * 4)), 16)
    ordered = [] if after is None else [after]

    def body(slot_ref, *refs):
        s_ref, o_ref = refs[len(ordered):]
        o_ref[...] = s_ref[...].astype(o_ref.dtype)

    return pl.pallas_call(
        body, name=name, out_shape=jax.ShapeDtypeStruct((n_slots, rows, cols), out_dtype),
        grid_spec=pltpu.PrefetchScalarGridSpec(
            num_scalar_prefetch=1, grid=(rows // tr,),
            in_specs=[pl.BlockSpec(t.shape, lambda r, sl: (0, 0)) for t in ordered]
            + [pl.BlockSpec((tr, cols), lambda r, sl: (r, window))],
            out_specs=pl.BlockSpec((None, tr, cols), lambda r, sl: (sl[0], r, 0))),
        compiler_params=_params(("parallel",)))(_scalar(slot), *ordered, src)


def _add_halves(g, theirs, core, chip, name):
    n, half, cols = theirs.shape
    tr = _divisor_tile(half, max(16, (2 * MIB) // (cols * 4)), 16)
    per = half // tr

    def body(at_ref, a_ref, b_ref, o_ref, own_ref):
        total = (a_ref[...].astype(F32) + b_ref[...].astype(F32)).astype(o_ref.dtype)
        o_ref[...] = total

        @pl.when(pl.program_id(1) == at_ref[1])
        def _():
            own_ref[...] = total

    spec = pl.BlockSpec((None, tr, cols), lambda r, i, at: (i, r, 0))
    shape = jax.ShapeDtypeStruct(theirs.shape, BF16)
    return pl.pallas_call(
        body, name=name, out_shape=[shape, shape],
        grid_spec=pltpu.PrefetchScalarGridSpec(
            num_scalar_prefetch=1, grid=(per, n),
            in_specs=[pl.BlockSpec((None, tr, cols), lambda r, i, at: (i, at[0] * per + r, 0)), spec],
            out_specs=[spec, pl.BlockSpec((None, tr, cols), lambda r, i, at: (at[1], r, 0))]),
        compiler_params=_params(("parallel", "arbitrary")))(jnp.concatenate([_scalar(core), _scalar(chip)]), g, theirs)


def _sum_slots(a, core, name):
    n, rows, cols = a.shape
    tr = _divisor_tile(rows, max(16, (2 * MIB) // (cols * 4 * n)), 16)

    def body(c_ref, a_ref, o_ref):
        acc = a_ref[0].astype(F32)
        for i in range(1, n):
            acc = acc + a_ref[i].astype(F32)
        o_ref[...] = acc

    return pl.pallas_call(
        body, name=name, out_shape=jax.ShapeDtypeStruct((2, rows, cols), F32),
        grid_spec=pltpu.PrefetchScalarGridSpec(
            num_scalar_prefetch=1, grid=(rows // tr,),
            in_specs=[pl.BlockSpec((n, tr, cols), lambda r, c: (0, r, 0))],
            out_specs=pl.BlockSpec((None, tr, cols), lambda r, c: (c[0], r, 0))),
        compiler_params=_params(("parallel",)))(_scalar(core), a)


def _adamw_math(w, g, m, v):
    m = ADAM_B1 * m + (1.0 - ADAM_B1) * g
    v = ADAM_B2 * v + (1.0 - ADAM_B2) * (g * g)
    m_hat = m / (1.0 - ADAM_B1 ** ADAM_STEP)
    v_hat = v / (1.0 - ADAM_B2 ** ADAM_STEP)
    delta = -ADAM_LR * (m_hat / (jnp.sqrt(v_hat) + ADAM_EPS) + ADAM_WD * w)
    return delta, m, v


def _adamw_half(w, g2, m, v, which, prev, name):
    rows, cols = w.shape
    half = rows // 2
    tr = _divisor_tile(half, max(8, MIB // (cols * 4)), 8)
    per = half // tr

    def body(h_ref, w_ref, g_ref, m_ref, v_ref, *refs):
        d_ref, nm_ref, nv_ref = refs[-3:]
        d, nm, nv = _adamw_math(w_ref[...], g_ref[...], m_ref[...], v_ref[...])
        d_ref[...] = d
        nm_ref[...] = nm
        nv_ref[...] = nv

    spec = pl.BlockSpec((tr, cols), lambda r, h: (h[0] * per + r, 0))
    in_specs = [spec, pl.BlockSpec((None, tr, cols), lambda r, h: (h[0], r, 0)), spec, spec]
    args = [_scalar(which), w, g2, m, v]
    aliases = {}
    if prev is not None:
        aliases = {len(args) + i: i for i in range(3)}
        in_specs += [ANY] * 3
        args += list(prev)
    return pl.pallas_call(
        body, name=name, out_shape=[jax.ShapeDtypeStruct((rows, cols), F32)] * 3,
        grid_spec=pltpu.PrefetchScalarGridSpec(num_scalar_prefetch=1, grid=(per,), in_specs=in_specs,
                                               out_specs=[spec] * 3),
        input_output_aliases=aliases, compiler_params=_params(("parallel",)))(*args)


def _sum_rows_adamw(parts, w, m, v):
    def body(p_ref, w_ref, m_ref, v_ref, g_ref, d_ref, nm_ref, nv_ref):
        g = p_ref[0]
        for i in range(1, N_DEV):
            g = g + p_ref[i]
        d, nm, nv = _adamw_math(w_ref[...], g, m_ref[...], v_ref[...])
        g_ref[...] = g
        d_ref[...] = d
        nm_ref[...] = nm
        nv_ref[...] = nv

    shape = jax.ShapeDtypeStruct(w.shape, F32)
    return pl.pallas_call(body, name="sum_rows_adamw", out_shape=[shape] * 4)(parts, w, m, v)


LANES = 128


def _permute_scratch(rows, width):
    return pltpu.VMEM((width // LANES, rows, LANES), F32)


def _split_rows(value, scratch, dil):
    if dil == 1:
        return [value]
    rows = value.shape[0] // dil
    slabs = value.shape[1] // LANES
    for c in range(slabs):
        scratch[c] = value[:, c * LANES:(c + 1) * LANES]
    return [jnp.concatenate([scratch[c, pl.ds(r, rows, stride=dil), :] for c in range(slabs)], axis=1)
            for r in range(dil)]


def _merge_rows(ref, scratch, dil):
    if dil == 1:
        return ref[0].astype(F32)
    rows = ref.shape[1]
    slabs = ref.shape[2] // LANES
    for r in range(dil):
        part = ref[r].astype(F32)
        for c in range(slabs):
            scratch[c, pl.ds(r, rows, stride=dil), :] = part[:, c * LANES:(c + 1) * LANES]
    return jnp.concatenate([scratch[c] for c in range(slabs)], axis=1)


def _grouped_view(t, dil):
    return t.reshape(dil, t.shape[0] // dil, t.shape[1])


def _grouped_spec(dil, rows, width, index):
    return pl.BlockSpec((dil, rows // dil, width), index)


W_CHUNKS = 4


def _pick(values, j):
    out = values[-1]
    for i in range(len(values) - 2, -1, -1):
        out = jnp.where(j == i, values[i], out)
    return out


def _chunk_of(col, per_chip):
    return (col % per_chip) // (per_chip // W_CHUNKS)


def _w_block(col, per_chip):
    return col // per_chip, 0, (col % per_chip) % (per_chip // W_CHUNKS)


def _in_proj(xb, wc, blocks, j0, ncols, tn, out_dtype, prev, after, name):
    s, d = xb.shape
    per_chip = wc.shape[2] * W_CHUNKS // tn
    tm = _divisor_tile(s, 1024, 16)
    extra = [t for t in after if t is not None]

    def body(*refs):
        a_ref, b_ref = refs[len(extra):len(extra) + 2]
        o_ref = refs[-1]
        o_ref[...] = jnp.dot(a_ref[...], b_ref[...], preferred_element_type=F32).astype(o_ref.dtype)

    in_specs = [pl.BlockSpec(t.shape, lambda j, m: (0, 0)) for t in extra] + [
        pl.BlockSpec((tm, d), lambda j, m: (m, 0)),
        pl.BlockSpec((None, d, tn), lambda j, m: _w_block(_pick(blocks, j), per_chip))]
    args = extra + [xb, wc]
    aliases = {}
    if prev is not None:
        aliases = {len(args): 0}
        in_specs.append(ANY)
        args.append(prev)
    return pl.pallas_call(
        body, name=name, grid=(len(blocks), s // tm), in_specs=in_specs,
        out_specs=pl.BlockSpec((tm, tn), lambda j, m: (m, _pick(blocks, j) - j0)),
        out_shape=jax.ShapeDtypeStruct((s, ncols), out_dtype), input_output_aliases=aliases,
        compiler_params=_params(("parallel", "parallel")))(*args)


def _in_proj_qkv(xb, wc, g, blocks, aw, tn, prev, after, name):
    s, d = xb.shape
    dil = DILATIONS[g]
    per_chip = wc.shape[2] * W_CHUNKS // tn
    sub = aw // tn
    tm = _divisor_tile(s, 1024, 16 * dil)
    extra = [t for t in after if t is not None]

    def body(*refs):
        a_ref, b_ref = refs[len(extra):len(extra) + 2]
        o_ref, scratch = refs[-2:]
        res = jnp.dot(a_ref[...], b_ref[...], preferred_element_type=F32)
        for r, part in enumerate(_split_rows(res, scratch, dil)):
            o_ref[r] = part.astype(BF16)

    def out_index(j, m):
        col = _pick(blocks, j)
        return (col // sub) // N_GROUPS, 0, m, col % sub

    in_specs = [pl.BlockSpec(t.shape, lambda j, m: (0, 0)) for t in extra] + [
        pl.BlockSpec((tm, d), lambda j, m: (m, 0)),
        pl.BlockSpec((None, d, tn), lambda j, m: _w_block(_pick(blocks, j), per_chip))]
    args = extra + [xb, wc]
    aliases = {}
    if prev is not None:
        aliases = {len(args): 0}
        in_specs.append(ANY)
        args.append(prev)
    return pl.pallas_call(
        body, name=name, grid=(len(blocks), s // tm), in_specs=in_specs,
        out_specs=pl.BlockSpec((None, dil, tm // dil, tn), out_index),
        out_shape=jax.ShapeDtypeStruct((3, dil, s // dil, aw), BF16), input_output_aliases=aliases,
        scratch_shapes=[_permute_scratch(tm, tn)],
        compiler_params=_params(("parallel", "parallel")))(*args)


def _window_mask(first):
    qi = lax.broadcasted_iota(jnp.int32, (STEPS, 2 * STEPS), 0)
    kj = lax.broadcasted_iota(jnp.int32, (STEPS, 2 * STEPS), 1)
    lowest = jnp.where(first, STEPS, 0)
    return (kj >= qi) & (kj <= qi + STEPS) & (kj >= lowest)


def _attn_fwd(qkv, g):
    _, s, aw = qkv.shape
    heads = aw // HEAD_DIM
    n_blocks = s // STEPS
    per_seq = n_blocks // DILATIONS[g]
    pair = 2 if n_blocks % 2 == 0 else 1

    def body(q_ref, kc_ref, kp_ref, vc_ref, vp_ref, o_ref, l_ref):
        masks = [_window_mask(lax.rem(pl.program_id(0) * pair + j, per_seq) == 0) for j in range(pair)]
        for h in range(heads):
            hs = slice(h * HEAD_DIM, (h + 1) * HEAD_DIM)
            keys = jnp.concatenate([kp_ref[:, hs], kc_ref[:, hs]], axis=0)
            values = jnp.concatenate([vp_ref[:, hs], vc_ref[:, hs]], axis=0)
            for j in range(pair):
                rows = slice(j * STEPS, (j + 1) * STEPS)
                window = slice(j * STEPS, (j + 2) * STEPS)
                sc = lax.dot_general(q_ref[rows, hs], keys[window], NT, preferred_element_type=F32) * SCORE_SCALE
                sc = jnp.where(masks[j], sc, NEG_INF)
                mx = jnp.max(sc, axis=1, keepdims=True)
                e = jnp.exp(sc - mx)
                den = jnp.sum(e, axis=1, keepdims=True)
                o_ref[rows, hs] = (jnp.dot(e.astype(BF16), values[window], preferred_element_type=F32)
                                   / den).astype(BF16)
                l_ref[rows, hs] = jnp.broadcast_to(mx + jnp.log(den), (STEPS, HEAD_DIM))

    def cur(which):
        return pl.BlockSpec((None, pair * STEPS, aw), lambda b: (which, b, 0))

    def prev(which):
        return pl.BlockSpec((None, STEPS, aw), lambda b: (which, jnp.maximum(pair * b - 1, 0), 0))

    out = pl.BlockSpec((pair * STEPS, aw), lambda b: (b, 0))
    return pl.pallas_call(
        body, name=f"attn_fwd{g}", grid=(n_blocks // pair,),
        in_specs=[cur(0), cur(1), prev(1), cur(2), prev(2)], out_specs=[out, out],
        out_shape=[jax.ShapeDtypeStruct((s, aw), BF16), jax.ShapeDtypeStruct((s, aw), F32)],
        compiler_params=_params(("parallel",)))(qkv, qkv, qkv, qkv, qkv)


def _combine_groups(os, ls, zuz, aw):
    s = zuz.shape[0]
    tr = _divisor_tile(s, 256, 8 * DILATIONS[-1])

    def body(*refs):
        o_refs, l_refs, z_ref = refs[0:3], refs[3:6], refs[6]
        oo_ref, y_ref, yt_ref = refs[7:10]
        lq_refs, scratch = refs[10:13], refs[13]
        ls_ = [_merge_rows(l_refs[g], scratch, dil) for g, dil in enumerate(DILATIONS)]
        mx = jnp.maximum(jnp.maximum(ls_[0], ls_[1]), ls_[2])
        ws = [jnp.exp(l - mx) for l in ls_]
        den = ws[0] + ws[1] + ws[2]
        o = ws[0] * _merge_rows(o_refs[0], scratch, DILATIONS[0])
        for g in range(1, N_GROUPS):
            o = o + ws[g] * _merge_rows(o_refs[g], scratch, DILATIONS[g])
        o = o / den
        z = z_ref[...].astype(F32)
        y = o * (z * _sigmoid(z))
        oo_ref[...] = o.astype(BF16)
        y_ref[...] = y.astype(BF16)
        yt_ref[...] = y.T.astype(BF16)
        for g, dil in enumerate(DILATIONS):
            for r, part in enumerate(_split_rows(mx + jnp.log(den), scratch, dil)):
                lq_refs[g][r] = part

    grouped = [_grouped_spec(dil, tr, aw, lambda r: (0, r, 0)) for dil in DILATIONS]
    one = pl.BlockSpec((tr, aw), lambda r: (r, 0))
    b16 = jax.ShapeDtypeStruct((s, aw), BF16)
    out = pl.pallas_call(
        body, name="combine_groups", grid=(s // tr,),
        in_specs=grouped + grouped + [one],
        out_specs=[one, one, pl.BlockSpec((aw, tr), lambda r: (0, r))] + grouped,
        out_shape=[b16, b16, jax.ShapeDtypeStruct((aw, s), BF16)]
        + [jax.ShapeDtypeStruct((dil, s // dil, aw), F32) for dil in DILATIONS],
        scratch_shapes=[_permute_scratch(tr, aw)],
        compiler_params=_params(("parallel",)))(
            *[_grouped_view(t, dil) for t, dil in zip(os, DILATIONS)],
            *[_grouped_view(t, dil) for t, dil in zip(ls, DILATIONS)], zuz)
    return out[0], out[1], out[2], [t.reshape(s, aw) for t in out[3:]]


def _pool_counts(row0, rows, window):
    t = row0 + lax.broadcasted_iota(jnp.int32, (rows, 1), 0)
    return jnp.minimum(t + 1, window).astype(F32)


def _pool_fwd(zuz, w_pool, pool_scale, aw, pw):
    s = zuz.shape[0]
    pg = pw // len(POOL_WINDOWS)
    tr = _divisor_tile(s, 256, 128)
    u_col, z_col = aw // pw, aw // pw + 1
    assert aw % pw == 0

    def body(u_ref, up_ref, z_ref, w_ref, sc_ref, p_ref, l_ref, y_ref, yt_ref):
        r = pl.program_id(0)
        u = u_ref[...].astype(F32)
        halo = jnp.where(r > 0, up_ref[...].astype(F32), 0.0)
        ext = jnp.concatenate([halo, u], axis=0)
        pieces, lins = [], []
        for gi, window in enumerate(POOL_WINDOWS):
            cs = slice(gi * pg, (gi + 1) * pg)
            acc = ext[:, cs]
            shift = 1
            while shift < window:
                acc = acc + pltpu.roll(acc, shift, 0)
                shift *= 2
            p = acc[POOL_HALO:] / _pool_counts(r * tr, tr, window) - u[:, cs]
            pieces.append(p)
            lins.append(jnp.dot(p.astype(BF16), w_ref[gi], preferred_element_type=F32))
        p = jnp.concatenate(pieces, axis=1)
        lin = jnp.concatenate(lins, axis=1)
        z = z_ref[...].astype(F32)
        y = lin * sc_ref[...] * (z * _sigmoid(z))
        p_ref[...] = p.astype(BF16)
        l_ref[...] = lin
        y_ref[...] = y.astype(BF16)
        yt_ref[...] = y.T.astype(BF16)

    per = tr // POOL_HALO
    out = pl.BlockSpec((tr, pw), lambda r: (r, 0))
    return pl.pallas_call(
        body, name="pool_fwd", grid=(s // tr,),
        in_specs=[pl.BlockSpec((tr, pw), lambda r: (r, u_col)),
                  pl.BlockSpec((POOL_HALO, pw), lambda r: (jnp.maximum(r * per - 1, 0), u_col)),
                  pl.BlockSpec((tr, pw), lambda r: (r, z_col)),
                  pl.BlockSpec((len(POOL_WINDOWS), pg, pg), lambda r: (0, 0, 0)),
                  pl.BlockSpec((1, pw), lambda r: (0, 0))],
        out_specs=[out, out, out, pl.BlockSpec((pw, tr), lambda r: (0, r))],
        out_shape=[jax.ShapeDtypeStruct((s, pw), BF16), jax.ShapeDtypeStruct((s, pw), F32),
                   jax.ShapeDtypeStruct((s, pw), BF16), jax.ShapeDtypeStruct((pw, s), BF16)],
        compiler_params=_params(("parallel",)))(zuz, zuz, zuz, w_pool, pool_scale)


def _proj_merge(y_attn, y_pool, wpa4, wpp4, gpre, b_gate):
    s, aw = y_attn.shape
    pw = y_pool.shape[1]
    tn = wpa4.shape[2]
    d = N_CHIPS * tn
    tm = _divisor_tile(s, 512, 128)

    def body(ya_ref, yp_ref, wa_ref, wp_ref, ga_ref, gp_ref, ba_ref, bp_ref, a_ref, p_ref, sa_ref, sp_ref, m_ref,
             mt_ref):
        a = jnp.dot(ya_ref[...], wa_ref[...], preferred_element_type=F32)
        p = jnp.dot(yp_ref[...], wp_ref[...], preferred_element_type=F32)
        sa = _sigmoid(ga_ref[...].astype(F32) + ba_ref[...])
        sp = _sigmoid(gp_ref[...].astype(F32) + bp_ref[...])
        merged = sa * a + sp * p
        a_ref[...] = a.astype(BF16)
        p_ref[...] = p.astype(BF16)
        sa_ref[...] = sa.astype(BF16)
        sp_ref[...] = sp.astype(BF16)
        m_ref[...] = merged.astype(BF16)
        mt_ref[...] = merged.T.astype(BF16)

    out = pl.BlockSpec((tm, tn), lambda n, m: (m, n))
    f = jax.ShapeDtypeStruct((s, d), BF16)
    return pl.pallas_call(
        body, name="proj_merge", grid=(N_CHIPS, s // tm),
        in_specs=[pl.BlockSpec((tm, aw), lambda n, m: (m, 0)), pl.BlockSpec((tm, pw), lambda n, m: (m, 0)),
                  pl.BlockSpec((None, aw, tn), lambda n, m: (n, 0, 0)),
                  pl.BlockSpec((None, pw, tn), lambda n, m: (n, 0, 0)),
                  pl.BlockSpec((tm, tn), lambda n, m: (m, n)), pl.BlockSpec((tm, tn), lambda n, m: (m, N_CHIPS + n)),
                  pl.BlockSpec((1, tn), lambda n, m: (0, n)), pl.BlockSpec((1, tn), lambda n, m: (0, N_CHIPS + n))],
        out_specs=[out] * 5 + [pl.BlockSpec((tn, tm), lambda n, m: (n, m))],
        out_shape=[f] * 5 + [jax.ShapeDtypeStruct((d, s), BF16)],
        compiler_params=_params(("parallel", "parallel")))(y_attn, y_pool, wpa4, wpp4, gpre, gpre, b_gate, b_gate)


def _out_norm_loss(merged, w_out, x, target, gamma, beta):
    s, d = x.shape
    tm = _divisor_tile(s, 256, 16)

    def body(m_ref, w_ref, x_ref, t_ref, g_ref, b_ref, dr_ref, drb_ref, loss_ref, dg_ref, db_ref):
        @pl.when(pl.program_id(0) == 0)
        def _():
            loss_ref[...] = jnp.zeros_like(loss_ref)
            dg_ref[...] = jnp.zeros_like(dg_ref)
            db_ref[...] = jnp.zeros_like(db_ref)

        r = ALPHA * x_ref[...] + jnp.dot(m_ref[...], w_ref[...], preferred_element_type=F32)
        mu = jnp.mean(r, axis=1, keepdims=True)
        rc = r - mu
        rstd = lax.rsqrt(jnp.mean(rc * rc, axis=1, keepdims=True) + LN_EPS)
        xhat = rc * rstd
        diff = xhat * g_ref[...] + b_ref[...] - t_ref[...]
        dy = diff / d
        loss_ref[...] += jnp.sum(diff * diff, axis=0, keepdims=True)
        dg_ref[...] += jnp.sum(dy * xhat, axis=0, keepdims=True)
        db_ref[...] += jnp.sum(dy, axis=0, keepdims=True)
        dxhat = dy * g_ref[...]
        dr = rstd * (dxhat - jnp.mean(dxhat, axis=1, keepdims=True)
                     - xhat * jnp.mean(dxhat * xhat, axis=1, keepdims=True))
        dr_ref[...] = dr
        drb_ref[...] = dr.astype(BF16)

    row = pl.BlockSpec((tm, d), lambda m: (m, 0))
    vec = pl.BlockSpec((1, d), lambda m: (0, 0))
    v = jax.ShapeDtypeStruct((1, d), F32)
    return pl.pallas_call(
        body, name="out_norm_loss", grid=(s // tm,),
        in_specs=[row, pl.BlockSpec((d, d), lambda m: (0, 0)), row, row, vec, vec],
        out_specs=[row, row, vec, vec, vec],
        out_shape=[jax.ShapeDtypeStruct((s, d), F32), jax.ShapeDtypeStruct((s, d), BF16), v, v, v],
        compiler_params=_params(("arbitrary",), vmem_mib=56))(merged, w_out, x, target, gamma, beta)


def _merge_bwd(drb, w_out, a, p, sa, sp):
    s, d = drb.shape
    tm = _divisor_tile(s, 512, 16)
    tn = d // N_CHIPS

    def body(dr_ref, w_ref, a_ref, p_ref, sa_ref, sp_ref, da_ref, dp_ref, dga_ref, dgp_ref, dba_ref, dbp_ref):
        @pl.when(pl.program_id(1) == 0)
        def _():
            dba_ref[...] = jnp.zeros_like(dba_ref)
            dbp_ref[...] = jnp.zeros_like(dbp_ref)

        dm = lax.dot_general(dr_ref[...], w_ref[...], NT, preferred_element_type=F32)
        sa = sa_ref[...].astype(F32)
        sp = sp_ref[...].astype(F32)
        da_ref[...] = (dm * sa).astype(BF16)
        dp_ref[...] = (dm * sp).astype(BF16)
        dga = dm * a_ref[...].astype(F32) * sa * (1.0 - sa)
        dgp = dm * p_ref[...].astype(F32) * sp * (1.0 - sp)
        dga_ref[...] = dga.astype(BF16)
        dgp_ref[...] = dgp.astype(BF16)
        dba_ref[...] += jnp.sum(dga, axis=0, keepdims=True)
        dbp_ref[...] += jnp.sum(dgp, axis=0, keepdims=True)

    blk = pl.BlockSpec((tm, tn), lambda n, m: (m, n))
    vec = pl.BlockSpec((1, tn), lambda n, m: (0, n))
    b16 = jax.ShapeDtypeStruct((s, d), BF16)
    v = jax.ShapeDtypeStruct((1, d), F32)
    return pl.pallas_call(
        body, name="merge_bwd", grid=(N_CHIPS, s // tm),
        in_specs=[pl.BlockSpec((tm, d), lambda n, m: (m, 0)), pl.BlockSpec((tn, d), lambda n, m: (n, 0)),
                  blk, blk, blk, blk],
        out_specs=[blk, blk, blk, blk, vec, vec], out_shape=[b16, b16, b16, b16, v, v],
        compiler_params=_params(("parallel", "arbitrary")))(drb, w_out, a, p, sa, sp)


def _proj_t(dy_ref, w_ref, tn):
    acc = None
    for n in range(N_CHIPS):
        t = lax.dot_general(dy_ref[:, n * tn:(n + 1) * tn], w_ref[n], NT, preferred_element_type=F32)
        acc = t if acc is None else acc + t
    return acc


def _attn_gate_bwd(da, wpa4, zuz, o):
    s, d = da.shape
    aw, tn = wpa4.shape[1], wpa4.shape[2]
    heads = aw // HEAD_DIM
    tm = _divisor_tile(s, 256, 16 * DILATIONS[-1])

    def body(*refs):
        da_ref, w_ref, z_ref, o_ref, dz_ref = refs[:5]
        do_refs, dd_refs, scratch = refs[5:8], refs[8:11], refs[11]
        dy = _proj_t(da_ref, w_ref, tn)
        z, o = z_ref[...].astype(F32), o_ref[...].astype(F32)
        sg = _sigmoid(z)
        do = dy * (z * sg)
        dz_ref[...] = (dy * o * _dsilu(z, sg)).astype(BF16)
        prod = do * o
        dd = jnp.concatenate(
            [jnp.broadcast_to(jnp.sum(prod[:, h * HEAD_DIM:(h + 1) * HEAD_DIM], axis=1, keepdims=True),
                              (tm, HEAD_DIM)) for h in range(heads)], axis=1)
        for g, dil in enumerate(DILATIONS):
            for r, part in enumerate(_split_rows(do, scratch, dil)):
                do_refs[g][r] = part.astype(BF16)
            for r, part in enumerate(_split_rows(dd, scratch, dil)):
                dd_refs[g][r] = part

    row = pl.BlockSpec((tm, aw), lambda m: (m, 0))
    grouped = [_grouped_spec(dil, tm, aw, lambda m: (0, m, 0)) for dil in DILATIONS]
    out = pl.pallas_call(
        body, name="attn_gate_bwd", grid=(s // tm,),
        in_specs=[pl.BlockSpec((tm, d), lambda m: (m, 0)), pl.BlockSpec((N_CHIPS, aw, tn), lambda m: (0, 0, 0)),
                  row, row],
        out_specs=[row] + grouped + grouped,
        out_shape=[jax.ShapeDtypeStruct((s, aw), BF16)]
        + [jax.ShapeDtypeStruct((dil, s // dil, aw), BF16) for dil in DILATIONS]
        + [jax.ShapeDtypeStruct((dil, s // dil, aw), F32) for dil in DILATIONS],
        scratch_shapes=[_permute_scratch(tm, aw)],
        compiler_params=_params(("parallel",)))(da, wpa4, zuz, o)
    return out[0], [t.reshape(s, aw) for t in out[1:4]], [t.reshape(s, aw) for t in out[4:7]]


def _pool_gate_bwd(dp_in, wpp4, zuz, lin, pooled, w_pool, pool_scale, aw):
    s, d = dp_in.shape
    pw, tn = wpp4.shape[1], wpp4.shape[2]
    n_win = len(POOL_WINDOWS)
    pg = pw // n_win
    tm = _divisor_tile(s, 256, 16)
    z_col = aw // pw + 1

    def body(dp_ref, w_ref, z_ref, l_ref, p_ref, wp_ref, sc_ref, dz_ref, dpo_ref, dw_ref, ds_ref):
        @pl.when(pl.program_id(0) == 0)
        def _():
            dw_ref[...] = jnp.zeros_like(dw_ref)
            ds_ref[...] = jnp.zeros_like(ds_ref)

        dy = _proj_t(dp_ref, w_ref, tn)
        z, lin_ = z_ref[...].astype(F32), l_ref[...]
        sg = _sigmoid(z)
        dypp = dy * (z * sg)
        dz_ref[...] = (dy * (lin_ * sc_ref[...]) * _dsilu(z, sg)).astype(BF16)
        ds_ref[...] += jnp.sum(dypp * lin_, axis=0, keepdims=True)
        dlin = (dypp * sc_ref[...]).astype(BF16)
        for gi in range(n_win):
            cs = slice(gi * pg, (gi + 1) * pg)
            dw_ref[gi] += lax.dot_general(p_ref[:, cs], dlin[:, cs], TN, preferred_element_type=F32)
            dpo_ref[:, cs] = lax.dot_general(dlin[:, cs], wp_ref[gi], NT, preferred_element_type=F32)

    row = pl.BlockSpec((tm, pw), lambda m: (m, 0))
    return pl.pallas_call(
        body, name="pool_gate_bwd", grid=(s // tm,),
        in_specs=[pl.BlockSpec((tm, d), lambda m: (m, 0)), pl.BlockSpec((N_CHIPS, pw, tn), lambda m: (0, 0, 0)),
                  pl.BlockSpec((tm, pw), lambda m: (m, z_col)), row, row,
                  pl.BlockSpec((n_win, pg, pg), lambda m: (0, 0, 0)), pl.BlockSpec((1, pw), lambda m: (0, 0))],
        out_specs=[row, row, pl.BlockSpec((n_win, pg, pg), lambda m: (0, 0, 0)),
                   pl.BlockSpec((1, pw), lambda m: (0, 0))],
        out_shape=[jax.ShapeDtypeStruct((s, pw), BF16), jax.ShapeDtypeStruct((s, pw), F32),
                   jax.ShapeDtypeStruct((n_win, pg, pg), F32), jax.ShapeDtypeStruct((1, pw), F32)],
        compiler_params=_params(("arbitrary",)))(dp_in, wpp4, zuz, lin, pooled, w_pool, pool_scale)


def _pool_bwd(dpooled):
    s, pw = dpooled.shape
    pg = pw // len(POOL_WINDOWS)
    tr = _divisor_tile(s, 256, POOL_HALO)
    per = tr // POOL_HALO
    n_tiles = s // tr

    def body(c_ref, n_ref, du_ref):
        r = pl.program_id(0)
        cur = c_ref[...]
        halo = jnp.where(r < n_tiles - 1, n_ref[...], 0.0)
        ext = jnp.concatenate([cur, halo], axis=0)
        rows = tr + POOL_HALO
        for gi, window in enumerate(POOL_WINDOWS):
            cs = slice(gi * pg, (gi + 1) * pg)
            acc = ext[:, cs] / _pool_counts(r * tr, rows, window)
            shift = 1
            while shift < window:
                acc = acc + pltpu.roll(acc, rows - shift, 0)
                shift *= 2
            du_ref[:, cs] = (acc[:tr] - cur[:, cs]).astype(BF16)

    return pl.pallas_call(
        body, name="pool_bwd", grid=(n_tiles,),
        in_specs=[pl.BlockSpec((tr, pw), lambda r: (r, 0)),
                  pl.BlockSpec((POOL_HALO, pw), lambda r: (jnp.minimum((r + 1) * per, s // POOL_HALO - 1), 0))],
        out_specs=pl.BlockSpec((tr, pw), lambda r: (r, 0)),
        out_shape=jax.ShapeDtypeStruct((s, pw), BF16), compiler_params=_params(("parallel",)))(dpooled, dpooled)


def _attn_bwd(qkv, do, lse, dd, g):
    _, s, aw = qkv.shape
    heads = aw // HEAD_DIM
    n_blocks = s // STEPS
    per_seq = n_blocks // DILATIONS[g]
    pair = 2 if n_blocks % 2 == 0 else 1
    rows_ = pair * STEPS
    n_steps = n_blocks // pair
    tail = slice(rows_ - STEPS, rows_)

    def body(q_ref, do_ref, l_ref, dd_ref, kc_ref, kp_ref, vc_ref, vp_ref, out_ref, cq_ref, ck_ref, cv_ref):
        b = pl.program_id(0)

        @pl.when(b == 0)
        def _():
            cq_ref[...] = jnp.zeros_like(cq_ref)
            ck_ref[...] = jnp.zeros_like(ck_ref)
            cv_ref[...] = jnp.zeros_like(cv_ref)

        out_ref[0] = cq_ref[...].astype(BF16)

        @pl.when(b < n_steps)
        def _():
            masks = [_window_mask(lax.rem(b * pair + j, per_seq) == 0) for j in range(pair)]
            for h in range(heads):
                hs = slice(h * HEAD_DIM, (h + 1) * HEAD_DIM)
                keys = jnp.concatenate([kp_ref[:, hs], kc_ref[:, hs]], axis=0)
                values = jnp.concatenate([vp_ref[:, hs], vc_ref[:, hs]], axis=0)
                dks, dvs = [], []
                for j in range(pair):
                    rows = slice(j * STEPS, (j + 1) * STEPS)
                    window = slice(j * STEPS, (j + 2) * STEPS)
                    q, do_, kk, vv = q_ref[rows, hs], do_ref[rows, hs], keys[window], values[window]
                    lse_ = jnp.concatenate([l_ref[rows, hs], l_ref[rows, hs]], axis=1)
                    dd_ = jnp.concatenate([dd_ref[rows, hs], dd_ref[rows, hs]], axis=1)
                    sc = lax.dot_general(q, kk, NT, preferred_element_type=F32) * SCORE_SCALE
                    prob = jnp.where(masks[j], jnp.exp(sc - lse_), 0.0)
                    dprob = lax.dot_general(do_, vv, NT, preferred_element_type=F32)
                    dsc = prob * (dprob - dd_) * SCORE_SCALE
                    cq_ref[rows, hs] = jnp.dot(dsc.astype(BF16), kk, preferred_element_type=F32)
                    dks.append(lax.dot_general(dsc.astype(BF16), q, TN, preferred_element_type=F32))
                    dvs.append(lax.dot_general(prob.astype(BF16), do_, TN, preferred_element_type=F32))
                for which, carry, parts in ((1, ck_ref, dks), (2, cv_ref, dvs)):
                    out_ref[which, tail, hs] = (carry[tail, hs] + parts[0][:STEPS]).astype(BF16)
                    if pair > 1:
                        out_ref[which, :rows_ - STEPS, hs] = carry[:rows_ - STEPS, hs].astype(BF16)
                    for j in range(pair):
                        total = parts[j][STEPS:]
                        if j + 1 < pair:
                            total = total + parts[j + 1][:STEPS]
                        carry[j * STEPS:(j + 1) * STEPS, hs] = total

        @pl.when(b == n_steps)
        def _():
            out_ref[1] = ck_ref[...].astype(BF16)
            out_ref[2] = cv_ref[...].astype(BF16)

    last = n_steps - 1

    def cur(which):
        return pl.BlockSpec((None, rows_, aw), lambda b: (which, jnp.minimum(b, last), 0))

    def prev(which):
        return pl.BlockSpec((None, STEPS, aw), lambda b: (which, jnp.clip(b * pair - 1, 0, n_blocks - 1), 0))

    row = pl.BlockSpec((rows_, aw), lambda b: (jnp.minimum(b, last), 0))
    return pl.pallas_call(
        body, name=f"attn_bwd{g}", grid=(n_steps + 1,),
        in_specs=[cur(0), row, row, row, cur(1), prev(1), cur(2), prev(2)],
        out_specs=pl.BlockSpec((3, rows_, aw), lambda b: (0, jnp.clip(b - 1, 0, last), 0)),
        out_shape=jax.ShapeDtypeStruct((3, s, aw), BF16),
        scratch_shapes=[pltpu.VMEM((rows_, aw), F32)] * 3,
        compiler_params=_params(("arbitrary",)))(qkv, do, lse, dd, qkv, qkv, qkv, qkv)


def _weight_grad(at, b, tn, col_blocks, name):
    m, k = at.shape
    n = b.shape[1]
    tm = _divisor_tile(m, 1024, 16)
    tk = _divisor_tile(k, 2048, 128)
    nk = k // tk

    def body(a_ref, b_ref, o_ref, acc_ref):
        kk = pl.program_id(2)

        @pl.when(kk == 0)
        def _():
            acc_ref[...] = jnp.zeros_like(acc_ref)

        acc_ref[...] += jnp.dot(a_ref[...], b_ref[...], preferred_element_type=F32)

        @pl.when(kk == nk - 1)
        def _():
            o_ref[...] = acc_ref[...].astype(BF16)

    if col_blocks:
        out_spec = pl.BlockSpec((None, tm, tn), lambda i, j, kk: (j, i, 0))
        out_shape = jax.ShapeDtypeStruct((n // tn, m, tn), BF16)
    else:
        out_spec = pl.BlockSpec((tm, tn), lambda i, j, kk: (i, j))
        out_shape = jax.ShapeDtypeStruct((m, n), BF16)
    return pl.pallas_call(
        body, name=name, grid=(m // tm, n // tn, nk),
        in_specs=[pl.BlockSpec((tm, tk), lambda i, j, kk: (i, kk)), pl.BlockSpec((tk, tn), lambda i, j, kk: (kk, j))],
        out_specs=out_spec, out_shape=out_shape, scratch_shapes=[pltpu.VMEM((tm, tn), F32)],
        compiler_params=_params(("parallel", "parallel", "arbitrary")))(at, b)


def _w_in_grad_part(xt, b, col_of, n_local, tn, w_shape, prev, name):
    d, s = xt.shape
    per_chip = w_shape[2] // tn
    tm = _divisor_tile(d, 1024, 16)
    tk = _divisor_tile(s, 2048, 128)
    nk = s // tk

    def body(*refs):
        a_ref, b_ref, o_ref, acc_ref = refs[0], refs[1], refs[-2], refs[-1]
        kk = pl.program_id(2)

        @pl.when(kk == 0)
        def _():
            acc_ref[...] = jnp.zeros_like(acc_ref)

        acc_ref[...] += jnp.dot(a_ref[...], b_ref[...], preferred_element_type=F32)

        @pl.when(kk == nk - 1)
        def _():
            o_ref[...] = acc_ref[...].astype(BF16)

    if b.ndim == 3:
        sub = b.shape[2] // tn
        b_spec = pl.BlockSpec((None, tk, tn), lambda j, i, kk: (j // sub, kk, j % sub))
    else:
        b_spec = pl.BlockSpec((tk, tn), lambda j, i, kk: (kk, j))
    in_specs = [pl.BlockSpec((tm, tk), lambda j, i, kk: (i, kk)), b_spec]
    args = [xt, b]
    aliases = {}
    if prev is not None:
        in_specs.append(ANY)
        args.append(prev)
        aliases = {2: 0}
    return pl.pallas_call(
        body, name=name, grid=(n_local, d // tm, nk), in_specs=in_specs,
        out_specs=pl.BlockSpec((None, tm, tn), lambda j, i, kk: (col_of(j) // per_chip, i, col_of(j) % per_chip)),
        out_shape=jax.ShapeDtypeStruct(w_shape, BF16), scratch_shapes=[pltpu.VMEM((tm, tn), F32)],
        input_output_aliases=aliases,
        compiler_params=_params(("parallel", "parallel", "arbitrary")))(*args)


def _assemble_w(wcs, after):
    n, d, wc = wcs[0].shape
    tr = _divisor_tile(d, 256, 16)

    def body(after_ref, *refs):
        o_ref = refs[-1]
        for ch in range(W_CHUNKS):
            o_ref[:, ch * wc:(ch + 1) * wc] = refs[ch][...]

    return pl.pallas_call(
        body, name="assemble_w", grid=(n, d // tr),
        in_specs=[pl.BlockSpec(after.shape, lambda b, r: (0, 0))]
        + [pl.BlockSpec((None, tr, wc), lambda b, r: (b, r, 0))] * W_CHUNKS,
        out_specs=pl.BlockSpec((None, tr, W_CHUNKS * wc), lambda b, r: (b, r, 0)),
        out_shape=jax.ShapeDtypeStruct((n, d, W_CHUNKS * wc), wcs[0].dtype),
        compiler_params=_params(("parallel", "parallel")))(after, *wcs)


def _x_grad(dqkv, rest, w4, dr, aw, tn):
    s, d = dr.shape
    sub = aw // tn
    n_qkv = 3 * N_GROUPS * sub
    los, lo = [], n_qkv
    for p in rest:
        los.append(lo)
        lo += p.shape[1] // tn
    n_blocks = lo
    per_chip = n_blocks // N_CHIPS
    tm = _divisor_tile(s, 512, 16 * DILATIONS[-1])

    def body(*refs):
        q_refs, r_refs = refs[:N_GROUPS], refs[N_GROUPS:N_GROUPS + len(rest)]
        w_ref, dr_ref, o_ref, acc_ref, scratch = refs[-5:]
        j = pl.program_id(1)

        @pl.when(j == 0)
        def _():
            acc_ref[...] = ALPHA * dr_ref[...]

        for g, dil in enumerate(DILATIONS):
            @pl.when((j < n_qkv) & (lax.rem(j // sub, N_GROUPS) == g))
            def _(g=g, dil=dil):
                rows = _merge_rows(q_refs[g], scratch, dil).astype(BF16)
                acc_ref[...] += lax.dot_general(rows, w_ref[...], NT, preferred_element_type=F32)

        for p_ref, lo_, piece in zip(r_refs, los, rest):
            @pl.when((j >= lo_) & (j < lo_ + piece.shape[1] // tn))
            def _(p_ref=p_ref):
                acc_ref[...] += lax.dot_general(p_ref[...], w_ref[...], NT, preferred_element_type=F32)

        @pl.when(j == n_blocks - 1)
        def _():
            o_ref[...] = acc_ref[...]

    def qkv_spec(dil):
        def index(i, j):
            region = jnp.minimum(j // sub, 3 * N_GROUPS - 1)
            return region // N_GROUPS, 0, i, jnp.where(j < n_qkv, j % sub, 0)

        return pl.BlockSpec((None, dil, tm // dil, tn), index)

    def rest_spec(lo_, piece):
        n = piece.shape[1] // tn
        return pl.BlockSpec((tm, tn), lambda i, j: (i, jnp.clip(j - lo_, 0, n - 1)))

    row = pl.BlockSpec((tm, d), lambda i, j: (i, 0))
    return pl.pallas_call(
        body, name="x_grad", grid=(s // tm, n_blocks),
        in_specs=[qkv_spec(dil) for dil in DILATIONS] + [rest_spec(lo_, p) for lo_, p in zip(los, rest)]
        + [pl.BlockSpec((None, d, tn), lambda i, j: (j // per_chip, 0, j % per_chip)), row],
        out_specs=row, out_shape=jax.ShapeDtypeStruct((s, d), F32),
        scratch_shapes=[pltpu.VMEM((tm, d), F32), _permute_scratch(tm, tn)],
        compiler_params=_params(("parallel", "arbitrary"), vmem_mib=56))(
            *[t.reshape(3, dil, s // dil, aw) for t, dil in zip(dqkv, DILATIONS)], *rest, w4, dr)


def _prepare_x(x, after=None):
    s, d = x.shape
    tc = 2 * LANES
    slabs = tc // LANES
    ordered = [] if after is None else [after]

    def body(*refs):
        x_ref, xb_ref = refs[len(ordered):len(ordered) + 2]
        xt_refs, scratch = refs[len(ordered) + 2:len(ordered) + 2 + N_GROUPS], refs[-1]
        t = x_ref[...]
        xb_ref[...] = t.astype(BF16)
        for c in range(slabs):
            scratch[c] = t[:, c * LANES:(c + 1) * LANES]
        for g, dil in enumerate(DILATIONS):
            length = s // dil
            for r in range(dil):
                part = t if dil == 1 else jnp.concatenate(
                    [scratch[c, pl.ds(r, length, stride=dil), :] for c in range(slabs)], axis=1)
                xt_refs[g][:, r * length:(r + 1) * length] = part.T.astype(BF16)

    col = pl.BlockSpec((s, tc), lambda j: (0, j))
    row = pl.BlockSpec((tc, s), lambda j: (j, 0))
    t_shape = jax.ShapeDtypeStruct((d, s), BF16)
    out = pl.pallas_call(
        body, name="prepare_x", grid=(d // tc,),
        in_specs=[pl.BlockSpec(t.shape, lambda j: (0, 0)) for t in ordered] + [col],
        out_specs=[col] + [row] * N_GROUPS,
        out_shape=[jax.ShapeDtypeStruct((s, d), BF16)] + [t_shape] * N_GROUPS,
        scratch_shapes=[_permute_scratch(s, tc)], compiler_params=_params(("parallel",)))(*ordered, x)
    return out[0], out[1:]


def _local_step(x, target, w_open, w_close, w_width, b_gate, pool_scale, gamma, beta, aw, pw, small_weights,
                start_exchange=None, first_token=None):
    s, d = x.shape
    tn = _col_tile(aw, pw, w_width)
    sub = aw // tn
    per_chip = w_width // tn
    qkv_w = 3 * N_GROUPS * aw
    w_shape = (N_CHIPS, d, w_width)

    regions = [dict(kind=g, blocks=[(which * N_GROUPS + g) * sub + i for which in range(3) for i in range(sub)])
               for g in range(N_GROUPS)]
    lo = qkv_w // tn
    for name, width in (("zuz", aw + 2 * pw), ("gates", 2 * d)):
        regions.append(dict(kind=name, blocks=list(range(lo, lo + width // tn)), j0=lo, width=width))
        lo += width // tn
    results = [None] * len(regions)
    xb, xts = _prepare_x(x, first_token)
    wcs, last = [], []
    w_open(0, [xb])
    for ch in range(W_CHUNKS):
        wc, token = w_close(ch, last)
        wcs.append(wc)
        calls = []
        for i, region in enumerate(regions):
            blocks = [b for b in region["blocks"] if _chunk_of(b, per_chip) == ch]
            if blocks:
                calls.append((i, region, blocks))
        done = []
        for k, (i, region, blocks) in enumerate(calls):
            after = [token]
            if k == len(calls) - 1 and ch + 1 < W_CHUNKS:
                after.append(w_open(ch + 1, done))
            if region["kind"] in range(N_GROUPS):
                results[i] = _in_proj_qkv(xb, wc, region["kind"], blocks, aw, tn, results[i], after,
                                          f"in_proj_qkv{region['kind']}_{ch}")
            else:
                results[i] = _in_proj(xb, wc, blocks, region["j0"], region["width"], tn, BF16, results[i], after,
                                      f"in_proj_{region['kind']}_{ch}")
            done.append(results[i])
        last = done[-1:]
    qkv = [results[g].reshape(3, s, aw) for g in range(N_GROUPS)]
    zuz, gpre = results[N_GROUPS], results[N_GROUPS + 1]

    attn = [_attn_fwd(qkv[g], g) for g in range(N_GROUPS)]
    o, y_attn, y_attn_t, lse = _combine_groups([a[0] for a in attn], [a[1] for a in attn], zuz, aw)
    w_pool, wpa4, wpp4, w_out = small_weights(o)
    pooled, lin, y_pool, y_pool_t = _pool_fwd(zuz, w_pool, pool_scale, aw, pw)
    a, p, sa, sp, merged, merged_t = _proj_merge(y_attn, y_pool, wpa4, wpp4, gpre, b_gate)
    dr, drb, loss_lanes, d_gamma, d_beta = _out_norm_loss(merged, w_out, x, target, gamma, beta)

    da, dp, d_gpre_a, d_gpre_p, d_b_a, d_b_p = _merge_bwd(drb, w_out, a, p, sa, sp)
    d_b_gate = jnp.concatenate([d_b_a, d_b_p], axis=1)
    d_w_out = _weight_grad(merged_t, drb, d // N_CHIPS, False, "w_out_grad")
    d_wpa4 = _weight_grad(y_attn_t, da, d // N_CHIPS, True, "w_proj_attn_grad")
    d_wpp4 = _weight_grad(y_pool_t, dp, d // N_CHIPS, True, "w_proj_pool_grad")
    d_z_attn, d_o, dd = _attn_gate_bwd(da, wpa4, zuz, o)
    d_z_pool, d_pooled, d_w_pool, d_pool_scale = _pool_gate_bwd(dp, wpp4, zuz, lin, pooled, w_pool, pool_scale, aw)
    d_u = _pool_bwd(d_pooled)
    dqkv = [_attn_bwd(qkv[g], d_o[g], lse[g], dd[g], g) for g in range(N_GROUPS)]

    rest = [d_z_attn, d_u, d_z_pool, d_gpre_a, d_gpre_p]
    d_w_in4 = None
    for g in range(N_GROUPS):
        d_w_in4 = _w_in_grad_part(xts[g], dqkv[g], lambda j, g=g: ((j // sub) * N_GROUPS + g) * sub + j % sub,
                                  3 * sub, tn, w_shape, d_w_in4, f"w_in_grad_qkv{g}")
    lo = qkv_w // tn
    for i, piece in enumerate(rest):
        n_local = piece.shape[1] // tn
        d_w_in4 = _w_in_grad_part(xts[0], piece, lambda j, lo=lo: lo + j, n_local, tn, w_shape, d_w_in4,
                                  f"w_in_grad_rest{i}")
        lo += n_local
    grads = dict(loss_lanes=loss_lanes, w_in=d_w_in4, b_gate=d_b_gate, w_pool=d_w_pool,
                 pool_scale=d_pool_scale, w_proj_attn=d_wpa4, w_proj_pool=d_wpp4, w_out=d_w_out,
                 ln_gamma=d_gamma, ln_beta=d_beta)
    token = jnp.zeros((8, 128), F32) if start_exchange is None else start_exchange(grads)
    grads["d_x"] = _x_grad(dqkv, rest, _assemble_w(wcs, token), dr, aw, tn)
    return grads


def _pack_small(wpa, wpp, w_out, w_pool):
    width = wpa.shape[1]
    return jnp.concatenate([wpa, wpp, w_out.reshape(-1, width), w_pool.reshape(-1, width)], axis=0)


def _unpack_small(packed, aw, pw, d, pg):
    lead = packed.shape[:-2]
    width = d // N_CHIPS
    r0, r1, r2 = aw, aw + pw, aw + pw + d
    return (packed[..., :r0, :], packed[..., r0:r1, :], packed[..., r1:r2, :].reshape(lead + (width, d)),
            packed[..., r2:, :].reshape(lead + (len(POOL_WINDOWS), pg // N_CHIPS, pg)))


def _pack_rows(vectors, rows):
    flat = jnp.concatenate([v.reshape(-1) for v in vectors])
    return jnp.pad(flat, (0, rows * 128 - flat.shape[0])).reshape(rows, 128)


def _unpack_rows(packed, sizes):
    flat, out, lo = packed.reshape(-1), [], 0
    for n in sizes:
        out.append(flat[lo:lo + n].reshape(1, n))
        lo += n
    return out


def kernel(x, w_in, b_gate, w_pool, pool_scale, w_proj_attn, w_proj_pool, w_out, ln_gamma, ln_beta, loss_target, m_w_in, m_b_gate, m_w_pool, m_pool_scale, m_w_proj_attn, m_w_proj_pool, m_w_out, m_ln_gamma, m_ln_beta, v_w_in, v_b_gate, v_w_pool, v_pool_scale, v_w_proj_attn, v_w_proj_pool, v_w_out, v_ln_gamma, v_ln_beta):
    s, d = x.shape[1], x.shape[2]
    aw, pw = w_proj_attn.shape[1], w_proj_pool.shape[1]
    pg = w_pool.shape[3]
    n_win = len(POOL_WINDOWS)

    def small(wpa, wpp, wo, wpl):
        return _pack_small(wpa[0], wpp[0], wo[0], wpl[0])

    chip = 2 * lax.axis_index("x") + lax.axis_index("y")
    core = lax.axis_index("c")

    flight = {"chunk": _halves_start(_place_block(w_in[0], N_CHIPS, chip, BF16, "place_w_in0", 0, W_CHUNKS), x,
                                     "gather_w_in0_start")}
    first_token = flight["chunk"][2]
    w_small = small(w_proj_attn, w_proj_pool, w_out, w_pool) + first_token[0, 0]
    placed = [None] + [_place_block(w_in[0], N_CHIPS, chip, BF16, f"place_w_in{ch}", ch, W_CHUNKS, first_token)
                       for ch in range(1, W_CHUNKS)]
    placed_small = _place_block(w_small, N_CHIPS, chip, BF16, "place_w_small", after=first_token)

    def w_open(ch, after):
        sems, thru, _ = flight["chunk"]
        if ch == 0:
            after = after + placed[1:] + [placed_small]
        landed = _halves_wait(sems, thru, after, f"gather_w_in{ch}_wait")
        if ch + 1 < W_CHUNKS:
            flight["chunk"] = _halves_start(placed[ch + 1], landed, f"gather_w_in{ch + 1}_start")
            flight["token"] = flight["chunk"][2]
        else:
            flight["small"] = _broadcast_start(placed_small, landed, "gather_small_start")
            flight["token"] = flight["small"][2]
        flight["forward"] = _forward_start(landed, f"forward_w_in{ch}_start")
        return flight["forward"][2]

    def w_close(ch, after):
        sems, thru, _ = flight["forward"]
        return _forward_wait(sems, thru, after, f"forward_w_in{ch}_wait"), flight["token"]

    def small_weights(after):
        sems, thru, _ = flight["small"]
        small4 = _broadcast_wait(sems, thru, after, "gather_small_wait")
        wpa4, wpp4, w_out4, w_pool4 = _unpack_small(small4, aw, pw, d, pg)
        return w_pool4.transpose(1, 0, 2, 3).reshape(n_win, pg, pg), wpa4, wpp4, w_out4.reshape(d, d)

    exchange = {}

    def start_exchange(g):
        g_pool4 = g["w_pool"].reshape(n_win, N_CHIPS, pg // N_CHIPS, pg).transpose(1, 0, 2, 3).astype(BF16)
        g_out4 = g["w_out"].reshape(N_CHIPS, d // N_CHIPS, d)
        g_small4 = jnp.concatenate([g["w_proj_attn"], g["w_proj_pool"], g_out4.reshape(N_CHIPS, -1, d // N_CHIPS),
                                    g_pool4.reshape(N_CHIPS, -1, d // N_CHIPS)], axis=1)
        theirs_big, theirs_small = _swap_halves([g["w_in"], g_small4])
        chip_big, placed_big = _add_halves(g["w_in"], theirs_big, core, chip, "add_cores_big")
        chip_small, placed_small = _add_halves(g_small4, theirs_small, core, chip, "add_cores_small")
        sems, sums, placed, token = _scatter_start([chip_big, chip_small], [placed_big, placed_small])
        exchange.update(sems=sems, sums=sums, placed=placed)
        return token

    g = _local_step(x[0], loss_target[0], w_open, w_close, w_in.shape[2], b_gate, pool_scale, ln_gamma, ln_beta,
                    aw, pw, small_weights, start_exchange, first_token)

    sizes = [b_gate.shape[1], pool_scale.shape[1], d, d, 1]
    rows = -(-sum(sizes) // (8 * 128)) * 8
    loss_part = (0.5 / d) * jnp.sum(g["loss_lanes"]).reshape(1, 1)
    parts = _gather_rows(_pack_rows([g["b_gate"], g["pool_scale"], g["ln_gamma"], g["ln_beta"], loss_part], rows))
    zero = jnp.zeros((1, 1), F32)
    packed = [_pack_rows(vs, rows) for vs in ([b_gate, pool_scale, ln_gamma, ln_beta, zero],
                                              [m_b_gate, m_pool_scale, m_ln_gamma, m_ln_beta, zero],
                                              [v_b_gate, v_pool_scale, v_ln_gamma, v_ln_beta, zero])]
    replicated = _sum_rows_adamw(parts, *packed)
    rep = [_unpack_rows(t, sizes) for t in replicated]
    loss = rep[0][4].reshape(())

    got_big, got_small = _scatter_wait(exchange["sems"], exchange["sums"], exchange["placed"],
                                       [g["d_x"], replicated[0]])
    join_sems, halves = _join_start([_sum_slots(got_big, core, "sum_chips_big"),
                                     _sum_slots(got_small, core, "sum_chips_small")])
    mv_small = (small(m_w_proj_attn, m_w_proj_pool, m_w_out, m_w_pool),
                small(v_w_proj_attn, v_w_proj_pool, v_w_out, v_w_pool))
    upd_in = _adamw_half(w_in[0], halves[0], m_w_in[0], v_w_in[0], core, None, "adamw_w_in_own")
    upd_small = _adamw_half(w_small, halves[1], *mv_small, core, None, "adamw_small_own")
    grad_w_in, grad_small = _join_wait(join_sems, halves, [upd_in[0], upd_small[0]])
    upd_in = _adamw_half(w_in[0], grad_w_in, m_w_in[0], v_w_in[0], 1 - core, upd_in, "adamw_w_in_other")
    upd_small = _adamw_half(w_small, grad_small, *mv_small, 1 - core, upd_small, "adamw_small_other")
    grad_w_in = grad_w_in.reshape(-1, grad_w_in.shape[2])
    grad_small = grad_small.reshape(-1, grad_small.shape[2])

    def leaves(big, packed_small, replicated):
        wpa_, wpp_, wo_, wpl_ = _unpack_small(packed_small, aw, pw, d, pg)
        return [big[None], replicated[0], wpl_[None], replicated[1], wpa_[None], wpp_[None], wo_[None],
                replicated[2], replicated[3]]

    out = [loss, g["d_x"][None]]
    out += leaves(grad_w_in, grad_small, rep[0])
    for i in range(3):
        out += leaves(upd_in[i], upd_small[i], rep[1 + i])
    return tuple(out)
```

```python
import math

import jax
import jax.numpy as jnp
from jax import lax
from jax.experimental import pallas as pl
from jax.experimental.pallas import tpu as pltpu

F32 = jnp.float32
BF16 = jnp.bfloat16
MESH = pl.DeviceIdType.MESH
ANY = pl.BlockSpec(memory_space=pl.ANY)

HEAD_DIM = 128
STEPS = 128
DILATIONS = (1, 4, 16)
N_GROUPS = len(DILATIONS)
POOL_WINDOWS = (2, 4, 8, 16)
POOL_HALO = 16
N_CHIPS = 4
N_DEV = 8
ALPHA = 2.0 ** 0.25
LN_EPS = 1e-5
NEG_INF = -1e30
SCORE_SCALE = HEAD_DIM ** -0.5
ADAM_LR = 0.001
ADAM_B1 = 0.9
ADAM_B2 = 0.999
ADAM_EPS = 1e-08
ADAM_WD = 0.01
ADAM_STEP = 10
MIB = 2 ** 20
NT = (((1,), (1,)), ((), ()))
TN = (((0,), (0,)), ((), ()))
DMA_STREAMS = 8


def _params(semantics=None, vmem_mib=48):
    return pltpu.CompilerParams(dimension_semantics=semantics, vmem_limit_bytes=vmem_mib * MIB)


def _divisor_tile(n, target, multiple):
    best = None
    for t in range(multiple, min(n, target) + 1, multiple):
        if n % t == 0:
            best = t
    assert best is not None, (n, target, multiple)
    return best


def _col_tile(*widths):
    g = 0
    for w in widths:
        g = math.gcd(g, w)
    return _divisor_tile(g, 1024, 128)


def _sigmoid(z):
    return jax.nn.sigmoid(z)


def _dsilu(z, sg):
    return sg * (1.0 + z * (1.0 - sg))


def _place():
    x, y, c = lax.axis_index("x"), lax.axis_index("y"), lax.axis_index("c")
    others = [(1 - x, y), (x, 1 - y), (1 - x, 1 - y)]
    return x, y, c, (x, y, 1 - c), others


def _remote(src, dst, send_sem, recv_sem, dev):
    return pltpu.make_async_remote_copy(src_ref=src, dst_ref=dst, send_sem=send_sem, recv_sem=recv_sem,
                                        device_id=dev, device_id_type=MESH)


def _row_pieces(n_rows, streams=DMA_STREAMS, multiple=16):
    size = -(-n_rows // (streams * multiple)) * multiple
    return [(lo, min(size, n_rows - lo)) for lo in range(0, n_rows, size)]


def _start_streams(make, n_rows):
    for lo, size in _row_pieces(n_rows):
        make(pl.ds(lo, size)).start()


def _half_copies(buf, send_sems, recv_sems):
    x, y, c, _, others = _place()
    half = buf.shape[1] // 2
    slab = buf.at[2 * x + y, pl.ds(c * half, half)]
    return [_remote(slab, slab, send_sems[j], recv_sems[j], (ox, oy, c)) for j, (ox, oy) in enumerate(others)]


def _halves_start(placed, after, name):
    k = N_CHIPS - 1

    def body(buf, after_ref, *refs):
        send_sems, recv_sems, token = refs[:k], refs[k:2 * k], refs[-1]
        for cp in _half_copies(buf, send_sems, recv_sems):
            cp.start()
        token[...] = jnp.zeros_like(token)

    out = pl.pallas_call(
        body, name=name,
        out_shape=[pltpu.SemaphoreType.DMA(())] * (2 * k) + [pltpu.HBM(placed.shape, placed.dtype),
                                                             jax.ShapeDtypeStruct((8, 128), F32)],
        in_specs=[HBM, ANY], out_specs=[SEM] * (2 * k) + [HBM, pl.BlockSpec(memory_space=pltpu.VMEM)],
        input_output_aliases={0: 2 * k},
        compiler_params=pltpu.CompilerParams(has_side_effects=DATAFLOW),
    )(pltpu.with_memory_space_constraint(placed, pltpu.HBM), after)
    return out[:2 * k], out[2 * k], out[-1]


def _halves_wait(sems, placed, after, name):
    k = N_CHIPS - 1

    def body(buf, *refs):
        send_sems, recv_sems = refs[:k], refs[k:2 * k]
        for cp in _half_copies(buf, send_sems, recv_sems):
            cp.wait_send()
            cp.wait_recv()

    return pl.pallas_call(
        body, name=name, out_shape=pltpu.HBM(placed.shape, placed.dtype),
        in_specs=[HBM] + [SEM] * (2 * k) + [ANY] * len(after), out_specs=HBM, input_output_aliases={0: 0},
        compiler_params=pltpu.CompilerParams(has_side_effects=DATAFLOW),
    )(placed, *sems, *after)


def _forward_copies(buf, send_sems, recv_sems):
    x, y, c, sibling, others = _place()
    half = buf.shape[1] // 2
    copies = []
    for j, (ox, oy) in enumerate(others):
        slab = buf.at[2 * ox + oy, pl.ds(c * half, half)]
        copies.append(_remote(slab, slab, send_sems[j], recv_sems[j], sibling))
    return copies


def _forward_start(buf, name):
    k = N_CHIPS - 1

    def body(b, *refs):
        send_sems, recv_sems, token = refs[:k], refs[k:2 * k], refs[-1]
        for cp in _forward_copies(b, send_sems, recv_sems):
            cp.start()
        token[...] = jnp.zeros_like(token)

    out = pl.pallas_call(
        body, name=name,
        out_shape=[pltpu.SemaphoreType.DMA(())] * (2 * k) + [pltpu.HBM(buf.shape, buf.dtype),
                                                             jax.ShapeDtypeStruct((8, 128), F32)],
        in_specs=[HBM], out_specs=[SEM] * (2 * k) + [HBM, pl.BlockSpec(memory_space=pltpu.VMEM)],
        input_output_aliases={0: 2 * k},
        compiler_params=pltpu.CompilerParams(has_side_effects=DATAFLOW),
    )(pltpu.with_memory_space_constraint(buf, pltpu.HBM))
    return out[:2 * k], out[2 * k], out[-1]


def _forward_wait(sems, buf, after, name):
    k = N_CHIPS - 1

    def body(b, *refs):
        send_sems, recv_sems = refs[:k], refs[k:2 * k]
        for cp in _forward_copies(b, send_sems, recv_sems):
            cp.wait_send()
            cp.wait_recv()

    return pl.pallas_call(
        body, name=name, out_shape=pltpu.HBM(buf.shape, buf.dtype),
        in_specs=[HBM] + [SEM] * (2 * k) + [ANY] * len(after), out_specs=HBM, input_output_aliases={0: 0},
        compiler_params=pltpu.CompilerParams(has_side_effects=DATAFLOW),
    )(buf, *sems, *after)


def _swap_halves(grads):
    n = len(grads)

    def body(*refs):
        g, theirs = refs[:n], refs[n:2 * n]
        send_sems, recv_sems = refs[2 * n:]
        x, y, c, sibling, _ = _place()
        for i in range(n):
            half = g[i].shape[1] // 2
            give = (1 - c) * half
            for b in range(N_CHIPS):
                _start_streams(lambda r, i=i, b=b: _remote(
                    g[i].at[b, pl.ds(give + r.start, r.size)], theirs[i].at[b, r], send_sems.at[i], recv_sems.at[i],
                    sibling), half)
        for i in range(n):
            _remote(theirs[i], theirs[i], send_sems.at[i], recv_sems.at[i], sibling).wait()

    return pl.pallas_call(
        body, name="swap_halves",
        out_shape=[jax.ShapeDtypeStruct((s.shape[0], s.shape[1] // 2) + s.shape[2:], s.dtype) for s in grads],
        in_specs=[ANY] * n, out_specs=[ANY] * n,
        scratch_shapes=[pltpu.SemaphoreType.DMA((n,)), pltpu.SemaphoreType.DMA((n,))],
    )(*grads)


HBM = pl.BlockSpec(memory_space=pltpu.HBM)
SEM = pl.BlockSpec(memory_space=pltpu.SEMAPHORE)
DATAFLOW = pltpu.SideEffectType.DATAFLOW_SIDE_EFFECTING


def _broadcast_copies(buf, send_sems, recv_sems):
    x, y, c, _, others = _place()
    mine = buf.at[2 * x + y]
    return [_remote(mine, mine, send_sems[j], recv_sems[j], (ox, oy, c)) for j, (ox, oy) in enumerate(others)]


def _broadcast_start(placed, after, name):
    k = N_CHIPS - 1

    def body(buf, after_ref, *refs):
        send_sems, recv_sems, token = refs[:k], refs[k:2 * k], refs[-1]
        for cp in _broadcast_copies(buf, send_sems, recv_sems):
            cp.start()
        token[...] = jnp.zeros_like(token)

    out = pl.pallas_call(
        body, name=name,
        out_shape=[pltpu.SemaphoreType.DMA(())] * (2 * k) + [pltpu.HBM(placed.shape, placed.dtype),
                                                             jax.ShapeDtypeStruct((8, 128), F32)],
        in_specs=[HBM, ANY], out_specs=[SEM] * (2 * k) + [HBM, pl.BlockSpec(memory_space=pltpu.VMEM)],
        input_output_aliases={0: 2 * k},
        compiler_params=pltpu.CompilerParams(has_side_effects=DATAFLOW),
    )(pltpu.with_memory_space_constraint(placed, pltpu.HBM), after)
    return out[:2 * k], out[2 * k], out[-1]


def _broadcast_wait(sems, placed, after, name):
    k = N_CHIPS - 1

    def body(buf, *refs):
        send_sems, recv_sems = refs[:k], refs[k:2 * k]
        for cp in _broadcast_copies(buf, send_sems, recv_sems):
            cp.wait_send()
            cp.wait_recv()

    return pl.pallas_call(
        body, name=name, out_shape=pltpu.HBM(placed.shape, placed.dtype),
        in_specs=[HBM] + [SEM] * (2 * k) + [ANY], out_specs=HBM, input_output_aliases={0: 0},
        compiler_params=pltpu.CompilerParams(has_side_effects=DATAFLOW),
    )(placed, *sems, after)


def _scatter_copies(s, got, send_sems, recv_sems):
    x, y, c, _, others = _place()
    me = 2 * x + y
    n = len(s)
    return [_remote(s[i].at[2 * ox + oy], got[i].at[me], send_sems[3 * i + j], recv_sems[3 * i + j], (ox, oy, c))
            for i in range(n) for j, (ox, oy) in enumerate(others)]


def _scatter_start(sums, placed):
    n = len(sums)
    k = 3 * n

    def body(*refs):
        s, got, token = refs[:n], refs[n:2 * n], refs[-1]
        send_sems, recv_sems = refs[2 * n:2 * n + k], refs[2 * n + k:2 * n + 2 * k]
        for cp in _scatter_copies(s, got, send_sems, recv_sems):
            cp.start()
        token[...] = jnp.zeros_like(token)

    hbm = [pltpu.HBM(a.shape, a.dtype) for a in list(sums) + list(placed)]
    out = pl.pallas_call(
        body, name="scatter_start",
        out_shape=[pltpu.SemaphoreType.DMA(())] * (2 * k) + hbm + [jax.ShapeDtypeStruct((8, 128), F32)],
        in_specs=[HBM] * (2 * n), out_specs=[SEM] * (2 * k) + [HBM] * (2 * n) + [pl.BlockSpec(memory_space=pltpu.VMEM)],
        input_output_aliases={i: 2 * k + i for i in range(2 * n)},
        compiler_params=pltpu.CompilerParams(has_side_effects=DATAFLOW),
    )(*[pltpu.with_memory_space_constraint(a, pltpu.HBM) for a in list(sums) + list(placed)])
    return out[:2 * k], out[2 * k:2 * k + n], out[2 * k + n:2 * k + 2 * n], out[-1]


def _scatter_wait(sems, sums, placed, after):
    n = len(sums)
    k = 3 * n

    def body(*refs):
        s, got = refs[:n], refs[n:2 * n]
        send_sems, recv_sems = refs[2 * n:2 * n + k], refs[2 * n + k:2 * n + 2 * k]
        for cp in _scatter_copies(s, got, send_sems, recv_sems):
            cp.wait_send()
            cp.wait_recv()

    hbm = [pltpu.HBM(a.shape, a.dtype) for a in list(sums) + list(placed)]
    out = pl.pallas_call(
        body, name="scatter_wait", out_shape=hbm,
        in_specs=[HBM] * (2 * n) + [SEM] * (2 * k) + [ANY] * len(after), out_specs=[HBM] * (2 * n),
        input_output_aliases={i: i for i in range(2 * n)},
        compiler_params=pltpu.CompilerParams(has_side_effects=DATAFLOW),
    )(*sums, *placed, *sems, *after)
    return out[n:]


def _join_copies(bufs, send_sems, recv_sems):
    x, y, c, sibling, _ = _place()
    return [_remote(b.at[c], b.at[c], send_sems[i], recv_sems[i], sibling) for i, b in enumerate(bufs)]


def _join_start(placed):
    n = len(placed)

    def body(*refs):
        bufs, send_sems, recv_sems = refs[:n], refs[n:2 * n], refs[2 * n:3 * n]
        for cp in _join_copies(bufs, send_sems, recv_sems):
            cp.start()

    hbm = [pltpu.HBM(a.shape, a.dtype) for a in placed]
    out = pl.pallas_call(
        body, name="join_start", out_shape=[pltpu.SemaphoreType.DMA(())] * (2 * n) + hbm,
        in_specs=[HBM] * n, out_specs=[SEM] * (2 * n) + [HBM] * n,
        input_output_aliases={i: 2 * n + i for i in range(n)},
        compiler_params=pltpu.CompilerParams(has_side_effects=DATAFLOW),
    )(*[pltpu.with_memory_space_constraint(a, pltpu.HBM) for a in placed])
    return out[:2 * n], out[2 * n:]


def _join_wait(sems, placed, after):
    n = len(placed)

    def body(*refs):
        bufs, send_sems, recv_sems = refs[:n], refs[n:2 * n], refs[2 * n:3 * n]
        for cp in _join_copies(bufs, send_sems, recv_sems):
            cp.wait_send()
            cp.wait_recv()

    return pl.pallas_call(
        body, name="join_wait", out_shape=[pltpu.HBM(a.shape, a.dtype) for a in placed],
        in_specs=[HBM] * n + [SEM] * (2 * n) + [ANY] * len(after), out_specs=[HBM] * n,
        input_output_aliases={i: i for i in range(n)},
        compiler_params=pltpu.CompilerParams(has_side_effects=DATAFLOW),
    )(*placed, *sems, *after)


def _gather_rows(row):
    def body(row_ref, out_ref, send_sems, recv_sems, local_sem):
        x, y, c = lax.axis_index("x"), lax.axis_index("y"), lax.axis_index("c")
        me = 4 * x + 2 * y + c
        local = pltpu.make_async_copy(row_ref, out_ref.at[me], local_sem)
        local.start()
        sent = []
        peers = []
        for k in range(1, N_DEV):
            px, py, pc = x ^ (k >> 2), y ^ ((k >> 1) & 1), c ^ (k & 1)
            peers.append((k, px, py, pc))
            cp = _remote(row_ref, out_ref.at[me], send_sems.at[k - 1], recv_sems.at[k - 1], (px, py, pc))
            cp.start()
            sent.append(cp)
        for k, px, py, pc in peers:
            slot = out_ref.at[4 * px + 2 * py + pc]
            _remote(slot, slot, send_sems.at[k - 1], recv_sems.at[k - 1], (px, py, pc)).wait_recv()
        for cp in sent:
            cp.wait_send()
        local.wait()

    return pl.pallas_call(
        body, name="gather_rows", out_shape=jax.ShapeDtypeStruct((N_DEV,) + row.shape, row.dtype),
        in_specs=[ANY], out_specs=ANY,
        scratch_shapes=[pltpu.SemaphoreType.DMA((N_DEV - 1,)), pltpu.SemaphoreType.DMA((N_DEV - 1,)),
                        pltpu.SemaphoreType.DMA],
    )(row)


def _scalar(i):
    return jnp.reshape(i, (1,)).astype(jnp.int32)


def _place_block(src, n_slots, slot, out_dtype, name, window=0, n_windows=1, after=None):
    rows, cols = src.shape[0], src.shape[1] // n_windows
    tr = _divisor_tile(rows, max(16, (2 * MIB) // (cols * 4)), 16)
    ordered = [] if after is None else [after]

    def body(slot_ref, *refs):
        s_ref, o_ref = refs[len(ordered):]
        o_ref[...] = s_ref[...].astype(o_ref.dtype)

    return pl.pallas_call(
        body, name=name, out_shape=jax.ShapeDtypeStruct((n_slots, rows, cols), out_dtype),
        grid_spec=pltpu.PrefetchScalarGridSpec(
            num_scalar_prefetch=1, grid=(rows // tr,),
            in_specs=[pl.BlockSpec(t.shape, lambda r, sl: (0, 0)) for t in ordered]
            + [pl.BlockSpec((tr, cols), lambda r, sl: (r, window))],
            out_specs=pl.BlockSpec((None, tr, cols), lambda r, sl: (sl[0], r, 0))),
        compiler_params=_params(("parallel",)))(_scalar(slot), *ordered, src)


def _add_halves(g, theirs, core, chip, name):
    n, half, cols = theirs.shape
    tr = _divisor_tile(half, max(16, (2 * MIB) // (cols * 4)), 16)
    per = half // tr

    def body(at_ref, a_ref, b_ref, o_ref, own_ref):
        total = (a_ref[...].astype(F32) + b_ref[...].astype(F32)).astype(o_ref.dtype)
        o_ref[...] = total

        @pl.when(pl.program_id(1) == at_ref[1])
        def _():
            own_ref[...] = total

    spec = pl.BlockSpec((None, tr, cols), lambda r, i, at: (i, r, 0))
    shape = jax.ShapeDtypeStruct(theirs.shape, BF16)
    return pl.pallas_call(
        body, name=name, out_shape=[shape, shape],
        grid_spec=pltpu.PrefetchScalarGridSpec(
            num_scalar_prefetch=1, grid=(per, n),
            in_specs=[pl.BlockSpec((None, tr, cols), lambda r, i, at: (i, at[0] * per + r, 0)), spec],
            out_specs=[spec, pl.BlockSpec((None, tr, cols), lambda r, i, at: (at[1], r, 0))]),
        compiler_params=_params(("parallel", "arbitrary")))(jnp.concatenate([_scalar(core), _scalar(chip)]), g, theirs)


def _sum_slots(a, core, name):
    n, rows, cols = a.shape
    tr = _divisor_tile(rows, max(16, (2 * MIB) // (cols * 4 * n)), 16)

    def body(c_ref, a_ref, o_ref):
        acc = a_ref[0].astype(F32)
        for i in range(1, n):
            acc = acc + a_ref[i].astype(F32)
        o_ref[...] = acc

    return pl.pallas_call(
        body, name=name, out_shape=jax.ShapeDtypeStruct((2, rows, cols), F32),
        grid_spec=pltpu.PrefetchScalarGridSpec(
            num_scalar_prefetch=1, grid=(rows // tr,),
            in_specs=[pl.BlockSpec((n, tr, cols), lambda r, c: (0, r, 0))],
            out_specs=pl.BlockSpec((None, tr, cols), lambda r, c: (c[0], r, 0))),
        compiler_params=_params(("parallel",)))(_scalar(core), a)


def _adamw_math(w, g, m, v):
    m = ADAM_B1 * m + (1.0 - ADAM_B1) * g
    v = ADAM_B2 * v + (1.0 - ADAM_B2) * (g * g)
    m_hat = m / (1.0 - ADAM_B1 ** ADAM_STEP)
    v_hat = v / (1.0 - ADAM_B2 ** ADAM_STEP)
    delta = -ADAM_LR * (m_hat / (jnp.sqrt(v_hat) + ADAM_EPS) + ADAM_WD * w)
    return delta, m, v


def _adamw_half(w, g2, m, v, which, prev, name):
    rows, cols = w.shape
    half = rows // 2
    tr = _divisor_tile(half, max(8, MIB // (cols * 4)), 8)
    per = half // tr
    n_out = 4

    def body(h_ref, w_ref, g_ref, m_ref, v_ref, *refs):
        d_ref, nm_ref, nv_ref, go_ref = refs[-n_out:]
        g = g_ref[...]
        d, nm, nv = _adamw_math(w_ref[...], g, m_ref[...], v_ref[...])
        d_ref[...] = d
        nm_ref[...] = nm
        nv_ref[...] = nv
        go_ref[...] = g

    spec = pl.BlockSpec((tr, cols), lambda r, h: (h[0] * per + r, 0))
    in_specs = [spec, pl.BlockSpec((None, tr, cols), lambda r, h: (h[0], r, 0)), spec, spec]
    args = [_scalar(which), w, g2, m, v]
    aliases = {}
    if prev is not None:
        aliases = {len(args) + i: i for i in range(n_out)}
        in_specs += [ANY] * n_out
        args += list(prev)
    return pl.pallas_call(
        body, name=name, out_shape=[jax.ShapeDtypeStruct((rows, cols), F32)] * n_out,
        grid_spec=pltpu.PrefetchScalarGridSpec(num_scalar_prefetch=1, grid=(per,), in_specs=in_specs,
                                               out_specs=[spec] * n_out),
        input_output_aliases=aliases, compiler_params=_params(("parallel",)))(*args)


def _sum_rows_adamw(parts, w, m, v):
    def body(p_ref, w_ref, m_ref, v_ref, g_ref, d_ref, nm_ref, nv_ref):
        g = p_ref[0]
        for i in range(1, N_DEV):
            g = g + p_ref[i]
        d, nm, nv = _adamw_math(w_ref[...], g, m_ref[...], v_ref[...])
        g_ref[...] = g
        d_ref[...] = d
        nm_ref[...] = nm
        nv_ref[...] = nv

    shape = jax.ShapeDtypeStruct(w.shape, F32)
    return pl.pallas_call(body, name="sum_rows_adamw", out_shape=[shape] * 4)(parts, w, m, v)


LANES = 128


def _permute_scratch(rows, width):
    return pltpu.VMEM((width // LANES, rows, LANES), F32)


def _split_rows(value, scratch, dil):
    if dil == 1:
        return [value]
    rows = value.shape[0] // dil
    slabs = value.shape[1] // LANES
    for c in range(slabs):
        scratch[c] = value[:, c * LANES:(c + 1) * LANES]
    return [jnp.concatenate([scratch[c, pl.ds(r, rows, stride=dil), :] for c in range(slabs)], axis=1)
            for r in range(dil)]


def _merge_rows(ref, scratch, dil):
    if dil == 1:
        return ref[0].astype(F32)
    rows = ref.shape[1]
    slabs = ref.shape[2] // LANES
    for r in range(dil):
        part = ref[r].astype(F32)
        for c in range(slabs):
            scratch[c, pl.ds(r, rows, stride=dil), :] = part[:, c * LANES:(c + 1) * LANES]
    return jnp.concatenate([scratch[c] for c in range(slabs)], axis=1)


def _grouped_view(t, dil):
    return t.reshape(dil, t.shape[0] // dil, t.shape[1])


def _grouped_spec(dil, rows, width, index):
    return pl.BlockSpec((dil, rows // dil, width), index)


W_CHUNKS = 4


def _pick(values, j):
    out = values[-1]
    for i in range(len(values) - 2, -1, -1):
        out = jnp.where(j == i, values[i], out)
    return out


def _chunk_of(col, per_chip):
    return (col % per_chip) // (per_chip // W_CHUNKS)


def _w_block(col, per_chip):
    return col // per_chip, 0, (col % per_chip) % (per_chip // W_CHUNKS)


def _in_proj(xb, wc, blocks, j0, ncols, tn, out_dtype, prev, after, name):
    s, d = xb.shape
    per_chip = wc.shape[2] * W_CHUNKS // tn
    tm = _divisor_tile(s, 1024, 16)
    extra = [t for t in after if t is not None]

    def body(*refs):
        a_ref, b_ref = refs[len(extra):len(extra) + 2]
        o_ref = refs[-1]
        o_ref[...] = jnp.dot(a_ref[...], b_ref[...], preferred_element_type=F32).astype(o_ref.dtype)

    in_specs = [pl.BlockSpec(t.shape, lambda j, m: (0, 0)) for t in extra] + [
        pl.BlockSpec((tm, d), lambda j, m: (m, 0)),
        pl.BlockSpec((None, d, tn), lambda j, m: _w_block(_pick(blocks, j), per_chip))]
    args = extra + [xb, wc]
    aliases = {}
    if prev is not None:
        aliases = {len(args): 0}
        in_specs.append(ANY)
        args.append(prev)
    return pl.pallas_call(
        body, name=name, grid=(len(blocks), s // tm), in_specs=in_specs,
        out_specs=pl.BlockSpec((tm, tn), lambda j, m: (m, _pick(blocks, j) - j0)),
        out_shape=jax.ShapeDtypeStruct((s, ncols), out_dtype), input_output_aliases=aliases,
        compiler_params=_params(("parallel", "parallel")))(*args)


def _in_proj_qkv(xb, wc, g, blocks, aw, tn, prev, after, name):
    s, d = xb.shape
    dil = DILATIONS[g]
    per_chip = wc.shape[2] * W_CHUNKS // tn
    sub = aw // tn
    tm = _divisor_tile(s, 1024, 16 * dil)
    extra = [t for t in after if t is not None]

    def body(*refs):
        a_ref, b_ref = refs[len(extra):len(extra) + 2]
        o_ref, scratch = refs[-2:]
        res = jnp.dot(a_ref[...], b_ref[...], preferred_element_type=F32)
        for r, part in enumerate(_split_rows(res, scratch, dil)):
            o_ref[r] = part.astype(BF16)

    def out_index(j, m):
        col = _pick(blocks, j)
        return (col // sub) // N_GROUPS, 0, m, col % sub

    in_specs = [pl.BlockSpec(t.shape, lambda j, m: (0, 0)) for t in extra] + [
        pl.BlockSpec((tm, d), lambda j, m: (m, 0)),
        pl.BlockSpec((None, d, tn), lambda j, m: _w_block(_pick(blocks, j), per_chip))]
    args = extra + [xb, wc]
    aliases = {}
    if prev is not None:
        aliases = {len(args): 0}
        in_specs.append(ANY)
        args.append(prev)
    return pl.pallas_call(
        body, name=name, grid=(len(blocks), s // tm), in_specs=in_specs,
        out_specs=pl.BlockSpec((None, dil, tm // dil, tn), out_index),
        out_shape=jax.ShapeDtypeStruct((3, dil, s // dil, aw), BF16), input_output_aliases=aliases,
        scratch_shapes=[_permute_scratch(tm, tn)],
        compiler_params=_params(("parallel", "parallel")))(*args)


def _window_mask(first):
    qi = lax.broadcasted_iota(jnp.int32, (STEPS, 2 * STEPS), 0)
    kj = lax.broadcasted_iota(jnp.int32, (STEPS, 2 * STEPS), 1)
    lowest = jnp.where(first, STEPS, 0)
    return (kj >= qi) & (kj <= qi + STEPS) & (kj >= lowest)


def _attn_fwd(qkv, g):
    _, s, aw = qkv.shape
    heads = aw // HEAD_DIM
    n_blocks = s // STEPS
    per_seq = n_blocks // DILATIONS[g]
    pair = 2 if n_blocks % 2 == 0 else 1

    def body(q_ref, kc_ref, kp_ref, vc_ref, vp_ref, o_ref, l_ref):
        masks = [_window_mask(lax.rem(pl.program_id(0) * pair + j, per_seq) == 0) for j in range(pair)]
        for h in range(heads):
            hs = slice(h * HEAD_DIM, (h + 1) * HEAD_DIM)
            keys = jnp.concatenate([kp_ref[:, hs], kc_ref[:, hs]], axis=0)
            values = jnp.concatenate([vp_ref[:, hs], vc_ref[:, hs]], axis=0)
            for j in range(pair):
                rows = slice(j * STEPS, (j + 1) * STEPS)
                window = slice(j * STEPS, (j + 2) * STEPS)
                sc = lax.dot_general(q_ref[rows, hs], keys[window], NT, preferred_element_type=F32) * SCORE_SCALE
                sc = jnp.where(masks[j], sc, NEG_INF)
                mx = jnp.max(sc, axis=1, keepdims=True)
                e = jnp.exp(sc - mx)
                den = jnp.sum(e, axis=1, keepdims=True)
                o_ref[rows, hs] = (jnp.dot(e.astype(BF16), values[window], preferred_element_type=F32)
                                   / den).astype(BF16)
                l_ref[rows, hs] = jnp.broadcast_to(mx + jnp.log(den), (STEPS, HEAD_DIM))

    def cur(which):
        return pl.BlockSpec((None, pair * STEPS, aw), lambda b: (which, b, 0))

    def prev(which):
        return pl.BlockSpec((None, STEPS, aw), lambda b: (which, jnp.maximum(pair * b - 1, 0), 0))

    out = pl.BlockSpec((pair * STEPS, aw), lambda b: (b, 0))
    return pl.pallas_call(
        body, name=f"attn_fwd{g}", grid=(n_blocks // pair,),
        in_specs=[cur(0), cur(1), prev(1), cur(2), prev(2)], out_specs=[out, out],
        out_shape=[jax.ShapeDtypeStruct((s, aw), BF16), jax.ShapeDtypeStruct((s, aw), F32)],
        compiler_params=_params(("parallel",)))(qkv, qkv, qkv, qkv, qkv)


def _combine_groups(os, ls, zuz, aw):
    s = zuz.shape[0]
    tr = _divisor_tile(s, 256, 8 * DILATIONS[-1])

    def body(*refs):
        o_refs, l_refs, z_ref = refs[0:3], refs[3:6], refs[6]
        oo_ref, y_ref, yt_ref = refs[7:10]
        lq_refs, scratch = refs[10:13], refs[13]
        ls_ = [_merge_rows(l_refs[g], scratch, dil) for g, dil in enumerate(DILATIONS)]
        mx = jnp.maximum(jnp.maximum(ls_[0], ls_[1]), ls_[2])
        ws = [jnp.exp(l - mx) for l in ls_]
        den = ws[0] + ws[1] + ws[2]
        o = ws[0] * _merge_rows(o_refs[0], scratch, DILATIONS[0])
        for g in range(1, N_GROUPS):
            o = o + ws[g] * _merge_rows(o_refs[g], scratch, DILATIONS[g])
        o = o / den
        z = z_ref[...].astype(F32)
        y = o * (z * _sigmoid(z))
        oo_ref[...] = o.astype(BF16)
        y_ref[...] = y.astype(BF16)
        yt_ref[...] = y.T.astype(BF16)
        for g, dil in enumerate(DILATIONS):
            for r, part in enumerate(_split_rows(mx + jnp.log(den), scratch, dil)):
                lq_refs[g][r] = part

    grouped = [_grouped_spec(dil, tr, aw, lambda r: (0, r, 0)) for dil in DILATIONS]
    one = pl.BlockSpec((tr, aw), lambda r: (r, 0))
    b16 = jax.ShapeDtypeStruct((s, aw), BF16)
    out = pl.pallas_call(
        body, name="combine_groups", grid=(s // tr,),
        in_specs=grouped + grouped + [one],
        out_specs=[one, one, pl.BlockSpec((aw, tr), lambda r: (0, r))] + grouped,
        out_shape=[b16, b16, jax.ShapeDtypeStruct((aw, s), BF16)]
        + [jax.ShapeDtypeStruct((dil, s // dil, aw), F32) for dil in DILATIONS],
        scratch_shapes=[_permute_scratch(tr, aw)],
        compiler_params=_params(("parallel",)))(
            *[_grouped_view(t, dil) for t, dil in zip(os, DILATIONS)],
            *[_grouped_view(t, dil) for t, dil in zip(ls, DILATIONS)], zuz)
    return out[0], out[1], out[2], [t.reshape(s, aw) for t in out[3:]]


def _pool_counts(row0, rows, window):
    t = row0 + lax.broadcasted_iota(jnp.int32, (rows, 1), 0)
    return jnp.minimum(t + 1, window).astype(F32)


def _pool_fwd(zuz, w_pool, pool_scale, aw, pw):
    s = zuz.shape[0]
    pg = pw // len(POOL_WINDOWS)
    tr = _divisor_tile(s, 256, 128)
    u_col, z_col = aw // pw, aw // pw + 1
    assert aw % pw == 0

    def body(u_ref, up_ref, z_ref, w_ref, sc_ref, p_ref, l_ref, y_ref, yt_ref):
        r = pl.program_id(0)
        u = u_ref[...].astype(F32)
        halo = jnp.where(r > 0, up_ref[...].astype(F32), 0.0)
        ext = jnp.concatenate([halo, u], axis=0)
        pieces, lins = [], []
        for gi, window in enumerate(POOL_WINDOWS):
            cs = slice(gi * pg, (gi + 1) * pg)
            acc = ext[:, cs]
            shift = 1
            while shift < window:
                acc = acc + pltpu.roll(acc, shift, 0)
                shift *= 2
            p = acc[POOL_HALO:] / _pool_counts(r * tr, tr, window) - u[:, cs]
            pieces.append(p)
            lins.append(jnp.dot(p.astype(BF16), w_ref[gi], preferred_element_type=F32))
        p = jnp.concatenate(pieces, axis=1)
        lin = jnp.concatenate(lins, axis=1)
        z = z_ref[...].astype(F32)
        y = lin * sc_ref[...] * (z * _sigmoid(z))
        p_ref[...] = p.astype(BF16)
        l_ref[...] = lin
        y_ref[...] = y.astype(BF16)
        yt_ref[...] = y.T.astype(BF16)

    per = tr // POOL_HALO
    out = pl.BlockSpec((tr, pw), lambda r: (r, 0))
    return pl.pallas_call(
        body, name="pool_fwd", grid=(s // tr,),
        in_specs=[pl.BlockSpec((tr, pw), lambda r: (r, u_col)),
                  pl.BlockSpec((POOL_HALO, pw), lambda r: (jnp.maximum(r * per - 1, 0), u_col)),
                  pl.BlockSpec((tr, pw), lambda r: (r, z_col)),
                  pl.BlockSpec((len(POOL_WINDOWS), pg, pg), lambda r: (0, 0, 0)),
                  pl.BlockSpec((1, pw), lambda r: (0, 0))],
        out_specs=[out, out, out, pl.BlockSpec((pw, tr), lambda r: (0, r))],
        out_shape=[jax.ShapeDtypeStruct((s, pw), BF16), jax.ShapeDtypeStruct((s, pw), F32),
                   jax.ShapeDtypeStruct((s, pw), BF16), jax.ShapeDtypeStruct((pw, s), BF16)],
        compiler_params=_params(("parallel",)))(zuz, zuz, zuz, w_pool, pool_scale)


def _proj_merge(y_attn, y_pool, wpa4, wpp4, gpre, b_gate):
    s, aw = y_attn.shape
    pw = y_pool.shape[1]
    tn = wpa4.shape[2]
    d = N_CHIPS * tn
    tm = _divisor_tile(s, 512, 128)

    def body(ya_ref, yp_ref, wa_ref, wp_ref, ga_ref, gp_ref, ba_ref, bp_ref, a_ref, p_ref, sa_ref, sp_ref, m_ref,
             mt_ref):
        a = jnp.dot(ya_ref[...], wa_ref[...], preferred_element_type=F32)
        p = jnp.dot(yp_ref[...], wp_ref[...], preferred_element_type=F32)
        sa = _sigmoid(ga_ref[...].astype(F32) + ba_ref[...])
        sp = _sigmoid(gp_ref[...].astype(F32) + bp_ref[...])
        merged = sa * a + sp * p
        a_ref[...] = a.astype(BF16)
        p_ref[...] = p.astype(BF16)
        sa_ref[...] = sa.astype(BF16)
        sp_ref[...] = sp.astype(BF16)
        m_ref[...] = merged.astype(BF16)
        mt_ref[...] = merged.T.astype(BF16)

    out = pl.BlockSpec((tm, tn), lambda n, m: (m, n))
    f = jax.ShapeDtypeStruct((s, d), BF16)
    return pl.pallas_call(
        body, name="proj_merge", grid=(N_CHIPS, s // tm),
        in_specs=[pl.BlockSpec((tm, aw), lambda n, m: (m, 0)), pl.BlockSpec((tm, pw), lambda n, m: (m, 0)),
                  pl.BlockSpec((None, aw, tn), lambda n, m: (n, 0, 0)),
                  pl.BlockSpec((None, pw, tn), lambda n, m: (n, 0, 0)),
                  pl.BlockSpec((tm, tn), lambda n, m: (m, n)), pl.BlockSpec((tm, tn), lambda n, m: (m, N_CHIPS + n)),
                  pl.BlockSpec((1, tn), lambda n, m: (0, n)), pl.BlockSpec((1, tn), lambda n, m: (0, N_CHIPS + n))],
        out_specs=[out] * 5 + [pl.BlockSpec((tn, tm), lambda n, m: (n, m))],
        out_shape=[f] * 5 + [jax.ShapeDtypeStruct((d, s), BF16)],
        compiler_params=_params(("parallel", "parallel")))(y_attn, y_pool, wpa4, wpp4, gpre, gpre, b_gate, b_gate)


def _out_norm_loss(merged, w_out, x, target, gamma, beta):
    s, d = x.shape
    tm = _divisor_tile(s, 256, 16)

    def body(m_ref, w_ref, x_ref, t_ref, g_ref, b_ref, dr_ref, drb_ref, loss_ref, dg_ref, db_ref):
        @pl.when(pl.program_id(0) == 0)
        def _():
            loss_ref[...] = jnp.zeros_like(loss_ref)
            dg_ref[...] = jnp.zeros_like(dg_ref)
            db_ref[...] = jnp.zeros_like(db_ref)

        r = ALPHA * x_ref[...] + jnp.dot(m_ref[...], w_ref[...], preferred_element_type=F32)
        mu = jnp.mean(r, axis=1, keepdims=True)
        rc = r - mu
        rstd = lax.rsqrt(jnp.mean(rc * rc, axis=1, keepdims=True) + LN_EPS)
        xhat = rc * rstd
        diff = xhat * g_ref[...] + b_ref[...] - t_ref[...]
        dy = diff / d
        loss_ref[...] += jnp.sum(diff * diff, axis=0, keepdims=True)
        dg_ref[...] += jnp.sum(dy * xhat, axis=0, keepdims=True)
        db_ref[...] += jnp.sum(dy, axis=0, keepdims=True)
        dxhat = dy * g_ref[...]
        dr = rstd * (dxhat - jnp.mean(dxhat, axis=1, keepdims=True)
                     - xhat * jnp.mean(dxhat * xhat, axis=1, keepdims=True))
        dr_ref[...] = dr
        drb_ref[...] = dr.astype(BF16)

    row = pl.BlockSpec((tm, d), lambda m: (m, 0))
    vec = pl.BlockSpec((1, d), lambda m: (0, 0))
    v = jax.ShapeDtypeStruct((1, d), F32)
    return pl.pallas_call(
        body, name="out_norm_loss", grid=(s // tm,),
        in_specs=[row, pl.BlockSpec((d, d), lambda m: (0, 0)), row, row, vec, vec],
        out_specs=[row, row, vec, vec, vec],
        out_shape=[jax.ShapeDtypeStruct((s, d), F32), jax.ShapeDtypeStruct((s, d), BF16), v, v, v],
        compiler_params=_params(("arbitrary",), vmem_mib=56))(merged, w_out, x, target, gamma, beta)


def _merge_bwd(drb, w_out, a, p, sa, sp):
    s, d = drb.shape
    tm = _divisor_tile(s, 512, 16)
    tn = d // N_CHIPS

    def body(dr_ref, w_ref, a_ref, p_ref, sa_ref, sp_ref, da_ref, dp_ref, dga_ref, dgp_ref, dba_ref, dbp_ref):
        @pl.when(pl.program_id(1) == 0)
        def _():
            dba_ref[...] = jnp.zeros_like(dba_ref)
            dbp_ref[...] = jnp.zeros_like(dbp_ref)

        dm = lax.dot_general(dr_ref[...], w_ref[...], NT, preferred_element_type=F32)
        sa = sa_ref[...].astype(F32)
        sp = sp_ref[...].astype(F32)
        da_ref[...] = (dm * sa).astype(BF16)
        dp_ref[...] = (dm * sp).astype(BF16)
        dga = dm * a_ref[...].astype(F32) * sa * (1.0 - sa)
        dgp = dm * p_ref[...].astype(F32) * sp * (1.0 - sp)
        dga_ref[...] = dga.astype(BF16)
        dgp_ref[...] = dgp.astype(BF16)
        dba_ref[...] += jnp.sum(dga, axis=0, keepdims=True)
        dbp_ref[...] += jnp.sum(dgp, axis=0, keepdims=True)

    blk = pl.BlockSpec((tm, tn), lambda n, m: (m, n))
    vec = pl.BlockSpec((1, tn), lambda n, m: (0, n))
    b16 = jax.ShapeDtypeStruct((s, d), BF16)
    v = jax.ShapeDtypeStruct((1, d), F32)
    return pl.pallas_call(
        body, name="merge_bwd", grid=(N_CHIPS, s // tm),
        in_specs=[pl.BlockSpec((tm, d), lambda n, m: (m, 0)), pl.BlockSpec((tn, d), lambda n, m: (n, 0)),
                  blk, blk, blk, blk],
        out_specs=[blk, blk, blk, blk, vec, vec], out_shape=[b16, b16, b16, b16, v, v],
        compiler_params=_params(("parallel", "arbitrary")))(drb, w_out, a, p, sa, sp)


def _proj_t(dy_ref, w_ref, tn):
    acc = None
    for n in range(N_CHIPS):
        t = lax.dot_general(dy_ref[:, n * tn:(n + 1) * tn], w_ref[n], NT, preferred_element_type=F32)
        acc = t if acc is None else acc + t
    return acc


def _attn_gate_bwd(da, wpa4, zuz, o):
    s, d = da.shape
    aw, tn = wpa4.shape[1], wpa4.shape[2]
    heads = aw // HEAD_DIM
    tm = _divisor_tile(s, 256, 16 * DILATIONS[-1])

    def body(*refs):
        da_ref, w_ref, z_ref, o_ref, dz_ref = refs[:5]
        do_refs, dd_refs, scratch = refs[5:8], refs[8:11], refs[11]
        dy = _proj_t(da_ref, w_ref, tn)
        z, o = z_ref[...].astype(F32), o_ref[...].astype(F32)
        sg = _sigmoid(z)
        do = dy * (z * sg)
        dz_ref[...] = (dy * o * _dsilu(z, sg)).astype(BF16)
        prod = do * o
        dd = jnp.concatenate(
            [jnp.broadcast_to(jnp.sum(prod[:, h * HEAD_DIM:(h + 1) * HEAD_DIM], axis=1, keepdims=True),
                              (tm, HEAD_DIM)) for h in range(heads)], axis=1)
        for g, dil in enumerate(DILATIONS):
            for r, part in enumerate(_split_rows(do, scratch, dil)):
                do_refs[g][r] = part.astype(BF16)
            for r, part in enumerate(_split_rows(dd, scratch, dil)):
                dd_refs[g][r] = part

    row = pl.BlockSpec((tm, aw), lambda m: (m, 0))
    grouped = [_grouped_spec(dil, tm, aw, lambda m: (0, m, 0)) for dil in DILATIONS]
    out = pl.pallas_call(
        body, name="attn_gate_bwd", grid=(s // tm,),
        in_specs=[pl.BlockSpec((tm, d), lambda m: (m, 0)), pl.BlockSpec((N_CHIPS, aw, tn), lambda m: (0, 0, 0)),
                  row, row],
        out_specs=[row] + grouped + grouped,
        out_shape=[jax.ShapeDtypeStruct((s, aw), BF16)]
        + [jax.ShapeDtypeStruct((dil, s // dil, aw), BF16) for dil in DILATIONS]
        + [jax.ShapeDtypeStruct((dil, s // dil, aw), F32) for dil in DILATIONS],
        scratch_shapes=[_permute_scratch(tm, aw)],
        compiler_params=_params(("parallel",)))(da, wpa4, zuz, o)
    return out[0], [t.reshape(s, aw) for t in out[1:4]], [t.reshape(s, aw) for t in out[4:7]]


def _pool_gate_bwd(dp_in, wpp4, zuz, lin, pooled, w_pool, pool_scale, aw):
    s, d = dp_in.shape
    pw, tn = wpp4.shape[1], wpp4.shape[2]
    n_win = len(POOL_WINDOWS)
    pg = pw // n_win
    tm = _divisor_tile(s, 256, 16)
    z_col = aw // pw + 1

    def body(dp_ref, w_ref, z_ref, l_ref, p_ref, wp_ref, sc_ref, dz_ref, dpo_ref, dw_ref, ds_ref):
        @pl.when(pl.program_id(0) == 0)
        def _():
            dw_ref[...] = jnp.zeros_like(dw_ref)
            ds_ref[...] = jnp.zeros_like(ds_ref)

        dy = _proj_t(dp_ref, w_ref, tn)
        z, lin_ = z_ref[...].astype(F32), l_ref[...]
        sg = _sigmoid(z)
        dypp = dy * (z * sg)
        dz_ref[...] = (dy * (lin_ * sc_ref[...]) * _dsilu(z, sg)).astype(BF16)
        ds_ref[...] += jnp.sum(dypp * lin_, axis=0, keepdims=True)
        dlin = (dypp * sc_ref[...]).astype(BF16)
        for gi in range(n_win):
            cs = slice(gi * pg, (gi + 1) * pg)
            dw_ref[gi] += lax.dot_general(p_ref[:, cs], dlin[:, cs], TN, preferred_element_type=F32)
            dpo_ref[:, cs] = lax.dot_general(dlin[:, cs], wp_ref[gi], NT, preferred_element_type=F32)

    row = pl.BlockSpec((tm, pw), lambda m: (m, 0))
    return pl.pallas_call(
        body, name="pool_gate_bwd", grid=(s // tm,),
        in_specs=[pl.BlockSpec((tm, d), lambda m: (m, 0)), pl.BlockSpec((N_CHIPS, pw, tn), lambda m: (0, 0, 0)),
                  pl.BlockSpec((tm, pw), lambda m: (m, z_col)), row, row,
                  pl.BlockSpec((n_win, pg, pg), lambda m: (0, 0, 0)), pl.BlockSpec((1, pw), lambda m: (0, 0))],
        out_specs=[row, row, pl.BlockSpec((n_win, pg, pg), lambda m: (0, 0, 0)),
                   pl.BlockSpec((1, pw), lambda m: (0, 0))],
        out_shape=[jax.ShapeDtypeStruct((s, pw), BF16), jax.ShapeDtypeStruct((s, pw), F32),
                   jax.ShapeDtypeStruct((n_win, pg, pg), F32), jax.ShapeDtypeStruct((1, pw), F32)],
        compiler_params=_params(("arbitrary",)))(dp_in, wpp4, zuz, lin, pooled, w_pool, pool_scale)


def _pool_bwd(dpooled):
    s, pw = dpooled.shape
    pg = pw // len(POOL_WINDOWS)
    tr = _divisor_tile(s, 256, POOL_HALO)
    per = tr // POOL_HALO
    n_tiles = s // tr

    def body(c_ref, n_ref, du_ref):
        r = pl.program_id(0)
        cur = c_ref[...]
        halo = jnp.where(r < n_tiles - 1, n_ref[...], 0.0)
        ext = jnp.concatenate([cur, halo], axis=0)
        rows = tr + POOL_HALO
        for gi, window in enumerate(POOL_WINDOWS):
            cs = slice(gi * pg, (gi + 1) * pg)
            acc = ext[:, cs] / _pool_counts(r * tr, rows, window)
            shift = 1
            while shift < window:
                acc = acc + pltpu.roll(acc, rows - shift, 0)
                shift *= 2
            du_ref[:, cs] = (acc[:tr] - cur[:, cs]).astype(BF16)

    return pl.pallas_call(
        body, name="pool_bwd", grid=(n_tiles,),
        in_specs=[pl.BlockSpec((tr, pw), lambda r: (r, 0)),
                  pl.BlockSpec((POOL_HALO, pw), lambda r: (jnp.minimum((r + 1) * per, s // POOL_HALO - 1), 0))],
        out_specs=pl.BlockSpec((tr, pw), lambda r: (r, 0)),
        out_shape=jax.ShapeDtypeStruct((s, pw), BF16), compiler_params=_params(("parallel",)))(dpooled, dpooled)


def _attn_bwd(qkv, do, lse, dd, g):
    _, s, aw = qkv.shape
    heads = aw // HEAD_DIM
    n_blocks = s // STEPS
    per_seq = n_blocks // DILATIONS[g]
    pair = 2 if n_blocks % 2 == 0 else 1
    rows_ = pair * STEPS
    n_steps = n_blocks // pair
    tail = slice(rows_ - STEPS, rows_)

    def body(q_ref, do_ref, l_ref, dd_ref, kc_ref, kp_ref, vc_ref, vp_ref, out_ref, cq_ref, ck_ref, cv_ref):
        b = pl.program_id(0)

        @pl.when(b == 0)
        def _():
            cq_ref[...] = jnp.zeros_like(cq_ref)
            ck_ref[...] = jnp.zeros_like(ck_ref)
            cv_ref[...] = jnp.zeros_like(cv_ref)

        out_ref[0] = cq_ref[...].astype(BF16)

        @pl.when(b < n_steps)
        def _():
            masks = [_window_mask(lax.rem(b * pair + j, per_seq) == 0) for j in range(pair)]
            for h in range(heads):
                hs = slice(h * HEAD_DIM, (h + 1) * HEAD_DIM)
                keys = jnp.concatenate([kp_ref[:, hs], kc_ref[:, hs]], axis=0)
                values = jnp.concatenate([vp_ref[:, hs], vc_ref[:, hs]], axis=0)
                dks, dvs = [], []
                for j in range(pair):
                    rows = slice(j * STEPS, (j + 1) * STEPS)
                    window = slice(j * STEPS, (j + 2) * STEPS)
                    q, do_, kk, vv = q_ref[rows, hs], do_ref[rows, hs], keys[window], values[window]
                    lse_ = jnp.concatenate([l_ref[rows, hs], l_ref[rows, hs]], axis=1)
                    dd_ = jnp.concatenate([dd_ref[rows, hs], dd_ref[rows, hs]], axis=1)
                    sc = lax.dot_general(q, kk, NT, preferred_element_type=F32) * SCORE_SCALE
                    prob = jnp.where(masks[j], jnp.exp(sc - lse_), 0.0)
                    dprob = lax.dot_general(do_, vv, NT, preferred_element_type=F32)
                    dsc = prob * (dprob - dd_) * SCORE_SCALE
                    cq_ref[rows, hs] = jnp.dot(dsc.astype(BF16), kk, preferred_element_type=F32)
                    dks.append(lax.dot_general(dsc.astype(BF16), q, TN, preferred_element_type=F32))
                    dvs.append(lax.dot_general(prob.astype(BF16), do_, TN, preferred_element_type=F32))
                for which, carry, parts in ((1, ck_ref, dks), (2, cv_ref, dvs)):
                    out_ref[which, tail, hs] = (carry[tail, hs] + parts[0][:STEPS]).astype(BF16)
                    if pair > 1:
                        out_ref[which, :rows_ - STEPS, hs] = carry[:rows_ - STEPS, hs].astype(BF16)
                    for j in range(pair):
                        total = parts[j][STEPS:]
                        if j + 1 < pair:
                            total = total + parts[j + 1][:STEPS]
                        carry[j * STEPS:(j + 1) * STEPS, hs] = total

        @pl.when(b == n_steps)
        def _():
            out_ref[1] = ck_ref[...].astype(BF16)
            out_ref[2] = cv_ref[...].astype(BF16)

    last = n_steps - 1

    def cur(which):
        return pl.BlockSpec((None, rows_, aw), lambda b: (which, jnp.minimum(b, last), 0))

    def prev(which):
        return pl.BlockSpec((None, STEPS, aw), lambda b: (which, jnp.clip(b * pair - 1, 0, n_blocks - 1), 0))

    row = pl.BlockSpec((rows_, aw), lambda b: (jnp.minimum(b, last), 0))
    return pl.pallas_call(
        body, name=f"attn_bwd{g}", grid=(n_steps + 1,),
        in_specs=[cur(0), row, row, row, cur(1), prev(1), cur(2), prev(2)],
        out_specs=pl.BlockSpec((3, rows_, aw), lambda b: (0, jnp.clip(b - 1, 0, last), 0)),
        out_shape=jax.ShapeDtypeStruct((3, s, aw), BF16),
        scratch_shapes=[pltpu.VMEM((rows_, aw), F32)] * 3,
        compiler_params=_params(("arbitrary",)))(qkv, do, lse, dd, qkv, qkv, qkv, qkv)


def _weight_grad(at, b, tn, col_blocks, name):
    m, k = at.shape
    n = b.shape[1]
    tm = _divisor_tile(m, 1024, 16)
    tk = _divisor_tile(k, 2048, 128)
    nk = k // tk

    def body(a_ref, b_ref, o_ref, acc_ref):
        kk = pl.program_id(2)

        @pl.when(kk == 0)
        def _():
            acc_ref[...] = jnp.zeros_like(acc_ref)

        acc_ref[...] += jnp.dot(a_ref[...], b_ref[...], preferred_element_type=F32)

        @pl.when(kk == nk - 1)
        def _():
            o_ref[...] = acc_ref[...].astype(BF16)

    if col_blocks:
        out_spec = pl.BlockSpec((None, tm, tn), lambda i, j, kk: (j, i, 0))
        out_shape = jax.ShapeDtypeStruct((n // tn, m, tn), BF16)
    else:
        out_spec = pl.BlockSpec((tm, tn), lambda i, j, kk: (i, j))
        out_shape = jax.ShapeDtypeStruct((m, n), BF16)
    return pl.pallas_call(
        body, name=name, grid=(m // tm, n // tn, nk),
        in_specs=[pl.BlockSpec((tm, tk), lambda i, j, kk: (i, kk)), pl.BlockSpec((tk, tn), lambda i, j, kk: (kk, j))],
        out_specs=out_spec, out_shape=out_shape, scratch_shapes=[pltpu.VMEM((tm, tn), F32)],
        compiler_params=_params(("parallel", "parallel", "arbitrary")))(at, b)


def _w_in_grad_part(xt, b, col_of, n_local, tn, w_shape, prev, name):
    d, s = xt.shape
    per_chip = w_shape[2] // tn
    tm = _divisor_tile(d, 1024, 16)
    tk = _divisor_tile(s, 2048, 128)
    nk = s // tk

    def body(*refs):
        a_ref, b_ref, o_ref, acc_ref = refs[0], refs[1], refs[-2], refs[-1]
        kk = pl.program_id(2)

        @pl.when(kk == 0)
        def _():
            acc_ref[...] = jnp.zeros_like(acc_ref)

        acc_ref[...] += jnp.dot(a_ref[...], b_ref[...], preferred_element_type=F32)

        @pl.when(kk == nk - 1)
        def _():
            o_ref[...] = acc_ref[...].astype(BF16)

    if b.ndim == 3:
        sub = b.shape[2] // tn
        b_spec = pl.BlockSpec((None, tk, tn), lambda j, i, kk: (j // sub, kk, j % sub))
    else:
        b_spec = pl.BlockSpec((tk, tn), lambda j, i, kk: (kk, j))
    in_specs = [pl.BlockSpec((tm, tk), lambda j, i, kk: (i, kk)), b_spec]
    args = [xt, b]
    aliases = {}
    if prev is not None:
        in_specs.append(ANY)
        args.append(prev)
        aliases = {2: 0}
    return pl.pallas_call(
        body, name=name, grid=(n_local, d // tm, nk), in_specs=in_specs,
        out_specs=pl.BlockSpec((None, tm, tn), lambda j, i, kk: (col_of(j) // per_chip, i, col_of(j) % per_chip)),
        out_shape=jax.ShapeDtypeStruct(w_shape, BF16), scratch_shapes=[pltpu.VMEM((tm, tn), F32)],
        input_output_aliases=aliases,
        compiler_params=_params(("parallel", "parallel", "arbitrary")))(*args)


def _assemble_w(wcs, after):
    n, d, wc = wcs[0].shape
    tr = _divisor_tile(d, 256, 16)

    def body(after_ref, *refs):
        o_ref = refs[-1]
        for ch in range(W_CHUNKS):
            o_ref[:, ch * wc:(ch + 1) * wc] = refs[ch][...]

    return pl.pallas_call(
        body, name="assemble_w", grid=(n, d // tr),
        in_specs=[pl.BlockSpec(after.shape, lambda b, r: (0, 0))]
        + [pl.BlockSpec((None, tr, wc), lambda b, r: (b, r, 0))] * W_CHUNKS,
        out_specs=pl.BlockSpec((None, tr, W_CHUNKS * wc), lambda b, r: (b, r, 0)),
        out_shape=jax.ShapeDtypeStruct((n, d, W_CHUNKS * wc), wcs[0].dtype),
        compiler_params=_params(("parallel", "parallel")))(after, *wcs)


def _x_grad(dqkv, rest, w4, dr, aw, tn):
    s, d = dr.shape
    sub = aw // tn
    n_qkv = 3 * N_GROUPS * sub
    los, lo = [], n_qkv
    for p in rest:
        los.append(lo)
        lo += p.shape[1] // tn
    n_blocks = lo
    per_chip = n_blocks // N_CHIPS
    tm = _divisor_tile(s, 512, 16 * DILATIONS[-1])

    def body(*refs):
        q_refs, r_refs = refs[:N_GROUPS], refs[N_GROUPS:N_GROUPS + len(rest)]
        w_ref, dr_ref, o_ref, acc_ref, scratch = refs[-5:]
        j = pl.program_id(1)

        @pl.when(j == 0)
        def _():
            acc_ref[...] = ALPHA * dr_ref[...]

        for g, dil in enumerate(DILATIONS):
            @pl.when((j < n_qkv) & (lax.rem(j // sub, N_GROUPS) == g))
            def _(g=g, dil=dil):
                rows = _merge_rows(q_refs[g], scratch, dil).astype(BF16)
                acc_ref[...] += lax.dot_general(rows, w_ref[...], NT, preferred_element_type=F32)

        for p_ref, lo_, piece in zip(r_refs, los, rest):
            @pl.when((j >= lo_) & (j < lo_ + piece.shape[1] // tn))
            def _(p_ref=p_ref):
                acc_ref[...] += lax.dot_general(p_ref[...], w_ref[...], NT, preferred_element_type=F32)

        @pl.when(j == n_blocks - 1)
        def _():
            o_ref[...] = acc_ref[...]

    def qkv_spec(dil):
        def index(i, j):
            region = jnp.minimum(j // sub, 3 * N_GROUPS - 1)
            return region // N_GROUPS, 0, i, jnp.where(j < n_qkv, j % sub, 0)

        return pl.BlockSpec((None, dil, tm // dil, tn), index)

    def rest_spec(lo_, piece):
        n = piece.shape[1] // tn
        return pl.BlockSpec((tm, tn), lambda i, j: (i, jnp.clip(j - lo_, 0, n - 1)))

    row = pl.BlockSpec((tm, d), lambda i, j: (i, 0))
    return pl.pallas_call(
        body, name="x_grad", grid=(s // tm, n_blocks),
        in_specs=[qkv_spec(dil) for dil in DILATIONS] + [rest_spec(lo_, p) for lo_, p in zip(los, rest)]
        + [pl.BlockSpec((None, d, tn), lambda i, j: (j // per_chip, 0, j % per_chip)), row],
        out_specs=row, out_shape=jax.ShapeDtypeStruct((s, d), F32),
        scratch_shapes=[pltpu.VMEM((tm, d), F32), _permute_scratch(tm, tn)],
        compiler_params=_params(("parallel", "arbitrary"), vmem_mib=56))(
            *[t.reshape(3, dil, s // dil, aw) for t, dil in zip(dqkv, DILATIONS)], *rest, w4, dr)


def _prepare_x(x, after=None):
    s, d = x.shape
    tc = 2 * LANES
    slabs = tc // LANES
    ordered = [] if after is None else [after]

    def body(*refs):
        x_ref, xb_ref = refs[len(ordered):len(ordered) + 2]
        xt_refs, scratch = refs[len(ordered) + 2:len(ordered) + 2 + N_GROUPS], refs[-1]
        t = x_ref[...]
        xb_ref[...] = t.astype(BF16)
        for c in range(slabs):
            scratch[c] = t[:, c * LANES:(c + 1) * LANES]
        for g, dil in enumerate(DILATIONS):
            length = s // dil
            for r in range(dil):
                part = t if dil == 1 else jnp.concatenate(
                    [scratch[c, pl.ds(r, length, stride=dil), :] for c in range(slabs)], axis=1)
                xt_refs[g][:, r * length:(r + 1) * length] = part.T.astype(BF16)

    col = pl.BlockSpec((s, tc), lambda j: (0, j))
    row = pl.BlockSpec((tc, s), lambda j: (j, 0))
    t_shape = jax.ShapeDtypeStruct((d, s), BF16)
    out = pl.pallas_call(
        body, name="prepare_x", grid=(d // tc,),
        in_specs=[pl.BlockSpec(t.shape, lambda j: (0, 0)) for t in ordered] + [col],
        out_specs=[col] + [row] * N_GROUPS,
        out_shape=[jax.ShapeDtypeStruct((s, d), BF16)] + [t_shape] * N_GROUPS,
        scratch_shapes=[_permute_scratch(s, tc)], compiler_params=_params(("parallel",)))(*ordered, x)
    return out[0], out[1:]


def _local_step(x, target, w_open, w_close, w_width, b_gate, pool_scale, gamma, beta, aw, pw, small_weights,
                start_exchange=None, first_token=None):
    s, d = x.shape
    tn = _col_tile(aw, pw, w_width)
    sub = aw // tn
    per_chip = w_width // tn
    qkv_w = 3 * N_GROUPS * aw
    w_shape = (N_CHIPS, d, w_width)

    regions = [dict(kind=g, blocks=[(which * N_GROUPS + g) * sub + i for which in range(3) for i in range(sub)])
               for g in range(N_GROUPS)]
    lo = qkv_w // tn
    for name, width in (("zuz", aw + 2 * pw), ("gates", 2 * d)):
        regions.append(dict(kind=name, blocks=list(range(lo, lo + width // tn)), j0=lo, width=width))
        lo += width // tn
    results = [None] * len(regions)
    xb, xts = _prepare_x(x, first_token)
    wcs, last = [], []
    w_open(0, [xb])
    for ch in range(W_CHUNKS):
        wc, token = w_close(ch, last)
        wcs.append(wc)
        calls = []
        for i, region in enumerate(regions):
            blocks = [b for b in region["blocks"] if _chunk_of(b, per_chip) == ch]
            if blocks:
                calls.append((i, region, blocks))
        done = []
        for k, (i, region, blocks) in enumerate(calls):
            after = [token]
            if k == len(calls) - 1 and ch + 1 < W_CHUNKS:
                after.append(w_open(ch + 1, done))
            if region["kind"] in range(N_GROUPS):
                results[i] = _in_proj_qkv(xb, wc, region["kind"], blocks, aw, tn, results[i], after,
                                          f"in_proj_qkv{region['kind']}_{ch}")
            else:
                results[i] = _in_proj(xb, wc, blocks, region["j0"], region["width"], tn, BF16, results[i], after,
                                      f"in_proj_{region['kind']}_{ch}")
            done.append(results[i])
        last = done[-1:]
    qkv = [results[g].reshape(3, s, aw) for g in range(N_GROUPS)]
    zuz, gpre = results[N_GROUPS], results[N_GROUPS + 1]

    attn = [_attn_fwd(qkv[g], g) for g in range(N_GROUPS)]
    o, y_attn, y_attn_t, lse = _combine_groups([a[0] for a in attn], [a[1] for a in attn], zuz, aw)
    w_pool, wpa4, wpp4, w_out = small_weights(o)
    pooled, lin, y_pool, y_pool_t = _pool_fwd(zuz, w_pool, pool_scale, aw, pw)
    a, p, sa, sp, merged, merged_t = _proj_merge(y_attn, y_pool, wpa4, wpp4, gpre, b_gate)
    dr, drb, loss_lanes, d_gamma, d_beta = _out_norm_loss(merged, w_out, x, target, gamma, beta)

    da, dp, d_gpre_a, d_gpre_p, d_b_a, d_b_p = _merge_bwd(drb, w_out, a, p, sa, sp)
    d_b_gate = jnp.concatenate([d_b_a, d_b_p], axis=1)
    d_w_out = _weight_grad(merged_t, drb, d // N_CHIPS, False, "w_out_grad")
    d_wpa4 = _weight_grad(y_attn_t, da, d // N_CHIPS, True, "w_proj_attn_grad")
    d_wpp4 = _weight_grad(y_pool_t, dp, d // N_CHIPS, True, "w_proj_pool_grad")
    d_z_attn, d_o, dd = _attn_gate_bwd(da, wpa4, zuz, o)
    d_z_pool, d_pooled, d_w_pool, d_pool_scale = _pool_gate_bwd(dp, wpp4, zuz, lin, pooled, w_pool, pool_scale, aw)
    d_u = _pool_bwd(d_pooled)
    dqkv = [_attn_bwd(qkv[g], d_o[g], lse[g], dd[g], g) for g in range(N_GROUPS)]

    rest = [d_z_attn, d_u, d_z_pool, d_gpre_a, d_gpre_p]
    d_w_in4 = None
    for g in range(N_GROUPS):
        d_w_in4 = _w_in_grad_part(xts[g], dqkv[g], lambda j, g=g: ((j // sub) * N_GROUPS + g) * sub + j % sub,
                                  3 * sub, tn, w_shape, d_w_in4, f"w_in_grad_qkv{g}")
    lo = qkv_w // tn
    for i, piece in enumerate(rest):
        n_local = piece.shape[1] // tn
        d_w_in4 = _w_in_grad_part(xts[0], piece, lambda j, lo=lo: lo + j, n_local, tn, w_shape, d_w_in4,
                                  f"w_in_grad_rest{i}")
        lo += n_local
    grads = dict(loss_lanes=loss_lanes, w_in=d_w_in4, b_gate=d_b_gate, w_pool=d_w_pool,
                 pool_scale=d_pool_scale, w_proj_attn=d_wpa4, w_proj_pool=d_wpp4, w_out=d_w_out,
                 ln_gamma=d_gamma, ln_beta=d_beta)
    token = jnp.zeros((8, 128), F32) if start_exchange is None else start_exchange(grads)
    grads["d_x"] = _x_grad(dqkv, rest, _assemble_w(wcs, token), dr, aw, tn)
    return grads


def _pack_small(wpa, wpp, w_out, w_pool):
    width = wpa.shape[1]
    return jnp.concatenate([wpa, wpp, w_out.reshape(-1, width), w_pool.reshape(-1, width)], axis=0)


def _unpack_small(packed, aw, pw, d, pg):
    lead = packed.shape[:-2]
    width = d // N_CHIPS
    r0, r1, r2 = aw, aw + pw, aw + pw + d
    return (packed[..., :r0, :], packed[..., r0:r1, :], packed[..., r1:r2, :].reshape(lead + (width, d)),
            packed[..., r2:, :].reshape(lead + (len(POOL_WINDOWS), pg // N_CHIPS, pg)))


def _pack_rows(vectors, rows):
    flat = jnp.concatenate([v.reshape(-1) for v in vectors])
    return jnp.pad(flat, (0, rows * 128 - flat.shape[0])).reshape(rows, 128)


def _unpack_rows(packed, sizes):
    flat, out, lo = packed.reshape(-1), [], 0
    for n in sizes:
        out.append(flat[lo:lo + n].reshape(1, n))
        lo += n
    return out


def kernel(x, w_in, b_gate, w_pool, pool_scale, w_proj_attn, w_proj_pool, w_out, ln_gamma, ln_beta, loss_target, m_w_in, m_b_gate, m_w_pool, m_pool_scale, m_w_proj_attn, m_w_proj_pool, m_w_out, m_ln_gamma, m_ln_beta, v_w_in, v_b_gate, v_w_pool, v_pool_scale, v_w_proj_attn, v_w_proj_pool, v_w_out, v_ln_gamma, v_ln_beta):
    s, d = x.shape[1], x.shape[2]
    aw, pw = w_proj_attn.shape[1], w_proj_pool.shape[1]
    pg = w_pool.shape[3]
    n_win = len(POOL_WINDOWS)

    def small(wpa, wpp, wo, wpl):
        return _pack_small(wpa[0], wpp[0], wo[0], wpl[0])

    chip = 2 * lax.axis_index("x") + lax.axis_index("y")
    core = lax.axis_index("c")

    flight = {"chunk": _halves_start(_place_block(w_in[0], N_CHIPS, chip, BF16, "place_w_in0", 0, W_CHUNKS), x,
                                     "gather_w_in0_start")}
    first_token = flight["chunk"][2]
    w_small = small(w_proj_attn, w_proj_pool, w_out, w_pool) + first_token[0, 0]
    placed = [None] + [_place_block(w_in[0], N_CHIPS, chip, BF16, f"place_w_in{ch}", ch, W_CHUNKS, first_token)
                       for ch in range(1, W_CHUNKS)]
    placed_small = _place_block(w_small, N_CHIPS, chip, BF16, "place_w_small", after=first_token)

    def w_open(ch, after):
        sems, thru, _ = flight["chunk"]
        if ch == 0:
            after = after + placed[1:] + [placed_small]
        landed = _halves_wait(sems, thru, after, f"gather_w_in{ch}_wait")
        if ch + 1 < W_CHUNKS:
            flight["chunk"] = _halves_start(placed[ch + 1], landed, f"gather_w_in{ch + 1}_start")
            flight["token"] = flight["chunk"][2]
        else:
            flight["small"] = _broadcast_start(placed_small, landed, "gather_small_start")
            flight["token"] = flight["small"][2]
        flight["forward"] = _forward_start(landed, f"forward_w_in{ch}_start")
        return flight["forward"][2]

    def w_close(ch, after):
        sems, thru, _ = flight["forward"]
        return _forward_wait(sems, thru, after, f"forward_w_in{ch}_wait"), flight["token"]

    def small_weights(after):
        sems, thru, _ = flight["small"]
        small4 = _broadcast_wait(sems, thru, after, "gather_small_wait")
        wpa4, wpp4, w_out4, w_pool4 = _unpack_small(small4, aw, pw, d, pg)
        return w_pool4.transpose(1, 0, 2, 3).reshape(n_win, pg, pg), wpa4, wpp4, w_out4.reshape(d, d)

    exchange = {}

    def start_exchange(g):
        g_pool4 = g["w_pool"].reshape(n_win, N_CHIPS, pg // N_CHIPS, pg).transpose(1, 0, 2, 3).astype(BF16)
        g_out4 = g["w_out"].reshape(N_CHIPS, d // N_CHIPS, d)
        g_small4 = jnp.concatenate([g["w_proj_attn"], g["w_proj_pool"], g_out4.reshape(N_CHIPS, -1, d // N_CHIPS),
                                    g_pool4.reshape(N_CHIPS, -1, d // N_CHIPS)], axis=1)
        theirs_big, theirs_small = _swap_halves([g["w_in"], g_small4])
        chip_big, placed_big = _add_halves(g["w_in"], theirs_big, core, chip, "add_cores_big")
        chip_small, placed_small = _add_halves(g_small4, theirs_small, core, chip, "add_cores_small")
        sems, sums, placed, token = _scatter_start([chip_big, chip_small], [placed_big, placed_small])
        exchange.update(sems=sems, sums=sums, placed=placed)
        return token

    g = _local_step(x[0], loss_target[0], w_open, w_close, w_in.shape[2], b_gate, pool_scale, ln_gamma, ln_beta,
                    aw, pw, small_weights, start_exchange, first_token)

    sizes = [b_gate.shape[1], pool_scale.shape[1], d, d, 1]
    rows = -(-sum(sizes) // (8 * 128)) * 8
    loss_part = (0.5 / d) * jnp.sum(g["loss_lanes"]).reshape(1, 1)
    parts = _gather_rows(_pack_rows([g["b_gate"], g["pool_scale"], g["ln_gamma"], g["ln_beta"], loss_part], rows))
    zero = jnp.zeros((1, 1), F32)
    packed = [_pack_rows(vs, rows) for vs in ([b_gate, pool_scale, ln_gamma, ln_beta, zero],
                                              [m_b_gate, m_pool_scale, m_ln_gamma, m_ln_beta, zero],
                                              [v_b_gate, v_pool_scale, v_ln_gamma, v_ln_beta, zero])]
    replicated = _sum_rows_adamw(parts, *packed)
    rep = [_unpack_rows(t, sizes) for t in replicated]
    loss = rep[0][4].reshape(())

    got_big, got_small = _scatter_wait(exchange["sems"], exchange["sums"], exchange["placed"],
                                       [g["d_x"], replicated[0]])
    join_sems, halves = _join_start([_sum_slots(got_big, core, "sum_chips_big"),
                                     _sum_slots(got_small, core, "sum_chips_small")])
    mv_small = (small(m_w_proj_attn, m_w_proj_pool, m_w_out, m_w_pool),
                small(v_w_proj_attn, v_w_proj_pool, v_w_out, v_w_pool))
    upd_in = _adamw_half(w_in[0], halves[0], m_w_in[0], v_w_in[0], core, None, "adamw_w_in_own")
    upd_small = _adamw_half(w_small, halves[1], *mv_small, core, None, "adamw_small_own")
    grad_w_in, grad_small = _join_wait(join_sems, halves, [upd_in[0], upd_small[0]])
    upd_in = _adamw_half(w_in[0], grad_w_in, m_w_in[0], v_w_in[0], 1 - core, upd_in, "adamw_w_in_other")
    upd_small = _adamw_half(w_small, grad_small, *mv_small, 1 - core, upd_small, "adamw_small_other")
    grad_w_in, grad_small = upd_in[3], upd_small[3]

    def leaves(big, packed_small, replicated):
        wpa_, wpp_, wo_, wpl_ = _unpack_small(packed_small, aw, pw, d, pg)
        return [big[None], replicated[0], wpl_[None], replicated[1], wpa_[None], wpp_[None], wo_[None],
                replicated[2], replicated[3]]

    out = [loss, g["d_x"][None]]
    out += leaves(grad_w_in, grad_small, rep[0])
    for i in range(3):
        out += leaves(upd_in[i], upd_small[i], rep[1 + i])
    return tuple(out)
```

```python
import math

import jax
import jax.numpy as jnp
from jax import lax
from jax.experimental import pallas as pl
from jax.experimental.pallas import tpu as pltpu

F32 = jnp.float32
BF16 = jnp.bfloat16
MESH = pl.DeviceIdType.MESH
ANY = pl.BlockSpec(memory_space=pl.ANY)

HEAD_DIM = 128
STEPS = 128
DILATIONS = (1, 4, 16)
N_GROUPS = len(DILATIONS)
POOL_WINDOWS = (2, 4, 8, 16)
POOL_HALO = 16
N_CHIPS = 4
N_DEV = 8
ALPHA = 2.0 ** 0.25
LN_EPS = 1e-5
NEG_INF = -1e30
SCORE_SCALE = HEAD_DIM ** -0.5
ADAM_LR = 0.001
ADAM_B1 = 0.9
ADAM_B2 = 0.999
ADAM_EPS = 1e-08
ADAM_WD = 0.01
ADAM_STEP = 10
MIB = 2 ** 20
NT = (((1,), (1,)), ((), ()))
TN = (((0,), (0,)), ((), ()))
DMA_STREAMS = 8


def _params(semantics=None, vmem_mib=48):
    return pltpu.CompilerParams(dimension_semantics=semantics, vmem_limit_bytes=vmem_mib * MIB)


def _divisor_tile(n, target, multiple):
    best = None
    for t in range(multiple, min(n, target) + 1, multiple):
        if n % t == 0:
            best = t
    assert best is not None, (n, target, multiple)
    return best


def _col_tile(*widths):
    g = 0
    for w in widths:
        g = math.gcd(g, w)
    return _divisor_tile(g, 1024, 128)


def _sigmoid(z):
    return jax.nn.sigmoid(z)


def _dsilu(z, sg):
    return sg * (1.0 + z * (1.0 - sg))


def _place():
    x, y, c = lax.axis_index("x"), lax.axis_index("y"), lax.axis_index("c")
    others = [(1 - x, y), (x, 1 - y), (1 - x, 1 - y)]
    return x, y, c, (x, y, 1 - c), others


def _remote(src, dst, send_sem, recv_sem, dev):
    return pltpu.make_async_remote_copy(src_ref=src, dst_ref=dst, send_sem=send_sem, recv_sem=recv_sem,
                                        device_id=dev, device_id_type=MESH)


def _row_pieces(n_rows, streams=DMA_STREAMS, multiple=16):
    size = -(-n_rows // (streams * multiple)) * multiple
    return [(lo, min(size, n_rows - lo)) for lo in range(0, n_rows, size)]


def _start_streams(make, n_rows):
    for lo, size in _row_pieces(n_rows):
        make(pl.ds(lo, size)).start()


def _half_copies(buf, send_sems, recv_sems):
    x, y, c, _, others = _place()
    half = buf.shape[1] // 2
    slab = buf.at[2 * x + y, pl.ds(c * half, half)]
    return [_remote(slab, slab, send_sems[j], recv_sems[j], (ox, oy, c)) for j, (ox, oy) in enumerate(others)]


def _halves_start(placed, after, name):
    k = N_CHIPS - 1

    def body(buf, after_ref, *refs):
        send_sems, recv_sems, token = refs[:k], refs[k:2 * k], refs[-1]
        for cp in _half_copies(buf, send_sems, recv_sems):
            cp.start()
        token[...] = jnp.zeros_like(token)

    out = pl.pallas_call(
        body, name=name,
        out_shape=[pltpu.SemaphoreType.DMA(())] * (2 * k) + [pltpu.HBM(placed.shape, placed.dtype),
                                                             jax.ShapeDtypeStruct((8, 128), F32)],
        in_specs=[HBM, ANY], out_specs=[SEM] * (2 * k) + [HBM, pl.BlockSpec(memory_space=pltpu.VMEM)],
        input_output_aliases={0: 2 * k},
        compiler_params=pltpu.CompilerParams(has_side_effects=DATAFLOW),
    )(pltpu.with_memory_space_constraint(placed, pltpu.HBM), after)
    return out[:2 * k], out[2 * k], out[-1]


def _halves_wait(sems, placed, after, name):
    k = N_CHIPS - 1

    def body(buf, *refs):
        send_sems, recv_sems = refs[:k], refs[k:2 * k]
        for cp in _half_copies(buf, send_sems, recv_sems):
            cp.wait_send()
            cp.wait_recv()

    return pl.pallas_call(
        body, name=name, out_shape=pltpu.HBM(placed.shape, placed.dtype),
        in_specs=[HBM] + [SEM] * (2 * k) + [ANY] * len(after), out_specs=HBM, input_output_aliases={0: 0},
        compiler_params=pltpu.CompilerParams(has_side_effects=DATAFLOW),
    )(placed, *sems, *after)


def _forward_copies(buf, send_sems, recv_sems):
    x, y, c, sibling, others = _place()
    half = buf.shape[1] // 2
    copies = []
    for j, (ox, oy) in enumerate(others):
        slab = buf.at[2 * ox + oy, pl.ds(c * half, half)]
        copies.append(_remote(slab, slab, send_sems[j], recv_sems[j], sibling))
    return copies


def _forward_start(buf, name):
    k = N_CHIPS - 1

    def body(b, *refs):
        send_sems, recv_sems, token = refs[:k], refs[k:2 * k], refs[-1]
        for cp in _forward_copies(b, send_sems, recv_sems):
            cp.start()
        token[...] = jnp.zeros_like(token)

    out = pl.pallas_call(
        body, name=name,
        out_shape=[pltpu.SemaphoreType.DMA(())] * (2 * k) + [pltpu.HBM(buf.shape, buf.dtype),
                                                             jax.ShapeDtypeStruct((8, 128), F32)],
        in_specs=[HBM], out_specs=[SEM] * (2 * k) + [HBM, pl.BlockSpec(memory_space=pltpu.VMEM)],
        input_output_aliases={0: 2 * k},
        compiler_params=pltpu.CompilerParams(has_side_effects=DATAFLOW),
    )(pltpu.with_memory_space_constraint(buf, pltpu.HBM))
    return out[:2 * k], out[2 * k], out[-1]


def _forward_wait(sems, buf, after, name):
    k = N_CHIPS - 1

    def body(b, *refs):
        send_sems, recv_sems = refs[:k], refs[k:2 * k]
        for cp in _forward_copies(b, send_sems, recv_sems):
            cp.wait_send()
            cp.wait_recv()

    return pl.pallas_call(
        body, name=name, out_shape=pltpu.HBM(buf.shape, buf.dtype),
        in_specs=[HBM] + [SEM] * (2 * k) + [ANY] * len(after), out_specs=HBM, input_output_aliases={0: 0},
        compiler_params=pltpu.CompilerParams(has_side_effects=DATAFLOW),
    )(buf, *sems, *after)


def _swap_halves(grads):
    n = len(grads)

    def body(*refs):
        g, theirs = refs[:n], refs[n:2 * n]
        send_sems, recv_sems = refs[2 * n:]
        x, y, c, sibling, _ = _place()
        for i in range(n):
            half = g[i].shape[1] // 2
            give = (1 - c) * half
            for b in range(N_CHIPS):
                _start_streams(lambda r, i=i, b=b: _remote(
                    g[i].at[b, pl.ds(give + r.start, r.size)], theirs[i].at[b, r], send_sems.at[i], recv_sems.at[i],
                    sibling), half)
        for i in range(n):
            _remote(theirs[i], theirs[i], send_sems.at[i], recv_sems.at[i], sibling).wait()

    return pl.pallas_call(
        body, name="swap_halves",
        out_shape=[jax.ShapeDtypeStruct((s.shape[0], s.shape[1] // 2) + s.shape[2:], s.dtype) for s in grads],
        in_specs=[ANY] * n, out_specs=[ANY] * n,
        scratch_shapes=[pltpu.SemaphoreType.DMA((n,)), pltpu.SemaphoreType.DMA((n,))],
    )(*grads)


HBM = pl.BlockSpec(memory_space=pltpu.HBM)
SEM = pl.BlockSpec(memory_space=pltpu.SEMAPHORE)
DATAFLOW = pltpu.SideEffectType.DATAFLOW_SIDE_EFFECTING


def _broadcast_copies(buf, send_sems, recv_sems):
    x, y, c, _, others = _place()
    mine = buf.at[2 * x + y]
    return [_remote(mine, mine, send_sems[j], recv_sems[j], (ox, oy, c)) for j, (ox, oy) in enumerate(others)]


def _broadcast_start(placed, after, name):
    k = N_CHIPS - 1

    def body(buf, after_ref, *refs):
        send_sems, recv_sems, token = refs[:k], refs[k:2 * k], refs[-1]
        for cp in _broadcast_copies(buf, send_sems, recv_sems):
            cp.start()
        token[...] = jnp.zeros_like(token)

    out = pl.pallas_call(
        body, name=name,
        out_shape=[pltpu.SemaphoreType.DMA(())] * (2 * k) + [pltpu.HBM(placed.shape, placed.dtype),
                                                             jax.ShapeDtypeStruct((8, 128), F32)],
        in_specs=[HBM, ANY], out_specs=[SEM] * (2 * k) + [HBM, pl.BlockSpec(memory_space=pltpu.VMEM)],
        input_output_aliases={0: 2 * k},
        compiler_params=pltpu.CompilerParams(has_side_effects=DATAFLOW),
    )(pltpu.with_memory_space_constraint(placed, pltpu.HBM), after)
    return out[:2 * k], out[2 * k], out[-1]


def _broadcast_wait(sems, placed, after, name):
    k = N_CHIPS - 1

    def body(buf, *refs):
        send_sems, recv_sems = refs[:k], refs[k:2 * k]
        for cp in _broadcast_copies(buf, send_sems, recv_sems):
            cp.wait_send()
            cp.wait_recv()

    return pl.pallas_call(
        body, name=name, out_shape=pltpu.HBM(placed.shape, placed.dtype),
        in_specs=[HBM] + [SEM] * (2 * k) + [ANY], out_specs=HBM, input_output_aliases={0: 0},
        compiler_params=pltpu.CompilerParams(has_side_effects=DATAFLOW),
    )(placed, *sems, after)


def _scatter_copies(s, got, send_sems, recv_sems):
    x, y, c, _, others = _place()
    me = 2 * x + y
    n = len(s)
    return [_remote(s[i].at[2 * ox + oy], got[i].at[me], send_sems[3 * i + j], recv_sems[3 * i + j], (ox, oy, c))
            for i in range(n) for j, (ox, oy) in enumerate(others)]


def _scatter_start(sums, placed):
    n = len(sums)
    k = 3 * n

    def body(*refs):
        s, got, token = refs[:n], refs[n:2 * n], refs[-1]
        send_sems, recv_sems = refs[2 * n:2 * n + k], refs[2 * n + k:2 * n + 2 * k]
        for cp in _scatter_copies(s, got, send_sems, recv_sems):
            cp.start()
        token[...] = jnp.zeros_like(token)

    hbm = [pltpu.HBM(a.shape, a.dtype) for a in list(sums) + list(placed)]
    out = pl.pallas_call(
        body, name="scatter_start",
        out_shape=[pltpu.SemaphoreType.DMA(())] * (2 * k) + hbm + [jax.ShapeDtypeStruct((8, 128), F32)],
        in_specs=[HBM] * (2 * n), out_specs=[SEM] * (2 * k) + [HBM] * (2 * n) + [pl.BlockSpec(memory_space=pltpu.VMEM)],
        input_output_aliases={i: 2 * k + i for i in range(2 * n)},
        compiler_params=pltpu.CompilerParams(has_side_effects=DATAFLOW),
    )(*[pltpu.with_memory_space_constraint(a, pltpu.HBM) for a in list(sums) + list(placed)])
    return out[:2 * k], out[2 * k:2 * k + n], out[2 * k + n:2 * k + 2 * n], out[-1]


def _scatter_wait(sems, sums, placed, after):
    n = len(sums)
    k = 3 * n

    def body(*refs):
        s, got = refs[:n], refs[n:2 * n]
        send_sems, recv_sems = refs[2 * n:2 * n + k], refs[2 * n + k:2 * n + 2 * k]
        for cp in _scatter_copies(s, got, send_sems, recv_sems):
            cp.wait_send()
            cp.wait_recv()

    hbm = [pltpu.HBM(a.shape, a.dtype) for a in list(sums) + list(placed)]
    out = pl.pallas_call(
        body, name="scatter_wait", out_shape=hbm,
        in_specs=[HBM] * (2 * n) + [SEM] * (2 * k) + [ANY] * len(after), out_specs=[HBM] * (2 * n),
        input_output_aliases={i: i for i in range(2 * n)},
        compiler_params=pltpu.CompilerParams(has_side_effects=DATAFLOW),
    )(*sums, *placed, *sems, *after)
    return out[n:]


def _join_copies(bufs, send_sems, recv_sems):
    x, y, c, sibling, _ = _place()
    return [_remote(b.at[c], b.at[c], send_sems[i], recv_sems[i], sibling) for i, b in enumerate(bufs)]


def _join_start(placed):
    n = len(placed)

    def body(*refs):
        bufs, send_sems, recv_sems = refs[:n], refs[n:2 * n], refs[2 * n:3 * n]
        for cp in _join_copies(bufs, send_sems, recv_sems):
            cp.start()

    hbm = [pltpu.HBM(a.shape, a.dtype) for a in placed]
    out = pl.pallas_call(
        body, name="join_start", out_shape=[pltpu.SemaphoreType.DMA(())] * (2 * n) + hbm,
        in_specs=[HBM] * n, out_specs=[SEM] * (2 * n) + [HBM] * n,
        input_output_aliases={i: 2 * n + i for i in range(n)},
        compiler_params=pltpu.CompilerParams(has_side_effects=DATAFLOW),
    )(*[pltpu.with_memory_space_constraint(a, pltpu.HBM) for a in placed])
    return out[:2 * n], out[2 * n:]


def _join_wait(sems, placed, after):
    n = len(placed)

    def body(*refs):
        bufs, send_sems, recv_sems = refs[:n], refs[n:2 * n], refs[2 * n:3 * n]
        for cp in _join_copies(bufs, send_sems, recv_sems):
            cp.wait_send()
            cp.wait_recv()

    return pl.pallas_call(
        body, name="join_wait", out_shape=[pltpu.HBM(a.shape, a.dtype) for a in placed],
        in_specs=[HBM] * n + [SEM] * (2 * n) + [ANY] * len(after), out_specs=[HBM] * n,
        input_output_aliases={i: i for i in range(n)},
        compiler_params=pltpu.CompilerParams(has_side_effects=DATAFLOW),
    )(*placed, *sems, *after)


def _gather_rows(row):
    def body(row_ref, out_ref, send_sems, recv_sems, local_sem):
        x, y, c = lax.axis_index("x"), lax.axis_index("y"), lax.axis_index("c")
        me = 4 * x + 2 * y + c
        local = pltpu.make_async_copy(row_ref, out_ref.at[me], local_sem)
        local.start()
        sent = []
        peers = []
        for k in range(1, N_DEV):
            px, py, pc = x ^ (k >> 2), y ^ ((k >> 1) & 1), c ^ (k & 1)
            peers.append((k, px, py, pc))
            cp = _remote(row_ref, out_ref.at[me], send_sems.at[k - 1], recv_sems.at[k - 1], (px, py, pc))
            cp.start()
            sent.append(cp)
        for k, px, py, pc in peers:
            slot = out_ref.at[4 * px + 2 * py + pc]
            _remote(slot, slot, send_sems.at[k - 1], recv_sems.at[k - 1], (px, py, pc)).wait_recv()
        for cp in sent:
            cp.wait_send()
        local.wait()

    return pl.pallas_call(
        body, name="gather_rows", out_shape=jax.ShapeDtypeStruct((N_DEV,) + row.shape, row.dtype),
        in_specs=[ANY], out_specs=ANY,
        scratch_shapes=[pltpu.SemaphoreType.DMA((N_DEV - 1,)), pltpu.SemaphoreType.DMA((N_DEV - 1,)),
                        pltpu.SemaphoreType.DMA],
    )(row)


def _scalar(i):
    return jnp.reshape(i, (1,)).astype(jnp.int32)


def _place_block(src, n_slots, slot, out_dtype, name, window=0, n_windows=1, after=None):
    rows, cols = src.shape[0], src.shape[1] // n_windows
    tr = _divisor_tile(rows, max(16, (2 * MIB) // (cols * 4)), 16)
    ordered = [] if after is None else [after]

    def body(slot_ref, *refs):
        s_ref, o_ref = refs[len(ordered):]
        o_ref[...] = s_ref[...].astype(o_ref.dtype)

    return pl.pallas_call(
        body, name=name, out_shape=jax.ShapeDtypeStruct((n_slots, rows, cols), out_dtype),
        grid_spec=pltpu.PrefetchScalarGridSpec(
            num_scalar_prefetch=1, grid=(rows // tr,),
            in_specs=[pl.BlockSpec(t.shape, lambda r, sl: (0, 0)) for t in ordered]
            + [pl.BlockSpec((tr, cols), lambda r, sl: (r, window))],
            out_specs=pl.BlockSpec((None, tr, cols), lambda r, sl: (sl[0], r, 0))),
        compiler_params=_params(("parallel",)))(_scalar(slot), *ordered, src)


def _add_halves(g, theirs, core, chip, name):
    n, half, cols = theirs.shape
    tr = _divisor_tile(half, max(16, (2 * MIB) // (cols * 4)), 16)
    per = half // tr

    def body(at_ref, a_ref, b_ref, o_ref, own_ref):
        total = (a_ref[...].astype(F32) + b_ref[...].astype(F32)).astype(o_ref.dtype)
        o_ref[...] = total

        @pl.when(pl.program_id(1) == at_ref[1])
        def _():
            own_ref[...] = total

    spec = pl.BlockSpec((None, tr, cols), lambda r, i, at: (i, r, 0))
    shape = jax.ShapeDtypeStruct(theirs.shape, BF16)
    return pl.pallas_call(
        body, name=name, out_shape=[shape, shape],
        grid_spec=pltpu.PrefetchScalarGridSpec(
            num_scalar_prefetch=1, grid=(per, n),
            in_specs=[pl.BlockSpec((None, tr, cols), lambda r, i, at: (i, at[0] * per + r, 0)), spec],
            out_specs=[spec, pl.BlockSpec((None, tr, cols), lambda r, i, at: (at[1], r, 0))]),
        compiler_params=_params(("parallel", "arbitrary")))(jnp.concatenate([_scalar(core), _scalar(chip)]), g, theirs)


def _sum_slots(a, core, name):
    n, rows, cols = a.shape
    tr = _divisor_tile(rows, max(16, (2 * MIB) // (cols * 4 * n)), 16)

    def body(c_ref, a_ref, o_ref):
        acc = a_ref[0].astype(F32)
        for i in range(1, n):
            acc = acc + a_ref[i].astype(F32)
        o_ref[...] = acc

    return pl.pallas_call(
        body, name=name, out_shape=jax.ShapeDtypeStruct((2, rows, cols), F32),
        grid_spec=pltpu.PrefetchScalarGridSpec(
            num_scalar_prefetch=1, grid=(rows // tr,),
            in_specs=[pl.BlockSpec((n, tr, cols), lambda r, c: (0, r, 0))],
            out_specs=pl.BlockSpec((None, tr, cols), lambda r, c: (c[0], r, 0))),
        compiler_params=_params(("parallel",)))(_scalar(core), a)


def _adamw_math(w, g, m, v):
    m = ADAM_B1 * m + (1.0 - ADAM_B1) * g
    v = ADAM_B2 * v + (1.0 - ADAM_B2) * (g * g)
    m_hat = m / (1.0 - ADAM_B1 ** ADAM_STEP)
    v_hat = v / (1.0 - ADAM_B2 ** ADAM_STEP)
    delta = -ADAM_LR * (m_hat / (jnp.sqrt(v_hat) + ADAM_EPS) + ADAM_WD * w)
    return delta, m, v


def _adamw_half(w, g2, m, v, which, prev, name):
    rows, cols = w.shape
    half = rows // 2
    tr = _divisor_tile(half, max(8, MIB // (cols * 4)), 8)
    per = half // tr
    n_out = 4

    def body(h_ref, w_ref, g_ref, m_ref, v_ref, *refs):
        d_ref, nm_ref, nv_ref, go_ref = refs[-n_out:]
        g = g_ref[...]
        d, nm, nv = _adamw_math(w_ref[...], g, m_ref[...], v_ref[...])
        d_ref[...] = d
        nm_ref[...] = nm
        nv_ref[...] = nv
        go_ref[...] = g

    spec = pl.BlockSpec((tr, cols), lambda r, h: (h[0] * per + r, 0))
    in_specs = [spec, pl.BlockSpec((None, tr, cols), lambda r, h: (h[0], r, 0)), spec, spec]
    args = [_scalar(which), w, g2, m, v]
    aliases = {}
    if prev is not None:
        aliases = {len(args) + i: i for i in range(n_out)}
        in_specs += [ANY] * n_out
        args += list(prev)
    return pl.pallas_call(
        body, name=name, out_shape=[jax.ShapeDtypeStruct((rows, cols), F32)] * n_out,
        grid_spec=pltpu.PrefetchScalarGridSpec(num_scalar_prefetch=1, grid=(per,), in_specs=in_specs,
                                               out_specs=[spec] * n_out),
        input_output_aliases=aliases, compiler_params=_params(("parallel",)))(*args)


def _sum_rows_adamw(parts, w, m, v):
    def body(p_ref, w_ref, m_ref, v_ref, g_ref, d_ref, nm_ref, nv_ref):
        g = p_ref[0]
        for i in range(1, N_DEV):
            g = g + p_ref[i]
        d, nm, nv = _adamw_math(w_ref[...], g, m_ref[...], v_ref[...])
        g_ref[...] = g
        d_ref[...] = d
        nm_ref[...] = nm
        nv_ref[...] = nv

    shape = jax.ShapeDtypeStruct(w.shape, F32)
    return pl.pallas_call(body, name="sum_rows_adamw", out_shape=[shape] * 4)(parts, w, m, v)


LANES = 128


def _permute_scratch(rows, width):
    return pltpu.VMEM((width // LANES, rows, LANES), F32)


def _split_rows(value, scratch, dil):
    if dil == 1:
        return [value]
    rows = value.shape[0] // dil
    slabs = value.shape[1] // LANES
    for c in range(slabs):
        scratch[c] = value[:, c * LANES:(c + 1) * LANES]
    return [jnp.concatenate([scratch[c, pl.ds(r, rows, stride=dil), :] for c in range(slabs)], axis=1)
            for r in range(dil)]


def _merge_rows(ref, scratch, dil):
    if dil == 1:
        return ref[0].astype(F32)
    rows = ref.shape[1]
    slabs = ref.shape[2] // LANES
    for r in range(dil):
        part = ref[r].astype(F32)
        for c in range(slabs):
            scratch[c, pl.ds(r, rows, stride=dil), :] = part[:, c * LANES:(c + 1) * LANES]
    return jnp.concatenate([scratch[c] for c in range(slabs)], axis=1)


def _grouped_view(t, dil):
    return t.reshape(dil, t.shape[0] // dil, t.shape[1])


def _grouped_spec(dil, rows, width, index):
    return pl.BlockSpec((dil, rows // dil, width), index)


W_CHUNKS = 4


def _pick(values, j):
    out = values[-1]
    for i in range(len(values) - 2, -1, -1):
        out = jnp.where(j == i, values[i], out)
    return out


def _chunk_of(col, per_chip):
    return (col % per_chip) // (per_chip // W_CHUNKS)


def _w_block(col, per_chip):
    return col // per_chip, 0, (col % per_chip) % (per_chip // W_CHUNKS)


def _in_proj(xb, wc, blocks, j0, ncols, tn, out_dtype, prev, after, name):
    s, d = xb.shape
    per_chip = wc.shape[2] * W_CHUNKS // tn
    tm = _divisor_tile(s, 1024, 16)
    extra = [t for t in after if t is not None]

    def body(*refs):
        a_ref, b_ref = refs[len(extra):len(extra) + 2]
        o_ref = refs[-1]
        o_ref[...] = jnp.dot(a_ref[...], b_ref[...], preferred_element_type=F32).astype(o_ref.dtype)

    in_specs = [pl.BlockSpec(t.shape, lambda j, m: (0, 0)) for t in extra] + [
        pl.BlockSpec((tm, d), lambda j, m: (m, 0)),
        pl.BlockSpec((None, d, tn), lambda j, m: _w_block(_pick(blocks, j), per_chip))]
    args = extra + [xb, wc]
    aliases = {}
    if prev is not None:
        aliases = {len(args): 0}
        in_specs.append(ANY)
        args.append(prev)
    return pl.pallas_call(
        body, name=name, grid=(len(blocks), s // tm), in_specs=in_specs,
        out_specs=pl.BlockSpec((tm, tn), lambda j, m: (m, _pick(blocks, j) - j0)),
        out_shape=jax.ShapeDtypeStruct((s, ncols), out_dtype), input_output_aliases=aliases,
        compiler_params=_params(("parallel", "parallel")))(*args)


def _in_proj_qkv(xb, wc, g, blocks, aw, tn, prev, after, name):
    s, d = xb.shape
    dil = DILATIONS[g]
    per_chip = wc.shape[2] * W_CHUNKS // tn
    sub = aw // tn
    tm = _divisor_tile(s, 1024, 16 * dil)
    extra = [t for t in after if t is not None]

    def body(*refs):
        a_ref, b_ref = refs[len(extra):len(extra) + 2]
        o_ref, scratch = refs[-2:]
        res = jnp.dot(a_ref[...], b_ref[...], preferred_element_type=F32)
        for r, part in enumerate(_split_rows(res, scratch, dil)):
            o_ref[r] = part.astype(BF16)

    def out_index(j, m):
        col = _pick(blocks, j)
        return (col // sub) // N_GROUPS, 0, m, col % sub

    in_specs = [pl.BlockSpec(t.shape, lambda j, m: (0, 0)) for t in extra] + [
        pl.BlockSpec((tm, d), lambda j, m: (m, 0)),
        pl.BlockSpec((None, d, tn), lambda j, m: _w_block(_pick(blocks, j), per_chip))]
    args = extra + [xb, wc]
    aliases = {}
    if prev is not None:
        aliases = {len(args): 0}
        in_specs.append(ANY)
        args.append(prev)
    return pl.pallas_call(
        body, name=name, grid=(len(blocks), s // tm), in_specs=in_specs,
        out_specs=pl.BlockSpec((None, dil, tm // dil, tn), out_index),
        out_shape=jax.ShapeDtypeStruct((3, dil, s // dil, aw), BF16), input_output_aliases=aliases,
        scratch_shapes=[_permute_scratch(tm, tn)],
        compiler_params=_params(("parallel", "parallel")))(*args)


def _window_mask(first):
    qi = lax.broadcasted_iota(jnp.int32, (STEPS, 2 * STEPS), 0)
    kj = lax.broadcasted_iota(jnp.int32, (STEPS, 2 * STEPS), 1)
    lowest = jnp.where(first, STEPS, 0)
    return (kj >= qi) & (kj <= qi + STEPS) & (kj >= lowest)


def _attn_fwd(qkv, g):
    _, s, aw = qkv.shape
    heads = aw // HEAD_DIM
    n_blocks = s // STEPS
    per_seq = n_blocks // DILATIONS[g]
    pair = 2 if n_blocks % 2 == 0 else 1

    def body(q_ref, kc_ref, kp_ref, vc_ref, vp_ref, o_ref, l_ref):
        masks = [_window_mask(lax.rem(pl.program_id(0) * pair + j, per_seq) == 0) for j in range(pair)]
        for h in range(heads):
            hs = slice(h * HEAD_DIM, (h + 1) * HEAD_DIM)
            keys = jnp.concatenate([kp_ref[:, hs], kc_ref[:, hs]], axis=0)
            values = jnp.concatenate([vp_ref[:, hs], vc_ref[:, hs]], axis=0)
            for j in range(pair):
                rows = slice(j * STEPS, (j + 1) * STEPS)
                window = slice(j * STEPS, (j + 2) * STEPS)
                sc = lax.dot_general(q_ref[rows, hs], keys[window], NT, preferred_element_type=F32) * SCORE_SCALE
                sc = jnp.where(masks[j], sc, NEG_INF)
                mx = jnp.max(sc, axis=1, keepdims=True)
                e = jnp.exp(sc - mx)
                den = jnp.sum(e, axis=1, keepdims=True)
                o_ref[rows, hs] = (jnp.dot(e.astype(BF16), values[window], preferred_element_type=F32)
                                   / den).astype(BF16)
                l_ref[rows, hs] = jnp.broadcast_to(mx + jnp.log(den), (STEPS, HEAD_DIM))

    def cur(which):
        return pl.BlockSpec((None, pair * STEPS, aw), lambda b: (which, b, 0))

    def prev(which):
        return pl.BlockSpec((None, STEPS, aw), lambda b: (which, jnp.maximum(pair * b - 1, 0), 0))

    out = pl.BlockSpec((pair * STEPS, aw), lambda b: (b, 0))
    return pl.pallas_call(
        body, name=f"attn_fwd{g}", grid=(n_blocks // pair,),
        in_specs=[cur(0), cur(1), prev(1), cur(2), prev(2)], out_specs=[out, out],
        out_shape=[jax.ShapeDtypeStruct((s, aw), BF16), jax.ShapeDtypeStruct((s, aw), F32)],
        compiler_params=_params(("parallel",)))(qkv, qkv, qkv, qkv, qkv)


def _combine_groups(os, ls, zuz, aw):
    s = zuz.shape[0]
    tr = _divisor_tile(s, 256, 8 * DILATIONS[-1])

    def body(*refs):
        o_refs, l_refs, z_ref = refs[0:3], refs[3:6], refs[6]
        oo_ref, y_ref, yt_ref = refs[7:10]
        lq_refs, scratch = refs[10:13], refs[13]
        ls_ = [_merge_rows(l_refs[g], scratch, dil) for g, dil in enumerate(DILATIONS)]
        mx = jnp.maximum(jnp.maximum(ls_[0], ls_[1]), ls_[2])
        ws = [jnp.exp(l - mx) for l in ls_]
        den = ws[0] + ws[1] + ws[2]
        o = ws[0] * _merge_rows(o_refs[0], scratch, DILATIONS[0])
        for g in range(1, N_GROUPS):
            o = o + ws[g] * _merge_rows(o_refs[g], scratch, DILATIONS[g])
        o = o / den
        z = z_ref[...].astype(F32)
        y = o * (z * _sigmoid(z))
        oo_ref[...] = o.astype(BF16)
        y_ref[...] = y.astype(BF16)
        yt_ref[...] = y.T.astype(BF16)
        for g, dil in enumerate(DILATIONS):
            for r, part in enumerate(_split_rows(mx + jnp.log(den), scratch, dil)):
                lq_refs[g][r] = part

    grouped = [_grouped_spec(dil, tr, aw, lambda r: (0, r, 0)) for dil in DILATIONS]
    one = pl.BlockSpec((tr, aw), lambda r: (r, 0))
    b16 = jax.ShapeDtypeStruct((s, aw), BF16)
    out = pl.pallas_call(
        body, name="combine_groups", grid=(s // tr,),
        in_specs=grouped + grouped + [one],
        out_specs=[one, one, pl.BlockSpec((aw, tr), lambda r: (0, r))] + grouped,
        out_shape=[b16, b16, jax.ShapeDtypeStruct((aw, s), BF16)]
        + [jax.ShapeDtypeStruct((dil, s // dil, aw), F32) for dil in DILATIONS],
        scratch_shapes=[_permute_scratch(tr, aw)],
        compiler_params=_params(("parallel",)))(
            *[_grouped_view(t, dil) for t, dil in zip(os, DILATIONS)],
            *[_grouped_view(t, dil) for t, dil in zip(ls, DILATIONS)], zuz)
    return out[0], out[1], out[2], [t.reshape(s, aw) for t in out[3:]]


def _pool_counts(row0, rows, window):
    t = row0 + lax.broadcasted_iota(jnp.int32, (rows, 1), 0)
    return jnp.minimum(t + 1, window).astype(F32)


def _pool_fwd(zuz, w_pool, pool_scale, aw, pw):
    s = zuz.shape[0]
    pg = pw // len(POOL_WINDOWS)
    tr = _divisor_tile(s, 256, 128)
    u_col, z_col = aw // pw, aw // pw + 1
    assert aw % pw == 0

    def body(u_ref, up_ref, z_ref, w_ref, sc_ref, p_ref, l_ref, y_ref, yt_ref):
        r = pl.program_id(0)
        u = u_ref[...].astype(F32)
        halo = jnp.where(r > 0, up_ref[...].astype(F32), 0.0)
        ext = jnp.concatenate([halo, u], axis=0)
        pieces, lins = [], []
        for gi, window in enumerate(POOL_WINDOWS):
            cs = slice(gi * pg, (gi + 1) * pg)
            acc = ext[:, cs]
            shift = 1
            while shift < window:
                acc = acc + pltpu.roll(acc, shift, 0)
                shift *= 2
            p = acc[POOL_HALO:] / _pool_counts(r * tr, tr, window) - u[:, cs]
            pieces.append(p)
            lins.append(jnp.dot(p.astype(BF16), w_ref[gi], preferred_element_type=F32))
        p = jnp.concatenate(pieces, axis=1)
        lin = jnp.concatenate(lins, axis=1)
        z = z_ref[...].astype(F32)
        y = lin * sc_ref[...] * (z * _sigmoid(z))
        p_ref[...] = p.astype(BF16)
        l_ref[...] = lin
        y_ref[...] = y.astype(BF16)
        yt_ref[...] = y.T.astype(BF16)

    per = tr // POOL_HALO
    out = pl.BlockSpec((tr, pw), lambda r: (r, 0))
    return pl.pallas_call(
        body, name="pool_fwd", grid=(s // tr,),
        in_specs=[pl.BlockSpec((tr, pw), lambda r: (r, u_col)),
                  pl.BlockSpec((POOL_HALO, pw), lambda r: (jnp.maximum(r * per - 1, 0), u_col)),
                  pl.BlockSpec((tr, pw), lambda r: (r, z_col)),
                  pl.BlockSpec((len(POOL_WINDOWS), pg, pg), lambda r: (0, 0, 0)),
                  pl.BlockSpec((1, pw), lambda r: (0, 0))],
        out_specs=[out, out, out, pl.BlockSpec((pw, tr), lambda r: (0, r))],
        out_shape=[jax.ShapeDtypeStruct((s, pw), BF16), jax.ShapeDtypeStruct((s, pw), F32),
                   jax.ShapeDtypeStruct((s, pw), BF16), jax.ShapeDtypeStruct((pw, s), BF16)],
        compiler_params=_params(("parallel",)))(zuz, zuz, zuz, w_pool, pool_scale)


def _proj_merge(y_attn, y_pool, wpa4, wpp4, gpre, b_gate):
    s, aw = y_attn.shape
    pw = y_pool.shape[1]
    tn = wpa4.shape[2]
    d = N_CHIPS * tn
    tm = _divisor_tile(s, 512, 128)

    def body(ya_ref, yp_ref, wa_ref, wp_ref, ga_ref, gp_ref, ba_ref, bp_ref, a_ref, p_ref, sa_ref, sp_ref, m_ref,
             mt_ref):
        a = jnp.dot(ya_ref[...], wa_ref[...], preferred_element_type=F32)
        p = jnp.dot(yp_ref[...], wp_ref[...], preferred_element_type=F32)
        sa = _sigmoid(ga_ref[...].astype(F32) + ba_ref[...])
        sp = _sigmoid(gp_ref[...].astype(F32) + bp_ref[...])
        merged = sa * a + sp * p
        a_ref[...] = a.astype(BF16)
        p_ref[...] = p.astype(BF16)
        sa_ref[...] = sa.astype(BF16)
        sp_ref[...] = sp.astype(BF16)
        m_ref[...] = merged.astype(BF16)
        mt_ref[...] = merged.T.astype(BF16)

    out = pl.BlockSpec((tm, tn), lambda n, m: (m, n))
    f = jax.ShapeDtypeStruct((s, d), BF16)
    return pl.pallas_call(
        body, name="proj_merge", grid=(N_CHIPS, s // tm),
        in_specs=[pl.BlockSpec((tm, aw), lambda n, m: (m, 0)), pl.BlockSpec((tm, pw), lambda n, m: (m, 0)),
                  pl.BlockSpec((None, aw, tn), lambda n, m: (n, 0, 0)),
                  pl.BlockSpec((None, pw, tn), lambda n, m: (n, 0, 0)),
                  pl.BlockSpec((tm, tn), lambda n, m: (m, n)), pl.BlockSpec((tm, tn), lambda n, m: (m, N_CHIPS + n)),
                  pl.BlockSpec((1, tn), lambda n, m: (0, n)), pl.BlockSpec((1, tn), lambda n, m: (0, N_CHIPS + n))],
        out_specs=[out] * 5 + [pl.BlockSpec((tn, tm), lambda n, m: (n, m))],
        out_shape=[f] * 5 + [jax.ShapeDtypeStruct((d, s), BF16)],
        compiler_params=_params(("parallel", "parallel")))(y_attn, y_pool, wpa4, wpp4, gpre, gpre, b_gate, b_gate)


def _out_norm_loss(merged, w_out, x, target, gamma, beta):
    s, d = x.shape
    tm = _divisor_tile(s, 256, 16)

    def body(m_ref, w_ref, x_ref, t_ref, g_ref, b_ref, dr_ref, drb_ref, loss_ref, dg_ref, db_ref):
        @pl.when(pl.program_id(0) == 0)
        def _():
            loss_ref[...] = jnp.zeros_like(loss_ref)
            dg_ref[...] = jnp.zeros_like(dg_ref)
            db_ref[...] = jnp.zeros_like(db_ref)

        r = ALPHA * x_ref[...] + jnp.dot(m_ref[...], w_ref[...], preferred_element_type=F32)
        mu = jnp.mean(r, axis=1, keepdims=True)
        rc = r - mu
        rstd = lax.rsqrt(jnp.mean(rc * rc, axis=1, keepdims=True) + LN_EPS)
        xhat = rc * rstd
        diff = xhat * g_ref[...] + b_ref[...] - t_ref[...]
        dy = diff / d
        loss_ref[...] += jnp.sum(diff * diff, axis=0, keepdims=True)
        dg_ref[...] += jnp.sum(dy * xhat, axis=0, keepdims=True)
        db_ref[...] += jnp.sum(dy, axis=0, keepdims=True)
        dxhat = dy * g_ref[...]
        dr = rstd * (dxhat - jnp.mean(dxhat, axis=1, keepdims=True)
                     - xhat * jnp.mean(dxhat * xhat, axis=1, keepdims=True))
        dr_ref[...] = dr
        drb_ref[...] = dr.astype(BF16)

    row = pl.BlockSpec((tm, d), lambda m: (m, 0))
    vec = pl.BlockSpec((1, d), lambda m: (0, 0))
    v = jax.ShapeDtypeStruct((1, d), F32)
    return pl.pallas_call(
        body, name="out_norm_loss", grid=(s // tm,),
        in_specs=[row, pl.BlockSpec((d, d), lambda m: (0, 0)), row, row, vec, vec],
        out_specs=[row, row, vec, vec, vec],
        out_shape=[jax.ShapeDtypeStruct((s, d), F32), jax.ShapeDtypeStruct((s, d), BF16), v, v, v],
        compiler_params=_params(("arbitrary",), vmem_mib=56))(merged, w_out, x, target, gamma, beta)


def _merge_bwd(drb, w_out, a, p, sa, sp):
    s, d = drb.shape
    tm = _divisor_tile(s, 512, 16)
    tn = d // N_CHIPS

    def body(dr_ref, w_ref, a_ref, p_ref, sa_ref, sp_ref, da_ref, dp_ref, dga_ref, dgp_ref, dba_ref, dbp_ref):
        @pl.when(pl.program_id(1) == 0)
        def _():
            dba_ref[...] = jnp.zeros_like(dba_ref)
            dbp_ref[...] = jnp.zeros_like(dbp_ref)

        dm = lax.dot_general(dr_ref[...], w_ref[...], NT, preferred_element_type=F32)
        sa = sa_ref[...].astype(F32)
        sp = sp_ref[...].astype(F32)
        da_ref[...] = (dm * sa).astype(BF16)
        dp_ref[...] = (dm * sp).astype(BF16)
        dga = dm * a_ref[...].astype(F32) * sa * (1.0 - sa)
        dgp = dm * p_ref[...].astype(F32) * sp * (1.0 - sp)
        dga_ref[...] = dga.astype(BF16)
        dgp_ref[...] = dgp.astype(BF16)
        dba_ref[...] += jnp.sum(dga, axis=0, keepdims=True)
        dbp_ref[...] += jnp.sum(dgp, axis=0, keepdims=True)

    blk = pl.BlockSpec((tm, tn), lambda n, m: (m, n))
    vec = pl.BlockSpec((1, tn), lambda n, m: (0, n))
    b16 = jax.ShapeDtypeStruct((s, d), BF16)
    v = jax.ShapeDtypeStruct((1, d), F32)
    return pl.pallas_call(
        body, name="merge_bwd", grid=(N_CHIPS, s // tm),
        in_specs=[pl.BlockSpec((tm, d), lambda n, m: (m, 0)), pl.BlockSpec((tn, d), lambda n, m: (n, 0)),
                  blk, blk, blk, blk],
        out_specs=[blk, blk, blk, blk, vec, vec], out_shape=[b16, b16, b16, b16, v, v],
        compiler_params=_params(("parallel", "arbitrary")))(drb, w_out, a, p, sa, sp)


def _proj_t(dy_ref, w_ref, tn):
    acc = None
    for n in range(N_CHIPS):
        t = lax.dot_general(dy_ref[:, n * tn:(n + 1) * tn], w_ref[n], NT, preferred_element_type=F32)
        acc = t if acc is None else acc + t
    return acc


def _attn_gate_bwd(da, wpa4, zuz, o):
    s, d = da.shape
    aw, tn = wpa4.shape[1], wpa4.shape[2]
    heads = aw // HEAD_DIM
    tm = _divisor_tile(s, 256, 16 * DILATIONS[-1])

    def body(*refs):
        da_ref, w_ref, z_ref, o_ref, dz_ref = refs[:5]
        do_refs, dd_refs, scratch = refs[5:8], refs[8:11], refs[11]
        dy = _proj_t(da_ref, w_ref, tn)
        z, o = z_ref[...].astype(F32), o_ref[...].astype(F32)
        sg = _sigmoid(z)
        do = dy * (z * sg)
        dz_ref[...] = (dy * o * _dsilu(z, sg)).astype(BF16)
        prod = do * o
        dd = jnp.concatenate(
            [jnp.broadcast_to(jnp.sum(prod[:, h * HEAD_DIM:(h + 1) * HEAD_DIM], axis=1, keepdims=True),
                              (tm, HEAD_DIM)) for h in range(heads)], axis=1)
        for g, dil in enumerate(DILATIONS):
            for r, part in enumerate(_split_rows(do, scratch, dil)):
                do_refs[g][r] = part.astype(BF16)
            for r, part in enumerate(_split_rows(dd, scratch, dil)):
                dd_refs[g][r] = part

    row = pl.BlockSpec((tm, aw), lambda m: (m, 0))
    grouped = [_grouped_spec(dil, tm, aw, lambda m: (0, m, 0)) for dil in DILATIONS]
    out = pl.pallas_call(
        body, name="attn_gate_bwd", grid=(s // tm,),
        in_specs=[pl.BlockSpec((tm, d), lambda m: (m, 0)), pl.BlockSpec((N_CHIPS, aw, tn), lambda m: (0, 0, 0)),
                  row, row],
        out_specs=[row] + grouped + grouped,
        out_shape=[jax.ShapeDtypeStruct((s, aw), BF16)]
        + [jax.ShapeDtypeStruct((dil, s // dil, aw), BF16) for dil in DILATIONS]
        + [jax.ShapeDtypeStruct((dil, s // dil, aw), F32) for dil in DILATIONS],
        scratch_shapes=[_permute_scratch(tm, aw)],
        compiler_params=_params(("parallel",)))(da, wpa4, zuz, o)
    return out[0], [t.reshape(s, aw) for t in out[1:4]], [t.reshape(s, aw) for t in out[4:7]]


def _pool_gate_bwd(dp_in, wpp4, zuz, lin, pooled, w_pool, pool_scale, aw):
    s, d = dp_in.shape
    pw, tn = wpp4.shape[1], wpp4.shape[2]
    n_win = len(POOL_WINDOWS)
    pg = pw // n_win
    tm = _divisor_tile(s, 256, 16)
    z_col = aw // pw + 1

    def body(dp_ref, w_ref, z_ref, l_ref, p_ref, wp_ref, sc_ref, dz_ref, dpo_ref, dw_ref, ds_ref):
        @pl.when(pl.program_id(0) == 0)
        def _():
            dw_ref[...] = jnp.zeros_like(dw_ref)
            ds_ref[...] = jnp.zeros_like(ds_ref)

        dy = _proj_t(dp_ref, w_ref, tn)
        z, lin_ = z_ref[...].astype(F32), l_ref[...]
        sg = _sigmoid(z)
        dypp = dy * (z * sg)
        dz_ref[...] = (dy * (lin_ * sc_ref[...]) * _dsilu(z, sg)).astype(BF16)
        ds_ref[...] += jnp.sum(dypp * lin_, axis=0, keepdims=True)
        dlin = (dypp * sc_ref[...]).astype(BF16)
        for gi in range(n_win):
            cs = slice(gi * pg, (gi + 1) * pg)
            dw_ref[gi] += lax.dot_general(p_ref[:, cs], dlin[:, cs], TN, preferred_element_type=F32)
            dpo_ref[:, cs] = lax.dot_general(dlin[:, cs], wp_ref[gi], NT, preferred_element_type=F32)

    row = pl.BlockSpec((tm, pw), lambda m: (m, 0))
    return pl.pallas_call(
        body, name="pool_gate_bwd", grid=(s // tm,),
        in_specs=[pl.BlockSpec((tm, d), lambda m: (m, 0)), pl.BlockSpec((N_CHIPS, pw, tn), lambda m: (0, 0, 0)),
                  pl.BlockSpec((tm, pw), lambda m: (m, z_col)), row, row,
                  pl.BlockSpec((n_win, pg, pg), lambda m: (0, 0, 0)), pl.BlockSpec((1, pw), lambda m: (0, 0))],
        out_specs=[row, row, pl.BlockSpec((n_win, pg, pg), lambda m: (0, 0, 0)),
                   pl.BlockSpec((1, pw), lambda m: (0, 0))],
        out_shape=[jax.ShapeDtypeStruct((s, pw), BF16), jax.ShapeDtypeStruct((s, pw), F32),
                   jax.ShapeDtypeStruct((n_win, pg, pg), F32), jax.ShapeDtypeStruct((1, pw), F32)],
        compiler_params=_params(("arbitrary",)))(dp_in, wpp4, zuz, lin, pooled, w_pool, pool_scale)


def _pool_bwd(dpooled):
    s, pw = dpooled.shape
    pg = pw // len(POOL_WINDOWS)
    tr = _divisor_tile(s, 256, POOL_HALO)
    per = tr // POOL_HALO
    n_tiles = s // tr

    def body(c_ref, n_ref, du_ref):
        r = pl.program_id(0)
        cur = c_ref[...]
        halo = jnp.where(r < n_tiles - 1, n_ref[...], 0.0)
        ext = jnp.concatenate([cur, halo], axis=0)
        rows = tr + POOL_HALO
        for gi, window in enumerate(POOL_WINDOWS):
            cs = slice(gi * pg, (gi + 1) * pg)
            acc = ext[:, cs] / _pool_counts(r * tr, rows, window)
            shift = 1
            while shift < window:
                acc = acc + pltpu.roll(acc, rows - shift, 0)
                shift *= 2
            du_ref[:, cs] = (acc[:tr] - cur[:, cs]).astype(BF16)

    return pl.pallas_call(
        body, name="pool_bwd", grid=(n_tiles,),
        in_specs=[pl.BlockSpec((tr, pw), lambda r: (r, 0)),
                  pl.BlockSpec((POOL_HALO, pw), lambda r: (jnp.minimum((r + 1) * per, s // POOL_HALO - 1), 0))],
        out_specs=pl.BlockSpec((tr, pw), lambda r: (r, 0)),
        out_shape=jax.ShapeDtypeStruct((s, pw), BF16), compiler_params=_params(("parallel",)))(dpooled, dpooled)


def _attn_bwd(qkv, do, lse, dd, g):
    _, s, aw = qkv.shape
    heads = aw // HEAD_DIM
    n_blocks = s // STEPS
    per_seq = n_blocks // DILATIONS[g]
    pair = 2 if n_blocks % 2 == 0 else 1
    rows_ = pair * STEPS
    n_steps = n_blocks // pair
    tail = slice(rows_ - STEPS, rows_)

    def body(q_ref, do_ref, l_ref, dd_ref, kc_ref, kp_ref, vc_ref, vp_ref, out_ref, cq_ref, ck_ref, cv_ref):
        b = pl.program_id(0)

        @pl.when(b == 0)
        def _():
            cq_ref[...] = jnp.zeros_like(cq_ref)
            ck_ref[...] = jnp.zeros_like(ck_ref)
            cv_ref[...] = jnp.zeros_like(cv_ref)

        out_ref[0] = cq_ref[...].astype(BF16)

        @pl.when(b < n_steps)
        def _():
            masks = [_window_mask(lax.rem(b * pair + j, per_seq) == 0) for j in range(pair)]
            for h in range(heads):
                hs = slice(h * HEAD_DIM, (h + 1) * HEAD_DIM)
                keys = jnp.concatenate([kp_ref[:, hs], kc_ref[:, hs]], axis=0)
                values = jnp.concatenate([vp_ref[:, hs], vc_ref[:, hs]], axis=0)
                dks, dvs = [], []
                for j in range(pair):
                    rows = slice(j * STEPS, (j + 1) * STEPS)
                    window = slice(j * STEPS, (j + 2) * STEPS)
                    q, do_, kk, vv = q_ref[rows, hs], do_ref[rows, hs], keys[window], values[window]
                    lse_ = jnp.concatenate([l_ref[rows, hs], l_ref[rows, hs]], axis=1)
                    dd_ = jnp.concatenate([dd_ref[rows, hs], dd_ref[rows, hs]], axis=1)
                    sc = lax.dot_general(q, kk, NT, preferred_element_type=F32) * SCORE_SCALE
                    prob = jnp.where(masks[j], jnp.exp(sc - lse_), 0.0)
                    dprob = lax.dot_general(do_, vv, NT, preferred_element_type=F32)
                    dsc = prob * (dprob - dd_) * SCORE_SCALE
                    cq_ref[rows, hs] = jnp.dot(dsc.astype(BF16), kk, preferred_element_type=F32)
                    dks.append(lax.dot_general(dsc.astype(BF16), q, TN, preferred_element_type=F32))
                    dvs.append(lax.dot_general(prob.astype(BF16), do_, TN, preferred_element_type=F32))
                for which, carry, parts in ((1, ck_ref, dks), (2, cv_ref, dvs)):
                    out_ref[which, tail, hs] = (carry[tail, hs] + parts[0][:STEPS]).astype(BF16)
                    if pair > 1:
                        out_ref[which, :rows_ - STEPS, hs] = carry[:rows_ - STEPS, hs].astype(BF16)
                    for j in range(pair):
                        total = parts[j][STEPS:]
                        if j + 1 < pair:
                            total = total + parts[j + 1][:STEPS]
                        carry[j * STEPS:(j + 1) * STEPS, hs] = total

        @pl.when(b == n_steps)
        def _():
            out_ref[1] = ck_ref[...].astype(BF16)
            out_ref[2] = cv_ref[...].astype(BF16)

    last = n_steps - 1

    def cur(which):
        return pl.BlockSpec((None, rows_, aw), lambda b: (which, jnp.minimum(b, last), 0))

    def prev(which):
        return pl.BlockSpec((None, STEPS, aw), lambda b: (which, jnp.clip(b * pair - 1, 0, n_blocks - 1), 0))

    row = pl.BlockSpec((rows_, aw), lambda b: (jnp.minimum(b, last), 0))
    return pl.pallas_call(
        body, name=f"attn_bwd{g}", grid=(n_steps + 1,),
        in_specs=[cur(0), row, row, row, cur(1), prev(1), cur(2), prev(2)],
        out_specs=pl.BlockSpec((3, rows_, aw), lambda b: (0, jnp.clip(b - 1, 0, last), 0)),
        out_shape=jax.ShapeDtypeStruct((3, s, aw), BF16),
        scratch_shapes=[pltpu.VMEM((rows_, aw), F32)] * 3,
        compiler_params=_params(("arbitrary",)))(qkv, do, lse, dd, qkv, qkv, qkv, qkv)


def _weight_grad(at, b, tn, col_blocks, name):
    m, k = at.shape
    n = b.shape[1]
    tm = _divisor_tile(m, 1024, 16)
    tk = _divisor_tile(k, 2048, 128)
    nk = k // tk

    def body(a_ref, b_ref, o_ref, acc_ref):
        kk = pl.program_id(2)

        @pl.when(kk == 0)
        def _():
            acc_ref[...] = jnp.zeros_like(acc_ref)

        acc_ref[...] += jnp.dot(a_ref[...], b_ref[...], preferred_element_type=F32)

        @pl.when(kk == nk - 1)
        def _():
            o_ref[...] = acc_ref[...].astype(BF16)

    if col_blocks:
        out_spec = pl.BlockSpec((None, tm, tn), lambda i, j, kk: (j, i, 0))
        out_shape = jax.ShapeDtypeStruct((n // tn, m, tn), BF16)
    else:
        out_spec = pl.BlockSpec((tm, tn), lambda i, j, kk: (i, j))
        out_shape = jax.ShapeDtypeStruct((m, n), BF16)
    return pl.pallas_call(
        body, name=name, grid=(m // tm, n // tn, nk),
        in_specs=[pl.BlockSpec((tm, tk), lambda i, j, kk: (i, kk)), pl.BlockSpec((tk, tn), lambda i, j, kk: (kk, j))],
        out_specs=out_spec, out_shape=out_shape, scratch_shapes=[pltpu.VMEM((tm, tn), F32)],
        compiler_params=_params(("parallel", "parallel", "arbitrary")))(at, b)


def _w_in_grad_part(xt, b, col_of, n_local, tn, w_shape, prev, name):
    d, s = xt.shape
    per_chip = w_shape[2] // tn
    tm = _divisor_tile(d, 1024, 16)
    tk = _divisor_tile(s, 2048, 128)
    nk = s // tk

    def body(*refs):
        a_ref, b_ref, o_ref, acc_ref = refs[0], refs[1], refs[-2], refs[-1]
        kk = pl.program_id(2)

        @pl.when(kk == 0)
        def _():
            acc_ref[...] = jnp.zeros_like(acc_ref)

        acc_ref[...] += jnp.dot(a_ref[...], b_ref[...], preferred_element_type=F32)

        @pl.when(kk == nk - 1)
        def _():
            o_ref[...] = acc_ref[...].astype(BF16)

    if b.ndim == 3:
        sub = b.shape[2] // tn
        b_spec = pl.BlockSpec((None, tk, tn), lambda j, i, kk: (j // sub, kk, j % sub))
    else:
        b_spec = pl.BlockSpec((tk, tn), lambda j, i, kk: (kk, j))
    in_specs = [pl.BlockSpec((tm, tk), lambda j, i, kk: (i, kk)), b_spec]
    args = [xt, b]
    aliases = {}
    if prev is not None:
        in_specs.append(ANY)
        args.append(prev)
        aliases = {2: 0}
    return pl.pallas_call(
        body, name=name, grid=(n_local, d // tm, nk), in_specs=in_specs,
        out_specs=pl.BlockSpec((None, tm, tn), lambda j, i, kk: (col_of(j) // per_chip, i, col_of(j) % per_chip)),
        out_shape=jax.ShapeDtypeStruct(w_shape, BF16), scratch_shapes=[pltpu.VMEM((tm, tn), F32)],
        input_output_aliases=aliases,
        compiler_params=_params(("parallel", "parallel", "arbitrary")))(*args)


def _assemble_w(wcs, after):
    n, d, wc = wcs[0].shape
    tr = _divisor_tile(d, 256, 16)

    def body(after_ref, *refs):
        o_ref = refs[-1]
        for ch in range(W_CHUNKS):
            o_ref[:, ch * wc:(ch + 1) * wc] = refs[ch][...]

    return pl.pallas_call(
        body, name="assemble_w", grid=(n, d // tr),
        in_specs=[pl.BlockSpec(after.shape, lambda b, r: (0, 0))]
        + [pl.BlockSpec((None, tr, wc), lambda b, r: (b, r, 0))] * W_CHUNKS,
        out_specs=pl.BlockSpec((None, tr, W_CHUNKS * wc), lambda b, r: (b, r, 0)),
        out_shape=jax.ShapeDtypeStruct((n, d, W_CHUNKS * wc), wcs[0].dtype),
        compiler_params=_params(("parallel", "parallel")))(after, *wcs)


def _x_grad(dqkv, rest, w4, dr, aw, tn):
    s, d = dr.shape
    sub = aw // tn
    n_qkv = 3 * N_GROUPS * sub
    los, lo = [], n_qkv
    for p in rest:
        los.append(lo)
        lo += p.shape[1] // tn
    n_blocks = lo
    per_chip = n_blocks // N_CHIPS
    tm = _divisor_tile(s, 512, 16 * DILATIONS[-1])

    def body(*refs):
        q_refs, r_refs = refs[:N_GROUPS], refs[N_GROUPS:N_GROUPS + len(rest)]
        w_ref, dr_ref, o_ref, acc_ref, scratch = refs[-5:]
        j = pl.program_id(1)

        @pl.when(j == 0)
        def _():
            acc_ref[...] = ALPHA * dr_ref[...]

        for g, dil in enumerate(DILATIONS):
            @pl.when((j < n_qkv) & (lax.rem(j // sub, N_GROUPS) == g))
            def _(g=g, dil=dil):
                rows = _merge_rows(q_refs[g], scratch, dil).astype(BF16)
                acc_ref[...] += lax.dot_general(rows, w_ref[...], NT, preferred_element_type=F32)

        for p_ref, lo_, piece in zip(r_refs, los, rest):
            @pl.when((j >= lo_) & (j < lo_ + piece.shape[1] // tn))
            def _(p_ref=p_ref):
                acc_ref[...] += lax.dot_general(p_ref[...], w_ref[...], NT, preferred_element_type=F32)

        @pl.when(j == n_blocks - 1)
        def _():
            o_ref[...] = acc_ref[...]

    def qkv_spec(dil):
        def index(i, j):
            region = jnp.minimum(j // sub, 3 * N_GROUPS - 1)
            return region // N_GROUPS, 0, i, jnp.where(j < n_qkv, j % sub, 0)

        return pl.BlockSpec((None, dil, tm // dil, tn), index)

    def rest_spec(lo_, piece):
        n = piece.shape[1] // tn
        return pl.BlockSpec((tm, tn), lambda i, j: (i, jnp.clip(j - lo_, 0, n - 1)))

    row = pl.BlockSpec((tm, d), lambda i, j: (i, 0))
    return pl.pallas_call(
        body, name="x_grad", grid=(s // tm, n_blocks),
        in_specs=[qkv_spec(dil) for dil in DILATIONS] + [rest_spec(lo_, p) for lo_, p in zip(los, rest)]
        + [pl.BlockSpec((None, d, tn), lambda i, j: (j // per_chip, 0, j % per_chip)), row],
        out_specs=row, out_shape=jax.ShapeDtypeStruct((s, d), F32),
        scratch_shapes=[pltpu.VMEM((tm, d), F32), _permute_scratch(tm, tn)],
        compiler_params=_params(("parallel", "arbitrary"), vmem_mib=56))(
            *[t.reshape(3, dil, s // dil, aw) for t, dil in zip(dqkv, DILATIONS)], *rest, w4, dr)


def _prepare_x(x, after=None):
    s, d = x.shape
    tc = 2 * LANES
    slabs = tc // LANES
    ordered = [] if after is None else [after]

    def body(*refs):
        x_ref, xb_ref = refs[len(ordered):len(ordered) + 2]
        xt_refs, scratch = refs[len(ordered) + 2:len(ordered) + 2 + N_GROUPS], refs[-1]
        t = x_ref[...]
        xb_ref[...] = t.astype(BF16)
        for c in range(slabs):
            scratch[c] = t[:, c * LANES:(c + 1) * LANES]
        for g, dil in enumerate(DILATIONS):
            length = s // dil
            for r in range(dil):
                part = t if dil == 1 else jnp.concatenate(
                    [scratch[c, pl.ds(r, length, stride=dil), :] for c in range(slabs)], axis=1)
                xt_refs[g][:, r * length:(r + 1) * length] = part.T.astype(BF16)

    col = pl.BlockSpec((s, tc), lambda j: (0, j))
    row = pl.BlockSpec((tc, s), lambda j: (j, 0))
    t_shape = jax.ShapeDtypeStruct((d, s), BF16)
    out = pl.pallas_call(
        body, name="prepare_x", grid=(d // tc,),
        in_specs=[pl.BlockSpec(t.shape, lambda j: (0, 0)) for t in ordered] + [col],
        out_specs=[col] + [row] * N_GROUPS,
        out_shape=[jax.ShapeDtypeStruct((s, d), BF16)] + [t_shape] * N_GROUPS,
        scratch_shapes=[_permute_scratch(s, tc)], compiler_params=_params(("parallel",)))(*ordered, x)
    return out[0], out[1:]


def _local_step(x, target, w_open, w_close, w_width, b_gate, pool_scale, gamma, beta, aw, pw, small_weights,
                start_exchange=None, first_token=None):
    s, d = x.shape
    tn = _col_tile(aw, pw, w_width)
    sub = aw // tn
    per_chip = w_width // tn
    qkv_w = 3 * N_GROUPS * aw
    w_shape = (N_CHIPS, d, w_width)

    regions = [dict(kind=g, blocks=[(which * N_GROUPS + g) * sub + i for which in range(3) for i in range(sub)])
               for g in range(N_GROUPS)]
    lo = qkv_w // tn
    for name, width in (("zuz", aw + 2 * pw), ("gates", 2 * d)):
        regions.append(dict(kind=name, blocks=list(range(lo, lo + width // tn)), j0=lo, width=width))
        lo += width // tn
    results = [None] * len(regions)
    xb, xts = _prepare_x(x, first_token)
    wcs, last = [], []
    w_open(0, [xb])
    for ch in range(W_CHUNKS):
        wc, token = w_close(ch, last)
        wcs.append(wc)
        calls = []
        for i, region in enumerate(regions):
            blocks = [b for b in region["blocks"] if _chunk_of(b, per_chip) == ch]
            if blocks:
                calls.append((i, region, blocks))
        done = []
        for k, (i, region, blocks) in enumerate(calls):
            after = [token]
            if k == len(calls) - 1 and ch + 1 < W_CHUNKS:
                after.append(w_open(ch + 1, done))
            if region["kind"] in range(N_GROUPS):
                results[i] = _in_proj_qkv(xb, wc, region["kind"], blocks, aw, tn, results[i], after,
                                          f"in_proj_qkv{region['kind']}_{ch}")
            else:
                results[i] = _in_proj(xb, wc, blocks, region["j0"], region["width"], tn, BF16, results[i], after,
                                      f"in_proj_{region['kind']}_{ch}")
            done.append(results[i])
        last = done[-1:]
    qkv = [results[g].reshape(3, s, aw) for g in range(N_GROUPS)]
    zuz, gpre = results[N_GROUPS], results[N_GROUPS + 1]

    attn = [_attn_fwd(qkv[g], g) for g in range(N_GROUPS)]
    o, y_attn, y_attn_t, lse = _combine_groups([a[0] for a in attn], [a[1] for a in attn], zuz, aw)
    w_pool, wpa4, wpp4, w_out = small_weights(o)
    pooled, lin, y_pool, y_pool_t = _pool_fwd(zuz, w_pool, pool_scale, aw, pw)
    a, p, sa, sp, merged, merged_t = _proj_merge(y_attn, y_pool, wpa4, wpp4, gpre, b_gate)
    dr, drb, loss_lanes, d_gamma, d_beta = _out_norm_loss(merged, w_out, x, target, gamma, beta)

    da, dp, d_gpre_a, d_gpre_p, d_b_a, d_b_p = _merge_bwd(drb, w_out, a, p, sa, sp)
    d_b_gate = jnp.concatenate([d_b_a, d_b_p], axis=1)
    d_w_out = _weight_grad(merged_t, drb, d // N_CHIPS, False, "w_out_grad")
    d_wpa4 = _weight_grad(y_attn_t, da, d // N_CHIPS, True, "w_proj_attn_grad")
    d_wpp4 = _weight_grad(y_pool_t, dp, d // N_CHIPS, True, "w_proj_pool_grad")
    d_z_attn, d_o, dd = _attn_gate_bwd(da, wpa4, zuz, o)
    d_z_pool, d_pooled, d_w_pool, d_pool_scale = _pool_gate_bwd(dp, wpp4, zuz, lin, pooled, w_pool, pool_scale, aw)
    d_u = _pool_bwd(d_pooled)
    dqkv = [_attn_bwd(qkv[g], d_o[g], lse[g], dd[g], g) for g in range(N_GROUPS)]

    rest = [d_z_attn, d_u, d_z_pool, d_gpre_a, d_gpre_p]
    d_w_in4 = None
    for g in range(N_GROUPS):
        d_w_in4 = _w_in_grad_part(xts[g], dqkv[g], lambda j, g=g: ((j // sub) * N_GROUPS + g) * sub + j % sub,
                                  3 * sub, tn, w_shape, d_w_in4, f"w_in_grad_qkv{g}")
    lo = qkv_w // tn
    for i, piece in enumerate(rest):
        n_local = piece.shape[1] // tn
        d_w_in4 = _w_in_grad_part(xts[0], piece, lambda j, lo=lo: lo + j, n_local, tn, w_shape, d_w_in4,
                                  f"w_in_grad_rest{i}")
        lo += n_local
    grads = dict(loss_lanes=loss_lanes, w_in=d_w_in4, b_gate=d_b_gate, w_pool=d_w_pool,
                 pool_scale=d_pool_scale, w_proj_attn=d_wpa4, w_proj_pool=d_wpp4, w_out=d_w_out,
                 ln_gamma=d_gamma, ln_beta=d_beta)
    token = jnp.zeros((8, 128), F32) if start_exchange is None else start_exchange(grads)
    grads["d_x"] = _x_grad(dqkv, rest, _assemble_w(wcs, token), dr, aw, tn)
    return grads


def _stack_column_blocks(t):
    lead, (r, c) = t.shape[:-2], t.shape[-2:]
    width = c // N_CHIPS
    return jnp.moveaxis(t.reshape(lead + (r, N_CHIPS, width)), -2, -3).reshape(lead + (N_CHIPS * r, width))


def _unstack_column_blocks(t):
    lead, (nr, width) = t.shape[:-2], t.shape[-2:]
    r = nr // N_CHIPS
    return jnp.moveaxis(t.reshape(lead + (N_CHIPS, r, width)), -3, -2).reshape(lead + (r, N_CHIPS * width))


def _pack_small(wpa, wpp, w_out, w_pool):
    width = wpa.shape[1]
    return jnp.concatenate([wpa, wpp, _stack_column_blocks(w_out), w_pool.reshape(-1, width)], axis=0)


def _unpack_small(packed, aw, pw, d, pg):
    lead = packed.shape[:-2]
    r0, r1, r2 = aw, aw + pw, aw + pw + d
    return (packed[..., :r0, :], packed[..., r0:r1, :], _unstack_column_blocks(packed[..., r1:r2, :]),
            packed[..., r2:, :].reshape(lead + (len(POOL_WINDOWS), pg // N_CHIPS, pg)))


def _pack_rows(vectors, rows):
    flat = jnp.concatenate([v.reshape(-1) for v in vectors])
    return jnp.pad(flat, (0, rows * 128 - flat.shape[0])).reshape(rows, 128)


def _unpack_rows(packed, sizes):
    flat, out, lo = packed.reshape(-1), [], 0
    for n in sizes:
        out.append(flat[lo:lo + n].reshape(1, n))
        lo += n
    return out


def kernel(x, w_in, b_gate, w_pool, pool_scale, w_proj_attn, w_proj_pool, w_out, ln_gamma, ln_beta, loss_target, m_w_in, m_b_gate, m_w_pool, m_pool_scale, m_w_proj_attn, m_w_proj_pool, m_w_out, m_ln_gamma, m_ln_beta, v_w_in, v_b_gate, v_w_pool, v_pool_scale, v_w_proj_attn, v_w_proj_pool, v_w_out, v_ln_gamma, v_ln_beta):
    s, d = x.shape[1], x.shape[2]
    aw, pw = w_proj_attn.shape[1], w_proj_pool.shape[1]
    pg = w_pool.shape[3]
    n_win = len(POOL_WINDOWS)

    def small(wpa, wpp, wo, wpl):
        return _pack_small(wpa[0], wpp[0], wo[0], wpl[0])

    chip = 2 * lax.axis_index("x") + lax.axis_index("y")
    core = lax.axis_index("c")

    flight = {"chunk": _halves_start(_place_block(w_in[0], N_CHIPS, chip, BF16, "place_w_in0", 0, W_CHUNKS), x,
                                     "gather_w_in0_start")}
    first_token = flight["chunk"][2]
    w_small = small(w_proj_attn, w_proj_pool, w_out, w_pool) + first_token[0, 0]
    placed = [None] + [_place_block(w_in[0], N_CHIPS, chip, BF16, f"place_w_in{ch}", ch, W_CHUNKS, first_token)
                       for ch in range(1, W_CHUNKS)]
    placed_small = _place_block(w_small, N_CHIPS, chip, BF16, "place_w_small", after=first_token)

    def w_open(ch, after):
        sems, thru, _ = flight["chunk"]
        if ch == 0:
            after = after + placed[1:] + [placed_small]
        landed = _halves_wait(sems, thru, after, f"gather_w_in{ch}_wait")
        if ch + 1 < W_CHUNKS:
            flight["chunk"] = _halves_start(placed[ch + 1], landed, f"gather_w_in{ch + 1}_start")
            flight["token"] = flight["chunk"][2]
        else:
            flight["small"] = _broadcast_start(placed_small, landed, "gather_small_start")
            flight["token"] = flight["small"][2]
        flight["forward"] = _forward_start(landed, f"forward_w_in{ch}_start")
        return flight["forward"][2]

    def w_close(ch, after):
        sems, thru, _ = flight["forward"]
        return _forward_wait(sems, thru, after, f"forward_w_in{ch}_wait"), flight["token"]

    def small_weights(after):
        sems, thru, _ = flight["small"]
        small4 = _broadcast_wait(sems, thru, after, "gather_small_wait")
        wpa4, wpp4, w_out4, w_pool4 = _unpack_small(small4, aw, pw, d, pg)
        return w_pool4.transpose(1, 0, 2, 3).reshape(n_win, pg, pg), wpa4, wpp4, w_out4.reshape(d, d)

    exchange = {}

    def start_exchange(g):
        g_pool4 = g["w_pool"].reshape(n_win, N_CHIPS, pg // N_CHIPS, pg).transpose(1, 0, 2, 3).astype(BF16)
        g_out4 = g["w_out"].reshape(N_CHIPS, d // N_CHIPS, d)
        g_small4 = jnp.concatenate([g["w_proj_attn"], g["w_proj_pool"], _stack_column_blocks(g_out4),
                                    g_pool4.reshape(N_CHIPS, -1, d // N_CHIPS)], axis=1)
        theirs_big, theirs_small = _swap_halves([g["w_in"], g_small4])
        chip_big, placed_big = _add_halves(g["w_in"], theirs_big, core, chip, "add_cores_big")
        chip_small, placed_small = _add_halves(g_small4, theirs_small, core, chip, "add_cores_small")
        sems, sums, placed, token = _scatter_start([chip_big, chip_small], [placed_big, placed_small])
        exchange.update(sems=sems, sums=sums, placed=placed)
        return token

    g = _local_step(x[0], loss_target[0], w_open, w_close, w_in.shape[2], b_gate, pool_scale, ln_gamma, ln_beta,
                    aw, pw, small_weights, start_exchange, first_token)

    sizes = [b_gate.shape[1], pool_scale.shape[1], d, d, 1]
    rows = -(-sum(sizes) // (8 * 128)) * 8
    loss_part = (0.5 / d) * jnp.sum(g["loss_lanes"]).reshape(1, 1)
    parts = _gather_rows(_pack_rows([g["b_gate"], g["pool_scale"], g["ln_gamma"], g["ln_beta"], loss_part], rows))
    zero = jnp.zeros((1, 1), F32)
    packed = [_pack_rows(vs, rows) for vs in ([b_gate, pool_scale, ln_gamma, ln_beta, zero],
                                              [m_b_gate, m_pool_scale, m_ln_gamma, m_ln_beta, zero],
                                              [v_b_gate, v_pool_scale, v_ln_gamma, v_ln_beta, zero])]
    replicated = _sum_rows_adamw(parts, *packed)
    rep = [_unpack_rows(t, sizes) for t in replicated]
    loss = rep[0][4].reshape(())

    got_big, got_small = _scatter_wait(exchange["sems"], exchange["sums"], exchange["placed"],
                                       [g["d_x"], replicated[0]])
    join_sems, halves = _join_start([_sum_slots(got_big, core, "sum_chips_big"),
                                     _sum_slots(got_small, core, "sum_chips_small")])
    mv_small = (small(m_w_proj_attn, m_w_proj_pool, m_w_out, m_w_pool),
                small(v_w_proj_attn, v_w_proj_pool, v_w_out, v_w_pool))
    upd_in = _adamw_half(w_in[0], halves[0], m_w_in[0], v_w_in[0], core, None, "adamw_w_in_own")
    upd_small = _adamw_half(w_small, halves[1], *mv_small, core, None, "adamw_small_own")
    grad_w_in, grad_small = _join_wait(join_sems, halves, [upd_in[0], upd_small[0]])
    upd_in = _adamw_half(w_in[0], grad_w_in, m_w_in[0], v_w_in[0], 1 - core, upd_in, "adamw_w_in_other")
    upd_small = _adamw_half(w_small, grad_small, *mv_small, 1 - core, upd_small, "adamw_small_other")
    grad_w_in, grad_small = upd_in[3], upd_small[3]

    def leaves(big, packed_small, replicated):
        wpa_, wpp_, wo_, wpl_ = _unpack_small(packed_small, aw, pw, d, pg)
        return [big[None], replicated[0], wpl_[None], replicated[1], wpa_[None], wpp_[None], wo_[None],
                replicated[2], replicated[3]]

    out = [loss, g["d_x"][None]]
    out += leaves(grad_w_in, grad_small, rep[0])
    for i in range(3):
        out += leaves(upd_in[i], upd_small[i], rep[1 + i])
    return tuple(out)
```

```python
import math

import jax
import jax.numpy as jnp
from jax import lax
from jax.experimental import pallas as pl
from jax.experimental.pallas import tpu as pltpu

F32 = jnp.float32
BF16 = jnp.bfloat16
MESH = pl.DeviceIdType.MESH
ANY = pl.BlockSpec(memory_space=pl.ANY)

HEAD_DIM = 128
STEPS = 128
DILATIONS = (1, 4, 16)
N_GROUPS = len(DILATIONS)
POOL_WINDOWS = (2, 4, 8, 16)
POOL_HALO = 16
N_CHIPS = 4
N_DEV = 8
ALPHA = 2.0 ** 0.25
LN_EPS = 1e-5
NEG_INF = -1e30
SCORE_SCALE = HEAD_DIM ** -0.5
ADAM_LR = 0.001
ADAM_B1 = 0.9
ADAM_B2 = 0.999
ADAM_EPS = 1e-08
ADAM_WD = 0.01
ADAM_STEP = 10
MIB = 2 ** 20
NT = (((1,), (1,)), ((), ()))
TN = (((0,), (0,)), ((), ()))
DMA_STREAMS = 8


def _params(semantics=None, vmem_mib=48):
    return pltpu.CompilerParams(dimension_semantics=semantics, vmem_limit_bytes=vmem_mib * MIB)


def _divisor_tile(n, target, multiple):
    best = None
    for t in range(multiple, min(n, target) + 1, multiple):
        if n % t == 0:
            best = t
    assert best is not None, (n, target, multiple)
    return best


def _col_tile(*widths):
    g = 0
    for w in widths:
        g = math.gcd(g, w)
    return _divisor_tile(g, 1024, 128)


def _sigmoid(z):
    return jax.nn.sigmoid(z)


def _dsilu(z, sg):
    return sg * (1.0 + z * (1.0 - sg))


def _place():
    x, y, c = lax.axis_index("x"), lax.axis_index("y"), lax.axis_index("c")
    others = [(1 - x, y), (x, 1 - y), (1 - x, 1 - y)]
    return x, y, c, (x, y, 1 - c), others


def _remote(src, dst, send_sem, recv_sem, dev):
    return pltpu.make_async_remote_copy(src_ref=src, dst_ref=dst, send_sem=send_sem, recv_sem=recv_sem,
                                        device_id=dev, device_id_type=MESH)


def _row_pieces(n_rows, streams=DMA_STREAMS, multiple=16):
    size = -(-n_rows // (streams * multiple)) * multiple
    return [(lo, min(size, n_rows - lo)) for lo in range(0, n_rows, size)]


def _start_streams(make, n_rows):
    for lo, size in _row_pieces(n_rows):
        make(pl.ds(lo, size)).start()


def _half_copies(buf, send_sems, recv_sems):
    x, y, c, _, others = _place()
    half = buf.shape[1] // 2
    slab = buf.at[2 * x + y, pl.ds(c * half, half)]
    return [_remote(slab, slab, send_sems[j], recv_sems[j], (ox, oy, c)) for j, (ox, oy) in enumerate(others)]


def _halves_start(placed, after, name):
    k = N_CHIPS - 1

    def body(buf, after_ref, *refs):
        send_sems, recv_sems, token = refs[:k], refs[k:2 * k], refs[-1]
        for cp in _half_copies(buf, send_sems, recv_sems):
            cp.start()
        token[...] = jnp.zeros_like(token)

    out = pl.pallas_call(
        body, name=name,
        out_shape=[pltpu.SemaphoreType.DMA(())] * (2 * k) + [pltpu.HBM(placed.shape, placed.dtype),
                                                             jax.ShapeDtypeStruct((8, 128), F32)],
        in_specs=[HBM, ANY], out_specs=[SEM] * (2 * k) + [HBM, pl.BlockSpec(memory_space=pltpu.VMEM)],
        input_output_aliases={0: 2 * k},
        compiler_params=pltpu.CompilerParams(has_side_effects=DATAFLOW),
    )(pltpu.with_memory_space_constraint(placed, pltpu.HBM), after)
    return out[:2 * k], out[2 * k], out[-1]


def _halves_wait(sems, placed, after, name):
    k = N_CHIPS - 1

    def body(buf, *refs):
        send_sems, recv_sems = refs[:k], refs[k:2 * k]
        for cp in _half_copies(buf, send_sems, recv_sems):
            cp.wait_send()
            cp.wait_recv()

    return pl.pallas_call(
        body, name=name, out_shape=pltpu.HBM(placed.shape, placed.dtype),
        in_specs=[HBM] + [SEM] * (2 * k) + [ANY] * len(after), out_specs=HBM, input_output_aliases={0: 0},
        compiler_params=pltpu.CompilerParams(has_side_effects=DATAFLOW),
    )(placed, *sems, *after)


def _forward_copies(buf, send_sems, recv_sems):
    x, y, c, sibling, others = _place()
    half = buf.shape[1] // 2
    copies = []
    for j, (ox, oy) in enumerate(others):
        slab = buf.at[2 * ox + oy, pl.ds(c * half, half)]
        copies.append(_remote(slab, slab, send_sems[j], recv_sems[j], sibling))
    return copies


def _advance_start(landed, nxt, whole_blocks, name):
    k = N_CHIPS - 1

    def body(a, b, *refs):
        for cp in _forward_copies(a, refs[:k], refs[k:2 * k]):
            cp.start()
        for cp in (_broadcast_copies if whole_blocks else _half_copies)(b, refs[2 * k:3 * k], refs[3 * k:4 * k]):
            cp.start()
        refs[-1][...] = jnp.zeros_like(refs[-1])

    out = pl.pallas_call(
        body, name=name,
        out_shape=[pltpu.SemaphoreType.DMA(())] * (4 * k) + [pltpu.HBM(landed.shape, landed.dtype),
                                                             pltpu.HBM(nxt.shape, nxt.dtype),
                                                             jax.ShapeDtypeStruct((8, 128), F32)],
        in_specs=[HBM, HBM], out_specs=[SEM] * (4 * k) + [HBM, HBM, pl.BlockSpec(memory_space=pltpu.VMEM)],
        input_output_aliases={0: 4 * k, 1: 4 * k + 1},
        compiler_params=pltpu.CompilerParams(has_side_effects=DATAFLOW),
    )(pltpu.with_memory_space_constraint(landed, pltpu.HBM), pltpu.with_memory_space_constraint(nxt, pltpu.HBM))
    token = out[-1]
    return (out[:2 * k], out[4 * k], token), (out[2 * k:4 * k], out[4 * k + 1], token)


def _forward_wait(sems, buf, after, name):
    k = N_CHIPS - 1

    def body(b, *refs):
        send_sems, recv_sems = refs[:k], refs[k:2 * k]
        for cp in _forward_copies(b, send_sems, recv_sems):
            cp.wait_send()
            cp.wait_recv()

    return pl.pallas_call(
        body, name=name, out_shape=pltpu.HBM(buf.shape, buf.dtype),
        in_specs=[HBM] + [SEM] * (2 * k) + [ANY] * len(after), out_specs=HBM, input_output_aliases={0: 0},
        compiler_params=pltpu.CompilerParams(has_side_effects=DATAFLOW),
    )(buf, *sems, *after)


def _swap_halves(grads):
    n = len(grads)

    def body(*refs):
        g, theirs = refs[:n], refs[n:2 * n]
        send_sems, recv_sems = refs[2 * n:]
        x, y, c, sibling, _ = _place()
        for i in range(n):
            half = g[i].shape[1] // 2
            give = (1 - c) * half
            for b in range(N_CHIPS):
                _start_streams(lambda r, i=i, b=b: _remote(
                    g[i].at[b, pl.ds(give + r.start, r.size)], theirs[i].at[b, r], send_sems.at[i], recv_sems.at[i],
                    sibling), half)
        for i in range(n):
            _remote(theirs[i], theirs[i], send_sems.at[i], recv_sems.at[i], sibling).wait()

    return pl.pallas_call(
        body, name="swap_halves",
        out_shape=[jax.ShapeDtypeStruct((s.shape[0], s.shape[1] // 2) + s.shape[2:], s.dtype) for s in grads],
        in_specs=[ANY] * n, out_specs=[ANY] * n,
        scratch_shapes=[pltpu.SemaphoreType.DMA((n,)), pltpu.SemaphoreType.DMA((n,))],
    )(*grads)


HBM = pl.BlockSpec(memory_space=pltpu.HBM)
SEM = pl.BlockSpec(memory_space=pltpu.SEMAPHORE)
DATAFLOW = pltpu.SideEffectType.DATAFLOW_SIDE_EFFECTING


def _broadcast_copies(buf, send_sems, recv_sems):
    x, y, c, _, others = _place()
    mine = buf.at[2 * x + y]
    return [_remote(mine, mine, send_sems[j], recv_sems[j], (ox, oy, c)) for j, (ox, oy) in enumerate(others)]


def _broadcast_wait(sems, placed, after, name):
    k = N_CHIPS - 1

    def body(buf, *refs):
        send_sems, recv_sems = refs[:k], refs[k:2 * k]
        for cp in _broadcast_copies(buf, send_sems, recv_sems):
            cp.wait_send()
            cp.wait_recv()

    return pl.pallas_call(
        body, name=name, out_shape=pltpu.HBM(placed.shape, placed.dtype),
        in_specs=[HBM] + [SEM] * (2 * k) + [ANY], out_specs=HBM, input_output_aliases={0: 0},
        compiler_params=pltpu.CompilerParams(has_side_effects=DATAFLOW),
    )(placed, *sems, after)


def _scatter_copies(s, got, send_sems, recv_sems):
    x, y, c, _, others = _place()
    me = 2 * x + y
    n = len(s)
    return [_remote(s[i].at[2 * ox + oy], got[i].at[me], send_sems[3 * i + j], recv_sems[3 * i + j], (ox, oy, c))
            for i in range(n) for j, (ox, oy) in enumerate(others)]


def _scatter_start(sums, placed):
    n = len(sums)
    k = 3 * n

    def body(*refs):
        s, got, token = refs[:n], refs[n:2 * n], refs[-1]
        send_sems, recv_sems = refs[2 * n:2 * n + k], refs[2 * n + k:2 * n + 2 * k]
        for cp in _scatter_copies(s, got, send_sems, recv_sems):
            cp.start()
        token[...] = jnp.zeros_like(token)

    hbm = [pltpu.HBM(a.shape, a.dtype) for a in list(sums) + list(placed)]
    out = pl.pallas_call(
        body, name="scatter_start",
        out_shape=[pltpu.SemaphoreType.DMA(())] * (2 * k) + hbm + [jax.ShapeDtypeStruct((8, 128), F32)],
        in_specs=[HBM] * (2 * n), out_specs=[SEM] * (2 * k) + [HBM] * (2 * n) + [pl.BlockSpec(memory_space=pltpu.VMEM)],
        input_output_aliases={i: 2 * k + i for i in range(2 * n)},
        compiler_params=pltpu.CompilerParams(has_side_effects=DATAFLOW),
    )(*[pltpu.with_memory_space_constraint(a, pltpu.HBM) for a in list(sums) + list(placed)])
    return out[:2 * k], out[2 * k:2 * k + n], out[2 * k + n:2 * k + 2 * n], out[-1]


def _scatter_wait(sems, sums, placed, after):
    n = len(sums)
    k = 3 * n

    def body(*refs):
        s, got = refs[:n], refs[n:2 * n]
        send_sems, recv_sems = refs[2 * n:2 * n + k], refs[2 * n + k:2 * n + 2 * k]
        for cp in _scatter_copies(s, got, send_sems, recv_sems):
            cp.wait_send()
            cp.wait_recv()

    hbm = [pltpu.HBM(a.shape, a.dtype) for a in list(sums) + list(placed)]
    out = pl.pallas_call(
        body, name="scatter_wait", out_shape=hbm,
        in_specs=[HBM] * (2 * n) + [SEM] * (2 * k) + [ANY] * len(after), out_specs=[HBM] * (2 * n),
        input_output_aliases={i: i for i in range(2 * n)},
        compiler_params=pltpu.CompilerParams(has_side_effects=DATAFLOW),
    )(*sums, *placed, *sems, *after)
    return out[n:]


def _join_copies(bufs, send_sems, recv_sems):
    x, y, c, sibling, _ = _place()
    return [_remote(b.at[c], b.at[c], send_sems[i], recv_sems[i], sibling) for i, b in enumerate(bufs)]


def _join_start(placed):
    n = len(placed)

    def body(*refs):
        bufs, send_sems, recv_sems = refs[:n], refs[n:2 * n], refs[2 * n:3 * n]
        for cp in _join_copies(bufs, send_sems, recv_sems):
            cp.start()

    hbm = [pltpu.HBM(a.shape, a.dtype) for a in placed]
    out = pl.pallas_call(
        body, name="join_start", out_shape=[pltpu.SemaphoreType.DMA(())] * (2 * n) + hbm,
        in_specs=[HBM] * n, out_specs=[SEM] * (2 * n) + [HBM] * n,
        input_output_aliases={i: 2 * n + i for i in range(n)},
        compiler_params=pltpu.CompilerParams(has_side_effects=DATAFLOW),
    )(*[pltpu.with_memory_space_constraint(a, pltpu.HBM) for a in placed])
    return out[:2 * n], out[2 * n:]


def _join_wait(sems, placed, after):
    n = len(placed)

    def body(*refs):
        bufs, send_sems, recv_sems = refs[:n], refs[n:2 * n], refs[2 * n:3 * n]
        for cp in _join_copies(bufs, send_sems, recv_sems):
            cp.wait_send()
            cp.wait_recv()

    return pl.pallas_call(
        body, name="join_wait", out_shape=[pltpu.HBM(a.shape, a.dtype) for a in placed],
        in_specs=[HBM] * n + [SEM] * (2 * n) + [ANY] * len(after), out_specs=[HBM] * n,
        input_output_aliases={i: i for i in range(n)},
        compiler_params=pltpu.CompilerParams(has_side_effects=DATAFLOW),
    )(*placed, *sems, *after)


def _gather_rows(row):
    def body(row_ref, out_ref, send_sems, recv_sems, local_sem):
        x, y, c = lax.axis_index("x"), lax.axis_index("y"), lax.axis_index("c")
        me = 4 * x + 2 * y + c
        local = pltpu.make_async_copy(row_ref, out_ref.at[me], local_sem)
        local.start()
        sent = []
        peers = []
        for k in range(1, N_DEV):
            px, py, pc = x ^ (k >> 2), y ^ ((k >> 1) & 1), c ^ (k & 1)
            peers.append((k, px, py, pc))
            cp = _remote(row_ref, out_ref.at[me], send_sems.at[k - 1], recv_sems.at[k - 1], (px, py, pc))
            cp.start()
            sent.append(cp)
        for k, px, py, pc in peers:
            slot = out_ref.at[4 * px + 2 * py + pc]
            _remote(slot, slot, send_sems.at[k - 1], recv_sems.at[k - 1], (px, py, pc)).wait_recv()
        for cp in sent:
            cp.wait_send()
        local.wait()

    return pl.pallas_call(
        body, name="gather_rows", out_shape=jax.ShapeDtypeStruct((N_DEV,) + row.shape, row.dtype),
        in_specs=[ANY], out_specs=ANY,
        scratch_shapes=[pltpu.SemaphoreType.DMA((N_DEV - 1,)), pltpu.SemaphoreType.DMA((N_DEV - 1,)),
                        pltpu.SemaphoreType.DMA],
    )(row)


def _scalar(i):
    return jnp.reshape(i, (1,)).astype(jnp.int32)


def _place_block(src, n_slots, slot, out_dtype, name, window=0, n_windows=1, after=None):
    rows, cols = src.shape[0], src.shape[1] // n_windows
    tr = _divisor_tile(rows, max(16, (2 * MIB) // (cols * 4)), 16)
    ordered = [] if after is None else [after]

    def body(slot_ref, *refs):
        s_ref, o_ref = refs[len(ordered):]
        o_ref[...] = s_ref[...].astype(o_ref.dtype)

    return pl.pallas_call(
        body, name=name, out_shape=jax.ShapeDtypeStruct((n_slots, rows, cols), out_dtype),
        grid_spec=pltpu.PrefetchScalarGridSpec(
            num_scalar_prefetch=1, grid=(rows // tr,),
            in_specs=[pl.BlockSpec(t.shape, lambda r, sl: (0, 0)) for t in ordered]
            + [pl.BlockSpec((tr, cols), lambda r, sl: (r, window))],
            out_specs=pl.BlockSpec((None, tr, cols), lambda r, sl: (sl[0], r, 0))),
        compiler_params=_params(("parallel",)))(_scalar(slot), *ordered, src)


def _add_halves(g, theirs, core, chip, name):
    n, half, cols = theirs.shape
    tr = _divisor_tile(half, max(16, (4 * MIB) // (cols * 4)), 16)
    per = half // tr

    def body(at_ref, a_ref, b_ref, o_ref, own_ref):
        total = (a_ref[...].astype(F32) + b_ref[...].astype(F32)).astype(o_ref.dtype)
        o_ref[...] = total

        @pl.when(pl.program_id(1) == at_ref[1])
        def _():
            own_ref[...] = total

    spec = pl.BlockSpec((None, tr, cols), lambda r, i, at: (i, r, 0))
    shape = jax.ShapeDtypeStruct(theirs.shape, BF16)
    return pl.pallas_call(
        body, name=name, out_shape=[shape, shape],
        grid_spec=pltpu.PrefetchScalarGridSpec(
            num_scalar_prefetch=1, grid=(per, n),
            in_specs=[pl.BlockSpec((None, tr, cols), lambda r, i, at: (i, at[0] * per + r, 0)), spec],
            out_specs=[spec, pl.BlockSpec((None, tr, cols), lambda r, i, at: (at[1], r, 0))]),
        compiler_params=_params(("parallel", "arbitrary")))(jnp.concatenate([_scalar(core), _scalar(chip)]), g, theirs)


def _sum_slots(a, core, name):
    n, rows, cols = a.shape
    tr = _divisor_tile(rows, max(16, (8 * MIB) // (cols * 4 * n)), 16)

    def body(c_ref, a_ref, o_ref):
        acc = a_ref[0].astype(F32)
        for i in range(1, n):
            acc = acc + a_ref[i].astype(F32)
        o_ref[...] = acc

    return pl.pallas_call(
        body, name=name, out_shape=jax.ShapeDtypeStruct((2, rows, cols), F32),
        grid_spec=pltpu.PrefetchScalarGridSpec(
            num_scalar_prefetch=1, grid=(rows // tr,),
            in_specs=[pl.BlockSpec((n, tr, cols), lambda r, c: (0, r, 0))],
            out_specs=pl.BlockSpec((None, tr, cols), lambda r, c: (c[0], r, 0))),
        compiler_params=_params(("parallel",)))(_scalar(core), a)


def _adamw_math(w, g, m, v):
    m = ADAM_B1 * m + (1.0 - ADAM_B1) * g
    v = ADAM_B2 * v + (1.0 - ADAM_B2) * (g * g)
    m_hat = m / (1.0 - ADAM_B1 ** ADAM_STEP)
    v_hat = v / (1.0 - ADAM_B2 ** ADAM_STEP)
    delta = -ADAM_LR * (m_hat / (jnp.sqrt(v_hat) + ADAM_EPS) + ADAM_WD * w)
    return delta, m, v


def _adamw_half(w, g2, m, v, which, prev, name):
    rows, cols = w.shape
    half = rows // 2
    tr = _divisor_tile(half, max(8, (2 * MIB) // (cols * 4)), 8)
    per = half // tr
    n_out = 4

    def body(h_ref, w_ref, g_ref, m_ref, v_ref, *refs):
        d_ref, nm_ref, nv_ref, go_ref = refs[-n_out:]
        g = g_ref[...]
        d, nm, nv = _adamw_math(w_ref[...], g, m_ref[...], v_ref[...])
        d_ref[...] = d
        nm_ref[...] = nm
        nv_ref[...] = nv
        go_ref[...] = g

    spec = pl.BlockSpec((tr, cols), lambda r, h: (h[0] * per + r, 0))
    in_specs = [spec, pl.BlockSpec((None, tr, cols), lambda r, h: (h[0], r, 0)), spec, spec]
    args = [_scalar(which), w, g2, m, v]
    aliases = {}
    if prev is not None:
        aliases = {len(args) + i: i for i in range(n_out)}
        in_specs += [ANY] * n_out
        args += list(prev)
    return pl.pallas_call(
        body, name=name, out_shape=[jax.ShapeDtypeStruct((rows, cols), F32)] * n_out,
        grid_spec=pltpu.PrefetchScalarGridSpec(num_scalar_prefetch=1, grid=(per,), in_specs=in_specs,
                                               out_specs=[spec] * n_out),
        input_output_aliases=aliases, compiler_params=_params(("parallel",)))(*args)


def _sum_rows_adamw(parts, w, m, v):
    def body(p_ref, w_ref, m_ref, v_ref, g_ref, d_ref, nm_ref, nv_ref):
        g = p_ref[0]
        for i in range(1, N_DEV):
            g = g + p_ref[i]
        d, nm, nv = _adamw_math(w_ref[...], g, m_ref[...], v_ref[...])
        g_ref[...] = g
        d_ref[...] = d
        nm_ref[...] = nm
        nv_ref[...] = nv

    shape = jax.ShapeDtypeStruct(w.shape, F32)
    return pl.pallas_call(body, name="sum_rows_adamw", out_shape=[shape] * 4)(parts, w, m, v)


LANES = 128


def _permute_scratch(rows, width):
    return pltpu.VMEM((width // LANES, rows, LANES), F32)


def _split_rows(value, scratch, dil):
    if dil == 1:
        return [value]
    rows = value.shape[0] // dil
    slabs = value.shape[1] // LANES
    for c in range(slabs):
        scratch[c] = value[:, c * LANES:(c + 1) * LANES]
    return [jnp.concatenate([scratch[c, pl.ds(r, rows, stride=dil), :] for c in range(slabs)], axis=1)
            for r in range(dil)]


def _merge_rows(ref, scratch, dil):
    if dil == 1:
        return ref[0].astype(F32)
    rows = ref.shape[1]
    slabs = ref.shape[2] // LANES
    for r in range(dil):
        part = ref[r].astype(F32)
        for c in range(slabs):
            scratch[c, pl.ds(r, rows, stride=dil), :] = part[:, c * LANES:(c + 1) * LANES]
    return jnp.concatenate([scratch[c] for c in range(slabs)], axis=1)


def _grouped_view(t, dil):
    return t.reshape(dil, t.shape[0] // dil, t.shape[1])


def _grouped_spec(dil, rows, width, index):
    return pl.BlockSpec((dil, rows // dil, width), index)


W_CHUNKS = 4


def _pick(values, j):
    out = values[-1]
    for i in range(len(values) - 2, -1, -1):
        out = jnp.where(j == i, values[i], out)
    return out


def _chunk_of(col, per_chip):
    return (col % per_chip) // (per_chip // W_CHUNKS)


def _w_block(col, per_chip):
    return col // per_chip, 0, (col % per_chip) % (per_chip // W_CHUNKS)


def _in_proj(xb, wc, blocks, j0, ncols, tn, out_dtype, prev, after, name):
    s, d = xb.shape
    per_chip = wc.shape[2] * W_CHUNKS // tn
    tm = _divisor_tile(s, 1024, 16)
    extra = [t for t in after if t is not None]

    def body(*refs):
        a_ref, b_ref = refs[len(extra):len(extra) + 2]
        o_ref = refs[-1]
        o_ref[...] = jnp.dot(a_ref[...], b_ref[...], preferred_element_type=F32).astype(o_ref.dtype)

    in_specs = [pl.BlockSpec(t.shape, lambda j, m: (0, 0)) for t in extra] + [
        pl.BlockSpec((tm, d), lambda j, m: (m, 0)),
        pl.BlockSpec((None, d, tn), lambda j, m: _w_block(_pick(blocks, j), per_chip))]
    args = extra + [xb, wc]
    aliases = {}
    if prev is not None:
        aliases = {len(args): 0}
        in_specs.append(ANY)
        args.append(prev)
    return pl.pallas_call(
        body, name=name, grid=(len(blocks), s // tm), in_specs=in_specs,
        out_specs=pl.BlockSpec((tm, tn), lambda j, m: (m, _pick(blocks, j) - j0)),
        out_shape=jax.ShapeDtypeStruct((s, ncols), out_dtype), input_output_aliases=aliases,
        compiler_params=_params(("parallel", "parallel")))(*args)


def _in_proj_qkv(xb, wc, g, blocks, aw, tn, prev, after, name):
    s, d = xb.shape
    dil = DILATIONS[g]
    per_chip = wc.shape[2] * W_CHUNKS // tn
    sub = aw // tn
    tm = _divisor_tile(s, 1024, 16 * dil)
    extra = [t for t in after if t is not None]

    def body(*refs):
        a_ref, b_ref = refs[len(extra):len(extra) + 2]
        o_ref, scratch = refs[-2:]
        res = jnp.dot(a_ref[...], b_ref[...], preferred_element_type=F32)
        for r, part in enumerate(_split_rows(res, scratch, dil)):
            o_ref[r] = part.astype(BF16)

    def out_index(j, m):
        col = _pick(blocks, j)
        return (col // sub) // N_GROUPS, 0, m, col % sub

    in_specs = [pl.BlockSpec(t.shape, lambda j, m: (0, 0)) for t in extra] + [
        pl.BlockSpec((tm, d), lambda j, m: (m, 0)),
        pl.BlockSpec((None, d, tn), lambda j, m: _w_block(_pick(blocks, j), per_chip))]
    args = extra + [xb, wc]
    aliases = {}
    if prev is not None:
        aliases = {len(args): 0}
        in_specs.append(ANY)
        args.append(prev)
    return pl.pallas_call(
        body, name=name, grid=(len(blocks), s // tm), in_specs=in_specs,
        out_specs=pl.BlockSpec((None, dil, tm // dil, tn), out_index),
        out_shape=jax.ShapeDtypeStruct((3, dil, s // dil, aw), BF16), input_output_aliases=aliases,
        scratch_shapes=[_permute_scratch(tm, tn)],
        compiler_params=_params(("parallel", "parallel")))(*args)


def _window_mask(first):
    qi = lax.broadcasted_iota(jnp.int32, (STEPS, 2 * STEPS), 0)
    kj = lax.broadcasted_iota(jnp.int32, (STEPS, 2 * STEPS), 1)
    lowest = jnp.where(first, STEPS, 0)
    return (kj >= qi) & (kj <= qi + STEPS) & (kj >= lowest)


def _attn_fwd(qkv, g):
    _, s, aw = qkv.shape
    heads = aw // HEAD_DIM
    n_blocks = s // STEPS
    per_seq = n_blocks // DILATIONS[g]
    pair = 2 if n_blocks % 2 == 0 else 1

    def body(q_ref, kc_ref, kp_ref, vc_ref, vp_ref, o_ref, l_ref):
        masks = [_window_mask(lax.rem(pl.program_id(0) * pair + j, per_seq) == 0) for j in range(pair)]
        for h in range(heads):
            hs = slice(h * HEAD_DIM, (h + 1) * HEAD_DIM)
            keys = jnp.concatenate([kp_ref[:, hs], kc_ref[:, hs]], axis=0)
            values = jnp.concatenate([vp_ref[:, hs], vc_ref[:, hs]], axis=0)
            for j in range(pair):
                rows = slice(j * STEPS, (j + 1) * STEPS)
                window = slice(j * STEPS, (j + 2) * STEPS)
                sc = lax.dot_general(q_ref[rows, hs], keys[window], NT, preferred_element_type=F32) * SCORE_SCALE
                sc = jnp.where(masks[j], sc, NEG_INF)
                mx = jnp.max(sc, axis=1, keepdims=True)
                e = jnp.exp(sc - mx)
                den = jnp.sum(e, axis=1, keepdims=True)
                o_ref[rows, hs] = (jnp.dot(e.astype(BF16), values[window], preferred_element_type=F32)
                                   / den).astype(BF16)
                l_ref[rows, hs] = jnp.broadcast_to(mx + jnp.log(den), (STEPS, HEAD_DIM))

    def cur(which):
        return pl.BlockSpec((None, pair * STEPS, aw), lambda b: (which, b, 0))

    def prev(which):
        return pl.BlockSpec((None, STEPS, aw), lambda b: (which, jnp.maximum(pair * b - 1, 0), 0))

    out = pl.BlockSpec((pair * STEPS, aw), lambda b: (b, 0))
    return pl.pallas_call(
        body, name=f"attn_fwd{g}", grid=(n_blocks // pair,),
        in_specs=[cur(0), cur(1), prev(1), cur(2), prev(2)], out_specs=[out, out],
        out_shape=[jax.ShapeDtypeStruct((s, aw), BF16), jax.ShapeDtypeStruct((s, aw), F32)],
        compiler_params=_params(("parallel",)))(qkv, qkv, qkv, qkv, qkv)


def _combine_groups(os, ls, zuz, aw):
    s = zuz.shape[0]
    tr = _divisor_tile(s, 256, 8 * DILATIONS[-1])

    def body(*refs):
        o_refs, l_refs, z_ref = refs[0:3], refs[3:6], refs[6]
        oo_ref, y_ref, yt_ref = refs[7:10]
        lq_refs, scratch = refs[10:13], refs[13]
        ls_ = [_merge_rows(l_refs[g], scratch, dil) for g, dil in enumerate(DILATIONS)]
        mx = jnp.maximum(jnp.maximum(ls_[0], ls_[1]), ls_[2])
        ws = [jnp.exp(l - mx) for l in ls_]
        den = ws[0] + ws[1] + ws[2]
        o = ws[0] * _merge_rows(o_refs[0], scratch, DILATIONS[0])
        for g in range(1, N_GROUPS):
            o = o + ws[g] * _merge_rows(o_refs[g], scratch, DILATIONS[g])
        o = o / den
        z = z_ref[...].astype(F32)
        y = o * (z * _sigmoid(z))
        oo_ref[...] = o.astype(BF16)
        y_ref[...] = y.astype(BF16)
        yt_ref[...] = y.T.astype(BF16)
        for g, dil in enumerate(DILATIONS):
            for r, part in enumerate(_split_rows(mx + jnp.log(den), scratch, dil)):
                lq_refs[g][r] = part

    grouped = [_grouped_spec(dil, tr, aw, lambda r: (0, r, 0)) for dil in DILATIONS]
    one = pl.BlockSpec((tr, aw), lambda r: (r, 0))
    b16 = jax.ShapeDtypeStruct((s, aw), BF16)
    out = pl.pallas_call(
        body, name="combine_groups", grid=(s // tr,),
        in_specs=grouped + grouped + [one],
        out_specs=[one, one, pl.BlockSpec((aw, tr), lambda r: (0, r))] + grouped,
        out_shape=[b16, b16, jax.ShapeDtypeStruct((aw, s), BF16)]
        + [jax.ShapeDtypeStruct((dil, s // dil, aw), F32) for dil in DILATIONS],
        scratch_shapes=[_permute_scratch(tr, aw)],
        compiler_params=_params(("parallel",)))(
            *[_grouped_view(t, dil) for t, dil in zip(os, DILATIONS)],
            *[_grouped_view(t, dil) for t, dil in zip(ls, DILATIONS)], zuz)
    return out[0], out[1], out[2], [t.reshape(s, aw) for t in out[3:]]


def _pool_counts(row0, rows, window):
    t = row0 + lax.broadcasted_iota(jnp.int32, (rows, 1), 0)
    return jnp.minimum(t + 1, window).astype(F32)


def _pool_fwd(zuz, w_pool, pool_scale, aw, pw):
    s = zuz.shape[0]
    pg = pw // len(POOL_WINDOWS)
    tr = _divisor_tile(s, 256, 128)
    u_col, z_col = aw // pw, aw // pw + 1
    assert aw % pw == 0

    def body(u_ref, up_ref, z_ref, w_ref, sc_ref, p_ref, l_ref, y_ref, yt_ref):
        r = pl.program_id(0)
        u = u_ref[...].astype(F32)
        halo = jnp.where(r > 0, up_ref[...].astype(F32), 0.0)
        ext = jnp.concatenate([halo, u], axis=0)
        pieces, lins = [], []
        for gi, window in enumerate(POOL_WINDOWS):
            cs = slice(gi * pg, (gi + 1) * pg)
            acc = ext[:, cs]
            shift = 1
            while shift < window:
                acc = acc + pltpu.roll(acc, shift, 0)
                shift *= 2
            p = acc[POOL_HALO:] / _pool_counts(r * tr, tr, window) - u[:, cs]
            pieces.append(p)
            lins.append(jnp.dot(p.astype(BF16), w_ref[gi], preferred_element_type=F32))
        p = jnp.concatenate(pieces, axis=1)
        lin = jnp.concatenate(lins, axis=1)
        z = z_ref[...].astype(F32)
        y = lin * sc_ref[...] * (z * _sigmoid(z))
        p_ref[...] = p.astype(BF16)
        l_ref[...] = lin
        y_ref[...] = y.astype(BF16)
        yt_ref[...] = y.T.astype(BF16)

    per = tr // POOL_HALO
    out = pl.BlockSpec((tr, pw), lambda r: (r, 0))
    return pl.pallas_call(
        body, name="pool_fwd", grid=(s // tr,),
        in_specs=[pl.BlockSpec((tr, pw), lambda r: (r, u_col)),
                  pl.BlockSpec((POOL_HALO, pw), lambda r: (jnp.maximum(r * per - 1, 0), u_col)),
                  pl.BlockSpec((tr, pw), lambda r: (r, z_col)),
                  pl.BlockSpec((len(POOL_WINDOWS), pg, pg), lambda r: (0, 0, 0)),
                  pl.BlockSpec((1, pw), lambda r: (0, 0))],
        out_specs=[out, out, out, pl.BlockSpec((pw, tr), lambda r: (0, r))],
        out_shape=[jax.ShapeDtypeStruct((s, pw), BF16), jax.ShapeDtypeStruct((s, pw), F32),
                   jax.ShapeDtypeStruct((s, pw), BF16), jax.ShapeDtypeStruct((pw, s), BF16)],
        compiler_params=_params(("parallel",)))(zuz, zuz, zuz, w_pool, pool_scale)


def _proj_merge(y_attn, y_pool, wpa4, wpp4, gpre, b_gate):
    s, aw = y_attn.shape
    pw = y_pool.shape[1]
    tn = wpa4.shape[2]
    d = N_CHIPS * tn
    tm = _divisor_tile(s, 512, 128)

    def body(ya_ref, yp_ref, wa_ref, wp_ref, ga_ref, gp_ref, ba_ref, bp_ref, a_ref, p_ref, sa_ref, sp_ref, m_ref,
             mt_ref):
        a = jnp.dot(ya_ref[...], wa_ref[...], preferred_element_type=F32)
        p = jnp.dot(yp_ref[...], wp_ref[...], preferred_element_type=F32)
        sa = _sigmoid(ga_ref[...].astype(F32) + ba_ref[...])
        sp = _sigmoid(gp_ref[...].astype(F32) + bp_ref[...])
        merged = sa * a + sp * p
        a_ref[...] = a.astype(BF16)
        p_ref[...] = p.astype(BF16)
        sa_ref[...] = sa.astype(BF16)
        sp_ref[...] = sp.astype(BF16)
        m_ref[...] = merged.astype(BF16)
        mt_ref[...] = merged.T.astype(BF16)

    out = pl.BlockSpec((tm, tn), lambda n, m: (m, n))
    f = jax.ShapeDtypeStruct((s, d), BF16)
    return pl.pallas_call(
        body, name="proj_merge", grid=(N_CHIPS, s // tm),
        in_specs=[pl.BlockSpec((tm, aw), lambda n, m: (m, 0)), pl.BlockSpec((tm, pw), lambda n, m: (m, 0)),
                  pl.BlockSpec((None, aw, tn), lambda n, m: (n, 0, 0)),
                  pl.BlockSpec((None, pw, tn), lambda n, m: (n, 0, 0)),
                  pl.BlockSpec((tm, tn), lambda n, m: (m, n)), pl.BlockSpec((tm, tn), lambda n, m: (m, N_CHIPS + n)),
                  pl.BlockSpec((1, tn), lambda n, m: (0, n)), pl.BlockSpec((1, tn), lambda n, m: (0, N_CHIPS + n))],
        out_specs=[out] * 5 + [pl.BlockSpec((tn, tm), lambda n, m: (n, m))],
        out_shape=[f] * 5 + [jax.ShapeDtypeStruct((d, s), BF16)],
        compiler_params=_params(("parallel", "parallel")))(y_attn, y_pool, wpa4, wpp4, gpre, gpre, b_gate, b_gate)


def _out_norm_loss(merged, w_out, x, target, gamma, beta):
    s, d = x.shape
    tm = _divisor_tile(s, 256, 16)

    def body(m_ref, w_ref, x_ref, t_ref, g_ref, b_ref, dr_ref, drb_ref, loss_ref, dg_ref, db_ref):
        @pl.when(pl.program_id(0) == 0)
        def _():
            loss_ref[...] = jnp.zeros_like(loss_ref)
            dg_ref[...] = jnp.zeros_like(dg_ref)
            db_ref[...] = jnp.zeros_like(db_ref)

        r = ALPHA * x_ref[...] + jnp.dot(m_ref[...], w_ref[...], preferred_element_type=F32)
        mu = jnp.mean(r, axis=1, keepdims=True)
        rc = r - mu
        rstd = lax.rsqrt(jnp.mean(rc * rc, axis=1, keepdims=True) + LN_EPS)
        xhat = rc * rstd
        diff = xhat * g_ref[...] + b_ref[...] - t_ref[...]
        dy = diff / d
        loss_ref[...] += jnp.sum(diff * diff, axis=0, keepdims=True)
        dg_ref[...] += jnp.sum(dy * xhat, axis=0, keepdims=True)
        db_ref[...] += jnp.sum(dy, axis=0, keepdims=True)
        dxhat = dy * g_ref[...]
        dr = rstd * (dxhat - jnp.mean(dxhat, axis=1, keepdims=True)
                     - xhat * jnp.mean(dxhat * xhat, axis=1, keepdims=True))
        dr_ref[...] = dr
        drb_ref[...] = dr.astype(BF16)

    row = pl.BlockSpec((tm, d), lambda m: (m, 0))
    vec = pl.BlockSpec((1, d), lambda m: (0, 0))
    v = jax.ShapeDtypeStruct((1, d), F32)
    return pl.pallas_call(
        body, name="out_norm_loss", grid=(s // tm,),
        in_specs=[row, pl.BlockSpec((d, d), lambda m: (0, 0)), row, row, vec, vec],
        out_specs=[row, row, vec, vec, vec],
        out_shape=[jax.ShapeDtypeStruct((s, d), F32), jax.ShapeDtypeStruct((s, d), BF16), v, v, v],
        compiler_params=_params(("arbitrary",), vmem_mib=56))(merged, w_out, x, target, gamma, beta)


def _merge_bwd(drb, w_out, a, p, sa, sp):
    s, d = drb.shape
    tm = _divisor_tile(s, 512, 16)
    tn = d // N_CHIPS

    def body(dr_ref, w_ref, a_ref, p_ref, sa_ref, sp_ref, da_ref, dp_ref, dga_ref, dgp_ref, dba_ref, dbp_ref):
        @pl.when(pl.program_id(1) == 0)
        def _():
            dba_ref[...] = jnp.zeros_like(dba_ref)
            dbp_ref[...] = jnp.zeros_like(dbp_ref)

        dm = lax.dot_general(dr_ref[...], w_ref[...], NT, preferred_element_type=F32)
        sa = sa_ref[...].astype(F32)
        sp = sp_ref[...].astype(F32)
        da_ref[...] = (dm * sa).astype(BF16)
        dp_ref[...] = (dm * sp).astype(BF16)
        dga = dm * a_ref[...].astype(F32) * sa * (1.0 - sa)
        dgp = dm * p_ref[...].astype(F32) * sp * (1.0 - sp)
        dga_ref[...] = dga.astype(BF16)
        dgp_ref[...] = dgp.astype(BF16)
        dba_ref[...] += jnp.sum(dga, axis=0, keepdims=True)
        dbp_ref[...] += jnp.sum(dgp, axis=0, keepdims=True)

    blk = pl.BlockSpec((tm, tn), lambda n, m: (m, n))
    vec = pl.BlockSpec((1, tn), lambda n, m: (0, n))
    b16 = jax.ShapeDtypeStruct((s, d), BF16)
    v = jax.ShapeDtypeStruct((1, d), F32)
    return pl.pallas_call(
        body, name="merge_bwd", grid=(N_CHIPS, s // tm),
        in_specs=[pl.BlockSpec((tm, d), lambda n, m: (m, 0)), pl.BlockSpec((tn, d), lambda n, m: (n, 0)),
                  blk, blk, blk, blk],
        out_specs=[blk, blk, blk, blk, vec, vec], out_shape=[b16, b16, b16, b16, v, v],
        compiler_params=_params(("parallel", "arbitrary")))(drb, w_out, a, p, sa, sp)


def _proj_t(dy_ref, w_ref, tn):
    acc = None
    for n in range(N_CHIPS):
        t = lax.dot_general(dy_ref[:, n * tn:(n + 1) * tn], w_ref[n], NT, preferred_element_type=F32)
        acc = t if acc is None else acc + t
    return acc


def _attn_gate_bwd(da, wpa4, zuz, o):
    s, d = da.shape
    aw, tn = wpa4.shape[1], wpa4.shape[2]
    heads = aw // HEAD_DIM
    tm = _divisor_tile(s, 256, 16 * DILATIONS[-1])

    def body(*refs):
        da_ref, w_ref, z_ref, o_ref, dz_ref = refs[:5]
        do_refs, dd_refs, scratch = refs[5:8], refs[8:11], refs[11]
        dy = _proj_t(da_ref, w_ref, tn)
        z, o = z_ref[...].astype(F32), o_ref[...].astype(F32)
        sg = _sigmoid(z)
        do = dy * (z * sg)
        dz_ref[...] = (dy * o * _dsilu(z, sg)).astype(BF16)
        prod = do * o
        dd = jnp.concatenate(
            [jnp.broadcast_to(jnp.sum(prod[:, h * HEAD_DIM:(h + 1) * HEAD_DIM], axis=1, keepdims=True),
                              (tm, HEAD_DIM)) for h in range(heads)], axis=1)
        for g, dil in enumerate(DILATIONS):
            for r, part in enumerate(_split_rows(do, scratch, dil)):
                do_refs[g][r] = part.astype(BF16)
            for r, part in enumerate(_split_rows(dd, scratch, dil)):
                dd_refs[g][r] = part

    row = pl.BlockSpec((tm, aw), lambda m: (m, 0))
    grouped = [_grouped_spec(dil, tm, aw, lambda m: (0, m, 0)) for dil in DILATIONS]
    out = pl.pallas_call(
        body, name="attn_gate_bwd", grid=(s // tm,),
        in_specs=[pl.BlockSpec((tm, d), lambda m: (m, 0)), pl.BlockSpec((N_CHIPS, aw, tn), lambda m: (0, 0, 0)),
                  row, row],
        out_specs=[row] + grouped + grouped,
        out_shape=[jax.ShapeDtypeStruct((s, aw), BF16)]
        + [jax.ShapeDtypeStruct((dil, s // dil, aw), BF16) for dil in DILATIONS]
        + [jax.ShapeDtypeStruct((dil, s // dil, aw), F32) for dil in DILATIONS],
        scratch_shapes=[_permute_scratch(tm, aw)],
        compiler_params=_params(("parallel",)))(da, wpa4, zuz, o)
    return out[0], [t.reshape(s, aw) for t in out[1:4]], [t.reshape(s, aw) for t in out[4:7]]


def _pool_gate_bwd(dp_in, wpp4, zuz, lin, pooled, w_pool, pool_scale, aw):
    s, d = dp_in.shape
    pw, tn = wpp4.shape[1], wpp4.shape[2]
    n_win = len(POOL_WINDOWS)
    pg = pw // n_win
    tm = _divisor_tile(s, 256, 16)
    z_col = aw // pw + 1

    def body(dp_ref, w_ref, z_ref, l_ref, p_ref, wp_ref, sc_ref, dz_ref, dpo_ref, dw_ref, ds_ref):
        @pl.when(pl.program_id(0) == 0)
        def _():
            dw_ref[...] = jnp.zeros_like(dw_ref)
            ds_ref[...] = jnp.zeros_like(ds_ref)

        dy = _proj_t(dp_ref, w_ref, tn)
        z, lin_ = z_ref[...].astype(F32), l_ref[...]
        sg = _sigmoid(z)
        dypp = dy * (z * sg)
        dz_ref[...] = (dy * (lin_ * sc_ref[...]) * _dsilu(z, sg)).astype(BF16)
        ds_ref[...] += jnp.sum(dypp * lin_, axis=0, keepdims=True)
        dlin = (dypp * sc_ref[...]).astype(BF16)
        for gi in range(n_win):
            cs = slice(gi * pg, (gi + 1) * pg)
            dw_ref[gi] += lax.dot_general(p_ref[:, cs], dlin[:, cs], TN, preferred_element_type=F32)
            dpo_ref[:, cs] = lax.dot_general(dlin[:, cs], wp_ref[gi], NT, preferred_element_type=F32)

    row = pl.BlockSpec((tm, pw), lambda m: (m, 0))
    return pl.pallas_call(
        body, name="pool_gate_bwd", grid=(s // tm,),
        in_specs=[pl.BlockSpec((tm, d), lambda m: (m, 0)), pl.BlockSpec((N_CHIPS, pw, tn), lambda m: (0, 0, 0)),
                  pl.BlockSpec((tm, pw), lambda m: (m, z_col)), row, row,
                  pl.BlockSpec((n_win, pg, pg), lambda m: (0, 0, 0)), pl.BlockSpec((1, pw), lambda m: (0, 0))],
        out_specs=[row, row, pl.BlockSpec((n_win, pg, pg), lambda m: (0, 0, 0)),
                   pl.BlockSpec((1, pw), lambda m: (0, 0))],
        out_shape=[jax.ShapeDtypeStruct((s, pw), BF16), jax.ShapeDtypeStruct((s, pw), F32),
                   jax.ShapeDtypeStruct((n_win, pg, pg), F32), jax.ShapeDtypeStruct((1, pw), F32)],
        compiler_params=_params(("arbitrary",)))(dp_in, wpp4, zuz, lin, pooled, w_pool, pool_scale)


def _pool_bwd(dpooled):
    s, pw = dpooled.shape
    pg = pw // len(POOL_WINDOWS)
    tr = _divisor_tile(s, 256, POOL_HALO)
    per = tr // POOL_HALO
    n_tiles = s // tr

    def body(c_ref, n_ref, du_ref):
        r = pl.program_id(0)
        cur = c_ref[...]
        halo = jnp.where(r < n_tiles - 1, n_ref[...], 0.0)
        ext = jnp.concatenate([cur, halo], axis=0)
        rows = tr + POOL_HALO
        for gi, window in enumerate(POOL_WINDOWS):
            cs = slice(gi * pg, (gi + 1) * pg)
            acc = ext[:, cs] / _pool_counts(r * tr, rows, window)
            shift = 1
            while shift < window:
                acc = acc + pltpu.roll(acc, rows - shift, 0)
                shift *= 2
            du_ref[:, cs] = (acc[:tr] - cur[:, cs]).astype(BF16)

    return pl.pallas_call(
        body, name="pool_bwd", grid=(n_tiles,),
        in_specs=[pl.BlockSpec((tr, pw), lambda r: (r, 0)),
                  pl.BlockSpec((POOL_HALO, pw), lambda r: (jnp.minimum((r + 1) * per, s // POOL_HALO - 1), 0))],
        out_specs=pl.BlockSpec((tr, pw), lambda r: (r, 0)),
        out_shape=jax.ShapeDtypeStruct((s, pw), BF16), compiler_params=_params(("parallel",)))(dpooled, dpooled)


def _attn_bwd(qkv, do, lse, dd, g):
    _, s, aw = qkv.shape
    heads = aw // HEAD_DIM
    n_blocks = s // STEPS
    per_seq = n_blocks // DILATIONS[g]
    pair = 2 if n_blocks % 2 == 0 else 1
    rows_ = pair * STEPS
    n_steps = n_blocks // pair
    tail = slice(rows_ - STEPS, rows_)

    def body(q_ref, do_ref, l_ref, dd_ref, kc_ref, kp_ref, vc_ref, vp_ref, out_ref, cq_ref, ck_ref, cv_ref):
        b = pl.program_id(0)

        @pl.when(b == 0)
        def _():
            cq_ref[...] = jnp.zeros_like(cq_ref)
            ck_ref[...] = jnp.zeros_like(ck_ref)
            cv_ref[...] = jnp.zeros_like(cv_ref)

        out_ref[0] = cq_ref[...].astype(BF16)

        @pl.when(b < n_steps)
        def _():
            masks = [_window_mask(lax.rem(b * pair + j, per_seq) == 0) for j in range(pair)]
            for h in range(heads):
                hs = slice(h * HEAD_DIM, (h + 1) * HEAD_DIM)
                keys = jnp.concatenate([kp_ref[:, hs], kc_ref[:, hs]], axis=0)
                values = jnp.concatenate([vp_ref[:, hs], vc_ref[:, hs]], axis=0)
                dks, dvs = [], []
                for j in range(pair):
                    rows = slice(j * STEPS, (j + 1) * STEPS)
                    window = slice(j * STEPS, (j + 2) * STEPS)
                    q, do_, kk, vv = q_ref[rows, hs], do_ref[rows, hs], keys[window], values[window]
                    lse_ = jnp.concatenate([l_ref[rows, hs], l_ref[rows, hs]], axis=1)
                    dd_ = jnp.concatenate([dd_ref[rows, hs], dd_ref[rows, hs]], axis=1)
                    sc = lax.dot_general(q, kk, NT, preferred_element_type=F32) * SCORE_SCALE
                    prob = jnp.where(masks[j], jnp.exp(sc - lse_), 0.0)
                    dprob = lax.dot_general(do_, vv, NT, preferred_element_type=F32)
                    dsc = prob * (dprob - dd_) * SCORE_SCALE
                    cq_ref[rows, hs] = jnp.dot(dsc.astype(BF16), kk, preferred_element_type=F32)
                    dks.append(lax.dot_general(dsc.astype(BF16), q, TN, preferred_element_type=F32))
                    dvs.append(lax.dot_general(prob.astype(BF16), do_, TN, preferred_element_type=F32))
                for which, carry, parts in ((1, ck_ref, dks), (2, cv_ref, dvs)):
                    out_ref[which, tail, hs] = (carry[tail, hs] + parts[0][:STEPS]).astype(BF16)
                    if pair > 1:
                        out_ref[which, :rows_ - STEPS, hs] = carry[:rows_ - STEPS, hs].astype(BF16)
                    for j in range(pair):
                        total = parts[j][STEPS:]
                        if j + 1 < pair:
                            total = total + parts[j + 1][:STEPS]
                        carry[j * STEPS:(j + 1) * STEPS, hs] = total

        @pl.when(b == n_steps)
        def _():
            out_ref[1] = ck_ref[...].astype(BF16)
            out_ref[2] = cv_ref[...].astype(BF16)

    last = n_steps - 1

    def cur(which):
        return pl.BlockSpec((None, rows_, aw), lambda b: (which, jnp.minimum(b, last), 0))

    def prev(which):
        return pl.BlockSpec((None, STEPS, aw), lambda b: (which, jnp.clip(b * pair - 1, 0, n_blocks - 1), 0))

    row = pl.BlockSpec((rows_, aw), lambda b: (jnp.minimum(b, last), 0))
    return pl.pallas_call(
        body, name=f"attn_bwd{g}", grid=(n_steps + 1,),
        in_specs=[cur(0), row, row, row, cur(1), prev(1), cur(2), prev(2)],
        out_specs=pl.BlockSpec((3, rows_, aw), lambda b: (0, jnp.clip(b - 1, 0, last), 0)),
        out_shape=jax.ShapeDtypeStruct((3, s, aw), BF16),
        scratch_shapes=[pltpu.VMEM((rows_, aw), F32)] * 3,
        compiler_params=_params(("arbitrary",)))(qkv, do, lse, dd, qkv, qkv, qkv, qkv)


def _weight_grad(at, b, tn, col_blocks, name):
    m, k = at.shape
    n = b.shape[1]
    tm = _divisor_tile(m, 1024, 16)
    tk = _divisor_tile(k, 2048, 128)
    nk = k // tk

    def body(a_ref, b_ref, o_ref, acc_ref):
        kk = pl.program_id(2)

        @pl.when(kk == 0)
        def _():
            acc_ref[...] = jnp.zeros_like(acc_ref)

        acc_ref[...] += jnp.dot(a_ref[...], b_ref[...], preferred_element_type=F32)

        @pl.when(kk == nk - 1)
        def _():
            o_ref[...] = acc_ref[...].astype(BF16)

    if col_blocks:
        out_spec = pl.BlockSpec((None, tm, tn), lambda i, j, kk: (j, i, 0))
        out_shape = jax.ShapeDtypeStruct((n // tn, m, tn), BF16)
    else:
        out_spec = pl.BlockSpec((tm, tn), lambda i, j, kk: (i, j))
        out_shape = jax.ShapeDtypeStruct((m, n), BF16)
    return pl.pallas_call(
        body, name=name, grid=(m // tm, n // tn, nk),
        in_specs=[pl.BlockSpec((tm, tk), lambda i, j, kk: (i, kk)), pl.BlockSpec((tk, tn), lambda i, j, kk: (kk, j))],
        out_specs=out_spec, out_shape=out_shape, scratch_shapes=[pltpu.VMEM((tm, tn), F32)],
        compiler_params=_params(("parallel", "parallel", "arbitrary")))(at, b)


def _w_in_grad_part(xt, b, col_of, n_local, tn, w_shape, prev, name):
    d, s = xt.shape
    per_chip = w_shape[2] // tn
    tm = _divisor_tile(d, 1024, 16)
    tk = _divisor_tile(s, 2048, 128)
    nk = s // tk

    def body(*refs):
        a_ref, b_ref, o_ref, acc_ref = refs[0], refs[1], refs[-2], refs[-1]
        kk = pl.program_id(2)

        @pl.when(kk == 0)
        def _():
            acc_ref[...] = jnp.zeros_like(acc_ref)

        acc_ref[...] += jnp.dot(a_ref[...], b_ref[...], preferred_element_type=F32)

        @pl.when(kk == nk - 1)
        def _():
            o_ref[...] = acc_ref[...].astype(BF16)

    if b.ndim == 3:
        sub = b.shape[2] // tn
        b_spec = pl.BlockSpec((None, tk, tn), lambda j, i, kk: (j // sub, kk, j % sub))
    else:
        b_spec = pl.BlockSpec((tk, tn), lambda j, i, kk: (kk, j))
    in_specs = [pl.BlockSpec((tm, tk), lambda j, i, kk: (i, kk)), b_spec]
    args = [xt, b]
    aliases = {}
    if prev is not None:
        in_specs.append(ANY)
        args.append(prev)
        aliases = {2: 0}
    return pl.pallas_call(
        body, name=name, grid=(n_local, d // tm, nk), in_specs=in_specs,
        out_specs=pl.BlockSpec((None, tm, tn), lambda j, i, kk: (col_of(j) // per_chip, i, col_of(j) % per_chip)),
        out_shape=jax.ShapeDtypeStruct(w_shape, BF16), scratch_shapes=[pltpu.VMEM((tm, tn), F32)],
        input_output_aliases=aliases,
        compiler_params=_params(("parallel", "parallel", "arbitrary")))(*args)


def _assemble_w(wcs, after):
    n, d, wc = wcs[0].shape
    tr = _divisor_tile(d, 256, 16)

    def body(after_ref, *refs):
        o_ref = refs[-1]
        for ch in range(W_CHUNKS):
            o_ref[:, ch * wc:(ch + 1) * wc] = refs[ch][...]

    return pl.pallas_call(
        body, name="assemble_w", grid=(n, d // tr),
        in_specs=[pl.BlockSpec(after.shape, lambda b, r: (0, 0))]
        + [pl.BlockSpec((None, tr, wc), lambda b, r: (b, r, 0))] * W_CHUNKS,
        out_specs=pl.BlockSpec((None, tr, W_CHUNKS * wc), lambda b, r: (b, r, 0)),
        out_shape=jax.ShapeDtypeStruct((n, d, W_CHUNKS * wc), wcs[0].dtype),
        compiler_params=_params(("parallel", "parallel")))(after, *wcs)


def _x_grad(dqkv, rest, w4, dr, aw, tn):
    s, d = dr.shape
    sub = aw // tn
    n_qkv = 3 * N_GROUPS * sub
    los, lo = [], n_qkv
    for p in rest:
        los.append(lo)
        lo += p.shape[1] // tn
    n_blocks = lo
    per_chip = n_blocks // N_CHIPS
    tm = _divisor_tile(s, 512, 16 * DILATIONS[-1])

    def body(*refs):
        q_refs, r_refs = refs[:N_GROUPS], refs[N_GROUPS:N_GROUPS + len(rest)]
        w_ref, dr_ref, o_ref, acc_ref, scratch = refs[-5:]
        j = pl.program_id(1)

        @pl.when(j == 0)
        def _():
            acc_ref[...] = ALPHA * dr_ref[...]

        for g, dil in enumerate(DILATIONS):
            @pl.when((j < n_qkv) & (lax.rem(j // sub, N_GROUPS) == g))
            def _(g=g, dil=dil):
                rows = _merge_rows(q_refs[g], scratch, dil).astype(BF16)
                acc_ref[...] += lax.dot_general(rows, w_ref[...], NT, preferred_element_type=F32)

        for p_ref, lo_, piece in zip(r_refs, los, rest):
            @pl.when((j >= lo_) & (j < lo_ + piece.shape[1] // tn))
            def _(p_ref=p_ref):
                acc_ref[...] += lax.dot_general(p_ref[...], w_ref[...], NT, preferred_element_type=F32)

        @pl.when(j == n_blocks - 1)
        def _():
            o_ref[...] = acc_ref[...]

    def qkv_spec(dil):
        def index(i, j):
            region = jnp.minimum(j // sub, 3 * N_GROUPS - 1)
            return region // N_GROUPS, 0, i, jnp.where(j < n_qkv, j % sub, 0)

        return pl.BlockSpec((None, dil, tm // dil, tn), index)

    def rest_spec(lo_, piece):
        n = piece.shape[1] // tn
        return pl.BlockSpec((tm, tn), lambda i, j: (i, jnp.clip(j - lo_, 0, n - 1)))

    row = pl.BlockSpec((tm, d), lambda i, j: (i, 0))
    return pl.pallas_call(
        body, name="x_grad", grid=(s // tm, n_blocks),
        in_specs=[qkv_spec(dil) for dil in DILATIONS] + [rest_spec(lo_, p) for lo_, p in zip(los, rest)]
        + [pl.BlockSpec((None, d, tn), lambda i, j: (j // per_chip, 0, j % per_chip)), row],
        out_specs=row, out_shape=jax.ShapeDtypeStruct((s, d), F32),
        scratch_shapes=[pltpu.VMEM((tm, d), F32), _permute_scratch(tm, tn)],
        compiler_params=_params(("parallel", "arbitrary"), vmem_mib=56))(
            *[t.reshape(3, dil, s // dil, aw) for t, dil in zip(dqkv, DILATIONS)], *rest, w4, dr)


def _prepare_x(x, after=None):
    s, d = x.shape
    tc = 2 * LANES
    slabs = tc // LANES
    ordered = [] if after is None else [after]

    def body(*refs):
        x_ref, xb_ref = refs[len(ordered):len(ordered) + 2]
        xt_refs, scratch = refs[len(ordered) + 2:len(ordered) + 2 + N_GROUPS], refs[-1]
        t = x_ref[...]
        xb_ref[...] = t.astype(BF16)
        for c in range(slabs):
            scratch[c] = t[:, c * LANES:(c + 1) * LANES]
        for g, dil in enumerate(DILATIONS):
            length = s // dil
            for r in range(dil):
                part = t if dil == 1 else jnp.concatenate(
                    [scratch[c, pl.ds(r, length, stride=dil), :] for c in range(slabs)], axis=1)
                xt_refs[g][:, r * length:(r + 1) * length] = part.T.astype(BF16)

    col = pl.BlockSpec((s, tc), lambda j: (0, j))
    row = pl.BlockSpec((tc, s), lambda j: (j, 0))
    t_shape = jax.ShapeDtypeStruct((d, s), BF16)
    out = pl.pallas_call(
        body, name="prepare_x", grid=(d // tc,),
        in_specs=[pl.BlockSpec(t.shape, lambda j: (0, 0)) for t in ordered] + [col],
        out_specs=[col] + [row] * N_GROUPS,
        out_shape=[jax.ShapeDtypeStruct((s, d), BF16)] + [t_shape] * N_GROUPS,
        scratch_shapes=[_permute_scratch(s, tc)], compiler_params=_params(("parallel",)))(*ordered, x)
    return out[0], out[1:]


def _local_step(x, target, w_open, w_close, w_width, b_gate, pool_scale, gamma, beta, aw, pw, small_weights,
                start_exchange=None, first_token=None):
    s, d = x.shape
    tn = _col_tile(aw, pw, w_width)
    sub = aw // tn
    per_chip = w_width // tn
    qkv_w = 3 * N_GROUPS * aw
    w_shape = (N_CHIPS, d, w_width)

    regions = [dict(kind=g, blocks=[(which * N_GROUPS + g) * sub + i for which in range(3) for i in range(sub)])
               for g in range(N_GROUPS)]
    lo = qkv_w // tn
    for name, width in (("zuz", aw + 2 * pw), ("gates", 2 * d)):
        regions.append(dict(kind=name, blocks=list(range(lo, lo + width // tn)), j0=lo, width=width))
        lo += width // tn
    results = [None] * len(regions)
    xb, xts = _prepare_x(x, first_token)
    wcs, last = [], []
    w_open(0, [xb])
    for ch in range(W_CHUNKS):
        wc, token = w_close(ch, last)
        wcs.append(wc)
        calls = []
        for i, region in enumerate(regions):
            blocks = [b for b in region["blocks"] if _chunk_of(b, per_chip) == ch]
            if blocks:
                calls.append((i, region, blocks))
        done = []
        for k, (i, region, blocks) in enumerate(calls):
            after = [token]
            if k == len(calls) - 1 and ch + 1 < W_CHUNKS:
                after.append(w_open(ch + 1, done))
            if region["kind"] in range(N_GROUPS):
                results[i] = _in_proj_qkv(xb, wc, region["kind"], blocks, aw, tn, results[i], after,
                                          f"in_proj_qkv{region['kind']}_{ch}")
            else:
                results[i] = _in_proj(xb, wc, blocks, region["j0"], region["width"], tn, BF16, results[i], after,
                                      f"in_proj_{region['kind']}_{ch}")
            done.append(results[i])
        last = done[-1:]
    qkv = [results[g].reshape(3, s, aw) for g in range(N_GROUPS)]
    zuz, gpre = results[N_GROUPS], results[N_GROUPS + 1]

    attn = [_attn_fwd(qkv[g], g) for g in range(N_GROUPS)]
    o, y_attn, y_attn_t, lse = _combine_groups([a[0] for a in attn], [a[1] for a in attn], zuz, aw)
    w_pool, wpa4, wpp4, w_out = small_weights(o)
    pooled, lin, y_pool, y_pool_t = _pool_fwd(zuz, w_pool, pool_scale, aw, pw)
    a, p, sa, sp, merged, merged_t = _proj_merge(y_attn, y_pool, wpa4, wpp4, gpre, b_gate)
    dr, drb, loss_lanes, d_gamma, d_beta = _out_norm_loss(merged, w_out, x, target, gamma, beta)

    da, dp, d_gpre_a, d_gpre_p, d_b_a, d_b_p = _merge_bwd(drb, w_out, a, p, sa, sp)
    d_b_gate = jnp.concatenate([d_b_a, d_b_p], axis=1)
    d_w_out = _weight_grad(merged_t, drb, d // N_CHIPS, False, "w_out_grad")
    d_wpa4 = _weight_grad(y_attn_t, da, d // N_CHIPS, True, "w_proj_attn_grad")
    d_wpp4 = _weight_grad(y_pool_t, dp, d // N_CHIPS, True, "w_proj_pool_grad")
    d_z_attn, d_o, dd = _attn_gate_bwd(da, wpa4, zuz, o)
    d_z_pool, d_pooled, d_w_pool, d_pool_scale = _pool_gate_bwd(dp, wpp4, zuz, lin, pooled, w_pool, pool_scale, aw)
    d_u = _pool_bwd(d_pooled)
    dqkv = [_attn_bwd(qkv[g], d_o[g], lse[g], dd[g], g) for g in range(N_GROUPS)]

    rest = [d_z_attn, d_u, d_z_pool, d_gpre_a, d_gpre_p]
    d_w_in4 = None
    for g in range(N_GROUPS):
        d_w_in4 = _w_in_grad_part(xts[g], dqkv[g], lambda j, g=g: ((j // sub) * N_GROUPS + g) * sub + j % sub,
                                  3 * sub, tn, w_shape, d_w_in4, f"w_in_grad_qkv{g}")
    lo = qkv_w // tn
    for i, piece in enumerate(rest):
        n_local = piece.shape[1] // tn
        d_w_in4 = _w_in_grad_part(xts[0], piece, lambda j, lo=lo: lo + j, n_local, tn, w_shape, d_w_in4,
                                  f"w_in_grad_rest{i}")
        lo += n_local
    grads = dict(loss_lanes=loss_lanes, w_in=d_w_in4, b_gate=d_b_gate, w_pool=d_w_pool,
                 pool_scale=d_pool_scale, w_proj_attn=d_wpa4, w_proj_pool=d_wpp4, w_out=d_w_out,
                 ln_gamma=d_gamma, ln_beta=d_beta)
    token = jnp.zeros((8, 128), F32) if start_exchange is None else start_exchange(grads)
    grads["d_x"] = _x_grad(dqkv, rest, _assemble_w(wcs, token), dr, aw, tn)
    return grads


def _pack_small(wpa, wpp, w_out, w_pool):
    width = wpa.shape[1]
    return jnp.concatenate([wpa, wpp, w_out.reshape(-1, width), w_pool.reshape(-1, width)], axis=0)


def _unpack_small(packed, aw, pw, d, pg):
    lead = packed.shape[:-2]
    width = d // N_CHIPS
    r0, r1, r2 = aw, aw + pw, aw + pw + d
    return (packed[..., :r0, :], packed[..., r0:r1, :], packed[..., r1:r2, :].reshape(lead + (width, d)),
            packed[..., r2:, :].reshape(lead + (len(POOL_WINDOWS), pg // N_CHIPS, pg)))


def _pack_rows(vectors, rows):
    flat = jnp.concatenate([v.reshape(-1) for v in vectors])
    return jnp.pad(flat, (0, rows * 128 - flat.shape[0])).reshape(rows, 128)


def _unpack_rows(packed, sizes):
    flat, out, lo = packed.reshape(-1), [], 0
    for n in sizes:
        out.append(flat[lo:lo + n].reshape(1, n))
        lo += n
    return out


def kernel(x, w_in, b_gate, w_pool, pool_scale, w_proj_attn, w_proj_pool, w_out, ln_gamma, ln_beta, loss_target, m_w_in, m_b_gate, m_w_pool, m_pool_scale, m_w_proj_attn, m_w_proj_pool, m_w_out, m_ln_gamma, m_ln_beta, v_w_in, v_b_gate, v_w_pool, v_pool_scale, v_w_proj_attn, v_w_proj_pool, v_w_out, v_ln_gamma, v_ln_beta):
    s, d = x.shape[1], x.shape[2]
    aw, pw = w_proj_attn.shape[1], w_proj_pool.shape[1]
    pg = w_pool.shape[3]
    n_win = len(POOL_WINDOWS)

    def small(wpa, wpp, wo, wpl):
        return _pack_small(wpa[0], wpp[0], wo[0], wpl[0])

    chip = 2 * lax.axis_index("x") + lax.axis_index("y")
    core = lax.axis_index("c")

    flight = {"chunk": _halves_start(_place_block(w_in[0], N_CHIPS, chip, BF16, "place_w_in0", 0, W_CHUNKS), x,
                                     "gather_w_in0_start")}
    first_token = flight["chunk"][2]
    w_small = small(w_proj_attn, w_proj_pool, w_out, w_pool) + first_token[0, 0]
    placed = [None] + [_place_block(w_in[0], N_CHIPS, chip, BF16, f"place_w_in{ch}", ch, W_CHUNKS, first_token)
                       for ch in range(1, W_CHUNKS)]
    placed_small = _place_block(w_small, N_CHIPS, chip, BF16, "place_w_small", after=first_token)

    def w_open(ch, after):
        sems, thru, _ = flight["chunk"]
        if ch == 0:
            after = after + placed[1:] + [placed_small]
        landed = _halves_wait(sems, thru, after, f"gather_w_in{ch}_wait")
        if ch + 1 < W_CHUNKS:
            flight["forward"], flight["chunk"] = _advance_start(landed, placed[ch + 1], False, f"advance_w_in{ch}")
        else:
            flight["forward"], flight["small"] = _advance_start(landed, placed_small, True, f"advance_w_in{ch}")
        flight["token"] = flight["forward"][2]
        return flight["token"]

    def w_close(ch, after):
        sems, thru, _ = flight["forward"]
        return _forward_wait(sems, thru, after, f"forward_w_in{ch}_wait"), flight["token"]

    def small_weights(after):
        sems, thru, _ = flight["small"]
        small4 = _broadcast_wait(sems, thru, after, "gather_small_wait")
        wpa4, wpp4, w_out4, w_pool4 = _unpack_small(small4, aw, pw, d, pg)
        return w_pool4.transpose(1, 0, 2, 3).reshape(n_win, pg, pg), wpa4, wpp4, w_out4.reshape(d, d)

    exchange = {}

    def start_exchange(g):
        g_pool4 = g["w_pool"].reshape(n_win, N_CHIPS, pg // N_CHIPS, pg).transpose(1, 0, 2, 3).astype(BF16)
        g_out4 = g["w_out"].reshape(N_CHIPS, d // N_CHIPS, d)
        g_small4 = jnp.concatenate([g["w_proj_attn"], g["w_proj_pool"], g_out4.reshape(N_CHIPS, -1, d // N_CHIPS),
                                    g_pool4.reshape(N_CHIPS, -1, d // N_CHIPS)], axis=1)
        theirs_big, theirs_small = _swap_halves([g["w_in"], g_small4])
        chip_big, placed_big = _add_halves(g["w_in"], theirs_big, core, chip, "add_cores_big")
        chip_small, placed_small = _add_halves(g_small4, theirs_small, core, chip, "add_cores_small")
        sems, sums, placed, token = _scatter_start([chip_big, chip_small], [placed_big, placed_small])
        exchange.update(sems=sems, sums=sums, placed=placed)
        return token

    g = _local_step(x[0], loss_target[0], w_open, w_close, w_in.shape[2], b_gate, pool_scale, ln_gamma, ln_beta,
                    aw, pw, small_weights, start_exchange, first_token)

    sizes = [b_gate.shape[1], pool_scale.shape[1], d, d, 1]
    rows = -(-sum(sizes) // (8 * 128)) * 8
    loss_part = (0.5 / d) * jnp.sum(g["loss_lanes"]).reshape(1, 1)
    parts = _gather_rows(_pack_rows([g["b_gate"], g["pool_scale"], g["ln_gamma"], g["ln_beta"], loss_part], rows))
    zero = jnp.zeros((1, 1), F32)
    packed = [_pack_rows(vs, rows) for vs in ([b_gate, pool_scale, ln_gamma, ln_beta, zero],
                                              [m_b_gate, m_pool_scale, m_ln_gamma, m_ln_beta, zero],
                                              [v_b_gate, v_pool_scale, v_ln_gamma, v_ln_beta, zero])]
    replicated = _sum_rows_adamw(parts, *packed)
    rep = [_unpack_rows(t, sizes) for t in replicated]
    loss = rep[0][4].reshape(())

    got_big, got_small = _scatter_wait(exchange["sems"], exchange["sums"], exchange["placed"],
                                       [g["d_x"], replicated[0]])
    join_sems, halves = _join_start([_sum_slots(got_big, core, "sum_chips_big"),
                                     _sum_slots(got_small, core, "sum_chips_small")])
    mv_small = (small(m_w_proj_attn, m_w_proj_pool, m_w_out, m_w_pool),
                small(v_w_proj_attn, v_w_proj_pool, v_w_out, v_w_pool))
    upd_in = _adamw_half(w_in[0], halves[0], m_w_in[0], v_w_in[0], core, None, "adamw_w_in_own")
    upd_small = _adamw_half(w_small, halves[1], *mv_small, core, None, "adamw_small_own")
    grad_w_in, grad_small = _join_wait(join_sems, halves, [upd_in[0], upd_small[0]])
    upd_in = _adamw_half(w_in[0], grad_w_in, m_w_in[0], v_w_in[0], 1 - core, upd_in, "adamw_w_in_other")
    upd_small = _adamw_half(w_small, grad_small, *mv_small, 1 - core, upd_small, "adamw_small_other")
    grad_w_in, grad_small = upd_in[3], upd_small[3]

    def leaves(big, packed_small, replicated):
        wpa_, wpp_, wo_, wpl_ = _unpack_small(packed_small, aw, pw, d, pg)
        return [big[None], replicated[0], wpl_[None], replicated[1], wpa_[None], wpp_[None], wo_[None],
                replicated[2], replicated[3]]

    out = [loss, g["d_x"][None]]
    out += leaves(grad_w_in, grad_small, rep[0])
    for i in range(3):
        out += leaves(upd_in[i], upd_small[i], rep[1 + i])
    return tuple(out)
```

```python
import math

import jax
import jax.numpy as jnp
from jax import lax
from jax.experimental import pallas as pl
from jax.experimental.pallas import tpu as pltpu

F32 = jnp.float32
BF16 = jnp.bfloat16
MESH = pl.DeviceIdType.MESH
ANY = pl.BlockSpec(memory_space=pl.ANY)

HEAD_DIM = 128
STEPS = 128
DILATIONS = (1, 4, 16)
N_GROUPS = len(DILATIONS)
POOL_WINDOWS = (2, 4, 8, 16)
POOL_HALO = 16
N_CHIPS = 4
N_DEV = 8
ALPHA = 2.0 ** 0.25
LN_EPS = 1e-5
NEG_INF = -1e30
SCORE_SCALE = HEAD_DIM ** -0.5
ADAM_LR = 0.001
ADAM_B1 = 0.9
ADAM_B2 = 0.999
ADAM_EPS = 1e-08
ADAM_WD = 0.01
ADAM_STEP = 10
MIB = 2 ** 20
NT = (((1,), (1,)), ((), ()))
TN = (((0,), (0,)), ((), ()))
DMA_STREAMS = 8


def _params(semantics=None, vmem_mib=48):
    return pltpu.CompilerParams(dimension_semantics=semantics, vmem_limit_bytes=vmem_mib * MIB)


def _divisor_tile(n, target, multiple):
    best = None
    for t in range(multiple, min(n, target) + 1, multiple):
        if n % t == 0:
            best = t
    assert best is not None, (n, target, multiple)
    return best


def _col_tile(*widths):
    g = 0
    for w in widths:
        g = math.gcd(g, w)
    return _divisor_tile(g, 1024, 128)


def _sigmoid(z):
    return jax.nn.sigmoid(z)


def _dsilu(z, sg):
    return sg * (1.0 + z * (1.0 - sg))


def _place():
    x, y, c = lax.axis_index("x"), lax.axis_index("y"), lax.axis_index("c")
    others = [(1 - x, y), (x, 1 - y), (1 - x, 1 - y)]
    return x, y, c, (x, y, 1 - c), others


def _remote(src, dst, send_sem, recv_sem, dev):
    return pltpu.make_async_remote_copy(src_ref=src, dst_ref=dst, send_sem=send_sem, recv_sem=recv_sem,
                                        device_id=dev, device_id_type=MESH)


def _row_pieces(n_rows, streams=DMA_STREAMS, multiple=16):
    size = -(-n_rows // (streams * multiple)) * multiple
    return [(lo, min(size, n_rows - lo)) for lo in range(0, n_rows, size)]


def _start_streams(make, n_rows):
    for lo, size in _row_pieces(n_rows):
        make(pl.ds(lo, size)).start()


def _half_copies(buf, send_sems, recv_sems):
    x, y, c, _, others = _place()
    half = buf.shape[1] // 2
    slab = buf.at[2 * x + y, pl.ds(c * half, half)]
    return [_remote(slab, slab, send_sems[j], recv_sems[j], (ox, oy, c)) for j, (ox, oy) in enumerate(others)]


def _halves_start(placed, after, name):
    k = N_CHIPS - 1

    def body(buf, after_ref, *refs):
        send_sems, recv_sems, token = refs[:k], refs[k:2 * k], refs[-1]
        for cp in _half_copies(buf, send_sems, recv_sems):
            cp.start()
        token[...] = jnp.zeros_like(token)

    out = pl.pallas_call(
        body, name=name,
        out_shape=[pltpu.SemaphoreType.DMA(())] * (2 * k) + [pltpu.HBM(placed.shape, placed.dtype),
                                                             jax.ShapeDtypeStruct((8, 128), F32)],
        in_specs=[HBM, ANY], out_specs=[SEM] * (2 * k) + [HBM, pl.BlockSpec(memory_space=pltpu.VMEM)],
        input_output_aliases={0: 2 * k},
        compiler_params=pltpu.CompilerParams(has_side_effects=DATAFLOW),
    )(pltpu.with_memory_space_constraint(placed, pltpu.HBM), after)
    return out[:2 * k], out[2 * k], out[-1]


def _halves_wait(sems, placed, after, name):
    k = N_CHIPS - 1

    def body(buf, *refs):
        send_sems, recv_sems = refs[:k], refs[k:2 * k]
        for cp in _half_copies(buf, send_sems, recv_sems):
            cp.wait_send()
            cp.wait_recv()

    return pl.pallas_call(
        body, name=name, out_shape=pltpu.HBM(placed.shape, placed.dtype),
        in_specs=[HBM] + [SEM] * (2 * k) + [ANY] * len(after), out_specs=HBM, input_output_aliases={0: 0},
        compiler_params=pltpu.CompilerParams(has_side_effects=DATAFLOW),
    )(placed, *sems, *after)


def _forward_copies(buf, send_sems, recv_sems):
    x, y, c, sibling, others = _place()
    half = buf.shape[1] // 2
    copies = []
    for j, (ox, oy) in enumerate(others):
        slab = buf.at[2 * ox + oy, pl.ds(c * half, half)]
        copies.append(_remote(slab, slab, send_sems[j], recv_sems[j], sibling))
    return copies


def _advance_start(landed, nxt, whole_blocks, name):
    k = N_CHIPS - 1

    def body(a, b, *refs):
        for cp in _forward_copies(a, refs[:k], refs[k:2 * k]):
            cp.start()
        for cp in (_broadcast_copies if whole_blocks else _half_copies)(b, refs[2 * k:3 * k], refs[3 * k:4 * k]):
            cp.start()
        refs[-1][...] = jnp.zeros_like(refs[-1])

    out = pl.pallas_call(
        body, name=name,
        out_shape=[pltpu.SemaphoreType.DMA(())] * (4 * k) + [pltpu.HBM(landed.shape, landed.dtype),
                                                             pltpu.HBM(nxt.shape, nxt.dtype),
                                                             jax.ShapeDtypeStruct((8, 128), F32)],
        in_specs=[HBM, HBM], out_specs=[SEM] * (4 * k) + [HBM, HBM, pl.BlockSpec(memory_space=pltpu.VMEM)],
        input_output_aliases={0: 4 * k, 1: 4 * k + 1},
        compiler_params=pltpu.CompilerParams(has_side_effects=DATAFLOW),
    )(pltpu.with_memory_space_constraint(landed, pltpu.HBM), pltpu.with_memory_space_constraint(nxt, pltpu.HBM))
    token = out[-1]
    return (out[:2 * k], out[4 * k], token), (out[2 * k:4 * k], out[4 * k + 1], token)


def _forward_wait(sems, buf, after, name):
    k = N_CHIPS - 1

    def body(b, *refs):
        send_sems, recv_sems = refs[:k], refs[k:2 * k]
        for cp in _forward_copies(b, send_sems, recv_sems):
            cp.wait_send()
            cp.wait_recv()

    return pl.pallas_call(
        body, name=name, out_shape=pltpu.HBM(buf.shape, buf.dtype),
        in_specs=[HBM] + [SEM] * (2 * k) + [ANY] * len(after), out_specs=HBM, input_output_aliases={0: 0},
        compiler_params=pltpu.CompilerParams(has_side_effects=DATAFLOW),
    )(buf, *sems, *after)


def _swap_halves(grads):
    n = len(grads)

    def body(*refs):
        g, theirs = refs[:n], refs[n:2 * n]
        send_sems, recv_sems = refs[2 * n:]
        x, y, c, sibling, _ = _place()
        for i in range(n):
            half = g[i].shape[1] // 2
            give = (1 - c) * half
            for b in range(N_CHIPS):
                _start_streams(lambda r, i=i, b=b: _remote(
                    g[i].at[b, pl.ds(give + r.start, r.size)], theirs[i].at[b, r], send_sems.at[i], recv_sems.at[i],
                    sibling), half)
        for i in range(n):
            _remote(theirs[i], theirs[i], send_sems.at[i], recv_sems.at[i], sibling).wait()

    return pl.pallas_call(
        body, name="swap_halves",
        out_shape=[jax.ShapeDtypeStruct((s.shape[0], s.shape[1] // 2) + s.shape[2:], s.dtype) for s in grads],
        in_specs=[ANY] * n, out_specs=[ANY] * n,
        scratch_shapes=[pltpu.SemaphoreType.DMA((n,)), pltpu.SemaphoreType.DMA((n,))],
    )(*grads)


HBM = pl.BlockSpec(memory_space=pltpu.HBM)
SEM = pl.BlockSpec(memory_space=pltpu.SEMAPHORE)
DATAFLOW = pltpu.SideEffectType.DATAFLOW_SIDE_EFFECTING


def _broadcast_copies(buf, send_sems, recv_sems):
    x, y, c, _, others = _place()
    mine = buf.at[2 * x + y]
    return [_remote(mine, mine, send_sems[j], recv_sems[j], (ox, oy, c)) for j, (ox, oy) in enumerate(others)]


def _broadcast_wait(sems, placed, after, name):
    k = N_CHIPS - 1

    def body(buf, *refs):
        send_sems, recv_sems = refs[:k], refs[k:2 * k]
        for cp in _broadcast_copies(buf, send_sems, recv_sems):
            cp.wait_send()
            cp.wait_recv()

    return pl.pallas_call(
        body, name=name, out_shape=pltpu.HBM(placed.shape, placed.dtype),
        in_specs=[HBM] + [SEM] * (2 * k) + [ANY], out_specs=HBM, input_output_aliases={0: 0},
        compiler_params=pltpu.CompilerParams(has_side_effects=DATAFLOW),
    )(placed, *sems, after)


def _scatter_copies(s, got, send_sems, recv_sems):
    x, y, c, _, others = _place()
    me = 2 * x + y
    n = len(s)
    return [_remote(s[i].at[2 * ox + oy], got[i].at[me], send_sems[3 * i + j], recv_sems[3 * i + j], (ox, oy, c))
            for i in range(n) for j, (ox, oy) in enumerate(others)]


def _scatter_start(sums, placed):
    n = len(sums)
    k = 3 * n

    def body(*refs):
        s, got, token = refs[:n], refs[n:2 * n], refs[-1]
        send_sems, recv_sems = refs[2 * n:2 * n + k], refs[2 * n + k:2 * n + 2 * k]
        for cp in _scatter_copies(s, got, send_sems, recv_sems):
            cp.start()
        token[...] = jnp.zeros_like(token)

    hbm = [pltpu.HBM(a.shape, a.dtype) for a in list(sums) + list(placed)]
    out = pl.pallas_call(
        body, name="scatter_start",
        out_shape=[pltpu.SemaphoreType.DMA(())] * (2 * k) + hbm + [jax.ShapeDtypeStruct((8, 128), F32)],
        in_specs=[HBM] * (2 * n), out_specs=[SEM] * (2 * k) + [HBM] * (2 * n) + [pl.BlockSpec(memory_space=pltpu.VMEM)],
        input_output_aliases={i: 2 * k + i for i in range(2 * n)},
        compiler_params=pltpu.CompilerParams(has_side_effects=DATAFLOW),
    )(*[pltpu.with_memory_space_constraint(a, pltpu.HBM) for a in list(sums) + list(placed)])
    return out[:2 * k], out[2 * k:2 * k + n], out[2 * k + n:2 * k + 2 * n], out[-1]


def _scatter_wait(sems, sums, placed, after):
    n = len(sums)
    k = 3 * n

    def body(*refs):
        s, got = refs[:n], refs[n:2 * n]
        send_sems, recv_sems = refs[2 * n:2 * n + k], refs[2 * n + k:2 * n + 2 * k]
        for cp in _scatter_copies(s, got, send_sems, recv_sems):
            cp.wait_send()
            cp.wait_recv()

    hbm = [pltpu.HBM(a.shape, a.dtype) for a in list(sums) + list(placed)]
    out = pl.pallas_call(
        body, name="scatter_wait", out_shape=hbm,
        in_specs=[HBM] * (2 * n) + [SEM] * (2 * k) + [ANY] * len(after), out_specs=[HBM] * (2 * n),
        input_output_aliases={i: i for i in range(2 * n)},
        compiler_params=pltpu.CompilerParams(has_side_effects=DATAFLOW),
    )(*sums, *placed, *sems, *after)
    return out[n:]


def _join_copies(bufs, send_sems, recv_sems):
    x, y, c, sibling, _ = _place()
    return [_remote(b.at[c], b.at[c], send_sems[i], recv_sems[i], sibling) for i, b in enumerate(bufs)]


def _join_start(placed):
    n = len(placed)

    def body(*refs):
        bufs, send_sems, recv_sems = refs[:n], refs[n:2 * n], refs[2 * n:3 * n]
        for cp in _join_copies(bufs, send_sems, recv_sems):
            cp.start()

    hbm = [pltpu.HBM(a.shape, a.dtype) for a in placed]
    out = pl.pallas_call(
        body, name="join_start", out_shape=[pltpu.SemaphoreType.DMA(())] * (2 * n) + hbm,
        in_specs=[HBM] * n, out_specs=[SEM] * (2 * n) + [HBM] * n,
        input_output_aliases={i: 2 * n + i for i in range(n)},
        compiler_params=pltpu.CompilerParams(has_side_effects=DATAFLOW),
    )(*[pltpu.with_memory_space_constraint(a, pltpu.HBM) for a in placed])
    return out[:2 * n], out[2 * n:]


def _join_wait(sems, placed, after):
    n = len(placed)

    def body(*refs):
        bufs, send_sems, recv_sems = refs[:n], refs[n:2 * n], refs[2 * n:3 * n]
        for cp in _join_copies(bufs, send_sems, recv_sems):
            cp.wait_send()
            cp.wait_recv()

    return pl.pallas_call(
        body, name="join_wait", out_shape=[pltpu.HBM(a.shape, a.dtype) for a in placed],
        in_specs=[HBM] * n + [SEM] * (2 * n) + [ANY] * len(after), out_specs=[HBM] * n,
        input_output_aliases={i: i for i in range(n)},
        compiler_params=pltpu.CompilerParams(has_side_effects=DATAFLOW),
    )(*placed, *sems, *after)


def _gather_rows(row):
    def body(row_ref, out_ref, send_sems, recv_sems, local_sem):
        x, y, c = lax.axis_index("x"), lax.axis_index("y"), lax.axis_index("c")
        me = 4 * x + 2 * y + c
        local = pltpu.make_async_copy(row_ref, out_ref.at[me], local_sem)
        local.start()
        sent = []
        peers = []
        for k in range(1, N_DEV):
            px, py, pc = x ^ (k >> 2), y ^ ((k >> 1) & 1), c ^ (k & 1)
            peers.append((k, px, py, pc))
            cp = _remote(row_ref, out_ref.at[me], send_sems.at[k - 1], recv_sems.at[k - 1], (px, py, pc))
            cp.start()
            sent.append(cp)
        for k, px, py, pc in peers:
            slot = out_ref.at[4 * px + 2 * py + pc]
            _remote(slot, slot, send_sems.at[k - 1], recv_sems.at[k - 1], (px, py, pc)).wait_recv()
        for cp in sent:
            cp.wait_send()
        local.wait()

    return pl.pallas_call(
        body, name="gather_rows", out_shape=jax.ShapeDtypeStruct((N_DEV,) + row.shape, row.dtype),
        in_specs=[ANY], out_specs=ANY,
        scratch_shapes=[pltpu.SemaphoreType.DMA((N_DEV - 1,)), pltpu.SemaphoreType.DMA((N_DEV - 1,)),
                        pltpu.SemaphoreType.DMA],
    )(row)


def _scalar(i):
    return jnp.reshape(i, (1,)).astype(jnp.int32)


def _place_block(src, n_slots, slot, out_dtype, name, window=0, n_windows=1, after=None):
    rows, cols = src.shape[0], src.shape[1] // n_windows
    tr = _divisor_tile(rows, max(16, (2 * MIB) // (cols * 4)), 16)
    ordered = [] if after is None else [after]

    def body(slot_ref, *refs):
        s_ref, o_ref = refs[len(ordered):]
        o_ref[...] = s_ref[...].astype(o_ref.dtype)

    return pl.pallas_call(
        body, name=name, out_shape=jax.ShapeDtypeStruct((n_slots, rows, cols), out_dtype),
        grid_spec=pltpu.PrefetchScalarGridSpec(
            num_scalar_prefetch=1, grid=(rows // tr,),
            in_specs=[pl.BlockSpec(t.shape, lambda r, sl: (0, 0)) for t in ordered]
            + [pl.BlockSpec((tr, cols), lambda r, sl: (r, window))],
            out_specs=pl.BlockSpec((None, tr, cols), lambda r, sl: (sl[0], r, 0))),
        compiler_params=_params(("parallel",)))(_scalar(slot), *ordered, src)


def _add_halves(g, theirs, core, chip, name):
    n, half, cols = theirs.shape
    tr = _divisor_tile(half, max(16, (4 * MIB) // (cols * 4)), 16)
    per = half // tr

    def body(at_ref, a_ref, b_ref, o_ref, own_ref):
        total = (a_ref[...].astype(F32) + b_ref[...].astype(F32)).astype(o_ref.dtype)
        o_ref[...] = total

        @pl.when(pl.program_id(1) == at_ref[1])
        def _():
            own_ref[...] = total

    spec = pl.BlockSpec((None, tr, cols), lambda r, i, at: (i, r, 0))
    shape = jax.ShapeDtypeStruct(theirs.shape, BF16)
    return pl.pallas_call(
        body, name=name, out_shape=[shape, shape],
        grid_spec=pltpu.PrefetchScalarGridSpec(
            num_scalar_prefetch=1, grid=(per, n),
            in_specs=[pl.BlockSpec((None, tr, cols), lambda r, i, at: (i, at[0] * per + r, 0)), spec],
            out_specs=[spec, pl.BlockSpec((None, tr, cols), lambda r, i, at: (at[1], r, 0))]),
        compiler_params=_params(("parallel", "arbitrary")))(jnp.concatenate([_scalar(core), _scalar(chip)]), g, theirs)


def _sum_slots(a, core, name):
    n, rows, cols = a.shape
    tr = _divisor_tile(rows, max(16, (8 * MIB) // (cols * 4 * n)), 16)

    def body(c_ref, a_ref, o_ref):
        acc = a_ref[0].astype(F32)
        for i in range(1, n):
            acc = acc + a_ref[i].astype(F32)
        o_ref[...] = acc

    return pl.pallas_call(
        body, name=name, out_shape=jax.ShapeDtypeStruct((2, rows, cols), F32),
        grid_spec=pltpu.PrefetchScalarGridSpec(
            num_scalar_prefetch=1, grid=(rows // tr,),
            in_specs=[pl.BlockSpec((n, tr, cols), lambda r, c: (0, r, 0))],
            out_specs=pl.BlockSpec((None, tr, cols), lambda r, c: (c[0], r, 0))),
        compiler_params=_params(("parallel",)))(_scalar(core), a)


def _adamw_math(w, g, m, v):
    m = ADAM_B1 * m + (1.0 - ADAM_B1) * g
    v = ADAM_B2 * v + (1.0 - ADAM_B2) * (g * g)
    m_hat = m / (1.0 - ADAM_B1 ** ADAM_STEP)
    v_hat = v / (1.0 - ADAM_B2 ** ADAM_STEP)
    delta = -ADAM_LR * (m_hat / (jnp.sqrt(v_hat) + ADAM_EPS) + ADAM_WD * w)
    return delta, m, v


def _adamw_half(w, g2, m, v, which, prev, name):
    rows, cols = w.shape
    half = rows // 2
    tr = _divisor_tile(half, max(8, (2 * MIB) // (cols * 4)), 8)
    per = half // tr
    n_out = 4

    def body(h_ref, w_ref, g_ref, m_ref, v_ref, *refs):
        d_ref, nm_ref, nv_ref, go_ref = refs[-n_out:]
        g = g_ref[...]
        d, nm, nv = _adamw_math(w_ref[...], g, m_ref[...], v_ref[...])
        d_ref[...] = d
        nm_ref[...] = nm
        nv_ref[...] = nv
        go_ref[...] = g

    spec = pl.BlockSpec((tr, cols), lambda r, h: (h[0] * per + r, 0))
    in_specs = [spec, pl.BlockSpec((None, tr, cols), lambda r, h: (h[0], r, 0)), spec, spec]
    args = [_scalar(which), w, g2, m, v]
    aliases = {}
    if prev is not None:
        aliases = {len(args) + i: i for i in range(n_out)}
        in_specs += [ANY] * n_out
        args += list(prev)
    return pl.pallas_call(
        body, name=name, out_shape=[jax.ShapeDtypeStruct((rows, cols), F32)] * n_out,
        grid_spec=pltpu.PrefetchScalarGridSpec(num_scalar_prefetch=1, grid=(per,), in_specs=in_specs,
                                               out_specs=[spec] * n_out),
        input_output_aliases=aliases, compiler_params=_params(("parallel",)))(*args)


def _sum_rows_adamw(parts, w, m, v):
    def body(p_ref, w_ref, m_ref, v_ref, g_ref, d_ref, nm_ref, nv_ref):
        g = p_ref[0]
        for i in range(1, N_DEV):
            g = g + p_ref[i]
        d, nm, nv = _adamw_math(w_ref[...], g, m_ref[...], v_ref[...])
        g_ref[...] = g
        d_ref[...] = d
        nm_ref[...] = nm
        nv_ref[...] = nv

    shape = jax.ShapeDtypeStruct(w.shape, F32)
    return pl.pallas_call(body, name="sum_rows_adamw", out_shape=[shape] * 4)(parts, w, m, v)


LANES = 128


def _permute_scratch(rows, width):
    return pltpu.VMEM((width // LANES, rows, LANES), F32)


def _split_rows(value, scratch, dil):
    if dil == 1:
        return [value]
    rows = value.shape[0] // dil
    slabs = value.shape[1] // LANES
    for c in range(slabs):
        scratch[c] = value[:, c * LANES:(c + 1) * LANES]
    return [jnp.concatenate([scratch[c, pl.ds(r, rows, stride=dil), :] for c in range(slabs)], axis=1)
            for r in range(dil)]


def _merge_rows(ref, scratch, dil):
    if dil == 1:
        return ref[0].astype(F32)
    rows = ref.shape[1]
    slabs = ref.shape[2] // LANES
    for r in range(dil):
        part = ref[r].astype(F32)
        for c in range(slabs):
            scratch[c, pl.ds(r, rows, stride=dil), :] = part[:, c * LANES:(c + 1) * LANES]
    return jnp.concatenate([scratch[c] for c in range(slabs)], axis=1)


def _grouped_view(t, dil):
    return t.reshape(dil, t.shape[0] // dil, t.shape[1])


def _grouped_spec(dil, rows, width, index):
    return pl.BlockSpec((dil, rows // dil, width), index)


W_CHUNKS = 4


def _pick(values, j):
    out = values[-1]
    for i in range(len(values) - 2, -1, -1):
        out = jnp.where(j == i, values[i], out)
    return out


def _chunk_of(col, per_chip):
    return (col % per_chip) // (per_chip // W_CHUNKS)


def _w_block(col, per_chip):
    return col // per_chip, 0, (col % per_chip) % (per_chip // W_CHUNKS)


def _in_proj(xb, wc, blocks, j0, ncols, tn, out_dtype, prev, after, name):
    s, d = xb.shape
    per_chip = wc.shape[2] * W_CHUNKS // tn
    tm = _divisor_tile(s, 1024, 16)
    extra = [t for t in after if t is not None]

    def body(*refs):
        a_ref, b_ref = refs[len(extra):len(extra) + 2]
        o_ref = refs[-1]
        o_ref[...] = jnp.dot(a_ref[...], b_ref[...], preferred_element_type=F32).astype(o_ref.dtype)

    in_specs = [pl.BlockSpec(t.shape, lambda j, m: (0, 0)) for t in extra] + [
        pl.BlockSpec((tm, d), lambda j, m: (m, 0)),
        pl.BlockSpec((None, d, tn), lambda j, m: _w_block(_pick(blocks, j), per_chip))]
    args = extra + [xb, wc]
    aliases = {}
    if prev is not None:
        aliases = {len(args): 0}
        in_specs.append(ANY)
        args.append(prev)
    return pl.pallas_call(
        body, name=name, grid=(len(blocks), s // tm), in_specs=in_specs,
        out_specs=pl.BlockSpec((tm, tn), lambda j, m: (m, _pick(blocks, j) - j0)),
        out_shape=jax.ShapeDtypeStruct((s, ncols), out_dtype), input_output_aliases=aliases,
        compiler_params=_params(("parallel", "parallel")))(*args)


def _in_proj_qkv(xb, wc, g, blocks, aw, tn, prev, after, name):
    s, d = xb.shape
    dil = DILATIONS[g]
    per_chip = wc.shape[2] * W_CHUNKS // tn
    sub = aw // tn
    tm = _divisor_tile(s, 1024, 16 * dil)
    extra = [t for t in after if t is not None]

    def body(*refs):
        a_ref, b_ref = refs[len(extra):len(extra) + 2]
        o_ref, scratch = refs[-2:]
        res = jnp.dot(a_ref[...], b_ref[...], preferred_element_type=F32)
        for r, part in enumerate(_split_rows(res, scratch, dil)):
            o_ref[r] = part.astype(BF16)

    def out_index(j, m):
        col = _pick(blocks, j)
        return (col // sub) // N_GROUPS, 0, m, col % sub

    in_specs = [pl.BlockSpec(t.shape, lambda j, m: (0, 0)) for t in extra] + [
        pl.BlockSpec((tm, d), lambda j, m: (m, 0)),
        pl.BlockSpec((None, d, tn), lambda j, m: _w_block(_pick(blocks, j), per_chip))]
    args = extra + [xb, wc]
    aliases = {}
    if prev is not None:
        aliases = {len(args): 0}
        in_specs.append(ANY)
        args.append(prev)
    return pl.pallas_call(
        body, name=name, grid=(len(blocks), s // tm), in_specs=in_specs,
        out_specs=pl.BlockSpec((None, dil, tm // dil, tn), out_index),
        out_shape=jax.ShapeDtypeStruct((3, dil, s // dil, aw), BF16), input_output_aliases=aliases,
        scratch_shapes=[_permute_scratch(tm, tn)],
        compiler_params=_params(("parallel", "parallel")))(*args)


def _window_mask(first):
    qi = lax.broadcasted_iota(jnp.int32, (STEPS, 2 * STEPS), 0)
    kj = lax.broadcasted_iota(jnp.int32, (STEPS, 2 * STEPS), 1)
    lowest = jnp.where(first, STEPS, 0)
    return (kj >= qi) & (kj <= qi + STEPS) & (kj >= lowest)


def _attn_fwd(qkv, g):
    _, s, aw = qkv.shape
    heads = aw // HEAD_DIM
    n_blocks = s // STEPS
    per_seq = n_blocks // DILATIONS[g]
    pair = 4 if n_blocks % 4 == 0 else 1

    def body(q_ref, kc_ref, kp_ref, vc_ref, vp_ref, o_ref, l_ref):
        masks = [_window_mask(lax.rem(pl.program_id(0) * pair + j, per_seq) == 0) for j in range(pair)]
        for h in range(heads):
            hs = slice(h * HEAD_DIM, (h + 1) * HEAD_DIM)
            keys = jnp.concatenate([kp_ref[:, hs], kc_ref[:, hs]], axis=0)
            values = jnp.concatenate([vp_ref[:, hs], vc_ref[:, hs]], axis=0)
            for j in range(pair):
                rows = slice(j * STEPS, (j + 1) * STEPS)
                window = slice(j * STEPS, (j + 2) * STEPS)
                sc = lax.dot_general(q_ref[rows, hs], keys[window], NT, preferred_element_type=F32) * SCORE_SCALE
                sc = jnp.where(masks[j], sc, NEG_INF)
                mx = jnp.max(sc, axis=1, keepdims=True)
                e = jnp.exp(sc - mx)
                den = jnp.sum(e, axis=1, keepdims=True)
                o_ref[rows, hs] = (jnp.dot(e.astype(BF16), values[window], preferred_element_type=F32)
                                   / den).astype(BF16)
                l_ref[rows, hs] = jnp.broadcast_to(mx + jnp.log(den), (STEPS, HEAD_DIM))

    def cur(which):
        return pl.BlockSpec((None, pair * STEPS, aw), lambda b: (which, b, 0))

    def prev(which):
        return pl.BlockSpec((None, STEPS, aw), lambda b: (which, jnp.maximum(pair * b - 1, 0), 0))

    out = pl.BlockSpec((pair * STEPS, aw), lambda b: (b, 0))
    return pl.pallas_call(
        body, name=f"attn_fwd{g}", grid=(n_blocks // pair,),
        in_specs=[cur(0), cur(1), prev(1), cur(2), prev(2)], out_specs=[out, out],
        out_shape=[jax.ShapeDtypeStruct((s, aw), BF16), jax.ShapeDtypeStruct((s, aw), F32)],
        compiler_params=_params(("parallel",)))(qkv, qkv, qkv, qkv, qkv)


def _combine_groups(os, ls, zuz, aw):
    s = zuz.shape[0]
    tr = _divisor_tile(s, 256, 8 * DILATIONS[-1])

    def body(*refs):
        o_refs, l_refs, z_ref = refs[0:3], refs[3:6], refs[6]
        oo_ref, y_ref, yt_ref = refs[7:10]
        lq_refs, scratch = refs[10:13], refs[13]
        ls_ = [_merge_rows(l_refs[g], scratch, dil) for g, dil in enumerate(DILATIONS)]
        mx = jnp.maximum(jnp.maximum(ls_[0], ls_[1]), ls_[2])
        ws = [jnp.exp(l - mx) for l in ls_]
        den = ws[0] + ws[1] + ws[2]
        o = ws[0] * _merge_rows(o_refs[0], scratch, DILATIONS[0])
        for g in range(1, N_GROUPS):
            o = o + ws[g] * _merge_rows(o_refs[g], scratch, DILATIONS[g])
        o = o / den
        z = z_ref[...].astype(F32)
        y = o * (z * _sigmoid(z))
        oo_ref[...] = o.astype(BF16)
        y_ref[...] = y.astype(BF16)
        yt_ref[...] = y.T.astype(BF16)
        for g, dil in enumerate(DILATIONS):
            for r, part in enumerate(_split_rows(mx + jnp.log(den), scratch, dil)):
                lq_refs[g][r] = part

    grouped = [_grouped_spec(dil, tr, aw, lambda r: (0, r, 0)) for dil in DILATIONS]
    one = pl.BlockSpec((tr, aw), lambda r: (r, 0))
    b16 = jax.ShapeDtypeStruct((s, aw), BF16)
    out = pl.pallas_call(
        body, name="combine_groups", grid=(s // tr,),
        in_specs=grouped + grouped + [one],
        out_specs=[one, one, pl.BlockSpec((aw, tr), lambda r: (0, r))] + grouped,
        out_shape=[b16, b16, jax.ShapeDtypeStruct((aw, s), BF16)]
        + [jax.ShapeDtypeStruct((dil, s // dil, aw), F32) for dil in DILATIONS],
        scratch_shapes=[_permute_scratch(tr, aw)],
        compiler_params=_params(("parallel",)))(
            *[_grouped_view(t, dil) for t, dil in zip(os, DILATIONS)],
            *[_grouped_view(t, dil) for t, dil in zip(ls, DILATIONS)], zuz)
    return out[0], out[1], out[2], [t.reshape(s, aw) for t in out[3:]]


def _pool_counts(row0, rows, window):
    t = row0 + lax.broadcasted_iota(jnp.int32, (rows, 1), 0)
    return jnp.minimum(t + 1, window).astype(F32)


def _pool_fwd(zuz, w_pool, pool_scale, aw, pw):
    s = zuz.shape[0]
    pg = pw // len(POOL_WINDOWS)
    tr = _divisor_tile(s, 256, 128)
    u_col, z_col = aw // pw, aw // pw + 1
    assert aw % pw == 0

    def body(u_ref, up_ref, z_ref, w_ref, sc_ref, p_ref, l_ref, y_ref, yt_ref):
        r = pl.program_id(0)
        u = u_ref[...].astype(F32)
        halo = jnp.where(r > 0, up_ref[...].astype(F32), 0.0)
        ext = jnp.concatenate([halo, u], axis=0)
        pieces, lins = [], []
        for gi, window in enumerate(POOL_WINDOWS):
            cs = slice(gi * pg, (gi + 1) * pg)
            acc = ext[:, cs]
            shift = 1
            while shift < window:
                acc = acc + pltpu.roll(acc, shift, 0)
                shift *= 2
            p = acc[POOL_HALO:] / _pool_counts(r * tr, tr, window) - u[:, cs]
            pieces.append(p)
            lins.append(jnp.dot(p.astype(BF16), w_ref[gi], preferred_element_type=F32))
        p = jnp.concatenate(pieces, axis=1)
        lin = jnp.concatenate(lins, axis=1)
        z = z_ref[...].astype(F32)
        y = lin * sc_ref[...] * (z * _sigmoid(z))
        p_ref[...] = p.astype(BF16)
        l_ref[...] = lin
        y_ref[...] = y.astype(BF16)
        yt_ref[...] = y.T.astype(BF16)

    per = tr // POOL_HALO
    out = pl.BlockSpec((tr, pw), lambda r: (r, 0))
    return pl.pallas_call(
        body, name="pool_fwd", grid=(s // tr,),
        in_specs=[pl.BlockSpec((tr, pw), lambda r: (r, u_col)),
                  pl.BlockSpec((POOL_HALO, pw), lambda r: (jnp.maximum(r * per - 1, 0), u_col)),
                  pl.BlockSpec((tr, pw), lambda r: (r, z_col)),
                  pl.BlockSpec((len(POOL_WINDOWS), pg, pg), lambda r: (0, 0, 0)),
                  pl.BlockSpec((1, pw), lambda r: (0, 0))],
        out_specs=[out, out, out, pl.BlockSpec((pw, tr), lambda r: (0, r))],
        out_shape=[jax.ShapeDtypeStruct((s, pw), BF16), jax.ShapeDtypeStruct((s, pw), F32),
                   jax.ShapeDtypeStruct((s, pw), BF16), jax.ShapeDtypeStruct((pw, s), BF16)],
        compiler_params=_params(("parallel",)))(zuz, zuz, zuz, w_pool, pool_scale)


def _proj_merge(y_attn, y_pool, wpa4, wpp4, gpre, b_gate):
    s, aw = y_attn.shape
    pw = y_pool.shape[1]
    tn = wpa4.shape[2]
    d = N_CHIPS * tn
    tm = _divisor_tile(s, 512, 128)

    def body(ya_ref, yp_ref, wa_ref, wp_ref, ga_ref, gp_ref, ba_ref, bp_ref, a_ref, p_ref, sa_ref, sp_ref, m_ref,
             mt_ref):
        a = jnp.dot(ya_ref[...], wa_ref[...], preferred_element_type=F32)
        p = jnp.dot(yp_ref[...], wp_ref[...], preferred_element_type=F32)
        sa = _sigmoid(ga_ref[...].astype(F32) + ba_ref[...])
        sp = _sigmoid(gp_ref[...].astype(F32) + bp_ref[...])
        merged = sa * a + sp * p
        a_ref[...] = a.astype(BF16)
        p_ref[...] = p.astype(BF16)
        sa_ref[...] = sa.astype(BF16)
        sp_ref[...] = sp.astype(BF16)
        m_ref[...] = merged.astype(BF16)
        mt_ref[...] = merged.T.astype(BF16)

    out = pl.BlockSpec((tm, tn), lambda n, m: (m, n))
    f = jax.ShapeDtypeStruct((s, d), BF16)
    return pl.pallas_call(
        body, name="proj_merge", grid=(N_CHIPS, s // tm),
        in_specs=[pl.BlockSpec((tm, aw), lambda n, m: (m, 0)), pl.BlockSpec((tm, pw), lambda n, m: (m, 0)),
                  pl.BlockSpec((None, aw, tn), lambda n, m: (n, 0, 0)),
                  pl.BlockSpec((None, pw, tn), lambda n, m: (n, 0, 0)),
                  pl.BlockSpec((tm, tn), lambda n, m: (m, n)), pl.BlockSpec((tm, tn), lambda n, m: (m, N_CHIPS + n)),
                  pl.BlockSpec((1, tn), lambda n, m: (0, n)), pl.BlockSpec((1, tn), lambda n, m: (0, N_CHIPS + n))],
        out_specs=[out] * 5 + [pl.BlockSpec((tn, tm), lambda n, m: (n, m))],
        out_shape=[f] * 5 + [jax.ShapeDtypeStruct((d, s), BF16)],
        compiler_params=_params(("parallel", "parallel")))(y_attn, y_pool, wpa4, wpp4, gpre, gpre, b_gate, b_gate)


def _out_norm_loss(merged, w_out, x, target, gamma, beta):
    s, d = x.shape
    tm = _divisor_tile(s, 256, 16)

    def body(m_ref, w_ref, x_ref, t_ref, g_ref, b_ref, dr_ref, drb_ref, loss_ref, dg_ref, db_ref):
        @pl.when(pl.program_id(0) == 0)
        def _():
            loss_ref[...] = jnp.zeros_like(loss_ref)
            dg_ref[...] = jnp.zeros_like(dg_ref)
            db_ref[...] = jnp.zeros_like(db_ref)

        r = ALPHA * x_ref[...] + jnp.dot(m_ref[...], w_ref[...], preferred_element_type=F32)
        mu = jnp.mean(r, axis=1, keepdims=True)
        rc = r - mu
        rstd = lax.rsqrt(jnp.mean(rc * rc, axis=1, keepdims=True) + LN_EPS)
        xhat = rc * rstd
        diff = xhat * g_ref[...] + b_ref[...] - t_ref[...]
        dy = diff / d
        loss_ref[...] += jnp.sum(diff * diff, axis=0, keepdims=True)
        dg_ref[...] += jnp.sum(dy * xhat, axis=0, keepdims=True)
        db_ref[...] += jnp.sum(dy, axis=0, keepdims=True)
        dxhat = dy * g_ref[...]
        dr = rstd * (dxhat - jnp.mean(dxhat, axis=1, keepdims=True)
                     - xhat * jnp.mean(dxhat * xhat, axis=1, keepdims=True))
        dr_ref[...] = dr
        drb_ref[...] = dr.astype(BF16)

    row = pl.BlockSpec((tm, d), lambda m: (m, 0))
    vec = pl.BlockSpec((1, d), lambda m: (0, 0))
    v = jax.ShapeDtypeStruct((1, d), F32)
    return pl.pallas_call(
        body, name="out_norm_loss", grid=(s // tm,),
        in_specs=[row, pl.BlockSpec((d, d), lambda m: (0, 0)), row, row, vec, vec],
        out_specs=[row, row, vec, vec, vec],
        out_shape=[jax.ShapeDtypeStruct((s, d), F32), jax.ShapeDtypeStruct((s, d), BF16), v, v, v],
        compiler_params=_params(("arbitrary",), vmem_mib=56))(merged, w_out, x, target, gamma, beta)


def _merge_bwd(drb, w_out, a, p, sa, sp):
    s, d = drb.shape
    tm = _divisor_tile(s, 512, 16)
    tn = d // N_CHIPS

    def body(dr_ref, w_ref, a_ref, p_ref, sa_ref, sp_ref, da_ref, dp_ref, dga_ref, dgp_ref, dba_ref, dbp_ref):
        @pl.when(pl.program_id(1) == 0)
        def _():
            dba_ref[...] = jnp.zeros_like(dba_ref)
            dbp_ref[...] = jnp.zeros_like(dbp_ref)

        dm = lax.dot_general(dr_ref[...], w_ref[...], NT, preferred_element_type=F32)
        sa = sa_ref[...].astype(F32)
        sp = sp_ref[...].astype(F32)
        da_ref[...] = (dm * sa).astype(BF16)
        dp_ref[...] = (dm * sp).astype(BF16)
        dga = dm * a_ref[...].astype(F32) * sa * (1.0 - sa)
        dgp = dm * p_ref[...].astype(F32) * sp * (1.0 - sp)
        dga_ref[...] = dga.astype(BF16)
        dgp_ref[...] = dgp.astype(BF16)
        dba_ref[...] += jnp.sum(dga, axis=0, keepdims=True)
        dbp_ref[...] += jnp.sum(dgp, axis=0, keepdims=True)

    blk = pl.BlockSpec((tm, tn), lambda n, m: (m, n))
    vec = pl.BlockSpec((1, tn), lambda n, m: (0, n))
    b16 = jax.ShapeDtypeStruct((s, d), BF16)
    v = jax.ShapeDtypeStruct((1, d), F32)
    return pl.pallas_call(
        body, name="merge_bwd", grid=(N_CHIPS, s // tm),
        in_specs=[pl.BlockSpec((tm, d), lambda n, m: (m, 0)), pl.BlockSpec((tn, d), lambda n, m: (n, 0)),
                  blk, blk, blk, blk],
        out_specs=[blk, blk, blk, blk, vec, vec], out_shape=[b16, b16, b16, b16, v, v],
        compiler_params=_params(("parallel", "arbitrary")))(drb, w_out, a, p, sa, sp)


def _proj_t(dy_ref, w_ref, tn):
    acc = None
    for n in range(N_CHIPS):
        t = lax.dot_general(dy_ref[:, n * tn:(n + 1) * tn], w_ref[n], NT, preferred_element_type=F32)
        acc = t if acc is None else acc + t
    return acc


def _attn_gate_bwd(da, wpa4, zuz, o):
    s, d = da.shape
    aw, tn = wpa4.shape[1], wpa4.shape[2]
    heads = aw // HEAD_DIM
    tm = _divisor_tile(s, 256, 16 * DILATIONS[-1])

    def body(*refs):
        da_ref, w_ref, z_ref, o_ref, dz_ref = refs[:5]
        do_refs, dd_refs, scratch = refs[5:8], refs[8:11], refs[11]
        dy = _proj_t(da_ref, w_ref, tn)
        z, o = z_ref[...].astype(F32), o_ref[...].astype(F32)
        sg = _sigmoid(z)
        do = dy * (z * sg)
        dz_ref[...] = (dy * o * _dsilu(z, sg)).astype(BF16)
        prod = do * o
        dd = jnp.concatenate(
            [jnp.broadcast_to(jnp.sum(prod[:, h * HEAD_DIM:(h + 1) * HEAD_DIM], axis=1, keepdims=True),
                              (tm, HEAD_DIM)) for h in range(heads)], axis=1)
        for g, dil in enumerate(DILATIONS):
            for r, part in enumerate(_split_rows(do, scratch, dil)):
                do_refs[g][r] = part.astype(BF16)
            for r, part in enumerate(_split_rows(dd, scratch, dil)):
                dd_refs[g][r] = part

    row = pl.BlockSpec((tm, aw), lambda m: (m, 0))
    grouped = [_grouped_spec(dil, tm, aw, lambda m: (0, m, 0)) for dil in DILATIONS]
    out = pl.pallas_call(
        body, name="attn_gate_bwd", grid=(s // tm,),
        in_specs=[pl.BlockSpec((tm, d), lambda m: (m, 0)), pl.BlockSpec((N_CHIPS, aw, tn), lambda m: (0, 0, 0)),
                  row, row],
        out_specs=[row] + grouped + grouped,
        out_shape=[jax.ShapeDtypeStruct((s, aw), BF16)]
        + [jax.ShapeDtypeStruct((dil, s // dil, aw), BF16) for dil in DILATIONS]
        + [jax.ShapeDtypeStruct((dil, s // dil, aw), F32) for dil in DILATIONS],
        scratch_shapes=[_permute_scratch(tm, aw)],
        compiler_params=_params(("parallel",)))(da, wpa4, zuz, o)
    return out[0], [t.reshape(s, aw) for t in out[1:4]], [t.reshape(s, aw) for t in out[4:7]]


def _pool_gate_bwd(dp_in, wpp4, zuz, lin, pooled, w_pool, pool_scale, aw):
    s, d = dp_in.shape
    pw, tn = wpp4.shape[1], wpp4.shape[2]
    n_win = len(POOL_WINDOWS)
    pg = pw // n_win
    tm = _divisor_tile(s, 256, 16)
    z_col = aw // pw + 1

    def body(dp_ref, w_ref, z_ref, l_ref, p_ref, wp_ref, sc_ref, dz_ref, dpo_ref, dw_ref, ds_ref):
        @pl.when(pl.program_id(0) == 0)
        def _():
            dw_ref[...] = jnp.zeros_like(dw_ref)
            ds_ref[...] = jnp.zeros_like(ds_ref)

        dy = _proj_t(dp_ref, w_ref, tn)
        z, lin_ = z_ref[...].astype(F32), l_ref[...]
        sg = _sigmoid(z)
        dypp = dy * (z * sg)
        dz_ref[...] = (dy * (lin_ * sc_ref[...]) * _dsilu(z, sg)).astype(BF16)
        ds_ref[...] += jnp.sum(dypp * lin_, axis=0, keepdims=True)
        dlin = (dypp * sc_ref[...]).astype(BF16)
        for gi in range(n_win):
            cs = slice(gi * pg, (gi + 1) * pg)
            dw_ref[gi] += lax.dot_general(p_ref[:, cs], dlin[:, cs], TN, preferred_element_type=F32)
            dpo_ref[:, cs] = lax.dot_general(dlin[:, cs], wp_ref[gi], NT, preferred_element_type=F32)

    row = pl.BlockSpec((tm, pw), lambda m: (m, 0))
    return pl.pallas_call(
        body, name="pool_gate_bwd", grid=(s // tm,),
        in_specs=[pl.BlockSpec((tm, d), lambda m: (m, 0)), pl.BlockSpec((N_CHIPS, pw, tn), lambda m: (0, 0, 0)),
                  pl.BlockSpec((tm, pw), lambda m: (m, z_col)), row, row,
                  pl.BlockSpec((n_win, pg, pg), lambda m: (0, 0, 0)), pl.BlockSpec((1, pw), lambda m: (0, 0))],
        out_specs=[row, row, pl.BlockSpec((n_win, pg, pg), lambda m: (0, 0, 0)),
                   pl.BlockSpec((1, pw), lambda m: (0, 0))],
        out_shape=[jax.ShapeDtypeStruct((s, pw), BF16), jax.ShapeDtypeStruct((s, pw), F32),
                   jax.ShapeDtypeStruct((n_win, pg, pg), F32), jax.ShapeDtypeStruct((1, pw), F32)],
        compiler_params=_params(("arbitrary",)))(dp_in, wpp4, zuz, lin, pooled, w_pool, pool_scale)


def _pool_bwd(dpooled):
    s, pw = dpooled.shape
    pg = pw // len(POOL_WINDOWS)
    tr = _divisor_tile(s, 256, POOL_HALO)
    per = tr // POOL_HALO
    n_tiles = s // tr

    def body(c_ref, n_ref, du_ref):
        r = pl.program_id(0)
        cur = c_ref[...]
        halo = jnp.where(r < n_tiles - 1, n_ref[...], 0.0)
        ext = jnp.concatenate([cur, halo], axis=0)
        rows = tr + POOL_HALO
        for gi, window in enumerate(POOL_WINDOWS):
            cs = slice(gi * pg, (gi + 1) * pg)
            acc = ext[:, cs] / _pool_counts(r * tr, rows, window)
            shift = 1
            while shift < window:
                acc = acc + pltpu.roll(acc, rows - shift, 0)
                shift *= 2
            du_ref[:, cs] = (acc[:tr] - cur[:, cs]).astype(BF16)

    return pl.pallas_call(
        body, name="pool_bwd", grid=(n_tiles,),
        in_specs=[pl.BlockSpec((tr, pw), lambda r: (r, 0)),
                  pl.BlockSpec((POOL_HALO, pw), lambda r: (jnp.minimum((r + 1) * per, s // POOL_HALO - 1), 0))],
        out_specs=pl.BlockSpec((tr, pw), lambda r: (r, 0)),
        out_shape=jax.ShapeDtypeStruct((s, pw), BF16), compiler_params=_params(("parallel",)))(dpooled, dpooled)


def _attn_bwd(qkv, do, lse, dd, g):
    _, s, aw = qkv.shape
    heads = aw // HEAD_DIM
    n_blocks = s // STEPS
    per_seq = n_blocks // DILATIONS[g]
    pair = 4 if n_blocks % 4 == 0 else 1
    rows_ = pair * STEPS
    n_steps = n_blocks // pair
    tail = slice(rows_ - STEPS, rows_)

    def body(q_ref, do_ref, l_ref, dd_ref, kc_ref, kp_ref, vc_ref, vp_ref, out_ref, cq_ref, ck_ref, cv_ref):
        b = pl.program_id(0)

        @pl.when(b == 0)
        def _():
            cq_ref[...] = jnp.zeros_like(cq_ref)
            ck_ref[...] = jnp.zeros_like(ck_ref)
            cv_ref[...] = jnp.zeros_like(cv_ref)

        out_ref[0] = cq_ref[...].astype(BF16)

        @pl.when(b < n_steps)
        def _():
            masks = [_window_mask(lax.rem(b * pair + j, per_seq) == 0) for j in range(pair)]
            for h in range(heads):
                hs = slice(h * HEAD_DIM, (h + 1) * HEAD_DIM)
                keys = jnp.concatenate([kp_ref[:, hs], kc_ref[:, hs]], axis=0)
                values = jnp.concatenate([vp_ref[:, hs], vc_ref[:, hs]], axis=0)
                dks, dvs = [], []
                for j in range(pair):
                    rows = slice(j * STEPS, (j + 1) * STEPS)
                    window = slice(j * STEPS, (j + 2) * STEPS)
                    q, do_, kk, vv = q_ref[rows, hs], do_ref[rows, hs], keys[window], values[window]
                    lse_ = jnp.concatenate([l_ref[rows, hs], l_ref[rows, hs]], axis=1)
                    dd_ = jnp.concatenate([dd_ref[rows, hs], dd_ref[rows, hs]], axis=1)
                    sc = lax.dot_general(q, kk, NT, preferred_element_type=F32) * SCORE_SCALE
                    prob = jnp.where(masks[j], jnp.exp(sc - lse_), 0.0)
                    dprob = lax.dot_general(do_, vv, NT, preferred_element_type=F32)
                    dsc = prob * (dprob - dd_) * SCORE_SCALE
                    cq_ref[rows, hs] = jnp.dot(dsc.astype(BF16), kk, preferred_element_type=F32)
                    dks.append(lax.dot_general(dsc.astype(BF16), q, TN, preferred_element_type=F32))
                    dvs.append(lax.dot_general(prob.astype(BF16), do_, TN, preferred_element_type=F32))
                for which, carry, parts in ((1, ck_ref, dks), (2, cv_ref, dvs)):
                    out_ref[which, tail, hs] = (carry[tail, hs] + parts[0][:STEPS]).astype(BF16)
                    if pair > 1:
                        out_ref[which, :rows_ - STEPS, hs] = carry[:rows_ - STEPS, hs].astype(BF16)
                    for j in range(pair):
                        total = parts[j][STEPS:]
                        if j + 1 < pair:
                            total = total + parts[j + 1][:STEPS]
                        carry[j * STEPS:(j + 1) * STEPS, hs] = total

        @pl.when(b == n_steps)
        def _():
            out_ref[1] = ck_ref[...].astype(BF16)
            out_ref[2] = cv_ref[...].astype(BF16)

    last = n_steps - 1

    def cur(which):
        return pl.BlockSpec((None, rows_, aw), lambda b: (which, jnp.minimum(b, last), 0))

    def prev(which):
        return pl.BlockSpec((None, STEPS, aw), lambda b: (which, jnp.clip(b * pair - 1, 0, n_blocks - 1), 0))

    row = pl.BlockSpec((rows_, aw), lambda b: (jnp.minimum(b, last), 0))
    return pl.pallas_call(
        body, name=f"attn_bwd{g}", grid=(n_steps + 1,),
        in_specs=[cur(0), row, row, row, cur(1), prev(1), cur(2), prev(2)],
        out_specs=pl.BlockSpec((3, rows_, aw), lambda b: (0, jnp.clip(b - 1, 0, last), 0)),
        out_shape=jax.ShapeDtypeStruct((3, s, aw), BF16),
        scratch_shapes=[pltpu.VMEM((rows_, aw), F32)] * 3,
        compiler_params=_params(("arbitrary",)))(qkv, do, lse, dd, qkv, qkv, qkv, qkv)


def _weight_grad(at, b, tn, col_blocks, name):
    m, k = at.shape
    n = b.shape[1]
    tm = _divisor_tile(m, 1024, 16)
    tk = _divisor_tile(k, 2048, 128)
    nk = k // tk

    def body(a_ref, b_ref, o_ref, acc_ref):
        kk = pl.program_id(2)

        @pl.when(kk == 0)
        def _():
            acc_ref[...] = jnp.zeros_like(acc_ref)

        acc_ref[...] += jnp.dot(a_ref[...], b_ref[...], preferred_element_type=F32)

        @pl.when(kk == nk - 1)
        def _():
            o_ref[...] = acc_ref[...].astype(BF16)

    if col_blocks:
        out_spec = pl.BlockSpec((None, tm, tn), lambda i, j, kk: (j, i, 0))
        out_shape = jax.ShapeDtypeStruct((n // tn, m, tn), BF16)
    else:
        out_spec = pl.BlockSpec((tm, tn), lambda i, j, kk: (i, j))
        out_shape = jax.ShapeDtypeStruct((m, n), BF16)
    return pl.pallas_call(
        body, name=name, grid=(m // tm, n // tn, nk),
        in_specs=[pl.BlockSpec((tm, tk), lambda i, j, kk: (i, kk)), pl.BlockSpec((tk, tn), lambda i, j, kk: (kk, j))],
        out_specs=out_spec, out_shape=out_shape, scratch_shapes=[pltpu.VMEM((tm, tn), F32)],
        compiler_params=_params(("parallel", "parallel", "arbitrary")))(at, b)


def _w_in_grad_part(xt, b, col_of, n_local, tn, w_shape, prev, name):
    d, s = xt.shape
    per_chip = w_shape[2] // tn
    tm = _divisor_tile(d, 1024, 16)
    tk = _divisor_tile(s, 2048, 128)
    nk = s // tk

    def body(*refs):
        a_ref, b_ref, o_ref, acc_ref = refs[0], refs[1], refs[-2], refs[-1]
        kk = pl.program_id(2)

        @pl.when(kk == 0)
        def _():
            acc_ref[...] = jnp.zeros_like(acc_ref)

        acc_ref[...] += jnp.dot(a_ref[...], b_ref[...], preferred_element_type=F32)

        @pl.when(kk == nk - 1)
        def _():
            o_ref[...] = acc_ref[...].astype(BF16)

    if b.ndim == 3:
        sub = b.shape[2] // tn
        b_spec = pl.BlockSpec((None, tk, tn), lambda j, i, kk: (j // sub, kk, j % sub))
    else:
        b_spec = pl.BlockSpec((tk, tn), lambda j, i, kk: (kk, j))
    in_specs = [pl.BlockSpec((tm, tk), lambda j, i, kk: (i, kk)), b_spec]
    args = [xt, b]
    aliases = {}
    if prev is not None:
        in_specs.append(ANY)
        args.append(prev)
        aliases = {2: 0}
    return pl.pallas_call(
        body, name=name, grid=(n_local, d // tm, nk), in_specs=in_specs,
        out_specs=pl.BlockSpec((None, tm, tn), lambda j, i, kk: (col_of(j) // per_chip, i, col_of(j) % per_chip)),
        out_shape=jax.ShapeDtypeStruct(w_shape, BF16), scratch_shapes=[pltpu.VMEM((tm, tn), F32)],
        input_output_aliases=aliases,
        compiler_params=_params(("parallel", "parallel", "arbitrary")))(*args)


def _assemble_w(wcs, after):
    n, d, wc = wcs[0].shape
    tr = _divisor_tile(d, 256, 16)

    def body(after_ref, *refs):
        o_ref = refs[-1]
        for ch in range(W_CHUNKS):
            o_ref[:, ch * wc:(ch + 1) * wc] = refs[ch][...]

    return pl.pallas_call(
        body, name="assemble_w", grid=(n, d // tr),
        in_specs=[pl.BlockSpec(after.shape, lambda b, r: (0, 0))]
        + [pl.BlockSpec((None, tr, wc), lambda b, r: (b, r, 0))] * W_CHUNKS,
        out_specs=pl.BlockSpec((None, tr, W_CHUNKS * wc), lambda b, r: (b, r, 0)),
        out_shape=jax.ShapeDtypeStruct((n, d, W_CHUNKS * wc), wcs[0].dtype),
        compiler_params=_params(("parallel", "parallel")))(after, *wcs)


def _x_grad(dqkv, rest, w4, dr, aw, tn):
    s, d = dr.shape
    sub = aw // tn
    n_qkv = 3 * N_GROUPS * sub
    los, lo = [], n_qkv
    for p in rest:
        los.append(lo)
        lo += p.shape[1] // tn
    n_blocks = lo
    per_chip = n_blocks // N_CHIPS
    tm = _divisor_tile(s, 512, 16 * DILATIONS[-1])

    def body(*refs):
        q_refs, r_refs = refs[:N_GROUPS], refs[N_GROUPS:N_GROUPS + len(rest)]
        w_ref, dr_ref, o_ref, acc_ref, scratch = refs[-5:]
        j = pl.program_id(1)

        @pl.when(j == 0)
        def _():
            acc_ref[...] = ALPHA * dr_ref[...]

        for g, dil in enumerate(DILATIONS):
            @pl.when((j < n_qkv) & (lax.rem(j // sub, N_GROUPS) == g))
            def _(g=g, dil=dil):
                rows = _merge_rows(q_refs[g], scratch, dil).astype(BF16)
                acc_ref[...] += lax.dot_general(rows, w_ref[...], NT, preferred_element_type=F32)

        for p_ref, lo_, piece in zip(r_refs, los, rest):
            @pl.when((j >= lo_) & (j < lo_ + piece.shape[1] // tn))
            def _(p_ref=p_ref):
                acc_ref[...] += lax.dot_general(p_ref[...], w_ref[...], NT, preferred_element_type=F32)

        @pl.when(j == n_blocks - 1)
        def _():
            o_ref[...] = acc_ref[...]

    def qkv_spec(dil):
        def index(i, j):
            region = jnp.minimum(j // sub, 3 * N_GROUPS - 1)
            return region // N_GROUPS, 0, i, jnp.where(j < n_qkv, j % sub, 0)

        return pl.BlockSpec((None, dil, tm // dil, tn), index)

    def rest_spec(lo_, piece):
        n = piece.shape[1] // tn
        return pl.BlockSpec((tm, tn), lambda i, j: (i, jnp.clip(j - lo_, 0, n - 1)))

    row = pl.BlockSpec((tm, d), lambda i, j: (i, 0))
    return pl.pallas_call(
        body, name="x_grad", grid=(s // tm, n_blocks),
        in_specs=[qkv_spec(dil) for dil in DILATIONS] + [rest_spec(lo_, p) for lo_, p in zip(los, rest)]
        + [pl.BlockSpec((None, d, tn), lambda i, j: (j // per_chip, 0, j % per_chip)), row],
        out_specs=row, out_shape=jax.ShapeDtypeStruct((s, d), F32),
        scratch_shapes=[pltpu.VMEM((tm, d), F32), _permute_scratch(tm, tn)],
        compiler_params=_params(("parallel", "arbitrary"), vmem_mib=56))(
            *[t.reshape(3, dil, s // dil, aw) for t, dil in zip(dqkv, DILATIONS)], *rest, w4, dr)


def _prepare_x(x, after=None):
    s, d = x.shape
    tc = 2 * LANES
    slabs = tc // LANES
    ordered = [] if after is None else [after]

    def body(*refs):
        x_ref, xb_ref = refs[len(ordered):len(ordered) + 2]
        xt_refs, scratch = refs[len(ordered) + 2:len(ordered) + 2 + N_GROUPS], refs[-1]
        t = x_ref[...]
        xb_ref[...] = t.astype(BF16)
        for c in range(slabs):
            scratch[c] = t[:, c * LANES:(c + 1) * LANES]
        for g, dil in enumerate(DILATIONS):
            length = s // dil
            for r in range(dil):
                part = t if dil == 1 else jnp.concatenate(
                    [scratch[c, pl.ds(r, length, stride=dil), :] for c in range(slabs)], axis=1)
                xt_refs[g][:, r * length:(r + 1) * length] = part.T.astype(BF16)

    col = pl.BlockSpec((s, tc), lambda j: (0, j))
    row = pl.BlockSpec((tc, s), lambda j: (j, 0))
    t_shape = jax.ShapeDtypeStruct((d, s), BF16)
    out = pl.pallas_call(
        body, name="prepare_x", grid=(d // tc,),
        in_specs=[pl.BlockSpec(t.shape, lambda j: (0, 0)) for t in ordered] + [col],
        out_specs=[col] + [row] * N_GROUPS,
        out_shape=[jax.ShapeDtypeStruct((s, d), BF16)] + [t_shape] * N_GROUPS,
        scratch_shapes=[_permute_scratch(s, tc)], compiler_params=_params(("parallel",)))(*ordered, x)
    return out[0], out[1:]


def _local_step(x, target, w_open, w_close, w_width, b_gate, pool_scale, gamma, beta, aw, pw, small_weights,
                start_exchange=None, first_token=None):
    s, d = x.shape
    tn = _col_tile(aw, pw, w_width)
    sub = aw // tn
    per_chip = w_width // tn
    qkv_w = 3 * N_GROUPS * aw
    w_shape = (N_CHIPS, d, w_width)

    regions = [dict(kind=g, blocks=[(which * N_GROUPS + g) * sub + i for which in range(3) for i in range(sub)])
               for g in range(N_GROUPS)]
    lo = qkv_w // tn
    for name, width in (("zuz", aw + 2 * pw), ("gates", 2 * d)):
        regions.append(dict(kind=name, blocks=list(range(lo, lo + width // tn)), j0=lo, width=width))
        lo += width // tn
    results = [None] * len(regions)
    xb, xts = _prepare_x(x, first_token)
    wcs, last = [], []
    w_open(0, [xb])
    for ch in range(W_CHUNKS):
        wc, token = w_close(ch, last)
        wcs.append(wc)
        calls = []
        for i, region in enumerate(regions):
            blocks = [b for b in region["blocks"] if _chunk_of(b, per_chip) == ch]
            if blocks:
                calls.append((i, region, blocks))
        done = []
        for k, (i, region, blocks) in enumerate(calls):
            after = [token]
            if k == len(calls) - 1 and ch + 1 < W_CHUNKS:
                after.append(w_open(ch + 1, done))
            if region["kind"] in range(N_GROUPS):
                results[i] = _in_proj_qkv(xb, wc, region["kind"], blocks, aw, tn, results[i], after,
                                          f"in_proj_qkv{region['kind']}_{ch}")
            else:
                results[i] = _in_proj(xb, wc, blocks, region["j0"], region["width"], tn, BF16, results[i], after,
                                      f"in_proj_{region['kind']}_{ch}")
            done.append(results[i])
        last = done[-1:]
    qkv = [results[g].reshape(3, s, aw) for g in range(N_GROUPS)]
    zuz, gpre = results[N_GROUPS], results[N_GROUPS + 1]

    attn = [_attn_fwd(qkv[g], g) for g in range(N_GROUPS)]
    o, y_attn, y_attn_t, lse = _combine_groups([a[0] for a in attn], [a[1] for a in attn], zuz, aw)
    w_pool, wpa4, wpp4, w_out = small_weights(o)
    pooled, lin, y_pool, y_pool_t = _pool_fwd(zuz, w_pool, pool_scale, aw, pw)
    a, p, sa, sp, merged, merged_t = _proj_merge(y_attn, y_pool, wpa4, wpp4, gpre, b_gate)
    dr, drb, loss_lanes, d_gamma, d_beta = _out_norm_loss(merged, w_out, x, target, gamma, beta)

    da, dp, d_gpre_a, d_gpre_p, d_b_a, d_b_p = _merge_bwd(drb, w_out, a, p, sa, sp)
    d_b_gate = jnp.concatenate([d_b_a, d_b_p], axis=1)
    d_w_out = _weight_grad(merged_t, drb, d // N_CHIPS, False, "w_out_grad")
    d_wpa4 = _weight_grad(y_attn_t, da, d // N_CHIPS, True, "w_proj_attn_grad")
    d_wpp4 = _weight_grad(y_pool_t, dp, d // N_CHIPS, True, "w_proj_pool_grad")
    d_z_attn, d_o, dd = _attn_gate_bwd(da, wpa4, zuz, o)
    d_z_pool, d_pooled, d_w_pool, d_pool_scale = _pool_gate_bwd(dp, wpp4, zuz, lin, pooled, w_pool, pool_scale, aw)
    d_u = _pool_bwd(d_pooled)
    dqkv = [_attn_bwd(qkv[g], d_o[g], lse[g], dd[g], g) for g in range(N_GROUPS)]

    rest = [d_z_attn, d_u, d_z_pool, d_gpre_a, d_gpre_p]
    d_w_in4 = None
    for g in range(N_GROUPS):
        d_w_in4 = _w_in_grad_part(xts[g], dqkv[g], lambda j, g=g: ((j // sub) * N_GROUPS + g) * sub + j % sub,
                                  3 * sub, tn, w_shape, d_w_in4, f"w_in_grad_qkv{g}")
    lo = qkv_w // tn
    for i, piece in enumerate(rest):
        n_local = piece.shape[1] // tn
        d_w_in4 = _w_in_grad_part(xts[0], piece, lambda j, lo=lo: lo + j, n_local, tn, w_shape, d_w_in4,
                                  f"w_in_grad_rest{i}")
        lo += n_local
    grads = dict(loss_lanes=loss_lanes, w_in=d_w_in4, b_gate=d_b_gate, w_pool=d_w_pool,
                 pool_scale=d_pool_scale, w_proj_attn=d_wpa4, w_proj_pool=d_wpp4, w_out=d_w_out,
                 ln_gamma=d_gamma, ln_beta=d_beta)
    token = jnp.zeros((8, 128), F32) if start_exchange is None else start_exchange(grads)
    grads["d_x"] = _x_grad(dqkv, rest, _assemble_w(wcs, token), dr, aw, tn)
    return grads


def _pack_small(wpa, wpp, w_out, w_pool):
    width = wpa.shape[1]
    return jnp.concatenate([wpa, wpp, w_out.reshape(-1, width), w_pool.reshape(-1, width)], axis=0)


def _unpack_small(packed, aw, pw, d, pg):
    lead = packed.shape[:-2]
    width = d // N_CHIPS
    r0, r1, r2 = aw, aw + pw, aw + pw + d
    return (packed[..., :r0, :], packed[..., r0:r1, :], packed[..., r1:r2, :].reshape(lead + (width, d)),
            packed[..., r2:, :].reshape(lead + (len(POOL_WINDOWS), pg // N_CHIPS, pg)))


def _pack_rows(vectors, rows):
    flat = jnp.concatenate([v.reshape(-1) for v in vectors])
    return jnp.pad(flat, (0, rows * 128 - flat.shape[0])).reshape(rows, 128)


def _unpack_rows(packed, sizes):
    flat, out, lo = packed.reshape(-1), [], 0
    for n in sizes:
        out.append(flat[lo:lo + n].reshape(1, n))
        lo += n
    return out


def kernel(x, w_in, b_gate, w_pool, pool_scale, w_proj_attn, w_proj_pool, w_out, ln_gamma, ln_beta, loss_target, m_w_in, m_b_gate, m_w_pool, m_pool_scale, m_w_proj_attn, m_w_proj_pool, m_w_out, m_ln_gamma, m_ln_beta, v_w_in, v_b_gate, v_w_pool, v_pool_scale, v_w_proj_attn, v_w_proj_pool, v_w_out, v_ln_gamma, v_ln_beta):
    s, d = x.shape[1], x.shape[2]
    aw, pw = w_proj_attn.shape[1], w_proj_pool.shape[1]
    pg = w_pool.shape[3]
    n_win = len(POOL_WINDOWS)

    def small(wpa, wpp, wo, wpl):
        return _pack_small(wpa[0], wpp[0], wo[0], wpl[0])

    chip = 2 * lax.axis_index("x") + lax.axis_index("y")
    core = lax.axis_index("c")

    flight = {"chunk": _halves_start(_place_block(w_in[0], N_CHIPS, chip, BF16, "place_w_in0", 0, W_CHUNKS), x,
                                     "gather_w_in0_start")}
    first_token = flight["chunk"][2]
    w_small = small(w_proj_attn, w_proj_pool, w_out, w_pool) + first_token[0, 0]
    placed = [None] + [_place_block(w_in[0], N_CHIPS, chip, BF16, f"place_w_in{ch}", ch, W_CHUNKS, first_token)
                       for ch in range(1, W_CHUNKS)]
    placed_small = _place_block(w_small, N_CHIPS, chip, BF16, "place_w_small", after=first_token)

    def w_open(ch, after):
        sems, thru, _ = flight["chunk"]
        if ch == 0:
            after = after + placed[1:] + [placed_small]
        landed = _halves_wait(sems, thru, after, f"gather_w_in{ch}_wait")
        if ch + 1 < W_CHUNKS:
            flight["forward"], flight["chunk"] = _advance_start(landed, placed[ch + 1], False, f"advance_w_in{ch}")
        else:
            flight["forward"], flight["small"] = _advance_start(landed, placed_small, True, f"advance_w_in{ch}")
        flight["token"] = flight["forward"][2]
        return flight["token"]

    def w_close(ch, after):
        sems, thru, _ = flight["forward"]
        return _forward_wait(sems, thru, after, f"forward_w_in{ch}_wait"), flight["token"]

    def small_weights(after):
        sems, thru, _ = flight["small"]
        small4 = _broadcast_wait(sems, thru, after, "gather_small_wait")
        wpa4, wpp4, w_out4, w_pool4 = _unpack_small(small4, aw, pw, d, pg)
        return w_pool4.transpose(1, 0, 2, 3).reshape(n_win, pg, pg), wpa4, wpp4, w_out4.reshape(d, d)

    exchange = {}

    def start_exchange(g):
        g_pool4 = g["w_pool"].reshape(n_win, N_CHIPS, pg // N_CHIPS, pg).transpose(1, 0, 2, 3).astype(BF16)
        g_out4 = g["w_out"].reshape(N_CHIPS, d // N_CHIPS, d)
        g_small4 = jnp.concatenate([g["w_proj_attn"], g["w_proj_pool"], g_out4.reshape(N_CHIPS, -1, d // N_CHIPS),
                                    g_pool4.reshape(N_CHIPS, -1, d // N_CHIPS)], axis=1)
        theirs_big, theirs_small = _swap_halves([g["w_in"], g_small4])
        chip_big, placed_big = _add_halves(g["w_in"], theirs_big, core, chip, "add_cores_big")
        chip_small, placed_small = _add_halves(g_small4, theirs_small, core, chip, "add_cores_small")
        sems, sums, placed, token = _scatter_start([chip_big, chip_small], [placed_big, placed_small])
        exchange.update(sems=sems, sums=sums, placed=placed)
        return token

    g = _local_step(x[0], loss_target[0], w_open, w_close, w_in.shape[2], b_gate, pool_scale, ln_gamma, ln_beta,
                    aw, pw, small_weights, start_exchange, first_token)

    sizes = [b_gate.shape[1], pool_scale.shape[1], d, d, 1]
    rows = -(-sum(sizes) // (8 * 128)) * 8
    loss_part = (0.5 / d) * jnp.sum(g["loss_lanes"]).reshape(1, 1)
    parts = _gather_rows(_pack_rows([g["b_gate"], g["pool_scale"], g["ln_gamma"], g["ln_beta"], loss_part], rows))
    zero = jnp.zeros((1, 1), F32)
    packed = [_pack_rows(vs, rows) for vs in ([b_gate, pool_scale, ln_gamma, ln_beta, zero],
                                              [m_b_gate, m_pool_scale, m_ln_gamma, m_ln_beta, zero],
                                              [v_b_gate, v_pool_scale, v_ln_gamma, v_ln_beta, zero])]
    replicated = _sum_rows_adamw(parts, *packed)
    rep = [_unpack_rows(t, sizes) for t in replicated]
    loss = rep[0][4].reshape(())

    got_big, got_small = _scatter_wait(exchange["sems"], exchange["sums"], exchange["placed"],
                                       [g["d_x"], replicated[0]])
    join_sems, halves = _join_start([_sum_slots(got_big, core, "sum_chips_big"),
                                     _sum_slots(got_small, core, "sum_chips_small")])
    mv_small = (small(m_w_proj_attn, m_w_proj_pool, m_w_out, m_w_pool),
                small(v_w_proj_attn, v_w_proj_pool, v_w_out, v_w_pool))
    upd_in = _adamw_half(w_in[0], halves[0], m_w_in[0], v_w_in[0], core, None, "adamw_w_in_own")
    upd_small = _adamw_half(w_small, halves[1], *mv_small, core, None, "adamw_small_own")
    grad_w_in, grad_small = _join_wait(join_sems, halves, [upd_in[0], upd_small[0]])
    upd_in = _adamw_half(w_in[0], grad_w_in, m_w_in[0], v_w_in[0], 1 - core, upd_in, "adamw_w_in_other")
    upd_small = _adamw_half(w_small, grad_small, *mv_small, 1 - core, upd_small, "adamw_small_other")
    grad_w_in, grad_small = upd_in[3], upd_small[3]

    def leaves(big, packed_small, replicated):
        wpa_, wpp_, wo_, wpl_ = _unpack_small(packed_small, aw, pw, d, pg)
        return [big[None], replicated[0], wpl_[None], replicated[1], wpa_[None], wpp_[None], wo_[None],
                replicated[2], replicated[3]]

    out = [loss, g["d_x"][None]]
    out += leaves(grad_w_in, grad_small, rep[0])
    for i in range(3):
        out += leaves(upd_in[i], upd_small[i], rep[1 + i])
    return tuple(out)
```

```python
import math

import jax
import jax.numpy as jnp
from jax import lax
from jax.experimental import pallas as pl
from jax.experimental.pallas import tpu as pltpu

F32 = jnp.float32
BF16 = jnp.bfloat16
MESH = pl.DeviceIdType.MESH
ANY = pl.BlockSpec(memory_space=pl.ANY)

HEAD_DIM = 128
STEPS = 128
DILATIONS = (1, 4, 16)
N_GROUPS = len(DILATIONS)
POOL_WINDOWS = (2, 4, 8, 16)
POOL_HALO = 16
N_CHIPS = 4
N_DEV = 8
ALPHA = 2.0 ** 0.25
LN_EPS = 1e-5
NEG_INF = -1e30
SCORE_SCALE = HEAD_DIM ** -0.5
ADAM_LR = 0.001
ADAM_B1 = 0.9
ADAM_B2 = 0.999
ADAM_EPS = 1e-08
ADAM_WD = 0.01
ADAM_STEP = 10
MIB = 2 ** 20
NT = (((1,), (1,)), ((), ()))
TN = (((0,), (0,)), ((), ()))
DMA_STREAMS = 8


def _params(semantics=None, vmem_mib=48):
    return pltpu.CompilerParams(dimension_semantics=semantics, vmem_limit_bytes=vmem_mib * MIB)


def _divisor_tile(n, target, multiple):
    best = None
    for t in range(multiple, min(n, target) + 1, multiple):
        if n % t == 0:
            best = t
    assert best is not None, (n, target, multiple)
    return best


def _col_tile(*widths):
    g = 0
    for w in widths:
        g = math.gcd(g, w)
    return _divisor_tile(g, 1024, 128)


def _sigmoid(z):
    return jax.nn.sigmoid(z)


def _dsilu(z, sg):
    return sg * (1.0 + z * (1.0 - sg))


def _place():
    x, y, c = lax.axis_index("x"), lax.axis_index("y"), lax.axis_index("c")
    others = [(1 - x, y), (x, 1 - y), (1 - x, 1 - y)]
    return x, y, c, (x, y, 1 - c), others


def _remote(src, dst, send_sem, recv_sem, dev):
    return pltpu.make_async_remote_copy(src_ref=src, dst_ref=dst, send_sem=send_sem, recv_sem=recv_sem,
                                        device_id=dev, device_id_type=MESH)


def _row_pieces(n_rows, streams=DMA_STREAMS, multiple=16):
    size = -(-n_rows // (streams * multiple)) * multiple
    return [(lo, min(size, n_rows - lo)) for lo in range(0, n_rows, size)]


def _start_streams(make, n_rows):
    for lo, size in _row_pieces(n_rows):
        make(pl.ds(lo, size)).start()


def _half_copies(buf, send_sems, recv_sems):
    x, y, c, _, others = _place()
    half = buf.shape[1] // 2
    slab = buf.at[2 * x + y, pl.ds(c * half, half)]
    return [_remote(slab, slab, send_sems[j], recv_sems[j], (ox, oy, c)) for j, (ox, oy) in enumerate(others)]


def _halves_start(placed, after, name):
    k = N_CHIPS - 1

    def body(buf, after_ref, *refs):
        send_sems, recv_sems, token = refs[:k], refs[k:2 * k], refs[-1]
        for cp in _half_copies(buf, send_sems, recv_sems):
            cp.start()
        token[...] = jnp.zeros_like(token)

    out = pl.pallas_call(
        body, name=name,
        out_shape=[pltpu.SemaphoreType.DMA(())] * (2 * k) + [pltpu.HBM(placed.shape, placed.dtype),
                                                             jax.ShapeDtypeStruct((8, 128), F32)],
        in_specs=[HBM, ANY], out_specs=[SEM] * (2 * k) + [HBM, pl.BlockSpec(memory_space=pltpu.VMEM)],
        input_output_aliases={0: 2 * k},
        compiler_params=pltpu.CompilerParams(has_side_effects=DATAFLOW),
    )(pltpu.with_memory_space_constraint(placed, pltpu.HBM), after)
    return out[:2 * k], out[2 * k], out[-1]


def _halves_wait(sems, placed, after, name):
    k = N_CHIPS - 1

    def body(buf, *refs):
        send_sems, recv_sems = refs[:k], refs[k:2 * k]
        for cp in _half_copies(buf, send_sems, recv_sems):
            cp.wait_send()
            cp.wait_recv()

    return pl.pallas_call(
        body, name=name, out_shape=pltpu.HBM(placed.shape, placed.dtype),
        in_specs=[HBM] + [SEM] * (2 * k) + [ANY] * len(after), out_specs=HBM, input_output_aliases={0: 0},
        compiler_params=pltpu.CompilerParams(has_side_effects=DATAFLOW),
    )(placed, *sems, *after)


def _forward_copies(buf, send_sems, recv_sems):
    x, y, c, sibling, others = _place()
    half = buf.shape[1] // 2
    copies = []
    for j, (ox, oy) in enumerate(others):
        slab = buf.at[2 * ox + oy, pl.ds(c * half, half)]
        copies.append(_remote(slab, slab, send_sems[j], recv_sems[j], sibling))
    return copies


def _advance_start(landed, nxt, whole_blocks, name):
    k = N_CHIPS - 1

    def body(a, b, *refs):
        for cp in _forward_copies(a, refs[:k], refs[k:2 * k]):
            cp.start()
        for cp in (_broadcast_copies if whole_blocks else _half_copies)(b, refs[2 * k:3 * k], refs[3 * k:4 * k]):
            cp.start()
        refs[-1][...] = jnp.zeros_like(refs[-1])

    out = pl.pallas_call(
        body, name=name,
        out_shape=[pltpu.SemaphoreType.DMA(())] * (4 * k) + [pltpu.HBM(landed.shape, landed.dtype),
                                                             pltpu.HBM(nxt.shape, nxt.dtype),
                                                             jax.ShapeDtypeStruct((8, 128), F32)],
        in_specs=[HBM, HBM], out_specs=[SEM] * (4 * k) + [HBM, HBM, pl.BlockSpec(memory_space=pltpu.VMEM)],
        input_output_aliases={0: 4 * k, 1: 4 * k + 1},
        compiler_params=pltpu.CompilerParams(has_side_effects=DATAFLOW),
    )(pltpu.with_memory_space_constraint(landed, pltpu.HBM), pltpu.with_memory_space_constraint(nxt, pltpu.HBM))
    token = out[-1]
    return (out[:2 * k], out[4 * k], token), (out[2 * k:4 * k], out[4 * k + 1], token)


def _forward_wait(sems, buf, after, name):
    k = N_CHIPS - 1

    def body(b, *refs):
        send_sems, recv_sems = refs[:k], refs[k:2 * k]
        for cp in _forward_copies(b, send_sems, recv_sems):
            cp.wait_send()
            cp.wait_recv()

    return pl.pallas_call(
        body, name=name, out_shape=pltpu.HBM(buf.shape, buf.dtype),
        in_specs=[HBM] + [SEM] * (2 * k) + [ANY] * len(after), out_specs=HBM, input_output_aliases={0: 0},
        compiler_params=pltpu.CompilerParams(has_side_effects=DATAFLOW),
    )(buf, *sems, *after)


def _swap_halves(grads):
    n = len(grads)

    def body(*refs):
        g, theirs = refs[:n], refs[n:2 * n]
        send_sems, recv_sems = refs[2 * n:]
        x, y, c, sibling, _ = _place()
        for i in range(n):
            half = g[i].shape[1] // 2
            give = (1 - c) * half
            for b in range(N_CHIPS):
                _start_streams(lambda r, i=i, b=b: _remote(
                    g[i].at[b, pl.ds(give + r.start, r.size)], theirs[i].at[b, r], send_sems.at[i], recv_sems.at[i],
                    sibling), half)
        for i in range(n):
            _remote(theirs[i], theirs[i], send_sems.at[i], recv_sems.at[i], sibling).wait()

    return pl.pallas_call(
        body, name="swap_halves",
        out_shape=[jax.ShapeDtypeStruct((s.shape[0], s.shape[1] // 2) + s.shape[2:], s.dtype) for s in grads],
        in_specs=[ANY] * n, out_specs=[ANY] * n,
        scratch_shapes=[pltpu.SemaphoreType.DMA((n,)), pltpu.SemaphoreType.DMA((n,))],
    )(*grads)


HBM = pl.BlockSpec(memory_space=pltpu.HBM)
SEM = pl.BlockSpec(memory_space=pltpu.SEMAPHORE)
DATAFLOW = pltpu.SideEffectType.DATAFLOW_SIDE_EFFECTING


def _broadcast_copies(buf, send_sems, recv_sems):
    x, y, c, _, others = _place()
    mine = buf.at[2 * x + y]
    return [_remote(mine, mine, send_sems[j], recv_sems[j], (ox, oy, c)) for j, (ox, oy) in enumerate(others)]


def _broadcast_wait(sems, placed, after, name):
    k = N_CHIPS - 1

    def body(buf, *refs):
        send_sems, recv_sems = refs[:k], refs[k:2 * k]
        for cp in _broadcast_copies(buf, send_sems, recv_sems):
            cp.wait_send()
            cp.wait_recv()

    return pl.pallas_call(
        body, name=name, out_shape=pltpu.HBM(placed.shape, placed.dtype),
        in_specs=[HBM] + [SEM] * (2 * k) + [ANY], out_specs=HBM, input_output_aliases={0: 0},
        compiler_params=pltpu.CompilerParams(has_side_effects=DATAFLOW),
    )(placed, *sems, after)


def _scatter_copies(s, got, send_sems, recv_sems):
    x, y, c, _, others = _place()
    me = 2 * x + y
    n = len(s)
    return [_remote(s[i].at[2 * ox + oy], got[i].at[me], send_sems[3 * i + j], recv_sems[3 * i + j], (ox, oy, c))
            for i in range(n) for j, (ox, oy) in enumerate(others)]


def _scatter_start(sums, placed):
    n = len(sums)
    k = 3 * n

    def body(*refs):
        s, got, token = refs[:n], refs[n:2 * n], refs[-1]
        send_sems, recv_sems = refs[2 * n:2 * n + k], refs[2 * n + k:2 * n + 2 * k]
        for cp in _scatter_copies(s, got, send_sems, recv_sems):
            cp.start()
        token[...] = jnp.zeros_like(token)

    hbm = [pltpu.HBM(a.shape, a.dtype) for a in list(sums) + list(placed)]
    out = pl.pallas_call(
        body, name="scatter_start",
        out_shape=[pltpu.SemaphoreType.DMA(())] * (2 * k) + hbm + [jax.ShapeDtypeStruct((8, 128), F32)],
        in_specs=[HBM] * (2 * n), out_specs=[SEM] * (2 * k) + [HBM] * (2 * n) + [pl.BlockSpec(memory_space=pltpu.VMEM)],
        input_output_aliases={i: 2 * k + i for i in range(2 * n)},
        compiler_params=pltpu.CompilerParams(has_side_effects=DATAFLOW),
    )(*[pltpu.with_memory_space_constraint(a, pltpu.HBM) for a in list(sums) + list(placed)])
    return out[:2 * k], out[2 * k:2 * k + n], out[2 * k + n:2 * k + 2 * n], out[-1]


def _scatter_wait(sems, sums, placed, after):
    n = len(sums)
    k = 3 * n

    def body(*refs):
        s, got = refs[:n], refs[n:2 * n]
        send_sems, recv_sems = refs[2 * n:2 * n + k], refs[2 * n + k:2 * n + 2 * k]
        for cp in _scatter_copies(s, got, send_sems, recv_sems):
            cp.wait_send()
            cp.wait_recv()

    hbm = [pltpu.HBM(a.shape, a.dtype) for a in list(sums) + list(placed)]
    out = pl.pallas_call(
        body, name="scatter_wait", out_shape=hbm,
        in_specs=[HBM] * (2 * n) + [SEM] * (2 * k) + [ANY] * len(after), out_specs=[HBM] * (2 * n),
        input_output_aliases={i: i for i in range(2 * n)},
        compiler_params=pltpu.CompilerParams(has_side_effects=DATAFLOW),
    )(*sums, *placed, *sems, *after)
    return out[n:]


def _join_copies(bufs, send_sems, recv_sems):
    x, y, c, sibling, _ = _place()
    return [_remote(b.at[c], b.at[c], send_sems[i], recv_sems[i], sibling) for i, b in enumerate(bufs)]


def _join_start(placed):
    n = len(placed)

    def body(*refs):
        bufs, send_sems, recv_sems = refs[:n], refs[n:2 * n], refs[2 * n:3 * n]
        for cp in _join_copies(bufs, send_sems, recv_sems):
            cp.start()

    hbm = [pltpu.HBM(a.shape, a.dtype) for a in placed]
    out = pl.pallas_call(
        body, name="join_start", out_shape=[pltpu.SemaphoreType.DMA(())] * (2 * n) + hbm,
        in_specs=[HBM] * n, out_specs=[SEM] * (2 * n) + [HBM] * n,
        input_output_aliases={i: 2 * n + i for i in range(n)},
        compiler_params=pltpu.CompilerParams(has_side_effects=DATAFLOW),
    )(*[pltpu.with_memory_space_constraint(a, pltpu.HBM) for a in placed])
    return out[:2 * n], out[2 * n:]


def _join_wait(sems, placed, after):
    n = len(placed)

    def body(*refs):
        bufs, send_sems, recv_sems = refs[:n], refs[n:2 * n], refs[2 * n:3 * n]
        for cp in _join_copies(bufs, send_sems, recv_sems):
            cp.wait_send()
            cp.wait_recv()

    return pl.pallas_call(
        body, name="join_wait", out_shape=[pltpu.HBM(a.shape, a.dtype) for a in placed],
        in_specs=[HBM] * n + [SEM] * (2 * n) + [ANY] * len(after), out_specs=[HBM] * n,
        input_output_aliases={i: i for i in range(n)},
        compiler_params=pltpu.CompilerParams(has_side_effects=DATAFLOW),
    )(*placed, *sems, *after)


def _gather_rows(row):
    def body(row_ref, out_ref, send_sems, recv_sems, local_sem):
        x, y, c = lax.axis_index("x"), lax.axis_index("y"), lax.axis_index("c")
        me = 4 * x + 2 * y + c
        local = pltpu.make_async_copy(row_ref, out_ref.at[me], local_sem)
        local.start()
        sent = []
        peers = []
        for k in range(1, N_DEV):
            px, py, pc = x ^ (k >> 2), y ^ ((k >> 1) & 1), c ^ (k & 1)
            peers.append((k, px, py, pc))
            cp = _remote(row_ref, out_ref.at[me], send_sems.at[k - 1], recv_sems.at[k - 1], (px, py, pc))
            cp.start()
            sent.append(cp)
        for k, px, py, pc in peers:
            slot = out_ref.at[4 * px + 2 * py + pc]
            _remote(slot, slot, send_sems.at[k - 1], recv_sems.at[k - 1], (px, py, pc)).wait_recv()
        for cp in sent:
            cp.wait_send()
        local.wait()

    return pl.pallas_call(
        body, name="gather_rows", out_shape=jax.ShapeDtypeStruct((N_DEV,) + row.shape, row.dtype),
        in_specs=[ANY], out_specs=ANY,
        scratch_shapes=[pltpu.SemaphoreType.DMA((N_DEV - 1,)), pltpu.SemaphoreType.DMA((N_DEV - 1,)),
                        pltpu.SemaphoreType.DMA],
    )(row)


def _scalar(i):
    return jnp.reshape(i, (1,)).astype(jnp.int32)


def _place_block(src, n_slots, slot, out_dtype, name, window=0, n_windows=1, after=None):
    rows, cols = src.shape[0], src.shape[1] // n_windows
    tr = _divisor_tile(rows, max(16, (2 * MIB) // (cols * 4)), 16)
    ordered = [] if after is None else [after]

    def body(slot_ref, *refs):
        s_ref, o_ref = refs[len(ordered):]
        o_ref[...] = s_ref[...].astype(o_ref.dtype)

    return pl.pallas_call(
        body, name=name, out_shape=jax.ShapeDtypeStruct((n_slots, rows, cols), out_dtype),
        grid_spec=pltpu.PrefetchScalarGridSpec(
            num_scalar_prefetch=1, grid=(rows // tr,),
            in_specs=[pl.BlockSpec(t.shape, lambda r, sl: (0, 0)) for t in ordered]
            + [pl.BlockSpec((tr, cols), lambda r, sl: (r, window))],
            out_specs=pl.BlockSpec((None, tr, cols), lambda r, sl: (sl[0], r, 0))),
        compiler_params=_params(("parallel",)))(_scalar(slot), *ordered, src)


def _add_halves(g, theirs, core, chip, name):
    n, half, cols = theirs.shape
    tr = _divisor_tile(half, max(16, (4 * MIB) // (cols * 4)), 16)
    per = half // tr

    def body(at_ref, a_ref, b_ref, o_ref, own_ref):
        total = (a_ref[...].astype(F32) + b_ref[...].astype(F32)).astype(o_ref.dtype)
        o_ref[...] = total

        @pl.when(pl.program_id(1) == at_ref[1])
        def _():
            own_ref[...] = total

    spec = pl.BlockSpec((None, tr, cols), lambda r, i, at: (i, r, 0))
    shape = jax.ShapeDtypeStruct(theirs.shape, BF16)
    return pl.pallas_call(
        body, name=name, out_shape=[shape, shape],
        grid_spec=pltpu.PrefetchScalarGridSpec(
            num_scalar_prefetch=1, grid=(per, n),
            in_specs=[pl.BlockSpec((None, tr, cols), lambda r, i, at: (i, at[0] * per + r, 0)), spec],
            out_specs=[spec, pl.BlockSpec((None, tr, cols), lambda r, i, at: (at[1], r, 0))]),
        compiler_params=_params(("parallel", "arbitrary")))(jnp.concatenate([_scalar(core), _scalar(chip)]), g, theirs)


def _sum_slots(a, core, name):
    n, rows, cols = a.shape
    tr = _divisor_tile(rows, max(16, (8 * MIB) // (cols * 4 * n)), 16)

    def body(c_ref, a_ref, o_ref):
        acc = a_ref[0].astype(F32)
        for i in range(1, n):
            acc = acc + a_ref[i].astype(F32)
        o_ref[...] = acc

    return pl.pallas_call(
        body, name=name, out_shape=jax.ShapeDtypeStruct((2, rows, cols), F32),
        grid_spec=pltpu.PrefetchScalarGridSpec(
            num_scalar_prefetch=1, grid=(rows // tr,),
            in_specs=[pl.BlockSpec((n, tr, cols), lambda r, c: (0, r, 0))],
            out_specs=pl.BlockSpec((None, tr, cols), lambda r, c: (c[0], r, 0))),
        compiler_params=_params(("parallel",)))(_scalar(core), a)


def _adamw_math(w, g, m, v):
    m = ADAM_B1 * m + (1.0 - ADAM_B1) * g
    v = ADAM_B2 * v + (1.0 - ADAM_B2) * (g * g)
    m_hat = m / (1.0 - ADAM_B1 ** ADAM_STEP)
    v_hat = v / (1.0 - ADAM_B2 ** ADAM_STEP)
    delta = -ADAM_LR * (m_hat / (jnp.sqrt(v_hat) + ADAM_EPS) + ADAM_WD * w)
    return delta, m, v


def _adamw_half(w, g2, m, v, which, prev, name):
    rows, cols = w.shape
    half = rows // 2
    tr = _divisor_tile(half, max(8, (2 * MIB) // (cols * 4)), 8)
    per = half // tr
    n_out = 4

    def body(h_ref, w_ref, g_ref, m_ref, v_ref, *refs):
        d_ref, nm_ref, nv_ref, go_ref = refs[-n_out:]
        g = g_ref[...]
        d, nm, nv = _adamw_math(w_ref[...], g, m_ref[...], v_ref[...])
        d_ref[...] = d
        nm_ref[...] = nm
        nv_ref[...] = nv
        go_ref[...] = g

    spec = pl.BlockSpec((tr, cols), lambda r, h: (h[0] * per + r, 0))
    in_specs = [spec, pl.BlockSpec((None, tr, cols), lambda r, h: (h[0], r, 0)), spec, spec]
    args = [_scalar(which), w, g2, m, v]
    aliases = {}
    if prev is not None:
        aliases = {len(args) + i: i for i in range(n_out)}
        in_specs += [ANY] * n_out
        args += list(prev)
    return pl.pallas_call(
        body, name=name, out_shape=[jax.ShapeDtypeStruct((rows, cols), F32)] * n_out,
        grid_spec=pltpu.PrefetchScalarGridSpec(num_scalar_prefetch=1, grid=(per,), in_specs=in_specs,
                                               out_specs=[spec] * n_out),
        input_output_aliases=aliases, compiler_params=_params(("parallel",)))(*args)


def _sum_rows_adamw(parts, w, m, v):
    def body(p_ref, w_ref, m_ref, v_ref, g_ref, d_ref, nm_ref, nv_ref):
        g = p_ref[0]
        for i in range(1, N_DEV):
            g = g + p_ref[i]
        d, nm, nv = _adamw_math(w_ref[...], g, m_ref[...], v_ref[...])
        g_ref[...] = g
        d_ref[...] = d
        nm_ref[...] = nm
        nv_ref[...] = nv

    shape = jax.ShapeDtypeStruct(w.shape, F32)
    return pl.pallas_call(body, name="sum_rows_adamw", out_shape=[shape] * 4)(parts, w, m, v)


LANES = 128


def _permute_scratch(rows, width):
    return pltpu.VMEM((width // LANES, rows, LANES), F32)


def _split_rows(value, scratch, dil):
    if dil == 1:
        return [value]
    rows = value.shape[0] // dil
    slabs = value.shape[1] // LANES
    for c in range(slabs):
        scratch[c] = value[:, c * LANES:(c + 1) * LANES]
    return [jnp.concatenate([scratch[c, pl.ds(r, rows, stride=dil), :] for c in range(slabs)], axis=1)
            for r in range(dil)]


def _merge_rows(ref, scratch, dil):
    if dil == 1:
        return ref[0].astype(F32)
    rows = ref.shape[1]
    slabs = ref.shape[2] // LANES
    for r in range(dil):
        part = ref[r].astype(F32)
        for c in range(slabs):
            scratch[c, pl.ds(r, rows, stride=dil), :] = part[:, c * LANES:(c + 1) * LANES]
    return jnp.concatenate([scratch[c] for c in range(slabs)], axis=1)


def _grouped_view(t, dil):
    return t.reshape(dil, t.shape[0] // dil, t.shape[1])


def _grouped_spec(dil, rows, width, index):
    return pl.BlockSpec((dil, rows // dil, width), index)


W_CHUNKS = 4


def _pick(values, j):
    out = values[-1]
    for i in range(len(values) - 2, -1, -1):
        out = jnp.where(j == i, values[i], out)
    return out


def _chunk_of(col, per_chip):
    return (col % per_chip) // (per_chip // W_CHUNKS)


def _w_block(col, per_chip):
    return col // per_chip, 0, (col % per_chip) % (per_chip // W_CHUNKS)


def _in_proj(xb, wc, blocks, j0, ncols, tn, out_dtype, prev, after, name):
    s, d = xb.shape
    per_chip = wc.shape[2] * W_CHUNKS // tn
    tm = _divisor_tile(s, 1024, 16)
    extra = [t for t in after if t is not None]

    def body(*refs):
        a_ref, b_ref = refs[len(extra):len(extra) + 2]
        o_ref = refs[-1]
        o_ref[...] = jnp.dot(a_ref[...], b_ref[...], preferred_element_type=F32).astype(o_ref.dtype)

    in_specs = [pl.BlockSpec(t.shape, lambda j, m: (0, 0)) for t in extra] + [
        pl.BlockSpec((tm, d), lambda j, m: (m, 0)),
        pl.BlockSpec((None, d, tn), lambda j, m: _w_block(_pick(blocks, j), per_chip))]
    args = extra + [xb, wc]
    aliases = {}
    if prev is not None:
        aliases = {len(args): 0}
        in_specs.append(ANY)
        args.append(prev)
    return pl.pallas_call(
        body, name=name, grid=(len(blocks), s // tm), in_specs=in_specs,
        out_specs=pl.BlockSpec((tm, tn), lambda j, m: (m, _pick(blocks, j) - j0)),
        out_shape=jax.ShapeDtypeStruct((s, ncols), out_dtype), input_output_aliases=aliases,
        compiler_params=_params(("parallel", "parallel")))(*args)


def _in_proj_qkv(xb, wc, g, blocks, aw, tn, prev, after, name):
    s, d = xb.shape
    dil = DILATIONS[g]
    per_chip = wc.shape[2] * W_CHUNKS // tn
    sub = aw // tn
    tm = _divisor_tile(s, 1024, 16 * dil)
    extra = [t for t in after if t is not None]

    def body(*refs):
        a_ref, b_ref = refs[len(extra):len(extra) + 2]
        o_ref, scratch = refs[-2:]
        res = jnp.dot(a_ref[...], b_ref[...], preferred_element_type=F32)
        for r, part in enumerate(_split_rows(res, scratch, dil)):
            o_ref[r] = part.astype(BF16)

    def out_index(j, m):
        col = _pick(blocks, j)
        return (col // sub) // N_GROUPS, 0, m, col % sub

    in_specs = [pl.BlockSpec(t.shape, lambda j, m: (0, 0)) for t in extra] + [
        pl.BlockSpec((tm, d), lambda j, m: (m, 0)),
        pl.BlockSpec((None, d, tn), lambda j, m: _w_block(_pick(blocks, j), per_chip))]
    args = extra + [xb, wc]
    aliases = {}
    if prev is not None:
        aliases = {len(args): 0}
        in_specs.append(ANY)
        args.append(prev)
    return pl.pallas_call(
        body, name=name, grid=(len(blocks), s // tm), in_specs=in_specs,
        out_specs=pl.BlockSpec((None, dil, tm // dil, tn), out_index),
        out_shape=jax.ShapeDtypeStruct((3, dil, s // dil, aw), BF16), input_output_aliases=aliases,
        scratch_shapes=[_permute_scratch(tm, tn)],
        compiler_params=_params(("parallel", "parallel")))(*args)


def _window_mask(first):
    qi = lax.broadcasted_iota(jnp.int32, (STEPS, 2 * STEPS), 0)
    kj = lax.broadcasted_iota(jnp.int32, (STEPS, 2 * STEPS), 1)
    lowest = jnp.where(first, STEPS, 0)
    return (kj >= qi) & (kj <= qi + STEPS) & (kj >= lowest)


def _attn_fwd(qkv, g):
    _, s, aw = qkv.shape
    heads = aw // HEAD_DIM
    n_blocks = s // STEPS
    per_seq = n_blocks // DILATIONS[g]
    pair = 4 if n_blocks % 4 == 0 else 1

    def body(q_ref, kc_ref, kp_ref, vc_ref, vp_ref, o_ref, l_ref):
        masks = [_window_mask(lax.rem(pl.program_id(0) * pair + j, per_seq) == 0) for j in range(pair)]
        for h in range(heads):
            hs = slice(h * HEAD_DIM, (h + 1) * HEAD_DIM)
            keys = jnp.concatenate([kp_ref[:, hs], kc_ref[:, hs]], axis=0)
            values = jnp.concatenate([vp_ref[:, hs], vc_ref[:, hs]], axis=0)
            for j in range(pair):
                rows = slice(j * STEPS, (j + 1) * STEPS)
                window = slice(j * STEPS, (j + 2) * STEPS)
                sc = lax.dot_general(q_ref[rows, hs], keys[window], NT, preferred_element_type=F32) * SCORE_SCALE
                sc = jnp.where(masks[j], sc, NEG_INF)
                mx = jnp.max(sc, axis=1, keepdims=True)
                e = jnp.exp(sc - mx)
                den = jnp.sum(e, axis=1, keepdims=True)
                o_ref[rows, hs] = (jnp.dot(e.astype(BF16), values[window], preferred_element_type=F32)
                                   / den).astype(BF16)
                l_ref[rows, hs] = jnp.broadcast_to(mx + jnp.log(den), (STEPS, HEAD_DIM))

    def cur(which):
        return pl.BlockSpec((None, pair * STEPS, aw), lambda b: (which, b, 0))

    def prev(which):
        return pl.BlockSpec((None, STEPS, aw), lambda b: (which, jnp.maximum(pair * b - 1, 0), 0))

    out = pl.BlockSpec((pair * STEPS, aw), lambda b: (b, 0))
    return pl.pallas_call(
        body, name=f"attn_fwd{g}", grid=(n_blocks // pair,),
        in_specs=[cur(0), cur(1), prev(1), cur(2), prev(2)], out_specs=[out, out],
        out_shape=[jax.ShapeDtypeStruct((s, aw), BF16), jax.ShapeDtypeStruct((s, aw), F32)],
        compiler_params=_params(("parallel",)))(qkv, qkv, qkv, qkv, qkv)


def _combine_groups(os, ls, zuz, aw):
    s = zuz.shape[0]
    tr = _divisor_tile(s, 256, 8 * DILATIONS[-1])

    def body(*refs):
        o_refs, l_refs, z_ref = refs[0:3], refs[3:6], refs[6]
        oo_ref, y_ref, yt_ref = refs[7:10]
        lq_refs, scratch = refs[10:13], refs[13]
        ls_ = [_merge_rows(l_refs[g], scratch, dil) for g, dil in enumerate(DILATIONS)]
        mx = jnp.maximum(jnp.maximum(ls_[0], ls_[1]), ls_[2])
        ws = [jnp.exp(l - mx) for l in ls_]
        den = ws[0] + ws[1] + ws[2]
        o = ws[0] * _merge_rows(o_refs[0], scratch, DILATIONS[0])
        for g in range(1, N_GROUPS):
            o = o + ws[g] * _merge_rows(o_refs[g], scratch, DILATIONS[g])
        o = o / den
        z = z_ref[...].astype(F32)
        y = o * (z * _sigmoid(z))
        oo_ref[...] = o.astype(BF16)
        y_ref[...] = y.astype(BF16)
        yt_ref[...] = y.T.astype(BF16)
        for g, dil in enumerate(DILATIONS):
            for r, part in enumerate(_split_rows(mx + jnp.log(den), scratch, dil)):
                lq_refs[g][r] = part

    grouped = [_grouped_spec(dil, tr, aw, lambda r: (0, r, 0)) for dil in DILATIONS]
    one = pl.BlockSpec((tr, aw), lambda r: (r, 0))
    b16 = jax.ShapeDtypeStruct((s, aw), BF16)
    out = pl.pallas_call(
        body, name="combine_groups", grid=(s // tr,),
        in_specs=grouped + grouped + [one],
        out_specs=[one, one, pl.BlockSpec((aw, tr), lambda r: (0, r))] + grouped,
        out_shape=[b16, b16, jax.ShapeDtypeStruct((aw, s), BF16)]
        + [jax.ShapeDtypeStruct((dil, s // dil, aw), F32) for dil in DILATIONS],
        scratch_shapes=[_permute_scratch(tr, aw)],
        compiler_params=_params(("parallel",)))(
            *[_grouped_view(t, dil) for t, dil in zip(os, DILATIONS)],
            *[_grouped_view(t, dil) for t, dil in zip(ls, DILATIONS)], zuz)
    return out[0], out[1], out[2], [t.reshape(s, aw) for t in out[3:]]


def _pool_counts(row0, rows, window):
    t = row0 + lax.broadcasted_iota(jnp.int32, (rows, 1), 0)
    return jnp.minimum(t + 1, window).astype(F32)


def _pool_fwd(zuz, w_pool, pool_scale, aw, pw):
    s = zuz.shape[0]
    pg = pw // len(POOL_WINDOWS)
    tr = _divisor_tile(s, 256, 128)
    u_col, z_col = aw // pw, aw // pw + 1
    assert aw % pw == 0

    def body(u_ref, up_ref, z_ref, w_ref, sc_ref, p_ref, l_ref, y_ref, yt_ref):
        r = pl.program_id(0)
        u = u_ref[...].astype(F32)
        halo = jnp.where(r > 0, up_ref[...].astype(F32), 0.0)
        ext = jnp.concatenate([halo, u], axis=0)
        pieces, lins = [], []
        for gi, window in enumerate(POOL_WINDOWS):
            cs = slice(gi * pg, (gi + 1) * pg)
            acc = ext[:, cs]
            shift = 1
            while shift < window:
                acc = acc + pltpu.roll(acc, shift, 0)
                shift *= 2
            p = acc[POOL_HALO:] / _pool_counts(r * tr, tr, window) - u[:, cs]
            pieces.append(p)
            lins.append(jnp.dot(p.astype(BF16), w_ref[gi], preferred_element_type=F32))
        p = jnp.concatenate(pieces, axis=1)
        lin = jnp.concatenate(lins, axis=1)
        z = z_ref[...].astype(F32)
        y = lin * sc_ref[...] * (z * _sigmoid(z))
        p_ref[...] = p.astype(BF16)
        l_ref[...] = lin
        y_ref[...] = y.astype(BF16)
        yt_ref[...] = y.T.astype(BF16)

    per = tr // POOL_HALO
    out = pl.BlockSpec((tr, pw), lambda r: (r, 0))
    return pl.pallas_call(
        body, name="pool_fwd", grid=(s // tr,),
        in_specs=[pl.BlockSpec((tr, pw), lambda r: (r, u_col)),
                  pl.BlockSpec((POOL_HALO, pw), lambda r: (jnp.maximum(r * per - 1, 0), u_col)),
                  pl.BlockSpec((tr, pw), lambda r: (r, z_col)),
                  pl.BlockSpec((len(POOL_WINDOWS), pg, pg), lambda r: (0, 0, 0)),
                  pl.BlockSpec((1, pw), lambda r: (0, 0))],
        out_specs=[out, out, out, pl.BlockSpec((pw, tr), lambda r: (0, r))],
        out_shape=[jax.ShapeDtypeStruct((s, pw), BF16), jax.ShapeDtypeStruct((s, pw), F32),
                   jax.ShapeDtypeStruct((s, pw), BF16), jax.ShapeDtypeStruct((pw, s), BF16)],
        compiler_params=_params(("parallel",)))(zuz, zuz, zuz, w_pool, pool_scale)


def _proj_merge(y_attn, y_pool, wpa4, wpp4, gpre, b_gate):
    s, aw = y_attn.shape
    pw = y_pool.shape[1]
    tn = wpa4.shape[2]
    d = N_CHIPS * tn
    tm = _divisor_tile(s, 1024, 128)

    def body(ya_ref, yp_ref, wa_ref, wp_ref, ga_ref, gp_ref, ba_ref, bp_ref, a_ref, p_ref, sa_ref, sp_ref, m_ref,
             mt_ref):
        a = jnp.dot(ya_ref[...], wa_ref[...], preferred_element_type=F32)
        p = jnp.dot(yp_ref[...], wp_ref[...], preferred_element_type=F32)
        sa = _sigmoid(ga_ref[...].astype(F32) + ba_ref[...])
        sp = _sigmoid(gp_ref[...].astype(F32) + bp_ref[...])
        merged = sa * a + sp * p
        a_ref[...] = a.astype(BF16)
        p_ref[...] = p.astype(BF16)
        sa_ref[...] = sa.astype(BF16)
        sp_ref[...] = sp.astype(BF16)
        m_ref[...] = merged.astype(BF16)
        mt_ref[...] = merged.T.astype(BF16)

    out = pl.BlockSpec((tm, tn), lambda n, m: (m, n))
    f = jax.ShapeDtypeStruct((s, d), BF16)
    return pl.pallas_call(
        body, name="proj_merge", grid=(N_CHIPS, s // tm),
        in_specs=[pl.BlockSpec((tm, aw), lambda n, m: (m, 0)), pl.BlockSpec((tm, pw), lambda n, m: (m, 0)),
                  pl.BlockSpec((None, aw, tn), lambda n, m: (n, 0, 0)),
                  pl.BlockSpec((None, pw, tn), lambda n, m: (n, 0, 0)),
                  pl.BlockSpec((tm, tn), lambda n, m: (m, n)), pl.BlockSpec((tm, tn), lambda n, m: (m, N_CHIPS + n)),
                  pl.BlockSpec((1, tn), lambda n, m: (0, n)), pl.BlockSpec((1, tn), lambda n, m: (0, N_CHIPS + n))],
        out_specs=[out] * 5 + [pl.BlockSpec((tn, tm), lambda n, m: (n, m))],
        out_shape=[f] * 5 + [jax.ShapeDtypeStruct((d, s), BF16)],
        compiler_params=_params(("parallel", "parallel")))(y_attn, y_pool, wpa4, wpp4, gpre, gpre, b_gate, b_gate)


def _out_norm_loss(merged, w_out, x, target, gamma, beta):
    s, d = x.shape
    tm = _divisor_tile(s, 256, 16)

    def body(m_ref, w_ref, x_ref, t_ref, g_ref, b_ref, dr_ref, drb_ref, loss_ref, dg_ref, db_ref):
        @pl.when(pl.program_id(0) == 0)
        def _():
            loss_ref[...] = jnp.zeros_like(loss_ref)
            dg_ref[...] = jnp.zeros_like(dg_ref)
            db_ref[...] = jnp.zeros_like(db_ref)

        r = ALPHA * x_ref[...] + jnp.dot(m_ref[...], w_ref[...], preferred_element_type=F32)
        mu = jnp.mean(r, axis=1, keepdims=True)
        rc = r - mu
        rstd = lax.rsqrt(jnp.mean(rc * rc, axis=1, keepdims=True) + LN_EPS)
        xhat = rc * rstd
        diff = xhat * g_ref[...] + b_ref[...] - t_ref[...]
        dy = diff / d
        loss_ref[...] += jnp.sum(diff * diff, axis=0, keepdims=True)
        dg_ref[...] += jnp.sum(dy * xhat, axis=0, keepdims=True)
        db_ref[...] += jnp.sum(dy, axis=0, keepdims=True)
        dxhat = dy * g_ref[...]
        dr = rstd * (dxhat - jnp.mean(dxhat, axis=1, keepdims=True)
                     - xhat * jnp.mean(dxhat * xhat, axis=1, keepdims=True))
        dr_ref[...] = dr
        drb_ref[...] = dr.astype(BF16)

    row = pl.BlockSpec((tm, d), lambda m: (m, 0))
    vec = pl.BlockSpec((1, d), lambda m: (0, 0))
    v = jax.ShapeDtypeStruct((1, d), F32)
    return pl.pallas_call(
        body, name="out_norm_loss", grid=(s // tm,),
        in_specs=[row, pl.BlockSpec((d, d), lambda m: (0, 0)), row, row, vec, vec],
        out_specs=[row, row, vec, vec, vec],
        out_shape=[jax.ShapeDtypeStruct((s, d), F32), jax.ShapeDtypeStruct((s, d), BF16), v, v, v],
        compiler_params=_params(("arbitrary",), vmem_mib=56))(merged, w_out, x, target, gamma, beta)


def _merge_bwd(drb, w_out, a, p, sa, sp):
    s, d = drb.shape
    tm = _divisor_tile(s, 1024, 16)
    tn = d // N_CHIPS

    def body(dr_ref, w_ref, a_ref, p_ref, sa_ref, sp_ref, da_ref, dp_ref, dga_ref, dgp_ref, dba_ref, dbp_ref):
        @pl.when(pl.program_id(1) == 0)
        def _():
            dba_ref[...] = jnp.zeros_like(dba_ref)
            dbp_ref[...] = jnp.zeros_like(dbp_ref)

        dm = lax.dot_general(dr_ref[...], w_ref[...], NT, preferred_element_type=F32)
        sa = sa_ref[...].astype(F32)
        sp = sp_ref[...].astype(F32)
        da_ref[...] = (dm * sa).astype(BF16)
        dp_ref[...] = (dm * sp).astype(BF16)
        dga = dm * a_ref[...].astype(F32) * sa * (1.0 - sa)
        dgp = dm * p_ref[...].astype(F32) * sp * (1.0 - sp)
        dga_ref[...] = dga.astype(BF16)
        dgp_ref[...] = dgp.astype(BF16)
        dba_ref[...] += jnp.sum(dga, axis=0, keepdims=True)
        dbp_ref[...] += jnp.sum(dgp, axis=0, keepdims=True)

    blk = pl.BlockSpec((tm, tn), lambda n, m: (m, n))
    vec = pl.BlockSpec((1, tn), lambda n, m: (0, n))
    b16 = jax.ShapeDtypeStruct((s, d), BF16)
    v = jax.ShapeDtypeStruct((1, d), F32)
    return pl.pallas_call(
        body, name="merge_bwd", grid=(N_CHIPS, s // tm),
        in_specs=[pl.BlockSpec((tm, d), lambda n, m: (m, 0)), pl.BlockSpec((tn, d), lambda n, m: (n, 0)),
                  blk, blk, blk, blk],
        out_specs=[blk, blk, blk, blk, vec, vec], out_shape=[b16, b16, b16, b16, v, v],
        compiler_params=_params(("parallel", "arbitrary")))(drb, w_out, a, p, sa, sp)


def _proj_t(dy_ref, w_ref, tn):
    acc = None
    for n in range(N_CHIPS):
        t = lax.dot_general(dy_ref[:, n * tn:(n + 1) * tn], w_ref[n], NT, preferred_element_type=F32)
        acc = t if acc is None else acc + t
    return acc


def _attn_gate_bwd(da, wpa4, zuz, o):
    s, d = da.shape
    aw, tn = wpa4.shape[1], wpa4.shape[2]
    heads = aw // HEAD_DIM
    tm = _divisor_tile(s, 256, 16 * DILATIONS[-1])

    def body(*refs):
        da_ref, w_ref, z_ref, o_ref, dz_ref = refs[:5]
        do_refs, dd_refs, scratch = refs[5:8], refs[8:11], refs[11]
        dy = _proj_t(da_ref, w_ref, tn)
        z, o = z_ref[...].astype(F32), o_ref[...].astype(F32)
        sg = _sigmoid(z)
        do = dy * (z * sg)
        dz_ref[...] = (dy * o * _dsilu(z, sg)).astype(BF16)
        prod = do * o
        dd = jnp.concatenate(
            [jnp.broadcast_to(jnp.sum(prod[:, h * HEAD_DIM:(h + 1) * HEAD_DIM], axis=1, keepdims=True),
                              (tm, HEAD_DIM)) for h in range(heads)], axis=1)
        for g, dil in enumerate(DILATIONS):
            for r, part in enumerate(_split_rows(do, scratch, dil)):
                do_refs[g][r] = part.astype(BF16)
            for r, part in enumerate(_split_rows(dd, scratch, dil)):
                dd_refs[g][r] = part

    row = pl.BlockSpec((tm, aw), lambda m: (m, 0))
    grouped = [_grouped_spec(dil, tm, aw, lambda m: (0, m, 0)) for dil in DILATIONS]
    out = pl.pallas_call(
        body, name="attn_gate_bwd", grid=(s // tm,),
        in_specs=[pl.BlockSpec((tm, d), lambda m: (m, 0)), pl.BlockSpec((N_CHIPS, aw, tn), lambda m: (0, 0, 0)),
                  row, row],
        out_specs=[row] + grouped + grouped,
        out_shape=[jax.ShapeDtypeStruct((s, aw), BF16)]
        + [jax.ShapeDtypeStruct((dil, s // dil, aw), BF16) for dil in DILATIONS]
        + [jax.ShapeDtypeStruct((dil, s // dil, aw), F32) for dil in DILATIONS],
        scratch_shapes=[_permute_scratch(tm, aw)],
        compiler_params=_params(("parallel",)))(da, wpa4, zuz, o)
    return out[0], [t.reshape(s, aw) for t in out[1:4]], [t.reshape(s, aw) for t in out[4:7]]


def _pool_gate_bwd(dp_in, wpp4, zuz, lin, pooled, w_pool, pool_scale, aw):
    s, d = dp_in.shape
    pw, tn = wpp4.shape[1], wpp4.shape[2]
    n_win = len(POOL_WINDOWS)
    pg = pw // n_win
    tm = _divisor_tile(s, 256, 16)
    z_col = aw // pw + 1

    def body(dp_ref, w_ref, z_ref, l_ref, p_ref, wp_ref, sc_ref, dz_ref, dpo_ref, dw_ref, ds_ref):
        @pl.when(pl.program_id(0) == 0)
        def _():
            dw_ref[...] = jnp.zeros_like(dw_ref)
            ds_ref[...] = jnp.zeros_like(ds_ref)

        dy = _proj_t(dp_ref, w_ref, tn)
        z, lin_ = z_ref[...].astype(F32), l_ref[...]
        sg = _sigmoid(z)
        dypp = dy * (z * sg)
        dz_ref[...] = (dy * (lin_ * sc_ref[...]) * _dsilu(z, sg)).astype(BF16)
        ds_ref[...] += jnp.sum(dypp * lin_, axis=0, keepdims=True)
        dlin = (dypp * sc_ref[...]).astype(BF16)
        for gi in range(n_win):
            cs = slice(gi * pg, (gi + 1) * pg)
            dw_ref[gi] += lax.dot_general(p_ref[:, cs], dlin[:, cs], TN, preferred_element_type=F32)
            dpo_ref[:, cs] = lax.dot_general(dlin[:, cs], wp_ref[gi], NT, preferred_element_type=F32)

    row = pl.BlockSpec((tm, pw), lambda m: (m, 0))
    return pl.pallas_call(
        body, name="pool_gate_bwd", grid=(s // tm,),
        in_specs=[pl.BlockSpec((tm, d), lambda m: (m, 0)), pl.BlockSpec((N_CHIPS, pw, tn), lambda m: (0, 0, 0)),
                  pl.BlockSpec((tm, pw), lambda m: (m, z_col)), row, row,
                  pl.BlockSpec((n_win, pg, pg), lambda m: (0, 0, 0)), pl.BlockSpec((1, pw), lambda m: (0, 0))],
        out_specs=[row, row, pl.BlockSpec((n_win, pg, pg), lambda m: (0, 0, 0)),
                   pl.BlockSpec((1, pw), lambda m: (0, 0))],
        out_shape=[jax.ShapeDtypeStruct((s, pw), BF16), jax.ShapeDtypeStruct((s, pw), F32),
                   jax.ShapeDtypeStruct((n_win, pg, pg), F32), jax.ShapeDtypeStruct((1, pw), F32)],
        compiler_params=_params(("arbitrary",)))(dp_in, wpp4, zuz, lin, pooled, w_pool, pool_scale)


def _pool_bwd(dpooled):
    s, pw = dpooled.shape
    pg = pw // len(POOL_WINDOWS)
    tr = _divisor_tile(s, 256, POOL_HALO)
    per = tr // POOL_HALO
    n_tiles = s // tr

    def body(c_ref, n_ref, du_ref):
        r = pl.program_id(0)
        cur = c_ref[...]
        halo = jnp.where(r < n_tiles - 1, n_ref[...], 0.0)
        ext = jnp.concatenate([cur, halo], axis=0)
        rows = tr + POOL_HALO
        for gi, window in enumerate(POOL_WINDOWS):
            cs = slice(gi * pg, (gi + 1) * pg)
            acc = ext[:, cs] / _pool_counts(r * tr, rows, window)
            shift = 1
            while shift < window:
                acc = acc + pltpu.roll(acc, rows - shift, 0)
                shift *= 2
            du_ref[:, cs] = (acc[:tr] - cur[:, cs]).astype(BF16)

    return pl.pallas_call(
        body, name="pool_bwd", grid=(n_tiles,),
        in_specs=[pl.BlockSpec((tr, pw), lambda r: (r, 0)),
                  pl.BlockSpec((POOL_HALO, pw), lambda r: (jnp.minimum((r + 1) * per, s // POOL_HALO - 1), 0))],
        out_specs=pl.BlockSpec((tr, pw), lambda r: (r, 0)),
        out_shape=jax.ShapeDtypeStruct((s, pw), BF16), compiler_params=_params(("parallel",)))(dpooled, dpooled)


def _attn_bwd(qkv, do, lse, dd, g):
    _, s, aw = qkv.shape
    heads = aw // HEAD_DIM
    n_blocks = s // STEPS
    per_seq = n_blocks // DILATIONS[g]
    pair = 4 if n_blocks % 4 == 0 else 1
    rows_ = pair * STEPS
    n_steps = n_blocks // pair
    tail = slice(rows_ - STEPS, rows_)

    def body(q_ref, do_ref, l_ref, dd_ref, kc_ref, kp_ref, vc_ref, vp_ref, out_ref, cq_ref, ck_ref, cv_ref):
        b = pl.program_id(0)

        @pl.when(b == 0)
        def _():
            cq_ref[...] = jnp.zeros_like(cq_ref)
            ck_ref[...] = jnp.zeros_like(ck_ref)
            cv_ref[...] = jnp.zeros_like(cv_ref)

        out_ref[0] = cq_ref[...].astype(BF16)

        @pl.when(b < n_steps)
        def _():
            masks = [_window_mask(lax.rem(b * pair + j, per_seq) == 0) for j in range(pair)]
            for h in range(heads):
                hs = slice(h * HEAD_DIM, (h + 1) * HEAD_DIM)
                keys = jnp.concatenate([kp_ref[:, hs], kc_ref[:, hs]], axis=0)
                values = jnp.concatenate([vp_ref[:, hs], vc_ref[:, hs]], axis=0)
                dks, dvs = [], []
                for j in range(pair):
                    rows = slice(j * STEPS, (j + 1) * STEPS)
                    window = slice(j * STEPS, (j + 2) * STEPS)
                    q, do_, kk, vv = q_ref[rows, hs], do_ref[rows, hs], keys[window], values[window]
                    lse_ = jnp.concatenate([l_ref[rows, hs], l_ref[rows, hs]], axis=1)
                    dd_ = jnp.concatenate([dd_ref[rows, hs], dd_ref[rows, hs]], axis=1)
                    sc = lax.dot_general(q, kk, NT, preferred_element_type=F32) * SCORE_SCALE
                    prob = jnp.where(masks[j], jnp.exp(sc - lse_), 0.0)
                    dprob = lax.dot_general(do_, vv, NT, preferred_element_type=F32)
                    dsc = prob * (dprob - dd_) * SCORE_SCALE
                    cq_ref[rows, hs] = jnp.dot(dsc.astype(BF16), kk, preferred_element_type=F32)
                    dks.append(lax.dot_general(dsc.astype(BF16), q, TN, preferred_element_type=F32))
                    dvs.append(lax.dot_general(prob.astype(BF16), do_, TN, preferred_element_type=F32))
                for which, carry, parts in ((1, ck_ref, dks), (2, cv_ref, dvs)):
                    out_ref[which, tail, hs] = (carry[tail, hs] + parts[0][:STEPS]).astype(BF16)
                    if pair > 1:
                        out_ref[which, :rows_ - STEPS, hs] = carry[:rows_ - STEPS, hs].astype(BF16)
                    for j in range(pair):
                        total = parts[j][STEPS:]
                        if j + 1 < pair:
                            total = total + parts[j + 1][:STEPS]
                        carry[j * STEPS:(j + 1) * STEPS, hs] = total

        @pl.when(b == n_steps)
        def _():
            out_ref[1] = ck_ref[...].astype(BF16)
            out_ref[2] = cv_ref[...].astype(BF16)

    last = n_steps - 1

    def cur(which):
        return pl.BlockSpec((None, rows_, aw), lambda b: (which, jnp.minimum(b, last), 0))

    def prev(which):
        return pl.BlockSpec((None, STEPS, aw), lambda b: (which, jnp.clip(b * pair - 1, 0, n_blocks - 1), 0))

    row = pl.BlockSpec((rows_, aw), lambda b: (jnp.minimum(b, last), 0))
    return pl.pallas_call(
        body, name=f"attn_bwd{g}", grid=(n_steps + 1,),
        in_specs=[cur(0), row, row, row, cur(1), prev(1), cur(2), prev(2)],
        out_specs=pl.BlockSpec((3, rows_, aw), lambda b: (0, jnp.clip(b - 1, 0, last), 0)),
        out_shape=jax.ShapeDtypeStruct((3, s, aw), BF16),
        scratch_shapes=[pltpu.VMEM((rows_, aw), F32)] * 3,
        compiler_params=_params(("arbitrary",)))(qkv, do, lse, dd, qkv, qkv, qkv, qkv)


def _weight_grad(at, b, tn, col_blocks, name):
    m, k = at.shape
    n = b.shape[1]
    tm = _divisor_tile(m, 1024, 16)
    tk = _divisor_tile(k, 2048, 128)
    nk = k // tk

    def body(a_ref, b_ref, o_ref, acc_ref):
        kk = pl.program_id(2)

        @pl.when(kk == 0)
        def _():
            acc_ref[...] = jnp.zeros_like(acc_ref)

        acc_ref[...] += jnp.dot(a_ref[...], b_ref[...], preferred_element_type=F32)

        @pl.when(kk == nk - 1)
        def _():
            o_ref[...] = acc_ref[...].astype(BF16)

    if col_blocks:
        out_spec = pl.BlockSpec((None, tm, tn), lambda i, j, kk: (j, i, 0))
        out_shape = jax.ShapeDtypeStruct((n // tn, m, tn), BF16)
    else:
        out_spec = pl.BlockSpec((tm, tn), lambda i, j, kk: (i, j))
        out_shape = jax.ShapeDtypeStruct((m, n), BF16)
    return pl.pallas_call(
        body, name=name, grid=(m // tm, n // tn, nk),
        in_specs=[pl.BlockSpec((tm, tk), lambda i, j, kk: (i, kk)), pl.BlockSpec((tk, tn), lambda i, j, kk: (kk, j))],
        out_specs=out_spec, out_shape=out_shape, scratch_shapes=[pltpu.VMEM((tm, tn), F32)],
        compiler_params=_params(("parallel", "parallel", "arbitrary")))(at, b)


def _w_in_grad_part(xt, b, col_of, n_local, tn, w_shape, prev, name):
    d, s = xt.shape
    per_chip = w_shape[2] // tn
    tm = _divisor_tile(d, 1024, 16)
    tk = _divisor_tile(s, 2048, 128)
    nk = s // tk

    def body(*refs):
        a_ref, b_ref, o_ref, acc_ref = refs[0], refs[1], refs[-2], refs[-1]
        kk = pl.program_id(2)

        @pl.when(kk == 0)
        def _():
            acc_ref[...] = jnp.zeros_like(acc_ref)

        acc_ref[...] += jnp.dot(a_ref[...], b_ref[...], preferred_element_type=F32)

        @pl.when(kk == nk - 1)
        def _():
            o_ref[...] = acc_ref[...].astype(BF16)

    if b.ndim == 3:
        sub = b.shape[2] // tn
        b_spec = pl.BlockSpec((None, tk, tn), lambda j, i, kk: (j // sub, kk, j % sub))
    else:
        b_spec = pl.BlockSpec((tk, tn), lambda j, i, kk: (kk, j))
    in_specs = [pl.BlockSpec((tm, tk), lambda j, i, kk: (i, kk)), b_spec]
    args = [xt, b]
    aliases = {}
    if prev is not None:
        in_specs.append(ANY)
        args.append(prev)
        aliases = {2: 0}
    return pl.pallas_call(
        body, name=name, grid=(n_local, d // tm, nk), in_specs=in_specs,
        out_specs=pl.BlockSpec((None, tm, tn), lambda j, i, kk: (col_of(j) // per_chip, i, col_of(j) % per_chip)),
        out_shape=jax.ShapeDtypeStruct(w_shape, BF16), scratch_shapes=[pltpu.VMEM((tm, tn), F32)],
        input_output_aliases=aliases,
        compiler_params=_params(("parallel", "parallel", "arbitrary")))(*args)


def _assemble_w(wcs, after):
    n, d, wc = wcs[0].shape
    tr = _divisor_tile(d, 256, 16)

    def body(after_ref, *refs):
        o_ref = refs[-1]
        for ch in range(W_CHUNKS):
            o_ref[:, ch * wc:(ch + 1) * wc] = refs[ch][...]

    return pl.pallas_call(
        body, name="assemble_w", grid=(n, d // tr),
        in_specs=[pl.BlockSpec(after.shape, lambda b, r: (0, 0))]
        + [pl.BlockSpec((None, tr, wc), lambda b, r: (b, r, 0))] * W_CHUNKS,
        out_specs=pl.BlockSpec((None, tr, W_CHUNKS * wc), lambda b, r: (b, r, 0)),
        out_shape=jax.ShapeDtypeStruct((n, d, W_CHUNKS * wc), wcs[0].dtype),
        compiler_params=_params(("parallel", "parallel")))(after, *wcs)


def _x_grad(dqkv, rest, w4, dr, aw, tn):
    s, d = dr.shape
    sub = aw // tn
    n_qkv = 3 * N_GROUPS * sub
    los, lo = [], n_qkv
    for p in rest:
        los.append(lo)
        lo += p.shape[1] // tn
    n_blocks = lo
    per_chip = n_blocks // N_CHIPS
    tm = _divisor_tile(s, 512, 16 * DILATIONS[-1])

    def body(*refs):
        q_refs, r_refs = refs[:N_GROUPS], refs[N_GROUPS:N_GROUPS + len(rest)]
        w_ref, dr_ref, o_ref, acc_ref, scratch = refs[-5:]
        j = pl.program_id(1)

        @pl.when(j == 0)
        def _():
            acc_ref[...] = ALPHA * dr_ref[...]

        for g, dil in enumerate(DILATIONS):
            @pl.when((j < n_qkv) & (lax.rem(j // sub, N_GROUPS) == g))
            def _(g=g, dil=dil):
                rows = _merge_rows(q_refs[g], scratch, dil).astype(BF16)
                acc_ref[...] += lax.dot_general(rows, w_ref[...], NT, preferred_element_type=F32)

        for p_ref, lo_, piece in zip(r_refs, los, rest):
            @pl.when((j >= lo_) & (j < lo_ + piece.shape[1] // tn))
            def _(p_ref=p_ref):
                acc_ref[...] += lax.dot_general(p_ref[...], w_ref[...], NT, preferred_element_type=F32)

        @pl.when(j == n_blocks - 1)
        def _():
            o_ref[...] = acc_ref[...]

    def qkv_spec(dil):
        def index(i, j):
            region = jnp.minimum(j // sub, 3 * N_GROUPS - 1)
            return region // N_GROUPS, 0, i, jnp.where(j < n_qkv, j % sub, 0)

        return pl.BlockSpec((None, dil, tm // dil, tn), index)

    def rest_spec(lo_, piece):
        n = piece.shape[1] // tn
        return pl.BlockSpec((tm, tn), lambda i, j: (i, jnp.clip(j - lo_, 0, n - 1)))

    row = pl.BlockSpec((tm, d), lambda i, j: (i, 0))
    return pl.pallas_call(
        body, name="x_grad", grid=(s // tm, n_blocks),
        in_specs=[qkv_spec(dil) for dil in DILATIONS] + [rest_spec(lo_, p) for lo_, p in zip(los, rest)]
        + [pl.BlockSpec((None, d, tn), lambda i, j: (j // per_chip, 0, j % per_chip)), row],
        out_specs=row, out_shape=jax.ShapeDtypeStruct((s, d), F32),
        scratch_shapes=[pltpu.VMEM((tm, d), F32), _permute_scratch(tm, tn)],
        compiler_params=_params(("parallel", "arbitrary"), vmem_mib=56))(
            *[t.reshape(3, dil, s // dil, aw) for t, dil in zip(dqkv, DILATIONS)], *rest, w4, dr)


def _prepare_x(x, after=None):
    s, d = x.shape
    tc = 2 * LANES
    slabs = tc // LANES
    ordered = [] if after is None else [after]

    def body(*refs):
        x_ref, xb_ref = refs[len(ordered):len(ordered) + 2]
        xt_refs, scratch = refs[len(ordered) + 2:len(ordered) + 2 + N_GROUPS], refs[-1]
        t = x_ref[...]
        xb_ref[...] = t.astype(BF16)
        for c in range(slabs):
            scratch[c] = t[:, c * LANES:(c + 1) * LANES]
        for g, dil in enumerate(DILATIONS):
            length = s // dil
            for r in range(dil):
                part = t if dil == 1 else jnp.concatenate(
                    [scratch[c, pl.ds(r, length, stride=dil), :] for c in range(slabs)], axis=1)
                xt_refs[g][:, r * length:(r + 1) * length] = part.T.astype(BF16)

    col = pl.BlockSpec((s, tc), lambda j: (0, j))
    row = pl.BlockSpec((tc, s), lambda j: (j, 0))
    t_shape = jax.ShapeDtypeStruct((d, s), BF16)
    out = pl.pallas_call(
        body, name="prepare_x", grid=(d // tc,),
        in_specs=[pl.BlockSpec(t.shape, lambda j: (0, 0)) for t in ordered] + [col],
        out_specs=[col] + [row] * N_GROUPS,
        out_shape=[jax.ShapeDtypeStruct((s, d), BF16)] + [t_shape] * N_GROUPS,
        scratch_shapes=[_permute_scratch(s, tc)], compiler_params=_params(("parallel",)))(*ordered, x)
    return out[0], out[1:]


def _local_step(x, target, w_open, w_close, w_width, b_gate, pool_scale, gamma, beta, aw, pw, small_weights,
                start_exchange=None, first_token=None):
    s, d = x.shape
    tn = _col_tile(aw, pw, w_width)
    sub = aw // tn
    per_chip = w_width // tn
    qkv_w = 3 * N_GROUPS * aw
    w_shape = (N_CHIPS, d, w_width)

    regions = [dict(kind=g, blocks=[(which * N_GROUPS + g) * sub + i for which in range(3) for i in range(sub)])
               for g in range(N_GROUPS)]
    lo = qkv_w // tn
    for name, width in (("zuz", aw + 2 * pw), ("gates", 2 * d)):
        regions.append(dict(kind=name, blocks=list(range(lo, lo + width // tn)), j0=lo, width=width))
        lo += width // tn
    results = [None] * len(regions)
    xb, xts = _prepare_x(x, first_token)
    wcs, last = [], []
    w_open(0, [xb])
    for ch in range(W_CHUNKS):
        wc, token = w_close(ch, last)
        wcs.append(wc)
        calls = []
        for i, region in enumerate(regions):
            blocks = [b for b in region["blocks"] if _chunk_of(b, per_chip) == ch]
            if blocks:
                calls.append((i, region, blocks))
        done = []
        for k, (i, region, blocks) in enumerate(calls):
            after = [token]
            if k == len(calls) - 1 and ch + 1 < W_CHUNKS:
                after.append(w_open(ch + 1, done))
            if region["kind"] in range(N_GROUPS):
                results[i] = _in_proj_qkv(xb, wc, region["kind"], blocks, aw, tn, results[i], after,
                                          f"in_proj_qkv{region['kind']}_{ch}")
            else:
                results[i] = _in_proj(xb, wc, blocks, region["j0"], region["width"], tn, BF16, results[i], after,
                                      f"in_proj_{region['kind']}_{ch}")
            done.append(results[i])
        last = done[-1:]
    qkv = [results[g].reshape(3, s, aw) for g in range(N_GROUPS)]
    zuz, gpre = results[N_GROUPS], results[N_GROUPS + 1]

    attn = [_attn_fwd(qkv[g], g) for g in range(N_GROUPS)]
    o, y_attn, y_attn_t, lse = _combine_groups([a[0] for a in attn], [a[1] for a in attn], zuz, aw)
    w_pool, wpa4, wpp4, w_out = small_weights(o)
    pooled, lin, y_pool, y_pool_t = _pool_fwd(zuz, w_pool, pool_scale, aw, pw)
    a, p, sa, sp, merged, merged_t = _proj_merge(y_attn, y_pool, wpa4, wpp4, gpre, b_gate)
    dr, drb, loss_lanes, d_gamma, d_beta = _out_norm_loss(merged, w_out, x, target, gamma, beta)

    da, dp, d_gpre_a, d_gpre_p, d_b_a, d_b_p = _merge_bwd(drb, w_out, a, p, sa, sp)
    d_b_gate = jnp.concatenate([d_b_a, d_b_p], axis=1)
    d_w_out = _weight_grad(merged_t, drb, d // N_CHIPS, False, "w_out_grad")
    d_wpa4 = _weight_grad(y_attn_t, da, d // N_CHIPS, True, "w_proj_attn_grad")
    d_wpp4 = _weight_grad(y_pool_t, dp, d // N_CHIPS, True, "w_proj_pool_grad")
    d_z_attn, d_o, dd = _attn_gate_bwd(da, wpa4, zuz, o)
    d_z_pool, d_pooled, d_w_pool, d_pool_scale = _pool_gate_bwd(dp, wpp4, zuz, lin, pooled, w_pool, pool_scale, aw)
    d_u = _pool_bwd(d_pooled)
    dqkv = [_attn_bwd(qkv[g], d_o[g], lse[g], dd[g], g) for g in range(N_GROUPS)]

    rest = [d_z_attn, d_u, d_z_pool, d_gpre_a, d_gpre_p]
    d_w_in4 = None
    for g in range(N_GROUPS):
        d_w_in4 = _w_in_grad_part(xts[g], dqkv[g], lambda j, g=g: ((j // sub) * N_GROUPS + g) * sub + j % sub,
                                  3 * sub, tn, w_shape, d_w_in4, f"w_in_grad_qkv{g}")
    lo = qkv_w // tn
    for i, piece in enumerate(rest):
        n_local = piece.shape[1] // tn
        d_w_in4 = _w_in_grad_part(xts[0], piece, lambda j, lo=lo: lo + j, n_local, tn, w_shape, d_w_in4,
                                  f"w_in_grad_rest{i}")
        lo += n_local
    grads = dict(loss_lanes=loss_lanes, w_in=d_w_in4, b_gate=d_b_gate, w_pool=d_w_pool,
                 pool_scale=d_pool_scale, w_proj_attn=d_wpa4, w_proj_pool=d_wpp4, w_out=d_w_out,
                 ln_gamma=d_gamma, ln_beta=d_beta)
    token = jnp.zeros((8, 128), F32) if start_exchange is None else start_exchange(grads)
    grads["d_x"] = _x_grad(dqkv, rest, _assemble_w(wcs, token), dr, aw, tn)
    return grads


def _pack_small(wpa, wpp, w_out, w_pool):
    width = wpa.shape[1]
    return jnp.concatenate([wpa, wpp, w_out.reshape(-1, width), w_pool.reshape(-1, width)], axis=0)


def _unpack_small(packed, aw, pw, d, pg):
    lead = packed.shape[:-2]
    width = d // N_CHIPS
    r0, r1, r2 = aw, aw + pw, aw + pw + d
    return (packed[..., :r0, :], packed[..., r0:r1, :], packed[..., r1:r2, :].reshape(lead + (width, d)),
            packed[..., r2:, :].reshape(lead + (len(POOL_WINDOWS), pg // N_CHIPS, pg)))


def _pack_rows(vectors, rows):
    flat = jnp.concatenate([v.reshape(-1) for v in vectors])
    return jnp.pad(flat, (0, rows * 128 - flat.shape[0])).reshape(rows, 128)


def _unpack_rows(packed, sizes):
    flat, out, lo = packed.reshape(-1), [], 0
    for n in sizes:
        out.append(flat[lo:lo + n].reshape(1, n))
        lo += n
    return out


def kernel(x, w_in, b_gate, w_pool, pool_scale, w_proj_attn, w_proj_pool, w_out, ln_gamma, ln_beta, loss_target, m_w_in, m_b_gate, m_w_pool, m_pool_scale, m_w_proj_attn, m_w_proj_pool, m_w_out, m_ln_gamma, m_ln_beta, v_w_in, v_b_gate, v_w_pool, v_pool_scale, v_w_proj_attn, v_w_proj_pool, v_w_out, v_ln_gamma, v_ln_beta):
    s, d = x.shape[1], x.shape[2]
    aw, pw = w_proj_attn.shape[1], w_proj_pool.shape[1]
    pg = w_pool.shape[3]
    n_win = len(POOL_WINDOWS)

    def small(wpa, wpp, wo, wpl):
        return _pack_small(wpa[0], wpp[0], wo[0], wpl[0])

    chip = 2 * lax.axis_index("x") + lax.axis_index("y")
    core = lax.axis_index("c")

    flight = {"chunk": _halves_start(_place_block(w_in[0], N_CHIPS, chip, BF16, "place_w_in0", 0, W_CHUNKS), x,
                                     "gather_w_in0_start")}
    first_token = flight["chunk"][2]
    w_small = small(w_proj_attn, w_proj_pool, w_out, w_pool) + first_token[0, 0]
    placed = [None] + [_place_block(w_in[0], N_CHIPS, chip, BF16, f"place_w_in{ch}", ch, W_CHUNKS, first_token)
                       for ch in range(1, W_CHUNKS)]
    placed_small = _place_block(w_small, N_CHIPS, chip, BF16, "place_w_small", after=first_token)

    def w_open(ch, after):
        sems, thru, _ = flight["chunk"]
        if ch == 0:
            after = after + placed[1:] + [placed_small]
        landed = _halves_wait(sems, thru, after, f"gather_w_in{ch}_wait")
        if ch + 1 < W_CHUNKS:
            flight["forward"], flight["chunk"] = _advance_start(landed, placed[ch + 1], False, f"advance_w_in{ch}")
        else:
            flight["forward"], flight["small"] = _advance_start(landed, placed_small, True, f"advance_w_in{ch}")
        flight["token"] = flight["forward"][2]
        return flight["token"]

    def w_close(ch, after):
        sems, thru, _ = flight["forward"]
        return _forward_wait(sems, thru, after, f"forward_w_in{ch}_wait"), flight["token"]

    def small_weights(after):
        sems, thru, _ = flight["small"]
        small4 = _broadcast_wait(sems, thru, after, "gather_small_wait")
        wpa4, wpp4, w_out4, w_pool4 = _unpack_small(small4, aw, pw, d, pg)
        return w_pool4.transpose(1, 0, 2, 3).reshape(n_win, pg, pg), wpa4, wpp4, w_out4.reshape(d, d)

    exchange = {}

    def start_exchange(g):
        g_pool4 = g["w_pool"].reshape(n_win, N_CHIPS, pg // N_CHIPS, pg).transpose(1, 0, 2, 3).astype(BF16)
        g_out4 = g["w_out"].reshape(N_CHIPS, d // N_CHIPS, d)
        g_small4 = jnp.concatenate([g["w_proj_attn"], g["w_proj_pool"], g_out4.reshape(N_CHIPS, -1, d // N_CHIPS),
                                    g_pool4.reshape(N_CHIPS, -1, d // N_CHIPS)], axis=1)
        theirs_big, theirs_small = _swap_halves([g["w_in"], g_small4])
        chip_big, placed_big = _add_halves(g["w_in"], theirs_big, core, chip, "add_cores_big")
        chip_small, placed_small = _add_halves(g_small4, theirs_small, core, chip, "add_cores_small")
        sems, sums, placed, token = _scatter_start([chip_big, chip_small], [placed_big, placed_small])
        exchange.update(sems=sems, sums=sums, placed=placed)
        return token

    g = _local_step(x[0], loss_target[0], w_open, w_close, w_in.shape[2], b_gate, pool_scale, ln_gamma, ln_beta,
                    aw, pw, small_weights, start_exchange, first_token)

    sizes = [b_gate.shape[1], pool_scale.shape[1], d, d, 1]
    rows = -(-sum(sizes) // (8 * 128)) * 8
    loss_part = (0.5 / d) * jnp.sum(g["loss_lanes"]).reshape(1, 1)
    parts = _gather_rows(_pack_rows([g["b_gate"], g["pool_scale"], g["ln_gamma"], g["ln_beta"], loss_part], rows))
    zero = jnp.zeros((1, 1), F32)
    packed = [_pack_rows(vs, rows) for vs in ([b_gate, pool_scale, ln_gamma, ln_beta, zero],
                                              [m_b_gate, m_pool_scale, m_ln_gamma, m_ln_beta, zero],
                                              [v_b_gate, v_pool_scale, v_ln_gamma, v_ln_beta, zero])]
    replicated = _sum_rows_adamw(parts, *packed)
    rep = [_unpack_rows(t, sizes) for t in replicated]
    loss = rep[0][4].reshape(())

    got_big, got_small = _scatter_wait(exchange["sems"], exchange["sums"], exchange["placed"],
                                       [g["d_x"], replicated[0]])
    join_sems, halves = _join_start([_sum_slots(got_big, core, "sum_chips_big"),
                                     _sum_slots(got_small, core, "sum_chips_small")])
    mv_small = (small(m_w_proj_attn, m_w_proj_pool, m_w_out, m_w_pool),
                small(v_w_proj_attn, v_w_proj_pool, v_w_out, v_w_pool))
    upd_in = _adamw_half(w_in[0], halves[0], m_w_in[0], v_w_in[0], core, None, "adamw_w_in_own")
    upd_small = _adamw_half(w_small, halves[1], *mv_small, core, None, "adamw_small_own")
    grad_w_in, grad_small = _join_wait(join_sems, halves, [upd_in[0], upd_small[0]])
    upd_in = _adamw_half(w_in[0], grad_w_in, m_w_in[0], v_w_in[0], 1 - core, upd_in, "adamw_w_in_other")
    upd_small = _adamw_half(w_small, grad_small, *mv_small, 1 - core, upd_small, "adamw_small_other")
    grad_w_in, grad_small = upd_in[3], upd_small[3]

    def leaves(big, packed_small, replicated):
        wpa_, wpp_, wo_, wpl_ = _unpack_small(packed_small, aw, pw, d, pg)
        return [big[None], replicated[0], wpl_[None], replicated[1], wpa_[None], wpp_[None], wo_[None],
                replicated[2], replicated[3]]

    out = [loss, g["d_x"][None]]
    out += leaves(grad_w_in, grad_small, rep[0])
    for i in range(3):
        out += leaves(upd_in[i], upd_small[i], rep[1 + i])
    return tuple(out)
```

```python
import math

import jax
import jax.numpy as jnp
from jax import lax
from jax.experimental import pallas as pl
from jax.experimental.pallas import tpu as pltpu

F32 = jnp.float32
BF16 = jnp.bfloat16
MESH = pl.DeviceIdType.MESH
ANY = pl.BlockSpec(memory_space=pl.ANY)

HEAD_DIM = 128
STEPS = 128
DILATIONS = (1, 4, 16)
N_GROUPS = len(DILATIONS)
POOL_WINDOWS = (2, 4, 8, 16)
POOL_HALO = 16
N_CHIPS = 4
N_DEV = 8
ALPHA = 2.0 ** 0.25
LN_EPS = 1e-5
NEG_INF = -1e30
SCORE_SCALE = HEAD_DIM ** -0.5
ADAM_LR = 0.001
ADAM_B1 = 0.9
ADAM_B2 = 0.999
ADAM_EPS = 1e-08
ADAM_WD = 0.01
ADAM_STEP = 10
MIB = 2 ** 20
NT = (((1,), (1,)), ((), ()))
TN = (((0,), (0,)), ((), ()))
DMA_STREAMS = 8


def _params(semantics=None, vmem_mib=48):
    return pltpu.CompilerParams(dimension_semantics=semantics, vmem_limit_bytes=vmem_mib * MIB)


def _divisor_tile(n, target, multiple):
    best = None
    for t in range(multiple, min(n, target) + 1, multiple):
        if n % t == 0:
            best = t
    assert best is not None, (n, target, multiple)
    return best


def _col_tile(*widths):
    g = 0
    for w in widths:
        g = math.gcd(g, w)
    return _divisor_tile(g, 1024, 128)


def _sigmoid(z):
    return jax.nn.sigmoid(z)


def _dsilu(z, sg):
    return sg * (1.0 + z * (1.0 - sg))


def _place():
    x, y, c = lax.axis_index("x"), lax.axis_index("y"), lax.axis_index("c")
    others = [(1 - x, y), (x, 1 - y), (1 - x, 1 - y)]
    return x, y, c, (x, y, 1 - c), others


def _remote(src, dst, send_sem, recv_sem, dev):
    return pltpu.make_async_remote_copy(src_ref=src, dst_ref=dst, send_sem=send_sem, recv_sem=recv_sem,
                                        device_id=dev, device_id_type=MESH)


def _row_pieces(n_rows, streams=DMA_STREAMS, multiple=16):
    size = -(-n_rows // (streams * multiple)) * multiple
    return [(lo, min(size, n_rows - lo)) for lo in range(0, n_rows, size)]


def _start_streams(make, n_rows):
    for lo, size in _row_pieces(n_rows):
        make(pl.ds(lo, size)).start()


def _half_copies(buf, send_sems, recv_sems):
    x, y, c, _, others = _place()
    half = buf.shape[1] // 2
    slab = buf.at[2 * x + y, pl.ds(c * half, half)]
    return [_remote(slab, slab, send_sems[j], recv_sems[j], (ox, oy, c)) for j, (ox, oy) in enumerate(others)]


def _halves_start(placed, after, name):
    k = N_CHIPS - 1

    def body(buf, after_ref, *refs):
        send_sems, recv_sems, token = refs[:k], refs[k:2 * k], refs[-1]
        for cp in _half_copies(buf, send_sems, recv_sems):
            cp.start()
        token[...] = jnp.zeros_like(token)

    out = pl.pallas_call(
        body, name=name,
        out_shape=[pltpu.SemaphoreType.DMA(())] * (2 * k) + [pltpu.HBM(placed.shape, placed.dtype),
                                                             jax.ShapeDtypeStruct((8, 128), F32)],
        in_specs=[HBM, ANY], out_specs=[SEM] * (2 * k) + [HBM, pl.BlockSpec(memory_space=pltpu.VMEM)],
        input_output_aliases={0: 2 * k},
        compiler_params=pltpu.CompilerParams(has_side_effects=DATAFLOW),
    )(pltpu.with_memory_space_constraint(placed, pltpu.HBM), after)
    return out[:2 * k], out[2 * k], out[-1]


def _halves_wait(sems, placed, after, name):
    k = N_CHIPS - 1

    def body(buf, *refs):
        send_sems, recv_sems = refs[:k], refs[k:2 * k]
        for cp in _half_copies(buf, send_sems, recv_sems):
            cp.wait_send()
            cp.wait_recv()

    return pl.pallas_call(
        body, name=name, out_shape=pltpu.HBM(placed.shape, placed.dtype),
        in_specs=[HBM] + [SEM] * (2 * k) + [ANY] * len(after), out_specs=HBM, input_output_aliases={0: 0},
        compiler_params=pltpu.CompilerParams(has_side_effects=DATAFLOW),
    )(placed, *sems, *after)


def _forward_copies(buf, send_sems, recv_sems):
    x, y, c, sibling, others = _place()
    half = buf.shape[1] // 2
    copies = []
    for j, (ox, oy) in enumerate(others):
        slab = buf.at[2 * ox + oy, pl.ds(c * half, half)]
        copies.append(_remote(slab, slab, send_sems[j], recv_sems[j], sibling))
    return copies


def _advance_start(landed, nxt, whole_blocks, name):
    k = N_CHIPS - 1

    def body(a, b, *refs):
        for cp in _forward_copies(a, refs[:k], refs[k:2 * k]):
            cp.start()
        for cp in (_broadcast_copies if whole_blocks else _half_copies)(b, refs[2 * k:3 * k], refs[3 * k:4 * k]):
            cp.start()
        refs[-1][...] = jnp.zeros_like(refs[-1])

    out = pl.pallas_call(
        body, name=name,
        out_shape=[pltpu.SemaphoreType.DMA(())] * (4 * k) + [pltpu.HBM(landed.shape, landed.dtype),
                                                             pltpu.HBM(nxt.shape, nxt.dtype),
                                                             jax.ShapeDtypeStruct((8, 128), F32)],
        in_specs=[HBM, HBM], out_specs=[SEM] * (4 * k) + [HBM, HBM, pl.BlockSpec(memory_space=pltpu.VMEM)],
        input_output_aliases={0: 4 * k, 1: 4 * k + 1},
        compiler_params=pltpu.CompilerParams(has_side_effects=DATAFLOW),
    )(pltpu.with_memory_space_constraint(landed, pltpu.HBM), pltpu.with_memory_space_constraint(nxt, pltpu.HBM))
    token = out[-1]
    return (out[:2 * k], out[4 * k], token), (out[2 * k:4 * k], out[4 * k + 1], token)


def _forward_wait(sems, buf, after, name):
    k = N_CHIPS - 1

    def body(b, *refs):
        send_sems, recv_sems = refs[:k], refs[k:2 * k]
        for cp in _forward_copies(b, send_sems, recv_sems):
            cp.wait_send()
            cp.wait_recv()

    return pl.pallas_call(
        body, name=name, out_shape=pltpu.HBM(buf.shape, buf.dtype),
        in_specs=[HBM] + [SEM] * (2 * k) + [ANY] * len(after), out_specs=HBM, input_output_aliases={0: 0},
        compiler_params=pltpu.CompilerParams(has_side_effects=DATAFLOW),
    )(buf, *sems, *after)


def _swap_halves(grads):
    n = len(grads)

    def body(*refs):
        g, theirs = refs[:n], refs[n:2 * n]
        send_sems, recv_sems = refs[2 * n:]
        x, y, c, sibling, _ = _place()
        for i in range(n):
            half = g[i].shape[1] // 2
            give = (1 - c) * half
            for b in range(N_CHIPS):
                _start_streams(lambda r, i=i, b=b: _remote(
                    g[i].at[b, pl.ds(give + r.start, r.size)], theirs[i].at[b, r], send_sems.at[i], recv_sems.at[i],
                    sibling), half)
        for i in range(n):
            _remote(theirs[i], theirs[i], send_sems.at[i], recv_sems.at[i], sibling).wait()

    return pl.pallas_call(
        body, name="swap_halves",
        out_shape=[jax.ShapeDtypeStruct((s.shape[0], s.shape[1] // 2) + s.shape[2:], s.dtype) for s in grads],
        in_specs=[ANY] * n, out_specs=[ANY] * n,
        scratch_shapes=[pltpu.SemaphoreType.DMA((n,)), pltpu.SemaphoreType.DMA((n,))],
    )(*grads)


HBM = pl.BlockSpec(memory_space=pltpu.HBM)
SEM = pl.BlockSpec(memory_space=pltpu.SEMAPHORE)
DATAFLOW = pltpu.SideEffectType.DATAFLOW_SIDE_EFFECTING


def _broadcast_copies(buf, send_sems, recv_sems):
    x, y, c, _, others = _place()
    mine = buf.at[2 * x + y]
    return [_remote(mine, mine, send_sems[j], recv_sems[j], (ox, oy, c)) for j, (ox, oy) in enumerate(others)]


def _broadcast_wait(sems, placed, after, name):
    k = N_CHIPS - 1

    def body(buf, *refs):
        send_sems, recv_sems = refs[:k], refs[k:2 * k]
        for cp in _broadcast_copies(buf, send_sems, recv_sems):
            cp.wait_send()
            cp.wait_recv()

    return pl.pallas_call(
        body, name=name, out_shape=pltpu.HBM(placed.shape, placed.dtype),
        in_specs=[HBM] + [SEM] * (2 * k) + [ANY], out_specs=HBM, input_output_aliases={0: 0},
        compiler_params=pltpu.CompilerParams(has_side_effects=DATAFLOW),
    )(placed, *sems, after)


def _scatter_copies(s, got, send_sems, recv_sems):
    x, y, c, _, others = _place()
    me = 2 * x + y
    n = len(s)
    return [_remote(s[i].at[2 * ox + oy], got[i].at[me], send_sems[3 * i + j], recv_sems[3 * i + j], (ox, oy, c))
            for i in range(n) for j, (ox, oy) in enumerate(others)]


def _scatter_start(sums, placed):
    n = len(sums)
    k = 3 * n

    def body(*refs):
        s, got, token = refs[:n], refs[n:2 * n], refs[-1]
        send_sems, recv_sems = refs[2 * n:2 * n + k], refs[2 * n + k:2 * n + 2 * k]
        for cp in _scatter_copies(s, got, send_sems, recv_sems):
            cp.start()
        token[...] = jnp.zeros_like(token)

    hbm = [pltpu.HBM(a.shape, a.dtype) for a in list(sums) + list(placed)]
    out = pl.pallas_call(
        body, name="scatter_start",
        out_shape=[pltpu.SemaphoreType.DMA(())] * (2 * k) + hbm + [jax.ShapeDtypeStruct((8, 128), F32)],
        in_specs=[HBM] * (2 * n), out_specs=[SEM] * (2 * k) + [HBM] * (2 * n) + [pl.BlockSpec(memory_space=pltpu.VMEM)],
        input_output_aliases={i: 2 * k + i for i in range(2 * n)},
        compiler_params=pltpu.CompilerParams(has_side_effects=DATAFLOW),
    )(*[pltpu.with_memory_space_constraint(a, pltpu.HBM) for a in list(sums) + list(placed)])
    return out[:2 * k], out[2 * k:2 * k + n], out[2 * k + n:2 * k + 2 * n], out[-1]


def _scatter_wait(sems, sums, placed, after):
    n = len(sums)
    k = 3 * n

    def body(*refs):
        s, got = refs[:n], refs[n:2 * n]
        send_sems, recv_sems = refs[2 * n:2 * n + k], refs[2 * n + k:2 * n + 2 * k]
        for cp in _scatter_copies(s, got, send_sems, recv_sems):
            cp.wait_send()
            cp.wait_recv()

    hbm = [pltpu.HBM(a.shape, a.dtype) for a in list(sums) + list(placed)]
    out = pl.pallas_call(
        body, name="scatter_wait", out_shape=hbm,
        in_specs=[HBM] * (2 * n) + [SEM] * (2 * k) + [ANY] * len(after), out_specs=[HBM] * (2 * n),
        input_output_aliases={i: i for i in range(2 * n)},
        compiler_params=pltpu.CompilerParams(has_side_effects=DATAFLOW),
    )(*sums, *placed, *sems, *after)
    return out[n:]


def _join_copies(bufs, send_sems, recv_sems):
    x, y, c, sibling, _ = _place()
    return [_remote(b.at[c], b.at[c], send_sems[i], recv_sems[i], sibling) for i, b in enumerate(bufs)]


def _join_start(placed):
    n = len(placed)

    def body(*refs):
        bufs, send_sems, recv_sems = refs[:n], refs[n:2 * n], refs[2 * n:3 * n]
        for cp in _join_copies(bufs, send_sems, recv_sems):
            cp.start()

    hbm = [pltpu.HBM(a.shape, a.dtype) for a in placed]
    out = pl.pallas_call(
        body, name="join_start", out_shape=[pltpu.SemaphoreType.DMA(())] * (2 * n) + hbm,
        in_specs=[HBM] * n, out_specs=[SEM] * (2 * n) + [HBM] * n,
        input_output_aliases={i: 2 * n + i for i in range(n)},
        compiler_params=pltpu.CompilerParams(has_side_effects=DATAFLOW),
    )(*[pltpu.with_memory_space_constraint(a, pltpu.HBM) for a in placed])
    return out[:2 * n], out[2 * n:]


def _join_wait(sems, placed, after):
    n = len(placed)

    def body(*refs):
        bufs, send_sems, recv_sems = refs[:n], refs[n:2 * n], refs[2 * n:3 * n]
        for cp in _join_copies(bufs, send_sems, recv_sems):
            cp.wait_send()
            cp.wait_recv()

    return pl.pallas_call(
        body, name="join_wait", out_shape=[pltpu.HBM(a.shape, a.dtype) for a in placed],
        in_specs=[HBM] * n + [SEM] * (2 * n) + [ANY] * len(after), out_specs=[HBM] * n,
        input_output_aliases={i: i for i in range(n)},
        compiler_params=pltpu.CompilerParams(has_side_effects=DATAFLOW),
    )(*placed, *sems, *after)


def _gather_rows(row):
    def body(row_ref, out_ref, send_sems, recv_sems, local_sem):
        x, y, c = lax.axis_index("x"), lax.axis_index("y"), lax.axis_index("c")
        me = 4 * x + 2 * y + c
        local = pltpu.make_async_copy(row_ref, out_ref.at[me], local_sem)
        local.start()
        sent = []
        peers = []
        for k in range(1, N_DEV):
            px, py, pc = x ^ (k >> 2), y ^ ((k >> 1) & 1), c ^ (k & 1)
            peers.append((k, px, py, pc))
            cp = _remote(row_ref, out_ref.at[me], send_sems.at[k - 1], recv_sems.at[k - 1], (px, py, pc))
            cp.start()
            sent.append(cp)
        for k, px, py, pc in peers:
            slot = out_ref.at[4 * px + 2 * py + pc]
            _remote(slot, slot, send_sems.at[k - 1], recv_sems.at[k - 1], (px, py, pc)).wait_recv()
        for cp in sent:
            cp.wait_send()
        local.wait()

    return pl.pallas_call(
        body, name="gather_rows", out_shape=jax.ShapeDtypeStruct((N_DEV,) + row.shape, row.dtype),
        in_specs=[ANY], out_specs=ANY,
        scratch_shapes=[pltpu.SemaphoreType.DMA((N_DEV - 1,)), pltpu.SemaphoreType.DMA((N_DEV - 1,)),
                        pltpu.SemaphoreType.DMA],
    )(row)


def _scalar(i):
    return jnp.reshape(i, (1,)).astype(jnp.int32)


def _place_block(src, n_slots, slot, out_dtype, name, window=0, n_windows=1, after=None):
    rows, cols = src.shape[0], src.shape[1] // n_windows
    tr = _divisor_tile(rows, max(16, (2 * MIB) // (cols * 4)), 16)
    ordered = [] if after is None else [after]

    def body(slot_ref, *refs):
        s_ref, o_ref = refs[len(ordered):]
        o_ref[...] = s_ref[...].astype(o_ref.dtype)

    return pl.pallas_call(
        body, name=name, out_shape=jax.ShapeDtypeStruct((n_slots, rows, cols), out_dtype),
        grid_spec=pltpu.PrefetchScalarGridSpec(
            num_scalar_prefetch=1, grid=(rows // tr,),
            in_specs=[pl.BlockSpec(t.shape, lambda r, sl: (0, 0)) for t in ordered]
            + [pl.BlockSpec((tr, cols), lambda r, sl: (r, window))],
            out_specs=pl.BlockSpec((None, tr, cols), lambda r, sl: (sl[0], r, 0))),
        compiler_params=_params(("parallel",)))(_scalar(slot), *ordered, src)


def _add_halves(g, theirs, core, chip, name):
    n, half, cols = theirs.shape
    tr = _divisor_tile(half, max(16, (4 * MIB) // (cols * 4)), 16)
    per = half // tr

    def body(at_ref, a_ref, b_ref, o_ref, own_ref):
        total = (a_ref[...].astype(F32) + b_ref[...].astype(F32)).astype(o_ref.dtype)
        o_ref[...] = total

        @pl.when(pl.program_id(1) == at_ref[1])
        def _():
            own_ref[...] = total

    spec = pl.BlockSpec((None, tr, cols), lambda r, i, at: (i, r, 0))
    shape = jax.ShapeDtypeStruct(theirs.shape, BF16)
    return pl.pallas_call(
        body, name=name, out_shape=[shape, shape],
        grid_spec=pltpu.PrefetchScalarGridSpec(
            num_scalar_prefetch=1, grid=(per, n),
            in_specs=[pl.BlockSpec((None, tr, cols), lambda r, i, at: (i, at[0] * per + r, 0)), spec],
            out_specs=[spec, pl.BlockSpec((None, tr, cols), lambda r, i, at: (at[1], r, 0))]),
        compiler_params=_params(("parallel", "arbitrary")))(jnp.concatenate([_scalar(core), _scalar(chip)]), g, theirs)


def _sum_slots(a, core, name):
    n, rows, cols = a.shape
    tr = _divisor_tile(rows, max(16, (8 * MIB) // (cols * 4 * n)), 16)

    def body(c_ref, a_ref, o_ref):
        acc = a_ref[0].astype(F32)
        for i in range(1, n):
            acc = acc + a_ref[i].astype(F32)
        o_ref[...] = acc

    return pl.pallas_call(
        body, name=name, out_shape=jax.ShapeDtypeStruct((2, rows, cols), F32),
        grid_spec=pltpu.PrefetchScalarGridSpec(
            num_scalar_prefetch=1, grid=(rows // tr,),
            in_specs=[pl.BlockSpec((n, tr, cols), lambda r, c: (0, r, 0))],
            out_specs=pl.BlockSpec((None, tr, cols), lambda r, c: (c[0], r, 0))),
        compiler_params=_params(("parallel",)))(_scalar(core), a)


def _adamw_math(w, g, m, v):
    m = ADAM_B1 * m + (1.0 - ADAM_B1) * g
    v = ADAM_B2 * v + (1.0 - ADAM_B2) * (g * g)
    m_hat = m / (1.0 - ADAM_B1 ** ADAM_STEP)
    v_hat = v / (1.0 - ADAM_B2 ** ADAM_STEP)
    delta = -ADAM_LR * (m_hat / (jnp.sqrt(v_hat) + ADAM_EPS) + ADAM_WD * w)
    return delta, m, v


def _adamw_half(w, g2, m, v, which, prev, name):
    rows, cols = w.shape
    half = rows // 2
    tr = _divisor_tile(half, max(8, (2 * MIB) // (cols * 4)), 8)
    per = half // tr
    n_out = 4

    def body(h_ref, w_ref, g_ref, m_ref, v_ref, *refs):
        d_ref, nm_ref, nv_ref, go_ref = refs[-n_out:]
        g = g_ref[...]
        d, nm, nv = _adamw_math(w_ref[...], g, m_ref[...], v_ref[...])
        d_ref[...] = d
        nm_ref[...] = nm
        nv_ref[...] = nv
        go_ref[...] = g

    spec = pl.BlockSpec((tr, cols), lambda r, h: (h[0] * per + r, 0))
    in_specs = [spec, pl.BlockSpec((None, tr, cols), lambda r, h: (h[0], r, 0)), spec, spec]
    args = [_scalar(which), w, g2, m, v]
    aliases = {}
    if prev is not None:
        aliases = {len(args) + i: i for i in range(n_out)}
        in_specs += [ANY] * n_out
        args += list(prev)
    return pl.pallas_call(
        body, name=name, out_shape=[jax.ShapeDtypeStruct((rows, cols), F32)] * n_out,
        grid_spec=pltpu.PrefetchScalarGridSpec(num_scalar_prefetch=1, grid=(per,), in_specs=in_specs,
                                               out_specs=[spec] * n_out),
        input_output_aliases=aliases, compiler_params=_params(("parallel",)))(*args)


def _sum_rows_adamw(parts, w, m, v):
    def body(p_ref, w_ref, m_ref, v_ref, g_ref, d_ref, nm_ref, nv_ref):
        g = p_ref[0]
        for i in range(1, N_DEV):
            g = g + p_ref[i]
        d, nm, nv = _adamw_math(w_ref[...], g, m_ref[...], v_ref[...])
        g_ref[...] = g
        d_ref[...] = d
        nm_ref[...] = nm
        nv_ref[...] = nv

    shape = jax.ShapeDtypeStruct(w.shape, F32)
    return pl.pallas_call(body, name="sum_rows_adamw", out_shape=[shape] * 4)(parts, w, m, v)


LANES = 128


def _permute_scratch(rows, width):
    return pltpu.VMEM((width // LANES, rows, LANES), F32)


def _split_rows(value, scratch, dil):
    if dil == 1:
        return [value]
    rows = value.shape[0] // dil
    slabs = value.shape[1] // LANES
    for c in range(slabs):
        scratch[c] = value[:, c * LANES:(c + 1) * LANES]
    return [jnp.concatenate([scratch[c, pl.ds(r, rows, stride=dil), :] for c in range(slabs)], axis=1)
            for r in range(dil)]


def _merge_rows(ref, scratch, dil):
    if dil == 1:
        return ref[0].astype(F32)
    rows = ref.shape[1]
    slabs = ref.shape[2] // LANES
    for r in range(dil):
        part = ref[r].astype(F32)
        for c in range(slabs):
            scratch[c, pl.ds(r, rows, stride=dil), :] = part[:, c * LANES:(c + 1) * LANES]
    return jnp.concatenate([scratch[c] for c in range(slabs)], axis=1)


def _grouped_view(t, dil):
    return t.reshape(dil, t.shape[0] // dil, t.shape[1])


def _grouped_spec(dil, rows, width, index):
    return pl.BlockSpec((dil, rows // dil, width), index)


W_CHUNKS = 4


def _pick(values, j):
    out = values[-1]
    for i in range(len(values) - 2, -1, -1):
        out = jnp.where(j == i, values[i], out)
    return out


def _chunk_of(col, per_chip):
    return (col % per_chip) // (per_chip // W_CHUNKS)


def _w_block(col, per_chip):
    return col // per_chip, 0, (col % per_chip) % (per_chip // W_CHUNKS)


def _in_proj(xb, wc, blocks, j0, ncols, tn, out_dtype, prev, after, name):
    s, d = xb.shape
    per_chip = wc.shape[2] * W_CHUNKS // tn
    tm = _divisor_tile(s, 1024, 16)
    extra = [t for t in after if t is not None]

    def body(*refs):
        a_ref, b_ref = refs[len(extra):len(extra) + 2]
        o_ref = refs[-1]
        o_ref[...] = jnp.dot(a_ref[...], b_ref[...], preferred_element_type=F32).astype(o_ref.dtype)

    in_specs = [pl.BlockSpec(t.shape, lambda j, m: (0, 0)) for t in extra] + [
        pl.BlockSpec((tm, d), lambda j, m: (m, 0)),
        pl.BlockSpec((None, d, tn), lambda j, m: _w_block(_pick(blocks, j), per_chip))]
    args = extra + [xb, wc]
    aliases = {}
    if prev is not None:
        aliases = {len(args): 0}
        in_specs.append(ANY)
        args.append(prev)
    return pl.pallas_call(
        body, name=name, grid=(len(blocks), s // tm), in_specs=in_specs,
        out_specs=pl.BlockSpec((tm, tn), lambda j, m: (m, _pick(blocks, j) - j0)),
        out_shape=jax.ShapeDtypeStruct((s, ncols), out_dtype), input_output_aliases=aliases,
        compiler_params=_params(("parallel", "parallel")))(*args)


def _in_proj_qkv(xb, wc, g, blocks, aw, tn, prev, after, name):
    s, d = xb.shape
    dil = DILATIONS[g]
    per_chip = wc.shape[2] * W_CHUNKS // tn
    sub = aw // tn
    tm = _divisor_tile(s, 1024, 16 * dil)
    extra = [t for t in after if t is not None]

    def body(*refs):
        a_ref, b_ref = refs[len(extra):len(extra) + 2]
        o_ref, scratch = refs[-2:]
        res = jnp.dot(a_ref[...], b_ref[...], preferred_element_type=F32)
        for r, part in enumerate(_split_rows(res, scratch, dil)):
            o_ref[r] = part.astype(BF16)

    def out_index(j, m):
        col = _pick(blocks, j)
        return (col // sub) // N_GROUPS, 0, m, col % sub

    in_specs = [pl.BlockSpec(t.shape, lambda j, m: (0, 0)) for t in extra] + [
        pl.BlockSpec((tm, d), lambda j, m: (m, 0)),
        pl.BlockSpec((None, d, tn), lambda j, m: _w_block(_pick(blocks, j), per_chip))]
    args = extra + [xb, wc]
    aliases = {}
    if prev is not None:
        aliases = {len(args): 0}
        in_specs.append(ANY)
        args.append(prev)
    return pl.pallas_call(
        body, name=name, grid=(len(blocks), s // tm), in_specs=in_specs,
        out_specs=pl.BlockSpec((None, dil, tm // dil, tn), out_index),
        out_shape=jax.ShapeDtypeStruct((3, dil, s // dil, aw), BF16), input_output_aliases=aliases,
        scratch_shapes=[_permute_scratch(tm, tn)],
        compiler_params=_params(("parallel", "parallel")))(*args)


def _window_mask(first):
    qi = lax.broadcasted_iota(jnp.int32, (STEPS, 2 * STEPS), 0)
    kj = lax.broadcasted_iota(jnp.int32, (STEPS, 2 * STEPS), 1)
    lowest = jnp.where(first, STEPS, 0)
    return (kj >= qi) & (kj <= qi + STEPS) & (kj >= lowest)


def _attn_fwd(qkv, g):
    _, s, aw = qkv.shape
    heads = aw // HEAD_DIM
    n_blocks = s // STEPS
    per_seq = n_blocks // DILATIONS[g]
    pair = 4 if n_blocks % 4 == 0 else 1

    def body(q_ref, kc_ref, kp_ref, vc_ref, vp_ref, o_ref, l_ref):
        masks = [_window_mask(lax.rem(pl.program_id(0) * pair + j, per_seq) == 0) for j in range(pair)]
        for h in range(heads):
            hs = slice(h * HEAD_DIM, (h + 1) * HEAD_DIM)
            keys = jnp.concatenate([kp_ref[:, hs], kc_ref[:, hs]], axis=0)
            values = jnp.concatenate([vp_ref[:, hs], vc_ref[:, hs]], axis=0)
            for j in range(pair):
                rows = slice(j * STEPS, (j + 1) * STEPS)
                window = slice(j * STEPS, (j + 2) * STEPS)
                sc = lax.dot_general(q_ref[rows, hs], keys[window], NT, preferred_element_type=F32) * SCORE_SCALE
                sc = jnp.where(masks[j], sc, NEG_INF)
                mx = jnp.max(sc, axis=1, keepdims=True)
                e = jnp.exp(sc - mx)
                den = jnp.sum(e, axis=1, keepdims=True)
                o_ref[rows, hs] = (jnp.dot(e.astype(BF16), values[window], preferred_element_type=F32)
                                   / den).astype(BF16)
                l_ref[rows, hs] = jnp.broadcast_to(mx + jnp.log(den), (STEPS, HEAD_DIM))

    def cur(which):
        return pl.BlockSpec((None, pair * STEPS, aw), lambda b: (which, b, 0))

    def prev(which):
        return pl.BlockSpec((None, STEPS, aw), lambda b: (which, jnp.maximum(pair * b - 1, 0), 0))

    out = pl.BlockSpec((pair * STEPS, aw), lambda b: (b, 0))
    return pl.pallas_call(
        body, name=f"attn_fwd{g}", grid=(n_blocks // pair,),
        in_specs=[cur(0), cur(1), prev(1), cur(2), prev(2)], out_specs=[out, out],
        out_shape=[jax.ShapeDtypeStruct((s, aw), BF16), jax.ShapeDtypeStruct((s, aw), F32)],
        compiler_params=_params(("parallel",)))(qkv, qkv, qkv, qkv, qkv)


def _combine_groups(os, ls, zuz, aw):
    s = zuz.shape[0]
    tr = _divisor_tile(s, 256, 8 * DILATIONS[-1])

    def body(*refs):
        o_refs, l_refs, z_ref = refs[0:3], refs[3:6], refs[6]
        oo_ref, y_ref, yt_ref = refs[7:10]
        lq_refs, scratch = refs[10:13], refs[13]
        ls_ = [_merge_rows(l_refs[g], scratch, dil) for g, dil in enumerate(DILATIONS)]
        mx = jnp.maximum(jnp.maximum(ls_[0], ls_[1]), ls_[2])
        ws = [jnp.exp(l - mx) for l in ls_]
        den = ws[0] + ws[1] + ws[2]
        o = ws[0] * _merge_rows(o_refs[0], scratch, DILATIONS[0])
        for g in range(1, N_GROUPS):
            o = o + ws[g] * _merge_rows(o_refs[g], scratch, DILATIONS[g])
        o = o / den
        z = z_ref[...].astype(F32)
        y = o * (z * _sigmoid(z))
        oo_ref[...] = o.astype(BF16)
        y_ref[...] = y.astype(BF16)
        yt_ref[...] = y.T.astype(BF16)
        for g, dil in enumerate(DILATIONS):
            for r, part in enumerate(_split_rows(mx + jnp.log(den), scratch, dil)):
                lq_refs[g][r] = part

    grouped = [_grouped_spec(dil, tr, aw, lambda r: (0, r, 0)) for dil in DILATIONS]
    one = pl.BlockSpec((tr, aw), lambda r: (r, 0))
    b16 = jax.ShapeDtypeStruct((s, aw), BF16)
    out = pl.pallas_call(
        body, name="combine_groups", grid=(s // tr,),
        in_specs=grouped + grouped + [one],
        out_specs=[one, one, pl.BlockSpec((aw, tr), lambda r: (0, r))] + grouped,
        out_shape=[b16, b16, jax.ShapeDtypeStruct((aw, s), BF16)]
        + [jax.ShapeDtypeStruct((dil, s // dil, aw), F32) for dil in DILATIONS],
        scratch_shapes=[_permute_scratch(tr, aw)],
        compiler_params=_params(("parallel",)))(
            *[_grouped_view(t, dil) for t, dil in zip(os, DILATIONS)],
            *[_grouped_view(t, dil) for t, dil in zip(ls, DILATIONS)], zuz)
    return out[0], out[1], out[2], [t.reshape(s, aw) for t in out[3:]]


def _pool_counts(row0, rows, window):
    t = row0 + lax.broadcasted_iota(jnp.int32, (rows, 1), 0)
    return jnp.minimum(t + 1, window).astype(F32)


def _pool_fwd(zuz, w_pool, pool_scale, aw, pw):
    s = zuz.shape[0]
    pg = pw // len(POOL_WINDOWS)
    tr = _divisor_tile(s, 512, 128)
    u_col, z_col = aw // pw, aw // pw + 1
    assert aw % pw == 0

    def body(u_ref, up_ref, z_ref, w_ref, sc_ref, p_ref, l_ref, y_ref, yt_ref):
        r = pl.program_id(0)
        u = u_ref[...].astype(F32)
        halo = jnp.where(r > 0, up_ref[...].astype(F32), 0.0)
        ext = jnp.concatenate([halo, u], axis=0)
        pieces, lins = [], []
        for gi, window in enumerate(POOL_WINDOWS):
            cs = slice(gi * pg, (gi + 1) * pg)
            acc = ext[:, cs]
            shift = 1
            while shift < window:
                acc = acc + pltpu.roll(acc, shift, 0)
                shift *= 2
            p = acc[POOL_HALO:] / _pool_counts(r * tr, tr, window) - u[:, cs]
            pieces.append(p)
            lins.append(jnp.dot(p.astype(BF16), w_ref[gi], preferred_element_type=F32))
        p = jnp.concatenate(pieces, axis=1)
        lin = jnp.concatenate(lins, axis=1)
        z = z_ref[...].astype(F32)
        y = lin * sc_ref[...] * (z * _sigmoid(z))
        p_ref[...] = p.astype(BF16)
        l_ref[...] = lin
        y_ref[...] = y.astype(BF16)
        yt_ref[...] = y.T.astype(BF16)

    per = tr // POOL_HALO
    out = pl.BlockSpec((tr, pw), lambda r: (r, 0))
    return pl.pallas_call(
        body, name="pool_fwd", grid=(s // tr,),
        in_specs=[pl.BlockSpec((tr, pw), lambda r: (r, u_col)),
                  pl.BlockSpec((POOL_HALO, pw), lambda r: (jnp.maximum(r * per - 1, 0), u_col)),
                  pl.BlockSpec((tr, pw), lambda r: (r, z_col)),
                  pl.BlockSpec((len(POOL_WINDOWS), pg, pg), lambda r: (0, 0, 0)),
                  pl.BlockSpec((1, pw), lambda r: (0, 0))],
        out_specs=[out, out, out, pl.BlockSpec((pw, tr), lambda r: (0, r))],
        out_shape=[jax.ShapeDtypeStruct((s, pw), BF16), jax.ShapeDtypeStruct((s, pw), F32),
                   jax.ShapeDtypeStruct((s, pw), BF16), jax.ShapeDtypeStruct((pw, s), BF16)],
        compiler_params=_params(("parallel",)))(zuz, zuz, zuz, w_pool, pool_scale)


def _proj_merge(y_attn, y_pool, wpa4, wpp4, gpre, b_gate):
    s, aw = y_attn.shape
    pw = y_pool.shape[1]
    tn = wpa4.shape[2]
    d = N_CHIPS * tn
    tm = _divisor_tile(s, 1024, 128)

    def body(ya_ref, yp_ref, wa_ref, wp_ref, ga_ref, gp_ref, ba_ref, bp_ref, a_ref, p_ref, sa_ref, sp_ref, m_ref,
             mt_ref):
        a = jnp.dot(ya_ref[...], wa_ref[...], preferred_element_type=F32)
        p = jnp.dot(yp_ref[...], wp_ref[...], preferred_element_type=F32)
        sa = _sigmoid(ga_ref[...].astype(F32) + ba_ref[...])
        sp = _sigmoid(gp_ref[...].astype(F32) + bp_ref[...])
        merged = sa * a + sp * p
        a_ref[...] = a.astype(BF16)
        p_ref[...] = p.astype(BF16)
        sa_ref[...] = sa.astype(BF16)
        sp_ref[...] = sp.astype(BF16)
        m_ref[...] = merged.astype(BF16)
        mt_ref[...] = merged.T.astype(BF16)

    out = pl.BlockSpec((tm, tn), lambda n, m: (m, n))
    f = jax.ShapeDtypeStruct((s, d), BF16)
    return pl.pallas_call(
        body, name="proj_merge", grid=(N_CHIPS, s // tm),
        in_specs=[pl.BlockSpec((tm, aw), lambda n, m: (m, 0)), pl.BlockSpec((tm, pw), lambda n, m: (m, 0)),
                  pl.BlockSpec((None, aw, tn), lambda n, m: (n, 0, 0)),
                  pl.BlockSpec((None, pw, tn), lambda n, m: (n, 0, 0)),
                  pl.BlockSpec((tm, tn), lambda n, m: (m, n)), pl.BlockSpec((tm, tn), lambda n, m: (m, N_CHIPS + n)),
                  pl.BlockSpec((1, tn), lambda n, m: (0, n)), pl.BlockSpec((1, tn), lambda n, m: (0, N_CHIPS + n))],
        out_specs=[out] * 5 + [pl.BlockSpec((tn, tm), lambda n, m: (n, m))],
        out_shape=[f] * 5 + [jax.ShapeDtypeStruct((d, s), BF16)],
        compiler_params=_params(("parallel", "parallel")))(y_attn, y_pool, wpa4, wpp4, gpre, gpre, b_gate, b_gate)


def _out_norm_loss(merged, w_out, x, target, gamma, beta):
    s, d = x.shape
    tm = _divisor_tile(s, 256, 16)

    def body(m_ref, w_ref, x_ref, t_ref, g_ref, b_ref, dr_ref, drb_ref, loss_ref, dg_ref, db_ref):
        @pl.when(pl.program_id(0) == 0)
        def _():
            loss_ref[...] = jnp.zeros_like(loss_ref)
            dg_ref[...] = jnp.zeros_like(dg_ref)
            db_ref[...] = jnp.zeros_like(db_ref)

        r = ALPHA * x_ref[...] + jnp.dot(m_ref[...], w_ref[...], preferred_element_type=F32)
        mu = jnp.mean(r, axis=1, keepdims=True)
        rc = r - mu
        rstd = lax.rsqrt(jnp.mean(rc * rc, axis=1, keepdims=True) + LN_EPS)
        xhat = rc * rstd
        diff = xhat * g_ref[...] + b_ref[...] - t_ref[...]
        dy = diff / d
        loss_ref[...] += jnp.sum(diff * diff, axis=0, keepdims=True)
        dg_ref[...] += jnp.sum(dy * xhat, axis=0, keepdims=True)
        db_ref[...] += jnp.sum(dy, axis=0, keepdims=True)
        dxhat = dy * g_ref[...]
        dr = rstd * (dxhat - jnp.mean(dxhat, axis=1, keepdims=True)
                     - xhat * jnp.mean(dxhat * xhat, axis=1, keepdims=True))
        dr_ref[...] = dr
        drb_ref[...] = dr.astype(BF16)

    row = pl.BlockSpec((tm, d), lambda m: (m, 0))
    vec = pl.BlockSpec((1, d), lambda m: (0, 0))
    v = jax.ShapeDtypeStruct((1, d), F32)
    return pl.pallas_call(
        body, name="out_norm_loss", grid=(s // tm,),
        in_specs=[row, pl.BlockSpec((d, d), lambda m: (0, 0)), row, row, vec, vec],
        out_specs=[row, row, vec, vec, vec],
        out_shape=[jax.ShapeDtypeStruct((s, d), F32), jax.ShapeDtypeStruct((s, d), BF16), v, v, v],
        compiler_params=_params(("arbitrary",), vmem_mib=56))(merged, w_out, x, target, gamma, beta)


def _merge_bwd(drb, w_out, a, p, sa, sp):
    s, d = drb.shape
    tm = _divisor_tile(s, 1024, 16)
    tn = d // N_CHIPS

    def body(dr_ref, w_ref, a_ref, p_ref, sa_ref, sp_ref, da_ref, dp_ref, dga_ref, dgp_ref, dba_ref, dbp_ref):
        @pl.when(pl.program_id(1) == 0)
        def _():
            dba_ref[...] = jnp.zeros_like(dba_ref)
            dbp_ref[...] = jnp.zeros_like(dbp_ref)

        dm = lax.dot_general(dr_ref[...], w_ref[...], NT, preferred_element_type=F32)
        sa = sa_ref[...].astype(F32)
        sp = sp_ref[...].astype(F32)
        da_ref[...] = (dm * sa).astype(BF16)
        dp_ref[...] = (dm * sp).astype(BF16)
        dga = dm * a_ref[...].astype(F32) * sa * (1.0 - sa)
        dgp = dm * p_ref[...].astype(F32) * sp * (1.0 - sp)
        dga_ref[...] = dga.astype(BF16)
        dgp_ref[...] = dgp.astype(BF16)
        dba_ref[...] += jnp.sum(dga, axis=0, keepdims=True)
        dbp_ref[...] += jnp.sum(dgp, axis=0, keepdims=True)

    blk = pl.BlockSpec((tm, tn), lambda n, m: (m, n))
    vec = pl.BlockSpec((1, tn), lambda n, m: (0, n))
    b16 = jax.ShapeDtypeStruct((s, d), BF16)
    v = jax.ShapeDtypeStruct((1, d), F32)
    return pl.pallas_call(
        body, name="merge_bwd", grid=(N_CHIPS, s // tm),
        in_specs=[pl.BlockSpec((tm, d), lambda n, m: (m, 0)), pl.BlockSpec((tn, d), lambda n, m: (n, 0)),
                  blk, blk, blk, blk],
        out_specs=[blk, blk, blk, blk, vec, vec], out_shape=[b16, b16, b16, b16, v, v],
        compiler_params=_params(("parallel", "arbitrary")))(drb, w_out, a, p, sa, sp)


def _proj_t(dy_ref, w_ref, tn):
    acc = None
    for n in range(N_CHIPS):
        t = lax.dot_general(dy_ref[:, n * tn:(n + 1) * tn], w_ref[n], NT, preferred_element_type=F32)
        acc = t if acc is None else acc + t
    return acc


def _attn_gate_bwd(da, wpa4, zuz, o):
    s, d = da.shape
    aw, tn = wpa4.shape[1], wpa4.shape[2]
    heads = aw // HEAD_DIM
    tm = _divisor_tile(s, 256, 16 * DILATIONS[-1])

    def body(*refs):
        da_ref, w_ref, z_ref, o_ref, dz_ref = refs[:5]
        do_refs, dd_refs, scratch = refs[5:8], refs[8:11], refs[11]
        dy = _proj_t(da_ref, w_ref, tn)
        z, o = z_ref[...].astype(F32), o_ref[...].astype(F32)
        sg = _sigmoid(z)
        do = dy * (z * sg)
        dz_ref[...] = (dy * o * _dsilu(z, sg)).astype(BF16)
        prod = do * o
        dd = jnp.concatenate(
            [jnp.broadcast_to(jnp.sum(prod[:, h * HEAD_DIM:(h + 1) * HEAD_DIM], axis=1, keepdims=True),
                              (tm, HEAD_DIM)) for h in range(heads)], axis=1)
        for g, dil in enumerate(DILATIONS):
            for r, part in enumerate(_split_rows(do, scratch, dil)):
                do_refs[g][r] = part.astype(BF16)
            for r, part in enumerate(_split_rows(dd, scratch, dil)):
                dd_refs[g][r] = part

    row = pl.BlockSpec((tm, aw), lambda m: (m, 0))
    grouped = [_grouped_spec(dil, tm, aw, lambda m: (0, m, 0)) for dil in DILATIONS]
    out = pl.pallas_call(
        body, name="attn_gate_bwd", grid=(s // tm,),
        in_specs=[pl.BlockSpec((tm, d), lambda m: (m, 0)), pl.BlockSpec((N_CHIPS, aw, tn), lambda m: (0, 0, 0)),
                  row, row],
        out_specs=[row] + grouped + grouped,
        out_shape=[jax.ShapeDtypeStruct((s, aw), BF16)]
        + [jax.ShapeDtypeStruct((dil, s // dil, aw), BF16) for dil in DILATIONS]
        + [jax.ShapeDtypeStruct((dil, s // dil, aw), F32) for dil in DILATIONS],
        scratch_shapes=[_permute_scratch(tm, aw)],
        compiler_params=_params(("parallel",)))(da, wpa4, zuz, o)
    return out[0], [t.reshape(s, aw) for t in out[1:4]], [t.reshape(s, aw) for t in out[4:7]]


def _pool_gate_bwd(dp_in, wpp4, zuz, lin, pooled, w_pool, pool_scale, aw):
    s, d = dp_in.shape
    pw, tn = wpp4.shape[1], wpp4.shape[2]
    n_win = len(POOL_WINDOWS)
    pg = pw // n_win
    tm = _divisor_tile(s, 512, 16)
    z_col = aw // pw + 1

    def body(dp_ref, w_ref, z_ref, l_ref, p_ref, wp_ref, sc_ref, dz_ref, dpo_ref, dw_ref, ds_ref):
        @pl.when(pl.program_id(0) == 0)
        def _():
            dw_ref[...] = jnp.zeros_like(dw_ref)
            ds_ref[...] = jnp.zeros_like(ds_ref)

        dy = _proj_t(dp_ref, w_ref, tn)
        z, lin_ = z_ref[...].astype(F32), l_ref[...]
        sg = _sigmoid(z)
        dypp = dy * (z * sg)
        dz_ref[...] = (dy * (lin_ * sc_ref[...]) * _dsilu(z, sg)).astype(BF16)
        ds_ref[...] += jnp.sum(dypp * lin_, axis=0, keepdims=True)
        dlin = (dypp * sc_ref[...]).astype(BF16)
        for gi in range(n_win):
            cs = slice(gi * pg, (gi + 1) * pg)
            dw_ref[gi] += lax.dot_general(p_ref[:, cs], dlin[:, cs], TN, preferred_element_type=F32)
            dpo_ref[:, cs] = lax.dot_general(dlin[:, cs], wp_ref[gi], NT, preferred_element_type=F32)

    row = pl.BlockSpec((tm, pw), lambda m: (m, 0))
    return pl.pallas_call(
        body, name="pool_gate_bwd", grid=(s // tm,),
        in_specs=[pl.BlockSpec((tm, d), lambda m: (m, 0)), pl.BlockSpec((N_CHIPS, pw, tn), lambda m: (0, 0, 0)),
                  pl.BlockSpec((tm, pw), lambda m: (m, z_col)), row, row,
                  pl.BlockSpec((n_win, pg, pg), lambda m: (0, 0, 0)), pl.BlockSpec((1, pw), lambda m: (0, 0))],
        out_specs=[row, row, pl.BlockSpec((n_win, pg, pg), lambda m: (0, 0, 0)),
                   pl.BlockSpec((1, pw), lambda m: (0, 0))],
        out_shape=[jax.ShapeDtypeStruct((s, pw), BF16), jax.ShapeDtypeStruct((s, pw), F32),
                   jax.ShapeDtypeStruct((n_win, pg, pg), F32), jax.ShapeDtypeStruct((1, pw), F32)],
        compiler_params=_params(("arbitrary",)))(dp_in, wpp4, zuz, lin, pooled, w_pool, pool_scale)


def _pool_bwd(dpooled):
    s, pw = dpooled.shape
    pg = pw // len(POOL_WINDOWS)
    tr = _divisor_tile(s, 512, POOL_HALO)
    per = tr // POOL_HALO
    n_tiles = s // tr

    def body(c_ref, n_ref, du_ref):
        r = pl.program_id(0)
        cur = c_ref[...]
        halo = jnp.where(r < n_tiles - 1, n_ref[...], 0.0)
        ext = jnp.concatenate([cur, halo], axis=0)
        rows = tr + POOL_HALO
        for gi, window in enumerate(POOL_WINDOWS):
            cs = slice(gi * pg, (gi + 1) * pg)
            acc = ext[:, cs] / _pool_counts(r * tr, rows, window)
            shift = 1
            while shift < window:
                acc = acc + pltpu.roll(acc, rows - shift, 0)
                shift *= 2
            du_ref[:, cs] = (acc[:tr] - cur[:, cs]).astype(BF16)

    return pl.pallas_call(
        body, name="pool_bwd", grid=(n_tiles,),
        in_specs=[pl.BlockSpec((tr, pw), lambda r: (r, 0)),
                  pl.BlockSpec((POOL_HALO, pw), lambda r: (jnp.minimum((r + 1) * per, s // POOL_HALO - 1), 0))],
        out_specs=pl.BlockSpec((tr, pw), lambda r: (r, 0)),
        out_shape=jax.ShapeDtypeStruct((s, pw), BF16), compiler_params=_params(("parallel",)))(dpooled, dpooled)


def _attn_bwd(qkv, do, lse, dd, g):
    _, s, aw = qkv.shape
    heads = aw // HEAD_DIM
    n_blocks = s // STEPS
    per_seq = n_blocks // DILATIONS[g]
    pair = 4 if n_blocks % 4 == 0 else 1
    rows_ = pair * STEPS
    n_steps = n_blocks // pair
    tail = slice(rows_ - STEPS, rows_)

    def body(q_ref, do_ref, l_ref, dd_ref, kc_ref, kp_ref, vc_ref, vp_ref, out_ref, cq_ref, ck_ref, cv_ref):
        b = pl.program_id(0)

        @pl.when(b == 0)
        def _():
            cq_ref[...] = jnp.zeros_like(cq_ref)
            ck_ref[...] = jnp.zeros_like(ck_ref)
            cv_ref[...] = jnp.zeros_like(cv_ref)

        out_ref[0] = cq_ref[...].astype(BF16)

        @pl.when(b < n_steps)
        def _():
            masks = [_window_mask(lax.rem(b * pair + j, per_seq) == 0) for j in range(pair)]
            for h in range(heads):
                hs = slice(h * HEAD_DIM, (h + 1) * HEAD_DIM)
                keys = jnp.concatenate([kp_ref[:, hs], kc_ref[:, hs]], axis=0)
                values = jnp.concatenate([vp_ref[:, hs], vc_ref[:, hs]], axis=0)
                dks, dvs = [], []
                for j in range(pair):
                    rows = slice(j * STEPS, (j + 1) * STEPS)
                    window = slice(j * STEPS, (j + 2) * STEPS)
                    q, do_, kk, vv = q_ref[rows, hs], do_ref[rows, hs], keys[window], values[window]
                    lse_ = jnp.concatenate([l_ref[rows, hs], l_ref[rows, hs]], axis=1)
                    dd_ = jnp.concatenate([dd_ref[rows, hs], dd_ref[rows, hs]], axis=1)
                    sc = lax.dot_general(q, kk, NT, preferred_element_type=F32) * SCORE_SCALE
                    prob = jnp.where(masks[j], jnp.exp(sc - lse_), 0.0)
                    dprob = lax.dot_general(do_, vv, NT, preferred_element_type=F32)
                    dsc = prob * (dprob - dd_) * SCORE_SCALE
                    cq_ref[rows, hs] = jnp.dot(dsc.astype(BF16), kk, preferred_element_type=F32)
                    dks.append(lax.dot_general(dsc.astype(BF16), q, TN, preferred_element_type=F32))
                    dvs.append(lax.dot_general(prob.astype(BF16), do_, TN, preferred_element_type=F32))
                for which, carry, parts in ((1, ck_ref, dks), (2, cv_ref, dvs)):
                    out_ref[which, tail, hs] = (carry[tail, hs] + parts[0][:STEPS]).astype(BF16)
                    if pair > 1:
                        out_ref[which, :rows_ - STEPS, hs] = carry[:rows_ - STEPS, hs].astype(BF16)
                    for j in range(pair):
                        total = parts[j][STEPS:]
                        if j + 1 < pair:
                            total = total + parts[j + 1][:STEPS]
                        carry[j * STEPS:(j + 1) * STEPS, hs] = total

        @pl.when(b == n_steps)
        def _():
            out_ref[1] = ck_ref[...].astype(BF16)
            out_ref[2] = cv_ref[...].astype(BF16)

    last = n_steps - 1

    def cur(which):
        return pl.BlockSpec((None, rows_, aw), lambda b: (which, jnp.minimum(b, last), 0))

    def prev(which):
        return pl.BlockSpec((None, STEPS, aw), lambda b: (which, jnp.clip(b * pair - 1, 0, n_blocks - 1), 0))

    row = pl.BlockSpec((rows_, aw), lambda b: (jnp.minimum(b, last), 0))
    return pl.pallas_call(
        body, name=f"attn_bwd{g}", grid=(n_steps + 1,),
        in_specs=[cur(0), row, row, row, cur(1), prev(1), cur(2), prev(2)],
        out_specs=pl.BlockSpec((3, rows_, aw), lambda b: (0, jnp.clip(b - 1, 0, last), 0)),
        out_shape=jax.ShapeDtypeStruct((3, s, aw), BF16),
        scratch_shapes=[pltpu.VMEM((rows_, aw), F32)] * 3,
        compiler_params=_params(("arbitrary",)))(qkv, do, lse, dd, qkv, qkv, qkv, qkv)


def _weight_grad(at, b, tn, col_blocks, name):
    m, k = at.shape
    n = b.shape[1]
    tm = _divisor_tile(m, 1024, 16)
    tk = _divisor_tile(k, 2048, 128)
    nk = k // tk

    def body(a_ref, b_ref, o_ref, acc_ref):
        kk = pl.program_id(2)

        @pl.when(kk == 0)
        def _():
            acc_ref[...] = jnp.zeros_like(acc_ref)

        acc_ref[...] += jnp.dot(a_ref[...], b_ref[...], preferred_element_type=F32)

        @pl.when(kk == nk - 1)
        def _():
            o_ref[...] = acc_ref[...].astype(BF16)

    if col_blocks:
        out_spec = pl.BlockSpec((None, tm, tn), lambda i, j, kk: (j, i, 0))
        out_shape = jax.ShapeDtypeStruct((n // tn, m, tn), BF16)
    else:
        out_spec = pl.BlockSpec((tm, tn), lambda i, j, kk: (i, j))
        out_shape = jax.ShapeDtypeStruct((m, n), BF16)
    return pl.pallas_call(
        body, name=name, grid=(m // tm, n // tn, nk),
        in_specs=[pl.BlockSpec((tm, tk), lambda i, j, kk: (i, kk)), pl.BlockSpec((tk, tn), lambda i, j, kk: (kk, j))],
        out_specs=out_spec, out_shape=out_shape, scratch_shapes=[pltpu.VMEM((tm, tn), F32)],
        compiler_params=_params(("parallel", "parallel", "arbitrary")))(at, b)


def _w_in_grad_part(xt, b, col_of, n_local, tn, w_shape, prev, name):
    d, s = xt.shape
    per_chip = w_shape[2] // tn
    tm = _divisor_tile(d, 1024, 16)
    tk = _divisor_tile(s, 2048, 128)
    nk = s // tk

    def body(*refs):
        a_ref, b_ref, o_ref, acc_ref = refs[0], refs[1], refs[-2], refs[-1]
        kk = pl.program_id(2)

        @pl.when(kk == 0)
        def _():
            acc_ref[...] = jnp.zeros_like(acc_ref)

        acc_ref[...] += jnp.dot(a_ref[...], b_ref[...], preferred_element_type=F32)

        @pl.when(kk == nk - 1)
        def _():
            o_ref[...] = acc_ref[...].astype(BF16)

    if b.ndim == 3:
        sub = b.shape[2] // tn
        b_spec = pl.BlockSpec((None, tk, tn), lambda j, i, kk: (j // sub, kk, j % sub))
    else:
        b_spec = pl.BlockSpec((tk, tn), lambda j, i, kk: (kk, j))
    in_specs = [pl.BlockSpec((tm, tk), lambda j, i, kk: (i, kk)), b_spec]
    args = [xt, b]
    aliases = {}
    if prev is not None:
        in_specs.append(ANY)
        args.append(prev)
        aliases = {2: 0}
    return pl.pallas_call(
        body, name=name, grid=(n_local, d // tm, nk), in_specs=in_specs,
        out_specs=pl.BlockSpec((None, tm, tn), lambda j, i, kk: (col_of(j) // per_chip, i, col_of(j) % per_chip)),
        out_shape=jax.ShapeDtypeStruct(w_shape, BF16), scratch_shapes=[pltpu.VMEM((tm, tn), F32)],
        input_output_aliases=aliases,
        compiler_params=_params(("parallel", "parallel", "arbitrary")))(*args)


def _assemble_w(wcs, after):
    n, d, wc = wcs[0].shape
    tr = _divisor_tile(d, 256, 16)

    def body(after_ref, *refs):
        o_ref = refs[-1]
        for ch in range(W_CHUNKS):
            o_ref[:, ch * wc:(ch + 1) * wc] = refs[ch][...]

    return pl.pallas_call(
        body, name="assemble_w", grid=(n, d // tr),
        in_specs=[pl.BlockSpec(after.shape, lambda b, r: (0, 0))]
        + [pl.BlockSpec((None, tr, wc), lambda b, r: (b, r, 0))] * W_CHUNKS,
        out_specs=pl.BlockSpec((None, tr, W_CHUNKS * wc), lambda b, r: (b, r, 0)),
        out_shape=jax.ShapeDtypeStruct((n, d, W_CHUNKS * wc), wcs[0].dtype),
        compiler_params=_params(("parallel", "parallel")))(after, *wcs)


def _x_grad(dqkv, rest, w4, dr, aw, tn):
    s, d = dr.shape
    sub = aw // tn
    n_qkv = 3 * N_GROUPS * sub
    los, lo = [], n_qkv
    for p in rest:
        los.append(lo)
        lo += p.shape[1] // tn
    n_blocks = lo
    per_chip = n_blocks // N_CHIPS
    tm = _divisor_tile(s, 512, 16 * DILATIONS[-1])

    def body(*refs):
        q_refs, r_refs = refs[:N_GROUPS], refs[N_GROUPS:N_GROUPS + len(rest)]
        w_ref, dr_ref, o_ref, acc_ref, scratch = refs[-5:]
        j = pl.program_id(1)

        @pl.when(j == 0)
        def _():
            acc_ref[...] = ALPHA * dr_ref[...]

        for g, dil in enumerate(DILATIONS):
            @pl.when((j < n_qkv) & (lax.rem(j // sub, N_GROUPS) == g))
            def _(g=g, dil=dil):
                rows = _merge_rows(q_refs[g], scratch, dil).astype(BF16)
                acc_ref[...] += lax.dot_general(rows, w_ref[...], NT, preferred_element_type=F32)

        for p_ref, lo_, piece in zip(r_refs, los, rest):
            @pl.when((j >= lo_) & (j < lo_ + piece.shape[1] // tn))
            def _(p_ref=p_ref):
                acc_ref[...] += lax.dot_general(p_ref[...], w_ref[...], NT, preferred_element_type=F32)

        @pl.when(j == n_blocks - 1)
        def _():
            o_ref[...] = acc_ref[...]

    def qkv_spec(dil):
        def index(i, j):
            region = jnp.minimum(j // sub, 3 * N_GROUPS - 1)
            return region // N_GROUPS, 0, i, jnp.where(j < n_qkv, j % sub, 0)

        return pl.BlockSpec((None, dil, tm // dil, tn), index)

    def rest_spec(lo_, piece):
        n = piece.shape[1] // tn
        return pl.BlockSpec((tm, tn), lambda i, j: (i, jnp.clip(j - lo_, 0, n - 1)))

    row = pl.BlockSpec((tm, d), lambda i, j: (i, 0))
    return pl.pallas_call(
        body, name="x_grad", grid=(s // tm, n_blocks),
        in_specs=[qkv_spec(dil) for dil in DILATIONS] + [rest_spec(lo_, p) for lo_, p in zip(los, rest)]
        + [pl.BlockSpec((None, d, tn), lambda i, j: (j // per_chip, 0, j % per_chip)), row],
        out_specs=row, out_shape=jax.ShapeDtypeStruct((s, d), F32),
        scratch_shapes=[pltpu.VMEM((tm, d), F32), _permute_scratch(tm, tn)],
        compiler_params=_params(("parallel", "arbitrary"), vmem_mib=56))(
            *[t.reshape(3, dil, s // dil, aw) for t, dil in zip(dqkv, DILATIONS)], *rest, w4, dr)


def _prepare_x(x, after=None):
    s, d = x.shape
    tc = 2 * LANES
    slabs = tc // LANES
    ordered = [] if after is None else [after]

    def body(*refs):
        x_ref, xb_ref = refs[len(ordered):len(ordered) + 2]
        xt_refs, scratch = refs[len(ordered) + 2:len(ordered) + 2 + N_GROUPS], refs[-1]
        t = x_ref[...]
        xb_ref[...] = t.astype(BF16)
        for c in range(slabs):
            scratch[c] = t[:, c * LANES:(c + 1) * LANES]
        for g, dil in enumerate(DILATIONS):
            length = s // dil
            for r in range(dil):
                part = t if dil == 1 else jnp.concatenate(
                    [scratch[c, pl.ds(r, length, stride=dil), :] for c in range(slabs)], axis=1)
                xt_refs[g][:, r * length:(r + 1) * length] = part.T.astype(BF16)

    col = pl.BlockSpec((s, tc), lambda j: (0, j))
    row = pl.BlockSpec((tc, s), lambda j: (j, 0))
    t_shape = jax.ShapeDtypeStruct((d, s), BF16)
    out = pl.pallas_call(
        body, name="prepare_x", grid=(d // tc,),
        in_specs=[pl.BlockSpec(t.shape, lambda j: (0, 0)) for t in ordered] + [col],
        out_specs=[col] + [row] * N_GROUPS,
        out_shape=[jax.ShapeDtypeStruct((s, d), BF16)] + [t_shape] * N_GROUPS,
        scratch_shapes=[_permute_scratch(s, tc)], compiler_params=_params(("parallel",)))(*ordered, x)
    return out[0], out[1:]


def _local_step(x, target, w_open, w_close, w_width, b_gate, pool_scale, gamma, beta, aw, pw, small_weights,
                start_exchange=None, first_token=None):
    s, d = x.shape
    tn = _col_tile(aw, pw, w_width)
    sub = aw // tn
    per_chip = w_width // tn
    qkv_w = 3 * N_GROUPS * aw
    w_shape = (N_CHIPS, d, w_width)

    regions = [dict(kind=g, blocks=[(which * N_GROUPS + g) * sub + i for which in range(3) for i in range(sub)])
               for g in range(N_GROUPS)]
    lo = qkv_w // tn
    for name, width in (("zuz", aw + 2 * pw), ("gates", 2 * d)):
        regions.append(dict(kind=name, blocks=list(range(lo, lo + width // tn)), j0=lo, width=width))
        lo += width // tn
    results = [None] * len(regions)
    xb, xts = _prepare_x(x, first_token)
    wcs, last = [], []
    w_open(0, [xb])
    for ch in range(W_CHUNKS):
        wc, token = w_close(ch, last)
        wcs.append(wc)
        calls = []
        for i, region in enumerate(regions):
            blocks = [b for b in region["blocks"] if _chunk_of(b, per_chip) == ch]
            if blocks:
                calls.append((i, region, blocks))
        done = []
        for k, (i, region, blocks) in enumerate(calls):
            after = [token]
            if k == len(calls) - 1 and ch + 1 < W_CHUNKS:
                after.append(w_open(ch + 1, done))
            if region["kind"] in range(N_GROUPS):
                results[i] = _in_proj_qkv(xb, wc, region["kind"], blocks, aw, tn, results[i], after,
                                          f"in_proj_qkv{region['kind']}_{ch}")
            else:
                results[i] = _in_proj(xb, wc, blocks, region["j0"], region["width"], tn, BF16, results[i], after,
                                      f"in_proj_{region['kind']}_{ch}")
            done.append(results[i])
        last = done[-1:]
    qkv = [results[g].reshape(3, s, aw) for g in range(N_GROUPS)]
    zuz, gpre = results[N_GROUPS], results[N_GROUPS + 1]

    attn = [_attn_fwd(qkv[g], g) for g in range(N_GROUPS)]
    o, y_attn, y_attn_t, lse = _combine_groups([a[0] for a in attn], [a[1] for a in attn], zuz, aw)
    w_pool, wpa4, wpp4, w_out = small_weights(o)
    pooled, lin, y_pool, y_pool_t = _pool_fwd(zuz, w_pool, pool_scale, aw, pw)
    a, p, sa, sp, merged, merged_t = _proj_merge(y_attn, y_pool, wpa4, wpp4, gpre, b_gate)
    dr, drb, loss_lanes, d_gamma, d_beta = _out_norm_loss(merged, w_out, x, target, gamma, beta)

    da, dp, d_gpre_a, d_gpre_p, d_b_a, d_b_p = _merge_bwd(drb, w_out, a, p, sa, sp)
    d_b_gate = jnp.concatenate([d_b_a, d_b_p], axis=1)
    d_w_out = _weight_grad(merged_t, drb, d // N_CHIPS, False, "w_out_grad")
    d_wpa4 = _weight_grad(y_attn_t, da, d // N_CHIPS, True, "w_proj_attn_grad")
    d_wpp4 = _weight_grad(y_pool_t, dp, d // N_CHIPS, True, "w_proj_pool_grad")
    d_z_attn, d_o, dd = _attn_gate_bwd(da, wpa4, zuz, o)
    d_z_pool, d_pooled, d_w_pool, d_pool_scale = _pool_gate_bwd(dp, wpp4, zuz, lin, pooled, w_pool, pool_scale, aw)
    d_u = _pool_bwd(d_pooled)
    dqkv = [_attn_bwd(qkv[g], d_o[g], lse[g], dd[g], g) for g in range(N_GROUPS)]

    rest = [d_z_attn, d_u, d_z_pool, d_gpre_a, d_gpre_p]
    d_w_in4 = None
    for g in range(N_GROUPS):
        d_w_in4 = _w_in_grad_part(xts[g], dqkv[g], lambda j, g=g: ((j // sub) * N_GROUPS + g) * sub + j % sub,
                                  3 * sub, tn, w_shape, d_w_in4, f"w_in_grad_qkv{g}")
    lo = qkv_w // tn
    for i, piece in enumerate(rest):
        n_local = piece.shape[1] // tn
        d_w_in4 = _w_in_grad_part(xts[0], piece, lambda j, lo=lo: lo + j, n_local, tn, w_shape, d_w_in4,
                                  f"w_in_grad_rest{i}")
        lo += n_local
    grads = dict(loss_lanes=loss_lanes, w_in=d_w_in4, b_gate=d_b_gate, w_pool=d_w_pool,
                 pool_scale=d_pool_scale, w_proj_attn=d_wpa4, w_proj_pool=d_wpp4, w_out=d_w_out,
                 ln_gamma=d_gamma, ln_beta=d_beta)
    token = jnp.zeros((8, 128), F32) if start_exchange is None else start_exchange(grads)
    grads["d_x"] = _x_grad(dqkv, rest, _assemble_w(wcs, token), dr, aw, tn)
    return grads


def _pack_small(wpa, wpp, w_out, w_pool):
    width = wpa.shape[1]
    return jnp.concatenate([wpa, wpp, w_out.reshape(-1, width), w_pool.reshape(-1, width)], axis=0)


def _unpack_small(packed, aw, pw, d, pg):
    lead = packed.shape[:-2]
    width = d // N_CHIPS
    r0, r1, r2 = aw, aw + pw, aw + pw + d
    return (packed[..., :r0, :], packed[..., r0:r1, :], packed[..., r1:r2, :].reshape(lead + (width, d)),
            packed[..., r2:, :].reshape(lead + (len(POOL_WINDOWS), pg // N_CHIPS, pg)))


def _pack_rows(vectors, rows):
    flat = jnp.concatenate([v.reshape(-1) for v in vectors])
    return jnp.pad(flat, (0, rows * 128 - flat.shape[0])).reshape(rows, 128)


def _unpack_rows(packed, sizes):
    flat, out, lo = packed.reshape(-1), [], 0
    for n in sizes:
        out.append(flat[lo:lo + n].reshape(1, n))
        lo += n
    return out


def kernel(x, w_in, b_gate, w_pool, pool_scale, w_proj_attn, w_proj_pool, w_out, ln_gamma, ln_beta, loss_target, m_w_in, m_b_gate, m_w_pool, m_pool_scale, m_w_proj_attn, m_w_proj_pool, m_w_out, m_ln_gamma, m_ln_beta, v_w_in, v_b_gate, v_w_pool, v_pool_scale, v_w_proj_attn, v_w_proj_pool, v_w_out, v_ln_gamma, v_ln_beta):
    s, d = x.shape[1], x.shape[2]
    aw, pw = w_proj_attn.shape[1], w_proj_pool.shape[1]
    pg = w_pool.shape[3]
    n_win = len(POOL_WINDOWS)

    def small(wpa, wpp, wo, wpl):
        return _pack_small(wpa[0], wpp[0], wo[0], wpl[0])

    chip = 2 * lax.axis_index("x") + lax.axis_index("y")
    core = lax.axis_index("c")

    flight = {"chunk": _halves_start(_place_block(w_in[0], N_CHIPS, chip, BF16, "place_w_in0", 0, W_CHUNKS), x,
                                     "gather_w_in0_start")}
    first_token = flight["chunk"][2]
    w_small = small(w_proj_attn, w_proj_pool, w_out, w_pool) + first_token[0, 0]
    placed = [None] + [_place_block(w_in[0], N_CHIPS, chip, BF16, f"place_w_in{ch}", ch, W_CHUNKS, first_token)
                       for ch in range(1, W_CHUNKS)]
    placed_small = _place_block(w_small, N_CHIPS, chip, BF16, "place_w_small", after=first_token)

    def w_open(ch, after):
        sems, thru, _ = flight["chunk"]
        if ch == 0:
            after = after + placed[1:] + [placed_small]
        landed = _halves_wait(sems, thru, after, f"gather_w_in{ch}_wait")
        if ch + 1 < W_CHUNKS:
            flight["forward"], flight["chunk"] = _advance_start(landed, placed[ch + 1], False, f"advance_w_in{ch}")
        else:
            flight["forward"], flight["small"] = _advance_start(landed, placed_small, True, f"advance_w_in{ch}")
        flight["token"] = flight["forward"][2]
        return flight["token"]

    def w_close(ch, after):
        sems, thru, _ = flight["forward"]
        return _forward_wait(sems, thru, after, f"forward_w_in{ch}_wait"), flight["token"]

    def small_weights(after):
        sems, thru, _ = flight["small"]
        small4 = _broadcast_wait(sems, thru, after, "gather_small_wait")
        wpa4, wpp4, w_out4, w_pool4 = _unpack_small(small4, aw, pw, d, pg)
        return w_pool4.transpose(1, 0, 2, 3).reshape(n_win, pg, pg), wpa4, wpp4, w_out4.reshape(d, d)

    exchange = {}

    def start_exchange(g):
        g_pool4 = g["w_pool"].reshape(n_win, N_CHIPS, pg // N_CHIPS, pg).transpose(1, 0, 2, 3).astype(BF16)
        g_out4 = g["w_out"].reshape(N_CHIPS, d // N_CHIPS, d)
        g_small4 = jnp.concatenate([g["w_proj_attn"], g["w_proj_pool"], g_out4.reshape(N_CHIPS, -1, d // N_CHIPS),
                                    g_pool4.reshape(N_CHIPS, -1, d // N_CHIPS)], axis=1)
        theirs_big, theirs_small = _swap_halves([g["w_in"], g_small4])
        chip_big, placed_big = _add_halves(g["w_in"], theirs_big, core, chip, "add_cores_big")
        chip_small, placed_small = _add_halves(g_small4, theirs_small, core, chip, "add_cores_small")
        sems, sums, placed, token = _scatter_start([chip_big, chip_small], [placed_big, placed_small])
        exchange.update(sems=sems, sums=sums, placed=placed)
        return token

    g = _local_step(x[0], loss_target[0], w_open, w_close, w_in.shape[2], b_gate, pool_scale, ln_gamma, ln_beta,
                    aw, pw, small_weights, start_exchange, first_token)

    sizes = [b_gate.shape[1], pool_scale.shape[1], d, d, 1]
    rows = -(-sum(sizes) // (8 * 128)) * 8
    loss_part = (0.5 / d) * jnp.sum(g["loss_lanes"]).reshape(1, 1)
    parts = _gather_rows(_pack_rows([g["b_gate"], g["pool_scale"], g["ln_gamma"], g["ln_beta"], loss_part], rows))
    zero = jnp.zeros((1, 1), F32)
    packed = [_pack_rows(vs, rows) for vs in ([b_gate, pool_scale, ln_gamma, ln_beta, zero],
                                              [m_b_gate, m_pool_scale, m_ln_gamma, m_ln_beta, zero],
                                              [v_b_gate, v_pool_scale, v_ln_gamma, v_ln_beta, zero])]
    replicated = _sum_rows_adamw(parts, *packed)
    rep = [_unpack_rows(t, sizes) for t in replicated]
    loss = rep[0][4].reshape(())

    got_big, got_small = _scatter_wait(exchange["sems"], exchange["sums"], exchange["placed"],
                                       [g["d_x"], replicated[0]])
    join_sems, halves = _join_start([_sum_slots(got_big, core, "sum_chips_big"),
                                     _sum_slots(got_small, core, "sum_chips_small")])
    mv_small = (small(m_w_proj_attn, m_w_proj_pool, m_w_out, m_w_pool),
                small(v_w_proj_attn, v_w_proj_pool, v_w_out, v_w_pool))
    upd_in = _adamw_half(w_in[0], halves[0], m_w_in[0], v_w_in[0], core, None, "adamw_w_in_own")
    upd_small = _adamw_half(w_small, halves[1], *mv_small, core, None, "adamw_small_own")
    grad_w_in, grad_small = _join_wait(join_sems, halves, [upd_in[0], upd_small[0]])
    upd_in = _adamw_half(w_in[0], grad_w_in, m_w_in[0], v_w_in[0], 1 - core, upd_in, "adamw_w_in_other")
    upd_small = _adamw_half(w_small, grad_small, *mv_small, 1 - core, upd_small, "adamw_small_other")
    grad_w_in, grad_small = upd_in[3], upd_small[3]

    def leaves(big, packed_small, replicated):
        wpa_, wpp_, wo_, wpl_ = _unpack_small(packed_small, aw, pw, d, pg)
        return [big[None], replicated[0], wpl_[None], replicated[1], wpa_[None], wpp_[None], wo_[None],
                replicated[2], replicated[3]]

    out = [loss, g["d_x"][None]]
    out += leaves(grad_w_in, grad_small, rep[0])
    for i in range(3):
        out += leaves(upd_in[i], upd_small[i], rep[1 + i])
    return tuple(out)
```

```python
import math

import jax
import jax.numpy as jnp
from jax import lax
from jax.experimental import pallas as pl
from jax.experimental.pallas import tpu as pltpu

F32 = jnp.float32
BF16 = jnp.bfloat16
MESH = pl.DeviceIdType.MESH
ANY = pl.BlockSpec(memory_space=pl.ANY)

HEAD_DIM = 128
STEPS = 128
DILATIONS = (1, 4, 16)
N_GROUPS = len(DILATIONS)
POOL_WINDOWS = (2, 4, 8, 16)
POOL_HALO = 16
N_CHIPS = 4
N_DEV = 8
ALPHA = 2.0 ** 0.25
LN_EPS = 1e-5
NEG_INF = -1e30
SCORE_SCALE = HEAD_DIM ** -0.5
ADAM_LR = 0.001
ADAM_B1 = 0.9
ADAM_B2 = 0.999
ADAM_EPS = 1e-08
ADAM_WD = 0.01
ADAM_STEP = 10
MIB = 2 ** 20
NT = (((1,), (1,)), ((), ()))
TN = (((0,), (0,)), ((), ()))
DMA_STREAMS = 8


def _params(semantics=None, vmem_mib=48):
    return pltpu.CompilerParams(dimension_semantics=semantics, vmem_limit_bytes=vmem_mib * MIB)


def _divisor_tile(n, target, multiple):
    best = None
    for t in range(multiple, min(n, target) + 1, multiple):
        if n % t == 0:
            best = t
    assert best is not None, (n, target, multiple)
    return best


def _col_tile(*widths):
    g = 0
    for w in widths:
        g = math.gcd(g, w)
    return _divisor_tile(g, 1024, 128)


def _sigmoid(z):
    return jax.nn.sigmoid(z)


def _dsilu(z, sg):
    return sg * (1.0 + z * (1.0 - sg))


def _place():
    x, y, c = lax.axis_index("x"), lax.axis_index("y"), lax.axis_index("c")
    others = [(1 - x, y), (x, 1 - y), (1 - x, 1 - y)]
    return x, y, c, (x, y, 1 - c), others


def _remote(src, dst, send_sem, recv_sem, dev):
    return pltpu.make_async_remote_copy(src_ref=src, dst_ref=dst, send_sem=send_sem, recv_sem=recv_sem,
                                        device_id=dev, device_id_type=MESH)


def _row_pieces(n_rows, streams=DMA_STREAMS, multiple=16):
    size = -(-n_rows // (streams * multiple)) * multiple
    return [(lo, min(size, n_rows - lo)) for lo in range(0, n_rows, size)]


def _start_streams(make, n_rows):
    for lo, size in _row_pieces(n_rows):
        make(pl.ds(lo, size)).start()


def _half_copies(buf, send_sems, recv_sems):
    x, y, c, _, others = _place()
    half = buf.shape[1] // 2
    slab = buf.at[2 * x + y, pl.ds(c * half, half)]
    return [_remote(slab, slab, send_sems[j], recv_sems[j], (ox, oy, c)) for j, (ox, oy) in enumerate(others)]


def _halves_start(placed, after, name):
    k = N_CHIPS - 1

    def body(buf, after_ref, *refs):
        send_sems, recv_sems, token = refs[:k], refs[k:2 * k], refs[-1]
        for cp in _half_copies(buf, send_sems, recv_sems):
            cp.start()
        token[...] = jnp.zeros_like(token)

    out = pl.pallas_call(
        body, name=name,
        out_shape=[pltpu.SemaphoreType.DMA(())] * (2 * k) + [pltpu.HBM(placed.shape, placed.dtype),
                                                             jax.ShapeDtypeStruct((8, 128), F32)],
        in_specs=[HBM, ANY], out_specs=[SEM] * (2 * k) + [HBM, pl.BlockSpec(memory_space=pltpu.VMEM)],
        input_output_aliases={0: 2 * k},
        compiler_params=pltpu.CompilerParams(has_side_effects=DATAFLOW),
    )(pltpu.with_memory_space_constraint(placed, pltpu.HBM), after)
    return out[:2 * k], out[2 * k], out[-1]


def _halves_wait(sems, placed, after, name):
    k = N_CHIPS - 1

    def body(buf, *refs):
        send_sems, recv_sems = refs[:k], refs[k:2 * k]
        for cp in _half_copies(buf, send_sems, recv_sems):
            cp.wait_send()
            cp.wait_recv()

    return pl.pallas_call(
        body, name=name, out_shape=pltpu.HBM(placed.shape, placed.dtype),
        in_specs=[HBM] + [SEM] * (2 * k) + [ANY] * len(after), out_specs=HBM, input_output_aliases={0: 0},
        compiler_params=pltpu.CompilerParams(has_side_effects=DATAFLOW),
    )(placed, *sems, *after)


def _forward_copies(buf, send_sems, recv_sems):
    x, y, c, sibling, others = _place()
    half = buf.shape[1] // 2
    copies = []
    for j, (ox, oy) in enumerate(others):
        slab = buf.at[2 * ox + oy, pl.ds(c * half, half)]
        copies.append(_remote(slab, slab, send_sems[j], recv_sems[j], sibling))
    return copies


def _advance_start(landed, nxt, whole_blocks, name):
    k = N_CHIPS - 1

    def body(a, b, *refs):
        for cp in _forward_copies(a, refs[:k], refs[k:2 * k]):
            cp.start()
        for cp in (_broadcast_copies if whole_blocks else _half_copies)(b, refs[2 * k:3 * k], refs[3 * k:4 * k]):
            cp.start()
        refs[-1][...] = jnp.zeros_like(refs[-1])

    out = pl.pallas_call(
        body, name=name,
        out_shape=[pltpu.SemaphoreType.DMA(())] * (4 * k) + [pltpu.HBM(landed.shape, landed.dtype),
                                                             pltpu.HBM(nxt.shape, nxt.dtype),
                                                             jax.ShapeDtypeStruct((8, 128), F32)],
        in_specs=[HBM, HBM], out_specs=[SEM] * (4 * k) + [HBM, HBM, pl.BlockSpec(memory_space=pltpu.VMEM)],
        input_output_aliases={0: 4 * k, 1: 4 * k + 1},
        compiler_params=pltpu.CompilerParams(has_side_effects=DATAFLOW),
    )(pltpu.with_memory_space_constraint(landed, pltpu.HBM), pltpu.with_memory_space_constraint(nxt, pltpu.HBM))
    token = out[-1]
    return (out[:2 * k], out[4 * k], token), (out[2 * k:4 * k], out[4 * k + 1], token)


def _forward_wait(sems, buf, after, name):
    k = N_CHIPS - 1

    def body(b, *refs):
        send_sems, recv_sems = refs[:k], refs[k:2 * k]
        for cp in _forward_copies(b, send_sems, recv_sems):
            cp.wait_send()
            cp.wait_recv()

    return pl.pallas_call(
        body, name=name, out_shape=pltpu.HBM(buf.shape, buf.dtype),
        in_specs=[HBM] + [SEM] * (2 * k) + [ANY] * len(after), out_specs=HBM, input_output_aliases={0: 0},
        compiler_params=pltpu.CompilerParams(has_side_effects=DATAFLOW),
    )(buf, *sems, *after)


def _swap_halves(grads):
    n = len(grads)

    def body(*refs):
        g, theirs = refs[:n], refs[n:2 * n]
        send_sems, recv_sems = refs[2 * n:]
        x, y, c, sibling, _ = _place()
        for i in range(n):
            half = g[i].shape[1] // 2
            give = (1 - c) * half
            for b in range(N_CHIPS):
                _start_streams(lambda r, i=i, b=b: _remote(
                    g[i].at[b, pl.ds(give + r.start, r.size)], theirs[i].at[b, r], send_sems.at[i], recv_sems.at[i],
                    sibling), half)
        for i in range(n):
            _remote(theirs[i], theirs[i], send_sems.at[i], recv_sems.at[i], sibling).wait()

    return pl.pallas_call(
        body, name="swap_halves",
        out_shape=[jax.ShapeDtypeStruct((s.shape[0], s.shape[1] // 2) + s.shape[2:], s.dtype) for s in grads],
        in_specs=[ANY] * n, out_specs=[ANY] * n,
        scratch_shapes=[pltpu.SemaphoreType.DMA((n,)), pltpu.SemaphoreType.DMA((n,))],
    )(*grads)


HBM = pl.BlockSpec(memory_space=pltpu.HBM)
SEM = pl.BlockSpec(memory_space=pltpu.SEMAPHORE)
DATAFLOW = pltpu.SideEffectType.DATAFLOW_SIDE_EFFECTING


def _broadcast_copies(buf, send_sems, recv_sems):
    x, y, c, _, others = _place()
    mine = buf.at[2 * x + y]
    return [_remote(mine, mine, send_sems[j], recv_sems[j], (ox, oy, c)) for j, (ox, oy) in enumerate(others)]


def _broadcast_wait(sems, placed, after, name):
    k = N_CHIPS - 1

    def body(buf, *refs):
        send_sems, recv_sems = refs[:k], refs[k:2 * k]
        for cp in _broadcast_copies(buf, send_sems, recv_sems):
            cp.wait_send()
            cp.wait_recv()

    return pl.pallas_call(
        body, name=name, out_shape=pltpu.HBM(placed.shape, placed.dtype),
        in_specs=[HBM] + [SEM] * (2 * k) + [ANY], out_specs=HBM, input_output_aliases={0: 0},
        compiler_params=pltpu.CompilerParams(has_side_effects=DATAFLOW),
    )(placed, *sems, after)


def _scatter_copies(s, got, send_sems, recv_sems):
    x, y, c, _, others = _place()
    me = 2 * x + y
    n = len(s)
    return [_remote(s[i].at[2 * ox + oy], got[i].at[me], send_sems[3 * i + j], recv_sems[3 * i + j], (ox, oy, c))
            for i in range(n) for j, (ox, oy) in enumerate(others)]


def _scatter_start(sums, placed):
    n = len(sums)
    k = 3 * n

    def body(*refs):
        s, got, token = refs[:n], refs[n:2 * n], refs[-1]
        send_sems, recv_sems = refs[2 * n:2 * n + k], refs[2 * n + k:2 * n + 2 * k]
        for cp in _scatter_copies(s, got, send_sems, recv_sems):
            cp.start()
        token[...] = jnp.zeros_like(token)

    hbm = [pltpu.HBM(a.shape, a.dtype) for a in list(sums) + list(placed)]
    out = pl.pallas_call(
        body, name="scatter_start",
        out_shape=[pltpu.SemaphoreType.DMA(())] * (2 * k) + hbm + [jax.ShapeDtypeStruct((8, 128), F32)],
        in_specs=[HBM] * (2 * n), out_specs=[SEM] * (2 * k) + [HBM] * (2 * n) + [pl.BlockSpec(memory_space=pltpu.VMEM)],
        input_output_aliases={i: 2 * k + i for i in range(2 * n)},
        compiler_params=pltpu.CompilerParams(has_side_effects=DATAFLOW),
    )(*[pltpu.with_memory_space_constraint(a, pltpu.HBM) for a in list(sums) + list(placed)])
    return out[:2 * k], out[2 * k:2 * k + n], out[2 * k + n:2 * k + 2 * n], out[-1]


def _scatter_wait(sems, sums, placed, after):
    n = len(sums)
    k = 3 * n

    def body(*refs):
        s, got = refs[:n], refs[n:2 * n]
        send_sems, recv_sems = refs[2 * n:2 * n + k], refs[2 * n + k:2 * n + 2 * k]
        for cp in _scatter_copies(s, got, send_sems, recv_sems):
            cp.wait_send()
            cp.wait_recv()

    hbm = [pltpu.HBM(a.shape, a.dtype) for a in list(sums) + list(placed)]
    out = pl.pallas_call(
        body, name="scatter_wait", out_shape=hbm,
        in_specs=[HBM] * (2 * n) + [SEM] * (2 * k) + [ANY] * len(after), out_specs=[HBM] * (2 * n),
        input_output_aliases={i: i for i in range(2 * n)},
        compiler_params=pltpu.CompilerParams(has_side_effects=DATAFLOW),
    )(*sums, *placed, *sems, *after)
    return out[n:]


def _join_copies(bufs, send_sems, recv_sems):
    x, y, c, sibling, _ = _place()
    return [_remote(b.at[c], b.at[c], send_sems[i], recv_sems[i], sibling) for i, b in enumerate(bufs)]


def _join_start(placed):
    n = len(placed)

    def body(*refs):
        bufs, send_sems, recv_sems = refs[:n], refs[n:2 * n], refs[2 * n:3 * n]
        for cp in _join_copies(bufs, send_sems, recv_sems):
            cp.start()

    hbm = [pltpu.HBM(a.shape, a.dtype) for a in placed]
    out = pl.pallas_call(
        body, name="join_start", out_shape=[pltpu.SemaphoreType.DMA(())] * (2 * n) + hbm,
        in_specs=[HBM] * n, out_specs=[SEM] * (2 * n) + [HBM] * n,
        input_output_aliases={i: 2 * n + i for i in range(n)},
        compiler_params=pltpu.CompilerParams(has_side_effects=DATAFLOW),
    )(*[pltpu.with_memory_space_constraint(a, pltpu.HBM) for a in placed])
    return out[:2 * n], out[2 * n:]


def _join_wait(sems, placed, after):
    n = len(placed)

    def body(*refs):
        bufs, send_sems, recv_sems = refs[:n], refs[n:2 * n], refs[2 * n:3 * n]
        for cp in _join_copies(bufs, send_sems, recv_sems):
            cp.wait_send()
            cp.wait_recv()

    return pl.pallas_call(
        body, name="join_wait", out_shape=[pltpu.HBM(a.shape, a.dtype) for a in placed],
        in_specs=[HBM] * n + [SEM] * (2 * n) + [ANY] * len(after), out_specs=[HBM] * n,
        input_output_aliases={i: i for i in range(n)},
        compiler_params=pltpu.CompilerParams(has_side_effects=DATAFLOW),
    )(*placed, *sems, *after)


def _gather_rows(row):
    def body(row_ref, out_ref, send_sems, recv_sems, local_sem):
        x, y, c = lax.axis_index("x"), lax.axis_index("y"), lax.axis_index("c")
        me = 4 * x + 2 * y + c
        local = pltpu.make_async_copy(row_ref, out_ref.at[me], local_sem)
        local.start()
        sent = []
        peers = []
        for k in range(1, N_DEV):
            px, py, pc = x ^ (k >> 2), y ^ ((k >> 1) & 1), c ^ (k & 1)
            peers.append((k, px, py, pc))
            cp = _remote(row_ref, out_ref.at[me], send_sems.at[k - 1], recv_sems.at[k - 1], (px, py, pc))
            cp.start()
            sent.append(cp)
        for k, px, py, pc in peers:
            slot = out_ref.at[4 * px + 2 * py + pc]
            _remote(slot, slot, send_sems.at[k - 1], recv_sems.at[k - 1], (px, py, pc)).wait_recv()
        for cp in sent:
            cp.wait_send()
        local.wait()

    return pl.pallas_call(
        body, name="gather_rows", out_shape=jax.ShapeDtypeStruct((N_DEV,) + row.shape, row.dtype),
        in_specs=[ANY], out_specs=ANY,
        scratch_shapes=[pltpu.SemaphoreType.DMA((N_DEV - 1,)), pltpu.SemaphoreType.DMA((N_DEV - 1,)),
                        pltpu.SemaphoreType.DMA],
    )(row)


def _scalar(i):
    return jnp.reshape(i, (1,)).astype(jnp.int32)


def _place_block(src, n_slots, slot, out_dtype, name, window=0, n_windows=1, after=None):
    rows, cols = src.shape[0], src.shape[1] // n_windows
    tr = _divisor_tile(rows, max(16, (2 * MIB) // (cols * 4)), 16)
    ordered = [] if after is None else [after]

    def body(slot_ref, *refs):
        s_ref, o_ref = refs[len(ordered):]
        o_ref[...] = s_ref[...].astype(o_ref.dtype)

    return pl.pallas_call(
        body, name=name, out_shape=jax.ShapeDtypeStruct((n_slots, rows, cols), out_dtype),
        grid_spec=pltpu.PrefetchScalarGridSpec(
            num_scalar_prefetch=1, grid=(rows // tr,),
            in_specs=[pl.BlockSpec(t.shape, lambda r, sl: (0, 0)) for t in ordered]
            + [pl.BlockSpec((tr, cols), lambda r, sl: (r, window))],
            out_specs=pl.BlockSpec((None, tr, cols), lambda r, sl: (sl[0], r, 0))),
        compiler_params=_params(("parallel",)))(_scalar(slot), *ordered, src)


def _add_halves(g, theirs, core, chip, name):
    n, half, cols = theirs.shape
    tr = _divisor_tile(half, max(16, (4 * MIB) // (cols * 4)), 16)
    per = half // tr

    def body(at_ref, a_ref, b_ref, o_ref, own_ref):
        total = (a_ref[...].astype(F32) + b_ref[...].astype(F32)).astype(o_ref.dtype)
        o_ref[...] = total

        @pl.when(pl.program_id(1) == at_ref[1])
        def _():
            own_ref[...] = total

    spec = pl.BlockSpec((None, tr, cols), lambda r, i, at: (i, r, 0))
    shape = jax.ShapeDtypeStruct(theirs.shape, BF16)
    return pl.pallas_call(
        body, name=name, out_shape=[shape, shape],
        grid_spec=pltpu.PrefetchScalarGridSpec(
            num_scalar_prefetch=1, grid=(per, n),
            in_specs=[pl.BlockSpec((None, tr, cols), lambda r, i, at: (i, at[0] * per + r, 0)), spec],
            out_specs=[spec, pl.BlockSpec((None, tr, cols), lambda r, i, at: (at[1], r, 0))]),
        compiler_params=_params(("parallel", "arbitrary")))(jnp.concatenate([_scalar(core), _scalar(chip)]), g, theirs)


def _sum_slots(a, core, name):
    n, rows, cols = a.shape
    tr = _divisor_tile(rows, max(16, (8 * MIB) // (cols * 4 * n)), 16)

    def body(c_ref, a_ref, o_ref):
        acc = a_ref[0].astype(F32)
        for i in range(1, n):
            acc = acc + a_ref[i].astype(F32)
        o_ref[...] = acc

    return pl.pallas_call(
        body, name=name, out_shape=jax.ShapeDtypeStruct((2, rows, cols), F32),
        grid_spec=pltpu.PrefetchScalarGridSpec(
            num_scalar_prefetch=1, grid=(rows // tr,),
            in_specs=[pl.BlockSpec((n, tr, cols), lambda r, c: (0, r, 0))],
            out_specs=pl.BlockSpec((None, tr, cols), lambda r, c: (c[0], r, 0))),
        compiler_params=_params(("parallel",)))(_scalar(core), a)


def _adamw_math(w, g, m, v):
    m = ADAM_B1 * m + (1.0 - ADAM_B1) * g
    v = ADAM_B2 * v + (1.0 - ADAM_B2) * (g * g)
    m_hat = m / (1.0 - ADAM_B1 ** ADAM_STEP)
    v_hat = v / (1.0 - ADAM_B2 ** ADAM_STEP)
    delta = -ADAM_LR * (m_hat / (jnp.sqrt(v_hat) + ADAM_EPS) + ADAM_WD * w)
    return delta, m, v


def _adamw_half(w, g2, m, v, which, prev, name):
    rows, cols = w.shape
    half = rows // 2
    tr = _divisor_tile(half, max(8, (2 * MIB) // (cols * 4)), 8)
    per = half // tr
    n_out = 4

    def body(h_ref, w_ref, g_ref, m_ref, v_ref, *refs):
        d_ref, nm_ref, nv_ref, go_ref = refs[-n_out:]
        g = g_ref[...]
        d, nm, nv = _adamw_math(w_ref[...], g, m_ref[...], v_ref[...])
        d_ref[...] = d
        nm_ref[...] = nm
        nv_ref[...] = nv
        go_ref[...] = g

    spec = pl.BlockSpec((tr, cols), lambda r, h: (h[0] * per + r, 0))
    in_specs = [spec, pl.BlockSpec((None, tr, cols), lambda r, h: (h[0], r, 0)), spec, spec]
    args = [_scalar(which), w, g2, m, v]
    aliases = {}
    if prev is not None:
        aliases = {len(args) + i: i for i in range(n_out)}
        in_specs += [ANY] * n_out
        args += list(prev)
    return pl.pallas_call(
        body, name=name, out_shape=[jax.ShapeDtypeStruct((rows, cols), F32)] * n_out,
        grid_spec=pltpu.PrefetchScalarGridSpec(num_scalar_prefetch=1, grid=(per,), in_specs=in_specs,
                                               out_specs=[spec] * n_out),
        input_output_aliases=aliases, compiler_params=_params(("parallel",)))(*args)


def _sum_rows_adamw(parts, w, m, v):
    def body(p_ref, w_ref, m_ref, v_ref, g_ref, d_ref, nm_ref, nv_ref):
        g = p_ref[0]
        for i in range(1, N_DEV):
            g = g + p_ref[i]
        d, nm, nv = _adamw_math(w_ref[...], g, m_ref[...], v_ref[...])
        g_ref[...] = g
        d_ref[...] = d
        nm_ref[...] = nm
        nv_ref[...] = nv

    shape = jax.ShapeDtypeStruct(w.shape, F32)
    return pl.pallas_call(body, name="sum_rows_adamw", out_shape=[shape] * 4)(parts, w, m, v)


LANES = 128


def _permute_scratch(rows, width):
    return pltpu.VMEM((width // LANES, rows, LANES), F32)


def _split_rows(value, scratch, dil):
    if dil == 1:
        return [value]
    rows = value.shape[0] // dil
    slabs = value.shape[1] // LANES
    for c in range(slabs):
        scratch[c] = value[:, c * LANES:(c + 1) * LANES]
    return [jnp.concatenate([scratch[c, pl.ds(r, rows, stride=dil), :] for c in range(slabs)], axis=1)
            for r in range(dil)]


def _merge_rows(ref, scratch, dil):
    if dil == 1:
        return ref[0].astype(F32)
    rows = ref.shape[1]
    slabs = ref.shape[2] // LANES
    for r in range(dil):
        part = ref[r].astype(F32)
        for c in range(slabs):
            scratch[c, pl.ds(r, rows, stride=dil), :] = part[:, c * LANES:(c + 1) * LANES]
    return jnp.concatenate([scratch[c] for c in range(slabs)], axis=1)


def _grouped_view(t, dil):
    return t.reshape(dil, t.shape[0] // dil, t.shape[1])


def _grouped_spec(dil, rows, width, index):
    return pl.BlockSpec((dil, rows // dil, width), index)


W_CHUNKS = 4


def _pick(values, j):
    out = values[-1]
    for i in range(len(values) - 2, -1, -1):
        out = jnp.where(j == i, values[i], out)
    return out


def _chunk_of(col, per_chip):
    return (col % per_chip) // (per_chip // W_CHUNKS)


def _w_block(col, per_chip):
    return col // per_chip, 0, (col % per_chip) % (per_chip // W_CHUNKS)


def _in_proj(xb, wc, blocks, j0, ncols, tn, out_dtype, prev, after, name):
    s, d = xb.shape
    per_chip = wc.shape[2] * W_CHUNKS // tn
    tm = _divisor_tile(s, 1024, 16)
    extra = [t for t in after if t is not None]

    def body(*refs):
        a_ref, b_ref = refs[len(extra):len(extra) + 2]
        o_ref = refs[-1]
        o_ref[...] = jnp.dot(a_ref[...], b_ref[...], preferred_element_type=F32).astype(o_ref.dtype)

    in_specs = [pl.BlockSpec(t.shape, lambda j, m: (0, 0)) for t in extra] + [
        pl.BlockSpec((tm, d), lambda j, m: (m, 0)),
        pl.BlockSpec((None, d, tn), lambda j, m: _w_block(_pick(blocks, j), per_chip))]
    args = extra + [xb, wc]
    aliases = {}
    if prev is not None:
        aliases = {len(args): 0}
        in_specs.append(ANY)
        args.append(prev)
    return pl.pallas_call(
        body, name=name, grid=(len(blocks), s // tm), in_specs=in_specs,
        out_specs=pl.BlockSpec((tm, tn), lambda j, m: (m, _pick(blocks, j) - j0)),
        out_shape=jax.ShapeDtypeStruct((s, ncols), out_dtype), input_output_aliases=aliases,
        compiler_params=_params(("parallel", "parallel")))(*args)


def _in_proj_qkv(xb, wc, g, blocks, aw, tn, prev, after, name):
    s, d = xb.shape
    dil = DILATIONS[g]
    per_chip = wc.shape[2] * W_CHUNKS // tn
    sub = aw // tn
    tm = _divisor_tile(s, 1024, 16 * dil)
    extra = [t for t in after if t is not None]

    def body(*refs):
        a_ref, b_ref = refs[len(extra):len(extra) + 2]
        o_ref, scratch = refs[-2:]
        res = jnp.dot(a_ref[...], b_ref[...], preferred_element_type=F32)
        for r, part in enumerate(_split_rows(res, scratch, dil)):
            o_ref[r] = part.astype(BF16)

    def out_index(j, m):
        col = _pick(blocks, j)
        return (col // sub) // N_GROUPS, 0, m, col % sub

    in_specs = [pl.BlockSpec(t.shape, lambda j, m: (0, 0)) for t in extra] + [
        pl.BlockSpec((tm, d), lambda j, m: (m, 0)),
        pl.BlockSpec((None, d, tn), lambda j, m: _w_block(_pick(blocks, j), per_chip))]
    args = extra + [xb, wc]
    aliases = {}
    if prev is not None:
        aliases = {len(args): 0}
        in_specs.append(ANY)
        args.append(prev)
    return pl.pallas_call(
        body, name=name, grid=(len(blocks), s // tm), in_specs=in_specs,
        out_specs=pl.BlockSpec((None, dil, tm // dil, tn), out_index),
        out_shape=jax.ShapeDtypeStruct((3, dil, s // dil, aw), BF16), input_output_aliases=aliases,
        scratch_shapes=[_permute_scratch(tm, tn)],
        compiler_params=_params(("parallel", "parallel")))(*args)


def _window_mask(first):
    qi = lax.broadcasted_iota(jnp.int32, (STEPS, 2 * STEPS), 0)
    kj = lax.broadcasted_iota(jnp.int32, (STEPS, 2 * STEPS), 1)
    lowest = jnp.where(first, STEPS, 0)
    return (kj >= qi) & (kj <= qi + STEPS) & (kj >= lowest)


def _attn_fwd(qkv, g):
    _, s, aw = qkv.shape
    heads = aw // HEAD_DIM
    n_blocks = s // STEPS
    per_seq = n_blocks // DILATIONS[g]
    pair = 4 if n_blocks % 4 == 0 else 1

    def body(q_ref, kc_ref, kp_ref, vc_ref, vp_ref, o_ref, l_ref):
        masks = [_window_mask(lax.rem(pl.program_id(0) * pair + j, per_seq) == 0) for j in range(pair)]
        for h in range(heads):
            hs = slice(h * HEAD_DIM, (h + 1) * HEAD_DIM)
            keys = jnp.concatenate([kp_ref[:, hs], kc_ref[:, hs]], axis=0)
            values = jnp.concatenate([vp_ref[:, hs], vc_ref[:, hs]], axis=0)
            for j in range(pair):
                rows = slice(j * STEPS, (j + 1) * STEPS)
                window = slice(j * STEPS, (j + 2) * STEPS)
                sc = lax.dot_general(q_ref[rows, hs], keys[window], NT, preferred_element_type=F32) * SCORE_SCALE
                sc = jnp.where(masks[j], sc, NEG_INF)
                mx = jnp.max(sc, axis=1, keepdims=True)
                e = jnp.exp(sc - mx)
                den = jnp.sum(e, axis=1, keepdims=True)
                o_ref[rows, hs] = (jnp.dot(e.astype(BF16), values[window], preferred_element_type=F32)
                                   / den).astype(BF16)
                l_ref[rows, hs] = jnp.broadcast_to(mx + jnp.log(den), (STEPS, HEAD_DIM))

    def cur(which):
        return pl.BlockSpec((None, pair * STEPS, aw), lambda b: (which, b, 0))

    def prev(which):
        return pl.BlockSpec((None, STEPS, aw), lambda b: (which, jnp.maximum(pair * b - 1, 0), 0))

    out = pl.BlockSpec((pair * STEPS, aw), lambda b: (b, 0))
    return pl.pallas_call(
        body, name=f"attn_fwd{g}", grid=(n_blocks // pair,),
        in_specs=[cur(0), cur(1), prev(1), cur(2), prev(2)], out_specs=[out, out],
        out_shape=[jax.ShapeDtypeStruct((s, aw), BF16), jax.ShapeDtypeStruct((s, aw), F32)],
        compiler_params=_params(("parallel",)))(qkv, qkv, qkv, qkv, qkv)


def _combine_groups(os, ls, zuz, aw):
    s = zuz.shape[0]
    tr = _divisor_tile(s, 256, 8 * DILATIONS[-1])

    def body(*refs):
        o_refs, l_refs, z_ref = refs[0:3], refs[3:6], refs[6]
        oo_ref, y_ref, yt_ref = refs[7:10]
        lq_refs, scratch = refs[10:13], refs[13]
        ls_ = [_merge_rows(l_refs[g], scratch, dil) for g, dil in enumerate(DILATIONS)]
        mx = jnp.maximum(jnp.maximum(ls_[0], ls_[1]), ls_[2])
        ws = [jnp.exp(l - mx) for l in ls_]
        den = ws[0] + ws[1] + ws[2]
        o = ws[0] * _merge_rows(o_refs[0], scratch, DILATIONS[0])
        for g in range(1, N_GROUPS):
            o = o + ws[g] * _merge_rows(o_refs[g], scratch, DILATIONS[g])
        o = o / den
        z = z_ref[...].astype(F32)
        y = o * (z * _sigmoid(z))
        oo_ref[...] = o.astype(BF16)
        y_ref[...] = y.astype(BF16)
        yt_ref[...] = y.T.astype(BF16)
        for g, dil in enumerate(DILATIONS):
            for r, part in enumerate(_split_rows(mx + jnp.log(den), scratch, dil)):
                lq_refs[g][r] = part

    grouped = [_grouped_spec(dil, tr, aw, lambda r: (0, r, 0)) for dil in DILATIONS]
    one = pl.BlockSpec((tr, aw), lambda r: (r, 0))
    b16 = jax.ShapeDtypeStruct((s, aw), BF16)
    out = pl.pallas_call(
        body, name="combine_groups", grid=(s // tr,),
        in_specs=grouped + grouped + [one],
        out_specs=[one, one, pl.BlockSpec((aw, tr), lambda r: (0, r))] + grouped,
        out_shape=[b16, b16, jax.ShapeDtypeStruct((aw, s), BF16)]
        + [jax.ShapeDtypeStruct((dil, s // dil, aw), F32) for dil in DILATIONS],
        scratch_shapes=[_permute_scratch(tr, aw)],
        compiler_params=_params(("parallel",)))(
            *[_grouped_view(t, dil) for t, dil in zip(os, DILATIONS)],
            *[_grouped_view(t, dil) for t, dil in zip(ls, DILATIONS)], zuz)
    return out[0], out[1], out[2], [t.reshape(s, aw) for t in out[3:]]


def _pool_counts(row0, rows, window):
    t = row0 + lax.broadcasted_iota(jnp.int32, (rows, 1), 0)
    return jnp.minimum(t + 1, window).astype(F32)


def _pool_fwd(zuz, w_pool, pool_scale, aw, pw):
    s = zuz.shape[0]
    pg = pw // len(POOL_WINDOWS)
    tr = _divisor_tile(s, 512, 128)
    u_col, z_col = aw // pw, aw // pw + 1
    assert aw % pw == 0

    def body(u_ref, up_ref, z_ref, w_ref, sc_ref, p_ref, l_ref, y_ref, yt_ref):
        r = pl.program_id(0)
        u = u_ref[...].astype(F32)
        halo = jnp.where(r > 0, up_ref[...].astype(F32), 0.0)
        ext = jnp.concatenate([halo, u], axis=0)
        pieces, lins = [], []
        for gi, window in enumerate(POOL_WINDOWS):
            cs = slice(gi * pg, (gi + 1) * pg)
            acc = ext[:, cs]
            shift = 1
            while shift < window:
                acc = acc + pltpu.roll(acc, shift, 0)
                shift *= 2
            p = acc[POOL_HALO:] / _pool_counts(r * tr, tr, window) - u[:, cs]
            pieces.append(p)
            lins.append(jnp.dot(p.astype(BF16), w_ref[gi], preferred_element_type=F32))
        p = jnp.concatenate(pieces, axis=1)
        lin = jnp.concatenate(lins, axis=1)
        z = z_ref[...].astype(F32)
        y = lin * sc_ref[...] * (z * _sigmoid(z))
        p_ref[...] = p.astype(BF16)
        l_ref[...] = lin
        y_ref[...] = y.astype(BF16)
        yt_ref[...] = y.T.astype(BF16)

    per = tr // POOL_HALO
    out = pl.BlockSpec((tr, pw), lambda r: (r, 0))
    return pl.pallas_call(
        body, name="pool_fwd", grid=(s // tr,),
        in_specs=[pl.BlockSpec((tr, pw), lambda r: (r, u_col)),
                  pl.BlockSpec((POOL_HALO, pw), lambda r: (jnp.maximum(r * per - 1, 0), u_col)),
                  pl.BlockSpec((tr, pw), lambda r: (r, z_col)),
                  pl.BlockSpec((len(POOL_WINDOWS), pg, pg), lambda r: (0, 0, 0)),
                  pl.BlockSpec((1, pw), lambda r: (0, 0))],
        out_specs=[out, out, out, pl.BlockSpec((pw, tr), lambda r: (0, r))],
        out_shape=[jax.ShapeDtypeStruct((s, pw), BF16), jax.ShapeDtypeStruct((s, pw), F32),
                   jax.ShapeDtypeStruct((s, pw), BF16), jax.ShapeDtypeStruct((pw, s), BF16)],
        compiler_params=_params(("parallel",)))(zuz, zuz, zuz, w_pool, pool_scale)


def _proj_merge(y_attn, y_pool, wpa4, wpp4, gpre, b_gate):
    s, aw = y_attn.shape
    pw = y_pool.shape[1]
    tn = wpa4.shape[2]
    d = N_CHIPS * tn
    tm = _divisor_tile(s, 1024, 128)

    def body(ya_ref, yp_ref, wa_ref, wp_ref, ga_ref, gp_ref, ba_ref, bp_ref, a_ref, p_ref, sa_ref, sp_ref, m_ref,
             mt_ref):
        a = jnp.dot(ya_ref[...], wa_ref[...], preferred_element_type=F32)
        p = jnp.dot(yp_ref[...], wp_ref[...], preferred_element_type=F32)
        sa = _sigmoid(ga_ref[...].astype(F32) + ba_ref[...])
        sp = _sigmoid(gp_ref[...].astype(F32) + bp_ref[...])
        merged = sa * a + sp * p
        a_ref[...] = a.astype(BF16)
        p_ref[...] = p.astype(BF16)
        sa_ref[...] = sa.astype(BF16)
        sp_ref[...] = sp.astype(BF16)
        m_ref[...] = merged.astype(BF16)
        mt_ref[...] = merged.T.astype(BF16)

    out = pl.BlockSpec((tm, tn), lambda n, m: (m, n))
    f = jax.ShapeDtypeStruct((s, d), BF16)
    return pl.pallas_call(
        body, name="proj_merge", grid=(N_CHIPS, s // tm),
        in_specs=[pl.BlockSpec((tm, aw), lambda n, m: (m, 0)), pl.BlockSpec((tm, pw), lambda n, m: (m, 0)),
                  pl.BlockSpec((None, aw, tn), lambda n, m: (n, 0, 0)),
                  pl.BlockSpec((None, pw, tn), lambda n, m: (n, 0, 0)),
                  pl.BlockSpec((tm, tn), lambda n, m: (m, n)), pl.BlockSpec((tm, tn), lambda n, m: (m, N_CHIPS + n)),
                  pl.BlockSpec((1, tn), lambda n, m: (0, n)), pl.BlockSpec((1, tn), lambda n, m: (0, N_CHIPS + n))],
        out_specs=[out] * 5 + [pl.BlockSpec((tn, tm), lambda n, m: (n, m))],
        out_shape=[f] * 5 + [jax.ShapeDtypeStruct((d, s), BF16)],
        compiler_params=_params(("parallel", "parallel")))(y_attn, y_pool, wpa4, wpp4, gpre, gpre, b_gate, b_gate)


def _out_norm_loss(merged, w_out, x, target, gamma, beta):
    s, d = x.shape
    tm = _divisor_tile(s, 512, 16)

    def body(m_ref, w_ref, x_ref, t_ref, g_ref, b_ref, dr_ref, drb_ref, loss_ref, dg_ref, db_ref):
        @pl.when(pl.program_id(0) == 0)
        def _():
            loss_ref[...] = jnp.zeros_like(loss_ref)
            dg_ref[...] = jnp.zeros_like(dg_ref)
            db_ref[...] = jnp.zeros_like(db_ref)

        r = ALPHA * x_ref[...] + jnp.dot(m_ref[...], w_ref[...], preferred_element_type=F32)
        mu = jnp.mean(r, axis=1, keepdims=True)
        rc = r - mu
        rstd = lax.rsqrt(jnp.mean(rc * rc, axis=1, keepdims=True) + LN_EPS)
        xhat = rc * rstd
        diff = xhat * g_ref[...] + b_ref[...] - t_ref[...]
        dy = diff / d
        loss_ref[...] += jnp.sum(diff * diff, axis=0, keepdims=True)
        dg_ref[...] += jnp.sum(dy * xhat, axis=0, keepdims=True)
        db_ref[...] += jnp.sum(dy, axis=0, keepdims=True)
        dxhat = dy * g_ref[...]
        dr = rstd * (dxhat - jnp.mean(dxhat, axis=1, keepdims=True)
                     - xhat * jnp.mean(dxhat * xhat, axis=1, keepdims=True))
        dr_ref[...] = dr
        drb_ref[...] = dr.astype(BF16)

    row = pl.BlockSpec((tm, d), lambda m: (m, 0))
    vec = pl.BlockSpec((1, d), lambda m: (0, 0))
    v = jax.ShapeDtypeStruct((1, d), F32)
    return pl.pallas_call(
        body, name="out_norm_loss", grid=(s // tm,),
        in_specs=[row, pl.BlockSpec((d, d), lambda m: (0, 0), pipeline_mode=pl.Buffered(1)), row, row, vec, vec],
        out_specs=[row, row, vec, vec, vec],
        out_shape=[jax.ShapeDtypeStruct((s, d), F32), jax.ShapeDtypeStruct((s, d), BF16), v, v, v],
        compiler_params=_params(("arbitrary",), vmem_mib=56))(merged, w_out, x, target, gamma, beta)


def _merge_bwd(drb, w_out, a, p, sa, sp):
    s, d = drb.shape
    tm = _divisor_tile(s, 1024, 16)
    tn = d // N_CHIPS

    def body(dr_ref, w_ref, a_ref, p_ref, sa_ref, sp_ref, da_ref, dp_ref, dga_ref, dgp_ref, dba_ref, dbp_ref):
        @pl.when(pl.program_id(1) == 0)
        def _():
            dba_ref[...] = jnp.zeros_like(dba_ref)
            dbp_ref[...] = jnp.zeros_like(dbp_ref)

        dm = lax.dot_general(dr_ref[...], w_ref[...], NT, preferred_element_type=F32)
        sa = sa_ref[...].astype(F32)
        sp = sp_ref[...].astype(F32)
        da_ref[...] = (dm * sa).astype(BF16)
        dp_ref[...] = (dm * sp).astype(BF16)
        dga = dm * a_ref[...].astype(F32) * sa * (1.0 - sa)
        dgp = dm * p_ref[...].astype(F32) * sp * (1.0 - sp)
        dga_ref[...] = dga.astype(BF16)
        dgp_ref[...] = dgp.astype(BF16)
        dba_ref[...] += jnp.sum(dga, axis=0, keepdims=True)
        dbp_ref[...] += jnp.sum(dgp, axis=0, keepdims=True)

    blk = pl.BlockSpec((tm, tn), lambda n, m: (m, n))
    vec = pl.BlockSpec((1, tn), lambda n, m: (0, n))
    b16 = jax.ShapeDtypeStruct((s, d), BF16)
    v = jax.ShapeDtypeStruct((1, d), F32)
    return pl.pallas_call(
        body, name="merge_bwd", grid=(N_CHIPS, s // tm),
        in_specs=[pl.BlockSpec((tm, d), lambda n, m: (m, 0)), pl.BlockSpec((tn, d), lambda n, m: (n, 0)),
                  blk, blk, blk, blk],
        out_specs=[blk, blk, blk, blk, vec, vec], out_shape=[b16, b16, b16, b16, v, v],
        compiler_params=_params(("parallel", "arbitrary")))(drb, w_out, a, p, sa, sp)


def _proj_t(dy_ref, w_ref, tn):
    acc = None
    for n in range(N_CHIPS):
        t = lax.dot_general(dy_ref[:, n * tn:(n + 1) * tn], w_ref[n], NT, preferred_element_type=F32)
        acc = t if acc is None else acc + t
    return acc


def _attn_gate_bwd(da, wpa4, zuz, o):
    s, d = da.shape
    aw, tn = wpa4.shape[1], wpa4.shape[2]
    heads = aw // HEAD_DIM
    tm = _divisor_tile(s, 256, 16 * DILATIONS[-1])

    def body(*refs):
        da_ref, w_ref, z_ref, o_ref, dz_ref = refs[:5]
        do_refs, dd_refs, scratch = refs[5:8], refs[8:11], refs[11]
        dy = _proj_t(da_ref, w_ref, tn)
        z, o = z_ref[...].astype(F32), o_ref[...].astype(F32)
        sg = _sigmoid(z)
        do = dy * (z * sg)
        dz_ref[...] = (dy * o * _dsilu(z, sg)).astype(BF16)
        prod = do * o
        dd = jnp.concatenate(
            [jnp.broadcast_to(jnp.sum(prod[:, h * HEAD_DIM:(h + 1) * HEAD_DIM], axis=1, keepdims=True),
                              (tm, HEAD_DIM)) for h in range(heads)], axis=1)
        for g, dil in enumerate(DILATIONS):
            for r, part in enumerate(_split_rows(do, scratch, dil)):
                do_refs[g][r] = part.astype(BF16)
            for r, part in enumerate(_split_rows(dd, scratch, dil)):
                dd_refs[g][r] = part

    row = pl.BlockSpec((tm, aw), lambda m: (m, 0))
    grouped = [_grouped_spec(dil, tm, aw, lambda m: (0, m, 0)) for dil in DILATIONS]
    out = pl.pallas_call(
        body, name="attn_gate_bwd", grid=(s // tm,),
        in_specs=[pl.BlockSpec((tm, d), lambda m: (m, 0)), pl.BlockSpec((N_CHIPS, aw, tn), lambda m: (0, 0, 0)),
                  row, row],
        out_specs=[row] + grouped + grouped,
        out_shape=[jax.ShapeDtypeStruct((s, aw), BF16)]
        + [jax.ShapeDtypeStruct((dil, s // dil, aw), BF16) for dil in DILATIONS]
        + [jax.ShapeDtypeStruct((dil, s // dil, aw), F32) for dil in DILATIONS],
        scratch_shapes=[_permute_scratch(tm, aw)],
        compiler_params=_params(("parallel",)))(da, wpa4, zuz, o)
    return out[0], [t.reshape(s, aw) for t in out[1:4]], [t.reshape(s, aw) for t in out[4:7]]


def _pool_gate_bwd(dp_in, wpp4, zuz, lin, pooled, w_pool, pool_scale, aw):
    s, d = dp_in.shape
    pw, tn = wpp4.shape[1], wpp4.shape[2]
    n_win = len(POOL_WINDOWS)
    pg = pw // n_win
    tm = _divisor_tile(s, 512, 16)
    z_col = aw // pw + 1

    def body(dp_ref, w_ref, z_ref, l_ref, p_ref, wp_ref, sc_ref, dz_ref, dpo_ref, dw_ref, ds_ref):
        @pl.when(pl.program_id(0) == 0)
        def _():
            dw_ref[...] = jnp.zeros_like(dw_ref)
            ds_ref[...] = jnp.zeros_like(ds_ref)

        dy = _proj_t(dp_ref, w_ref, tn)
        z, lin_ = z_ref[...].astype(F32), l_ref[...]
        sg = _sigmoid(z)
        dypp = dy * (z * sg)
        dz_ref[...] = (dy * (lin_ * sc_ref[...]) * _dsilu(z, sg)).astype(BF16)
        ds_ref[...] += jnp.sum(dypp * lin_, axis=0, keepdims=True)
        dlin = (dypp * sc_ref[...]).astype(BF16)
        for gi in range(n_win):
            cs = slice(gi * pg, (gi + 1) * pg)
            dw_ref[gi] += lax.dot_general(p_ref[:, cs], dlin[:, cs], TN, preferred_element_type=F32)
            dpo_ref[:, cs] = lax.dot_general(dlin[:, cs], wp_ref[gi], NT, preferred_element_type=F32)

    row = pl.BlockSpec((tm, pw), lambda m: (m, 0))
    return pl.pallas_call(
        body, name="pool_gate_bwd", grid=(s // tm,),
        in_specs=[pl.BlockSpec((tm, d), lambda m: (m, 0)), pl.BlockSpec((N_CHIPS, pw, tn), lambda m: (0, 0, 0)),
                  pl.BlockSpec((tm, pw), lambda m: (m, z_col)), row, row,
                  pl.BlockSpec((n_win, pg, pg), lambda m: (0, 0, 0)), pl.BlockSpec((1, pw), lambda m: (0, 0))],
        out_specs=[row, row, pl.BlockSpec((n_win, pg, pg), lambda m: (0, 0, 0)),
                   pl.BlockSpec((1, pw), lambda m: (0, 0))],
        out_shape=[jax.ShapeDtypeStruct((s, pw), BF16), jax.ShapeDtypeStruct((s, pw), F32),
                   jax.ShapeDtypeStruct((n_win, pg, pg), F32), jax.ShapeDtypeStruct((1, pw), F32)],
        compiler_params=_params(("arbitrary",)))(dp_in, wpp4, zuz, lin, pooled, w_pool, pool_scale)


def _pool_bwd(dpooled):
    s, pw = dpooled.shape
    pg = pw // len(POOL_WINDOWS)
    tr = _divisor_tile(s, 512, POOL_HALO)
    per = tr // POOL_HALO
    n_tiles = s // tr

    def body(c_ref, n_ref, du_ref):
        r = pl.program_id(0)
        cur = c_ref[...]
        halo = jnp.where(r < n_tiles - 1, n_ref[...], 0.0)
        ext = jnp.concatenate([cur, halo], axis=0)
        rows = tr + POOL_HALO
        for gi, window in enumerate(POOL_WINDOWS):
            cs = slice(gi * pg, (gi + 1) * pg)
            acc = ext[:, cs] / _pool_counts(r * tr, rows, window)
            shift = 1
            while shift < window:
                acc = acc + pltpu.roll(acc, rows - shift, 0)
                shift *= 2
            du_ref[:, cs] = (acc[:tr] - cur[:, cs]).astype(BF16)

    return pl.pallas_call(
        body, name="pool_bwd", grid=(n_tiles,),
        in_specs=[pl.BlockSpec((tr, pw), lambda r: (r, 0)),
                  pl.BlockSpec((POOL_HALO, pw), lambda r: (jnp.minimum((r + 1) * per, s // POOL_HALO - 1), 0))],
        out_specs=pl.BlockSpec((tr, pw), lambda r: (r, 0)),
        out_shape=jax.ShapeDtypeStruct((s, pw), BF16), compiler_params=_params(("parallel",)))(dpooled, dpooled)


def _attn_bwd(qkv, do, lse, dd, g):
    _, s, aw = qkv.shape
    heads = aw // HEAD_DIM
    n_blocks = s // STEPS
    per_seq = n_blocks // DILATIONS[g]
    pair = 4 if n_blocks % 4 == 0 else 1
    rows_ = pair * STEPS
    n_steps = n_blocks // pair
    tail = slice(rows_ - STEPS, rows_)

    def body(q_ref, do_ref, l_ref, dd_ref, kc_ref, kp_ref, vc_ref, vp_ref, out_ref, cq_ref, ck_ref, cv_ref):
        b = pl.program_id(0)

        @pl.when(b == 0)
        def _():
            cq_ref[...] = jnp.zeros_like(cq_ref)
            ck_ref[...] = jnp.zeros_like(ck_ref)
            cv_ref[...] = jnp.zeros_like(cv_ref)

        out_ref[0] = cq_ref[...].astype(BF16)

        @pl.when(b < n_steps)
        def _():
            masks = [_window_mask(lax.rem(b * pair + j, per_seq) == 0) for j in range(pair)]
            for h in range(heads):
                hs = slice(h * HEAD_DIM, (h + 1) * HEAD_DIM)
                keys = jnp.concatenate([kp_ref[:, hs], kc_ref[:, hs]], axis=0)
                values = jnp.concatenate([vp_ref[:, hs], vc_ref[:, hs]], axis=0)
                dks, dvs = [], []
                for j in range(pair):
                    rows = slice(j * STEPS, (j + 1) * STEPS)
                    window = slice(j * STEPS, (j + 2) * STEPS)
                    q, do_, kk, vv = q_ref[rows, hs], do_ref[rows, hs], keys[window], values[window]
                    lse_ = jnp.concatenate([l_ref[rows, hs], l_ref[rows, hs]], axis=1)
                    dd_ = jnp.concatenate([dd_ref[rows, hs], dd_ref[rows, hs]], axis=1)
                    sc = lax.dot_general(q, kk, NT, preferred_element_type=F32) * SCORE_SCALE
                    prob = jnp.where(masks[j], jnp.exp(sc - lse_), 0.0)
                    dprob = lax.dot_general(do_, vv, NT, preferred_element_type=F32)
                    dsc = prob * (dprob - dd_) * SCORE_SCALE
                    cq_ref[rows, hs] = jnp.dot(dsc.astype(BF16), kk, preferred_element_type=F32)
                    dks.append(lax.dot_general(dsc.astype(BF16), q, TN, preferred_element_type=F32))
                    dvs.append(lax.dot_general(prob.astype(BF16), do_, TN, preferred_element_type=F32))
                for which, carry, parts in ((1, ck_ref, dks), (2, cv_ref, dvs)):
                    out_ref[which, tail, hs] = (carry[tail, hs] + parts[0][:STEPS]).astype(BF16)
                    if pair > 1:
                        out_ref[which, :rows_ - STEPS, hs] = carry[:rows_ - STEPS, hs].astype(BF16)
                    for j in range(pair):
                        total = parts[j][STEPS:]
                        if j + 1 < pair:
                            total = total + parts[j + 1][:STEPS]
                        carry[j * STEPS:(j + 1) * STEPS, hs] = total

        @pl.when(b == n_steps)
        def _():
            out_ref[1] = ck_ref[...].astype(BF16)
            out_ref[2] = cv_ref[...].astype(BF16)

    last = n_steps - 1

    def cur(which):
        return pl.BlockSpec((None, rows_, aw), lambda b: (which, jnp.minimum(b, last), 0))

    def prev(which):
        return pl.BlockSpec((None, STEPS, aw), lambda b: (which, jnp.clip(b * pair - 1, 0, n_blocks - 1), 0))

    row = pl.BlockSpec((rows_, aw), lambda b: (jnp.minimum(b, last), 0))
    return pl.pallas_call(
        body, name=f"attn_bwd{g}", grid=(n_steps + 1,),
        in_specs=[cur(0), row, row, row, cur(1), prev(1), cur(2), prev(2)],
        out_specs=pl.BlockSpec((3, rows_, aw), lambda b: (0, jnp.clip(b - 1, 0, last), 0)),
        out_shape=jax.ShapeDtypeStruct((3, s, aw), BF16),
        scratch_shapes=[pltpu.VMEM((rows_, aw), F32)] * 3,
        compiler_params=_params(("arbitrary",)))(qkv, do, lse, dd, qkv, qkv, qkv, qkv)


def _weight_grad(at, b, tn, col_blocks, name):
    m, k = at.shape
    n = b.shape[1]
    tm = _divisor_tile(m, 1024, 16)
    tk = _divisor_tile(k, 2048, 128)
    nk = k // tk

    def body(a_ref, b_ref, o_ref, acc_ref):
        kk = pl.program_id(2)

        @pl.when(kk == 0)
        def _():
            acc_ref[...] = jnp.zeros_like(acc_ref)

        acc_ref[...] += jnp.dot(a_ref[...], b_ref[...], preferred_element_type=F32)

        @pl.when(kk == nk - 1)
        def _():
            o_ref[...] = acc_ref[...].astype(BF16)

    if col_blocks:
        out_spec = pl.BlockSpec((None, tm, tn), lambda i, j, kk: (j, i, 0))
        out_shape = jax.ShapeDtypeStruct((n // tn, m, tn), BF16)
    else:
        out_spec = pl.BlockSpec((tm, tn), lambda i, j, kk: (i, j))
        out_shape = jax.ShapeDtypeStruct((m, n), BF16)
    return pl.pallas_call(
        body, name=name, grid=(m // tm, n // tn, nk),
        in_specs=[pl.BlockSpec((tm, tk), lambda i, j, kk: (i, kk)), pl.BlockSpec((tk, tn), lambda i, j, kk: (kk, j))],
        out_specs=out_spec, out_shape=out_shape, scratch_shapes=[pltpu.VMEM((tm, tn), F32)],
        compiler_params=_params(("parallel", "parallel", "arbitrary")))(at, b)


def _w_in_grad_part(xt, b, col_of, n_local, tn, w_shape, prev, name):
    d, s = xt.shape
    per_chip = w_shape[2] // tn
    tm = _divisor_tile(d, 1024, 16)
    tk = _divisor_tile(s, 2048, 128)
    nk = s // tk

    def body(*refs):
        a_ref, b_ref, o_ref, acc_ref = refs[0], refs[1], refs[-2], refs[-1]
        kk = pl.program_id(2)

        @pl.when(kk == 0)
        def _():
            acc_ref[...] = jnp.zeros_like(acc_ref)

        acc_ref[...] += jnp.dot(a_ref[...], b_ref[...], preferred_element_type=F32)

        @pl.when(kk == nk - 1)
        def _():
            o_ref[...] = acc_ref[...].astype(BF16)

    if b.ndim == 3:
        sub = b.shape[2] // tn
        b_spec = pl.BlockSpec((None, tk, tn), lambda j, i, kk: (j // sub, kk, j % sub))
    else:
        b_spec = pl.BlockSpec((tk, tn), lambda j, i, kk: (kk, j))
    in_specs = [pl.BlockSpec((tm, tk), lambda j, i, kk: (i, kk)), b_spec]
    args = [xt, b]
    aliases = {}
    if prev is not None:
        in_specs.append(ANY)
        args.append(prev)
        aliases = {2: 0}
    return pl.pallas_call(
        body, name=name, grid=(n_local, d // tm, nk), in_specs=in_specs,
        out_specs=pl.BlockSpec((None, tm, tn), lambda j, i, kk: (col_of(j) // per_chip, i, col_of(j) % per_chip)),
        out_shape=jax.ShapeDtypeStruct(w_shape, BF16), scratch_shapes=[pltpu.VMEM((tm, tn), F32)],
        input_output_aliases=aliases,
        compiler_params=_params(("parallel", "parallel", "arbitrary")))(*args)


def _assemble_w(wcs, after):
    n, d, wc = wcs[0].shape
    tr = _divisor_tile(d, 256, 16)

    def body(after_ref, *refs):
        o_ref = refs[-1]
        for ch in range(W_CHUNKS):
            o_ref[:, ch * wc:(ch + 1) * wc] = refs[ch][...]

    return pl.pallas_call(
        body, name="assemble_w", grid=(n, d // tr),
        in_specs=[pl.BlockSpec(after.shape, lambda b, r: (0, 0))]
        + [pl.BlockSpec((None, tr, wc), lambda b, r: (b, r, 0))] * W_CHUNKS,
        out_specs=pl.BlockSpec((None, tr, W_CHUNKS * wc), lambda b, r: (b, r, 0)),
        out_shape=jax.ShapeDtypeStruct((n, d, W_CHUNKS * wc), wcs[0].dtype),
        compiler_params=_params(("parallel", "parallel")))(after, *wcs)


def _x_grad(dqkv, rest, w4, dr, aw, tn):
    s, d = dr.shape
    sub = aw // tn
    n_qkv = 3 * N_GROUPS * sub
    los, lo = [], n_qkv
    for p in rest:
        los.append(lo)
        lo += p.shape[1] // tn
    n_blocks = lo
    per_chip = n_blocks // N_CHIPS
    tm = _divisor_tile(s, 512, 16 * DILATIONS[-1])

    def body(*refs):
        q_refs, r_refs = refs[:N_GROUPS], refs[N_GROUPS:N_GROUPS + len(rest)]
        w_ref, dr_ref, o_ref, acc_ref, scratch = refs[-5:]
        j = pl.program_id(1)

        @pl.when(j == 0)
        def _():
            acc_ref[...] = ALPHA * dr_ref[...]

        for g, dil in enumerate(DILATIONS):
            @pl.when((j < n_qkv) & (lax.rem(j // sub, N_GROUPS) == g))
            def _(g=g, dil=dil):
                rows = _merge_rows(q_refs[g], scratch, dil).astype(BF16)
                acc_ref[...] += lax.dot_general(rows, w_ref[...], NT, preferred_element_type=F32)

        for p_ref, lo_, piece in zip(r_refs, los, rest):
            @pl.when((j >= lo_) & (j < lo_ + piece.shape[1] // tn))
            def _(p_ref=p_ref):
                acc_ref[...] += lax.dot_general(p_ref[...], w_ref[...], NT, preferred_element_type=F32)

        @pl.when(j == n_blocks - 1)
        def _():
            o_ref[...] = acc_ref[...]

    def qkv_spec(dil):
        def index(i, j):
            region = jnp.minimum(j // sub, 3 * N_GROUPS - 1)
            return region // N_GROUPS, 0, i, jnp.where(j < n_qkv, j % sub, 0)

        return pl.BlockSpec((None, dil, tm // dil, tn), index)

    def rest_spec(lo_, piece):
        n = piece.shape[1] // tn
        return pl.BlockSpec((tm, tn), lambda i, j: (i, jnp.clip(j - lo_, 0, n - 1)))

    row = pl.BlockSpec((tm, d), lambda i, j: (i, 0))
    return pl.pallas_call(
        body, name="x_grad", grid=(s // tm, n_blocks),
        in_specs=[qkv_spec(dil) for dil in DILATIONS] + [rest_spec(lo_, p) for lo_, p in zip(los, rest)]
        + [pl.BlockSpec((None, d, tn), lambda i, j: (j // per_chip, 0, j % per_chip)), row],
        out_specs=row, out_shape=jax.ShapeDtypeStruct((s, d), F32),
        scratch_shapes=[pltpu.VMEM((tm, d), F32), _permute_scratch(tm, tn)],
        compiler_params=_params(("parallel", "arbitrary"), vmem_mib=56))(
            *[t.reshape(3, dil, s // dil, aw) for t, dil in zip(dqkv, DILATIONS)], *rest, w4, dr)


def _prepare_x(x, after=None):
    s, d = x.shape
    tc = 2 * LANES
    slabs = tc // LANES
    ordered = [] if after is None else [after]

    def body(*refs):
        x_ref, xb_ref = refs[len(ordered):len(ordered) + 2]
        xt_refs, scratch = refs[len(ordered) + 2:len(ordered) + 2 + N_GROUPS], refs[-1]
        t = x_ref[...]
        xb_ref[...] = t.astype(BF16)
        for c in range(slabs):
            scratch[c] = t[:, c * LANES:(c + 1) * LANES]
        for g, dil in enumerate(DILATIONS):
            length = s // dil
            for r in range(dil):
                part = t if dil == 1 else jnp.concatenate(
                    [scratch[c, pl.ds(r, length, stride=dil), :] for c in range(slabs)], axis=1)
                xt_refs[g][:, r * length:(r + 1) * length] = part.T.astype(BF16)

    col = pl.BlockSpec((s, tc), lambda j: (0, j))
    row = pl.BlockSpec((tc, s), lambda j: (j, 0))
    t_shape = jax.ShapeDtypeStruct((d, s), BF16)
    out = pl.pallas_call(
        body, name="prepare_x", grid=(d // tc,),
        in_specs=[pl.BlockSpec(t.shape, lambda j: (0, 0)) for t in ordered] + [col],
        out_specs=[col] + [row] * N_GROUPS,
        out_shape=[jax.ShapeDtypeStruct((s, d), BF16)] + [t_shape] * N_GROUPS,
        scratch_shapes=[_permute_scratch(s, tc)], compiler_params=_params(("parallel",)))(*ordered, x)
    return out[0], out[1:]


def _local_step(x, target, w_open, w_close, w_width, b_gate, pool_scale, gamma, beta, aw, pw, small_weights,
                start_exchange=None, first_token=None):
    s, d = x.shape
    tn = _col_tile(aw, pw, w_width)
    sub = aw // tn
    per_chip = w_width // tn
    qkv_w = 3 * N_GROUPS * aw
    w_shape = (N_CHIPS, d, w_width)

    regions = [dict(kind=g, blocks=[(which * N_GROUPS + g) * sub + i for which in range(3) for i in range(sub)])
               for g in range(N_GROUPS)]
    lo = qkv_w // tn
    for name, width in (("zuz", aw + 2 * pw), ("gates", 2 * d)):
        regions.append(dict(kind=name, blocks=list(range(lo, lo + width // tn)), j0=lo, width=width))
        lo += width // tn
    results = [None] * len(regions)
    xb, xts = _prepare_x(x, first_token)
    wcs, last = [], []
    w_open(0, [xb])
    for ch in range(W_CHUNKS):
        wc, token = w_close(ch, last)
        wcs.append(wc)
        calls = []
        for i, region in enumerate(regions):
            blocks = [b for b in region["blocks"] if _chunk_of(b, per_chip) == ch]
            if blocks:
                calls.append((i, region, blocks))
        done = []
        for k, (i, region, blocks) in enumerate(calls):
            after = [token]
            if k == len(calls) - 1 and ch + 1 < W_CHUNKS:
                after.append(w_open(ch + 1, done))
            if region["kind"] in range(N_GROUPS):
                results[i] = _in_proj_qkv(xb, wc, region["kind"], blocks, aw, tn, results[i], after,
                                          f"in_proj_qkv{region['kind']}_{ch}")
            else:
                results[i] = _in_proj(xb, wc, blocks, region["j0"], region["width"], tn, BF16, results[i], after,
                                      f"in_proj_{region['kind']}_{ch}")
            done.append(results[i])
        last = done[-1:]
    qkv = [results[g].reshape(3, s, aw) for g in range(N_GROUPS)]
    zuz, gpre = results[N_GROUPS], results[N_GROUPS + 1]

    attn = [_attn_fwd(qkv[g], g) for g in range(N_GROUPS)]
    o, y_attn, y_attn_t, lse = _combine_groups([a[0] for a in attn], [a[1] for a in attn], zuz, aw)
    w_pool, wpa4, wpp4, w_out = small_weights(o)
    pooled, lin, y_pool, y_pool_t = _pool_fwd(zuz, w_pool, pool_scale, aw, pw)
    a, p, sa, sp, merged, merged_t = _proj_merge(y_attn, y_pool, wpa4, wpp4, gpre, b_gate)
    dr, drb, loss_lanes, d_gamma, d_beta = _out_norm_loss(merged, w_out, x, target, gamma, beta)

    da, dp, d_gpre_a, d_gpre_p, d_b_a, d_b_p = _merge_bwd(drb, w_out, a, p, sa, sp)
    d_b_gate = jnp.concatenate([d_b_a, d_b_p], axis=1)
    d_w_out = _weight_grad(merged_t, drb, d // N_CHIPS, False, "w_out_grad")
    d_wpa4 = _weight_grad(y_attn_t, da, d // N_CHIPS, True, "w_proj_attn_grad")
    d_wpp4 = _weight_grad(y_pool_t, dp, d // N_CHIPS, True, "w_proj_pool_grad")
    d_z_attn, d_o, dd = _attn_gate_bwd(da, wpa4, zuz, o)
    d_z_pool, d_pooled, d_w_pool, d_pool_scale = _pool_gate_bwd(dp, wpp4, zuz, lin, pooled, w_pool, pool_scale, aw)
    d_u = _pool_bwd(d_pooled)
    dqkv = [_attn_bwd(qkv[g], d_o[g], lse[g], dd[g], g) for g in range(N_GROUPS)]

    rest = [d_z_attn, d_u, d_z_pool, d_gpre_a, d_gpre_p]
    d_w_in4 = None
    for g in range(N_GROUPS):
        d_w_in4 = _w_in_grad_part(xts[g], dqkv[g], lambda j, g=g: ((j // sub) * N_GROUPS + g) * sub + j % sub,
                                  3 * sub, tn, w_shape, d_w_in4, f"w_in_grad_qkv{g}")
    lo = qkv_w // tn
    for i, piece in enumerate(rest):
        n_local = piece.shape[1] // tn
        d_w_in4 = _w_in_grad_part(xts[0], piece, lambda j, lo=lo: lo + j, n_local, tn, w_shape, d_w_in4,
                                  f"w_in_grad_rest{i}")
        lo += n_local
    grads = dict(loss_lanes=loss_lanes, w_in=d_w_in4, b_gate=d_b_gate, w_pool=d_w_pool,
                 pool_scale=d_pool_scale, w_proj_attn=d_wpa4, w_proj_pool=d_wpp4, w_out=d_w_out,
                 ln_gamma=d_gamma, ln_beta=d_beta)
    token = jnp.zeros((8, 128), F32) if start_exchange is None else start_exchange(grads)
    grads["d_x"] = _x_grad(dqkv, rest, _assemble_w(wcs, token), dr, aw, tn)
    return grads


def _pack_small(wpa, wpp, w_out, w_pool):
    width = wpa.shape[1]
    return jnp.concatenate([wpa, wpp, w_out.reshape(-1, width), w_pool.reshape(-1, width)], axis=0)


def _unpack_small(packed, aw, pw, d, pg):
    lead = packed.shape[:-2]
    width = d // N_CHIPS
    r0, r1, r2 = aw, aw + pw, aw + pw + d
    return (packed[..., :r0, :], packed[..., r0:r1, :], packed[..., r1:r2, :].reshape(lead + (width, d)),
            packed[..., r2:, :].reshape(lead + (len(POOL_WINDOWS), pg // N_CHIPS, pg)))


def _pack_rows(vectors, rows):
    flat = jnp.concatenate([v.reshape(-1) for v in vectors])
    return jnp.pad(flat, (0, rows * 128 - flat.shape[0])).reshape(rows, 128)


def _unpack_rows(packed, sizes):
    flat, out, lo = packed.reshape(-1), [], 0
    for n in sizes:
        out.append(flat[lo:lo + n].reshape(1, n))
        lo += n
    return out


def kernel(x, w_in, b_gate, w_pool, pool_scale, w_proj_attn, w_proj_pool, w_out, ln_gamma, ln_beta, loss_target, m_w_in, m_b_gate, m_w_pool, m_pool_scale, m_w_proj_attn, m_w_proj_pool, m_w_out, m_ln_gamma, m_ln_beta, v_w_in, v_b_gate, v_w_pool, v_pool_scale, v_w_proj_attn, v_w_proj_pool, v_w_out, v_ln_gamma, v_ln_beta):
    s, d = x.shape[1], x.shape[2]
    aw, pw = w_proj_attn.shape[1], w_proj_pool.shape[1]
    pg = w_pool.shape[3]
    n_win = len(POOL_WINDOWS)

    def small(wpa, wpp, wo, wpl):
        return _pack_small(wpa[0], wpp[0], wo[0], wpl[0])

    chip = 2 * lax.axis_index("x") + lax.axis_index("y")
    core = lax.axis_index("c")

    flight = {"chunk": _halves_start(_place_block(w_in[0], N_CHIPS, chip, BF16, "place_w_in0", 0, W_CHUNKS), x,
                                     "gather_w_in0_start")}
    first_token = flight["chunk"][2]
    w_small = small(w_proj_attn, w_proj_pool, w_out, w_pool) + first_token[0, 0]
    placed = [None] + [_place_block(w_in[0], N_CHIPS, chip, BF16, f"place_w_in{ch}", ch, W_CHUNKS, first_token)
                       for ch in range(1, W_CHUNKS)]
    placed_small = _place_block(w_small, N_CHIPS, chip, BF16, "place_w_small", after=first_token)

    def w_open(ch, after):
        sems, thru, _ = flight["chunk"]
        if ch == 0:
            after = after + placed[1:] + [placed_small]
        landed = _halves_wait(sems, thru, after, f"gather_w_in{ch}_wait")
        if ch + 1 < W_CHUNKS:
            flight["forward"], flight["chunk"] = _advance_start(landed, placed[ch + 1], False, f"advance_w_in{ch}")
        else:
            flight["forward"], flight["small"] = _advance_start(landed, placed_small, True, f"advance_w_in{ch}")
        flight["token"] = flight["forward"][2]
        return flight["token"]

    def w_close(ch, after):
        sems, thru, _ = flight["forward"]
        return _forward_wait(sems, thru, after, f"forward_w_in{ch}_wait"), flight["token"]

    def small_weights(after):
        sems, thru, _ = flight["small"]
        small4 = _broadcast_wait(sems, thru, after, "gather_small_wait")
        wpa4, wpp4, w_out4, w_pool4 = _unpack_small(small4, aw, pw, d, pg)
        return w_pool4.transpose(1, 0, 2, 3).reshape(n_win, pg, pg), wpa4, wpp4, w_out4.reshape(d, d)

    exchange = {}

    def start_exchange(g):
        g_pool4 = g["w_pool"].reshape(n_win, N_CHIPS, pg // N_CHIPS, pg).transpose(1, 0, 2, 3).astype(BF16)
        g_out4 = g["w_out"].reshape(N_CHIPS, d // N_CHIPS, d)
        g_small4 = jnp.concatenate([g["w_proj_attn"], g["w_proj_pool"], g_out4.reshape(N_CHIPS, -1, d // N_CHIPS),
                                    g_pool4.reshape(N_CHIPS, -1, d // N_CHIPS)], axis=1)
        theirs_big, theirs_small = _swap_halves([g["w_in"], g_small4])
        chip_big, placed_big = _add_halves(g["w_in"], theirs_big, core, chip, "add_cores_big")
        chip_small, placed_small = _add_halves(g_small4, theirs_small, core, chip, "add_cores_small")
        sems, sums, placed, token = _scatter_start([chip_big, chip_small], [placed_big, placed_small])
        exchange.update(sems=sems, sums=sums, placed=placed)
        return token

    g = _local_step(x[0], loss_target[0], w_open, w_close, w_in.shape[2], b_gate, pool_scale, ln_gamma, ln_beta,
                    aw, pw, small_weights, start_exchange, first_token)

    sizes = [b_gate.shape[1], pool_scale.shape[1], d, d, 1]
    rows = -(-sum(sizes) // (8 * 128)) * 8
    loss_part = (0.5 / d) * jnp.sum(g["loss_lanes"]).reshape(1, 1)
    parts = _gather_rows(_pack_rows([g["b_gate"], g["pool_scale"], g["ln_gamma"], g["ln_beta"], loss_part], rows))
    zero = jnp.zeros((1, 1), F32)
    packed = [_pack_rows(vs, rows) for vs in ([b_gate, pool_scale, ln_gamma, ln_beta, zero],
                                              [m_b_gate, m_pool_scale, m_ln_gamma, m_ln_beta, zero],
                                              [v_b_gate, v_pool_scale, v_ln_gamma, v_ln_beta, zero])]
    replicated = _sum_rows_adamw(parts, *packed)
    rep = [_unpack_rows(t, sizes) for t in replicated]
    loss = rep[0][4].reshape(())

    got_big, got_small = _scatter_wait(exchange["sems"], exchange["sums"], exchange["placed"],
                                       [g["d_x"], replicated[0]])
    join_sems, halves = _join_start([_sum_slots(got_big, core, "sum_chips_big"),
                                     _sum_slots(got_small, core, "sum_chips_small")])
    mv_small = (small(m_w_proj_attn, m_w_proj_pool, m_w_out, m_w_pool),
                small(v_w_proj_attn, v_w_proj_pool, v_w_out, v_w_pool))
    upd_in = _adamw_half(w_in[0], halves[0], m_w_in[0], v_w_in[0], core, None, "adamw_w_in_own")
    upd_small = _adamw_half(w_small, halves[1], *mv_small, core, None, "adamw_small_own")
    grad_w_in, grad_small = _join_wait(join_sems, halves, [upd_in[0], upd_small[0]])
    upd_in = _adamw_half(w_in[0], grad_w_in, m_w_in[0], v_w_in[0], 1 - core, upd_in, "adamw_w_in_other")
    upd_small = _adamw_half(w_small, grad_small, *mv_small, 1 - core, upd_small, "adamw_small_other")
    grad_w_in, grad_small = upd_in[3], upd_small[3]

    def leaves(big, packed_small, replicated):
        wpa_, wpp_, wo_, wpl_ = _unpack_small(packed_small, aw, pw, d, pg)
        return [big[None], replicated[0], wpl_[None], replicated[1], wpa_[None], wpp_[None], wo_[None],
                replicated[2], replicated[3]]

    out = [loss, g["d_x"][None]]
    out += leaves(grad_w_in, grad_small, rep[0])
    for i in range(3):
        out += leaves(upd_in[i], upd_small[i], rep[1 + i])
    return tuple(out)
```
